```python
import jax, jax.numpy as jnp
from jax import lax
import numpy as np

D_MODEL = 1024
BATCH = 8
SEQ = 4096
DEPTH = 2

N_META = 16
N_MIXERS = 2
RMS_EPS = 1e-6

MLA_HEADS = 8
QK_NOPE = 128
QK_ROPE = 64
V_HEAD = 128
Q_LORA = 384
KV_LORA = 256
MLA_WIDTH = MLA_HEADS * V_HEAD
ROPE_BASE = 10000.0
Q_BLOCK = 128
MASK_VALUE = -1e30

LRU_WIDTH = 1024
LRU_BLOCKS = 4
LRU_BLOCK = LRU_WIDTH // LRU_BLOCKS
CONV_WIDTH = 4
LRU_C = 8.0

kernel_name = 'mla_rglru_interleaved_hybrid'


def rmsnorm(x, g):
    xf = x.astype(jnp.float32)
    y = xf * lax.rsqrt(jnp.mean(xf * xf, axis=-1, keepdims=True) + RMS_EPS)
    return (y * g.astype(jnp.float32)).astype(x.dtype)


def rotate_half_split(x, cos, sin):
    x1, x2 = jnp.split(x, 2, axis=-1)
    return jnp.concatenate([x1 * cos - x2 * sin, x1 * sin + x2 * cos], axis=-1).astype(x.dtype)


def block_causal_attention(q_nope, q_rope, k_nope, k_rope, v):
    B, T, H, _ = q_nope.shape
    pad = (-T) % Q_BLOCK
    Tp = T + pad
    nb = Tp // Q_BLOCK

    def padt(a):
        return jnp.pad(a, [(0, 0), (pad, 0)] + [(0, 0)] * (a.ndim - 2))

    q_nope, q_rope, k_nope, k_rope, v = (padt(a) for a in (q_nope, q_rope, k_nope, k_rope, v))
    scale = (QK_NOPE + QK_ROPE) ** -0.5
    key_idx = jnp.arange(Tp)

    def to_blocks(a):
        return jnp.moveaxis(a.reshape(B, nb, Q_BLOCK, *a.shape[2:]), 1, 0)

    def one_block(args):
        blk, qn, qr = args
        s = (jnp.einsum('bqhd,bkhd->bhqk', qn, k_nope, preferred_element_type=jnp.float32)
             + jnp.einsum('bqhr,bkr->bhqk', qr, k_rope, preferred_element_type=jnp.float32)) * scale
        q_idx = blk * Q_BLOCK + jnp.arange(Q_BLOCK)
        mask = (key_idx[None, :] <= q_idx[:, None]) & (key_idx[None, :] >= pad)
        s = jnp.where(mask[None, None], s, MASK_VALUE)
        p = jax.nn.softmax(s, axis=-1)
        return jnp.einsum('bhqk,bkhd->bqhd', p.astype(v.dtype), v)

    out = lax.map(one_block, (jnp.arange(nb), to_blocks(q_nope), to_blocks(q_rope)))
    out = jnp.moveaxis(out, 0, 1).reshape(B, Tp, H, V_HEAD)
    return out[:, pad:]


def mla_mixer(h, w_in, q_norm_g, kv_norm_g, w_uq, w_ukv, w_out):
    B, T, _ = h.shape
    proj = h @ w_in
    q_lat, kv_lat, k_rope, gate = jnp.split(
        proj, [Q_LORA, Q_LORA + KV_LORA, Q_LORA + KV_LORA + QK_ROPE], axis=-1)
    q = (rmsnorm(q_lat, q_norm_g) @ w_uq).reshape(B, T, MLA_HEADS, QK_NOPE + QK_ROPE)
    q_nope, q_rope = q[..., :QK_NOPE], q[..., QK_NOPE:]
    kv = (rmsnorm(kv_lat, kv_norm_g) @ w_ukv).reshape(B, T, MLA_HEADS, QK_NOPE + V_HEAD)
    k_nope, v = kv[..., :QK_NOPE], kv[..., QK_NOPE:]
    pos = jnp.arange(T, dtype=jnp.float32)
    inv_freq = ROPE_BASE ** (-jnp.arange(0, QK_ROPE, 2, dtype=jnp.float32) / QK_ROPE)
    ang = pos[:, None] * inv_freq[None, :]
    cos, sin = jnp.cos(ang), jnp.sin(ang)
    q_rope = rotate_half_split(q_rope, cos[:, None, :], sin[:, None, :])
    k_rope = rotate_half_split(k_rope, cos, sin)
    attn = block_causal_attention(q_nope, q_rope, k_nope, k_rope, v)
    y = attn.reshape(B, T, MLA_WIDTH) * jax.nn.silu(gate)
    return y @ w_out


def rglru_mixer(h, w_in, conv_w, conv_b, w_rg, b_rg, w_ig, b_ig, lam, w_out):
    B, T, _ = h.shape
    proj = h @ w_in
    u, gate = jnp.split(proj, [LRU_WIDTH], axis=-1)
    up = jnp.pad(u, ((0, 0), (CONV_WIDTH - 1, 0), (0, 0)))
    uc = conv_b + up[:, 0:T] * conv_w[0]
    for j in range(1, CONV_WIDTH):
        uc = uc + up[:, j:j + T] * conv_w[j]
    ub = uc.reshape(B, T, LRU_BLOCKS, LRU_BLOCK)
    r = jax.nn.sigmoid(jnp.einsum('btgi,gij->btgj', ub, w_rg).reshape(B, T, LRU_WIDTH) + b_rg)
    i = jax.nn.sigmoid(jnp.einsum('btgi,gij->btgj', ub, w_ig).reshape(B, T, LRU_WIDTH) + b_ig)
    log_a = -LRU_C * r.astype(jnp.float32) * jax.nn.softplus(-lam.astype(jnp.float32))
    a = jnp.exp(log_a)
    mult = jnp.sqrt(-jnp.expm1(2.0 * log_a))
    mult = jnp.where(jnp.arange(T)[None, :, None] == 0, 1.0, mult)
    b = mult * (i * uc).astype(jnp.float32)

    def combine(left, right):
        a1, b1 = left
        a2, b2 = right
        return a1 * a2, a2 * b1 + b2

    _, hs = lax.associative_scan(combine, (a, b), axis=1)
    y = hs.astype(h.dtype) * jax.nn.silu(gate)
    return y @ w_out


def _fwd_setup_inputs(seed: int = 0) -> dict:
    key = jax.random.key(seed)
    ks = jax.random.split(key, 24)
    n_a = (DEPTH + 1) // 2
    n_b = DEPTH // 2
    d = D_MODEL
    f32 = jnp.float32

    def nrm(k, shape, fan_in):
        return jax.random.normal(k, shape, f32) * (fan_in ** -0.5)

    def gain(k, shape):
        return 1.0 + 0.01 * jax.random.normal(k, shape, f32)

    a_in_cols = Q_LORA + KV_LORA + QK_ROPE + MLA_WIDTH
    u0 = jax.random.uniform(ks[17], (n_b, LRU_WIDTH), f32, minval=0.9, maxval=0.999)
    s0 = u0 ** (1.0 / LRU_C)
    lam = jnp.log(s0) - jnp.log1p(-s0)
    return {
        'x': jax.random.normal(ks[0], (BATCH, SEQ, d), f32),
        'meta_tokens': jax.random.normal(ks[1], (N_META, d), f32),
        'a_norm_g': gain(ks[2], (n_a, d)),
        'a_w_in': nrm(ks[3], (n_a, d, a_in_cols), d),
        'a_q_norm_g': gain(ks[4], (n_a, Q_LORA)),
        'a_kv_norm_g': gain(ks[5], (n_a, KV_LORA)),
        'a_w_uq': nrm(ks[6], (n_a, Q_LORA, MLA_HEADS * (QK_NOPE + QK_ROPE)), Q_LORA),
        'a_w_ukv': nrm(ks[7], (n_a, KV_LORA, MLA_HEADS * (QK_NOPE + V_HEAD)), KV_LORA),
        'a_w_out': nrm(ks[8], (n_a, MLA_WIDTH, d), MLA_WIDTH),
        'b_norm_g': gain(ks[9], (n_b, d)),
        'b_w_in': nrm(ks[10], (n_b, d, 2 * LRU_WIDTH), d),
        'b_conv_w': nrm(ks[11], (n_b, CONV_WIDTH, LRU_WIDTH), CONV_WIDTH),
        'b_conv_b': 0.01 * jax.random.normal(ks[12], (n_b, LRU_WIDTH), f32),
        'b_w_rg': nrm(ks[13], (n_b, LRU_BLOCKS, LRU_BLOCK, LRU_BLOCK), LRU_BLOCK),
        'b_b_rg': 0.01 * jax.random.normal(ks[14], (n_b, LRU_WIDTH), f32),
        'b_w_ig': nrm(ks[15], (n_b, LRU_BLOCKS, LRU_BLOCK, LRU_BLOCK), LRU_BLOCK),
        'b_b_ig': 0.01 * jax.random.normal(ks[16], (n_b, LRU_WIDTH), f32),
        'b_lam': lam,
        'b_w_out': nrm(ks[18], (n_b, LRU_WIDTH, d), LRU_WIDTH),
        'final_norm_g': gain(ks[19], (d,)),
    }


def _fwd_reference(x, meta_tokens, a_norm_g, a_w_in, a_q_norm_g, a_kv_norm_g, a_w_uq, a_w_ukv,
              a_w_out, b_norm_g, b_w_in, b_conv_w, b_conv_b, b_w_rg, b_b_rg, b_w_ig, b_b_ig,
              b_lam, b_w_out, final_norm_g):
    B = x.shape[0]
    meta = jnp.broadcast_to(meta_tokens[None].astype(x.dtype), (B, N_META, x.shape[-1]))
    h = jnp.concatenate([meta, x], axis=1)
    for layer in range(DEPTH):
        j = layer // N_MIXERS
        if layer % N_MIXERS == 0:
            h = h + mla_mixer(rmsnorm(h, a_norm_g[j]), a_w_in[j], a_q_norm_g[j], a_kv_norm_g[j],
                              a_w_uq[j], a_w_ukv[j], a_w_out[j])
        else:
            h = h + rglru_mixer(rmsnorm(h, b_norm_g[j]), b_w_in[j], b_conv_w[j], b_conv_b[j],
                                b_w_rg[j], b_b_rg[j], b_w_ig[j], b_b_ig[j], b_lam[j], b_w_out[j])
    h = rmsnorm(h, final_norm_g)
    return h[:, N_META:]


import jax as _jax
import jax.numpy as _jnp

TWIN_FORMAT = 'train_step'
FWD_PARAMS = ['x', 'meta_tokens', 'a_norm_g', 'a_w_in', 'a_q_norm_g', 'a_kv_norm_g', 'a_w_uq', 'a_w_ukv', 'a_w_out', 'b_norm_g', 'b_w_in', 'b_conv_w', 'b_conv_b', 'b_w_rg', 'b_b_rg', 'b_w_ig', 'b_b_ig', 'b_lam', 'b_w_out', 'final_norm_g']
TWIN_WEIGHTS = ['meta_tokens', 'a_norm_g', 'a_w_in', 'a_q_norm_g', 'a_kv_norm_g', 'a_w_uq', 'a_w_ukv', 'a_w_out', 'b_norm_g', 'b_w_in', 'b_conv_w', 'b_conv_b', 'b_w_rg', 'b_b_rg', 'b_w_ig', 'b_b_ig', 'b_lam', 'b_w_out', 'final_norm_g']
TWIN_DIFF_INPUT = 'x'
TWIN_INPUTS = ['x', 'meta_tokens', 'a_norm_g', 'a_w_in', 'a_q_norm_g', 'a_kv_norm_g', 'a_w_uq', 'a_w_ukv', 'a_w_out', 'b_norm_g', 'b_w_in', 'b_conv_w', 'b_conv_b', 'b_w_rg', 'b_b_rg', 'b_w_ig', 'b_b_ig', 'b_lam', 'b_w_out', 'final_norm_g', 'loss_target', 'm_meta_tokens', 'm_a_norm_g', 'm_a_w_in', 'm_a_q_norm_g', 'm_a_kv_norm_g', 'm_a_w_uq', 'm_a_w_ukv', 'm_a_w_out', 'm_b_norm_g', 'm_b_w_in', 'm_b_conv_w', 'm_b_conv_b', 'm_b_w_rg', 'm_b_b_rg', 'm_b_w_ig', 'm_b_b_ig', 'm_b_lam', 'm_b_w_out', 'm_final_norm_g', 'v_meta_tokens', 'v_a_norm_g', 'v_a_w_in', 'v_a_q_norm_g', 'v_a_kv_norm_g', 'v_a_w_uq', 'v_a_w_ukv', 'v_a_w_out', 'v_b_norm_g', 'v_b_w_in', 'v_b_conv_w', 'v_b_conv_b', 'v_b_w_rg', 'v_b_b_rg', 'v_b_w_ig', 'v_b_b_ig', 'v_b_lam', 'v_b_w_out', 'v_final_norm_g']
TWIN_OUTPUTS = ['loss', 'grad_x', 'grad_meta_tokens', 'grad_a_norm_g', 'grad_a_w_in', 'grad_a_q_norm_g', 'grad_a_kv_norm_g', 'grad_a_w_uq', 'grad_a_w_ukv', 'grad_a_w_out', 'grad_b_norm_g', 'grad_b_w_in', 'grad_b_conv_w', 'grad_b_conv_b', 'grad_b_w_rg', 'grad_b_b_rg', 'grad_b_w_ig', 'grad_b_b_ig', 'grad_b_lam', 'grad_b_w_out', 'grad_final_norm_g', 'delta_meta_tokens', 'delta_a_norm_g', 'delta_a_w_in', 'delta_a_q_norm_g', 'delta_a_kv_norm_g', 'delta_a_w_uq', 'delta_a_w_ukv', 'delta_a_w_out', 'delta_b_norm_g', 'delta_b_w_in', 'delta_b_conv_w', 'delta_b_conv_b', 'delta_b_w_rg', 'delta_b_b_rg', 'delta_b_w_ig', 'delta_b_b_ig', 'delta_b_lam', 'delta_b_w_out', 'delta_final_norm_g', 'new_m_meta_tokens', 'new_m_a_norm_g', 'new_m_a_w_in', 'new_m_a_q_norm_g', 'new_m_a_kv_norm_g', 'new_m_a_w_uq', 'new_m_a_w_ukv', 'new_m_a_w_out', 'new_m_b_norm_g', 'new_m_b_w_in', 'new_m_b_conv_w', 'new_m_b_conv_b', 'new_m_b_w_rg', 'new_m_b_b_rg', 'new_m_b_w_ig', 'new_m_b_b_ig', 'new_m_b_lam', 'new_m_b_w_out', 'new_m_final_norm_g', 'new_v_meta_tokens', 'new_v_a_norm_g', 'new_v_a_w_in', 'new_v_a_q_norm_g', 'new_v_a_kv_norm_g', 'new_v_a_w_uq', 'new_v_a_w_ukv', 'new_v_a_w_out', 'new_v_b_norm_g', 'new_v_b_w_in', 'new_v_b_conv_w', 'new_v_b_conv_b', 'new_v_b_w_rg', 'new_v_b_b_rg', 'new_v_b_w_ig', 'new_v_b_b_ig', 'new_v_b_lam', 'new_v_b_w_out', 'new_v_final_norm_g']
TWIN_LEAF_KINDS = {'loss': 'loss', 'grad_x': 'grad_x', 'grad_meta_tokens': 'grad_w', 'grad_a_norm_g': 'grad_w', 'grad_a_w_in': 'grad_w', 'grad_a_q_norm_g': 'grad_w', 'grad_a_kv_norm_g': 'grad_w', 'grad_a_w_uq': 'grad_w', 'grad_a_w_ukv': 'grad_w', 'grad_a_w_out': 'grad_w', 'grad_b_norm_g': 'grad_w', 'grad_b_w_in': 'grad_w', 'grad_b_conv_w': 'grad_w', 'grad_b_conv_b': 'grad_w', 'grad_b_w_rg': 'grad_w', 'grad_b_b_rg': 'grad_w', 'grad_b_w_ig': 'grad_w', 'grad_b_b_ig': 'grad_w', 'grad_b_lam': 'grad_w', 'grad_b_w_out': 'grad_w', 'grad_final_norm_g': 'grad_w', 'delta_meta_tokens': 'delta_w', 'delta_a_norm_g': 'delta_w', 'delta_a_w_in': 'delta_w', 'delta_a_q_norm_g': 'delta_w', 'delta_a_kv_norm_g': 'delta_w', 'delta_a_w_uq': 'delta_w', 'delta_a_w_ukv': 'delta_w', 'delta_a_w_out': 'delta_w', 'delta_b_norm_g': 'delta_w', 'delta_b_w_in': 'delta_w', 'delta_b_conv_w': 'delta_w', 'delta_b_conv_b': 'delta_w', 'delta_b_w_rg': 'delta_w', 'delta_b_b_rg': 'delta_w', 'delta_b_w_ig': 'delta_w', 'delta_b_b_ig': 'delta_w', 'delta_b_lam': 'delta_w', 'delta_b_w_out': 'delta_w', 'delta_final_norm_g': 'delta_w', 'new_m_meta_tokens': 'new_m', 'new_m_a_norm_g': 'new_m', 'new_m_a_w_in': 'new_m', 'new_m_a_q_norm_g': 'new_m', 'new_m_a_kv_norm_g': 'new_m', 'new_m_a_w_uq': 'new_m', 'new_m_a_w_ukv': 'new_m', 'new_m_a_w_out': 'new_m', 'new_m_b_norm_g': 'new_m', 'new_m_b_w_in': 'new_m', 'new_m_b_conv_w': 'new_m', 'new_m_b_conv_b': 'new_m', 'new_m_b_w_rg': 'new_m', 'new_m_b_b_rg': 'new_m', 'new_m_b_w_ig': 'new_m', 'new_m_b_b_ig': 'new_m', 'new_m_b_lam': 'new_m', 'new_m_b_w_out': 'new_m', 'new_m_final_norm_g': 'new_m', 'new_v_meta_tokens': 'new_v', 'new_v_a_norm_g': 'new_v', 'new_v_a_w_in': 'new_v', 'new_v_a_q_norm_g': 'new_v', 'new_v_a_kv_norm_g': 'new_v', 'new_v_a_w_uq': 'new_v', 'new_v_a_w_ukv': 'new_v', 'new_v_a_w_out': 'new_v', 'new_v_b_norm_g': 'new_v', 'new_v_b_w_in': 'new_v', 'new_v_b_conv_w': 'new_v', 'new_v_b_conv_b': 'new_v', 'new_v_b_w_rg': 'new_v', 'new_v_b_b_rg': 'new_v', 'new_v_b_w_ig': 'new_v', 'new_v_b_b_ig': 'new_v', 'new_v_b_lam': 'new_v', 'new_v_b_w_out': 'new_v', 'new_v_final_norm_g': 'new_v'}


def _forward(args):
    return _fwd_reference(*[args[k] for k in FWD_PARAMS])


def _output_shape():
    out = _jax.eval_shape(lambda: _forward(_fwd_setup_inputs(0)))
    return out.shape, out.dtype

N_MICROBATCH = 1
ADAM_LR = 0.001
ADAM_B1 = 0.9
ADAM_B2 = 0.999
ADAM_EPS = 1e-08
ADAM_WD = 0.01
ADAM_STEP = 10
PER_EXAMPLE_BATCH_AXIS = {'x': 0, 'loss_target': 0}
SHARED_INPUTS = []
_WEIGHT_DTYPES = {'meta_tokens': _jnp.float32, 'a_norm_g': _jnp.float32, 'a_w_in': _jnp.float32, 'a_q_norm_g': _jnp.float32, 'a_kv_norm_g': _jnp.float32, 'a_w_uq': _jnp.float32, 'a_w_ukv': _jnp.float32, 'a_w_out': _jnp.float32, 'b_norm_g': _jnp.float32, 'b_w_in': _jnp.float32, 'b_conv_w': _jnp.float32, 'b_conv_b': _jnp.float32, 'b_w_rg': _jnp.float32, 'b_b_rg': _jnp.float32, 'b_w_ig': _jnp.float32, 'b_b_ig': _jnp.float32, 'b_lam': _jnp.float32, 'b_w_out': _jnp.float32, 'final_norm_g': _jnp.float32}
MOMENT_SCALE = {'meta_tokens': 5.462429e-03, 'a_norm_g': 3.915450e-02, 'a_w_in': 2.957309e-02, 'a_q_norm_g': 2.770455e-02, 'a_kv_norm_g': 5.777311e-02, 'a_w_uq': 1.469080e-02, 'a_w_ukv': 1.864952e-02, 'a_w_out': 2.158283e-02, 'b_norm_g': 9.751167e-02, 'b_w_in': 6.310103e-02, 'b_conv_w': 6.551053e-02, 'b_conv_b': 7.813880e-01, 'b_w_rg': 1.883115e-02, 'b_b_rg': 1.488639e-02, 'b_w_ig': 3.320426e-02, 'b_b_ig': 2.458590e-02, 'b_lam': 2.910924e-02, 'b_w_out': 6.177931e-02, 'final_norm_g': 3.196749e+01}


def _to_microbatches(a, axis):
    t = _jnp.moveaxis(a, axis, 0)
    t = t.reshape((N_MICROBATCH, t.shape[0] // N_MICROBATCH) + t.shape[1:])
    return _jnp.moveaxis(t, 1, axis + 1)


def setup_inputs(seed: int = 0) -> dict:
    inp = _fwd_setup_inputs(seed)
    key = _jax.random.fold_in(_jax.random.key(seed), 7919)
    shape, _ = _output_shape()
    out = dict(inp)
    out["loss_target"] = _jax.random.normal(_jax.random.fold_in(key, 0), shape, _jnp.float32)
    for i, name in enumerate(TWIN_WEIGHTS):
        w = inp[name].astype(_jnp.float32)
        if MOMENT_SCALE is None:
            s = _jnp.sqrt(_jnp.mean(_jnp.square(w)) + 1e-30)
        else:
            s = MOMENT_SCALE[name]
        km, kv = _jax.random.split(_jax.random.fold_in(key, i + 1))
        out[name] = w
        out["m_" + name] = s * _jax.random.normal(km, w.shape, _jnp.float32)
        out["v_" + name] = (s * s) * _jax.random.uniform(kv, w.shape, _jnp.float32, 0.5, 1.5)
    if N_MICROBATCH > 1:
        for name, axis in PER_EXAMPLE_BATCH_AXIS.items():
            out[name] = _to_microbatches(out[name], axis)
    return {'x': out['x'], 'meta_tokens': out['meta_tokens'], 'a_norm_g': out['a_norm_g'], 'a_w_in': out['a_w_in'], 'a_q_norm_g': out['a_q_norm_g'], 'a_kv_norm_g': out['a_kv_norm_g'], 'a_w_uq': out['a_w_uq'], 'a_w_ukv': out['a_w_ukv'], 'a_w_out': out['a_w_out'], 'b_norm_g': out['b_norm_g'], 'b_w_in': out['b_w_in'], 'b_conv_w': out['b_conv_w'], 'b_conv_b': out['b_conv_b'], 'b_w_rg': out['b_w_rg'], 'b_b_rg': out['b_b_rg'], 'b_w_ig': out['b_w_ig'], 'b_b_ig': out['b_b_ig'], 'b_lam': out['b_lam'], 'b_w_out': out['b_w_out'], 'final_norm_g': out['final_norm_g'], 'loss_target': out['loss_target'], 'm_meta_tokens': out['m_meta_tokens'], 'm_a_norm_g': out['m_a_norm_g'], 'm_a_w_in': out['m_a_w_in'], 'm_a_q_norm_g': out['m_a_q_norm_g'], 'm_a_kv_norm_g': out['m_a_kv_norm_g'], 'm_a_w_uq': out['m_a_w_uq'], 'm_a_w_ukv': out['m_a_w_ukv'], 'm_a_w_out': out['m_a_w_out'], 'm_b_norm_g': out['m_b_norm_g'], 'm_b_w_in': out['m_b_w_in'], 'm_b_conv_w': out['m_b_conv_w'], 'm_b_conv_b': out['m_b_conv_b'], 'm_b_w_rg': out['m_b_w_rg'], 'm_b_b_rg': out['m_b_b_rg'], 'm_b_w_ig': out['m_b_w_ig'], 'm_b_b_ig': out['m_b_b_ig'], 'm_b_lam': out['m_b_lam'], 'm_b_w_out': out['m_b_w_out'], 'm_final_norm_g': out['m_final_norm_g'], 'v_meta_tokens': out['v_meta_tokens'], 'v_a_norm_g': out['v_a_norm_g'], 'v_a_w_in': out['v_a_w_in'], 'v_a_q_norm_g': out['v_a_q_norm_g'], 'v_a_kv_norm_g': out['v_a_kv_norm_g'], 'v_a_w_uq': out['v_a_w_uq'], 'v_a_w_ukv': out['v_a_w_ukv'], 'v_a_w_out': out['v_a_w_out'], 'v_b_norm_g': out['v_b_norm_g'], 'v_b_w_in': out['v_b_w_in'], 'v_b_conv_w': out['v_b_conv_w'], 'v_b_conv_b': out['v_b_conv_b'], 'v_b_w_rg': out['v_b_w_rg'], 'v_b_b_rg': out['v_b_b_rg'], 'v_b_w_ig': out['v_b_w_ig'], 'v_b_b_ig': out['v_b_b_ig'], 'v_b_lam': out['v_b_lam'], 'v_b_w_out': out['v_b_w_out'], 'v_final_norm_g': out['v_final_norm_g']}


def _loss(weights, diff, rest, loss_target):
    with _jax.named_scope("forward"):
        args = {**rest, TWIN_DIFF_INPUT: diff, **{k: w.astype(_WEIGHT_DTYPES[k]) for k, w in weights.items()}}
        y = _forward(args)
    with _jax.named_scope("loss_head"):
        err = _jnp.square(y.astype(_jnp.float32) - loss_target)
        return 0.5 * _jnp.sum(_jnp.mean(err, axis=-1)) if err.ndim else 0.5 * err


def _adamw(w, g, m, v):
    m = ADAM_B1 * m + (1.0 - ADAM_B1) * g
    v = ADAM_B2 * v + (1.0 - ADAM_B2) * _jnp.square(g)
    m_hat = m / (1.0 - ADAM_B1 ** ADAM_STEP)
    v_hat = v / (1.0 - ADAM_B2 ** ADAM_STEP)
    delta = -ADAM_LR * (m_hat / (_jnp.sqrt(v_hat) + ADAM_EPS) + ADAM_WD * w)
    return delta, m, v


def reference(x, meta_tokens, a_norm_g, a_w_in, a_q_norm_g, a_kv_norm_g, a_w_uq, a_w_ukv, a_w_out, b_norm_g, b_w_in, b_conv_w, b_conv_b, b_w_rg, b_b_rg, b_w_ig, b_b_ig, b_lam, b_w_out, final_norm_g, loss_target, m_meta_tokens, m_a_norm_g, m_a_w_in, m_a_q_norm_g, m_a_kv_norm_g, m_a_w_uq, m_a_w_ukv, m_a_w_out, m_b_norm_g, m_b_w_in, m_b_conv_w, m_b_conv_b, m_b_w_rg, m_b_b_rg, m_b_w_ig, m_b_b_ig, m_b_lam, m_b_w_out, m_final_norm_g, v_meta_tokens, v_a_norm_g, v_a_w_in, v_a_q_norm_g, v_a_kv_norm_g, v_a_w_uq, v_a_w_ukv, v_a_w_out, v_b_norm_g, v_b_w_in, v_b_conv_w, v_b_conv_b, v_b_w_rg, v_b_b_rg, v_b_w_ig, v_b_b_ig, v_b_lam, v_b_w_out, v_final_norm_g):
    given = dict(x=x, meta_tokens=meta_tokens, a_norm_g=a_norm_g, a_w_in=a_w_in, a_q_norm_g=a_q_norm_g, a_kv_norm_g=a_kv_norm_g, a_w_uq=a_w_uq, a_w_ukv=a_w_ukv, a_w_out=a_w_out, b_norm_g=b_norm_g, b_w_in=b_w_in, b_conv_w=b_conv_w, b_conv_b=b_conv_b, b_w_rg=b_w_rg, b_b_rg=b_b_rg, b_w_ig=b_w_ig, b_b_ig=b_b_ig, b_lam=b_lam, b_w_out=b_w_out, final_norm_g=final_norm_g, loss_target=loss_target, m_meta_tokens=m_meta_tokens, m_a_norm_g=m_a_norm_g, m_a_w_in=m_a_w_in, m_a_q_norm_g=m_a_q_norm_g, m_a_kv_norm_g=m_a_kv_norm_g, m_a_w_uq=m_a_w_uq, m_a_w_ukv=m_a_w_ukv, m_a_w_out=m_a_w_out, m_b_norm_g=m_b_norm_g, m_b_w_in=m_b_w_in, m_b_conv_w=m_b_conv_w, m_b_conv_b=m_b_conv_b, m_b_w_rg=m_b_w_rg, m_b_b_rg=m_b_b_rg, m_b_w_ig=m_b_w_ig, m_b_b_ig=m_b_b_ig, m_b_lam=m_b_lam, m_b_w_out=m_b_w_out, m_final_norm_g=m_final_norm_g, v_meta_tokens=v_meta_tokens, v_a_norm_g=v_a_norm_g, v_a_w_in=v_a_w_in, v_a_q_norm_g=v_a_q_norm_g, v_a_kv_norm_g=v_a_kv_norm_g, v_a_w_uq=v_a_w_uq, v_a_w_ukv=v_a_w_ukv, v_a_w_out=v_a_w_out, v_b_norm_g=v_b_norm_g, v_b_w_in=v_b_w_in, v_b_conv_w=v_b_conv_w, v_b_conv_b=v_b_conv_b, v_b_w_rg=v_b_w_rg, v_b_b_rg=v_b_b_rg, v_b_w_ig=v_b_w_ig, v_b_b_ig=v_b_b_ig, v_b_lam=v_b_lam, v_b_w_out=v_b_w_out, v_final_norm_g=v_final_norm_g)
    weights = {n: given[n] for n in TWIN_WEIGHTS}
    shared = {n: given[n] for n in SHARED_INPUTS}
    per_example = {n: given[n] for n in ['x']}
    grad_fn = _jax.value_and_grad(_loss, argnums=(0, 1))

    def one_microbatch(ex, loss_target):
        ex = dict(ex)
        diff = ex.pop(TWIN_DIFF_INPUT)
        return grad_fn(weights, diff, {**shared, **ex}, loss_target)

    if N_MICROBATCH == 1:
        loss, (grad_w, grad_x) = one_microbatch(per_example, given["loss_target"])
    else:
        def body(carry, xs):
            loss_sum, grad_sum = carry
            l_k, (gw_k, gx_k) = one_microbatch(xs[0], xs[1])
            with _jax.named_scope("update"):
                return (loss_sum + l_k, _jax.tree.map(_jnp.add, grad_sum, gw_k)), gx_k

        init = (_jnp.zeros((), _jnp.float32), _jax.tree.map(_jnp.zeros_like, weights))
        (loss, grad_w), grad_x = _jax.lax.scan(body, init, (per_example, given["loss_target"]))
    with _jax.named_scope("update"):
        delta_w, new_m, new_v = {}, {}, {}
        for n in TWIN_WEIGHTS:
            delta_w[n], new_m[n], new_v[n] = _adamw(weights[n], grad_w[n], given["m_" + n], given["v_" + n])
    return (loss, grad_x, *[grad_w[n] for n in TWIN_WEIGHTS], *[delta_w[n] for n in TWIN_WEIGHTS],
            *[new_m[n] for n in TWIN_WEIGHTS], *[new_v[n] for n in TWIN_WEIGHTS])
```

```python
import functools

import jax
import jax.numpy as jnp
from jax import lax
from jax.experimental import pallas as pl
from jax.experimental.pallas import tpu as pltpu

F32 = jnp.float32
BF16 = jnp.bfloat16
MESH = pl.DeviceIdType.MESH

RMS_EPS = 1e-6
QK_NOPE = 128
QK_ROPE = 64
V_HEAD = 128
HEAD_W = 256
ROPE_BASE = 10000.0
Q_BLOCK = 128
MASK_VALUE = -1e30
CONV_WIDTH = 4
LRU_C = 8.0
N_CHIPS = 4

ADAM_LR = 0.001
ADAM_B1 = 0.9
ADAM_B2 = 0.999
ADAM_EPS = 1e-08
ADAM_WD = 0.01
ADAM_STEP = 10

VMEM_LIMIT_V7X = 56 * 1024 * 1024
LANES = 128
SUBLANES = 8

WEIGHTS = ['meta_tokens', 'a_norm_g', 'a_w_in', 'a_q_norm_g', 'a_kv_norm_g', 'a_w_uq', 'a_w_ukv',
           'a_w_out', 'b_norm_g', 'b_w_in', 'b_conv_w', 'b_conv_b', 'b_w_rg', 'b_b_rg', 'b_w_ig',
           'b_b_ig', 'b_lam', 'b_w_out', 'final_norm_g']
SHARD_AXIS = {'meta_tokens': 1, 'a_norm_g': None, 'a_w_in': 2, 'a_q_norm_g': None, 'a_kv_norm_g': None,
              'a_w_uq': 2, 'a_w_ukv': 2, 'a_w_out': 1, 'b_norm_g': 1, 'b_w_in': 2, 'b_conv_w': 2,
              'b_conv_b': 1, 'b_w_rg': 2, 'b_b_rg': 1, 'b_w_ig': 2, 'b_b_ig': 1, 'b_lam': 1,
              'b_w_out': 1, 'final_norm_g': None}


def _params(sem=None):
    return pltpu.CompilerParams(dimension_semantics=sem, vmem_limit_bytes=VMEM_LIMIT_V7X)


def _row_tile(tp):
    return 384 if (tp % 384 == 0 and tp >= 1152) else 128


def _sigmoid(x):
    return 1.0 / (1.0 + jnp.exp(-x))


def _rms(x):
    return lax.rsqrt(jnp.mean(x * x, axis=-1, keepdims=True) + RMS_EPS)


def _swap32(x):
    lane = lax.broadcasted_iota(jnp.int32, x.shape, 1)
    return jnp.where(lane < 32, pltpu.roll(x, 96, 1), pltpu.roll(x, 32, 1))


def _rope(x, cos_t, sin_t):
    return x * cos_t + _swap32(x) * sin_t


def _unrope(d, cos_t, sin_t):
    lane = lax.broadcasted_iota(jnp.int32, d.shape, 1)
    return jnp.where(lane < QK_ROPE, d * cos_t + _swap32(d * sin_t), 0.0)


def _dot(a, b):
    return jnp.dot(a, b, preferred_element_type=F32)


def _dot_nt(a, b):
    return lax.dot_general(a, b, (((1,), (1,)), ((), ())), preferred_element_type=F32)


def _dot_tn(a, b):
    return lax.dot_general(a, b, (((0,), (0,)), ((), ())), preferred_element_type=F32)


def _norm_matmul(name, x, g, w, splits, tm):
    tp, kin = x.shape
    n = w.shape[1]

    def body(x_ref, g_ref, w_ref, *outs):
        xv = x_ref[...]
        nrm = ((xv * _rms(xv)) * g_ref[...]).astype(BF16)
        y = _dot(nrm, w_ref[...])
        for o_ref, (c0, c1) in zip(outs, splits):
            o_ref[...] = y[:, c0:c1]

    return pl.pallas_call(
        body, name=name, grid=(tp // tm,),
        in_specs=[pl.BlockSpec((tm, kin), lambda i: (i, 0)),
                  pl.BlockSpec((1, kin), lambda i: (0, 0)),
                  pl.BlockSpec((kin, n), lambda i: (0, 0))],
        out_specs=[pl.BlockSpec((tm, c1 - c0), lambda i: (i, 0)) for c0, c1 in splits],
        out_shape=[jax.ShapeDtypeStruct((tp, c1 - c0), F32) for c0, c1 in splits],
        compiler_params=_params(("parallel",)),
    )(x, g, w)


def _q_proj(q_lat, g, w_uq, cos_t, sin_t, heads, tm):
    tp, kin = q_lat.shape
    n = heads * HEAD_W

    def body(x_ref, g_ref, w_ref, cos_ref, sin_ref, q_ref):
        xv = x_ref[...]
        nrm = ((xv * _rms(xv)) * g_ref[...]).astype(BF16)
        y = _dot(nrm, w_ref[...])
        cos_v, sin_v = cos_ref[...], sin_ref[...]
        for h in range(heads):
            c0 = h * HEAD_W
            q_ref[:, c0:c0 + QK_NOPE] = y[:, c0:c0 + QK_NOPE].astype(BF16)
            q_ref[:, c0 + QK_NOPE:c0 + HEAD_W] = _rope(y[:, c0 + QK_NOPE:c0 + HEAD_W], cos_v, sin_v).astype(BF16)

    return pl.pallas_call(
        body, name="a_q_proj", grid=(tp // tm,),
        in_specs=[pl.BlockSpec((tm, kin), lambda i: (i, 0)),
                  pl.BlockSpec((1, kin), lambda i: (0, 0)),
                  pl.BlockSpec((kin, n), lambda i: (0, 0)),
                  pl.BlockSpec((tm, LANES), lambda i: (i, 0)),
                  pl.BlockSpec((tm, LANES), lambda i: (i, 0))],
        out_specs=pl.BlockSpec((tm, n), lambda i: (i, 0)),
        out_shape=jax.ShapeDtypeStruct((tp, n), BF16),
        compiler_params=_params(("parallel",)),
    )(q_lat, g, w_uq, cos_t, sin_t)


def _kv_proj(kv_lat, g, w_ukv, k_rope_raw, cos_t, sin_t, heads, tm):
    tp, kin = kv_lat.shape
    n = heads * (QK_NOPE + V_HEAD)

    def body(x_ref, g_ref, w_ref, kr_ref, cos_ref, sin_ref, k_ref, v_ref):
        xv = x_ref[...]
        nrm = ((xv * _rms(xv)) * g_ref[...]).astype(BF16)
        y = _dot(nrm, w_ref[...])
        kr = _rope(kr_ref[...], cos_ref[...], sin_ref[...]).astype(BF16)
        for h in range(heads):
            c0 = h * (QK_NOPE + V_HEAD)
            k_ref[:, h * HEAD_W:h * HEAD_W + QK_NOPE] = y[:, c0:c0 + QK_NOPE].astype(BF16)
            k_ref[:, h * HEAD_W + QK_NOPE:(h + 1) * HEAD_W] = kr
            v_ref[:, h * V_HEAD:(h + 1) * V_HEAD] = y[:, c0 + QK_NOPE:c0 + QK_NOPE + V_HEAD].astype(BF16)

    return pl.pallas_call(
        body, name="a_kv_proj", grid=(tp // tm,),
        in_specs=[pl.BlockSpec((tm, kin), lambda i: (i, 0)),
                  pl.BlockSpec((1, kin), lambda i: (0, 0)),
                  pl.BlockSpec((kin, n), lambda i: (0, 0)),
                  pl.BlockSpec((tm, LANES), lambda i: (i, 0)),
                  pl.BlockSpec((tm, LANES), lambda i: (i, 0)),
                  pl.BlockSpec((tm, LANES), lambda i: (i, 0))],
        out_specs=[pl.BlockSpec((tm, heads * HEAD_W), lambda i: (i, 0)),
                   pl.BlockSpec((tm, heads * V_HEAD), lambda i: (i, 0))],
        out_shape=[jax.ShapeDtypeStruct((tp, heads * HEAD_W), BF16),
                   jax.ShapeDtypeStruct((tp, heads * V_HEAD), BF16)],
        compiler_params=_params(("parallel",)),
    )(kv_lat, g, w_ukv, k_rope_raw, cos_t, sin_t)


def _attn_mask(i, j, tq, pad):
    row = i * tq + lax.broadcasted_iota(jnp.int32, (tq, tq), 0)
    col = j * tq + lax.broadcasted_iota(jnp.int32, (tq, tq), 1)
    return (col <= row) & (col >= pad)


def _flash_fwd(q, k, v, heads, pad, tq):
    tp = q.shape[0]
    scale = (QK_NOPE + QK_ROPE) ** -0.5

    def body(q_ref, k_ref, v_ref, o_ref, lse_ref):
        i = pl.program_id(1)
        qv = q_ref[...]

        def step(j, carry):
            m, l, acc = carry
            off = pl.multiple_of(j * tq, tq)
            s = _dot_nt(qv, k_ref[pl.ds(off, tq), :]) * scale
            s = jnp.where(_attn_mask(i, j, tq, pad), s, MASK_VALUE)
            m_new = jnp.maximum(m, jnp.max(s, axis=-1, keepdims=True))
            p = jnp.exp(s - m_new)
            alpha = jnp.exp(m - m_new)
            l = alpha * l + jnp.sum(p, axis=-1, keepdims=True)
            acc = alpha * acc + _dot(p.astype(BF16), v_ref[pl.ds(off, tq), :])
            return m_new, l, acc

        m, l, acc = lax.fori_loop(
            0, i + 1, step,
            (jnp.full((tq, 1), MASK_VALUE, F32), jnp.zeros((tq, 1), F32), jnp.zeros((tq, V_HEAD), F32)))
        o_ref[...] = acc / l
        lse_ref[...] = jnp.broadcast_to(m + jnp.log(l), (tq, LANES))

    return pl.pallas_call(
        body, name="a_flash_fwd", grid=(heads, tp // tq),
        in_specs=[pl.BlockSpec((tq, HEAD_W), lambda h, i: (i, h)),
                  pl.BlockSpec((tp, HEAD_W), lambda h, i: (0, h)),
                  pl.BlockSpec((tp, V_HEAD), lambda h, i: (0, h))],
        out_specs=[pl.BlockSpec((tq, V_HEAD), lambda h, i: (i, h)),
                   pl.BlockSpec((None, tq, LANES), lambda h, i: (h, i, 0))],
        out_shape=[jax.ShapeDtypeStruct((tp, heads * V_HEAD), F32),
                   jax.ShapeDtypeStruct((heads, tp, LANES), F32)],
        compiler_params=_params(("parallel", "parallel")),
    )(q, k, v)


def _gated_out(name, a, gate, w, resid, tm):
    tp, wd = a.shape
    d = w.shape[1]

    def body(a_ref, gate_ref, w_ref, res_ref, o_ref):
        gv = gate_ref[...]
        y = (a_ref[...] * (gv * _sigmoid(gv))).astype(BF16)
        o_ref[...] = res_ref[...] + _dot(y, w_ref[...])

    return pl.pallas_call(
        body, name=name, grid=(tp // tm,),
        in_specs=[pl.BlockSpec((tm, wd), lambda i: (i, 0)),
                  pl.BlockSpec((tm, wd), lambda i: (i, 0)),
                  pl.BlockSpec((wd, d), lambda i: (0, 0)),
                  pl.BlockSpec((tm, d), lambda i: (i, 0))],
        out_specs=pl.BlockSpec((tm, d), lambda i: (i, 0)),
        out_shape=jax.ShapeDtypeStruct((tp, d), F32),
        compiler_params=_params(("parallel",)),
    )(a, gate, w, resid)


def _lru_decay(r, sp):
    log_a = -LRU_C * r * sp
    x2 = 2.0 * log_a
    e2 = jnp.exp(x2)
    series = x2 * (1.0 + x2 * (0.5 + x2 * (1.0 / 6.0 + x2 * (1.0 / 24.0 + x2 * (1.0 / 120.0)))))
    em1 = jnp.where(x2 > -0.1, series, e2 - 1.0)
    return jnp.exp(log_a), e2, jnp.sqrt(-em1)


def _softplus(x):
    return jnp.maximum(x, 0.0) + jnp.log1p(jnp.exp(-jnp.abs(x)))


def _rglru_fwd(u, conv_w, conv_b, w_rg, b_rg, w_ig, b_ig, lam, pad, tm):
    tp, w = u.shape
    groups, blk = w_rg.shape[0], w_rg.shape[1]

    def body(u_ref, cw_ref, cb_ref, wr_ref, br_ref, wi_ref, bi_ref, lam_ref,
             uc_ref, r_ref, ig_ref, hs_ref, uext, a_s, b_s, hc):
        i = pl.program_id(0)

        @pl.when(i == 0)
        def _():
            uext[0:SUBLANES, :] = jnp.zeros((SUBLANES, w), F32)
            hc[...] = jnp.zeros((SUBLANES, w), F32)

        uext[SUBLANES:SUBLANES + tm, :] = u_ref[...]
        cw = cw_ref[...]
        uc = cb_ref[...] + uext[pl.ds(SUBLANES - 3, tm), :] * cw[0:1, :]
        uc = uc + uext[pl.ds(SUBLANES - 2, tm), :] * cw[1:2, :]
        uc = uc + uext[pl.ds(SUBLANES - 1, tm), :] * cw[2:3, :]
        uc = uc + uext[pl.ds(SUBLANES, tm), :] * cw[3:4, :]
        uc_ref[...] = uc
        uext[0:SUBLANES, :] = uext[tm:tm + SUBLANES, :]

        sp = _softplus(-lam_ref[...])
        row = i * tm + lax.broadcasted_iota(jnp.int32, (tm, blk), 0)
        for g in range(groups):
            sl = slice(g * blk, (g + 1) * blk)
            ucg = uc_ref[:, sl]
            ucb = ucg.astype(BF16)
            r = _sigmoid(_dot(ucb, wr_ref[g]) + br_ref[:, sl])
            ig = _sigmoid(_dot(ucb, wi_ref[g]) + bi_ref[:, sl])
            r_ref[:, sl] = r
            ig_ref[:, sl] = ig
            a, _, mult = _lru_decay(r, sp[:, sl])
            mult = jnp.where(row == pad, 1.0, mult)
            a_s[:, sl] = a
            b_s[:, sl] = jnp.where(row < pad, 0.0, mult * (ig * ucg))

        row8 = lax.broadcasted_iota(jnp.int32, (SUBLANES, w), 0)

        def group(gi, h_in):
            off = pl.multiple_of(gi * SUBLANES, SUBLANES)
            av = a_s[pl.ds(off, SUBLANES), :]
            bv = b_s[pl.ds(off, SUBLANES), :]
            for k in (1, 2, 4):
                keep = row8 >= k
                bv = jnp.where(keep, av * pltpu.roll(bv, k, 0) + bv, bv)
                av = jnp.where(keep, av * pltpu.roll(av, k, 0), av)
            hv = av * h_in + bv
            hs_ref[pl.ds(off, SUBLANES), :] = hv
            return jnp.broadcast_to(hv[SUBLANES - 1:SUBLANES, :], (SUBLANES, w))

        hc[...] = lax.fori_loop(0, tm // SUBLANES, group, hc[...])

    row_spec = pl.BlockSpec((tm, w), lambda i: (i, 0))
    vec_spec = pl.BlockSpec((1, w), lambda i: (0, 0))
    mat_spec = pl.BlockSpec((groups, blk, blk), lambda i: (0, 0, 0))
    return pl.pallas_call(
        body, name="b_rglru_fwd", grid=(tp // tm,),
        in_specs=[row_spec, pl.BlockSpec((CONV_WIDTH, w), lambda i: (0, 0)), vec_spec,
                  mat_spec, vec_spec, mat_spec, vec_spec, vec_spec],
        out_specs=[row_spec, row_spec, row_spec, row_spec],
        out_shape=[jax.ShapeDtypeStruct((tp, w), F32)] * 4,
        scratch_shapes=[pltpu.VMEM((tm + SUBLANES, w), F32), pltpu.VMEM((tm, w), F32),
                        pltpu.VMEM((tm, w), F32), pltpu.VMEM((SUBLANES, w), F32)],
        compiler_params=_params(("arbitrary",)),
    )(u, conv_w, conv_b, w_rg, b_rg, w_ig, b_ig, lam)


def _final_loss(h, g, target_p, x0, tm):
    tp, d = h.shape

    def body(h_ref, g_ref, t_ref, dh_ref, loss_ref, dg_ref):
        i = pl.program_id(0)

        @pl.when(i == 0)
        def _():
            loss_ref[...] = jnp.zeros_like(loss_ref)
            dg_ref[...] = jnp.zeros_like(dg_ref)

        xv = h_ref[...]
        gv = g_ref[...]
        r = _rms(xv)
        xh = xv * r
        row = i * tm + lax.broadcasted_iota(jnp.int32, (tm, d), 0)
        err = jnp.where(row >= x0, xh * gv - t_ref[...], 0.0)
        loss_ref[...] += 0.5 * jnp.sum(jnp.mean(err * err, axis=-1, keepdims=True))
        dy = err / d
        dg_ref[...] += jnp.sum(dy * xh, axis=0, keepdims=True)
        dxh = dy * gv
        dh_ref[...] = r * (dxh - xh * jnp.mean(dxh * xh, axis=-1, keepdims=True))

    return pl.pallas_call(
        body, name="final_loss", grid=(tp // tm,),
        in_specs=[pl.BlockSpec((tm, d), lambda i: (i, 0)),
                  pl.BlockSpec((1, d), lambda i: (0, 0)),
                  pl.BlockSpec((tm, d), lambda i: (i, 0))],
        out_specs=[pl.BlockSpec((tm, d), lambda i: (i, 0)),
                   pl.BlockSpec((SUBLANES, LANES), lambda i: (0, 0)),
                   pl.BlockSpec((1, d), lambda i: (0, 0))],
        out_shape=[jax.ShapeDtypeStruct((tp, d), F32),
                   jax.ShapeDtypeStruct((SUBLANES, LANES), F32),
                   jax.ShapeDtypeStruct((1, d), F32)],
        compiler_params=_params(("arbitrary",)),
    )(h, g, target_p)


def _gated_out_bwd(name, dout, a, gate, w, tm):
    tp, wd = a.shape
    d = w.shape[1]

    def body(do_ref, a_ref, gate_ref, w_ref, da_ref, dgate_ref, dw_ref):
        @pl.when(pl.program_id(0) == 0)
        def _():
            dw_ref[...] = jnp.zeros_like(dw_ref)

        gv = gate_ref[...]
        av = a_ref[...]
        sg = _sigmoid(gv)
        silu = gv * sg
        dob = do_ref[...].astype(BF16)
        dy = _dot_nt(dob, w_ref[...])
        da_ref[...] = dy * silu
        dgate_ref[...] = dy * av * (sg * (1.0 + gv * (1.0 - sg)))
        dw_ref[...] += _dot_tn((av * silu).astype(BF16), dob)

    return pl.pallas_call(
        body, name=name, grid=(tp // tm,),
        in_specs=[pl.BlockSpec((tm, d), lambda i: (i, 0)),
                  pl.BlockSpec((tm, wd), lambda i: (i, 0)),
                  pl.BlockSpec((tm, wd), lambda i: (i, 0)),
                  pl.BlockSpec((wd, d), lambda i: (0, 0))],
        out_specs=[pl.BlockSpec((tm, wd), lambda i: (i, 0)),
                   pl.BlockSpec((tm, wd), lambda i: (i, 0)),
                   pl.BlockSpec((wd, d), lambda i: (0, 0))],
        out_shape=[jax.ShapeDtypeStruct((tp, wd), F32),
                   jax.ShapeDtypeStruct((tp, wd), F32),
                   jax.ShapeDtypeStruct((wd, d), F32)],
        compiler_params=_params(("arbitrary",)),
    )(dout, a, gate, w)


def _rglru_bwd(dhs, hs, r, ig, uc, u, conv_w, w_rg, w_ig, lam, pad, tm):
    tp, w = u.shape
    groups, blk = w_rg.shape[0], w_rg.shape[1]
    nt = tp // tm
    per8 = tm // SUBLANES

    def body(dhs_ref, hs_ref, hprev_ref, r_ref, ig_ref, uc_ref, u_ref, uprev_ref, cw_ref, wr_ref, wi_ref, lam_ref,
             du_ref, dcw_ref, dcb_ref, dwr_ref, dbr_ref, dwi_ref, dbi_ref, dlam_ref,
             aext, c_s, g_s, hext, uext, ducext, gc):
        step = pl.program_id(0)
        ti = nt - 1 - step

        @pl.when(step == 0)
        def _():
            for ref in (dcw_ref, dcb_ref, dwr_ref, dbr_ref, dwi_ref, dbi_ref, dlam_ref):
                ref[...] = jnp.zeros_like(ref)
            aext[tm:tm + SUBLANES, :] = jnp.zeros((SUBLANES, w), F32)
            ducext[tm:tm + SUBLANES, :] = jnp.zeros((SUBLANES, w), F32)
            gc[...] = jnp.zeros((SUBLANES, w), F32)

        lam_v = lam_ref[...]
        sp = _softplus(-lam_v)
        row = ti * tm + lax.broadcasted_iota(jnp.int32, (tm, w), 0)

        rv = r_ref[...]
        a, e2, mult = _lru_decay(rv, sp)
        aext[0:tm, :] = a
        c_s[...] = aext[pl.ds(1, tm), :]
        row8 = lax.broadcasted_iota(jnp.int32, (SUBLANES, w), 0)

        def group(gi, g_in):
            off = pl.multiple_of((per8 - 1 - gi) * SUBLANES, SUBLANES)
            cv = c_s[pl.ds(off, SUBLANES), :]
            dv = dhs_ref[pl.ds(off, SUBLANES), :]
            for k in (1, 2, 4):
                keep = row8 < SUBLANES - k
                dv = jnp.where(keep, cv * pltpu.roll(dv, SUBLANES - k, 0) + dv, dv)
                cv = jnp.where(keep, cv * pltpu.roll(cv, SUBLANES - k, 0), cv)
            gv = cv * g_in + dv
            g_s[pl.ds(off, SUBLANES), :] = gv
            return jnp.broadcast_to(gv[0:1, :], (SUBLANES, w))

        gc[...] = lax.fori_loop(0, per8, group, gc[...])
        aext[tm:tm + SUBLANES, :] = aext[0:SUBLANES, :]

        gsc = jnp.where(row < pad, 0.0, g_s[...])
        hext[0:SUBLANES, :] = hprev_ref[...]
        hext[SUBLANES:SUBLANES + tm, :] = hs_ref[...]
        hprev = jnp.where(row == 0, 0.0, hext[pl.ds(SUBLANES - 1, tm), :])
        igv = ig_ref[...]
        ucv = uc_ref[...]
        first = row == pad
        mult = jnp.where(first, 1.0, mult)
        dmult = gsc * (igv * ucv)
        dig = gsc * mult * ucv
        duc = gsc * mult * igv
        dlog_a = (gsc * hprev) * a + jnp.where(first, 0.0, dmult * (-e2 / mult))
        dlam_ref[...] += jnp.sum(dlog_a * rv, axis=0, keepdims=True) * (LRU_C * _sigmoid(-lam_v))
        dpre_r = dlog_a * (-LRU_C * sp) * (rv * (1.0 - rv))
        dpre_i = dig * (igv * (1.0 - igv))
        dbr_ref[...] += jnp.sum(dpre_r, axis=0, keepdims=True)
        dbi_ref[...] += jnp.sum(dpre_i, axis=0, keepdims=True)
        for g in range(groups):
            sl = slice(g * blk, (g + 1) * blk)
            ucb = ucv[:, sl].astype(BF16)
            drb = dpre_r[:, sl].astype(BF16)
            dib = dpre_i[:, sl].astype(BF16)
            dwr_ref[g] += _dot_tn(ucb, drb)
            dwi_ref[g] += _dot_tn(ucb, dib)
            ducext[0:tm, sl] = duc[:, sl] + _dot_nt(drb, wr_ref[g]) + _dot_nt(dib, wi_ref[g])

        ducv = ducext[0:tm, :]
        cw = cw_ref[...]
        dcb_ref[...] += jnp.sum(ducv, axis=0, keepdims=True)
        uext[0:SUBLANES, :] = jnp.where(ti == 0, 0.0, uprev_ref[...])
        uext[SUBLANES:SUBLANES + tm, :] = u_ref[...]
        for j in range(CONV_WIDTH):
            ush = uext[pl.ds(SUBLANES - (CONV_WIDTH - 1 - j), tm), :]
            dcw_ref[j:j + 1, :] += jnp.sum(ducv * ush, axis=0, keepdims=True)
        du = ducv * cw[3:4, :]
        for k in range(1, CONV_WIDTH):
            du = du + ducext[pl.ds(k, tm), :] * cw[3 - k:4 - k, :]
        du_ref[...] = du
        ducext[tm:tm + SUBLANES, :] = ducext[0:SUBLANES, :]

    rev = lambda s: (nt - 1 - s, 0)
    halo = lambda s: (jnp.maximum((nt - 1 - s) * per8 - 1, 0), 0)
    row_spec = pl.BlockSpec((tm, w), rev)
    halo_spec = pl.BlockSpec((SUBLANES, w), halo)
    vec_spec = pl.BlockSpec((1, w), lambda s: (0, 0))
    mat_spec = pl.BlockSpec((groups, blk, blk), lambda s: (0, 0, 0))
    cw_spec = pl.BlockSpec((CONV_WIDTH, w), lambda s: (0, 0))
    return pl.pallas_call(
        body, name="b_rglru_bwd", grid=(nt,),
        in_specs=[row_spec, row_spec, halo_spec, row_spec, row_spec, row_spec, row_spec, halo_spec,
                  cw_spec, mat_spec, mat_spec, vec_spec],
        out_specs=[row_spec, cw_spec, vec_spec, mat_spec, vec_spec, mat_spec, vec_spec, vec_spec],
        out_shape=[jax.ShapeDtypeStruct((tp, w), F32), jax.ShapeDtypeStruct((CONV_WIDTH, w), F32),
                   jax.ShapeDtypeStruct((1, w), F32), jax.ShapeDtypeStruct((groups, blk, blk), F32),
                   jax.ShapeDtypeStruct((1, w), F32), jax.ShapeDtypeStruct((groups, blk, blk), F32),
                   jax.ShapeDtypeStruct((1, w), F32), jax.ShapeDtypeStruct((1, w), F32)],
        scratch_shapes=[pltpu.VMEM((tm + SUBLANES, w), F32), pltpu.VMEM((tm, w), F32), pltpu.VMEM((tm, w), F32),
                        pltpu.VMEM((tm + SUBLANES, w), F32), pltpu.VMEM((tm + SUBLANES, w), F32),
                        pltpu.VMEM((tm + SUBLANES, w), F32), pltpu.VMEM((SUBLANES, w), F32)],
        compiler_params=_params(("arbitrary",)),
    )(dhs, hs, hs, r, ig, uc, u, u, conv_w, w_rg, w_ig, lam)


def _norm_matmul_bwd(name, x, g, w, dys, tm, resid=None, prologue=None, extra_out=None):
    tp, kin = x.shape
    n = w.shape[1]
    n_dy = len(dys)
    has_res = resid is not None
    has_extra = extra_out is not None

    def body(*refs):
        x_ref, g_ref, w_ref = refs[:3]
        dy_refs = refs[3:3 + n_dy]
        pos = 3 + n_dy
        res_ref = refs[pos] if has_res else None
        pos += int(has_res)
        dx_ref, dw_ref, dg_ref = refs[pos:pos + 3]
        pos += 3
        ex_ref = refs[pos] if has_extra else None
        pos += int(has_extra)
        dy_s = refs[pos]

        @pl.when(pl.program_id(0) == 0)
        def _():
            dw_ref[...] = jnp.zeros_like(dw_ref)
            dg_ref[...] = jnp.zeros_like(dg_ref)

        if prologue is None:
            c0 = 0
            for ref in dy_refs:
                dy_s[:, c0:c0 + ref.shape[1]] = ref[...].astype(BF16)
                c0 += ref.shape[1]
        else:
            prologue(dy_refs, dy_s, ex_ref)

        xv = x_ref[...]
        gv = g_ref[...]
        r = _rms(xv)
        xh = xv * r
        dyb = dy_s[...]
        dn = _dot_nt(dyb, w_ref[...])
        dw_ref[...] += _dot_tn((xh * gv).astype(BF16), dyb)
        dg_ref[...] += jnp.sum(dn * xh, axis=0, keepdims=True)
        dxh = dn * gv
        dx = r * (dxh - xh * jnp.mean(dxh * xh, axis=-1, keepdims=True))
        if has_res:
            dx = dx + res_ref[...]
        dx_ref[...] = dx

    row = lambda width: pl.BlockSpec((tm, width), lambda i: (i, 0))
    in_specs = [row(kin), pl.BlockSpec((1, kin), lambda i: (0, 0)), pl.BlockSpec((kin, n), lambda i: (0, 0))]
    in_specs += [row(a.shape[1]) for a in dys]
    args = [x, g, w, *dys]
    if has_res:
        in_specs.append(row(kin))
        args.append(resid)
    out_specs = [row(kin), pl.BlockSpec((kin, n), lambda i: (0, 0)), pl.BlockSpec((1, kin), lambda i: (0, 0))]
    out_shape = [jax.ShapeDtypeStruct((tp, kin), F32), jax.ShapeDtypeStruct((kin, n), F32),
                 jax.ShapeDtypeStruct((1, kin), F32)]
    if has_extra:
        out_specs.append(row(extra_out[0]))
        out_shape.append(jax.ShapeDtypeStruct((tp, extra_out[0]), extra_out[1]))
    return pl.pallas_call(
        body, name=name, grid=(tp // tm,),
        in_specs=in_specs, out_specs=out_specs, out_shape=out_shape,
        scratch_shapes=[pltpu.VMEM((tm, n), BF16)],
        compiler_params=_params(("arbitrary",)),
    )(*args)


def _flash_bwd(q, k, v, o, lse, do, heads, pad, tq):
    tp = q.shape[0]
    nq = tp // tq
    scale = (QK_NOPE + QK_ROPE) ** -0.5

    def body(q_ref, k_ref, v_ref, o_ref, lse_ref, do_ref, dq_ref, dk_ref, dv_ref):
        j = pl.program_id(1)

        @pl.when(j == 0)
        def _():
            dq_ref[...] = jnp.zeros_like(dq_ref)

        kv = k_ref[...]
        vv = v_ref[...]

        def step(i, carry):
            dk, dv = carry
            off = pl.multiple_of(i * tq, tq)
            qv = q_ref[pl.ds(off, tq), :]
            dov = do_ref[pl.ds(off, tq), :]
            delta = jnp.sum(dov * o_ref[pl.ds(off, tq), :], axis=-1, keepdims=True)
            dob = dov.astype(BF16)
            s = _dot_nt(qv, kv) * scale
            mask = _attn_mask(i, j, tq, pad)
            p = jnp.where(mask, jnp.exp(s - lse_ref[pl.ds(off, tq), 0:1]), 0.0)
            dv = dv + _dot_tn(p.astype(BF16), dob)
            dp = _dot_nt(dob, vv)
            ds = (p * (dp - delta) * scale).astype(BF16)
            dk = dk + _dot_tn(ds, qv)
            dq_ref[pl.ds(off, tq), :] += _dot(ds, kv)
            return dk, dv

        dk, dv = lax.fori_loop(j, nq, step, (jnp.zeros((tq, HEAD_W), F32), jnp.zeros((tq, V_HEAD), F32)))
        dk_ref[...] = dk
        dv_ref[...] = dv

    return pl.pallas_call(
        body, name="a_flash_bwd", grid=(heads, nq),
        in_specs=[pl.BlockSpec((tp, HEAD_W), lambda h, j: (0, h)),
                  pl.BlockSpec((tq, HEAD_W), lambda h, j: (j, h)),
                  pl.BlockSpec((tq, V_HEAD), lambda h, j: (j, h)),
                  pl.BlockSpec((tp, V_HEAD), lambda h, j: (0, h)),
                  pl.BlockSpec((None, tp, LANES), lambda h, j: (h, 0, 0)),
                  pl.BlockSpec((tp, V_HEAD), lambda h, j: (0, h))],
        out_specs=[pl.BlockSpec((tp, HEAD_W), lambda h, j: (0, h)),
                   pl.BlockSpec((tq, HEAD_W), lambda h, j: (j, h)),
                   pl.BlockSpec((tq, V_HEAD), lambda h, j: (j, h))],
        out_shape=[jax.ShapeDtypeStruct((tp, heads * HEAD_W), F32),
                   jax.ShapeDtypeStruct((tp, heads * HEAD_W), F32),
                   jax.ShapeDtypeStruct((tp, heads * V_HEAD), F32)],
        compiler_params=_params(("parallel", "arbitrary")),
    )(q, k, v, o, lse, do)


def _position():
    return lax.axis_index("x"), lax.axis_index("y"), lax.axis_index("c")


def _other_chips(x, y):
    return [(1 - x, y), (x, 1 - y), (1 - x, 1 - y)]


def _gather_chips(arrs):
    n = len(arrs)

    def body(*refs):
        ins, outs = refs[:n], refs[n:2 * n]
        send_sems, recv_sems, local_sems = refs[2 * n:]
        x, y, c = _position()
        me = 2 * x + y
        copies = []
        for a in range(n):
            mine = pltpu.make_async_copy(ins[a], outs[a].at[me], local_sems.at[a])
            mine.start()
            copies.append(mine)
            for j, (px, py) in enumerate(_other_chips(x, y)):
                cp = pltpu.make_async_remote_copy(
                    src_ref=ins[a], dst_ref=outs[a].at[me],
                    send_sem=send_sems.at[3 * a + j], recv_sem=recv_sems.at[3 * a + j],
                    device_id=(px, py, c), device_id_type=MESH)
                cp.start()
                copies.append(cp)
        for cp in copies:
            cp.wait()

    any_spec = pl.BlockSpec(memory_space=pl.ANY)
    return pl.pallas_call(
        body, name="gather_weights",
        in_specs=[any_spec] * n, out_specs=[any_spec] * n,
        out_shape=[jax.ShapeDtypeStruct((N_CHIPS, *a.shape), a.dtype) for a in arrs],
        scratch_shapes=[pltpu.SemaphoreType.DMA((3 * n,)), pltpu.SemaphoreType.DMA((3 * n,)),
                        pltpu.SemaphoreType.DMA((n,))],
        compiler_params=pltpu.CompilerParams(has_side_effects=True),
    )(*arrs)


def _swap_half_with_sibling(grads):
    _, nk, rows, cols = grads.shape

    def body(g_ref, got_ref, send_sem, recv_sem):
        x, y, c = _position()
        cp = pltpu.make_async_remote_copy(
            src_ref=g_ref.at[1 - c], dst_ref=got_ref, send_sem=send_sem, recv_sem=recv_sem,
            device_id=(x, y, 1 - c), device_id_type=MESH)
        cp.start()
        cp.wait()

    any_spec = pl.BlockSpec(memory_space=pl.ANY)
    return pl.pallas_call(
        body, name="grads_to_sibling",
        in_specs=[any_spec], out_specs=any_spec,
        out_shape=jax.ShapeDtypeStruct((nk, rows, cols), grads.dtype),
        scratch_shapes=[pltpu.SemaphoreType.DMA, pltpu.SemaphoreType.DMA],
        compiler_params=pltpu.CompilerParams(has_side_effects=True),
    )(grads)


def _add_halves(grads, got, c):
    _, nk, rows, cols = grads.shape
    tr = 256 if rows % 256 == 0 else SUBLANES

    def body(c_ref, g_ref, got_ref, o_ref):
        o_ref[...] = g_ref[...] + got_ref[...]

    return pl.pallas_call(
        body, name="grads_chip_sum",
        grid_spec=pltpu.PrefetchScalarGridSpec(
            num_scalar_prefetch=1, grid=(nk, rows // tr),
            in_specs=[pl.BlockSpec((None, None, tr, cols), lambda k, i, c_ref: (c_ref[0], k, i, 0)),
                      pl.BlockSpec((None, tr, cols), lambda k, i, c_ref: (k, i, 0))],
            out_specs=pl.BlockSpec((None, tr, cols), lambda k, i, c_ref: (k, i, 0))),
        out_shape=jax.ShapeDtypeStruct((nk, rows, cols), F32),
        compiler_params=_params(("parallel", "parallel")),
    )(c, grads, got)


def _scatter_to_chips(part):
    nk, rows, cols = part.shape

    def body(p_ref, got_ref, send_sems, recv_sems, local_sem):
        x, y, c = _position()
        me = 2 * x + y
        mine = pltpu.make_async_copy(p_ref.at[me], got_ref.at[me], local_sem)
        mine.start()
        copies = [mine]
        for j, (px, py) in enumerate(_other_chips(x, y)):
            cp = pltpu.make_async_remote_copy(
                src_ref=p_ref.at[2 * px + py], dst_ref=got_ref.at[me],
                send_sem=send_sems.at[j], recv_sem=recv_sems.at[j],
                device_id=(px, py, c), device_id_type=MESH)
            cp.start()
            copies.append(cp)
        for cp in copies:
            cp.wait()

    any_spec = pl.BlockSpec(memory_space=pl.ANY)
    return pl.pallas_call(
        body, name="grads_to_chips",
        in_specs=[any_spec], out_specs=any_spec,
        out_shape=jax.ShapeDtypeStruct((nk, rows, cols), part.dtype),
        scratch_shapes=[pltpu.SemaphoreType.DMA((3,)), pltpu.SemaphoreType.DMA((3,)), pltpu.SemaphoreType.DMA],
        compiler_params=pltpu.CompilerParams(has_side_effects=True),
    )(part)


def _sum_chips(got):
    nk, rows, cols = got.shape
    tr = 256 if rows % 256 == 0 else SUBLANES

    def body(g_ref, o_ref):
        acc = g_ref[0].astype(F32)
        for k in range(1, nk):
            acc = acc + g_ref[k].astype(F32)
        o_ref[...] = acc

    return pl.pallas_call(
        body, name="grads_sum", grid=(rows // tr,),
        in_specs=[pl.BlockSpec((nk, tr, cols), lambda i: (0, i, 0))],
        out_specs=pl.BlockSpec((tr, cols), lambda i: (i, 0)),
        out_shape=jax.ShapeDtypeStruct((rows, cols), F32),
        compiler_params=_params(("parallel",)),
    )(got)


def _share_with_sibling(half):
    rows, cols = half.shape

    def body(h_ref, full_ref, send_sem, recv_sem, local_sem):
        x, y, c = _position()
        mine = pltpu.make_async_copy(h_ref, full_ref.at[c], local_sem)
        mine.start()
        cp = pltpu.make_async_remote_copy(
            src_ref=h_ref, dst_ref=full_ref.at[c], send_sem=send_sem, recv_sem=recv_sem,
            device_id=(x, y, 1 - c), device_id_type=MESH)
        cp.start()
        cp.wait()
        mine.wait()

    any_spec = pl.BlockSpec(memory_space=pl.ANY)
    return pl.pallas_call(
        body, name="grads_share",
        in_specs=[any_spec], out_specs=any_spec,
        out_shape=jax.ShapeDtypeStruct((2, rows, cols), half.dtype),
        scratch_shapes=[pltpu.SemaphoreType.DMA, pltpu.SemaphoreType.DMA, pltpu.SemaphoreType.DMA],
        compiler_params=pltpu.CompilerParams(has_side_effects=True),
    )(half)


def _adamw(name, w, g, m, v):
    rows, cols = w.shape
    tr = 256 if rows % 256 == 0 else rows

    def body(w_ref, g_ref, m_ref, v_ref, d_ref, nm_ref, nv_ref):
        gv = g_ref[...]
        mn = ADAM_B1 * m_ref[...] + (1.0 - ADAM_B1) * gv
        vn = ADAM_B2 * v_ref[...] + (1.0 - ADAM_B2) * (gv * gv)
        m_hat = mn / (1.0 - ADAM_B1 ** ADAM_STEP)
        v_hat = vn / (1.0 - ADAM_B2 ** ADAM_STEP)
        d_ref[...] = -ADAM_LR * (m_hat / (jnp.sqrt(v_hat) + ADAM_EPS) + ADAM_WD * w_ref[...])
        nm_ref[...] = mn
        nv_ref[...] = vn

    spec = pl.BlockSpec((tr, cols), lambda i: (i, 0))
    return pl.pallas_call(
        body, name=name, grid=(rows // tr,),
        in_specs=[spec] * 4, out_specs=[spec] * 3,
        out_shape=[jax.ShapeDtypeStruct((rows, cols), F32)] * 3,
        compiler_params=_params(("parallel",)),
    )(w, g, m, v)


def _as2d(a):
    if a.ndim == 1:
        return a.reshape(1, -1)
    return a.reshape(-1, a.shape[-1])


def _unshard(gathered, axis):
    moved = jnp.moveaxis(gathered, 0, axis)
    shape = list(gathered.shape[1:])
    shape[axis] *= N_CHIPS
    return moved.reshape(shape)


def _rope_tables(tp, pad):
    pos = jnp.arange(tp, dtype=F32) - pad
    inv_freq = ROPE_BASE ** (-jnp.arange(0, QK_ROPE, 2, dtype=F32) / QK_ROPE)
    ang = pos[:, None] * inv_freq[None, :]
    cos, sin = jnp.cos(ang), jnp.sin(ang)
    zeros = jnp.zeros((tp, LANES - QK_ROPE), F32)
    return jnp.concatenate([cos, cos, zeros], axis=1), jnp.concatenate([-sin, sin, zeros], axis=1)


def _local_grads(x, target, wt, heads):
    seq, d = x.shape
    n_meta = wt['meta_tokens'].shape[0]
    t = seq + n_meta
    pad = (-t) % Q_BLOCK
    tp = t + pad
    x0 = pad + n_meta
    tm = _row_tile(tp)
    ql = wt['a_q_norm_g'].shape[1]
    kvl = wt['a_kv_norm_g'].shape[1]
    mla_w = heads * V_HEAD
    lru_w = wt['b_conv_w'].shape[1]

    h0 = jnp.concatenate([jnp.zeros((pad, d), F32), wt['meta_tokens'], x], axis=0)
    target_p = jnp.concatenate([jnp.zeros((x0, d), F32), target], axis=0)
    cos_t, sin_t = _rope_tables(tp, pad)

    w_in_a = wt['a_w_in']
    zcol = jnp.zeros((d, LANES - QK_ROPE), BF16)
    w_in_a = jnp.concatenate([w_in_a[:, :ql + kvl + QK_ROPE], zcol, w_in_a[:, ql + kvl + QK_ROPE:]], axis=1)
    c_kv, c_kr, c_gate = ql, ql + kvl, ql + kvl + LANES
    splits_a = [(0, c_kv), (c_kv, c_kr), (c_kr, c_gate), (c_gate, c_gate + mla_w)]
    w_uq = wt['a_w_uq'].reshape(ql, heads, QK_NOPE + QK_ROPE)
    w_uq = jnp.pad(w_uq, ((0, 0), (0, 0), (0, HEAD_W - QK_NOPE - QK_ROPE))).reshape(ql, heads * HEAD_W)
    w_ukv = wt['a_w_ukv']

    q_lat, kv_lat, kr_raw, gate_a = _norm_matmul("a_in_proj", h0, wt['a_norm_g'], w_in_a, splits_a, tm)
    q = _q_proj(q_lat, wt['a_q_norm_g'], w_uq, cos_t, sin_t, heads, tm)
    k, v = _kv_proj(kv_lat, wt['a_kv_norm_g'], w_ukv, kr_raw, cos_t, sin_t, heads, tm)
    attn, lse = _flash_fwd(q, k, v, heads, pad, tm)
    h1 = _gated_out("a_out_proj", attn, gate_a, wt['a_w_out'], h0, tm)

    u, gate_b = _norm_matmul("b_in_proj", h1, wt['b_norm_g'], wt['b_w_in'], [(0, lru_w), (lru_w, 2 * lru_w)], tm)
    uc, r, ig, hs = _rglru_fwd(u, wt['b_conv_w'], wt['b_conv_b'], wt['b_w_rg'], wt['b_b_rg'],
                               wt['b_w_ig'], wt['b_b_ig'], wt['b_lam'], pad, tm)
    h2 = _gated_out("b_out_proj", hs, gate_b, wt['b_w_out'], h1, tm)

    dh2, loss, d_final_g = _final_loss(h2, wt['final_norm_g'], target_p, x0, tm)

    dhs, dgate_b, dw_out_b = _gated_out_bwd("b_out_proj_bwd", dh2, hs, gate_b, wt['b_w_out'], tm)
    du, dconv_w, dconv_b, dw_rg, db_rg, dw_ig, db_ig, dlam = _rglru_bwd(
        dhs, hs, r, ig, uc, u, wt['b_conv_w'], wt['b_w_rg'], wt['b_w_ig'], wt['b_lam'], pad, tm)
    dh1, dw_in_b, dg_b = _norm_matmul_bwd("b_in_proj_bwd", h1, wt['b_norm_g'], wt['b_w_in'], [du, dgate_b], tm, resid=dh2)

    dattn, dgate_a, dw_out_a = _gated_out_bwd("a_out_proj_bwd", dh1, attn, gate_a, wt['a_w_out'], tm)
    dq, dk, dv = _flash_bwd(q, k, v, attn, lse, dattn, heads, pad, tm)

    def q_prologue(dy_refs, dy_s, ex_ref):
        (dq_ref,), cos_v, sin_v = dy_refs[:1], dy_refs[1][...], dy_refs[2][...]
        for h in range(heads):
            c0 = h * HEAD_W
            dy_s[:, c0:c0 + QK_NOPE] = dq_ref[:, c0:c0 + QK_NOPE].astype(BF16)
            dy_s[:, c0 + QK_NOPE:c0 + HEAD_W] = _unrope(dq_ref[:, c0 + QK_NOPE:c0 + HEAD_W], cos_v, sin_v).astype(BF16)

    dq_lat, dw_uq, dg_q = _norm_matmul_bwd("a_q_proj_bwd", q_lat, wt['a_q_norm_g'], w_uq, [dq, cos_t, sin_t], tm,
                                           prologue=q_prologue)

    def kv_prologue(dy_refs, dy_s, ex_ref):
        dk_ref, dv_ref = dy_refs[:2]
        cos_v, sin_v = dy_refs[2][...], dy_refs[3][...]
        dkr = jnp.zeros((dk_ref.shape[0], LANES), F32)
        for h in range(heads):
            c0 = h * (QK_NOPE + V_HEAD)
            dy_s[:, c0:c0 + QK_NOPE] = dk_ref[:, h * HEAD_W:h * HEAD_W + QK_NOPE].astype(BF16)
            dy_s[:, c0 + QK_NOPE:c0 + QK_NOPE + V_HEAD] = dv_ref[:, h * V_HEAD:(h + 1) * V_HEAD].astype(BF16)
            dkr = dkr + dk_ref[:, h * HEAD_W + QK_NOPE:(h + 1) * HEAD_W]
        ex_ref[...] = _unrope(dkr, cos_v, sin_v)

    dkv_lat, dw_ukv, dg_kv, dkr_raw = _norm_matmul_bwd(
        "a_kv_proj_bwd", kv_lat, wt['a_kv_norm_g'], w_ukv, [dk, dv, cos_t, sin_t], tm,
        prologue=kv_prologue, extra_out=(LANES, F32))

    dh0, dw_in_a, dg_a = _norm_matmul_bwd("a_in_proj_bwd", h0, wt['a_norm_g'], w_in_a,
                                          [dq_lat, dkv_lat, dkr_raw, dgate_a], tm, resid=dh1)

    dw_in_a = jnp.concatenate([dw_in_a[:, :c_kr + QK_ROPE], dw_in_a[:, c_gate:]], axis=1)
    dw_uq = dw_uq.reshape(ql, heads, HEAD_W)[:, :, :QK_NOPE + QK_ROPE].reshape(ql, heads * (QK_NOPE + QK_ROPE))
    grads = {
        'meta_tokens': dh0[pad:x0], 'a_norm_g': dg_a, 'a_w_in': dw_in_a, 'a_q_norm_g': dg_q, 'a_kv_norm_g': dg_kv,
        'a_w_uq': dw_uq, 'a_w_ukv': dw_ukv, 'a_w_out': dw_out_a, 'b_norm_g': dg_b, 'b_w_in': dw_in_b,
        'b_conv_w': dconv_w, 'b_conv_b': dconv_b, 'b_w_rg': dw_rg, 'b_b_rg': db_rg, 'b_w_ig': dw_ig,
        'b_b_ig': db_ig, 'b_lam': dlam, 'b_w_out': dw_out_b, 'final_norm_g': d_final_g,
    }
    return loss, dh0[x0:], grads


def _shard_of(whole, local_shape, axis, k):
    g = whole.reshape([s * (N_CHIPS if a == axis else 1) for a, s in enumerate(local_shape)])
    if axis is None:
        return g
    return lax.slice_in_dim(g, k * local_shape[axis], (k + 1) * local_shape[axis], axis=axis)


def kernel(x, meta_tokens, a_norm_g, a_w_in, a_q_norm_g, a_kv_norm_g, a_w_uq, a_w_ukv, a_w_out, b_norm_g, b_w_in, b_conv_w, b_conv_b, b_w_rg, b_b_rg, b_w_ig, b_b_ig, b_lam, b_w_out, final_norm_g, loss_target, m_meta_tokens, m_a_norm_g, m_a_w_in, m_a_q_norm_g, m_a_kv_norm_g, m_a_w_uq, m_a_w_ukv, m_a_w_out, m_b_norm_g, m_b_w_in, m_b_conv_w, m_b_conv_b, m_b_w_rg, m_b_b_rg, m_b_w_ig, m_b_b_ig, m_b_lam, m_b_w_out, m_final_norm_g, v_meta_tokens, v_a_norm_g, v_a_w_in, v_a_q_norm_g, v_a_kv_norm_g, v_a_w_uq, v_a_w_ukv, v_a_w_out, v_b_norm_g, v_b_w_in, v_b_conv_w, v_b_conv_b, v_b_w_rg, v_b_b_rg, v_b_w_ig, v_b_b_ig, v_b_lam, v_b_w_out, v_final_norm_g):
    local_w = dict(zip(WEIGHTS, (meta_tokens, a_norm_g, a_w_in, a_q_norm_g, a_kv_norm_g, a_w_uq, a_w_ukv, a_w_out,
                                 b_norm_g, b_w_in, b_conv_w, b_conv_b, b_w_rg, b_b_rg, b_w_ig, b_b_ig, b_lam,
                                 b_w_out, final_norm_g)))
    local_m = dict(zip(WEIGHTS, (m_meta_tokens, m_a_norm_g, m_a_w_in, m_a_q_norm_g, m_a_kv_norm_g, m_a_w_uq,
                                 m_a_w_ukv, m_a_w_out, m_b_norm_g, m_b_w_in, m_b_conv_w, m_b_conv_b, m_b_w_rg,
                                 m_b_b_rg, m_b_w_ig, m_b_b_ig, m_b_lam, m_b_w_out, m_final_norm_g)))
    local_v = dict(zip(WEIGHTS, (v_meta_tokens, v_a_norm_g, v_a_w_in, v_a_q_norm_g, v_a_kv_norm_g, v_a_w_uq,
                                 v_a_w_ukv, v_a_w_out, v_b_norm_g, v_b_w_in, v_b_conv_w, v_b_conv_b, v_b_w_rg,
                                 v_b_b_rg, v_b_w_ig, v_b_b_ig, v_b_lam, v_b_w_out, v_final_norm_g)))
    matrices = ('a_w_in', 'a_w_uq', 'a_w_ukv', 'a_w_out', 'b_w_in', 'b_w_rg', 'b_w_ig', 'b_w_out')
    heads = a_w_uq.shape[-1] * N_CHIPS // (QK_NOPE + QK_ROPE)

    sharded = [n for n in WEIGHTS if SHARD_AXIS[n] is not None]
    send = [local_w[n].astype(BF16) if n in matrices else local_w[n] for n in sharded]
    gathered = _gather_chips(send)
    wt = {}
    for n in WEIGHTS:
        if SHARD_AXIS[n] is None:
            whole = local_w[n]
        else:
            whole = _unshard(gathered[sharded.index(n)], SHARD_AXIS[n])
        if n in ('b_w_rg', 'b_w_ig'):
            wt[n] = whole[0]
        elif n == 'b_conv_w':
            wt[n] = whole[0]
        else:
            wt[n] = _as2d(whole)

    loss, grad_x, grads = _local_grads(x[0], loss_target[0], wt, heads)

    per_chip = []
    for k in range(N_CHIPS):
        per_chip.append(jnp.concatenate(
            [_shard_of(grads[n], local_w[n].shape, SHARD_AXIS[n], k).reshape(-1) for n in WEIGHTS]))
    length = per_chip[0].shape[0]
    unit = 2 * (256 if length >= 2 ** 20 else SUBLANES) * 1024
    padded = -(-length // unit) * unit
    rows = padded // 2048
    flat = jnp.stack([jnp.pad(p, (0, padded - length)) for p in per_chip])
    flat = flat.reshape(N_CHIPS, 2, rows, 1024).transpose(1, 0, 2, 3)
    c_idx = lax.axis_index("c").astype(jnp.int32).reshape(1)
    got = _swap_half_with_sibling(flat)
    chip_sum = _add_halves(flat, got, c_idx)
    from_chips = _scatter_to_chips(chip_sum)
    my_half = _sum_chips(from_chips)
    total = _share_with_sibling(my_half).reshape(-1)

    out_g, out_d, out_m, out_v = [], [], [], []
    off = 0
    for n in WEIGHTS:
        shape = local_w[n].shape
        size = 1
        for s in shape:
            size *= s
        g = total[off:off + size].reshape(shape)
        off += size
        delta, new_m, new_v = _adamw("adamw_" + n, _as2d(local_w[n]), _as2d(g), _as2d(local_m[n]), _as2d(local_v[n]))
        out_g.append(g)
        out_d.append(delta.reshape(shape))
        out_m.append(new_m.reshape(shape))
        out_v.append(new_v.reshape(shape))

    loss = lax.psum(loss[0, 0], ("x", "y", "c"))
    return (loss, grad_x[None], *out_g, *out_d, *out_m, *out_v)
```

```python
import functools

import jax
import jax.numpy as jnp
from jax import lax
from jax.experimental import pallas as pl
from jax.experimental.pallas import tpu as pltpu

F32 = jnp.float32
BF16 = jnp.bfloat16
MESH = pl.DeviceIdType.MESH

RMS_EPS = 1e-6
QK_NOPE = 128
QK_ROPE = 64
V_HEAD = 128
HEAD_W = 256
ROPE_BASE = 10000.0
Q_BLOCK = 128
MASK_VALUE = -1e30
CONV_WIDTH = 4
LRU_C = 8.0
N_CHIPS = 4

ADAM_LR = 0.001
ADAM_B1 = 0.9
ADAM_B2 = 0.999
ADAM_EPS = 1e-08
ADAM_WD = 0.01
ADAM_STEP = 10

VMEM_LIMIT_V7X = 56 * 1024 * 1024
LANES = 128
SUBLANES = 8

WEIGHTS = ['meta_tokens', 'a_norm_g', 'a_w_in', 'a_q_norm_g', 'a_kv_norm_g', 'a_w_uq', 'a_w_ukv',
           'a_w_out', 'b_norm_g', 'b_w_in', 'b_conv_w', 'b_conv_b', 'b_w_rg', 'b_b_rg', 'b_w_ig',
           'b_b_ig', 'b_lam', 'b_w_out', 'final_norm_g']
SHARD_AXIS = {'meta_tokens': 1, 'a_norm_g': None, 'a_w_in': 2, 'a_q_norm_g': None, 'a_kv_norm_g': None,
              'a_w_uq': 2, 'a_w_ukv': 2, 'a_w_out': 1, 'b_norm_g': 1, 'b_w_in': 2, 'b_conv_w': 2,
              'b_conv_b': 1, 'b_w_rg': 2, 'b_b_rg': 1, 'b_w_ig': 2, 'b_b_ig': 1, 'b_lam': 1,
              'b_w_out': 1, 'final_norm_g': None}


def _params(sem=None):
    return pltpu.CompilerParams(dimension_semantics=sem, vmem_limit_bytes=VMEM_LIMIT_V7X)


def _row_tile(tp):
    return 384 if (tp % 384 == 0 and tp >= 1152) else 128


def _sigmoid(x):
    return 1.0 / (1.0 + jnp.exp(-x))


def _rms(x):
    return lax.rsqrt(jnp.mean(x * x, axis=-1, keepdims=True) + RMS_EPS)


def _swap32(x):
    lane = lax.broadcasted_iota(jnp.int32, x.shape, 1)
    return jnp.where(lane < 32, pltpu.roll(x, 96, 1), pltpu.roll(x, 32, 1))


def _rope(x, cos_t, sin_t):
    return x * cos_t + _swap32(x) * sin_t


def _unrope(d, cos_t, sin_t):
    lane = lax.broadcasted_iota(jnp.int32, d.shape, 1)
    return jnp.where(lane < QK_ROPE, d * cos_t + _swap32(d * sin_t), 0.0)


def _dot(a, b):
    return jnp.dot(a, b, preferred_element_type=F32)


def _dot_nt(a, b):
    return lax.dot_general(a, b, (((1,), (1,)), ((), ())), preferred_element_type=F32)


def _dot_tn(a, b):
    return lax.dot_general(a, b, (((0,), (0,)), ((), ())), preferred_element_type=F32)


def _norm_matmul(name, x, g, w, splits, tm):
    tp, kin = x.shape
    n = w.shape[1]

    def body(x_ref, g_ref, w_ref, *outs):
        xv = x_ref[...]
        nrm = ((xv * _rms(xv)) * g_ref[...]).astype(BF16)
        y = _dot(nrm, w_ref[...])
        for o_ref, (c0, c1) in zip(outs, splits):
            o_ref[...] = y[:, c0:c1]

    return pl.pallas_call(
        body, name=name, grid=(tp // tm,),
        in_specs=[pl.BlockSpec((tm, kin), lambda i: (i, 0)),
                  pl.BlockSpec((1, kin), lambda i: (0, 0)),
                  pl.BlockSpec((kin, n), lambda i: (0, 0))],
        out_specs=[pl.BlockSpec((tm, c1 - c0), lambda i: (i, 0)) for c0, c1 in splits],
        out_shape=[jax.ShapeDtypeStruct((tp, c1 - c0), F32) for c0, c1 in splits],
        compiler_params=_params(("parallel",)),
    )(x, g, w)


def _q_proj(q_lat, g, w_uq, cos_t, sin_t, heads, tm):
    tp, kin = q_lat.shape
    n = heads * HEAD_W

    def body(x_ref, g_ref, w_ref, cos_ref, sin_ref, q_ref):
        xv = x_ref[...]
        nrm = ((xv * _rms(xv)) * g_ref[...]).astype(BF16)
        y = _dot(nrm, w_ref[...])
        cos_v, sin_v = cos_ref[...], sin_ref[...]
        for h in range(heads):
            c0 = h * HEAD_W
            q_ref[:, c0:c0 + QK_NOPE] = y[:, c0:c0 + QK_NOPE].astype(BF16)
            q_ref[:, c0 + QK_NOPE:c0 + HEAD_W] = _rope(y[:, c0 + QK_NOPE:c0 + HEAD_W], cos_v, sin_v).astype(BF16)

    return pl.pallas_call(
        body, name="a_q_proj", grid=(tp // tm,),
        in_specs=[pl.BlockSpec((tm, kin), lambda i: (i, 0)),
                  pl.BlockSpec((1, kin), lambda i: (0, 0)),
                  pl.BlockSpec((kin, n), lambda i: (0, 0)),
                  pl.BlockSpec((tm, LANES), lambda i: (i, 0)),
                  pl.BlockSpec((tm, LANES), lambda i: (i, 0))],
        out_specs=pl.BlockSpec((tm, n), lambda i: (i, 0)),
        out_shape=jax.ShapeDtypeStruct((tp, n), BF16),
        compiler_params=_params(("parallel",)),
    )(q_lat, g, w_uq, cos_t, sin_t)


def _kv_proj(kv_lat, g, w_ukv, k_rope_raw, cos_t, sin_t, heads, tm):
    tp, kin = kv_lat.shape
    n = heads * (QK_NOPE + V_HEAD)

    def body(x_ref, g_ref, w_ref, kr_ref, cos_ref, sin_ref, k_ref, v_ref):
        xv = x_ref[...]
        nrm = ((xv * _rms(xv)) * g_ref[...]).astype(BF16)
        y = _dot(nrm, w_ref[...])
        kr = _rope(kr_ref[...], cos_ref[...], sin_ref[...]).astype(BF16)
        for h in range(heads):
            c0 = h * (QK_NOPE + V_HEAD)
            k_ref[:, h * HEAD_W:h * HEAD_W + QK_NOPE] = y[:, c0:c0 + QK_NOPE].astype(BF16)
            k_ref[:, h * HEAD_W + QK_NOPE:(h + 1) * HEAD_W] = kr
            v_ref[:, h * V_HEAD:(h + 1) * V_HEAD] = y[:, c0 + QK_NOPE:c0 + QK_NOPE + V_HEAD].astype(BF16)

    return pl.pallas_call(
        body, name="a_kv_proj", grid=(tp // tm,),
        in_specs=[pl.BlockSpec((tm, kin), lambda i: (i, 0)),
                  pl.BlockSpec((1, kin), lambda i: (0, 0)),
                  pl.BlockSpec((kin, n), lambda i: (0, 0)),
                  pl.BlockSpec((tm, LANES), lambda i: (i, 0)),
                  pl.BlockSpec((tm, LANES), lambda i: (i, 0)),
                  pl.BlockSpec((tm, LANES), lambda i: (i, 0))],
        out_specs=[pl.BlockSpec((tm, heads * HEAD_W), lambda i: (i, 0)),
                   pl.BlockSpec((tm, heads * V_HEAD), lambda i: (i, 0))],
        out_shape=[jax.ShapeDtypeStruct((tp, heads * HEAD_W), BF16),
                   jax.ShapeDtypeStruct((tp, heads * V_HEAD), BF16)],
        compiler_params=_params(("parallel",)),
    )(kv_lat, g, w_ukv, k_rope_raw, cos_t, sin_t)


def _attn_mask(row0, col0, rows, cols, pad):
    row = row0 + lax.broadcasted_iota(jnp.int32, (rows, cols), 0)
    col = col0 + lax.broadcasted_iota(jnp.int32, (rows, cols), 1)
    return (col <= row) & (col >= pad)


LOG2E = 1.4426950408889634
FLASH_SUBTILES = 1


def _flash_fwd(q, k, v, heads, pad, tq):
    tp = q.shape[0]
    c2 = (QK_NOPE + QK_ROPE) ** -0.5 * LOG2E

    def body(q_ref, k_ref, v_ref, o_ref, lse_ref):
        i = pl.program_id(1)
        qv = q_ref[...]

        def make_step(masked):
            def step(j, carry):
                m, l, acc = carry
                off = pl.multiple_of(j * tq, tq)
                s = _dot_nt(qv, k_ref[pl.ds(off, tq), :]) * c2
                if masked:
                    s = jnp.where(_attn_mask(i * tq, j * tq, tq, tq, pad), s, MASK_VALUE)
                m_new = jnp.maximum(m, jnp.max(s, axis=-1, keepdims=True))
                p = jnp.exp2(s - m_new)
                alpha = jnp.exp2(m - m_new)
                l = alpha * l + jnp.sum(p, axis=-1, keepdims=True)
                acc = alpha * acc + _dot(p.astype(BF16), v_ref[pl.ds(off, tq), :])
                return m_new, l, acc
            return step

        init = (jnp.full((tq, 1), MASK_VALUE, F32), jnp.zeros((tq, 1), F32), jnp.zeros((tq, V_HEAD), F32))
        carry = make_step(True)(0, init)
        carry = lax.fori_loop(1, i, make_step(False), carry)
        m, l, acc = lax.fori_loop(jnp.maximum(i, 1), i + 1, make_step(True), carry)
        o_ref[...] = acc / l
        lse_ref[...] = jnp.broadcast_to(m + jnp.log(l) * LOG2E, (tq, LANES))

    return pl.pallas_call(
        body, name="a_flash_fwd", grid=(heads, tp // tq),
        in_specs=[pl.BlockSpec((tq, HEAD_W), lambda h, i: (i, h)),
                  pl.BlockSpec((tp, HEAD_W), lambda h, i: (0, h)),
                  pl.BlockSpec((tp, V_HEAD), lambda h, i: (0, h))],
        out_specs=[pl.BlockSpec((tq, V_HEAD), lambda h, i: (i, h)),
                   pl.BlockSpec((None, tq, LANES), lambda h, i: (h, i, 0))],
        out_shape=[jax.ShapeDtypeStruct((tp, heads * V_HEAD), F32),
                   jax.ShapeDtypeStruct((heads, tp, LANES), F32)],
        compiler_params=_params(("parallel", "parallel")),
    )(q, k, v)


def _gated_out(name, a, gate, w, resid, tm):
    tp, wd = a.shape
    d = w.shape[1]

    def body(a_ref, gate_ref, w_ref, res_ref, o_ref):
        gv = gate_ref[...]
        y = (a_ref[...] * (gv * _sigmoid(gv))).astype(BF16)
        o_ref[...] = res_ref[...] + _dot(y, w_ref[...])

    return pl.pallas_call(
        body, name=name, grid=(tp // tm,),
        in_specs=[pl.BlockSpec((tm, wd), lambda i: (i, 0)),
                  pl.BlockSpec((tm, wd), lambda i: (i, 0)),
                  pl.BlockSpec((wd, d), lambda i: (0, 0)),
                  pl.BlockSpec((tm, d), lambda i: (i, 0))],
        out_specs=pl.BlockSpec((tm, d), lambda i: (i, 0)),
        out_shape=jax.ShapeDtypeStruct((tp, d), F32),
        compiler_params=_params(("parallel",)),
    )(a, gate, w, resid)


def _lru_decay(r, sp):
    log_a = -LRU_C * r * sp
    x2 = 2.0 * log_a
    e2 = jnp.exp(x2)
    series = x2 * (1.0 + x2 * (0.5 + x2 * (1.0 / 6.0 + x2 * (1.0 / 24.0 + x2 * (1.0 / 120.0)))))
    em1 = jnp.where(x2 > -0.1, series, e2 - 1.0)
    return jnp.exp(log_a), e2, jnp.sqrt(-em1)


def _softplus(x):
    return jnp.maximum(x, 0.0) + jnp.log1p(jnp.exp(-jnp.abs(x)))


def _rglru_fwd(u, conv_w, conv_b, w_rg, b_rg, w_ig, b_ig, lam, pad, tm):
    tp, w = u.shape
    groups, blk = w_rg.shape[0], w_rg.shape[1]

    def body(u_ref, cw_ref, cb_ref, wr_ref, br_ref, wi_ref, bi_ref, lam_ref,
             uc_ref, r_ref, ig_ref, hs_ref, uext, a_s, b_s, hc):
        i = pl.program_id(0)

        @pl.when(i == 0)
        def _():
            uext[0:SUBLANES, :] = jnp.zeros((SUBLANES, w), F32)
            hc[...] = jnp.zeros((SUBLANES, w), F32)

        uext[SUBLANES:SUBLANES + tm, :] = u_ref[...]
        cw = cw_ref[...]
        uc = cb_ref[...] + uext[pl.ds(SUBLANES - 3, tm), :] * cw[0:1, :]
        uc = uc + uext[pl.ds(SUBLANES - 2, tm), :] * cw[1:2, :]
        uc = uc + uext[pl.ds(SUBLANES - 1, tm), :] * cw[2:3, :]
        uc = uc + uext[pl.ds(SUBLANES, tm), :] * cw[3:4, :]
        uc_ref[...] = uc
        uext[0:SUBLANES, :] = uext[tm:tm + SUBLANES, :]

        sp = _softplus(-lam_ref[...])
        row = i * tm + lax.broadcasted_iota(jnp.int32, (tm, blk), 0)
        for g in range(groups):
            sl = slice(g * blk, (g + 1) * blk)
            ucg = uc_ref[:, sl]
            ucb = ucg.astype(BF16)
            r = _sigmoid(_dot(ucb, wr_ref[g]) + br_ref[:, sl])
            ig = _sigmoid(_dot(ucb, wi_ref[g]) + bi_ref[:, sl])
            r_ref[:, sl] = r
            ig_ref[:, sl] = ig
            a, _, mult = _lru_decay(r, sp[:, sl])
            mult = jnp.where(row == pad, 1.0, mult)
            a_s[:, sl] = a
            b_s[:, sl] = jnp.where(row < pad, 0.0, mult * (ig * ucg))

        row8 = lax.broadcasted_iota(jnp.int32, (SUBLANES, w), 0)

        def group(gi, h_in):
            off = pl.multiple_of(gi * SUBLANES, SUBLANES)
            av = a_s[pl.ds(off, SUBLANES), :]
            bv = b_s[pl.ds(off, SUBLANES), :]
            for k in (1, 2, 4):
                keep = row8 >= k
                bv = jnp.where(keep, av * pltpu.roll(bv, k, 0) + bv, bv)
                av = jnp.where(keep, av * pltpu.roll(av, k, 0), av)
            hv = av * h_in + bv
            hs_ref[pl.ds(off, SUBLANES), :] = hv
            return jnp.broadcast_to(hv[SUBLANES - 1:SUBLANES, :], (SUBLANES, w))

        hc[...] = lax.fori_loop(0, tm // SUBLANES, group, hc[...])

    row_spec = pl.BlockSpec((tm, w), lambda i: (i, 0))
    vec_spec = pl.BlockSpec((1, w), lambda i: (0, 0))
    mat_spec = pl.BlockSpec((groups, blk, blk), lambda i: (0, 0, 0))
    return pl.pallas_call(
        body, name="b_rglru_fwd", grid=(tp // tm,),
        in_specs=[row_spec, pl.BlockSpec((CONV_WIDTH, w), lambda i: (0, 0)), vec_spec,
                  mat_spec, vec_spec, mat_spec, vec_spec, vec_spec],
        out_specs=[row_spec, row_spec, row_spec, row_spec],
        out_shape=[jax.ShapeDtypeStruct((tp, w), F32)] * 4,
        scratch_shapes=[pltpu.VMEM((tm + SUBLANES, w), F32), pltpu.VMEM((tm, w), F32),
                        pltpu.VMEM((tm, w), F32), pltpu.VMEM((SUBLANES, w), F32)],
        compiler_params=_params(("arbitrary",)),
    )(u, conv_w, conv_b, w_rg, b_rg, w_ig, b_ig, lam)


def _final_loss(h, g, target_p, x0, tm):
    tp, d = h.shape

    def body(h_ref, g_ref, t_ref, dh_ref, loss_ref, dg_ref):
        i = pl.program_id(0)

        @pl.when(i == 0)
        def _():
            loss_ref[...] = jnp.zeros_like(loss_ref)
            dg_ref[...] = jnp.zeros_like(dg_ref)

        xv = h_ref[...]
        gv = g_ref[...]
        r = _rms(xv)
        xh = xv * r
        row = i * tm + lax.broadcasted_iota(jnp.int32, (tm, d), 0)
        err = jnp.where(row >= x0, xh * gv - t_ref[...], 0.0)
        loss_ref[...] += 0.5 * jnp.sum(jnp.mean(err * err, axis=-1, keepdims=True))
        dy = err / d
        dg_ref[...] += jnp.sum(dy * xh, axis=0, keepdims=True)
        dxh = dy * gv
        dh_ref[...] = r * (dxh - xh * jnp.mean(dxh * xh, axis=-1, keepdims=True))

    return pl.pallas_call(
        body, name="final_loss", grid=(tp // tm,),
        in_specs=[pl.BlockSpec((tm, d), lambda i: (i, 0)),
                  pl.BlockSpec((1, d), lambda i: (0, 0)),
                  pl.BlockSpec((tm, d), lambda i: (i, 0))],
        out_specs=[pl.BlockSpec((tm, d), lambda i: (i, 0)),
                   pl.BlockSpec((SUBLANES, LANES), lambda i: (0, 0)),
                   pl.BlockSpec((1, d), lambda i: (0, 0))],
        out_shape=[jax.ShapeDtypeStruct((tp, d), F32),
                   jax.ShapeDtypeStruct((SUBLANES, LANES), F32),
                   jax.ShapeDtypeStruct((1, d), F32)],
        compiler_params=_params(("arbitrary",)),
    )(h, g, target_p)


def _gated_out_bwd(name, dout, a, gate, w, tm, delta_heads=0):
    tp, wd = a.shape
    d = w.shape[1]

    def body(do_ref, a_ref, gate_ref, w_ref, da_ref, dgate_ref, dw_ref, *delta_ref):
        @pl.when(pl.program_id(0) == 0)
        def _():
            dw_ref[...] = jnp.zeros_like(dw_ref)

        gv = gate_ref[...]
        av = a_ref[...]
        sg = _sigmoid(gv)
        silu = gv * sg
        dob = do_ref[...].astype(BF16)
        dy = _dot_nt(dob, w_ref[...])
        da = dy * silu
        da_ref[...] = da
        dgate_ref[...] = dy * av * (sg * (1.0 + gv * (1.0 - sg)))
        dw_ref[...] += _dot_tn((av * silu).astype(BF16), dob)
        for h in range(delta_heads):
            sl = slice(h * V_HEAD, (h + 1) * V_HEAD)
            delta_ref[0][h] = jnp.broadcast_to(jnp.sum(da[:, sl] * av[:, sl], axis=-1, keepdims=True), (tm, LANES))

    out_specs = [pl.BlockSpec((tm, wd), lambda i: (i, 0)),
                 pl.BlockSpec((tm, wd), lambda i: (i, 0)),
                 pl.BlockSpec((wd, d), lambda i: (0, 0))]
    out_shape = [jax.ShapeDtypeStruct((tp, wd), F32),
                 jax.ShapeDtypeStruct((tp, wd), F32),
                 jax.ShapeDtypeStruct((wd, d), F32)]
    if delta_heads:
        out_specs.append(pl.BlockSpec((delta_heads, tm, LANES), lambda i: (0, i, 0)))
        out_shape.append(jax.ShapeDtypeStruct((delta_heads, tp, LANES), F32))
    return pl.pallas_call(
        body, name=name, grid=(tp // tm,),
        in_specs=[pl.BlockSpec((tm, d), lambda i: (i, 0)),
                  pl.BlockSpec((tm, wd), lambda i: (i, 0)),
                  pl.BlockSpec((tm, wd), lambda i: (i, 0)),
                  pl.BlockSpec((wd, d), lambda i: (0, 0))],
        out_specs=out_specs, out_shape=out_shape,
        compiler_params=_params(("arbitrary",)),
    )(dout, a, gate, w)


def _rglru_bwd(dhs, hs, r, ig, uc, u, conv_w, w_rg, w_ig, lam, pad, tm):
    tp, w = u.shape
    groups, blk = w_rg.shape[0], w_rg.shape[1]
    nt = tp // tm
    per8 = tm // SUBLANES

    def body(dhs_ref, hs_ref, hprev_ref, r_ref, ig_ref, uc_ref, u_ref, uprev_ref, cw_ref, wr_ref, wi_ref, lam_ref,
             du_ref, dcw_ref, dcb_ref, dwr_ref, dbr_ref, dwi_ref, dbi_ref, dlam_ref,
             aext, c_s, g_s, hext, uext, ducext, gc):
        step = pl.program_id(0)
        ti = nt - 1 - step

        @pl.when(step == 0)
        def _():
            for ref in (dcw_ref, dcb_ref, dwr_ref, dbr_ref, dwi_ref, dbi_ref, dlam_ref):
                ref[...] = jnp.zeros_like(ref)
            aext[tm:tm + SUBLANES, :] = jnp.zeros((SUBLANES, w), F32)
            ducext[tm:tm + SUBLANES, :] = jnp.zeros((SUBLANES, w), F32)
            gc[...] = jnp.zeros((SUBLANES, w), F32)

        lam_v = lam_ref[...]
        sp = _softplus(-lam_v)
        row = ti * tm + lax.broadcasted_iota(jnp.int32, (tm, w), 0)

        rv = r_ref[...]
        a, e2, mult = _lru_decay(rv, sp)
        aext[0:tm, :] = a
        c_s[...] = aext[pl.ds(1, tm), :]
        row8 = lax.broadcasted_iota(jnp.int32, (SUBLANES, w), 0)

        def group(gi, g_in):
            off = pl.multiple_of((per8 - 1 - gi) * SUBLANES, SUBLANES)
            cv = c_s[pl.ds(off, SUBLANES), :]
            dv = dhs_ref[pl.ds(off, SUBLANES), :]
            for k in (1, 2, 4):
                keep = row8 < SUBLANES - k
                dv = jnp.where(keep, cv * pltpu.roll(dv, SUBLANES - k, 0) + dv, dv)
                cv = jnp.where(keep, cv * pltpu.roll(cv, SUBLANES - k, 0), cv)
            gv = cv * g_in + dv
            g_s[pl.ds(off, SUBLANES), :] = gv
            return jnp.broadcast_to(gv[0:1, :], (SUBLANES, w))

        gc[...] = lax.fori_loop(0, per8, group, gc[...])
        aext[tm:tm + SUBLANES, :] = aext[0:SUBLANES, :]

        gsc = jnp.where(row < pad, 0.0, g_s[...])
        hext[0:SUBLANES, :] = hprev_ref[...]
        hext[SUBLANES:SUBLANES + tm, :] = hs_ref[...]
        hprev = jnp.where(row == 0, 0.0, hext[pl.ds(SUBLANES - 1, tm), :])
        igv = ig_ref[...]
        ucv = uc_ref[...]
        first = row == pad
        mult = jnp.where(first, 1.0, mult)
        dmult = gsc * (igv * ucv)
        dig = gsc * mult * ucv
        duc = gsc * mult * igv
        dlog_a = (gsc * hprev) * a + jnp.where(first, 0.0, dmult * (-e2 / mult))
        dlam_ref[...] += jnp.sum(dlog_a * rv, axis=0, keepdims=True) * (LRU_C * _sigmoid(-lam_v))
        dpre_r = dlog_a * (-LRU_C * sp) * (rv * (1.0 - rv))
        dpre_i = dig * (igv * (1.0 - igv))
        dbr_ref[...] += jnp.sum(dpre_r, axis=0, keepdims=True)
        dbi_ref[...] += jnp.sum(dpre_i, axis=0, keepdims=True)
        for g in range(groups):
            sl = slice(g * blk, (g + 1) * blk)
            ucb = ucv[:, sl].astype(BF16)
            drb = dpre_r[:, sl].astype(BF16)
            dib = dpre_i[:, sl].astype(BF16)
            dwr_ref[g] += _dot_tn(ucb, drb)
            dwi_ref[g] += _dot_tn(ucb, dib)
            ducext[0:tm, sl] = duc[:, sl] + _dot_nt(drb, wr_ref[g]) + _dot_nt(dib, wi_ref[g])

        ducv = ducext[0:tm, :]
        cw = cw_ref[...]
        dcb_ref[...] += jnp.sum(ducv, axis=0, keepdims=True)
        uext[0:SUBLANES, :] = jnp.where(ti == 0, 0.0, uprev_ref[...])
        uext[SUBLANES:SUBLANES + tm, :] = u_ref[...]
        for j in range(CONV_WIDTH):
            ush = uext[pl.ds(SUBLANES - (CONV_WIDTH - 1 - j), tm), :]
            dcw_ref[j:j + 1, :] += jnp.sum(ducv * ush, axis=0, keepdims=True)
        du = ducv * cw[3:4, :]
        for k in range(1, CONV_WIDTH):
            du = du + ducext[pl.ds(k, tm), :] * cw[3 - k:4 - k, :]
        du_ref[...] = du
        ducext[tm:tm + SUBLANES, :] = ducext[0:SUBLANES, :]

    rev = lambda s: (nt - 1 - s, 0)
    halo = lambda s: (jnp.maximum((nt - 1 - s) * per8 - 1, 0), 0)
    row_spec = pl.BlockSpec((tm, w), rev)
    halo_spec = pl.BlockSpec((SUBLANES, w), halo)
    vec_spec = pl.BlockSpec((1, w), lambda s: (0, 0))
    mat_spec = pl.BlockSpec((groups, blk, blk), lambda s: (0, 0, 0))
    cw_spec = pl.BlockSpec((CONV_WIDTH, w), lambda s: (0, 0))
    return pl.pallas_call(
        body, name="b_rglru_bwd", grid=(nt,),
        in_specs=[row_spec, row_spec, halo_spec, row_spec, row_spec, row_spec, row_spec, halo_spec,
                  cw_spec, mat_spec, mat_spec, vec_spec],
        out_specs=[row_spec, cw_spec, vec_spec, mat_spec, vec_spec, mat_spec, vec_spec, vec_spec],
        out_shape=[jax.ShapeDtypeStruct((tp, w), F32), jax.ShapeDtypeStruct((CONV_WIDTH, w), F32),
                   jax.ShapeDtypeStruct((1, w), F32), jax.ShapeDtypeStruct((groups, blk, blk), F32),
                   jax.ShapeDtypeStruct((1, w), F32), jax.ShapeDtypeStruct((groups, blk, blk), F32),
                   jax.ShapeDtypeStruct((1, w), F32), jax.ShapeDtypeStruct((1, w), F32)],
        scratch_shapes=[pltpu.VMEM((tm + SUBLANES, w), F32), pltpu.VMEM((tm, w), F32), pltpu.VMEM((tm, w), F32),
                        pltpu.VMEM((tm + SUBLANES, w), F32), pltpu.VMEM((tm + SUBLANES, w), F32),
                        pltpu.VMEM((tm + SUBLANES, w), F32), pltpu.VMEM((SUBLANES, w), F32)],
        compiler_params=_params(("arbitrary",)),
    )(dhs, hs, hs, r, ig, uc, u, u, conv_w, w_rg, w_ig, lam)


def _norm_matmul_bwd(name, x, g, w, dys, tm, resid=None, prologue=None, extra_out=None):
    tp, kin = x.shape
    n = w.shape[1]
    n_dy = len(dys)
    has_res = resid is not None
    has_extra = extra_out is not None

    def body(*refs):
        x_ref, g_ref, w_ref = refs[:3]
        dy_refs = refs[3:3 + n_dy]
        pos = 3 + n_dy
        res_ref = refs[pos] if has_res else None
        pos += int(has_res)
        dx_ref, dw_ref, dg_ref = refs[pos:pos + 3]
        pos += 3
        ex_ref = refs[pos] if has_extra else None
        pos += int(has_extra)
        dy_s = refs[pos]

        @pl.when(pl.program_id(0) == 0)
        def _():
            dw_ref[...] = jnp.zeros_like(dw_ref)
            dg_ref[...] = jnp.zeros_like(dg_ref)

        if prologue is None:
            c0 = 0
            for ref in dy_refs:
                dy_s[:, c0:c0 + ref.shape[1]] = ref[...].astype(BF16)
                c0 += ref.shape[1]
        else:
            prologue(dy_refs, dy_s, ex_ref)

        xv = x_ref[...]
        gv = g_ref[...]
        r = _rms(xv)
        xh = xv * r
        dyb = dy_s[...]
        dn = _dot_nt(dyb, w_ref[...])
        dw_ref[...] += _dot_tn((xh * gv).astype(BF16), dyb)
        dg_ref[...] += jnp.sum(dn * xh, axis=0, keepdims=True)
        dxh = dn * gv
        dx = r * (dxh - xh * jnp.mean(dxh * xh, axis=-1, keepdims=True))
        if has_res:
            dx = dx + res_ref[...]
        dx_ref[...] = dx

    row = lambda width: pl.BlockSpec((tm, width), lambda i: (i, 0))
    in_specs = [row(kin), pl.BlockSpec((1, kin), lambda i: (0, 0)), pl.BlockSpec((kin, n), lambda i: (0, 0))]
    in_specs += [row(a.shape[1]) for a in dys]
    args = [x, g, w, *dys]
    if has_res:
        in_specs.append(row(kin))
        args.append(resid)
    out_specs = [row(kin), pl.BlockSpec((kin, n), lambda i: (0, 0)), pl.BlockSpec((1, kin), lambda i: (0, 0))]
    out_shape = [jax.ShapeDtypeStruct((tp, kin), F32), jax.ShapeDtypeStruct((kin, n), F32),
                 jax.ShapeDtypeStruct((1, kin), F32)]
    if has_extra:
        out_specs.append(row(extra_out[0]))
        out_shape.append(jax.ShapeDtypeStruct((tp, extra_out[0]), extra_out[1]))
    return pl.pallas_call(
        body, name=name, grid=(tp // tm,),
        in_specs=in_specs, out_specs=out_specs, out_shape=out_shape,
        scratch_shapes=[pltpu.VMEM((tm, n), BF16)],
        compiler_params=_params(("arbitrary",)),
    )(*args)


def _flash_bwd(q, k, v, lse, delta, do, heads, pad, tq):
    tp = q.shape[0]
    nq = tp // tq
    scale = (QK_NOPE + QK_ROPE) ** -0.5
    c2 = scale * LOG2E
    ts = tq // FLASH_SUBTILES

    def body(q_ref, k_ref, v_ref, lse_ref, delta_ref, do_ref, dq_ref, dk_ref, dv_ref):
        j = pl.program_id(1)

        @pl.when(j == 0)
        def _():
            dq_ref[...] = jnp.zeros_like(dq_ref)

        kv = k_ref[...]
        vv = v_ref[...]

        def make_step(masked):
            def step(i, carry):
                dk, dv = carry
                for u in range(FLASH_SUBTILES):
                    off = pl.multiple_of(i * tq + u * ts, ts)
                    qv = q_ref[pl.ds(off, ts), :]
                    dob = do_ref[pl.ds(off, ts), :].astype(BF16)
                    p = jnp.exp2(_dot_nt(qv, kv) * c2 - lse_ref[pl.ds(off, ts), 0:1])
                    if masked:
                        p = jnp.where(_attn_mask(j * tq + u * ts, j * tq, ts, tq, pad), p, 0.0)
                    dv = dv + _dot_tn(p.astype(BF16), dob)
                    dp = _dot_nt(dob, vv)
                    ds = (p * (dp - delta_ref[pl.ds(off, ts), 0:1]) * scale).astype(BF16)
                    dk = dk + _dot_tn(ds, qv)
                    dq_ref[pl.ds(off, ts), :] += _dot(ds, kv)
                return dk, dv
            return step

        carry = make_step(True)(j, (jnp.zeros((tq, HEAD_W), F32), jnp.zeros((tq, V_HEAD), F32)))
        dk, dv = lax.fori_loop(j + 1, nq, make_step(False), carry)
        dk_ref[...] = dk
        dv_ref[...] = dv

    return pl.pallas_call(
        body, name="a_flash_bwd", grid=(heads, nq),
        in_specs=[pl.BlockSpec((tp, HEAD_W), lambda h, j: (0, h)),
                  pl.BlockSpec((tq, HEAD_W), lambda h, j: (j, h)),
                  pl.BlockSpec((tq, V_HEAD), lambda h, j: (j, h)),
                  pl.BlockSpec((None, tp, LANES), lambda h, j: (h, 0, 0)),
                  pl.BlockSpec((None, tp, LANES), lambda h, j: (h, 0, 0)),
                  pl.BlockSpec((tp, V_HEAD), lambda h, j: (0, h))],
        out_specs=[pl.BlockSpec((tp, HEAD_W), lambda h, j: (0, h)),
                   pl.BlockSpec((tq, HEAD_W), lambda h, j: (j, h)),
                   pl.BlockSpec((tq, V_HEAD), lambda h, j: (j, h))],
        out_shape=[jax.ShapeDtypeStruct((tp, heads * HEAD_W), F32),
                   jax.ShapeDtypeStruct((tp, heads * HEAD_W), F32),
                   jax.ShapeDtypeStruct((tp, heads * V_HEAD), F32)],
        compiler_params=_params(("parallel", "arbitrary")),
    )(q, k, v, lse, delta, do)


def _position():
    return lax.axis_index("x"), lax.axis_index("y"), lax.axis_index("c")


def _other_chips(x, y):
    return [(1 - x, y), (x, 1 - y), (1 - x, 1 - y)]


def _block(ref, shard_axis, n, k, split_axis=None, m=None, h=None):
    idx = []
    for a in range(len(ref.shape)):
        start = 0
        size = None
        if a == shard_axis:
            start, size = k * n, n
        if a == split_axis:
            size = (n if a == shard_axis else m) // 2
            start = start + h * size
        idx.append(slice(None) if size is None else pl.ds(start, size))
    return ref.at[tuple(idx)]


def _gather_weights(split, whole_small):
    ns, nw = len(split), len(whole_small)
    n = ns + nw
    arrs = [s[0] for s in split] + [s[0] for s in whole_small]
    axes = [s[1] for s in split] + [s[1] for s in whole_small]

    def body(*refs):
        ins, outs = refs[:n], refs[n:2 * n]
        ici_send, ici_recv, d2d_send, d2d_recv, local_sems = refs[2 * n:]
        x, y, c = _position()
        me = 2 * x + y
        others = _other_chips(x, y)
        sent, local = [], []

        def remote(src, dst, sems, idx, to):
            return pltpu.make_async_remote_copy(src_ref=src, dst_ref=dst, send_sem=sems[0].at[idx],
                                                recv_sem=sems[1].at[idx], device_id=to, device_id_type=MESH)

        for a in range(n):
            width = ins[a].shape[axes[a]]
            mine = pltpu.make_async_copy(ins[a], _block(outs[a], axes[a], width, me), local_sems.at[a])
            mine.start()
            local.append(mine)
            for j, (px, py) in enumerate(others):
                if a < ns:
                    sx = split[a][2]
                    src = _block(ins[a], None, None, None, sx, ins[a].shape[sx], c)
                    dst = _block(outs[a], axes[a], width, me, sx, outs[a].shape[sx], c)
                else:
                    src, dst = ins[a], _block(outs[a], axes[a], width, me)
                cp = remote(src, dst, (ici_send, ici_recv), 3 * a + j, (px, py, c))
                cp.start()
                sent.append(cp)
        for a in range(ns):
            width = ins[a].shape[axes[a]]
            sx = split[a][2]
            for j, (px, py) in enumerate(others):
                theirs = _block(outs[a], axes[a], width, 2 * px + py, sx, outs[a].shape[sx], c)
                remote(theirs, theirs, (ici_send, ici_recv), 3 * a + j, (px, py, c)).wait_recv()
                fwd = remote(theirs, theirs, (d2d_send, d2d_recv), 3 * a + j, (x, y, 1 - c))
                fwd.start()
                sent.append(fwd)
        for a in range(ns, n):
            width = ins[a].shape[axes[a]]
            for j, (px, py) in enumerate(others):
                theirs = _block(outs[a], axes[a], width, 2 * px + py)
                remote(theirs, theirs, (ici_send, ici_recv), 3 * a + j, (px, py, c)).wait_recv()
        for a in range(ns):
            width = ins[a].shape[axes[a]]
            sx = split[a][2]
            for j, (px, py) in enumerate(others):
                from_sibling = _block(outs[a], axes[a], width, 2 * px + py, sx, outs[a].shape[sx], 1 - c)
                remote(from_sibling, from_sibling, (d2d_send, d2d_recv), 3 * a + j, (x, y, 1 - c)).wait_recv()
        for cp in sent:
            cp.wait_send()
        for cp in local:
            cp.wait()

    def whole_shape(a, axis):
        shape = list(a.shape)
        shape[axis] *= N_CHIPS
        return jax.ShapeDtypeStruct(tuple(shape), a.dtype)

    any_spec = pl.BlockSpec(memory_space=pl.ANY)
    return pl.pallas_call(
        body, name="gather_weights",
        in_specs=[any_spec] * n, out_specs=[any_spec] * n,
        out_shape=[whole_shape(a, ax) for a, ax in zip(arrs, axes)],
        scratch_shapes=[pltpu.SemaphoreType.DMA((3 * n,)), pltpu.SemaphoreType.DMA((3 * n,)),
                        pltpu.SemaphoreType.DMA((3 * ns,)), pltpu.SemaphoreType.DMA((3 * ns,)),
                        pltpu.SemaphoreType.DMA((n,))],
        compiler_params=pltpu.CompilerParams(has_side_effects=True),
    )(*arrs)


def _swap_half_with_sibling(grads):
    nk, rows2, cols = grads.shape
    rows = rows2 // 2

    def body(g_ref, got_ref, send_sem, recv_sem):
        x, y, c = _position()
        cp = pltpu.make_async_remote_copy(
            src_ref=g_ref.at[:, pl.ds((1 - c) * rows, rows), :], dst_ref=got_ref, send_sem=send_sem, recv_sem=recv_sem,
            device_id=(x, y, 1 - c), device_id_type=MESH)
        cp.start()
        cp.wait()

    any_spec = pl.BlockSpec(memory_space=pl.ANY)
    return pl.pallas_call(
        body, name="grads_to_sibling",
        in_specs=[any_spec], out_specs=any_spec,
        out_shape=jax.ShapeDtypeStruct((nk, rows, cols), grads.dtype),
        scratch_shapes=[pltpu.SemaphoreType.DMA, pltpu.SemaphoreType.DMA],
        compiler_params=pltpu.CompilerParams(has_side_effects=True),
    )(grads)


def _add_halves(grads, got, c):
    nk, rows, cols = got.shape
    tr = 256 if rows % 256 == 0 else SUBLANES
    nblk = rows // tr

    def body(c_ref, g_ref, got_ref, o_ref):
        o_ref[...] = (g_ref[...] + got_ref[...]).astype(BF16)

    return pl.pallas_call(
        body, name="grads_chip_sum",
        grid_spec=pltpu.PrefetchScalarGridSpec(
            num_scalar_prefetch=1, grid=(nk, nblk),
            in_specs=[pl.BlockSpec((None, tr, cols), lambda k, i, c_ref: (k, c_ref[0] * nblk + i, 0)),
                      pl.BlockSpec((None, tr, cols), lambda k, i, c_ref: (k, i, 0))],
            out_specs=pl.BlockSpec((None, tr, cols), lambda k, i, c_ref: (k, i, 0))),
        out_shape=jax.ShapeDtypeStruct((nk, rows, cols), BF16),
        compiler_params=_params(("parallel", "parallel")),
    )(c, grads, got)


def _scatter_to_chips(part):
    nk, rows, cols = part.shape

    def body(p_ref, got_ref, send_sems, recv_sems, local_sem):
        x, y, c = _position()
        me = 2 * x + y
        mine = pltpu.make_async_copy(p_ref.at[me], got_ref.at[me], local_sem)
        mine.start()
        copies = [mine]
        for j, (px, py) in enumerate(_other_chips(x, y)):
            cp = pltpu.make_async_remote_copy(
                src_ref=p_ref.at[2 * px + py], dst_ref=got_ref.at[me],
                send_sem=send_sems.at[j], recv_sem=recv_sems.at[j],
                device_id=(px, py, c), device_id_type=MESH)
            cp.start()
            copies.append(cp)
        for cp in copies:
            cp.wait()

    any_spec = pl.BlockSpec(memory_space=pl.ANY)
    return pl.pallas_call(
        body, name="grads_to_chips",
        in_specs=[any_spec], out_specs=any_spec,
        out_shape=jax.ShapeDtypeStruct((nk, rows, cols), part.dtype),
        scratch_shapes=[pltpu.SemaphoreType.DMA((3,)), pltpu.SemaphoreType.DMA((3,)), pltpu.SemaphoreType.DMA],
        compiler_params=pltpu.CompilerParams(has_side_effects=True),
    )(part)


def _sum_chips(got):
    nk, rows, cols = got.shape
    tr = 256 if rows % 256 == 0 else SUBLANES

    def body(g_ref, o_ref):
        acc = g_ref[0].astype(F32)
        for k in range(1, nk):
            acc = acc + g_ref[k].astype(F32)
        o_ref[...] = acc

    return pl.pallas_call(
        body, name="grads_sum", grid=(rows // tr,),
        in_specs=[pl.BlockSpec((nk, tr, cols), lambda i: (0, i, 0))],
        out_specs=pl.BlockSpec((tr, cols), lambda i: (i, 0)),
        out_shape=jax.ShapeDtypeStruct((rows, cols), F32),
        compiler_params=_params(("parallel",)),
    )(got)


def _share_with_sibling(half):
    rows, cols = half.shape

    def body(h_ref, full_ref, send_sem, recv_sem, local_sem):
        x, y, c = _position()
        mine = pltpu.make_async_copy(h_ref, full_ref.at[c], local_sem)
        mine.start()
        cp = pltpu.make_async_remote_copy(
            src_ref=h_ref, dst_ref=full_ref.at[c], send_sem=send_sem, recv_sem=recv_sem,
            device_id=(x, y, 1 - c), device_id_type=MESH)
        cp.start()
        cp.wait()
        mine.wait()

    any_spec = pl.BlockSpec(memory_space=pl.ANY)
    return pl.pallas_call(
        body, name="grads_share",
        in_specs=[any_spec], out_specs=any_spec,
        out_shape=jax.ShapeDtypeStruct((2, rows, cols), half.dtype),
        scratch_shapes=[pltpu.SemaphoreType.DMA, pltpu.SemaphoreType.DMA, pltpu.SemaphoreType.DMA],
        compiler_params=pltpu.CompilerParams(has_side_effects=True),
    )(half)


def _adamw(name, w, g, m, v):
    rows, cols = w.shape
    tr = 256 if rows % 256 == 0 else rows

    def body(w_ref, g_ref, m_ref, v_ref, d_ref, nm_ref, nv_ref):
        gv = g_ref[...]
        mn = ADAM_B1 * m_ref[...] + (1.0 - ADAM_B1) * gv
        vn = ADAM_B2 * v_ref[...] + (1.0 - ADAM_B2) * (gv * gv)
        m_hat = mn / (1.0 - ADAM_B1 ** ADAM_STEP)
        v_hat = vn / (1.0 - ADAM_B2 ** ADAM_STEP)
        d_ref[...] = -ADAM_LR * (m_hat / (jnp.sqrt(v_hat) + ADAM_EPS) + ADAM_WD * w_ref[...])
        nm_ref[...] = mn
        nv_ref[...] = vn

    spec = pl.BlockSpec((tr, cols), lambda i: (i, 0))
    return pl.pallas_call(
        body, name=name, grid=(rows // tr,),
        in_specs=[spec] * 4, out_specs=[spec] * 3,
        out_shape=[jax.ShapeDtypeStruct((rows, cols), F32)] * 3,
        compiler_params=_params(("parallel",)),
    )(w, g, m, v)


def _as2d(a):
    if a.ndim == 1:
        return a.reshape(1, -1)
    return a.reshape(-1, a.shape[-1])


def _unshard(gathered, axis):
    moved = jnp.moveaxis(gathered, 0, axis)
    shape = list(gathered.shape[1:])
    shape[axis] *= N_CHIPS
    return moved.reshape(shape)


def _rope_tables(tp, pad):
    pos = jnp.arange(tp, dtype=F32) - pad
    inv_freq = ROPE_BASE ** (-jnp.arange(0, QK_ROPE, 2, dtype=F32) / QK_ROPE)
    ang = pos[:, None] * inv_freq[None, :]
    cos, sin = jnp.cos(ang), jnp.sin(ang)
    zeros = jnp.zeros((tp, LANES - QK_ROPE), F32)
    return jnp.concatenate([cos, cos, zeros], axis=1), jnp.concatenate([-sin, sin, zeros], axis=1)


def _local_grads(x, target, wt, heads):
    seq, d = x.shape
    n_meta = wt['meta_tokens'].shape[0]
    t = seq + n_meta
    pad = (-t) % Q_BLOCK
    tp = t + pad
    x0 = pad + n_meta
    tm = _row_tile(tp)
    ql = wt['a_q_norm_g'].shape[1]
    kvl = wt['a_kv_norm_g'].shape[1]
    mla_w = heads * V_HEAD
    lru_w = wt['b_conv_w'].shape[1]

    h0 = jnp.concatenate([jnp.zeros((pad, d), F32), wt['meta_tokens'], x], axis=0)
    target_p = jnp.concatenate([jnp.zeros((x0, d), F32), target], axis=0)
    cos_t, sin_t = _rope_tables(tp, pad)

    w_in_a = wt['a_w_in']
    zcol = jnp.zeros((d, LANES - QK_ROPE), BF16)
    w_in_a = jnp.concatenate([w_in_a[:, :ql + kvl + QK_ROPE], zcol, w_in_a[:, ql + kvl + QK_ROPE:]], axis=1)
    c_kv, c_kr, c_gate = ql, ql + kvl, ql + kvl + LANES
    splits_a = [(0, c_kv), (c_kv, c_kr), (c_kr, c_gate), (c_gate, c_gate + mla_w)]
    w_uq = wt['a_w_uq'].reshape(ql, heads, QK_NOPE + QK_ROPE)
    w_uq = jnp.pad(w_uq, ((0, 0), (0, 0), (0, HEAD_W - QK_NOPE - QK_ROPE))).reshape(ql, heads * HEAD_W)
    w_ukv = wt['a_w_ukv']

    q_lat, kv_lat, kr_raw, gate_a = _norm_matmul("a_in_proj", h0, wt['a_norm_g'], w_in_a, splits_a, tm)
    q = _q_proj(q_lat, wt['a_q_norm_g'], w_uq, cos_t, sin_t, heads, tm)
    k, v = _kv_proj(kv_lat, wt['a_kv_norm_g'], w_ukv, kr_raw, cos_t, sin_t, heads, tm)
    attn, lse = _flash_fwd(q, k, v, heads, pad, tm)
    h1 = _gated_out("a_out_proj", attn, gate_a, wt['a_w_out'], h0, tm)

    u, gate_b = _norm_matmul("b_in_proj", h1, wt['b_norm_g'], wt['b_w_in'], [(0, lru_w), (lru_w, 2 * lru_w)], tm)
    uc, r, ig, hs = _rglru_fwd(u, wt['b_conv_w'], wt['b_conv_b'], wt['b_w_rg'], wt['b_b_rg'],
                               wt['b_w_ig'], wt['b_b_ig'], wt['b_lam'], pad, tm)
    h2 = _gated_out("b_out_proj", hs, gate_b, wt['b_w_out'], h1, tm)

    dh2, loss, d_final_g = _final_loss(h2, wt['final_norm_g'], target_p, x0, tm)

    dhs, dgate_b, dw_out_b = _gated_out_bwd("b_out_proj_bwd", dh2, hs, gate_b, wt['b_w_out'], tm)
    du, dconv_w, dconv_b, dw_rg, db_rg, dw_ig, db_ig, dlam = _rglru_bwd(
        dhs, hs, r, ig, uc, u, wt['b_conv_w'], wt['b_w_rg'], wt['b_w_ig'], wt['b_lam'], pad, tm)
    dh1, dw_in_b, dg_b = _norm_matmul_bwd("b_in_proj_bwd", h1, wt['b_norm_g'], wt['b_w_in'], [du, dgate_b], tm, resid=dh2)

    dattn, dgate_a, dw_out_a, delta = _gated_out_bwd("a_out_proj_bwd", dh1, attn, gate_a, wt['a_w_out'], tm,
                                                     delta_heads=heads)
    dq, dk, dv = _flash_bwd(q, k, v, lse, delta, dattn, heads, pad, tm)

    def q_prologue(dy_refs, dy_s, ex_ref):
        (dq_ref,), cos_v, sin_v = dy_refs[:1], dy_refs[1][...], dy_refs[2][...]
        for h in range(heads):
            c0 = h * HEAD_W
            dy_s[:, c0:c0 + QK_NOPE] = dq_ref[:, c0:c0 + QK_NOPE].astype(BF16)
            dy_s[:, c0 + QK_NOPE:c0 + HEAD_W] = _unrope(dq_ref[:, c0 + QK_NOPE:c0 + HEAD_W], cos_v, sin_v).astype(BF16)

    dq_lat, dw_uq, dg_q = _norm_matmul_bwd("a_q_proj_bwd", q_lat, wt['a_q_norm_g'], w_uq, [dq, cos_t, sin_t], tm,
                                           prologue=q_prologue)

    def kv_prologue(dy_refs, dy_s, ex_ref):
        dk_ref, dv_ref = dy_refs[:2]
        cos_v, sin_v = dy_refs[2][...], dy_refs[3][...]
        dkr = jnp.zeros((dk_ref.shape[0], LANES), F32)
        for h in range(heads):
            c0 = h * (QK_NOPE + V_HEAD)
            dy_s[:, c0:c0 + QK_NOPE] = dk_ref[:, h * HEAD_W:h * HEAD_W + QK_NOPE].astype(BF16)
            dy_s[:, c0 + QK_NOPE:c0 + QK_NOPE + V_HEAD] = dv_ref[:, h * V_HEAD:(h + 1) * V_HEAD].astype(BF16)
            dkr = dkr + dk_ref[:, h * HEAD_W + QK_NOPE:(h + 1) * HEAD_W]
        ex_ref[...] = _unrope(dkr, cos_v, sin_v)

    dkv_lat, dw_ukv, dg_kv, dkr_raw = _norm_matmul_bwd(
        "a_kv_proj_bwd", kv_lat, wt['a_kv_norm_g'], w_ukv, [dk, dv, cos_t, sin_t], tm,
        prologue=kv_prologue, extra_out=(LANES, F32))

    dh0, dw_in_a, dg_a = _norm_matmul_bwd("a_in_proj_bwd", h0, wt['a_norm_g'], w_in_a,
                                          [dq_lat, dkv_lat, dkr_raw, dgate_a], tm, resid=dh1)

    dw_in_a = jnp.concatenate([dw_in_a[:, :c_kr + QK_ROPE], dw_in_a[:, c_gate:]], axis=1)
    dw_uq = dw_uq.reshape(ql, heads, HEAD_W)[:, :, :QK_NOPE + QK_ROPE].reshape(ql, heads * (QK_NOPE + QK_ROPE))
    grads = {
        'meta_tokens': dh0[pad:x0], 'a_norm_g': dg_a, 'a_w_in': dw_in_a, 'a_q_norm_g': dg_q, 'a_kv_norm_g': dg_kv,
        'a_w_uq': dw_uq, 'a_w_ukv': dw_ukv, 'a_w_out': dw_out_a, 'b_norm_g': dg_b, 'b_w_in': dw_in_b,
        'b_conv_w': dconv_w, 'b_conv_b': dconv_b, 'b_w_rg': dw_rg, 'b_b_rg': db_rg, 'b_w_ig': dw_ig,
        'b_b_ig': db_ig, 'b_lam': dlam, 'b_w_out': dw_out_b, 'final_norm_g': d_final_g,
    }
    return loss, dh0[x0:], grads


def _chip_major(whole, local_shape, axis):
    if axis is None:
        return jnp.broadcast_to(whole.reshape(1, -1), (N_CHIPS, whole.size))
    shape = list(local_shape)
    g = whole.reshape(shape[:axis] + [N_CHIPS, shape[axis]] + shape[axis + 1:])
    return jnp.moveaxis(g, axis, 0).reshape(N_CHIPS, -1)


def kernel(x, meta_tokens, a_norm_g, a_w_in, a_q_norm_g, a_kv_norm_g, a_w_uq, a_w_ukv, a_w_out, b_norm_g, b_w_in, b_conv_w, b_conv_b, b_w_rg, b_b_rg, b_w_ig, b_b_ig, b_lam, b_w_out, final_norm_g, loss_target, m_meta_tokens, m_a_norm_g, m_a_w_in, m_a_q_norm_g, m_a_kv_norm_g, m_a_w_uq, m_a_w_ukv, m_a_w_out, m_b_norm_g, m_b_w_in, m_b_conv_w, m_b_conv_b, m_b_w_rg, m_b_b_rg, m_b_w_ig, m_b_b_ig, m_b_lam, m_b_w_out, m_final_norm_g, v_meta_tokens, v_a_norm_g, v_a_w_in, v_a_q_norm_g, v_a_kv_norm_g, v_a_w_uq, v_a_w_ukv, v_a_w_out, v_b_norm_g, v_b_w_in, v_b_conv_w, v_b_conv_b, v_b_w_rg, v_b_b_rg, v_b_w_ig, v_b_b_ig, v_b_lam, v_b_w_out, v_final_norm_g):
    local_w = dict(zip(WEIGHTS, (meta_tokens, a_norm_g, a_w_in, a_q_norm_g, a_kv_norm_g, a_w_uq, a_w_ukv, a_w_out,
                                 b_norm_g, b_w_in, b_conv_w, b_conv_b, b_w_rg, b_b_rg, b_w_ig, b_b_ig, b_lam,
                                 b_w_out, final_norm_g)))
    local_m = dict(zip(WEIGHTS, (m_meta_tokens, m_a_norm_g, m_a_w_in, m_a_q_norm_g, m_a_kv_norm_g, m_a_w_uq,
                                 m_a_w_ukv, m_a_w_out, m_b_norm_g, m_b_w_in, m_b_conv_w, m_b_conv_b, m_b_w_rg,
                                 m_b_b_rg, m_b_w_ig, m_b_b_ig, m_b_lam, m_b_w_out, m_final_norm_g)))
    local_v = dict(zip(WEIGHTS, (v_meta_tokens, v_a_norm_g, v_a_w_in, v_a_q_norm_g, v_a_kv_norm_g, v_a_w_uq,
                                 v_a_w_ukv, v_a_w_out, v_b_norm_g, v_b_w_in, v_b_conv_w, v_b_conv_b, v_b_w_rg,
                                 v_b_b_rg, v_b_w_ig, v_b_b_ig, v_b_lam, v_b_w_out, v_final_norm_g)))
    matrices = ('a_w_in', 'a_w_uq', 'a_w_ukv', 'a_w_out', 'b_w_in', 'b_w_rg', 'b_w_ig', 'b_w_out')
    heads = a_w_uq.shape[-1] * N_CHIPS // (QK_NOPE + QK_ROPE)

    split, small = [], []
    for n in WEIGHTS:
        if SHARD_AXIS[n] is None:
            continue
        if n == 'a_w_in':
            split.append((n, local_w[n].astype(BF16)[None], 0, 2))
        elif n in matrices:
            split.append((n, local_w[n].astype(BF16), SHARD_AXIS[n], 1))
        else:
            small.append((n, local_w[n], SHARD_AXIS[n]))
    gathered = _gather_weights([s[1:] for s in split], [s[1:] for s in small])
    whole = dict(zip([s[0] for s in split + small], gathered))
    whole['a_w_in'] = _unshard(whole['a_w_in'], SHARD_AXIS['a_w_in'])
    wt = {}
    for n in WEIGHTS:
        w = whole.get(n, local_w[n])
        wt[n] = w[0] if n in ('b_w_rg', 'b_w_ig', 'b_conv_w') else _as2d(w)

    loss, grad_x, grads = _local_grads(x[0], loss_target[0], wt, heads)

    pieces = [_chip_major(grads[n], local_w[n].shape, SHARD_AXIS[n]) for n in WEIGHTS]
    length = sum(p.shape[1] for p in pieces)
    unit = 2 * (256 if length >= 2 ** 20 else SUBLANES) * 1024
    padded = -(-length // unit) * unit
    flat = jnp.concatenate(pieces + [jnp.zeros((N_CHIPS, padded - length), F32)], axis=1)
    flat = flat.reshape(N_CHIPS, padded // 1024, 1024)
    c_idx = lax.axis_index("c").astype(jnp.int32).reshape(1)
    got = _swap_half_with_sibling(flat)
    chip_sum = _add_halves(flat, got, c_idx)
    from_chips = _scatter_to_chips(chip_sum)
    my_half = _sum_chips(from_chips)
    total = _share_with_sibling(my_half).reshape(-1)

    out_g, out_d, out_m, out_v = [], [], [], []
    off = 0
    for n in WEIGHTS:
        shape = local_w[n].shape
        size = 1
        for s in shape:
            size *= s
        g = total[off:off + size].reshape(shape)
        off += size
        delta, new_m, new_v = _adamw("adamw_" + n, _as2d(local_w[n]), _as2d(g), _as2d(local_m[n]), _as2d(local_v[n]))
        out_g.append(g)
        out_d.append(delta.reshape(shape))
        out_m.append(new_m.reshape(shape))
        out_v.append(new_v.reshape(shape))

    loss = lax.psum(loss[0, 0], ("x", "y", "c"))
    return (loss, grad_x[None], *out_g, *out_d, *out_m, *out_v)
```

```python
import functools

import jax
import jax.numpy as jnp
from jax import lax
from jax.experimental import pallas as pl
from jax.experimental.pallas import tpu as pltpu

F32 = jnp.float32
BF16 = jnp.bfloat16
MESH = pl.DeviceIdType.MESH

RMS_EPS = 1e-6
QK_NOPE = 128
QK_ROPE = 64
V_HEAD = 128
HEAD_W = 256
ROPE_BASE = 10000.0
Q_BLOCK = 128
MASK_VALUE = -1e30
CONV_WIDTH = 4
LRU_C = 8.0
N_CHIPS = 4

ADAM_LR = 0.001
ADAM_B1 = 0.9
ADAM_B2 = 0.999
ADAM_EPS = 1e-08
ADAM_WD = 0.01
ADAM_STEP = 10

VMEM_LIMIT_V7X = 56 * 1024 * 1024
LANES = 128
SUBLANES = 8

WEIGHTS = ['meta_tokens', 'a_norm_g', 'a_w_in', 'a_q_norm_g', 'a_kv_norm_g', 'a_w_uq', 'a_w_ukv',
           'a_w_out', 'b_norm_g', 'b_w_in', 'b_conv_w', 'b_conv_b', 'b_w_rg', 'b_b_rg', 'b_w_ig',
           'b_b_ig', 'b_lam', 'b_w_out', 'final_norm_g']
SHARD_AXIS = {'meta_tokens': 1, 'a_norm_g': None, 'a_w_in': 2, 'a_q_norm_g': None, 'a_kv_norm_g': None,
              'a_w_uq': 2, 'a_w_ukv': 2, 'a_w_out': 1, 'b_norm_g': 1, 'b_w_in': 2, 'b_conv_w': 2,
              'b_conv_b': 1, 'b_w_rg': 2, 'b_b_rg': 1, 'b_w_ig': 2, 'b_b_ig': 1, 'b_lam': 1,
              'b_w_out': 1, 'final_norm_g': None}


def _params(sem=None):
    return pltpu.CompilerParams(dimension_semantics=sem, vmem_limit_bytes=VMEM_LIMIT_V7X)


def _row_tile(tp):
    return 384 if (tp % 384 == 0 and tp >= 1152) else 128


def _sigmoid(x):
    return 1.0 / (1.0 + jnp.exp(-x))


def _rms(x):
    return lax.rsqrt(jnp.mean(x * x, axis=-1, keepdims=True) + RMS_EPS)


def _swap32(x):
    lane = lax.broadcasted_iota(jnp.int32, x.shape, 1)
    return jnp.where(lane < 32, pltpu.roll(x, 96, 1), pltpu.roll(x, 32, 1))


def _rope(x, cos_t, sin_t):
    return x * cos_t + _swap32(x) * sin_t


def _unrope(d, cos_t, sin_t):
    lane = lax.broadcasted_iota(jnp.int32, d.shape, 1)
    return jnp.where(lane < QK_ROPE, d * cos_t + _swap32(d * sin_t), 0.0)


def _dot(a, b):
    return jnp.dot(a, b, preferred_element_type=F32)


def _dot_nt(a, b):
    return lax.dot_general(a, b, (((1,), (1,)), ((), ())), preferred_element_type=F32)


def _dot_tn(a, b):
    return lax.dot_general(a, b, (((0,), (0,)), ((), ())), preferred_element_type=F32)


def _norm_matmul(name, x, g, w, splits, tm):
    tp, kin = x.shape
    n = w.shape[1]

    def body(x_ref, g_ref, w_ref, *outs):
        xv = x_ref[...]
        nrm = ((xv * _rms(xv)) * g_ref[...]).astype(BF16)
        y = _dot(nrm, w_ref[...])
        for o_ref, (c0, c1) in zip(outs, splits):
            o_ref[...] = y[:, c0:c1]

    return pl.pallas_call(
        body, name=name, grid=(tp // tm,),
        in_specs=[pl.BlockSpec((tm, kin), lambda i: (i, 0)),
                  pl.BlockSpec((1, kin), lambda i: (0, 0)),
                  pl.BlockSpec((kin, n), lambda i: (0, 0))],
        out_specs=[pl.BlockSpec((tm, c1 - c0), lambda i: (i, 0)) for c0, c1 in splits],
        out_shape=[jax.ShapeDtypeStruct((tp, c1 - c0), F32) for c0, c1 in splits],
        compiler_params=_params(("parallel",)),
    )(x, g, w)


def _q_proj(q_lat, g, w_uq, cos_t, sin_t, heads, tm):
    tp, kin = q_lat.shape
    n = heads * HEAD_W

    def body(x_ref, g_ref, w_ref, cos_ref, sin_ref, q_ref):
        xv = x_ref[...]
        nrm = ((xv * _rms(xv)) * g_ref[...]).astype(BF16)
        y = _dot(nrm, w_ref[...])
        cos_v, sin_v = cos_ref[...], sin_ref[...]
        for h in range(heads):
            c0 = h * HEAD_W
            q_ref[:, c0:c0 + QK_NOPE] = y[:, c0:c0 + QK_NOPE].astype(BF16)
            q_ref[:, c0 + QK_NOPE:c0 + HEAD_W] = _rope(y[:, c0 + QK_NOPE:c0 + HEAD_W], cos_v, sin_v).astype(BF16)

    return pl.pallas_call(
        body, name="a_q_proj", grid=(tp // tm,),
        in_specs=[pl.BlockSpec((tm, kin), lambda i: (i, 0)),
                  pl.BlockSpec((1, kin), lambda i: (0, 0)),
                  pl.BlockSpec((kin, n), lambda i: (0, 0)),
                  pl.BlockSpec((tm, LANES), lambda i: (i, 0)),
                  pl.BlockSpec((tm, LANES), lambda i: (i, 0))],
        out_specs=pl.BlockSpec((tm, n), lambda i: (i, 0)),
        out_shape=jax.ShapeDtypeStruct((tp, n), BF16),
        compiler_params=_params(("parallel",)),
    )(q_lat, g, w_uq, cos_t, sin_t)


def _kv_proj(kv_lat, g, w_ukv, k_rope_raw, cos_t, sin_t, heads, tm):
    tp, kin = kv_lat.shape
    n = heads * (QK_NOPE + V_HEAD)

    def body(x_ref, g_ref, w_ref, kr_ref, cos_ref, sin_ref, k_ref, v_ref):
        xv = x_ref[...]
        nrm = ((xv * _rms(xv)) * g_ref[...]).astype(BF16)
        y = _dot(nrm, w_ref[...])
        kr = _rope(kr_ref[...], cos_ref[...], sin_ref[...]).astype(BF16)
        for h in range(heads):
            c0 = h * (QK_NOPE + V_HEAD)
            k_ref[:, h * HEAD_W:h * HEAD_W + QK_NOPE] = y[:, c0:c0 + QK_NOPE].astype(BF16)
            k_ref[:, h * HEAD_W + QK_NOPE:(h + 1) * HEAD_W] = kr
            v_ref[:, h * V_HEAD:(h + 1) * V_HEAD] = y[:, c0 + QK_NOPE:c0 + QK_NOPE + V_HEAD].astype(BF16)

    return pl.pallas_call(
        body, name="a_kv_proj", grid=(tp // tm,),
        in_specs=[pl.BlockSpec((tm, kin), lambda i: (i, 0)),
                  pl.BlockSpec((1, kin), lambda i: (0, 0)),
                  pl.BlockSpec((kin, n), lambda i: (0, 0)),
                  pl.BlockSpec((tm, LANES), lambda i: (i, 0)),
                  pl.BlockSpec((tm, LANES), lambda i: (i, 0)),
                  pl.BlockSpec((tm, LANES), lambda i: (i, 0))],
        out_specs=[pl.BlockSpec((tm, heads * HEAD_W), lambda i: (i, 0)),
                   pl.BlockSpec((tm, heads * V_HEAD), lambda i: (i, 0))],
        out_shape=[jax.ShapeDtypeStruct((tp, heads * HEAD_W), BF16),
                   jax.ShapeDtypeStruct((tp, heads * V_HEAD), BF16)],
        compiler_params=_params(("parallel",)),
    )(kv_lat, g, w_ukv, k_rope_raw, cos_t, sin_t)


def _attn_mask(row0, col0, rows, cols, pad):
    row = row0 + lax.broadcasted_iota(jnp.int32, (rows, cols), 0)
    col = col0 + lax.broadcasted_iota(jnp.int32, (rows, cols), 1)
    return (col <= row) & (col >= pad)


LOG2E = 1.4426950408889634
FLASH_SUBTILES = 1


def _flash_fwd(q, k, v, heads, pad, tq):
    tp = q.shape[0]
    c2 = (QK_NOPE + QK_ROPE) ** -0.5 * LOG2E

    def body(q_ref, k_ref, v_ref, o_ref, lse_ref):
        i = pl.program_id(1)

        def make_step(masked):
            def step(j, carry):
                m, l, acc = carry
                off = pl.multiple_of(j * tq, tq)
                s = _dot_nt(q_ref[...], k_ref[pl.ds(off, tq), :]) * c2
                if masked:
                    s = jnp.where(_attn_mask(i * tq, j * tq, tq, tq, pad), s, MASK_VALUE)
                m_new = jnp.maximum(m, jnp.max(s, axis=-1, keepdims=True))
                p = jnp.exp2(s - m_new)
                alpha = jnp.exp2(m - m_new)
                l = alpha * l + jnp.sum(p, axis=-1, keepdims=True)
                acc = alpha * acc + _dot(p.astype(BF16), v_ref[pl.ds(off, tq), :])
                return m_new, l, acc
            return step

        init = (jnp.full((tq, 1), MASK_VALUE, F32), jnp.zeros((tq, 1), F32), jnp.zeros((tq, V_HEAD), F32))
        carry = make_step(True)(0, init)
        carry = lax.fori_loop(1, i, make_step(False), carry)
        m, l, acc = lax.fori_loop(jnp.maximum(i, 1), i + 1, make_step(True), carry)
        o_ref[...] = acc / l
        lse_ref[...] = jnp.broadcast_to(m + jnp.log(l) * LOG2E, (tq, LANES))

    return pl.pallas_call(
        body, name="a_flash_fwd", grid=(heads, tp // tq),
        in_specs=[pl.BlockSpec((tq, HEAD_W), lambda h, i: (i, h)),
                  pl.BlockSpec((tp, HEAD_W), lambda h, i: (0, h)),
                  pl.BlockSpec((tp, V_HEAD), lambda h, i: (0, h))],
        out_specs=[pl.BlockSpec((tq, V_HEAD), lambda h, i: (i, h)),
                   pl.BlockSpec((None, tq, LANES), lambda h, i: (h, i, 0))],
        out_shape=[jax.ShapeDtypeStruct((tp, heads * V_HEAD), F32),
                   jax.ShapeDtypeStruct((heads, tp, LANES), F32)],
        compiler_params=_params(("parallel", "parallel")),
    )(q, k, v)


def _gated_out(name, a, gate, w, resid, tm):
    tp, wd = a.shape
    d = w.shape[1]

    def body(a_ref, gate_ref, w_ref, res_ref, o_ref):
        gv = gate_ref[...]
        y = (a_ref[...] * (gv * _sigmoid(gv))).astype(BF16)
        o_ref[...] = res_ref[...] + _dot(y, w_ref[...])

    return pl.pallas_call(
        body, name=name, grid=(tp // tm,),
        in_specs=[pl.BlockSpec((tm, wd), lambda i: (i, 0)),
                  pl.BlockSpec((tm, wd), lambda i: (i, 0)),
                  pl.BlockSpec((wd, d), lambda i: (0, 0)),
                  pl.BlockSpec((tm, d), lambda i: (i, 0))],
        out_specs=pl.BlockSpec((tm, d), lambda i: (i, 0)),
        out_shape=jax.ShapeDtypeStruct((tp, d), F32),
        compiler_params=_params(("parallel",)),
    )(a, gate, w, resid)


def _lru_decay(r, sp):
    log_a = -LRU_C * r * sp
    x2 = 2.0 * log_a
    e2 = jnp.exp(x2)
    series = x2 * (1.0 + x2 * (0.5 + x2 * (1.0 / 6.0 + x2 * (1.0 / 24.0 + x2 * (1.0 / 120.0)))))
    em1 = jnp.where(x2 > -0.1, series, e2 - 1.0)
    return jnp.exp(log_a), e2, jnp.sqrt(-em1)


def _softplus(x):
    return jnp.maximum(x, 0.0) + jnp.log1p(jnp.exp(-jnp.abs(x)))


def _rglru_fwd(u, conv_w, conv_b, w_rg, b_rg, w_ig, b_ig, lam, pad, tm):
    tp, w = u.shape
    groups, blk = w_rg.shape[0], w_rg.shape[1]

    def body(u_ref, cw_ref, cb_ref, wr_ref, br_ref, wi_ref, bi_ref, lam_ref,
             uc_ref, r_ref, ig_ref, hs_ref, uext, a_s, b_s, hc):
        i = pl.program_id(0)

        @pl.when(i == 0)
        def _():
            uext[0:SUBLANES, :] = jnp.zeros((SUBLANES, w), F32)
            hc[...] = jnp.zeros((SUBLANES, w), F32)

        uext[SUBLANES:SUBLANES + tm, :] = u_ref[...]
        cw = cw_ref[...]
        uc = cb_ref[...] + uext[pl.ds(SUBLANES - 3, tm), :] * cw[0:1, :]
        uc = uc + uext[pl.ds(SUBLANES - 2, tm), :] * cw[1:2, :]
        uc = uc + uext[pl.ds(SUBLANES - 1, tm), :] * cw[2:3, :]
        uc = uc + uext[pl.ds(SUBLANES, tm), :] * cw[3:4, :]
        uc_ref[...] = uc
        uext[0:SUBLANES, :] = uext[tm:tm + SUBLANES, :]

        sp = _softplus(-lam_ref[...])
        row = i * tm + lax.broadcasted_iota(jnp.int32, (tm, blk), 0)
        for g in range(groups):
            sl = slice(g * blk, (g + 1) * blk)
            ucg = uc_ref[:, sl]
            ucb = ucg.astype(BF16)
            r = _sigmoid(_dot(ucb, wr_ref[g]) + br_ref[:, sl])
            ig = _sigmoid(_dot(ucb, wi_ref[g]) + bi_ref[:, sl])
            r_ref[:, sl] = r
            ig_ref[:, sl] = ig
            a, _, mult = _lru_decay(r, sp[:, sl])
            mult = jnp.where(row == pad, 1.0, mult)
            a_s[:, sl] = a
            b_s[:, sl] = jnp.where(row < pad, 0.0, mult * (ig * ucg))

        row8 = lax.broadcasted_iota(jnp.int32, (SUBLANES, w), 0)

        def group(gi, h_in):
            off = pl.multiple_of(gi * SUBLANES, SUBLANES)
            av = a_s[pl.ds(off, SUBLANES), :]
            bv = b_s[pl.ds(off, SUBLANES), :]
            for k in (1, 2, 4):
                keep = row8 >= k
                bv = jnp.where(keep, av * pltpu.roll(bv, k, 0) + bv, bv)
                av = jnp.where(keep, av * pltpu.roll(av, k, 0), av)
            hv = av * h_in + bv
            hs_ref[pl.ds(off, SUBLANES), :] = hv
            return jnp.broadcast_to(hv[SUBLANES - 1:SUBLANES, :], (SUBLANES, w))

        hc[...] = lax.fori_loop(0, tm // SUBLANES, group, hc[...])

    row_spec = pl.BlockSpec((tm, w), lambda i: (i, 0))
    vec_spec = pl.BlockSpec((1, w), lambda i: (0, 0))
    mat_spec = pl.BlockSpec((groups, blk, blk), lambda i: (0, 0, 0))
    return pl.pallas_call(
        body, name="b_rglru_fwd", grid=(tp // tm,),
        in_specs=[row_spec, pl.BlockSpec((CONV_WIDTH, w), lambda i: (0, 0)), vec_spec,
                  mat_spec, vec_spec, mat_spec, vec_spec, vec_spec],
        out_specs=[row_spec, row_spec, row_spec, row_spec],
        out_shape=[jax.ShapeDtypeStruct((tp, w), F32)] * 4,
        scratch_shapes=[pltpu.VMEM((tm + SUBLANES, w), F32), pltpu.VMEM((tm, w), F32),
                        pltpu.VMEM((tm, w), F32), pltpu.VMEM((SUBLANES, w), F32)],
        compiler_params=_params(("arbitrary",)),
    )(u, conv_w, conv_b, w_rg, b_rg, w_ig, b_ig, lam)


def _final_loss(h, g, target, x0):
    tp, d = h.shape
    tm = Q_BLOCK
    assert x0 % tm == 0 and target.shape[0] == tp - x0
    lead = x0 // tm

    def body(h_ref, g_ref, t_ref, dh_ref, loss_ref, dg_ref):
        i = pl.program_id(0)

        @pl.when(i == 0)
        def _():
            loss_ref[...] = jnp.zeros_like(loss_ref)
            dg_ref[...] = jnp.zeros_like(dg_ref)

        xv = h_ref[...]
        gv = g_ref[...]
        r = _rms(xv)
        xh = xv * r
        err = jnp.where(i >= lead, xh * gv - t_ref[...], 0.0)
        loss_ref[...] += 0.5 * jnp.sum(jnp.mean(err * err, axis=-1, keepdims=True))
        dy = err / d
        dg_ref[...] += jnp.sum(dy * xh, axis=0, keepdims=True)
        dxh = dy * gv
        dh_ref[...] = r * (dxh - xh * jnp.mean(dxh * xh, axis=-1, keepdims=True))

    return pl.pallas_call(
        body, name="final_loss", grid=(tp // tm,),
        in_specs=[pl.BlockSpec((tm, d), lambda i: (i, 0)),
                  pl.BlockSpec((1, d), lambda i: (0, 0)),
                  pl.BlockSpec((tm, d), lambda i: (jnp.maximum(i - lead, 0), 0))],
        out_specs=[pl.BlockSpec((tm, d), lambda i: (i, 0)),
                   pl.BlockSpec((SUBLANES, LANES), lambda i: (0, 0)),
                   pl.BlockSpec((1, d), lambda i: (0, 0))],
        out_shape=[jax.ShapeDtypeStruct((tp, d), F32),
                   jax.ShapeDtypeStruct((SUBLANES, LANES), F32),
                   jax.ShapeDtypeStruct((1, d), F32)],
        compiler_params=_params(("arbitrary",)),
    )(h, g, target)


def _gated_out_bwd(name, dout, a, gate, w, tm, delta_heads=0):
    tp, wd = a.shape
    d = w.shape[1]

    def body(do_ref, a_ref, gate_ref, w_ref, da_ref, dgate_ref, dw_ref, *delta_ref):
        @pl.when(pl.program_id(0) == 0)
        def _():
            dw_ref[...] = jnp.zeros_like(dw_ref)

        gv = gate_ref[...]
        av = a_ref[...]
        sg = _sigmoid(gv)
        silu = gv * sg
        dob = do_ref[...].astype(BF16)
        dy = _dot_nt(dob, w_ref[...])
        da = dy * silu
        da_ref[...] = da
        dgate_ref[...] = dy * av * (sg * (1.0 + gv * (1.0 - sg)))
        dw_ref[...] += _dot_tn((av * silu).astype(BF16), dob)
        for h in range(delta_heads):
            sl = slice(h * V_HEAD, (h + 1) * V_HEAD)
            delta_ref[0][h] = jnp.broadcast_to(jnp.sum(da[:, sl] * av[:, sl], axis=-1, keepdims=True), (tm, LANES))

    out_specs = [pl.BlockSpec((tm, wd), lambda i: (i, 0)),
                 pl.BlockSpec((tm, wd), lambda i: (i, 0)),
                 pl.BlockSpec((wd, d), lambda i: (0, 0))]
    out_shape = [jax.ShapeDtypeStruct((tp, wd), F32),
                 jax.ShapeDtypeStruct((tp, wd), F32),
                 jax.ShapeDtypeStruct((wd, d), F32)]
    if delta_heads:
        out_specs.append(pl.BlockSpec((delta_heads, tm, LANES), lambda i: (0, i, 0)))
        out_shape.append(jax.ShapeDtypeStruct((delta_heads, tp, LANES), F32))
    return pl.pallas_call(
        body, name=name, grid=(tp // tm,),
        in_specs=[pl.BlockSpec((tm, d), lambda i: (i, 0)),
                  pl.BlockSpec((tm, wd), lambda i: (i, 0)),
                  pl.BlockSpec((tm, wd), lambda i: (i, 0)),
                  pl.BlockSpec((wd, d), lambda i: (0, 0))],
        out_specs=out_specs, out_shape=out_shape,
        compiler_params=_params(("arbitrary",)),
    )(dout, a, gate, w)


def _rglru_bwd(dhs, hs, r, ig, uc, u, conv_w, w_rg, w_ig, lam, pad, tm):
    tp, w = u.shape
    groups, blk = w_rg.shape[0], w_rg.shape[1]
    nt = tp // tm
    per8 = tm // SUBLANES

    def body(dhs_ref, hs_ref, hprev_ref, r_ref, ig_ref, uc_ref, u_ref, uprev_ref, cw_ref, wr_ref, wi_ref, lam_ref,
             du_ref, dcw_ref, dcb_ref, dwr_ref, dbr_ref, dwi_ref, dbi_ref, dlam_ref,
             aext, c_s, g_s, hext, uext, ducext, gc):
        step = pl.program_id(0)
        ti = nt - 1 - step

        @pl.when(step == 0)
        def _():
            for ref in (dcw_ref, dcb_ref, dwr_ref, dbr_ref, dwi_ref, dbi_ref, dlam_ref):
                ref[...] = jnp.zeros_like(ref)
            aext[tm:tm + SUBLANES, :] = jnp.zeros((SUBLANES, w), F32)
            ducext[tm:tm + SUBLANES, :] = jnp.zeros((SUBLANES, w), F32)
            gc[...] = jnp.zeros((SUBLANES, w), F32)

        lam_v = lam_ref[...]
        sp = _softplus(-lam_v)
        row = ti * tm + lax.broadcasted_iota(jnp.int32, (tm, w), 0)

        rv = r_ref[...]
        a, e2, mult = _lru_decay(rv, sp)
        aext[0:tm, :] = a
        c_s[...] = aext[pl.ds(1, tm), :]
        row8 = lax.broadcasted_iota(jnp.int32, (SUBLANES, w), 0)

        def group(gi, g_in):
            off = pl.multiple_of((per8 - 1 - gi) * SUBLANES, SUBLANES)
            cv = c_s[pl.ds(off, SUBLANES), :]
            dv = dhs_ref[pl.ds(off, SUBLANES), :]
            for k in (1, 2, 4):
                keep = row8 < SUBLANES - k
                dv = jnp.where(keep, cv * pltpu.roll(dv, SUBLANES - k, 0) + dv, dv)
                cv = jnp.where(keep, cv * pltpu.roll(cv, SUBLANES - k, 0), cv)
            gv = cv * g_in + dv
            g_s[pl.ds(off, SUBLANES), :] = gv
            return jnp.broadcast_to(gv[0:1, :], (SUBLANES, w))

        gc[...] = lax.fori_loop(0, per8, group, gc[...])
        aext[tm:tm + SUBLANES, :] = aext[0:SUBLANES, :]

        gsc = jnp.where(row < pad, 0.0, g_s[...])
        hext[0:SUBLANES, :] = hprev_ref[...]
        hext[SUBLANES:SUBLANES + tm, :] = hs_ref[...]
        hprev = jnp.where(row == 0, 0.0, hext[pl.ds(SUBLANES - 1, tm), :])
        igv = ig_ref[...]
        ucv = uc_ref[...]
        first = row == pad
        mult = jnp.where(first, 1.0, mult)
        dmult = gsc * (igv * ucv)
        dig = gsc * mult * ucv
        duc = gsc * mult * igv
        dlog_a = (gsc * hprev) * a + jnp.where(first, 0.0, dmult * (-e2 / mult))
        dlam_ref[...] += jnp.sum(dlog_a * rv, axis=0, keepdims=True) * (LRU_C * _sigmoid(-lam_v))
        dpre_r = dlog_a * (-LRU_C * sp) * (rv * (1.0 - rv))
        dpre_i = dig * (igv * (1.0 - igv))
        dbr_ref[...] += jnp.sum(dpre_r, axis=0, keepdims=True)
        dbi_ref[...] += jnp.sum(dpre_i, axis=0, keepdims=True)
        for g in range(groups):
            sl = slice(g * blk, (g + 1) * blk)
            ucb = ucv[:, sl].astype(BF16)
            drb = dpre_r[:, sl].astype(BF16)
            dib = dpre_i[:, sl].astype(BF16)
            dwr_ref[g] += _dot_tn(ucb, drb)
            dwi_ref[g] += _dot_tn(ucb, dib)
            ducext[0:tm, sl] = duc[:, sl] + _dot_nt(drb, wr_ref[g]) + _dot_nt(dib, wi_ref[g])

        ducv = ducext[0:tm, :]
        cw = cw_ref[...]
        dcb_ref[...] += jnp.sum(ducv, axis=0, keepdims=True)
        uext[0:SUBLANES, :] = jnp.where(ti == 0, 0.0, uprev_ref[...])
        uext[SUBLANES:SUBLANES + tm, :] = u_ref[...]
        for j in range(CONV_WIDTH):
            ush = uext[pl.ds(SUBLANES - (CONV_WIDTH - 1 - j), tm), :]
            dcw_ref[j:j + 1, :] += jnp.sum(ducv * ush, axis=0, keepdims=True)
        du = ducv * cw[3:4, :]
        for k in range(1, CONV_WIDTH):
            du = du + ducext[pl.ds(k, tm), :] * cw[3 - k:4 - k, :]
        du_ref[...] = du
        ducext[tm:tm + SUBLANES, :] = ducext[0:SUBLANES, :]

    rev = lambda s: (nt - 1 - s, 0)
    halo = lambda s: (jnp.maximum((nt - 1 - s) * per8 - 1, 0), 0)
    row_spec = pl.BlockSpec((tm, w), rev)
    halo_spec = pl.BlockSpec((SUBLANES, w), halo)
    vec_spec = pl.BlockSpec((1, w), lambda s: (0, 0))
    mat_spec = pl.BlockSpec((groups, blk, blk), lambda s: (0, 0, 0))
    cw_spec = pl.BlockSpec((CONV_WIDTH, w), lambda s: (0, 0))
    return pl.pallas_call(
        body, name="b_rglru_bwd", grid=(nt,),
        in_specs=[row_spec, row_spec, halo_spec, row_spec, row_spec, row_spec, row_spec, halo_spec,
                  cw_spec, mat_spec, mat_spec, vec_spec],
        out_specs=[row_spec, cw_spec, vec_spec, mat_spec, vec_spec, mat_spec, vec_spec, vec_spec],
        out_shape=[jax.ShapeDtypeStruct((tp, w), F32), jax.ShapeDtypeStruct((CONV_WIDTH, w), F32),
                   jax.ShapeDtypeStruct((1, w), F32), jax.ShapeDtypeStruct((groups, blk, blk), F32),
                   jax.ShapeDtypeStruct((1, w), F32), jax.ShapeDtypeStruct((groups, blk, blk), F32),
                   jax.ShapeDtypeStruct((1, w), F32), jax.ShapeDtypeStruct((1, w), F32)],
        scratch_shapes=[pltpu.VMEM((tm + SUBLANES, w), F32), pltpu.VMEM((tm, w), F32), pltpu.VMEM((tm, w), F32),
                        pltpu.VMEM((tm + SUBLANES, w), F32), pltpu.VMEM((tm + SUBLANES, w), F32),
                        pltpu.VMEM((tm + SUBLANES, w), F32), pltpu.VMEM((SUBLANES, w), F32)],
        compiler_params=_params(("arbitrary",)),
    )(dhs, hs, hs, r, ig, uc, u, u, conv_w, w_rg, w_ig, lam)


def _norm_matmul_bwd(name, x, g, w, dys, tm, resid=None, prologue=None, extra_out=None):
    tp, kin = x.shape
    n = w.shape[1]
    n_dy = len(dys)
    has_res = resid is not None
    has_extra = extra_out is not None

    def body(*refs):
        x_ref, g_ref, w_ref = refs[:3]
        dy_refs = refs[3:3 + n_dy]
        pos = 3 + n_dy
        res_ref = refs[pos] if has_res else None
        pos += int(has_res)
        dx_ref, dw_ref, dg_ref = refs[pos:pos + 3]
        pos += 3
        ex_ref = refs[pos] if has_extra else None
        pos += int(has_extra)
        dy_s = refs[pos]

        @pl.when(pl.program_id(0) == 0)
        def _():
            dw_ref[...] = jnp.zeros_like(dw_ref)
            dg_ref[...] = jnp.zeros_like(dg_ref)

        if prologue is None:
            c0 = 0
            for ref in dy_refs:
                dy_s[:, c0:c0 + ref.shape[1]] = ref[...].astype(BF16)
                c0 += ref.shape[1]
        else:
            prologue(dy_refs, dy_s, ex_ref)

        xv = x_ref[...]
        gv = g_ref[...]
        r = _rms(xv)
        xh = xv * r
        dyb = dy_s[...]
        dn = _dot_nt(dyb, w_ref[...])
        dw_ref[...] += _dot_tn((xh * gv).astype(BF16), dyb)
        dg_ref[...] += jnp.sum(dn * xh, axis=0, keepdims=True)
        dxh = dn * gv
        dx = r * (dxh - xh * jnp.mean(dxh * xh, axis=-1, keepdims=True))
        if has_res:
            dx = dx + res_ref[...]
        dx_ref[...] = dx

    row = lambda width: pl.BlockSpec((tm, width), lambda i: (i, 0))
    in_specs = [row(kin), pl.BlockSpec((1, kin), lambda i: (0, 0)), pl.BlockSpec((kin, n), lambda i: (0, 0))]
    in_specs += [row(a.shape[1]) for a in dys]
    args = [x, g, w, *dys]
    if has_res:
        in_specs.append(row(kin))
        args.append(resid)
    out_specs = [row(kin), pl.BlockSpec((kin, n), lambda i: (0, 0)), pl.BlockSpec((1, kin), lambda i: (0, 0))]
    out_shape = [jax.ShapeDtypeStruct((tp, kin), F32), jax.ShapeDtypeStruct((kin, n), F32),
                 jax.ShapeDtypeStruct((1, kin), F32)]
    if has_extra:
        out_specs.append(row(extra_out[0]))
        out_shape.append(jax.ShapeDtypeStruct((tp, extra_out[0]), extra_out[1]))
    return pl.pallas_call(
        body, name=name, grid=(tp // tm,),
        in_specs=in_specs, out_specs=out_specs, out_shape=out_shape,
        scratch_shapes=[pltpu.VMEM((tm, n), BF16)],
        compiler_params=_params(("arbitrary",)),
    )(*args)


def _flash_bwd(q, k, v, lse, delta, do, heads, pad, tq):
    tp = q.shape[0]
    nq = tp // tq
    scale = (QK_NOPE + QK_ROPE) ** -0.5
    c2 = scale * LOG2E
    ts = tq // FLASH_SUBTILES

    def body(q_ref, k_ref, v_ref, lse_ref, delta_ref, do_ref, dq_ref, dk_ref, dv_ref):
        j = pl.program_id(1)

        @pl.when(j == 0)
        def _():
            dq_ref[...] = jnp.zeros_like(dq_ref)

        kv = k_ref[...]
        vv = v_ref[...]

        def make_step(masked):
            def step(i, carry):
                dk, dv = carry
                for u in range(FLASH_SUBTILES):
                    off = pl.multiple_of(i * tq + u * ts, ts)
                    qv = q_ref[pl.ds(off, ts), :]
                    dob = do_ref[pl.ds(off, ts), :].astype(BF16)
                    p = jnp.exp2(_dot_nt(qv, kv) * c2 - lse_ref[pl.ds(off, ts), 0:1])
                    if masked:
                        p = jnp.where(_attn_mask(j * tq + u * ts, j * tq, ts, tq, pad), p, 0.0)
                    dv = dv + _dot_tn(p.astype(BF16), dob)
                    dp = _dot_nt(dob, vv)
                    ds = (p * (dp - delta_ref[pl.ds(off, ts), 0:1]) * scale).astype(BF16)
                    dk = dk + _dot_tn(ds, qv)
                    dq_ref[pl.ds(off, ts), :] += _dot(ds, kv)
                return dk, dv
            return step

        carry = make_step(True)(j, (jnp.zeros((tq, HEAD_W), F32), jnp.zeros((tq, V_HEAD), F32)))
        dk, dv = lax.fori_loop(j + 1, nq, make_step(False), carry)
        dk_ref[...] = dk
        dv_ref[...] = dv

    return pl.pallas_call(
        body, name="a_flash_bwd", grid=(heads, nq),
        in_specs=[pl.BlockSpec((tp, HEAD_W), lambda h, j: (0, h)),
                  pl.BlockSpec((tq, HEAD_W), lambda h, j: (j, h)),
                  pl.BlockSpec((tq, V_HEAD), lambda h, j: (j, h)),
                  pl.BlockSpec((None, tp, LANES), lambda h, j: (h, 0, 0)),
                  pl.BlockSpec((None, tp, LANES), lambda h, j: (h, 0, 0)),
                  pl.BlockSpec((tp, V_HEAD), lambda h, j: (0, h))],
        out_specs=[pl.BlockSpec((tp, HEAD_W), lambda h, j: (0, h)),
                   pl.BlockSpec((tq, HEAD_W), lambda h, j: (j, h)),
                   pl.BlockSpec((tq, V_HEAD), lambda h, j: (j, h))],
        out_shape=[jax.ShapeDtypeStruct((tp, heads * HEAD_W), F32),
                   jax.ShapeDtypeStruct((tp, heads * HEAD_W), F32),
                   jax.ShapeDtypeStruct((tp, heads * V_HEAD), F32)],
        compiler_params=_params(("parallel", "arbitrary")),
    )(q, k, v, lse, delta, do)


def _position():
    return lax.axis_index("x"), lax.axis_index("y"), lax.axis_index("c")


def _other_chips(x, y):
    return [(1 - x, y), (x, 1 - y), (1 - x, 1 - y)]


def _block(ref, shard_axis, n, k, split_axis=None, m=None, h=None):
    idx = []
    for a in range(len(ref.shape)):
        start = 0
        size = None
        if a == shard_axis:
            start, size = k * n, n
        if a == split_axis:
            size = (n if a == shard_axis else m) // 2
            start = start + h * size
        idx.append(slice(None) if size is None else pl.ds(start, size))
    return ref.at[tuple(idx)]


def _gather_weights(split, whole_small):
    ns, nw = len(split), len(whole_small)
    n = ns + nw
    arrs = [s[0] for s in split] + [s[0] for s in whole_small]
    axes = [s[1] for s in split] + [s[1] for s in whole_small]

    def body(*refs):
        ins, outs = refs[:n], refs[n:2 * n]
        ici_send, ici_recv, d2d_send, d2d_recv, sib_send, sib_recv = refs[2 * n:]
        x, y, c = _position()
        me = 2 * x + y
        others = _other_chips(x, y)
        sent, local = [], []

        def remote(src, dst, sems, idx, to):
            return pltpu.make_async_remote_copy(src_ref=src, dst_ref=dst, send_sem=sems[0].at[idx],
                                                recv_sem=sems[1].at[idx], device_id=to, device_id_type=MESH)

        for a in range(n):
            width = ins[a].shape[axes[a]]
            mine = remote(ins[a], _block(outs[a], axes[a], width, me), (sib_send, sib_recv), a, (x, y, 1 - c))
            mine.start()
            local.append(mine)
            for j, (px, py) in enumerate(others):
                if a < ns:
                    sx = split[a][2]
                    src = _block(ins[a], None, None, None, sx, ins[a].shape[sx], c)
                    dst = _block(outs[a], axes[a], width, me, sx, outs[a].shape[sx], c)
                else:
                    src, dst = ins[a], _block(outs[a], axes[a], width, me)
                cp = remote(src, dst, (ici_send, ici_recv), 3 * a + j, (px, py, c))
                cp.start()
                sent.append(cp)
        for a in range(ns):
            width = ins[a].shape[axes[a]]
            sx = split[a][2]
            for j, (px, py) in enumerate(others):
                theirs = _block(outs[a], axes[a], width, 2 * px + py, sx, outs[a].shape[sx], c)
                remote(theirs, theirs, (ici_send, ici_recv), 3 * a + j, (px, py, c)).wait_recv()
                fwd = remote(theirs, theirs, (d2d_send, d2d_recv), 3 * a + j, (x, y, 1 - c))
                fwd.start()
                sent.append(fwd)
        for a in range(ns, n):
            width = ins[a].shape[axes[a]]
            for j, (px, py) in enumerate(others):
                theirs = _block(outs[a], axes[a], width, 2 * px + py)
                remote(theirs, theirs, (ici_send, ici_recv), 3 * a + j, (px, py, c)).wait_recv()
        for a in range(ns):
            width = ins[a].shape[axes[a]]
            sx = split[a][2]
            for j, (px, py) in enumerate(others):
                from_sibling = _block(outs[a], axes[a], width, 2 * px + py, sx, outs[a].shape[sx], 1 - c)
                remote(from_sibling, from_sibling, (d2d_send, d2d_recv), 3 * a + j, (x, y, 1 - c)).wait_recv()
        for cp in sent:
            cp.wait_send()
        for cp in local:
            cp.wait()

    def whole_shape(a, axis):
        shape = list(a.shape)
        shape[axis] *= N_CHIPS
        return jax.ShapeDtypeStruct(tuple(shape), a.dtype)

    any_spec = pl.BlockSpec(memory_space=pl.ANY)
    return pl.pallas_call(
        body, name="gather_weights",
        in_specs=[any_spec] * n, out_specs=[any_spec] * n,
        out_shape=[whole_shape(a, ax) for a, ax in zip(arrs, axes)],
        scratch_shapes=[pltpu.SemaphoreType.DMA((3 * n,)), pltpu.SemaphoreType.DMA((3 * n,)),
                        pltpu.SemaphoreType.DMA((3 * ns,)), pltpu.SemaphoreType.DMA((3 * ns,)),
                        pltpu.SemaphoreType.DMA((n,)), pltpu.SemaphoreType.DMA((n,))],
        compiler_params=pltpu.CompilerParams(has_side_effects=True),
    )(*arrs)


def _swap_half_with_sibling(grads):
    nk, rows2, cols = grads.shape
    rows = rows2 // 2

    def body(g_ref, got_ref, send_sem, recv_sem):
        x, y, c = _position()
        cp = pltpu.make_async_remote_copy(
            src_ref=g_ref.at[:, pl.ds((1 - c) * rows, rows), :], dst_ref=got_ref, send_sem=send_sem, recv_sem=recv_sem,
            device_id=(x, y, 1 - c), device_id_type=MESH)
        cp.start()
        cp.wait()

    any_spec = pl.BlockSpec(memory_space=pl.ANY)
    return pl.pallas_call(
        body, name="grads_to_sibling",
        in_specs=[any_spec], out_specs=any_spec,
        out_shape=jax.ShapeDtypeStruct((nk, rows, cols), grads.dtype),
        scratch_shapes=[pltpu.SemaphoreType.DMA, pltpu.SemaphoreType.DMA],
        compiler_params=pltpu.CompilerParams(has_side_effects=True),
    )(grads)


def _add_halves(grads, got, c):
    nk, rows, cols = got.shape
    tr = 256 if rows % 256 == 0 else SUBLANES
    nblk = rows // tr

    def body(c_ref, g_ref, got_ref, o_ref):
        o_ref[...] = (g_ref[...] + got_ref[...]).astype(BF16)

    return pl.pallas_call(
        body, name="grads_chip_sum",
        grid_spec=pltpu.PrefetchScalarGridSpec(
            num_scalar_prefetch=1, grid=(nk, nblk),
            in_specs=[pl.BlockSpec((None, tr, cols), lambda k, i, c_ref: (k, c_ref[0] * nblk + i, 0)),
                      pl.BlockSpec((None, tr, cols), lambda k, i, c_ref: (k, i, 0))],
            out_specs=pl.BlockSpec((None, tr, cols), lambda k, i, c_ref: (k, i, 0))),
        out_shape=jax.ShapeDtypeStruct((nk, rows, cols), BF16),
        compiler_params=_params(("parallel", "parallel")),
    )(c, grads, got)


def _scatter_to_chips(part):
    nk, rows, cols = part.shape

    def body(p_ref, got_ref, send_sems, recv_sems):
        x, y, c = _position()
        copies = []
        for j, (px, py) in enumerate(_other_chips(x, y)):
            cp = pltpu.make_async_remote_copy(
                src_ref=p_ref.at[2 * px + py], dst_ref=got_ref.at[j],
                send_sem=send_sems.at[j], recv_sem=recv_sems.at[j],
                device_id=(px, py, c), device_id_type=MESH)
            cp.start()
            copies.append(cp)
        for cp in copies:
            cp.wait()

    any_spec = pl.BlockSpec(memory_space=pl.ANY)
    return pl.pallas_call(
        body, name="grads_to_chips",
        in_specs=[any_spec], out_specs=any_spec,
        out_shape=jax.ShapeDtypeStruct((3, rows, cols), part.dtype),
        scratch_shapes=[pltpu.SemaphoreType.DMA((3,)), pltpu.SemaphoreType.DMA((3,))],
        compiler_params=pltpu.CompilerParams(has_side_effects=True),
    )(part)


def _sum_chips(part, got, me, c):
    nk, rows, cols = part.shape
    tr = 256 if rows % 256 == 0 else SUBLANES

    def body(me_ref, c_ref, own_ref, *rest):
        got_refs, o_ref = rest[:nk], rest[nk]
        own = own_ref[...].astype(F32)
        acc = None
        for k in range(nk):
            term = jnp.where(me_ref[0] == k, own, got_refs[k][...].astype(F32))
            acc = term if acc is None else acc + term
        o_ref[...] = acc

    def got_map(k):
        def index(i, me_ref, c_ref):
            xor = jnp.bitwise_xor(me_ref[0], k)
            slot = jnp.where(xor == 1, 1, jnp.where(xor == 3, 2, 0))
            return (slot, i, 0)
        return index

    return pl.pallas_call(
        body, name="grads_sum",
        grid_spec=pltpu.PrefetchScalarGridSpec(
            num_scalar_prefetch=2, grid=(rows // tr,),
            in_specs=[pl.BlockSpec((None, tr, cols), lambda i, me_ref, c_ref: (me_ref[0], i, 0))]
            + [pl.BlockSpec((None, tr, cols), got_map(k)) for k in range(nk)],
            out_specs=pl.BlockSpec((None, tr, cols), lambda i, me_ref, c_ref: (c_ref[0], i, 0))),
        out_shape=jax.ShapeDtypeStruct((2, rows, cols), F32),
        compiler_params=_params(("parallel",)),
    )(me, c, part, *([got] * nk))


def _share_with_sibling(halves):
    def body(h_ref, full_ref, send_sem, recv_sem):
        x, y, c = _position()
        cp = pltpu.make_async_remote_copy(
            src_ref=full_ref.at[c], dst_ref=full_ref.at[c], send_sem=send_sem, recv_sem=recv_sem,
            device_id=(x, y, 1 - c), device_id_type=MESH)
        cp.start()
        cp.wait()

    any_spec = pl.BlockSpec(memory_space=pl.ANY)
    return pl.pallas_call(
        body, name="grads_share",
        in_specs=[any_spec], out_specs=any_spec,
        out_shape=jax.ShapeDtypeStruct(halves.shape, halves.dtype),
        input_output_aliases={0: 0},
        scratch_shapes=[pltpu.SemaphoreType.DMA, pltpu.SemaphoreType.DMA],
        compiler_params=pltpu.CompilerParams(has_side_effects=True),
    )(halves)


def _adamw(name, w, g, m, v):
    rows, cols = w.shape
    tr = 256 if rows % 256 == 0 else rows

    def body(w_ref, g_ref, m_ref, v_ref, d_ref, nm_ref, nv_ref):
        gv = g_ref[...]
        mn = ADAM_B1 * m_ref[...] + (1.0 - ADAM_B1) * gv
        vn = ADAM_B2 * v_ref[...] + (1.0 - ADAM_B2) * (gv * gv)
        m_hat = mn / (1.0 - ADAM_B1 ** ADAM_STEP)
        v_hat = vn / (1.0 - ADAM_B2 ** ADAM_STEP)
        d_ref[...] = -ADAM_LR * (m_hat / (jnp.sqrt(v_hat) + ADAM_EPS) + ADAM_WD * w_ref[...])
        nm_ref[...] = mn
        nv_ref[...] = vn

    spec = pl.BlockSpec((tr, cols), lambda i: (i, 0))
    return pl.pallas_call(
        body, name=name, grid=(rows // tr,),
        in_specs=[spec] * 4, out_specs=[spec] * 3,
        out_shape=[jax.ShapeDtypeStruct((rows, cols), F32)] * 3,
        compiler_params=_params(("parallel",)),
    )(w, g, m, v)


def _as2d(a):
    if a.ndim == 1:
        return a.reshape(1, -1)
    return a.reshape(-1, a.shape[-1])


def _unshard(gathered, axis):
    moved = jnp.moveaxis(gathered, 0, axis)
    shape = list(gathered.shape[1:])
    shape[axis] *= N_CHIPS
    return moved.reshape(shape)


def _rope_tables(tp, pad):
    pos = jnp.arange(tp, dtype=F32) - pad
    inv_freq = ROPE_BASE ** (-jnp.arange(0, QK_ROPE, 2, dtype=F32) / QK_ROPE)
    ang = pos[:, None] * inv_freq[None, :]
    cos, sin = jnp.cos(ang), jnp.sin(ang)
    zeros = jnp.zeros((tp, LANES - QK_ROPE), F32)
    return jnp.concatenate([cos, cos, zeros], axis=1), jnp.concatenate([-sin, sin, zeros], axis=1)


def _local_grads(x, target, wt, heads):
    seq, d = x.shape
    n_meta = wt['meta_tokens'].shape[0]
    t = seq + n_meta
    pad = (-t) % Q_BLOCK
    tp = t + pad
    x0 = pad + n_meta
    tm = _row_tile(tp)
    ql = wt['a_q_norm_g'].shape[1]
    kvl = wt['a_kv_norm_g'].shape[1]
    mla_w = heads * V_HEAD
    lru_w = wt['b_conv_w'].shape[1]

    h0 = jnp.concatenate([jnp.zeros((pad, d), F32), wt['meta_tokens'], x], axis=0)
    cos_t, sin_t = _rope_tables(tp, pad)

    w_in_a = wt['a_w_in']
    zcol = jnp.zeros((d, LANES - QK_ROPE), BF16)
    w_in_a = jnp.concatenate([w_in_a[:, :ql + kvl + QK_ROPE], zcol, w_in_a[:, ql + kvl + QK_ROPE:]], axis=1)
    c_kv, c_kr, c_gate = ql, ql + kvl, ql + kvl + LANES
    splits_a = [(0, c_kv), (c_kv, c_kr), (c_kr, c_gate), (c_gate, c_gate + mla_w)]
    w_uq = wt['a_w_uq'].reshape(ql, heads, QK_NOPE + QK_ROPE)
    w_uq = jnp.pad(w_uq, ((0, 0), (0, 0), (0, HEAD_W - QK_NOPE - QK_ROPE))).reshape(ql, heads * HEAD_W)
    w_ukv = wt['a_w_ukv']

    q_lat, kv_lat, kr_raw, gate_a = _norm_matmul("a_in_proj", h0, wt['a_norm_g'], w_in_a, splits_a, tm)
    q = _q_proj(q_lat, wt['a_q_norm_g'], w_uq, cos_t, sin_t, heads, tm)
    k, v = _kv_proj(kv_lat, wt['a_kv_norm_g'], w_ukv, kr_raw, cos_t, sin_t, heads, tm)
    attn, lse = _flash_fwd(q, k, v, heads, pad, tm)
    h1 = _gated_out("a_out_proj", attn, gate_a, wt['a_w_out'], h0, tm)

    u, gate_b = _norm_matmul("b_in_proj", h1, wt['b_norm_g'], wt['b_w_in'], [(0, lru_w), (lru_w, 2 * lru_w)], tm)
    uc, r, ig, hs = _rglru_fwd(u, wt['b_conv_w'], wt['b_conv_b'], wt['b_w_rg'], wt['b_b_rg'],
                               wt['b_w_ig'], wt['b_b_ig'], wt['b_lam'], pad, tm)
    h2 = _gated_out("b_out_proj", hs, gate_b, wt['b_w_out'], h1, tm)

    dh2, loss, d_final_g = _final_loss(h2, wt['final_norm_g'], target, x0)

    dhs, dgate_b, dw_out_b = _gated_out_bwd("b_out_proj_bwd", dh2, hs, gate_b, wt['b_w_out'], tm)
    du, dconv_w, dconv_b, dw_rg, db_rg, dw_ig, db_ig, dlam = _rglru_bwd(
        dhs, hs, r, ig, uc, u, wt['b_conv_w'], wt['b_w_rg'], wt['b_w_ig'], wt['b_lam'], pad, tm)
    dh1, dw_in_b, dg_b = _norm_matmul_bwd("b_in_proj_bwd", h1, wt['b_norm_g'], wt['b_w_in'], [du, dgate_b], tm, resid=dh2)

    dattn, dgate_a, dw_out_a, delta = _gated_out_bwd("a_out_proj_bwd", dh1, attn, gate_a, wt['a_w_out'], tm,
                                                     delta_heads=heads)
    dq, dk, dv = _flash_bwd(q, k, v, lse, delta, dattn, heads, pad, tm)

    def q_prologue(dy_refs, dy_s, ex_ref):
        (dq_ref,), cos_v, sin_v = dy_refs[:1], dy_refs[1][...], dy_refs[2][...]
        for h in range(heads):
            c0 = h * HEAD_W
            dy_s[:, c0:c0 + QK_NOPE] = dq_ref[:, c0:c0 + QK_NOPE].astype(BF16)
            dy_s[:, c0 + QK_NOPE:c0 + HEAD_W] = _unrope(dq_ref[:, c0 + QK_NOPE:c0 + HEAD_W], cos_v, sin_v).astype(BF16)

    dq_lat, dw_uq, dg_q = _norm_matmul_bwd("a_q_proj_bwd", q_lat, wt['a_q_norm_g'], w_uq, [dq, cos_t, sin_t], tm,
                                           prologue=q_prologue)

    def kv_prologue(dy_refs, dy_s, ex_ref):
        dk_ref, dv_ref = dy_refs[:2]
        cos_v, sin_v = dy_refs[2][...], dy_refs[3][...]
        dkr = jnp.zeros((dk_ref.shape[0], LANES), F32)
        for h in range(heads):
            c0 = h * (QK_NOPE + V_HEAD)
            dy_s[:, c0:c0 + QK_NOPE] = dk_ref[:, h * HEAD_W:h * HEAD_W + QK_NOPE].astype(BF16)
            dy_s[:, c0 + QK_NOPE:c0 + QK_NOPE + V_HEAD] = dv_ref[:, h * V_HEAD:(h + 1) * V_HEAD].astype(BF16)
            dkr = dkr + dk_ref[:, h * HEAD_W + QK_NOPE:(h + 1) * HEAD_W]
        ex_ref[...] = _unrope(dkr, cos_v, sin_v)

    dkv_lat, dw_ukv, dg_kv, dkr_raw = _norm_matmul_bwd(
        "a_kv_proj_bwd", kv_lat, wt['a_kv_norm_g'], w_ukv, [dk, dv, cos_t, sin_t], tm,
        prologue=kv_prologue, extra_out=(LANES, F32))

    dh0, dw_in_a, dg_a = _norm_matmul_bwd("a_in_proj_bwd", h0, wt['a_norm_g'], w_in_a,
                                          [dq_lat, dkv_lat, dkr_raw, dgate_a], tm, resid=dh1)

    dw_in_a = jnp.concatenate([dw_in_a[:, :c_kr + QK_ROPE], dw_in_a[:, c_gate:]], axis=1)
    dw_uq = dw_uq.reshape(ql, heads, HEAD_W)[:, :, :QK_NOPE + QK_ROPE].reshape(ql, heads * (QK_NOPE + QK_ROPE))
    grads = {
        'meta_tokens': dh0[pad:x0], 'a_norm_g': dg_a, 'a_w_in': dw_in_a, 'a_q_norm_g': dg_q, 'a_kv_norm_g': dg_kv,
        'a_w_uq': dw_uq, 'a_w_ukv': dw_ukv, 'a_w_out': dw_out_a, 'b_norm_g': dg_b, 'b_w_in': dw_in_b,
        'b_conv_w': dconv_w, 'b_conv_b': dconv_b, 'b_w_rg': dw_rg, 'b_b_rg': db_rg, 'b_w_ig': dw_ig,
        'b_b_ig': db_ig, 'b_lam': dlam, 'b_w_out': dw_out_b, 'final_norm_g': d_final_g,
    }
    return loss, dh0[x0:], grads


def _chip_major(whole, local_shape, axis):
    if axis is None:
        return jnp.broadcast_to(whole.reshape(1, -1), (N_CHIPS, whole.size))
    shape = list(local_shape)
    g = whole.reshape(shape[:axis] + [N_CHIPS, shape[axis]] + shape[axis + 1:])
    return jnp.moveaxis(g, axis, 0).reshape(N_CHIPS, -1)


def kernel(x, meta_tokens, a_norm_g, a_w_in, a_q_norm_g, a_kv_norm_g, a_w_uq, a_w_ukv, a_w_out, b_norm_g, b_w_in, b_conv_w, b_conv_b, b_w_rg, b_b_rg, b_w_ig, b_b_ig, b_lam, b_w_out, final_norm_g, loss_target, m_meta_tokens, m_a_norm_g, m_a_w_in, m_a_q_norm_g, m_a_kv_norm_g, m_a_w_uq, m_a_w_ukv, m_a_w_out, m_b_norm_g, m_b_w_in, m_b_conv_w, m_b_conv_b, m_b_w_rg, m_b_b_rg, m_b_w_ig, m_b_b_ig, m_b_lam, m_b_w_out, m_final_norm_g, v_meta_tokens, v_a_norm_g, v_a_w_in, v_a_q_norm_g, v_a_kv_norm_g, v_a_w_uq, v_a_w_ukv, v_a_w_out, v_b_norm_g, v_b_w_in, v_b_conv_w, v_b_conv_b, v_b_w_rg, v_b_b_rg, v_b_w_ig, v_b_b_ig, v_b_lam, v_b_w_out, v_final_norm_g):
    local_w = dict(zip(WEIGHTS, (meta_tokens, a_norm_g, a_w_in, a_q_norm_g, a_kv_norm_g, a_w_uq, a_w_ukv, a_w_out,
                                 b_norm_g, b_w_in, b_conv_w, b_conv_b, b_w_rg, b_b_rg, b_w_ig, b_b_ig, b_lam,
                                 b_w_out, final_norm_g)))
    local_m = dict(zip(WEIGHTS, (m_meta_tokens, m_a_norm_g, m_a_w_in, m_a_q_norm_g, m_a_kv_norm_g, m_a_w_uq,
                                 m_a_w_ukv, m_a_w_out, m_b_norm_g, m_b_w_in, m_b_conv_w, m_b_conv_b, m_b_w_rg,
                                 m_b_b_rg, m_b_w_ig, m_b_b_ig, m_b_lam, m_b_w_out, m_final_norm_g)))
    local_v = dict(zip(WEIGHTS, (v_meta_tokens, v_a_norm_g, v_a_w_in, v_a_q_norm_g, v_a_kv_norm_g, v_a_w_uq,
                                 v_a_w_ukv, v_a_w_out, v_b_norm_g, v_b_w_in, v_b_conv_w, v_b_conv_b, v_b_w_rg,
                                 v_b_b_rg, v_b_w_ig, v_b_b_ig, v_b_lam, v_b_w_out, v_final_norm_g)))
    matrices = ('a_w_in', 'a_w_uq', 'a_w_ukv', 'a_w_out', 'b_w_in', 'b_w_rg', 'b_w_ig', 'b_w_out')
    heads = a_w_uq.shape[-1] * N_CHIPS // (QK_NOPE + QK_ROPE)

    split, small = [], []
    for n in WEIGHTS:
        if SHARD_AXIS[n] is None:
            continue
        if n == 'a_w_in':
            split.append((n, local_w[n].astype(BF16)[None], 0, 2))
        elif n in matrices:
            split.append((n, local_w[n].astype(BF16), SHARD_AXIS[n], 1))
        else:
            small.append((n, local_w[n], SHARD_AXIS[n]))
    gathered = _gather_weights([s[1:] for s in split], [s[1:] for s in small])
    whole = dict(zip([s[0] for s in split + small], gathered))
    whole['a_w_in'] = _unshard(whole['a_w_in'], SHARD_AXIS['a_w_in'])
    wt = {}
    for n in WEIGHTS:
        w = whole.get(n, local_w[n])
        wt[n] = w[0] if n in ('b_w_rg', 'b_w_ig', 'b_conv_w') else _as2d(w)

    loss, grad_x, grads = _local_grads(x[0], loss_target[0], wt, heads)

    pieces = [_chip_major(grads[n], local_w[n].shape, SHARD_AXIS[n]) for n in WEIGHTS]
    length = sum(p.shape[1] for p in pieces)
    unit = 2 * (256 if length >= 2 ** 20 else SUBLANES) * 1024
    padded = -(-length // unit) * unit
    flat = jnp.concatenate(pieces + [jnp.zeros((N_CHIPS, padded - length), F32)], axis=1)
    flat = flat.reshape(N_CHIPS, padded // 1024, 1024)
    c_idx = lax.axis_index("c").astype(jnp.int32).reshape(1)
    got = _swap_half_with_sibling(flat)
    chip_sum = _add_halves(flat, got, c_idx)
    from_chips = _scatter_to_chips(chip_sum)
    me_idx = (2 * lax.axis_index("x") + lax.axis_index("y")).astype(jnp.int32).reshape(1)
    halves = _sum_chips(chip_sum, from_chips, me_idx, c_idx)
    total = _share_with_sibling(halves).reshape(-1)

    out_g, out_d, out_m, out_v = [], [], [], []
    off = 0
    for n in WEIGHTS:
        shape = local_w[n].shape
        size = 1
        for s in shape:
            size *= s
        g = total[off:off + size].reshape(shape)
        off += size
        delta, new_m, new_v = _adamw("adamw_" + n, _as2d(local_w[n]), _as2d(g), _as2d(local_m[n]), _as2d(local_v[n]))
        out_g.append(g)
        out_d.append(delta.reshape(shape))
        out_m.append(new_m.reshape(shape))
        out_v.append(new_v.reshape(shape))

    loss = lax.psum(loss[0, 0], ("x", "y", "c"))
    return (loss, grad_x[None], *out_g, *out_d, *out_m, *out_v)
```

```python
import functools

import jax
import jax.numpy as jnp
from jax import lax
from jax.experimental import pallas as pl
from jax.experimental.pallas import tpu as pltpu

F32 = jnp.float32
BF16 = jnp.bfloat16
MESH = pl.DeviceIdType.MESH

RMS_EPS = 1e-6
QK_NOPE = 128
QK_ROPE = 64
V_HEAD = 128
HEAD_W = 256
ROPE_BASE = 10000.0
Q_BLOCK = 128
MASK_VALUE = -1e30
CONV_WIDTH = 4
LRU_C = 8.0
N_CHIPS = 4

ADAM_LR = 0.001
ADAM_B1 = 0.9
ADAM_B2 = 0.999
ADAM_EPS = 1e-08
ADAM_WD = 0.01
ADAM_STEP = 10

VMEM_LIMIT_V7X = 56 * 1024 * 1024
LANES = 128
SUBLANES = 8

WEIGHTS = ['meta_tokens', 'a_norm_g', 'a_w_in', 'a_q_norm_g', 'a_kv_norm_g', 'a_w_uq', 'a_w_ukv',
           'a_w_out', 'b_norm_g', 'b_w_in', 'b_conv_w', 'b_conv_b', 'b_w_rg', 'b_b_rg', 'b_w_ig',
           'b_b_ig', 'b_lam', 'b_w_out', 'final_norm_g']
SHARD_AXIS = {'meta_tokens': 1, 'a_norm_g': None, 'a_w_in': 2, 'a_q_norm_g': None, 'a_kv_norm_g': None,
              'a_w_uq': 2, 'a_w_ukv': 2, 'a_w_out': 1, 'b_norm_g': 1, 'b_w_in': 2, 'b_conv_w': 2,
              'b_conv_b': 1, 'b_w_rg': 2, 'b_b_rg': 1, 'b_w_ig': 2, 'b_b_ig': 1, 'b_lam': 1,
              'b_w_out': 1, 'final_norm_g': None}


def _params(sem=None):
    return pltpu.CompilerParams(dimension_semantics=sem, vmem_limit_bytes=VMEM_LIMIT_V7X)


def _row_tile(tp):
    return 384 if (tp % 384 == 0 and tp >= 1152) else 128


def _sigmoid(x):
    return 1.0 / (1.0 + jnp.exp(-x))


def _rms(x):
    return lax.rsqrt(jnp.mean(x * x, axis=-1, keepdims=True) + RMS_EPS)


def _swap32(x):
    lane = lax.broadcasted_iota(jnp.int32, x.shape, 1)
    return jnp.where(lane < 32, pltpu.roll(x, 96, 1), pltpu.roll(x, 32, 1))


def _rope(x, cos_t, sin_t):
    return x * cos_t + _swap32(x) * sin_t


def _unrope(d, cos_t, sin_t):
    lane = lax.broadcasted_iota(jnp.int32, d.shape, 1)
    return jnp.where(lane < QK_ROPE, d * cos_t + _swap32(d * sin_t), 0.0)


def _dot(a, b):
    return jnp.dot(a, b, preferred_element_type=F32)


def _dot_nt(a, b):
    return lax.dot_general(a, b, (((1,), (1,)), ((), ())), preferred_element_type=F32)


def _dot_tn(a, b):
    return lax.dot_general(a, b, (((0,), (0,)), ((), ())), preferred_element_type=F32)


def _norm_matmul(name, x, g, w, splits, tm):
    tp, kin = x.shape
    n = w.shape[1]

    def body(x_ref, g_ref, w_ref, *outs):
        xv = x_ref[...]
        nrm = ((xv * _rms(xv)) * g_ref[...]).astype(BF16)
        y = _dot(nrm, w_ref[...])
        for o_ref, (c0, c1) in zip(outs, splits):
            o_ref[...] = y[:, c0:c1]

    return pl.pallas_call(
        body, name=name, grid=(tp // tm,),
        in_specs=[pl.BlockSpec((tm, kin), lambda i: (i, 0)),
                  pl.BlockSpec((1, kin), lambda i: (0, 0)),
                  pl.BlockSpec((kin, n), lambda i: (0, 0))],
        out_specs=[pl.BlockSpec((tm, c1 - c0), lambda i: (i, 0)) for c0, c1 in splits],
        out_shape=[jax.ShapeDtypeStruct((tp, c1 - c0), F32) for c0, c1 in splits],
        compiler_params=_params(("parallel",)),
    )(x, g, w)


def _q_proj(q_lat, g, w_uq, cos_t, sin_t, heads, tm):
    tp, kin = q_lat.shape
    n = heads * HEAD_W

    def body(x_ref, g_ref, w_ref, cos_ref, sin_ref, q_ref):
        xv = x_ref[...]
        nrm = ((xv * _rms(xv)) * g_ref[...]).astype(BF16)
        y = _dot(nrm, w_ref[...])
        cos_v, sin_v = cos_ref[...], sin_ref[...]
        for h in range(heads):
            c0 = h * HEAD_W
            q_ref[:, c0:c0 + QK_NOPE] = y[:, c0:c0 + QK_NOPE].astype(BF16)
            q_ref[:, c0 + QK_NOPE:c0 + HEAD_W] = _rope(y[:, c0 + QK_NOPE:c0 + HEAD_W], cos_v, sin_v).astype(BF16)

    return pl.pallas_call(
        body, name="a_q_proj", grid=(tp // tm,),
        in_specs=[pl.BlockSpec((tm, kin), lambda i: (i, 0)),
                  pl.BlockSpec((1, kin), lambda i: (0, 0)),
                  pl.BlockSpec((kin, n), lambda i: (0, 0)),
                  pl.BlockSpec((tm, LANES), lambda i: (i, 0)),
                  pl.BlockSpec((tm, LANES), lambda i: (i, 0))],
        out_specs=pl.BlockSpec((tm, n), lambda i: (i, 0)),
        out_shape=jax.ShapeDtypeStruct((tp, n), BF16),
        compiler_params=_params(("parallel",)),
    )(q_lat, g, w_uq, cos_t, sin_t)


def _kv_proj(kv_lat, g, w_ukv, k_rope_raw, cos_t, sin_t, heads, tm):
    tp, kin = kv_lat.shape
    n = heads * (QK_NOPE + V_HEAD)

    def body(x_ref, g_ref, w_ref, kr_ref, cos_ref, sin_ref, k_ref, v_ref):
        xv = x_ref[...]
        nrm = ((xv * _rms(xv)) * g_ref[...]).astype(BF16)
        y = _dot(nrm, w_ref[...])
        kr = _rope(kr_ref[...], cos_ref[...], sin_ref[...]).astype(BF16)
        for h in range(heads):
            c0 = h * (QK_NOPE + V_HEAD)
            k_ref[:, h * HEAD_W:h * HEAD_W + QK_NOPE] = y[:, c0:c0 + QK_NOPE].astype(BF16)
            k_ref[:, h * HEAD_W + QK_NOPE:(h + 1) * HEAD_W] = kr
            v_ref[:, h * V_HEAD:(h + 1) * V_HEAD] = y[:, c0 + QK_NOPE:c0 + QK_NOPE + V_HEAD].astype(BF16)

    return pl.pallas_call(
        body, name="a_kv_proj", grid=(tp // tm,),
        in_specs=[pl.BlockSpec((tm, kin), lambda i: (i, 0)),
                  pl.BlockSpec((1, kin), lambda i: (0, 0)),
                  pl.BlockSpec((kin, n), lambda i: (0, 0)),
                  pl.BlockSpec((tm, LANES), lambda i: (i, 0)),
                  pl.BlockSpec((tm, LANES), lambda i: (i, 0)),
                  pl.BlockSpec((tm, LANES), lambda i: (i, 0))],
        out_specs=[pl.BlockSpec((tm, heads * HEAD_W), lambda i: (i, 0)),
                   pl.BlockSpec((tm, heads * V_HEAD), lambda i: (i, 0))],
        out_shape=[jax.ShapeDtypeStruct((tp, heads * HEAD_W), BF16),
                   jax.ShapeDtypeStruct((tp, heads * V_HEAD), BF16)],
        compiler_params=_params(("parallel",)),
    )(kv_lat, g, w_ukv, k_rope_raw, cos_t, sin_t)


def _attn_mask(row0, col0, rows, cols, pad):
    row = row0 + lax.broadcasted_iota(jnp.int32, (rows, cols), 0)
    col = col0 + lax.broadcasted_iota(jnp.int32, (rows, cols), 1)
    return (col <= row) & (col >= pad)


LOG2E = 1.4426950408889634
FLASH_BLOCKS_PER_TRIP = 2
FLASH_FWD_BLOCKS_PER_TRIP = (4, 2, 1)


def _flash_fwd(q, k, v, heads, pad, tq):
    tp = q.shape[0]
    c2 = (QK_NOPE + QK_ROPE) ** -0.5 * LOG2E

    def body(q_ref, k_ref, v_ref, o_ref, lse_ref):
        i = pl.program_id(1)

        def make_step(masked, blocks):
            keys = blocks * tq

            def step(j, carry):
                m, l, acc = carry
                off = pl.multiple_of(j * tq, tq)
                s = _dot_nt(q_ref[...], k_ref[pl.ds(off, keys), :]) * c2
                if masked:
                    s = jnp.where(_attn_mask(i * tq, j * tq, tq, keys, pad), s, MASK_VALUE)
                m_new = jnp.maximum(m, jnp.max(s, axis=-1, keepdims=True))
                p = jnp.exp2(s - m_new)
                alpha = jnp.exp2(m - m_new)
                l = alpha * l + jnp.sum(p, axis=-1, keepdims=True)
                acc = alpha * acc + _dot(p.astype(BF16), v_ref[pl.ds(off, keys), :])
                return m_new, l, acc
            return step

        init = (jnp.full((tq, 1), MASK_VALUE, F32), jnp.zeros((tq, 1), F32), jnp.zeros((tq, V_HEAD), F32))
        carry = make_step(True, 1)(0, init)
        first = 1
        for blocks in FLASH_FWD_BLOCKS_PER_TRIP:
            trips = jnp.maximum(i - first, 0) // blocks
            step_n = make_step(False, blocks)
            carry = lax.fori_loop(0, trips, lambda t, cr, f=first, b=blocks, s=step_n: s(f + b * t, cr), carry)
            first = first + blocks * trips
        m, l, acc = lax.fori_loop(jnp.maximum(i, 1), i + 1, make_step(True, 1), carry)
        o_ref[...] = acc / l
        lse_ref[...] = jnp.broadcast_to(m + jnp.log(l) * LOG2E, (tq, LANES))

    return pl.pallas_call(
        body, name="a_flash_fwd", grid=(heads, tp // tq),
        in_specs=[pl.BlockSpec((tq, HEAD_W), lambda h, i: (i, h)),
                  pl.BlockSpec((tp, HEAD_W), lambda h, i: (0, h)),
                  pl.BlockSpec((tp, V_HEAD), lambda h, i: (0, h))],
        out_specs=[pl.BlockSpec((tq, V_HEAD), lambda h, i: (i, h)),
                   pl.BlockSpec((None, tq, LANES), lambda h, i: (h, i, 0))],
        out_shape=[jax.ShapeDtypeStruct((tp, heads * V_HEAD), F32),
                   jax.ShapeDtypeStruct((heads, tp, LANES), F32)],
        compiler_params=_params(("parallel", "parallel")),
    )(q, k, v)


def _gated_out(name, a, gate, w, resid, tm):
    tp, wd = a.shape
    d = w.shape[1]

    def body(a_ref, gate_ref, w_ref, res_ref, o_ref):
        gv = gate_ref[...]
        y = (a_ref[...] * (gv * _sigmoid(gv))).astype(BF16)
        o_ref[...] = res_ref[...] + _dot(y, w_ref[...])

    return pl.pallas_call(
        body, name=name, grid=(tp // tm,),
        in_specs=[pl.BlockSpec((tm, wd), lambda i: (i, 0)),
                  pl.BlockSpec((tm, wd), lambda i: (i, 0)),
                  pl.BlockSpec((wd, d), lambda i: (0, 0)),
                  pl.BlockSpec((tm, d), lambda i: (i, 0))],
        out_specs=pl.BlockSpec((tm, d), lambda i: (i, 0)),
        out_shape=jax.ShapeDtypeStruct((tp, d), F32),
        compiler_params=_params(("parallel",)),
    )(a, gate, w, resid)


def _lru_decay(r, sp):
    log_a = -LRU_C * r * sp
    x2 = 2.0 * log_a
    e2 = jnp.exp(x2)
    series = x2 * (1.0 + x2 * (0.5 + x2 * (1.0 / 6.0 + x2 * (1.0 / 24.0 + x2 * (1.0 / 120.0)))))
    em1 = jnp.where(x2 > -0.1, series, e2 - 1.0)
    return jnp.exp(log_a), e2, jnp.sqrt(-em1)


def _softplus(x):
    return jnp.maximum(x, 0.0) + jnp.log1p(jnp.exp(-jnp.abs(x)))


def _rglru_fwd(u, conv_w, conv_b, w_rg, b_rg, w_ig, b_ig, lam, pad, tm):
    tp, w = u.shape
    groups, blk = w_rg.shape[0], w_rg.shape[1]

    def body(u_ref, cw_ref, cb_ref, wr_ref, br_ref, wi_ref, bi_ref, lam_ref,
             uc_ref, r_ref, ig_ref, hs_ref, uext, a_s, b_s, hc):
        i = pl.program_id(0)

        @pl.when(i == 0)
        def _():
            uext[0:SUBLANES, :] = jnp.zeros((SUBLANES, w), F32)
            hc[...] = jnp.zeros((SUBLANES, w), F32)

        uext[SUBLANES:SUBLANES + tm, :] = u_ref[...]
        cw = cw_ref[...]
        uc = cb_ref[...] + uext[pl.ds(SUBLANES - 3, tm), :] * cw[0:1, :]
        uc = uc + uext[pl.ds(SUBLANES - 2, tm), :] * cw[1:2, :]
        uc = uc + uext[pl.ds(SUBLANES - 1, tm), :] * cw[2:3, :]
        uc = uc + uext[pl.ds(SUBLANES, tm), :] * cw[3:4, :]
        uc_ref[...] = uc
        uext[0:SUBLANES, :] = uext[tm:tm + SUBLANES, :]

        sp = _softplus(-lam_ref[...])
        row = i * tm + lax.broadcasted_iota(jnp.int32, (tm, blk), 0)
        for g in range(groups):
            sl = slice(g * blk, (g + 1) * blk)
            ucg = uc_ref[:, sl]
            ucb = ucg.astype(BF16)
            r = _sigmoid(_dot(ucb, wr_ref[g]) + br_ref[:, sl])
            ig = _sigmoid(_dot(ucb, wi_ref[g]) + bi_ref[:, sl])
            r_ref[:, sl] = r
            ig_ref[:, sl] = ig
            a, _, mult = _lru_decay(r, sp[:, sl])
            mult = jnp.where(row == pad, 1.0, mult)
            a_s[:, sl] = a
            b_s[:, sl] = jnp.where(row < pad, 0.0, mult * (ig * ucg))

        row8 = lax.broadcasted_iota(jnp.int32, (SUBLANES, w), 0)

        def group(gi, h_in):
            off = pl.multiple_of(gi * SUBLANES, SUBLANES)
            av = a_s[pl.ds(off, SUBLANES), :]
            bv = b_s[pl.ds(off, SUBLANES), :]
            for k in (1, 2, 4):
                keep = row8 >= k
                bv = jnp.where(keep, av * pltpu.roll(bv, k, 0) + bv, bv)
                av = jnp.where(keep, av * pltpu.roll(av, k, 0), av)
            hv = av * h_in + bv
            hs_ref[pl.ds(off, SUBLANES), :] = hv
            return jnp.broadcast_to(hv[SUBLANES - 1:SUBLANES, :], (SUBLANES, w))

        hc[...] = lax.fori_loop(0, tm // SUBLANES, group, hc[...])

    row_spec = pl.BlockSpec((tm, w), lambda i: (i, 0))
    vec_spec = pl.BlockSpec((1, w), lambda i: (0, 0))
    mat_spec = pl.BlockSpec((groups, blk, blk), lambda i: (0, 0, 0))
    return pl.pallas_call(
        body, name="b_rglru_fwd", grid=(tp // tm,),
        in_specs=[row_spec, pl.BlockSpec((CONV_WIDTH, w), lambda i: (0, 0)), vec_spec,
                  mat_spec, vec_spec, mat_spec, vec_spec, vec_spec],
        out_specs=[row_spec, row_spec, row_spec, row_spec],
        out_shape=[jax.ShapeDtypeStruct((tp, w), F32)] * 4,
        scratch_shapes=[pltpu.VMEM((tm + SUBLANES, w), F32), pltpu.VMEM((tm, w), F32),
                        pltpu.VMEM((tm, w), F32), pltpu.VMEM((SUBLANES, w), F32)],
        compiler_params=_params(("arbitrary",)),
    )(u, conv_w, conv_b, w_rg, b_rg, w_ig, b_ig, lam)


def _final_loss(h, g, target, x0):
    tp, d = h.shape
    tm = Q_BLOCK
    assert x0 % tm == 0 and target.shape[0] == tp - x0
    lead = x0 // tm

    def body(h_ref, g_ref, t_ref, dh_ref, loss_ref, dg_ref):
        i = pl.program_id(0)

        @pl.when(i == 0)
        def _():
            loss_ref[...] = jnp.zeros_like(loss_ref)
            dg_ref[...] = jnp.zeros_like(dg_ref)

        xv = h_ref[...]
        gv = g_ref[...]
        r = _rms(xv)
        xh = xv * r
        err = jnp.where(i >= lead, xh * gv - t_ref[...], 0.0)
        loss_ref[...] += 0.5 * jnp.sum(jnp.mean(err * err, axis=-1, keepdims=True))
        dy = err / d
        dg_ref[...] += jnp.sum(dy * xh, axis=0, keepdims=True)
        dxh = dy * gv
        dh_ref[...] = r * (dxh - xh * jnp.mean(dxh * xh, axis=-1, keepdims=True))

    return pl.pallas_call(
        body, name="final_loss", grid=(tp // tm,),
        in_specs=[pl.BlockSpec((tm, d), lambda i: (i, 0)),
                  pl.BlockSpec((1, d), lambda i: (0, 0)),
                  pl.BlockSpec((tm, d), lambda i: (jnp.maximum(i - lead, 0), 0))],
        out_specs=[pl.BlockSpec((tm, d), lambda i: (i, 0)),
                   pl.BlockSpec((SUBLANES, LANES), lambda i: (0, 0)),
                   pl.BlockSpec((1, d), lambda i: (0, 0))],
        out_shape=[jax.ShapeDtypeStruct((tp, d), F32),
                   jax.ShapeDtypeStruct((SUBLANES, LANES), F32),
                   jax.ShapeDtypeStruct((1, d), F32)],
        compiler_params=_params(("arbitrary",)),
    )(h, g, target)


def _gated_out_bwd(name, dout, a, gate, w, tm, delta_heads=0):
    tp, wd = a.shape
    d = w.shape[1]

    def body(do_ref, a_ref, gate_ref, w_ref, da_ref, dgate_ref, dw_ref, *delta_ref):
        @pl.when(pl.program_id(0) == 0)
        def _():
            dw_ref[...] = jnp.zeros_like(dw_ref)

        gv = gate_ref[...]
        av = a_ref[...]
        sg = _sigmoid(gv)
        silu = gv * sg
        dob = do_ref[...].astype(BF16)
        dy = _dot_nt(dob, w_ref[...])
        da = dy * silu
        da_ref[...] = da
        dgate_ref[...] = dy * av * (sg * (1.0 + gv * (1.0 - sg)))
        dw_ref[...] += _dot_tn((av * silu).astype(BF16), dob)
        for h in range(delta_heads):
            sl = slice(h * V_HEAD, (h + 1) * V_HEAD)
            delta_ref[0][h] = jnp.broadcast_to(jnp.sum(da[:, sl] * av[:, sl], axis=-1, keepdims=True), (tm, LANES))

    out_specs = [pl.BlockSpec((tm, wd), lambda i: (i, 0)),
                 pl.BlockSpec((tm, wd), lambda i: (i, 0)),
                 pl.BlockSpec((wd, d), lambda i: (0, 0))]
    out_shape = [jax.ShapeDtypeStruct((tp, wd), F32),
                 jax.ShapeDtypeStruct((tp, wd), F32),
                 jax.ShapeDtypeStruct((wd, d), F32)]
    if delta_heads:
        out_specs.append(pl.BlockSpec((delta_heads, tm, LANES), lambda i: (0, i, 0)))
        out_shape.append(jax.ShapeDtypeStruct((delta_heads, tp, LANES), F32))
    return pl.pallas_call(
        body, name=name, grid=(tp // tm,),
        in_specs=[pl.BlockSpec((tm, d), lambda i: (i, 0)),
                  pl.BlockSpec((tm, wd), lambda i: (i, 0)),
                  pl.BlockSpec((tm, wd), lambda i: (i, 0)),
                  pl.BlockSpec((wd, d), lambda i: (0, 0))],
        out_specs=out_specs, out_shape=out_shape,
        compiler_params=_params(("arbitrary",)),
    )(dout, a, gate, w)


def _rglru_bwd(dhs, hs, r, ig, uc, u, conv_w, w_rg, w_ig, lam, pad, tm):
    tp, w = u.shape
    groups, blk = w_rg.shape[0], w_rg.shape[1]
    nt = tp // tm
    per8 = tm // SUBLANES

    def body(dhs_ref, hs_ref, hprev_ref, r_ref, ig_ref, uc_ref, u_ref, uprev_ref, cw_ref, wr_ref, wi_ref, lam_ref,
             du_ref, dcw_ref, dcb_ref, dwr_ref, dbr_ref, dwi_ref, dbi_ref, dlam_ref,
             aext, c_s, g_s, hext, uext, ducext, gc):
        step = pl.program_id(0)
        ti = nt - 1 - step

        @pl.when(step == 0)
        def _():
            for ref in (dcw_ref, dcb_ref, dwr_ref, dbr_ref, dwi_ref, dbi_ref, dlam_ref):
                ref[...] = jnp.zeros_like(ref)
            aext[tm:tm + SUBLANES, :] = jnp.zeros((SUBLANES, w), F32)
            ducext[tm:tm + SUBLANES, :] = jnp.zeros((SUBLANES, w), F32)
            gc[...] = jnp.zeros((SUBLANES, w), F32)

        lam_v = lam_ref[...]
        sp = _softplus(-lam_v)
        row = ti * tm + lax.broadcasted_iota(jnp.int32, (tm, w), 0)

        rv = r_ref[...]
        a, e2, mult = _lru_decay(rv, sp)
        aext[0:tm, :] = a
        c_s[...] = aext[pl.ds(1, tm), :]
        row8 = lax.broadcasted_iota(jnp.int32, (SUBLANES, w), 0)

        def group(gi, g_in):
            off = pl.multiple_of((per8 - 1 - gi) * SUBLANES, SUBLANES)
            cv = c_s[pl.ds(off, SUBLANES), :]
            dv = dhs_ref[pl.ds(off, SUBLANES), :]
            for k in (1, 2, 4):
                keep = row8 < SUBLANES - k
                dv = jnp.where(keep, cv * pltpu.roll(dv, SUBLANES - k, 0) + dv, dv)
                cv = jnp.where(keep, cv * pltpu.roll(cv, SUBLANES - k, 0), cv)
            gv = cv * g_in + dv
            g_s[pl.ds(off, SUBLANES), :] = gv
            return jnp.broadcast_to(gv[0:1, :], (SUBLANES, w))

        gc[...] = lax.fori_loop(0, per8, group, gc[...])
        aext[tm:tm + SUBLANES, :] = aext[0:SUBLANES, :]

        gsc = jnp.where(row < pad, 0.0, g_s[...])
        hext[0:SUBLANES, :] = hprev_ref[...]
        hext[SUBLANES:SUBLANES + tm, :] = hs_ref[...]
        hprev = jnp.where(row == 0, 0.0, hext[pl.ds(SUBLANES - 1, tm), :])
        igv = ig_ref[...]
        ucv = uc_ref[...]
        first = row == pad
        mult = jnp.where(first, 1.0, mult)
        dmult = gsc * (igv * ucv)
        dig = gsc * mult * ucv
        duc = gsc * mult * igv
        dlog_a = (gsc * hprev) * a + jnp.where(first, 0.0, dmult * (-e2 / mult))
        dlam_ref[...] += jnp.sum(dlog_a * rv, axis=0, keepdims=True) * (LRU_C * _sigmoid(-lam_v))
        dpre_r = dlog_a * (-LRU_C * sp) * (rv * (1.0 - rv))
        dpre_i = dig * (igv * (1.0 - igv))
        dbr_ref[...] += jnp.sum(dpre_r, axis=0, keepdims=True)
        dbi_ref[...] += jnp.sum(dpre_i, axis=0, keepdims=True)
        for g in range(groups):
            sl = slice(g * blk, (g + 1) * blk)
            ucb = ucv[:, sl].astype(BF16)
            drb = dpre_r[:, sl].astype(BF16)
            dib = dpre_i[:, sl].astype(BF16)
            dwr_ref[g] += _dot_tn(ucb, drb)
            dwi_ref[g] += _dot_tn(ucb, dib)
            ducext[0:tm, sl] = duc[:, sl] + _dot_nt(drb, wr_ref[g]) + _dot_nt(dib, wi_ref[g])

        ducv = ducext[0:tm, :]
        cw = cw_ref[...]
        dcb_ref[...] += jnp.sum(ducv, axis=0, keepdims=True)
        uext[0:SUBLANES, :] = jnp.where(ti == 0, 0.0, uprev_ref[...])
        uext[SUBLANES:SUBLANES + tm, :] = u_ref[...]
        for j in range(CONV_WIDTH):
            ush = uext[pl.ds(SUBLANES - (CONV_WIDTH - 1 - j), tm), :]
            dcw_ref[j:j + 1, :] += jnp.sum(ducv * ush, axis=0, keepdims=True)
        du = ducv * cw[3:4, :]
        for k in range(1, CONV_WIDTH):
            du = du + ducext[pl.ds(k, tm), :] * cw[3 - k:4 - k, :]
        du_ref[...] = du
        ducext[tm:tm + SUBLANES, :] = ducext[0:SUBLANES, :]

    rev = lambda s: (nt - 1 - s, 0)
    halo = lambda s: (jnp.maximum((nt - 1 - s) * per8 - 1, 0), 0)
    row_spec = pl.BlockSpec((tm, w), rev)
    halo_spec = pl.BlockSpec((SUBLANES, w), halo)
    vec_spec = pl.BlockSpec((1, w), lambda s: (0, 0))
    mat_spec = pl.BlockSpec((groups, blk, blk), lambda s: (0, 0, 0))
    cw_spec = pl.BlockSpec((CONV_WIDTH, w), lambda s: (0, 0))
    return pl.pallas_call(
        body, name="b_rglru_bwd", grid=(nt,),
        in_specs=[row_spec, row_spec, halo_spec, row_spec, row_spec, row_spec, row_spec, halo_spec,
                  cw_spec, mat_spec, mat_spec, vec_spec],
        out_specs=[row_spec, cw_spec, vec_spec, mat_spec, vec_spec, mat_spec, vec_spec, vec_spec],
        out_shape=[jax.ShapeDtypeStruct((tp, w), F32), jax.ShapeDtypeStruct((CONV_WIDTH, w), F32),
                   jax.ShapeDtypeStruct((1, w), F32), jax.ShapeDtypeStruct((groups, blk, blk), F32),
                   jax.ShapeDtypeStruct((1, w), F32), jax.ShapeDtypeStruct((groups, blk, blk), F32),
                   jax.ShapeDtypeStruct((1, w), F32), jax.ShapeDtypeStruct((1, w), F32)],
        scratch_shapes=[pltpu.VMEM((tm + SUBLANES, w), F32), pltpu.VMEM((tm, w), F32), pltpu.VMEM((tm, w), F32),
                        pltpu.VMEM((tm + SUBLANES, w), F32), pltpu.VMEM((tm + SUBLANES, w), F32),
                        pltpu.VMEM((tm + SUBLANES, w), F32), pltpu.VMEM((SUBLANES, w), F32)],
        compiler_params=_params(("arbitrary",)),
    )(dhs, hs, hs, r, ig, uc, u, u, conv_w, w_rg, w_ig, lam)


def _norm_matmul_bwd(name, x, g, w, dys, tm, resid=None, prologue=None, extra_out=None):
    tp, kin = x.shape
    n = w.shape[1]
    n_dy = len(dys)
    has_res = resid is not None
    has_extra = extra_out is not None

    def body(*refs):
        x_ref, g_ref, w_ref = refs[:3]
        dy_refs = refs[3:3 + n_dy]
        pos = 3 + n_dy
        res_ref = refs[pos] if has_res else None
        pos += int(has_res)
        dx_ref, dw_ref, dg_ref = refs[pos:pos + 3]
        pos += 3
        ex_ref = refs[pos] if has_extra else None
        pos += int(has_extra)
        dy_s = refs[pos]

        @pl.when(pl.program_id(0) == 0)
        def _():
            dw_ref[...] = jnp.zeros_like(dw_ref)
            dg_ref[...] = jnp.zeros_like(dg_ref)

        if prologue is None:
            c0 = 0
            for ref in dy_refs:
                dy_s[:, c0:c0 + ref.shape[1]] = ref[...].astype(BF16)
                c0 += ref.shape[1]
        else:
            prologue(dy_refs, dy_s, ex_ref)

        xv = x_ref[...]
        gv = g_ref[...]
        r = _rms(xv)
        xh = xv * r
        dyb = dy_s[...]
        dn = _dot_nt(dyb, w_ref[...])
        dw_ref[...] += _dot_tn((xh * gv).astype(BF16), dyb)
        dg_ref[...] += jnp.sum(dn * xh, axis=0, keepdims=True)
        dxh = dn * gv
        dx = r * (dxh - xh * jnp.mean(dxh * xh, axis=-1, keepdims=True))
        if has_res:
            dx = dx + res_ref[...]
        dx_ref[...] = dx

    row = lambda width: pl.BlockSpec((tm, width), lambda i: (i, 0))
    in_specs = [row(kin), pl.BlockSpec((1, kin), lambda i: (0, 0)), pl.BlockSpec((kin, n), lambda i: (0, 0))]
    in_specs += [row(a.shape[1]) for a in dys]
    args = [x, g, w, *dys]
    if has_res:
        in_specs.append(row(kin))
        args.append(resid)
    out_specs = [row(kin), pl.BlockSpec((kin, n), lambda i: (0, 0)), pl.BlockSpec((1, kin), lambda i: (0, 0))]
    out_shape = [jax.ShapeDtypeStruct((tp, kin), F32), jax.ShapeDtypeStruct((kin, n), F32),
                 jax.ShapeDtypeStruct((1, kin), F32)]
    if has_extra:
        out_specs.append(row(extra_out[0]))
        out_shape.append(jax.ShapeDtypeStruct((tp, extra_out[0]), extra_out[1]))
    return pl.pallas_call(
        body, name=name, grid=(tp // tm,),
        in_specs=in_specs, out_specs=out_specs, out_shape=out_shape,
        scratch_shapes=[pltpu.VMEM((tm, n), BF16)],
        compiler_params=_params(("arbitrary",)),
    )(*args)


def _flash_bwd(q, k, v, lse, delta, do, heads, pad, tq):
    tp = q.shape[0]
    nq = tp // tq
    scale = (QK_NOPE + QK_ROPE) ** -0.5
    c2 = scale * LOG2E

    def body(q_ref, k_ref, v_ref, lse_ref, delta_ref, do_ref, dq_ref, dk_ref, dv_ref):
        j = pl.program_id(1)

        @pl.when(j == 0)
        def _():
            dq_ref[...] = jnp.zeros_like(dq_ref)

        kv = k_ref[...]
        vv = v_ref[...]

        def make_step(masked, blocks):
            rows = blocks * tq

            def step(i, carry):
                dk, dv = carry
                off = pl.multiple_of(i * tq, tq)
                qv = q_ref[pl.ds(off, rows), :]
                dob = do_ref[pl.ds(off, rows), :].astype(BF16)
                p = jnp.exp2(_dot_nt(qv, kv) * c2 - lse_ref[pl.ds(off, rows), 0:1])
                if masked:
                    p = jnp.where(_attn_mask(j * tq, j * tq, rows, tq, pad), p, 0.0)
                dv = dv + _dot_tn(p.astype(BF16), dob)
                dp = _dot_nt(dob, vv)
                ds = (p * (dp - delta_ref[pl.ds(off, rows), 0:1]) * scale).astype(BF16)
                dk = dk + _dot_tn(ds, qv)
                dq_ref[pl.ds(off, rows), :] += _dot(ds, kv)
                return dk, dv
            return step

        carry = make_step(True, 1)(j, (jnp.zeros((tq, HEAD_W), F32), jnp.zeros((tq, V_HEAD), F32)))
        trips = (nq - 1 - j) // FLASH_BLOCKS_PER_TRIP
        step_n = make_step(False, FLASH_BLOCKS_PER_TRIP)
        carry = lax.fori_loop(0, trips, lambda t, cr: step_n(j + 1 + FLASH_BLOCKS_PER_TRIP * t, cr), carry)
        dk, dv = lax.fori_loop(j + 1 + FLASH_BLOCKS_PER_TRIP * trips, nq, make_step(False, 1), carry)
        dk_ref[...] = dk
        dv_ref[...] = dv

    return pl.pallas_call(
        body, name="a_flash_bwd", grid=(heads, nq),
        in_specs=[pl.BlockSpec((tp, HEAD_W), lambda h, j: (0, h)),
                  pl.BlockSpec((tq, HEAD_W), lambda h, j: (j, h)),
                  pl.BlockSpec((tq, V_HEAD), lambda h, j: (j, h)),
                  pl.BlockSpec((None, tp, LANES), lambda h, j: (h, 0, 0)),
                  pl.BlockSpec((None, tp, LANES), lambda h, j: (h, 0, 0)),
                  pl.BlockSpec((tp, V_HEAD), lambda h, j: (0, h))],
        out_specs=[pl.BlockSpec((tp, HEAD_W), lambda h, j: (0, h)),
                   pl.BlockSpec((tq, HEAD_W), lambda h, j: (j, h)),
                   pl.BlockSpec((tq, V_HEAD), lambda h, j: (j, h))],
        out_shape=[jax.ShapeDtypeStruct((tp, heads * HEAD_W), F32),
                   jax.ShapeDtypeStruct((tp, heads * HEAD_W), F32),
                   jax.ShapeDtypeStruct((tp, heads * V_HEAD), F32)],
        compiler_params=_params(("parallel", "arbitrary")),
    )(q, k, v, lse, delta, do)


def _position():
    return lax.axis_index("x"), lax.axis_index("y"), lax.axis_index("c")


def _other_chips(x, y):
    return [(1 - x, y), (x, 1 - y), (1 - x, 1 - y)]


def _block(ref, shard_axis, n, k, split_axis=None, m=None, h=None):
    idx = []
    for a in range(len(ref.shape)):
        start = 0
        size = None
        if a == shard_axis:
            start, size = k * n, n
        if a == split_axis:
            size = (n if a == shard_axis else m) // 2
            start = start + h * size
        idx.append(slice(None) if size is None else pl.ds(start, size))
    return ref.at[tuple(idx)]


def _gather_weights(split, whole_small):
    ns, nw = len(split), len(whole_small)
    n = ns + nw
    arrs = [s[0] for s in split] + [s[0] for s in whole_small]
    axes = [s[1] for s in split] + [s[1] for s in whole_small]

    def body(*refs):
        ins, outs = refs[:n], refs[n:2 * n]
        ici_send, ici_recv, d2d_send, d2d_recv, sib_send, sib_recv = refs[2 * n:]
        x, y, c = _position()
        me = 2 * x + y
        others = _other_chips(x, y)
        sent, local = [], []

        def remote(src, dst, sems, idx, to):
            return pltpu.make_async_remote_copy(src_ref=src, dst_ref=dst, send_sem=sems[0].at[idx],
                                                recv_sem=sems[1].at[idx], device_id=to, device_id_type=MESH)

        for a in range(n):
            width = ins[a].shape[axes[a]]
            mine = remote(ins[a], _block(outs[a], axes[a], width, me), (sib_send, sib_recv), a, (x, y, 1 - c))
            mine.start()
            local.append(mine)
            for j, (px, py) in enumerate(others):
                if a < ns:
                    sx = split[a][2]
                    src = _block(ins[a], None, None, None, sx, ins[a].shape[sx], c)
                    dst = _block(outs[a], axes[a], width, me, sx, outs[a].shape[sx], c)
                else:
                    src, dst = ins[a], _block(outs[a], axes[a], width, me)
                cp = remote(src, dst, (ici_send, ici_recv), 3 * a + j, (px, py, c))
                cp.start()
                sent.append(cp)
        for a in range(ns):
            width = ins[a].shape[axes[a]]
            sx = split[a][2]
            for j, (px, py) in enumerate(others):
                theirs = _block(outs[a], axes[a], width, 2 * px + py, sx, outs[a].shape[sx], c)
                remote(theirs, theirs, (ici_send, ici_recv), 3 * a + j, (px, py, c)).wait_recv()
                fwd = remote(theirs, theirs, (d2d_send, d2d_recv), 3 * a + j, (x, y, 1 - c))
                fwd.start()
                sent.append(fwd)
        for a in range(ns, n):
            width = ins[a].shape[axes[a]]
            for j, (px, py) in enumerate(others):
                theirs = _block(outs[a], axes[a], width, 2 * px + py)
                remote(theirs, theirs, (ici_send, ici_recv), 3 * a + j, (px, py, c)).wait_recv()
        for a in range(ns):
            width = ins[a].shape[axes[a]]
            sx = split[a][2]
            for j, (px, py) in enumerate(others):
                from_sibling = _block(outs[a], axes[a], width, 2 * px + py, sx, outs[a].shape[sx], 1 - c)
                remote(from_sibling, from_sibling, (d2d_send, d2d_recv), 3 * a + j, (x, y, 1 - c)).wait_recv()
        for cp in sent:
            cp.wait_send()
        for cp in local:
            cp.wait()

    def whole_shape(a, axis):
        shape = list(a.shape)
        shape[axis] *= N_CHIPS
        return jax.ShapeDtypeStruct(tuple(shape), a.dtype)

    any_spec = pl.BlockSpec(memory_space=pl.ANY)
    return pl.pallas_call(
        body, name="gather_weights",
        in_specs=[any_spec] * n, out_specs=[any_spec] * n,
        out_shape=[whole_shape(a, ax) for a, ax in zip(arrs, axes)],
        scratch_shapes=[pltpu.SemaphoreType.DMA((3 * n,)), pltpu.SemaphoreType.DMA((3 * n,)),
                        pltpu.SemaphoreType.DMA((3 * ns,)), pltpu.SemaphoreType.DMA((3 * ns,)),
                        pltpu.SemaphoreType.DMA((n,)), pltpu.SemaphoreType.DMA((n,))],
        compiler_params=pltpu.CompilerParams(has_side_effects=True),
    )(*arrs)


def _swap_half_with_sibling(grads):
    nk, rows2, cols = grads.shape
    rows = rows2 // 2

    def body(g_ref, got_ref, send_sem, recv_sem):
        x, y, c = _position()
        cp = pltpu.make_async_remote_copy(
            src_ref=g_ref.at[:, pl.ds((1 - c) * rows, rows), :], dst_ref=got_ref, send_sem=send_sem, recv_sem=recv_sem,
            device_id=(x, y, 1 - c), device_id_type=MESH)
        cp.start()
        cp.wait()

    any_spec = pl.BlockSpec(memory_space=pl.ANY)
    return pl.pallas_call(
        body, name="grads_to_sibling",
        in_specs=[any_spec], out_specs=any_spec,
        out_shape=jax.ShapeDtypeStruct((nk, rows, cols), grads.dtype),
        scratch_shapes=[pltpu.SemaphoreType.DMA, pltpu.SemaphoreType.DMA],
        compiler_params=pltpu.CompilerParams(has_side_effects=True),
    )(grads)


def _add_halves(grads, got, c):
    nk, rows, cols = got.shape
    tr = 256 if rows % 256 == 0 else SUBLANES
    nblk = rows // tr

    def body(c_ref, g_ref, got_ref, o_ref):
        o_ref[...] = (g_ref[...] + got_ref[...]).astype(BF16)

    return pl.pallas_call(
        body, name="grads_chip_sum",
        grid_spec=pltpu.PrefetchScalarGridSpec(
            num_scalar_prefetch=1, grid=(nk, nblk),
            in_specs=[pl.BlockSpec((None, tr, cols), lambda k, i, c_ref: (k, c_ref[0] * nblk + i, 0)),
                      pl.BlockSpec((None, tr, cols), lambda k, i, c_ref: (k, i, 0))],
            out_specs=pl.BlockSpec((None, tr, cols), lambda k, i, c_ref: (k, i, 0))),
        out_shape=jax.ShapeDtypeStruct((nk, rows, cols), BF16),
        compiler_params=_params(("parallel", "parallel")),
    )(c, grads, got)


def _scatter_to_chips(part):
    nk, rows, cols = part.shape

    def body(p_ref, got_ref, send_sems, recv_sems):
        x, y, c = _position()
        copies = []
        for j, (px, py) in enumerate(_other_chips(x, y)):
            cp = pltpu.make_async_remote_copy(
                src_ref=p_ref.at[2 * px + py], dst_ref=got_ref.at[j],
                send_sem=send_sems.at[j], recv_sem=recv_sems.at[j],
                device_id=(px, py, c), device_id_type=MESH)
            cp.start()
            copies.append(cp)
        for cp in copies:
            cp.wait()

    any_spec = pl.BlockSpec(memory_space=pl.ANY)
    return pl.pallas_call(
        body, name="grads_to_chips",
        in_specs=[any_spec], out_specs=any_spec,
        out_shape=jax.ShapeDtypeStruct((3, rows, cols), part.dtype),
        scratch_shapes=[pltpu.SemaphoreType.DMA((3,)), pltpu.SemaphoreType.DMA((3,))],
        compiler_params=pltpu.CompilerParams(has_side_effects=True),
    )(part)


def _sum_chips(part, got, me, c):
    nk, rows, cols = part.shape
    tr = 256 if rows % 256 == 0 else SUBLANES

    def body(me_ref, c_ref, own_ref, *rest):
        got_refs, o_ref = rest[:nk], rest[nk]
        own = own_ref[...].astype(F32)
        acc = None
        for k in range(nk):
            term = jnp.where(me_ref[0] == k, own, got_refs[k][...].astype(F32))
            acc = term if acc is None else acc + term
        o_ref[...] = acc

    def got_map(k):
        def index(i, me_ref, c_ref):
            xor = jnp.bitwise_xor(me_ref[0], k)
            slot = jnp.where(xor == 1, 1, jnp.where(xor == 3, 2, 0))
            return (slot, i, 0)
        return index

    return pl.pallas_call(
        body, name="grads_sum",
        grid_spec=pltpu.PrefetchScalarGridSpec(
            num_scalar_prefetch=2, grid=(rows // tr,),
            in_specs=[pl.BlockSpec((None, tr, cols), lambda i, me_ref, c_ref: (me_ref[0], i, 0))]
            + [pl.BlockSpec((None, tr, cols), got_map(k)) for k in range(nk)],
            out_specs=pl.BlockSpec((None, tr, cols), lambda i, me_ref, c_ref: (c_ref[0], i, 0))),
        out_shape=jax.ShapeDtypeStruct((2, rows, cols), F32),
        compiler_params=_params(("parallel",)),
    )(me, c, part, *([got] * nk))


def _share_with_sibling(halves):
    def body(h_ref, full_ref, send_sem, recv_sem):
        x, y, c = _position()
        cp = pltpu.make_async_remote_copy(
            src_ref=full_ref.at[c], dst_ref=full_ref.at[c], send_sem=send_sem, recv_sem=recv_sem,
            device_id=(x, y, 1 - c), device_id_type=MESH)
        cp.start()
        cp.wait()

    any_spec = pl.BlockSpec(memory_space=pl.ANY)
    return pl.pallas_call(
        body, name="grads_share",
        in_specs=[any_spec], out_specs=any_spec,
        out_shape=jax.ShapeDtypeStruct(halves.shape, halves.dtype),
        input_output_aliases={0: 0},
        scratch_shapes=[pltpu.SemaphoreType.DMA, pltpu.SemaphoreType.DMA],
        compiler_params=pltpu.CompilerParams(has_side_effects=True),
    )(halves)


def _adamw(name, w, g, m, v):
    rows, cols = w.shape
    tr = 256 if rows % 256 == 0 else rows

    def body(w_ref, g_ref, m_ref, v_ref, d_ref, nm_ref, nv_ref):
        gv = g_ref[...]
        mn = ADAM_B1 * m_ref[...] + (1.0 - ADAM_B1) * gv
        vn = ADAM_B2 * v_ref[...] + (1.0 - ADAM_B2) * (gv * gv)
        m_hat = mn / (1.0 - ADAM_B1 ** ADAM_STEP)
        v_hat = vn / (1.0 - ADAM_B2 ** ADAM_STEP)
        d_ref[...] = -ADAM_LR * (m_hat / (jnp.sqrt(v_hat) + ADAM_EPS) + ADAM_WD * w_ref[...])
        nm_ref[...] = mn
        nv_ref[...] = vn

    spec = pl.BlockSpec((tr, cols), lambda i: (i, 0))
    return pl.pallas_call(
        body, name=name, grid=(rows // tr,),
        in_specs=[spec] * 4, out_specs=[spec] * 3,
        out_shape=[jax.ShapeDtypeStruct((rows, cols), F32)] * 3,
        compiler_params=_params(("parallel",)),
    )(w, g, m, v)


def _as2d(a):
    if a.ndim == 1:
        return a.reshape(1, -1)
    return a.reshape(-1, a.shape[-1])


def _unshard(gathered, axis):
    moved = jnp.moveaxis(gathered, 0, axis)
    shape = list(gathered.shape[1:])
    shape[axis] *= N_CHIPS
    return moved.reshape(shape)


def _rope_tables(tp, pad):
    pos = jnp.arange(tp, dtype=F32) - pad
    inv_freq = ROPE_BASE ** (-jnp.arange(0, QK_ROPE, 2, dtype=F32) / QK_ROPE)
    ang = pos[:, None] * inv_freq[None, :]
    cos, sin = jnp.cos(ang), jnp.sin(ang)
    zeros = jnp.zeros((tp, LANES - QK_ROPE), F32)
    return jnp.concatenate([cos, cos, zeros], axis=1), jnp.concatenate([-sin, sin, zeros], axis=1)


def _local_grads(x, target, wt, heads):
    seq, d = x.shape
    n_meta = wt['meta_tokens'].shape[0]
    t = seq + n_meta
    pad = (-t) % Q_BLOCK
    tp = t + pad
    x0 = pad + n_meta
    tm = _row_tile(tp)
    ql = wt['a_q_norm_g'].shape[1]
    kvl = wt['a_kv_norm_g'].shape[1]
    mla_w = heads * V_HEAD
    lru_w = wt['b_conv_w'].shape[1]

    h0 = jnp.concatenate([jnp.zeros((pad, d), F32), wt['meta_tokens'], x], axis=0)
    cos_t, sin_t = _rope_tables(tp, pad)

    w_in_a = wt['a_w_in']
    zcol = jnp.zeros((d, LANES - QK_ROPE), BF16)
    w_in_a = jnp.concatenate([w_in_a[:, :ql + kvl + QK_ROPE], zcol, w_in_a[:, ql + kvl + QK_ROPE:]], axis=1)
    c_kv, c_kr, c_gate = ql, ql + kvl, ql + kvl + LANES
    splits_a = [(0, c_kv), (c_kv, c_kr), (c_kr, c_gate), (c_gate, c_gate + mla_w)]
    w_uq = wt['a_w_uq'].reshape(ql, heads, QK_NOPE + QK_ROPE)
    w_uq = jnp.pad(w_uq, ((0, 0), (0, 0), (0, HEAD_W - QK_NOPE - QK_ROPE))).reshape(ql, heads * HEAD_W)
    w_ukv = wt['a_w_ukv']

    q_lat, kv_lat, kr_raw, gate_a = _norm_matmul("a_in_proj", h0, wt['a_norm_g'], w_in_a, splits_a, tm)
    q = _q_proj(q_lat, wt['a_q_norm_g'], w_uq, cos_t, sin_t, heads, tm)
    k, v = _kv_proj(kv_lat, wt['a_kv_norm_g'], w_ukv, kr_raw, cos_t, sin_t, heads, tm)
    attn, lse = _flash_fwd(q, k, v, heads, pad, tm)
    h1 = _gated_out("a_out_proj", attn, gate_a, wt['a_w_out'], h0, tm)

    u, gate_b = _norm_matmul("b_in_proj", h1, wt['b_norm_g'], wt['b_w_in'], [(0, lru_w), (lru_w, 2 * lru_w)], tm)
    uc, r, ig, hs = _rglru_fwd(u, wt['b_conv_w'], wt['b_conv_b'], wt['b_w_rg'], wt['b_b_rg'],
                               wt['b_w_ig'], wt['b_b_ig'], wt['b_lam'], pad, tm)
    h2 = _gated_out("b_out_proj", hs, gate_b, wt['b_w_out'], h1, tm)

    dh2, loss, d_final_g = _final_loss(h2, wt['final_norm_g'], target, x0)

    dhs, dgate_b, dw_out_b = _gated_out_bwd("b_out_proj_bwd", dh2, hs, gate_b, wt['b_w_out'], tm)
    du, dconv_w, dconv_b, dw_rg, db_rg, dw_ig, db_ig, dlam = _rglru_bwd(
        dhs, hs, r, ig, uc, u, wt['b_conv_w'], wt['b_w_rg'], wt['b_w_ig'], wt['b_lam'], pad, tm)
    dh1, dw_in_b, dg_b = _norm_matmul_bwd("b_in_proj_bwd", h1, wt['b_norm_g'], wt['b_w_in'], [du, dgate_b], tm, resid=dh2)

    dattn, dgate_a, dw_out_a, delta = _gated_out_bwd("a_out_proj_bwd", dh1, attn, gate_a, wt['a_w_out'], tm,
                                                     delta_heads=heads)
    dq, dk, dv = _flash_bwd(q, k, v, lse, delta, dattn, heads, pad, tm)

    def q_prologue(dy_refs, dy_s, ex_ref):
        (dq_ref,), cos_v, sin_v = dy_refs[:1], dy_refs[1][...], dy_refs[2][...]
        for h in range(heads):
            c0 = h * HEAD_W
            dy_s[:, c0:c0 + QK_NOPE] = dq_ref[:, c0:c0 + QK_NOPE].astype(BF16)
            dy_s[:, c0 + QK_NOPE:c0 + HEAD_W] = _unrope(dq_ref[:, c0 + QK_NOPE:c0 + HEAD_W], cos_v, sin_v).astype(BF16)

    dq_lat, dw_uq, dg_q = _norm_matmul_bwd("a_q_proj_bwd", q_lat, wt['a_q_norm_g'], w_uq, [dq, cos_t, sin_t], tm,
                                           prologue=q_prologue)

    def kv_prologue(dy_refs, dy_s, ex_ref):
        dk_ref, dv_ref = dy_refs[:2]
        cos_v, sin_v = dy_refs[2][...], dy_refs[3][...]
        dkr = jnp.zeros((dk_ref.shape[0], LANES), F32)
        for h in range(heads):
            c0 = h * (QK_NOPE + V_HEAD)
            dy_s[:, c0:c0 + QK_NOPE] = dk_ref[:, h * HEAD_W:h * HEAD_W + QK_NOPE].astype(BF16)
            dy_s[:, c0 + QK_NOPE:c0 + QK_NOPE + V_HEAD] = dv_ref[:, h * V_HEAD:(h + 1) * V_HEAD].astype(BF16)
            dkr = dkr + dk_ref[:, h * HEAD_W + QK_NOPE:(h + 1) * HEAD_W]
        ex_ref[...] = _unrope(dkr, cos_v, sin_v)

    dkv_lat, dw_ukv, dg_kv, dkr_raw = _norm_matmul_bwd(
        "a_kv_proj_bwd", kv_lat, wt['a_kv_norm_g'], w_ukv, [dk, dv, cos_t, sin_t], tm,
        prologue=kv_prologue, extra_out=(LANES, F32))

    dh0, dw_in_a, dg_a = _norm_matmul_bwd("a_in_proj_bwd", h0, wt['a_norm_g'], w_in_a,
                                          [dq_lat, dkv_lat, dkr_raw, dgate_a], tm, resid=dh1)

    dw_in_a = jnp.concatenate([dw_in_a[:, :c_kr + QK_ROPE], dw_in_a[:, c_gate:]], axis=1)
    dw_uq = dw_uq.reshape(ql, heads, HEAD_W)[:, :, :QK_NOPE + QK_ROPE].reshape(ql, heads * (QK_NOPE + QK_ROPE))
    grads = {
        'meta_tokens': dh0[pad:x0], 'a_norm_g': dg_a, 'a_w_in': dw_in_a, 'a_q_norm_g': dg_q, 'a_kv_norm_g': dg_kv,
        'a_w_uq': dw_uq, 'a_w_ukv': dw_ukv, 'a_w_out': dw_out_a, 'b_norm_g': dg_b, 'b_w_in': dw_in_b,
        'b_conv_w': dconv_w, 'b_conv_b': dconv_b, 'b_w_rg': dw_rg, 'b_b_rg': db_rg, 'b_w_ig': dw_ig,
        'b_b_ig': db_ig, 'b_lam': dlam, 'b_w_out': dw_out_b, 'final_norm_g': d_final_g,
    }
    return loss, dh0[x0:], grads


def _chip_major(whole, local_shape, axis):
    if axis is None:
        return jnp.broadcast_to(whole.reshape(1, -1), (N_CHIPS, whole.size))
    shape = list(local_shape)
    g = whole.reshape(shape[:axis] + [N_CHIPS, shape[axis]] + shape[axis + 1:])
    return jnp.moveaxis(g, axis, 0).reshape(N_CHIPS, -1)


def kernel(x, meta_tokens, a_norm_g, a_w_in, a_q_norm_g, a_kv_norm_g, a_w_uq, a_w_ukv, a_w_out, b_norm_g, b_w_in, b_conv_w, b_conv_b, b_w_rg, b_b_rg, b_w_ig, b_b_ig, b_lam, b_w_out, final_norm_g, loss_target, m_meta_tokens, m_a_norm_g, m_a_w_in, m_a_q_norm_g, m_a_kv_norm_g, m_a_w_uq, m_a_w_ukv, m_a_w_out, m_b_norm_g, m_b_w_in, m_b_conv_w, m_b_conv_b, m_b_w_rg, m_b_b_rg, m_b_w_ig, m_b_b_ig, m_b_lam, m_b_w_out, m_final_norm_g, v_meta_tokens, v_a_norm_g, v_a_w_in, v_a_q_norm_g, v_a_kv_norm_g, v_a_w_uq, v_a_w_ukv, v_a_w_out, v_b_norm_g, v_b_w_in, v_b_conv_w, v_b_conv_b, v_b_w_rg, v_b_b_rg, v_b_w_ig, v_b_b_ig, v_b_lam, v_b_w_out, v_final_norm_g):
    local_w = dict(zip(WEIGHTS, (meta_tokens, a_norm_g, a_w_in, a_q_norm_g, a_kv_norm_g, a_w_uq, a_w_ukv, a_w_out,
                                 b_norm_g, b_w_in, b_conv_w, b_conv_b, b_w_rg, b_b_rg, b_w_ig, b_b_ig, b_lam,
                                 b_w_out, final_norm_g)))
    local_m = dict(zip(WEIGHTS, (m_meta_tokens, m_a_norm_g, m_a_w_in, m_a_q_norm_g, m_a_kv_norm_g, m_a_w_uq,
                                 m_a_w_ukv, m_a_w_out, m_b_norm_g, m_b_w_in, m_b_conv_w, m_b_conv_b, m_b_w_rg,
                                 m_b_b_rg, m_b_w_ig, m_b_b_ig, m_b_lam, m_b_w_out, m_final_norm_g)))
    local_v = dict(zip(WEIGHTS, (v_meta_tokens, v_a_norm_g, v_a_w_in, v_a_q_norm_g, v_a_kv_norm_g, v_a_w_uq,
                                 v_a_w_ukv, v_a_w_out, v_b_norm_g, v_b_w_in, v_b_conv_w, v_b_conv_b, v_b_w_rg,
                                 v_b_b_rg, v_b_w_ig, v_b_b_ig, v_b_lam, v_b_w_out, v_final_norm_g)))
    matrices = ('a_w_in', 'a_w_uq', 'a_w_ukv', 'a_w_out', 'b_w_in', 'b_w_rg', 'b_w_ig', 'b_w_out')
    heads = a_w_uq.shape[-1] * N_CHIPS // (QK_NOPE + QK_ROPE)

    split, small = [], []
    for n in WEIGHTS:
        if SHARD_AXIS[n] is None:
            continue
        if n == 'a_w_in':
            split.append((n, local_w[n].astype(BF16)[None], 0, 2))
        elif n in matrices:
            split.append((n, local_w[n].astype(BF16), SHARD_AXIS[n], 1))
        else:
            small.append((n, local_w[n], SHARD_AXIS[n]))
    gathered = _gather_weights([s[1:] for s in split], [s[1:] for s in small])
    whole = dict(zip([s[0] for s in split + small], gathered))
    whole['a_w_in'] = _unshard(whole['a_w_in'], SHARD_AXIS['a_w_in'])
    wt = {}
    for n in WEIGHTS:
        w = whole.get(n, local_w[n])
        wt[n] = w[0] if n in ('b_w_rg', 'b_w_ig', 'b_conv_w') else _as2d(w)

    loss, grad_x, grads = _local_grads(x[0], loss_target[0], wt, heads)

    pieces = [_chip_major(grads[n], local_w[n].shape, SHARD_AXIS[n]) for n in WEIGHTS]
    length = sum(p.shape[1] for p in pieces)
    unit = 2 * (256 if length >= 2 ** 20 else SUBLANES) * 1024
    padded = -(-length // unit) * unit
    flat = jnp.concatenate(pieces + [jnp.zeros((N_CHIPS, padded - length), F32)], axis=1)
    flat = flat.reshape(N_CHIPS, padded // 1024, 1024)
    c_idx = lax.axis_index("c").astype(jnp.int32).reshape(1)
    got = _swap_half_with_sibling(flat)
    chip_sum = _add_halves(flat, got, c_idx)
    from_chips = _scatter_to_chips(chip_sum)
    me_idx = (2 * lax.axis_index("x") + lax.axis_index("y")).astype(jnp.int32).reshape(1)
    halves = _sum_chips(chip_sum, from_chips, me_idx, c_idx)
    total = _share_with_sibling(halves).reshape(-1)

    out_g, out_d, out_m, out_v = [], [], [], []
    off = 0
    for n in WEIGHTS:
        shape = local_w[n].shape
        size = 1
        for s in shape:
            size *= s
        g = total[off:off + size].reshape(shape)
        off += size
        delta, new_m, new_v = _adamw("adamw_" + n, _as2d(local_w[n]), _as2d(g), _as2d(local_m[n]), _as2d(local_v[n]))
        out_g.append(g)
        out_d.append(delta.reshape(shape))
        out_m.append(new_m.reshape(shape))
        out_v.append(new_v.reshape(shape))

    loss = lax.psum(loss[0, 0], ("x", "y", "c"))
    return (loss, grad_x[None], *out_g, *out_d, *out_m, *out_v)
```

```python
import functools

import jax
import jax.numpy as jnp
from jax import lax
from jax.experimental import pallas as pl
from jax.experimental.pallas import tpu as pltpu

F32 = jnp.float32
BF16 = jnp.bfloat16
MESH = pl.DeviceIdType.MESH

RMS_EPS = 1e-6
QK_NOPE = 128
QK_ROPE = 64
V_HEAD = 128
HEAD_W = 256
ROPE_BASE = 10000.0
Q_BLOCK = 128
MASK_VALUE = -1e30
CONV_WIDTH = 4
LRU_C = 8.0
N_CHIPS = 4

ADAM_LR = 0.001
ADAM_B1 = 0.9
ADAM_B2 = 0.999
ADAM_EPS = 1e-08
ADAM_WD = 0.01
ADAM_STEP = 10

VMEM_LIMIT_V7X = 56 * 1024 * 1024
LANES = 128
SUBLANES = 8

WEIGHTS = ['meta_tokens', 'a_norm_g', 'a_w_in', 'a_q_norm_g', 'a_kv_norm_g', 'a_w_uq', 'a_w_ukv',
           'a_w_out', 'b_norm_g', 'b_w_in', 'b_conv_w', 'b_conv_b', 'b_w_rg', 'b_b_rg', 'b_w_ig',
           'b_b_ig', 'b_lam', 'b_w_out', 'final_norm_g']
SHARD_AXIS = {'meta_tokens': 1, 'a_norm_g': None, 'a_w_in': 2, 'a_q_norm_g': None, 'a_kv_norm_g': None,
              'a_w_uq': 2, 'a_w_ukv': 2, 'a_w_out': 1, 'b_norm_g': 1, 'b_w_in': 2, 'b_conv_w': 2,
              'b_conv_b': 1, 'b_w_rg': 2, 'b_b_rg': 1, 'b_w_ig': 2, 'b_b_ig': 1, 'b_lam': 1,
              'b_w_out': 1, 'final_norm_g': None}


def _params(sem=None):
    return pltpu.CompilerParams(dimension_semantics=sem, vmem_limit_bytes=VMEM_LIMIT_V7X)


def _row_tile(tp):
    return 384 if (tp % 384 == 0 and tp >= 1152) else 128


def _sigmoid(x):
    return 1.0 / (1.0 + jnp.exp(-x))


def _rms(x):
    return lax.rsqrt(jnp.mean(x * x, axis=-1, keepdims=True) + RMS_EPS)


def _swap32(x):
    lane = lax.broadcasted_iota(jnp.int32, x.shape, 1)
    return jnp.where(lane < 32, pltpu.roll(x, 96, 1), pltpu.roll(x, 32, 1))


def _rope(x, cos_t, sin_t):
    return x * cos_t + _swap32(x) * sin_t


def _unrope(d, cos_t, sin_t):
    lane = lax.broadcasted_iota(jnp.int32, d.shape, 1)
    return jnp.where(lane < QK_ROPE, d * cos_t + _swap32(d * sin_t), 0.0)


def _dot(a, b):
    return jnp.dot(a, b, preferred_element_type=F32)


def _dot_nt(a, b):
    return lax.dot_general(a, b, (((1,), (1,)), ((), ())), preferred_element_type=F32)


def _dot_tn(a, b):
    return lax.dot_general(a, b, (((0,), (0,)), ((), ())), preferred_element_type=F32)


def _norm_matmul(name, x, g, w, splits, tm):
    tp, kin = x.shape
    n = w.shape[1]

    def body(x_ref, g_ref, w_ref, *outs):
        xv = x_ref[...]
        nrm = ((xv * _rms(xv)) * g_ref[...]).astype(BF16)
        y = _dot(nrm, w_ref[...])
        for o_ref, (c0, c1) in zip(outs, splits):
            o_ref[...] = y[:, c0:c1]

    return pl.pallas_call(
        body, name=name, grid=(tp // tm,),
        in_specs=[pl.BlockSpec((tm, kin), lambda i: (i, 0)),
                  pl.BlockSpec((1, kin), lambda i: (0, 0)),
                  pl.BlockSpec((kin, n), lambda i: (0, 0))],
        out_specs=[pl.BlockSpec((tm, c1 - c0), lambda i: (i, 0)) for c0, c1 in splits],
        out_shape=[jax.ShapeDtypeStruct((tp, c1 - c0), F32) for c0, c1 in splits],
        compiler_params=_params(("parallel",)),
    )(x, g, w)


def _q_proj(q_lat, g, w_uq, cos_t, sin_t, heads, tm):
    tp, kin = q_lat.shape
    n = heads * HEAD_W

    def body(x_ref, g_ref, w_ref, cos_ref, sin_ref, q_ref):
        xv = x_ref[...]
        nrm = ((xv * _rms(xv)) * g_ref[...]).astype(BF16)
        y = _dot(nrm, w_ref[...])
        cos_v, sin_v = cos_ref[...], sin_ref[...]
        for h in range(heads):
            c0 = h * HEAD_W
            q_ref[:, c0:c0 + QK_NOPE] = y[:, c0:c0 + QK_NOPE].astype(BF16)
            q_ref[:, c0 + QK_NOPE:c0 + HEAD_W] = _rope(y[:, c0 + QK_NOPE:c0 + HEAD_W], cos_v, sin_v).astype(BF16)

    return pl.pallas_call(
        body, name="a_q_proj", grid=(tp // tm,),
        in_specs=[pl.BlockSpec((tm, kin), lambda i: (i, 0)),
                  pl.BlockSpec((1, kin), lambda i: (0, 0)),
                  pl.BlockSpec((kin, n), lambda i: (0, 0)),
                  pl.BlockSpec((tm, LANES), lambda i: (i, 0)),
                  pl.BlockSpec((tm, LANES), lambda i: (i, 0))],
        out_specs=pl.BlockSpec((tm, n), lambda i: (i, 0)),
        out_shape=jax.ShapeDtypeStruct((tp, n), BF16),
        compiler_params=_params(("parallel",)),
    )(q_lat, g, w_uq, cos_t, sin_t)


def _kv_proj(kv_lat, g, w_ukv, k_rope_raw, cos_t, sin_t, heads, tm):
    tp, kin = kv_lat.shape
    n = heads * (QK_NOPE + V_HEAD)

    def body(x_ref, g_ref, w_ref, kr_ref, cos_ref, sin_ref, k_ref, v_ref):
        xv = x_ref[...]
        nrm = ((xv * _rms(xv)) * g_ref[...]).astype(BF16)
        y = _dot(nrm, w_ref[...])
        kr = _rope(kr_ref[...], cos_ref[...], sin_ref[...]).astype(BF16)
        for h in range(heads):
            c0 = h * (QK_NOPE + V_HEAD)
            k_ref[:, h * HEAD_W:h * HEAD_W + QK_NOPE] = y[:, c0:c0 + QK_NOPE].astype(BF16)
            k_ref[:, h * HEAD_W + QK_NOPE:(h + 1) * HEAD_W] = kr
            v_ref[:, h * V_HEAD:(h + 1) * V_HEAD] = y[:, c0 + QK_NOPE:c0 + QK_NOPE + V_HEAD].astype(BF16)

    return pl.pallas_call(
        body, name="a_kv_proj", grid=(tp // tm,),
        in_specs=[pl.BlockSpec((tm, kin), lambda i: (i, 0)),
                  pl.BlockSpec((1, kin), lambda i: (0, 0)),
                  pl.BlockSpec((kin, n), lambda i: (0, 0)),
                  pl.BlockSpec((tm, LANES), lambda i: (i, 0)),
                  pl.BlockSpec((tm, LANES), lambda i: (i, 0)),
                  pl.BlockSpec((tm, LANES), lambda i: (i, 0))],
        out_specs=[pl.BlockSpec((tm, heads * HEAD_W), lambda i: (i, 0)),
                   pl.BlockSpec((tm, heads * V_HEAD), lambda i: (i, 0))],
        out_shape=[jax.ShapeDtypeStruct((tp, heads * HEAD_W), BF16),
                   jax.ShapeDtypeStruct((tp, heads * V_HEAD), BF16)],
        compiler_params=_params(("parallel",)),
    )(kv_lat, g, w_ukv, k_rope_raw, cos_t, sin_t)


def _attn_mask(row0, col0, rows, cols, pad):
    row = row0 + lax.broadcasted_iota(jnp.int32, (rows, cols), 0)
    col = col0 + lax.broadcasted_iota(jnp.int32, (rows, cols), 1)
    return (col <= row) & (col >= pad)


LOG2E = 1.4426950408889634
FLASH_BLOCKS_PER_TRIP = 2
FLASH_FWD_BLOCKS_PER_TRIP = (4, 2, 1)


def _flash_fwd(q, k, v, heads, pad, tq):
    tp = q.shape[0]
    c2 = (QK_NOPE + QK_ROPE) ** -0.5 * LOG2E

    def body(q_ref, k_ref, v_ref, o_ref, lse_ref):
        i = pl.program_id(1)

        def make_step(masked, blocks):
            keys = blocks * tq

            def step(j, carry):
                m, l, acc = carry
                off = pl.multiple_of(j * tq, tq)
                s = _dot_nt(q_ref[...], k_ref[pl.ds(off, keys), :]) * c2
                if masked:
                    s = jnp.where(_attn_mask(i * tq, j * tq, tq, keys, pad), s, MASK_VALUE)
                m_new = jnp.maximum(m, jnp.max(s, axis=-1, keepdims=True))
                p = jnp.exp2(s - m_new)
                alpha = jnp.exp2(m - m_new)
                l = alpha * l + jnp.sum(p, axis=-1, keepdims=True)
                acc = alpha * acc + _dot(p.astype(BF16), v_ref[pl.ds(off, keys), :])
                return m_new, l, acc
            return step

        init = (jnp.full((tq, 1), MASK_VALUE, F32), jnp.zeros((tq, 1), F32), jnp.zeros((tq, V_HEAD), F32))
        carry = make_step(True, 1)(0, init)
        first = 1
        for blocks in FLASH_FWD_BLOCKS_PER_TRIP:
            trips = jnp.maximum(i - first, 0) // blocks
            step_n = make_step(False, blocks)
            carry = lax.fori_loop(0, trips, lambda t, cr, f=first, b=blocks, s=step_n: s(f + b * t, cr), carry)
            first = first + blocks * trips
        m, l, acc = lax.fori_loop(jnp.maximum(i, 1), i + 1, make_step(True, 1), carry)
        o_ref[...] = acc / l
        lse_ref[...] = jnp.broadcast_to(m + jnp.log(l) * LOG2E, (tq, LANES))

    return pl.pallas_call(
        body, name="a_flash_fwd", grid=(heads, tp // tq),
        in_specs=[pl.BlockSpec((tq, HEAD_W), lambda h, i: (i, h)),
                  pl.BlockSpec((tp, HEAD_W), lambda h, i: (0, h)),
                  pl.BlockSpec((tp, V_HEAD), lambda h, i: (0, h))],
        out_specs=[pl.BlockSpec((tq, V_HEAD), lambda h, i: (i, h)),
                   pl.BlockSpec((None, tq, LANES), lambda h, i: (h, i, 0))],
        out_shape=[jax.ShapeDtypeStruct((tp, heads * V_HEAD), F32),
                   jax.ShapeDtypeStruct((heads, tp, LANES), F32)],
        compiler_params=_params(("parallel", "parallel")),
    )(q, k, v)


def _gated_out(name, a, gate, w, resid, tm):
    tp, wd = a.shape
    d = w.shape[1]

    def body(a_ref, gate_ref, w_ref, res_ref, o_ref):
        gv = gate_ref[...]
        y = (a_ref[...] * (gv * _sigmoid(gv))).astype(BF16)
        o_ref[...] = res_ref[...] + _dot(y, w_ref[...])

    return pl.pallas_call(
        body, name=name, grid=(tp // tm,),
        in_specs=[pl.BlockSpec((tm, wd), lambda i: (i, 0)),
                  pl.BlockSpec((tm, wd), lambda i: (i, 0)),
                  pl.BlockSpec((wd, d), lambda i: (0, 0)),
                  pl.BlockSpec((tm, d), lambda i: (i, 0))],
        out_specs=pl.BlockSpec((tm, d), lambda i: (i, 0)),
        out_shape=jax.ShapeDtypeStruct((tp, d), F32),
        compiler_params=_params(("parallel",)),
    )(a, gate, w, resid)


def _lru_decay(r, sp):
    log_a = -LRU_C * r * sp
    a = jnp.exp(log_a)
    e2 = a * a
    x2 = 2.0 * log_a
    series = x2 * (1.0 + x2 * (0.5 + x2 * (1.0 / 6.0)))
    em1 = jnp.where(x2 > -0.02, series, e2 - 1.0)
    return a, e2, jnp.sqrt(-em1)


def _softplus(x):
    return jnp.maximum(x, 0.0) + jnp.log1p(jnp.exp(-jnp.abs(x)))


def _rglru_fwd(u, conv_w, conv_b, w_rg, b_rg, w_ig, b_ig, lam, pad, tm):
    tp, w = u.shape
    groups, blk = w_rg.shape[0], w_rg.shape[1]

    def body(u_ref, cw_ref, cb_ref, wr_ref, br_ref, wi_ref, bi_ref, lam_ref,
             uc_ref, r_ref, ig_ref, hs_ref, uext, a_s, b_s, hc):
        i = pl.program_id(0)

        @pl.when(i == 0)
        def _():
            uext[0:SUBLANES, :] = jnp.zeros((SUBLANES, w), F32)
            hc[...] = jnp.zeros((SUBLANES, w), F32)

        uext[SUBLANES:SUBLANES + tm, :] = u_ref[...]
        cw = cw_ref[...]
        uc = cb_ref[...] + uext[pl.ds(SUBLANES - 3, tm), :] * cw[0:1, :]
        uc = uc + uext[pl.ds(SUBLANES - 2, tm), :] * cw[1:2, :]
        uc = uc + uext[pl.ds(SUBLANES - 1, tm), :] * cw[2:3, :]
        uc = uc + uext[pl.ds(SUBLANES, tm), :] * cw[3:4, :]
        uc_ref[...] = uc
        uext[0:SUBLANES, :] = uext[tm:tm + SUBLANES, :]

        sp = _softplus(-lam_ref[...])
        for g in range(groups):
            sl = slice(g * blk, (g + 1) * blk)
            ucg = uc_ref[:, sl]
            ucb = ucg.astype(BF16)
            r = _sigmoid(_dot(ucb, wr_ref[g]) + br_ref[:, sl])
            ig = _sigmoid(_dot(ucb, wi_ref[g]) + bi_ref[:, sl])
            r_ref[:, sl] = r
            ig_ref[:, sl] = ig
            a, _, mult = _lru_decay(r, sp[:, sl])
            a_s[:, sl] = a
            b_s[:, sl] = mult * (ig * ucg)

        @pl.when(i == 0)
        def _():
            row = lax.broadcasted_iota(jnp.int32, (Q_BLOCK, w), 0)
            start = ig_ref[0:Q_BLOCK, :] * uc_ref[0:Q_BLOCK, :]
            b_s[0:Q_BLOCK, :] = jnp.where(row < pad, 0.0, jnp.where(row == pad, start, b_s[0:Q_BLOCK, :]))

        row8 = lax.broadcasted_iota(jnp.int32, (SUBLANES, w), 0)

        def group(gi, h_in):
            off = pl.multiple_of(gi * SUBLANES, SUBLANES)
            av = a_s[pl.ds(off, SUBLANES), :]
            bv = b_s[pl.ds(off, SUBLANES), :]
            for k in (1, 2, 4):
                keep = row8 >= k
                bv = jnp.where(keep, av * pltpu.roll(bv, k, 0) + bv, bv)
                av = jnp.where(keep, av * pltpu.roll(av, k, 0), av)
            hv = av * h_in + bv
            hs_ref[pl.ds(off, SUBLANES), :] = hv
            return jnp.broadcast_to(hv[SUBLANES - 1:SUBLANES, :], (SUBLANES, w))

        hc[...] = lax.fori_loop(0, tm // SUBLANES, group, hc[...])

    row_spec = pl.BlockSpec((tm, w), lambda i: (i, 0))
    vec_spec = pl.BlockSpec((1, w), lambda i: (0, 0))
    mat_spec = pl.BlockSpec((groups, blk, blk), lambda i: (0, 0, 0))
    return pl.pallas_call(
        body, name="b_rglru_fwd", grid=(tp // tm,),
        in_specs=[row_spec, pl.BlockSpec((CONV_WIDTH, w), lambda i: (0, 0)), vec_spec,
                  mat_spec, vec_spec, mat_spec, vec_spec, vec_spec],
        out_specs=[row_spec, row_spec, row_spec, row_spec],
        out_shape=[jax.ShapeDtypeStruct((tp, w), F32)] * 4,
        scratch_shapes=[pltpu.VMEM((tm + SUBLANES, w), F32), pltpu.VMEM((tm, w), F32),
                        pltpu.VMEM((tm, w), F32), pltpu.VMEM((SUBLANES, w), F32)],
        compiler_params=_params(("arbitrary",)),
    )(u, conv_w, conv_b, w_rg, b_rg, w_ig, b_ig, lam)


def _final_loss(h, g, target, x0):
    tp, d = h.shape
    tm = Q_BLOCK
    assert x0 % tm == 0 and target.shape[0] == tp - x0
    lead = x0 // tm

    def body(h_ref, g_ref, t_ref, dh_ref, loss_ref, dg_ref):
        i = pl.program_id(0)

        @pl.when(i == 0)
        def _():
            loss_ref[...] = jnp.zeros_like(loss_ref)
            dg_ref[...] = jnp.zeros_like(dg_ref)

        xv = h_ref[...]
        gv = g_ref[...]
        r = _rms(xv)
        xh = xv * r
        err = jnp.where(i >= lead, xh * gv - t_ref[...], 0.0)
        loss_ref[...] += 0.5 * jnp.sum(jnp.mean(err * err, axis=-1, keepdims=True))
        dy = err / d
        dg_ref[...] += jnp.sum(dy * xh, axis=0, keepdims=True)
        dxh = dy * gv
        dh_ref[...] = r * (dxh - xh * jnp.mean(dxh * xh, axis=-1, keepdims=True))

    return pl.pallas_call(
        body, name="final_loss", grid=(tp // tm,),
        in_specs=[pl.BlockSpec((tm, d), lambda i: (i, 0)),
                  pl.BlockSpec((1, d), lambda i: (0, 0)),
                  pl.BlockSpec((tm, d), lambda i: (jnp.maximum(i - lead, 0), 0))],
        out_specs=[pl.BlockSpec((tm, d), lambda i: (i, 0)),
                   pl.BlockSpec((SUBLANES, LANES), lambda i: (0, 0)),
                   pl.BlockSpec((1, d), lambda i: (0, 0))],
        out_shape=[jax.ShapeDtypeStruct((tp, d), F32),
                   jax.ShapeDtypeStruct((SUBLANES, LANES), F32),
                   jax.ShapeDtypeStruct((1, d), F32)],
        compiler_params=_params(("arbitrary",)),
    )(h, g, target)


def _gated_out_bwd(name, dout, a, gate, w, tm, delta_heads=0):
    tp, wd = a.shape
    d = w.shape[1]

    def body(do_ref, a_ref, gate_ref, w_ref, da_ref, dgate_ref, dw_ref, *delta_ref):
        @pl.when(pl.program_id(0) == 0)
        def _():
            dw_ref[...] = jnp.zeros_like(dw_ref)

        gv = gate_ref[...]
        av = a_ref[...]
        sg = _sigmoid(gv)
        silu = gv * sg
        dob = do_ref[...].astype(BF16)
        dy = _dot_nt(dob, w_ref[...])
        da = dy * silu
        da_ref[...] = da
        dgate_ref[...] = dy * av * (sg * (1.0 + gv * (1.0 - sg)))
        dw_ref[...] += _dot_tn((av * silu).astype(BF16), dob)
        for h in range(delta_heads):
            sl = slice(h * V_HEAD, (h + 1) * V_HEAD)
            delta_ref[0][h] = jnp.broadcast_to(jnp.sum(da[:, sl] * av[:, sl], axis=-1, keepdims=True), (tm, LANES))

    out_specs = [pl.BlockSpec((tm, wd), lambda i: (i, 0)),
                 pl.BlockSpec((tm, wd), lambda i: (i, 0)),
                 pl.BlockSpec((wd, d), lambda i: (0, 0))]
    out_shape = [jax.ShapeDtypeStruct((tp, wd), F32),
                 jax.ShapeDtypeStruct((tp, wd), F32),
                 jax.ShapeDtypeStruct((wd, d), F32)]
    if delta_heads:
        out_specs.append(pl.BlockSpec((delta_heads, tm, LANES), lambda i: (0, i, 0)))
        out_shape.append(jax.ShapeDtypeStruct((delta_heads, tp, LANES), F32))
    return pl.pallas_call(
        body, name=name, grid=(tp // tm,),
        in_specs=[pl.BlockSpec((tm, d), lambda i: (i, 0)),
                  pl.BlockSpec((tm, wd), lambda i: (i, 0)),
                  pl.BlockSpec((tm, wd), lambda i: (i, 0)),
                  pl.BlockSpec((wd, d), lambda i: (0, 0))],
        out_specs=out_specs, out_shape=out_shape,
        compiler_params=_params(("arbitrary",)),
    )(dout, a, gate, w)


def _rglru_bwd(dhs, hs, r, ig, uc, u, conv_w, w_rg, w_ig, lam, pad, tm):
    tp, w = u.shape
    groups, blk = w_rg.shape[0], w_rg.shape[1]
    nt = tp // tm
    per8 = tm // SUBLANES

    def body(dhs_ref, hs_ref, hprev_ref, r_ref, ig_ref, uc_ref, u_ref, uprev_ref, cw_ref, wr_ref, wi_ref, lam_ref,
             du_ref, dcw_ref, dcb_ref, dwr_ref, dbr_ref, dwi_ref, dbi_ref, dlam_ref,
             aext, c_s, g_s, hext, uext, ducext, gc):
        step = pl.program_id(0)
        ti = nt - 1 - step

        @pl.when(step == 0)
        def _():
            for ref in (dcw_ref, dcb_ref, dwr_ref, dbr_ref, dwi_ref, dbi_ref, dlam_ref):
                ref[...] = jnp.zeros_like(ref)
            aext[tm:tm + SUBLANES, :] = jnp.zeros((SUBLANES, w), F32)
            ducext[tm:tm + SUBLANES, :] = jnp.zeros((SUBLANES, w), F32)
            gc[...] = jnp.zeros((SUBLANES, w), F32)

        lam_v = lam_ref[...]
        sp = _softplus(-lam_v)
        row = ti * tm + lax.broadcasted_iota(jnp.int32, (tm, w), 0)

        rv = r_ref[...]
        a, e2, mult = _lru_decay(rv, sp)
        aext[0:tm, :] = a
        c_s[...] = aext[pl.ds(1, tm), :]
        row8 = lax.broadcasted_iota(jnp.int32, (SUBLANES, w), 0)

        def group(gi, g_in):
            off = pl.multiple_of((per8 - 1 - gi) * SUBLANES, SUBLANES)
            cv = c_s[pl.ds(off, SUBLANES), :]
            dv = dhs_ref[pl.ds(off, SUBLANES), :]
            for k in (1, 2, 4):
                keep = row8 < SUBLANES - k
                dv = jnp.where(keep, cv * pltpu.roll(dv, SUBLANES - k, 0) + dv, dv)
                cv = jnp.where(keep, cv * pltpu.roll(cv, SUBLANES - k, 0), cv)
            gv = cv * g_in + dv
            g_s[pl.ds(off, SUBLANES), :] = gv
            return jnp.broadcast_to(gv[0:1, :], (SUBLANES, w))

        gc[...] = lax.fori_loop(0, per8, group, gc[...])
        aext[tm:tm + SUBLANES, :] = aext[0:SUBLANES, :]

        gsc = jnp.where(row < pad, 0.0, g_s[...])
        hext[0:SUBLANES, :] = hprev_ref[...]
        hext[SUBLANES:SUBLANES + tm, :] = hs_ref[...]
        hprev = jnp.where(row == 0, 0.0, hext[pl.ds(SUBLANES - 1, tm), :])
        igv = ig_ref[...]
        ucv = uc_ref[...]
        first = row == pad
        mult = jnp.where(first, 1.0, mult)
        dmult = gsc * (igv * ucv)
        dig = gsc * mult * ucv
        duc = gsc * mult * igv
        dlog_a = (gsc * hprev) * a + jnp.where(first, 0.0, dmult * (-e2 / mult))
        dlam_ref[...] += jnp.sum(dlog_a * rv, axis=0, keepdims=True) * (LRU_C * _sigmoid(-lam_v))
        dpre_r = dlog_a * (-LRU_C * sp) * (rv * (1.0 - rv))
        dpre_i = dig * (igv * (1.0 - igv))
        dbr_ref[...] += jnp.sum(dpre_r, axis=0, keepdims=True)
        dbi_ref[...] += jnp.sum(dpre_i, axis=0, keepdims=True)
        for g in range(groups):
            sl = slice(g * blk, (g + 1) * blk)
            ucb = ucv[:, sl].astype(BF16)
            drb = dpre_r[:, sl].astype(BF16)
            dib = dpre_i[:, sl].astype(BF16)
            dwr_ref[g] += _dot_tn(ucb, drb)
            dwi_ref[g] += _dot_tn(ucb, dib)
            ducext[0:tm, sl] = duc[:, sl] + _dot_nt(drb, wr_ref[g]) + _dot_nt(dib, wi_ref[g])

        ducv = ducext[0:tm, :]
        cw = cw_ref[...]
        dcb_ref[...] += jnp.sum(ducv, axis=0, keepdims=True)
        uext[0:SUBLANES, :] = jnp.where(ti == 0, 0.0, uprev_ref[...])
        uext[SUBLANES:SUBLANES + tm, :] = u_ref[...]
        for j in range(CONV_WIDTH):
            ush = uext[pl.ds(SUBLANES - (CONV_WIDTH - 1 - j), tm), :]
            dcw_ref[j:j + 1, :] += jnp.sum(ducv * ush, axis=0, keepdims=True)
        du = ducv * cw[3:4, :]
        for k in range(1, CONV_WIDTH):
            du = du + ducext[pl.ds(k, tm), :] * cw[3 - k:4 - k, :]
        du_ref[...] = du
        ducext[tm:tm + SUBLANES, :] = ducext[0:SUBLANES, :]

    rev = lambda s: (nt - 1 - s, 0)
    halo = lambda s: (jnp.maximum((nt - 1 - s) * per8 - 1, 0), 0)
    row_spec = pl.BlockSpec((tm, w), rev)
    halo_spec = pl.BlockSpec((SUBLANES, w), halo)
    vec_spec = pl.BlockSpec((1, w), lambda s: (0, 0))
    mat_spec = pl.BlockSpec((groups, blk, blk), lambda s: (0, 0, 0))
    cw_spec = pl.BlockSpec((CONV_WIDTH, w), lambda s: (0, 0))
    return pl.pallas_call(
        body, name="b_rglru_bwd", grid=(nt,),
        in_specs=[row_spec, row_spec, halo_spec, row_spec, row_spec, row_spec, row_spec, halo_spec,
                  cw_spec, mat_spec, mat_spec, vec_spec],
        out_specs=[row_spec, cw_spec, vec_spec, mat_spec, vec_spec, mat_spec, vec_spec, vec_spec],
        out_shape=[jax.ShapeDtypeStruct((tp, w), F32), jax.ShapeDtypeStruct((CONV_WIDTH, w), F32),
                   jax.ShapeDtypeStruct((1, w), F32), jax.ShapeDtypeStruct((groups, blk, blk), F32),
                   jax.ShapeDtypeStruct((1, w), F32), jax.ShapeDtypeStruct((groups, blk, blk), F32),
                   jax.ShapeDtypeStruct((1, w), F32), jax.ShapeDtypeStruct((1, w), F32)],
        scratch_shapes=[pltpu.VMEM((tm + SUBLANES, w), F32), pltpu.VMEM((tm, w), F32), pltpu.VMEM((tm, w), F32),
                        pltpu.VMEM((tm + SUBLANES, w), F32), pltpu.VMEM((tm + SUBLANES, w), F32),
                        pltpu.VMEM((tm + SUBLANES, w), F32), pltpu.VMEM((SUBLANES, w), F32)],
        compiler_params=_params(("arbitrary",)),
    )(dhs, hs, hs, r, ig, uc, u, u, conv_w, w_rg, w_ig, lam)


def _norm_matmul_bwd(name, x, g, w, dys, tm, resid=None, prologue=None, extra_out=None):
    tp, kin = x.shape
    n = w.shape[1]
    n_dy = len(dys)
    has_res = resid is not None
    has_extra = extra_out is not None

    def body(*refs):
        x_ref, g_ref, w_ref = refs[:3]
        dy_refs = refs[3:3 + n_dy]
        pos = 3 + n_dy
        res_ref = refs[pos] if has_res else None
        pos += int(has_res)
        dx_ref, dw_ref, dg_ref = refs[pos:pos + 3]
        pos += 3
        ex_ref = refs[pos] if has_extra else None
        pos += int(has_extra)
        dy_s = refs[pos]

        @pl.when(pl.program_id(0) == 0)
        def _():
            dw_ref[...] = jnp.zeros_like(dw_ref)
            dg_ref[...] = jnp.zeros_like(dg_ref)

        if prologue is None:
            c0 = 0
            for ref in dy_refs:
                dy_s[:, c0:c0 + ref.shape[1]] = ref[...].astype(BF16)
                c0 += ref.shape[1]
        else:
            prologue(dy_refs, dy_s, ex_ref)

        xv = x_ref[...]
        gv = g_ref[...]
        r = _rms(xv)
        xh = xv * r
        dyb = dy_s[...]
        dn = _dot_nt(dyb, w_ref[...])
        dw_ref[...] += _dot_tn((xh * gv).astype(BF16), dyb)
        dg_ref[...] += jnp.sum(dn * xh, axis=0, keepdims=True)
        dxh = dn * gv
        dx = r * (dxh - xh * jnp.mean(dxh * xh, axis=-1, keepdims=True))
        if has_res:
            dx = dx + res_ref[...]
        dx_ref[...] = dx

    row = lambda width: pl.BlockSpec((tm, width), lambda i: (i, 0))
    in_specs = [row(kin), pl.BlockSpec((1, kin), lambda i: (0, 0)), pl.BlockSpec((kin, n), lambda i: (0, 0))]
    in_specs += [row(a.shape[1]) for a in dys]
    args = [x, g, w, *dys]
    if has_res:
        in_specs.append(row(kin))
        args.append(resid)
    out_specs = [row(kin), pl.BlockSpec((kin, n), lambda i: (0, 0)), pl.BlockSpec((1, kin), lambda i: (0, 0))]
    out_shape = [jax.ShapeDtypeStruct((tp, kin), F32), jax.ShapeDtypeStruct((kin, n), F32),
                 jax.ShapeDtypeStruct((1, kin), F32)]
    if has_extra:
        out_specs.append(row(extra_out[0]))
        out_shape.append(jax.ShapeDtypeStruct((tp, extra_out[0]), extra_out[1]))
    return pl.pallas_call(
        body, name=name, grid=(tp // tm,),
        in_specs=in_specs, out_specs=out_specs, out_shape=out_shape,
        scratch_shapes=[pltpu.VMEM((tm, n), BF16)],
        compiler_params=_params(("arbitrary",)),
    )(*args)


def _flash_bwd(q, k, v, lse, delta, do, heads, pad, tq):
    tp = q.shape[0]
    nq = tp // tq
    scale = (QK_NOPE + QK_ROPE) ** -0.5
    c2 = scale * LOG2E

    def body(q_ref, k_ref, v_ref, lse_ref, delta_ref, do_ref, dq_ref, dk_ref, dv_ref):
        j = pl.program_id(1)

        @pl.when(j == 0)
        def _():
            dq_ref[...] = jnp.zeros_like(dq_ref)

        kv = k_ref[...]
        vv = v_ref[...]

        def make_step(masked, blocks):
            rows = blocks * tq

            def step(i, carry):
                dk, dv = carry
                off = pl.multiple_of(i * tq, tq)
                qv = q_ref[pl.ds(off, rows), :]
                dob = do_ref[pl.ds(off, rows), :].astype(BF16)
                p = jnp.exp2(_dot_nt(qv, kv) * c2 - lse_ref[pl.ds(off, rows), 0:1])
                if masked:
                    p = jnp.where(_attn_mask(j * tq, j * tq, rows, tq, pad), p, 0.0)
                dv = dv + _dot_tn(p.astype(BF16), dob)
                dp = _dot_nt(dob, vv)
                ds = (p * (dp - delta_ref[pl.ds(off, rows), 0:1]) * scale).astype(BF16)
                dk = dk + _dot_tn(ds, qv)
                dq_ref[pl.ds(off, rows), :] += _dot(ds, kv)
                return dk, dv
            return step

        carry = make_step(True, 1)(j, (jnp.zeros((tq, HEAD_W), F32), jnp.zeros((tq, V_HEAD), F32)))
        trips = (nq - 1 - j) // FLASH_BLOCKS_PER_TRIP
        step_n = make_step(False, FLASH_BLOCKS_PER_TRIP)
        carry = lax.fori_loop(0, trips, lambda t, cr: step_n(j + 1 + FLASH_BLOCKS_PER_TRIP * t, cr), carry)
        dk, dv = lax.fori_loop(j + 1 + FLASH_BLOCKS_PER_TRIP * trips, nq, make_step(False, 1), carry)
        dk_ref[...] = dk
        dv_ref[...] = dv

    return pl.pallas_call(
        body, name="a_flash_bwd", grid=(heads, nq),
        in_specs=[pl.BlockSpec((tp, HEAD_W), lambda h, j: (0, h)),
                  pl.BlockSpec((tq, HEAD_W), lambda h, j: (j, h)),
                  pl.BlockSpec((tq, V_HEAD), lambda h, j: (j, h)),
                  pl.BlockSpec((None, tp, LANES), lambda h, j: (h, 0, 0)),
                  pl.BlockSpec((None, tp, LANES), lambda h, j: (h, 0, 0)),
                  pl.BlockSpec((tp, V_HEAD), lambda h, j: (0, h))],
        out_specs=[pl.BlockSpec((tp, HEAD_W), lambda h, j: (0, h)),
                   pl.BlockSpec((tq, HEAD_W), lambda h, j: (j, h)),
                   pl.BlockSpec((tq, V_HEAD), lambda h, j: (j, h))],
        out_shape=[jax.ShapeDtypeStruct((tp, heads * HEAD_W), F32),
                   jax.ShapeDtypeStruct((tp, heads * HEAD_W), F32),
                   jax.ShapeDtypeStruct((tp, heads * V_HEAD), F32)],
        compiler_params=_params(("parallel", "arbitrary")),
    )(q, k, v, lse, delta, do)


def _position():
    return lax.axis_index("x"), lax.axis_index("y"), lax.axis_index("c")


def _other_chips(x, y):
    return [(1 - x, y), (x, 1 - y), (1 - x, 1 - y)]


def _block(ref, shard_axis, n, k, split_axis=None, m=None, h=None):
    idx = []
    for a in range(len(ref.shape)):
        start = 0
        size = None
        if a == shard_axis:
            start, size = k * n, n
        if a == split_axis:
            size = (n if a == shard_axis else m) // 2
            start = start + h * size
        idx.append(slice(None) if size is None else pl.ds(start, size))
    return ref.at[tuple(idx)]


def _gather_weights(split, whole_small):
    ns, nw = len(split), len(whole_small)
    n = ns + nw
    arrs = [s[0] for s in split] + [s[0] for s in whole_small]
    axes = [s[1] for s in split] + [s[1] for s in whole_small]

    def body(*refs):
        ins, outs = refs[:n], refs[n:2 * n]
        ici_send, ici_recv, d2d_send, d2d_recv, sib_send, sib_recv = refs[2 * n:]
        x, y, c = _position()
        me = 2 * x + y
        others = _other_chips(x, y)
        sent, local = [], []

        def remote(src, dst, sems, idx, to):
            return pltpu.make_async_remote_copy(src_ref=src, dst_ref=dst, send_sem=sems[0].at[idx],
                                                recv_sem=sems[1].at[idx], device_id=to, device_id_type=MESH)

        for a in range(n):
            width = ins[a].shape[axes[a]]
            mine = remote(ins[a], _block(outs[a], axes[a], width, me), (sib_send, sib_recv), a, (x, y, 1 - c))
            mine.start()
            local.append(mine)
            for j, (px, py) in enumerate(others):
                if a < ns:
                    sx = split[a][2]
                    src = _block(ins[a], None, None, None, sx, ins[a].shape[sx], c)
                    dst = _block(outs[a], axes[a], width, me, sx, outs[a].shape[sx], c)
                else:
                    src, dst = ins[a], _block(outs[a], axes[a], width, me)
                cp = remote(src, dst, (ici_send, ici_recv), 3 * a + j, (px, py, c))
                cp.start()
                sent.append(cp)
        for a in range(ns):
            width = ins[a].shape[axes[a]]
            sx = split[a][2]
            for j, (px, py) in enumerate(others):
                theirs = _block(outs[a], axes[a], width, 2 * px + py, sx, outs[a].shape[sx], c)
                remote(theirs, theirs, (ici_send, ici_recv), 3 * a + j, (px, py, c)).wait_recv()
                fwd = remote(theirs, theirs, (d2d_send, d2d_recv), 3 * a + j, (x, y, 1 - c))
                fwd.start()
                sent.append(fwd)
        for a in range(ns, n):
            width = ins[a].shape[axes[a]]
            for j, (px, py) in enumerate(others):
                theirs = _block(outs[a], axes[a], width, 2 * px + py)
                remote(theirs, theirs, (ici_send, ici_recv), 3 * a + j, (px, py, c)).wait_recv()
        for a in range(ns):
            width = ins[a].shape[axes[a]]
            sx = split[a][2]
            for j, (px, py) in enumerate(others):
                from_sibling = _block(outs[a], axes[a], width, 2 * px + py, sx, outs[a].shape[sx], 1 - c)
                remote(from_sibling, from_sibling, (d2d_send, d2d_recv), 3 * a + j, (x, y, 1 - c)).wait_recv()
        for cp in sent:
            cp.wait_send()
        for cp in local:
            cp.wait()

    def whole_shape(a, axis):
        shape = list(a.shape)
        shape[axis] *= N_CHIPS
        return jax.ShapeDtypeStruct(tuple(shape), a.dtype)

    any_spec = pl.BlockSpec(memory_space=pl.ANY)
    return pl.pallas_call(
        body, name="gather_weights",
        in_specs=[any_spec] * n, out_specs=[any_spec] * n,
        out_shape=[whole_shape(a, ax) for a, ax in zip(arrs, axes)],
        scratch_shapes=[pltpu.SemaphoreType.DMA((3 * n,)), pltpu.SemaphoreType.DMA((3 * n,)),
                        pltpu.SemaphoreType.DMA((3 * ns,)), pltpu.SemaphoreType.DMA((3 * ns,)),
                        pltpu.SemaphoreType.DMA((n,)), pltpu.SemaphoreType.DMA((n,))],
        compiler_params=pltpu.CompilerParams(has_side_effects=True),
    )(*arrs)


class _Grad:
    def __init__(self, name, g, kind, rh, cols, groups=None):
        self.name, self.g, self.kind, self.rh, self.cols, self.groups = name, g, kind, rh, cols, groups
        if kind == 'rows':
            self.tr = rh
        elif kind == 'gate':
            self.tr = rh // (groups // 2)
        else:
            self.tr = 256 if rh % 256 == 0 else rh
        self.nb = rh // self.tr

    def pieces(self, ref, k, h):
        rh, cols = self.rh, self.cols
        if self.kind == 'cols':
            return [(ref.at[pl.ds(h * rh, rh), pl.ds(k * cols, cols)], 0, rh)]
        if self.kind == 'rows':
            return [(ref.at[pl.ds((2 * k + h) * rh, rh), :], 0, rh)]
        if self.kind == 'lead':
            return [(ref.at[k, pl.ds(h * rh, rh), :], 0, rh)]
        per = self.groups // 2
        return [(ref.at[pl.ds((((h * per + gi) * N_CHIPS) + k) * self.tr, self.tr), :], gi * self.tr, self.tr)
                for gi in range(per)]

    def block_spec(self):
        tr, nb, cols = self.tr, self.nb, self.cols
        if self.kind == 'cols':
            return pl.BlockSpec((tr, cols), lambda k, i, c: (c[0] * nb + i, k))
        if self.kind == 'rows':
            return pl.BlockSpec((tr, cols), lambda k, i, c: (2 * k + c[0], 0))
        if self.kind == 'lead':
            return pl.BlockSpec((None, tr, cols), lambda k, i, c: (k, c[0] * nb + i, 0))
        return pl.BlockSpec((tr, cols), lambda k, i, c: ((c[0] * nb + i) * N_CHIPS + k, 0))


def _halves_to_sibling(grads):
    n = len(grads)
    total = sum(len(gr.pieces(gr.g, 0, 0)) * N_CHIPS for gr in grads)

    def body(*refs):
        ins, outs = refs[:n], refs[n:2 * n]
        send_sems, recv_sems = refs[2 * n:]
        x, y, c = _position()
        copies = []
        for gr, g_ref, got_ref in zip(grads, ins, outs):
            for k in range(N_CHIPS):
                for src, r0, nr in gr.pieces(g_ref, k, 1 - c):
                    idx = len(copies)
                    cp = pltpu.make_async_remote_copy(
                        src_ref=src, dst_ref=got_ref.at[k, pl.ds(r0, nr), :],
                        send_sem=send_sems.at[idx], recv_sem=recv_sems.at[idx],
                        device_id=(x, y, 1 - c), device_id_type=MESH)
                    cp.start()
                    copies.append(cp)
        for cp in copies:
            cp.wait()

    any_spec = pl.BlockSpec(memory_space=pl.ANY)
    return pl.pallas_call(
        body, name="grads_to_sibling",
        in_specs=[any_spec] * n, out_specs=[any_spec] * n,
        out_shape=[jax.ShapeDtypeStruct((N_CHIPS, gr.rh, gr.cols), F32) for gr in grads],
        scratch_shapes=[pltpu.SemaphoreType.DMA((total,)), pltpu.SemaphoreType.DMA((total,))],
        compiler_params=pltpu.CompilerParams(has_side_effects=True),
    )(*[gr.g for gr in grads])


def _chip_sum(gr, got, c):
    def body(c_ref, g_ref, got_ref, o_ref):
        o_ref[...] = (g_ref[...] + got_ref[...]).astype(BF16)

    tile = pl.BlockSpec((None, gr.tr, gr.cols), lambda k, i, c_ref: (k, i, 0))
    return pl.pallas_call(
        body, name="chip_sum_" + gr.name,
        grid_spec=pltpu.PrefetchScalarGridSpec(
            num_scalar_prefetch=1, grid=(N_CHIPS, gr.nb),
            in_specs=[gr.block_spec(), tile], out_specs=tile),
        out_shape=jax.ShapeDtypeStruct((N_CHIPS, gr.rh, gr.cols), BF16),
        compiler_params=_params(("parallel", "parallel")),
    )(c, gr.g, got)


def _blocks_to_chips(parts):
    n = len(parts)

    def body(*refs):
        ins, outs = refs[:n], refs[n:2 * n]
        send_sems, recv_sems = refs[2 * n:]
        x, y, c = _position()
        copies = []
        for a in range(n):
            for j, (px, py) in enumerate(_other_chips(x, y)):
                cp = pltpu.make_async_remote_copy(
                    src_ref=ins[a].at[2 * px + py], dst_ref=outs[a].at[j],
                    send_sem=send_sems.at[3 * a + j], recv_sem=recv_sems.at[3 * a + j],
                    device_id=(px, py, c), device_id_type=MESH)
                cp.start()
                copies.append(cp)
        for cp in copies:
            cp.wait()

    any_spec = pl.BlockSpec(memory_space=pl.ANY)
    return pl.pallas_call(
        body, name="grads_to_chips",
        in_specs=[any_spec] * n, out_specs=[any_spec] * n,
        out_shape=[jax.ShapeDtypeStruct((3,) + p.shape[1:], p.dtype) for p in parts],
        scratch_shapes=[pltpu.SemaphoreType.DMA((3 * n,)), pltpu.SemaphoreType.DMA((3 * n,))],
        compiler_params=pltpu.CompilerParams(has_side_effects=True),
    )(*parts)


def _sum_chips(name, part, got, me, c):
    nk, rh, cols = part.shape
    tr = 256 if rh % 256 == 0 else rh
    nb = rh // tr

    def body(me_ref, c_ref, own_ref, *rest):
        got_refs, o_ref = rest[:nk], rest[nk]
        own = own_ref[...].astype(F32)
        acc = None
        for k in range(nk):
            term = jnp.where(me_ref[0] == k, own, got_refs[k][...].astype(F32))
            acc = term if acc is None else acc + term
        o_ref[...] = acc

    def got_map(k):
        def index(i, me_ref, c_ref):
            xor = jnp.bitwise_xor(me_ref[0], k)
            slot = jnp.where(xor == 1, 1, jnp.where(xor == 3, 2, 0))
            return (slot, i, 0)
        return index

    return pl.pallas_call(
        body, name="sum_" + name,
        grid_spec=pltpu.PrefetchScalarGridSpec(
            num_scalar_prefetch=2, grid=(nb,),
            in_specs=[pl.BlockSpec((None, tr, cols), lambda i, me_ref, c_ref: (me_ref[0], i, 0))]
            + [pl.BlockSpec((None, tr, cols), got_map(k)) for k in range(nk)],
            out_specs=pl.BlockSpec((tr, cols), lambda i, me_ref, c_ref: (c_ref[0] * nb + i, 0))),
        out_shape=jax.ShapeDtypeStruct((2 * rh, cols), F32),
        compiler_params=_params(("parallel",)),
    )(me, c, part, *([got] * nk))


def _share_with_sibling(halves):
    n = len(halves)

    def body(*refs):
        outs = refs[n:2 * n]
        send_sems, recv_sems = refs[2 * n:]
        x, y, c = _position()
        copies = []
        for a in range(n):
            rh = outs[a].shape[0] // 2
            mine = outs[a].at[pl.ds(c * rh, rh), :]
            cp = pltpu.make_async_remote_copy(
                src_ref=mine, dst_ref=mine, send_sem=send_sems.at[a], recv_sem=recv_sems.at[a],
                device_id=(x, y, 1 - c), device_id_type=MESH)
            cp.start()
            copies.append(cp)
        for cp in copies:
            cp.wait()

    any_spec = pl.BlockSpec(memory_space=pl.ANY)
    return pl.pallas_call(
        body, name="grads_share",
        in_specs=[any_spec] * n, out_specs=[any_spec] * n,
        out_shape=[jax.ShapeDtypeStruct(h.shape, h.dtype) for h in halves],
        input_output_aliases={a: a for a in range(n)},
        scratch_shapes=[pltpu.SemaphoreType.DMA((n,)), pltpu.SemaphoreType.DMA((n,))],
        compiler_params=pltpu.CompilerParams(has_side_effects=True),
    )(*halves)


def _adamw(name, w, g, m, v):
    rows, cols = w.shape
    tr = 256 if rows % 256 == 0 else rows

    def body(w_ref, g_ref, m_ref, v_ref, d_ref, nm_ref, nv_ref):
        gv = g_ref[...]
        mn = ADAM_B1 * m_ref[...] + (1.0 - ADAM_B1) * gv
        vn = ADAM_B2 * v_ref[...] + (1.0 - ADAM_B2) * (gv * gv)
        m_hat = mn / (1.0 - ADAM_B1 ** ADAM_STEP)
        v_hat = vn / (1.0 - ADAM_B2 ** ADAM_STEP)
        d_ref[...] = -ADAM_LR * (m_hat / (jnp.sqrt(v_hat) + ADAM_EPS) + ADAM_WD * w_ref[...])
        nm_ref[...] = mn
        nv_ref[...] = vn

    spec = pl.BlockSpec((tr, cols), lambda i: (i, 0))
    return pl.pallas_call(
        body, name=name, grid=(rows // tr,),
        in_specs=[spec] * 4, out_specs=[spec] * 3,
        out_shape=[jax.ShapeDtypeStruct((rows, cols), F32)] * 3,
        compiler_params=_params(("parallel",)),
    )(w, g, m, v)


def _as2d(a):
    if a.ndim == 1:
        return a.reshape(1, -1)
    return a.reshape(-1, a.shape[-1])


def _unshard(gathered, axis):
    moved = jnp.moveaxis(gathered, 0, axis)
    shape = list(gathered.shape[1:])
    shape[axis] *= N_CHIPS
    return moved.reshape(shape)


def _rope_tables(tp, pad):
    pos = jnp.arange(tp, dtype=F32) - pad
    inv_freq = ROPE_BASE ** (-jnp.arange(0, QK_ROPE, 2, dtype=F32) / QK_ROPE)
    ang = pos[:, None] * inv_freq[None, :]
    cos, sin = jnp.cos(ang), jnp.sin(ang)
    zeros = jnp.zeros((tp, LANES - QK_ROPE), F32)
    return jnp.concatenate([cos, cos, zeros], axis=1), jnp.concatenate([-sin, sin, zeros], axis=1)


def _local_grads(x, target, wt, heads):
    seq, d = x.shape
    n_meta = wt['meta_tokens'].shape[0]
    t = seq + n_meta
    pad = (-t) % Q_BLOCK
    tp = t + pad
    x0 = pad + n_meta
    tm = _row_tile(tp)
    ql = wt['a_q_norm_g'].shape[1]
    kvl = wt['a_kv_norm_g'].shape[1]
    mla_w = heads * V_HEAD
    lru_w = wt['b_conv_w'].shape[1]

    h0 = jnp.concatenate([jnp.zeros((pad, d), F32), wt['meta_tokens'], x], axis=0)
    cos_t, sin_t = _rope_tables(tp, pad)

    w_in_a = wt['a_w_in']
    zcol = jnp.zeros((d, LANES - QK_ROPE), BF16)
    w_in_a = jnp.concatenate([w_in_a[:, :ql + kvl + QK_ROPE], zcol, w_in_a[:, ql + kvl + QK_ROPE:]], axis=1)
    c_kv, c_kr, c_gate = ql, ql + kvl, ql + kvl + LANES
    splits_a = [(0, c_kv), (c_kv, c_kr), (c_kr, c_gate), (c_gate, c_gate + mla_w)]
    w_uq = wt['a_w_uq'].reshape(ql, heads, QK_NOPE + QK_ROPE)
    w_uq = jnp.pad(w_uq, ((0, 0), (0, 0), (0, HEAD_W - QK_NOPE - QK_ROPE))).reshape(ql, heads * HEAD_W)
    w_ukv = wt['a_w_ukv']

    q_lat, kv_lat, kr_raw, gate_a = _norm_matmul("a_in_proj", h0, wt['a_norm_g'], w_in_a, splits_a, tm)
    q = _q_proj(q_lat, wt['a_q_norm_g'], w_uq, cos_t, sin_t, heads, tm)
    k, v = _kv_proj(kv_lat, wt['a_kv_norm_g'], w_ukv, kr_raw, cos_t, sin_t, heads, tm)
    attn, lse = _flash_fwd(q, k, v, heads, pad, tm)
    h1 = _gated_out("a_out_proj", attn, gate_a, wt['a_w_out'], h0, tm)

    u, gate_b = _norm_matmul("b_in_proj", h1, wt['b_norm_g'], wt['b_w_in'], [(0, lru_w), (lru_w, 2 * lru_w)], tm)
    uc, r, ig, hs = _rglru_fwd(u, wt['b_conv_w'], wt['b_conv_b'], wt['b_w_rg'], wt['b_b_rg'],
                               wt['b_w_ig'], wt['b_b_ig'], wt['b_lam'], pad, tm)
    h2 = _gated_out("b_out_proj", hs, gate_b, wt['b_w_out'], h1, tm)

    dh2, loss, d_final_g = _final_loss(h2, wt['final_norm_g'], target, x0)

    dhs, dgate_b, dw_out_b = _gated_out_bwd("b_out_proj_bwd", dh2, hs, gate_b, wt['b_w_out'], tm)
    du, dconv_w, dconv_b, dw_rg, db_rg, dw_ig, db_ig, dlam = _rglru_bwd(
        dhs, hs, r, ig, uc, u, wt['b_conv_w'], wt['b_w_rg'], wt['b_w_ig'], wt['b_lam'], pad, tm)
    dh1, dw_in_b, dg_b = _norm_matmul_bwd("b_in_proj_bwd", h1, wt['b_norm_g'], wt['b_w_in'], [du, dgate_b], tm, resid=dh2)

    dattn, dgate_a, dw_out_a, delta = _gated_out_bwd("a_out_proj_bwd", dh1, attn, gate_a, wt['a_w_out'], tm,
                                                     delta_heads=heads)
    dq, dk, dv = _flash_bwd(q, k, v, lse, delta, dattn, heads, pad, tm)

    def q_prologue(dy_refs, dy_s, ex_ref):
        (dq_ref,), cos_v, sin_v = dy_refs[:1], dy_refs[1][...], dy_refs[2][...]
        for h in range(heads):
            c0 = h * HEAD_W
            dy_s[:, c0:c0 + QK_NOPE] = dq_ref[:, c0:c0 + QK_NOPE].astype(BF16)
            dy_s[:, c0 + QK_NOPE:c0 + HEAD_W] = _unrope(dq_ref[:, c0 + QK_NOPE:c0 + HEAD_W], cos_v, sin_v).astype(BF16)

    dq_lat, dw_uq, dg_q = _norm_matmul_bwd("a_q_proj_bwd", q_lat, wt['a_q_norm_g'], w_uq, [dq, cos_t, sin_t], tm,
                                           prologue=q_prologue)

    def kv_prologue(dy_refs, dy_s, ex_ref):
        dk_ref, dv_ref = dy_refs[:2]
        cos_v, sin_v = dy_refs[2][...], dy_refs[3][...]
        dkr = jnp.zeros((dk_ref.shape[0], LANES), F32)
        for h in range(heads):
            c0 = h * (QK_NOPE + V_HEAD)
            dy_s[:, c0:c0 + QK_NOPE] = dk_ref[:, h * HEAD_W:h * HEAD_W + QK_NOPE].astype(BF16)
            dy_s[:, c0 + QK_NOPE:c0 + QK_NOPE + V_HEAD] = dv_ref[:, h * V_HEAD:(h + 1) * V_HEAD].astype(BF16)
            dkr = dkr + dk_ref[:, h * HEAD_W + QK_NOPE:(h + 1) * HEAD_W]
        ex_ref[...] = _unrope(dkr, cos_v, sin_v)

    dkv_lat, dw_ukv, dg_kv, dkr_raw = _norm_matmul_bwd(
        "a_kv_proj_bwd", kv_lat, wt['a_kv_norm_g'], w_ukv, [dk, dv, cos_t, sin_t], tm,
        prologue=kv_prologue, extra_out=(LANES, F32))

    dh0, dw_in_a, dg_a = _norm_matmul_bwd("a_in_proj_bwd", h0, wt['a_norm_g'], w_in_a,
                                          [dq_lat, dkv_lat, dkr_raw, dgate_a], tm, resid=dh1)

    dw_in_a = jnp.concatenate([dw_in_a[:, :c_kr + QK_ROPE], dw_in_a[:, c_gate:]], axis=1)
    grads = {
        'meta_tokens': dh0[pad:x0], 'a_norm_g': dg_a, 'a_w_in': dw_in_a, 'a_q_norm_g': dg_q, 'a_kv_norm_g': dg_kv,
        'a_w_uq': dw_uq, 'a_w_ukv': dw_ukv, 'a_w_out': dw_out_a, 'b_norm_g': dg_b, 'b_w_in': dw_in_b,
        'b_conv_w': dconv_w, 'b_conv_b': dconv_b, 'b_w_rg': dw_rg, 'b_b_rg': db_rg, 'b_w_ig': dw_ig,
        'b_b_ig': db_ig, 'b_lam': dlam, 'b_w_out': dw_out_b, 'final_norm_g': d_final_g,
    }
    return loss, dh0[x0:], grads


def _chip_major(whole, local_shape, axis):
    if axis is None:
        return jnp.broadcast_to(whole.reshape(1, -1), (N_CHIPS, whole.size))
    shape = list(local_shape)
    g = whole.reshape(shape[:axis] + [N_CHIPS, shape[axis]] + shape[axis + 1:])
    return jnp.moveaxis(g, axis, 0).reshape(N_CHIPS, -1)


def kernel(x, meta_tokens, a_norm_g, a_w_in, a_q_norm_g, a_kv_norm_g, a_w_uq, a_w_ukv, a_w_out, b_norm_g, b_w_in, b_conv_w, b_conv_b, b_w_rg, b_b_rg, b_w_ig, b_b_ig, b_lam, b_w_out, final_norm_g, loss_target, m_meta_tokens, m_a_norm_g, m_a_w_in, m_a_q_norm_g, m_a_kv_norm_g, m_a_w_uq, m_a_w_ukv, m_a_w_out, m_b_norm_g, m_b_w_in, m_b_conv_w, m_b_conv_b, m_b_w_rg, m_b_b_rg, m_b_w_ig, m_b_b_ig, m_b_lam, m_b_w_out, m_final_norm_g, v_meta_tokens, v_a_norm_g, v_a_w_in, v_a_q_norm_g, v_a_kv_norm_g, v_a_w_uq, v_a_w_ukv, v_a_w_out, v_b_norm_g, v_b_w_in, v_b_conv_w, v_b_conv_b, v_b_w_rg, v_b_b_rg, v_b_w_ig, v_b_b_ig, v_b_lam, v_b_w_out, v_final_norm_g):
    local_w = dict(zip(WEIGHTS, (meta_tokens, a_norm_g, a_w_in, a_q_norm_g, a_kv_norm_g, a_w_uq, a_w_ukv, a_w_out,
                                 b_norm_g, b_w_in, b_conv_w, b_conv_b, b_w_rg, b_b_rg, b_w_ig, b_b_ig, b_lam,
                                 b_w_out, final_norm_g)))
    local_m = dict(zip(WEIGHTS, (m_meta_tokens, m_a_norm_g, m_a_w_in, m_a_q_norm_g, m_a_kv_norm_g, m_a_w_uq,
                                 m_a_w_ukv, m_a_w_out, m_b_norm_g, m_b_w_in, m_b_conv_w, m_b_conv_b, m_b_w_rg,
                                 m_b_b_rg, m_b_w_ig, m_b_b_ig, m_b_lam, m_b_w_out, m_final_norm_g)))
    local_v = dict(zip(WEIGHTS, (v_meta_tokens, v_a_norm_g, v_a_w_in, v_a_q_norm_g, v_a_kv_norm_g, v_a_w_uq,
                                 v_a_w_ukv, v_a_w_out, v_b_norm_g, v_b_w_in, v_b_conv_w, v_b_conv_b, v_b_w_rg,
                                 v_b_b_rg, v_b_w_ig, v_b_b_ig, v_b_lam, v_b_w_out, v_final_norm_g)))
    matrices = ('a_w_in', 'a_w_uq', 'a_w_ukv', 'a_w_out', 'b_w_in', 'b_w_rg', 'b_w_ig', 'b_w_out')
    heads = a_w_uq.shape[-1] * N_CHIPS // (QK_NOPE + QK_ROPE)

    split, small = [], []
    for n in WEIGHTS:
        if SHARD_AXIS[n] is None:
            continue
        if n == 'a_w_in':
            split.append((n, local_w[n].astype(BF16)[None], 0, 2))
        elif n in matrices:
            split.append((n, local_w[n].astype(BF16), SHARD_AXIS[n], 1))
        else:
            small.append((n, local_w[n], SHARD_AXIS[n]))
    gathered = _gather_weights([s[1:] for s in split], [s[1:] for s in small])
    whole = dict(zip([s[0] for s in split + small], gathered))
    whole['a_w_in'] = _unshard(whole['a_w_in'], SHARD_AXIS['a_w_in'])
    wt = {}
    for n in WEIGHTS:
        w = whole.get(n, local_w[n])
        wt[n] = w[0] if n in ('b_w_rg', 'b_w_ig', 'b_conv_w') else _as2d(w)

    loss, grad_x, grads = _local_grads(x[0], loss_target[0], wt, heads)

    def shard2d(n):
        s = local_w[n].shape
        return (s[-3] * s[-2], s[-1]) if n in ('b_w_rg', 'b_w_ig') else (s[-2], s[-1])

    ext_uq = heads % N_CHIPS == 0
    exchange = []
    for n in matrices:
        rows, cols = shard2d(n)
        g = grads[n]
        if n == 'a_w_in' or (n == 'a_w_uq' and not ext_uq):
            if n == 'a_w_uq':
                g = g.reshape(g.shape[0], heads, HEAD_W)[:, :, :QK_NOPE + QK_ROPE].reshape(g.shape[0], -1)
            exchange.append(_Grad(n, _chip_major(g, local_w[n].shape, SHARD_AXIS[n]).reshape(N_CHIPS, rows, cols),
                                  'lead', rows // 2, cols))
        elif n == 'a_w_uq':
            exchange.append(_Grad(n, g, 'cols', rows // 2, g.shape[1] // N_CHIPS))
        elif n in ('b_w_rg', 'b_w_ig'):
            groups = g.shape[0]
            exchange.append(_Grad(n, g.reshape(-1, cols), 'gate', rows // 2, cols, groups))
        elif SHARD_AXIS[n] == 2:
            exchange.append(_Grad(n, g, 'cols', rows // 2, cols))
        else:
            exchange.append(_Grad(n, g, 'rows', rows // 2, cols))
    rest = [n for n in WEIGHTS if n not in matrices]
    pieces = [_chip_major(grads[n], local_w[n].shape, SHARD_AXIS[n]) for n in rest]
    length = sum(p.shape[1] for p in pieces)
    unit = 2 * SUBLANES * 1024
    padded = -(-length // unit) * unit
    flat = jnp.concatenate(pieces + [jnp.zeros((N_CHIPS, padded - length), F32)], axis=1)
    exchange.append(_Grad('small', flat.reshape(N_CHIPS, padded // 1024, 1024), 'lead', padded // 2048, 1024))

    c_idx = lax.axis_index("c").astype(jnp.int32).reshape(1)
    me_idx = (2 * lax.axis_index("x") + lax.axis_index("y")).astype(jnp.int32).reshape(1)
    got = _halves_to_sibling(exchange)
    chip_sums = [_chip_sum(gr, r, c_idx) for gr, r in zip(exchange, got)]
    from_chips = _blocks_to_chips(chip_sums)
    halves = [_sum_chips(gr.name, p, r, me_idx, c_idx) for gr, p, r in zip(exchange, chip_sums, from_chips)]
    summed = dict(zip([gr.name for gr in exchange], _share_with_sibling(halves)))
    if ext_uq:
        g = summed['a_w_uq']
        summed['a_w_uq'] = g.reshape(g.shape[0], -1, HEAD_W)[:, :, :QK_NOPE + QK_ROPE]
    total = summed['small'].reshape(-1)

    out_g, out_d, out_m, out_v = [], [], [], []
    off = 0
    for n in WEIGHTS:
        shape = local_w[n].shape
        if n in matrices:
            g = summed[n].reshape(shape)
        else:
            size = 1
            for s in shape:
                size *= s
            g = total[off:off + size].reshape(shape)
            off += size
        delta, new_m, new_v = _adamw("adamw_" + n, _as2d(local_w[n]), _as2d(g), _as2d(local_m[n]), _as2d(local_v[n]))
        out_g.append(g)
        out_d.append(delta.reshape(shape))
        out_m.append(new_m.reshape(shape))
        out_v.append(new_v.reshape(shape))

    loss = lax.psum(loss[0, 0], ("x", "y", "c"))
    return (loss, grad_x[None], *out_g, *out_d, *out_m, *out_v)
```

```python
import functools

import jax
import jax.numpy as jnp
from jax import lax
from jax.experimental import pallas as pl
from jax.experimental.pallas import tpu as pltpu

F32 = jnp.float32
BF16 = jnp.bfloat16
MESH = pl.DeviceIdType.MESH

RMS_EPS = 1e-6
QK_NOPE = 128
QK_ROPE = 64
V_HEAD = 128
HEAD_W = 256
ROPE_BASE = 10000.0
Q_BLOCK = 128
MASK_VALUE = -1e30
CONV_WIDTH = 4
LRU_C = 8.0
N_CHIPS = 4

ADAM_LR = 0.001
ADAM_B1 = 0.9
ADAM_B2 = 0.999
ADAM_EPS = 1e-08
ADAM_WD = 0.01
ADAM_STEP = 10

VMEM_LIMIT_V7X = 56 * 1024 * 1024
LANES = 128
SUBLANES = 8

WEIGHTS = ['meta_tokens', 'a_norm_g', 'a_w_in', 'a_q_norm_g', 'a_kv_norm_g', 'a_w_uq', 'a_w_ukv',
           'a_w_out', 'b_norm_g', 'b_w_in', 'b_conv_w', 'b_conv_b', 'b_w_rg', 'b_b_rg', 'b_w_ig',
           'b_b_ig', 'b_lam', 'b_w_out', 'final_norm_g']
SHARD_AXIS = {'meta_tokens': 1, 'a_norm_g': None, 'a_w_in': 2, 'a_q_norm_g': None, 'a_kv_norm_g': None,
              'a_w_uq': 2, 'a_w_ukv': 2, 'a_w_out': 1, 'b_norm_g': 1, 'b_w_in': 2, 'b_conv_w': 2,
              'b_conv_b': 1, 'b_w_rg': 2, 'b_b_rg': 1, 'b_w_ig': 2, 'b_b_ig': 1, 'b_lam': 1,
              'b_w_out': 1, 'final_norm_g': None}


def _params(sem=None):
    return pltpu.CompilerParams(dimension_semantics=sem, vmem_limit_bytes=VMEM_LIMIT_V7X)


def _row_tile(tp):
    return 384 if (tp % 384 == 0 and tp >= 1152) else 128


def _sigmoid(x):
    return 1.0 / (1.0 + jnp.exp(-x))


def _rms(x):
    return lax.rsqrt(jnp.mean(x * x, axis=-1, keepdims=True) + RMS_EPS)


def _swap32(x):
    lane = lax.broadcasted_iota(jnp.int32, x.shape, 1)
    return jnp.where(lane < 32, pltpu.roll(x, 96, 1), pltpu.roll(x, 32, 1))


def _rope(x, cos_t, sin_t):
    return x * cos_t + _swap32(x) * sin_t


def _unrope(d, cos_t, sin_t):
    lane = lax.broadcasted_iota(jnp.int32, d.shape, 1)
    return jnp.where(lane < QK_ROPE, d * cos_t + _swap32(d * sin_t), 0.0)


def _dot(a, b):
    return jnp.dot(a, b, preferred_element_type=F32)


def _dot_nt(a, b):
    return lax.dot_general(a, b, (((1,), (1,)), ((), ())), preferred_element_type=F32)


def _dot_tn(a, b):
    return lax.dot_general(a, b, (((0,), (0,)), ((), ())), preferred_element_type=F32)


def _norm_matmul(name, x, g, w, splits, tm):
    tp, kin = x.shape
    n = w.shape[1]

    def body(x_ref, g_ref, w_ref, *outs):
        xv = x_ref[...]
        nrm = ((xv * _rms(xv)) * g_ref[...]).astype(BF16)
        y = _dot(nrm, w_ref[...])
        for o_ref, (c0, c1) in zip(outs, splits):
            o_ref[...] = y[:, c0:c1]

    return pl.pallas_call(
        body, name=name, grid=(tp // tm,),
        in_specs=[pl.BlockSpec((tm, kin), lambda i: (i, 0)),
                  pl.BlockSpec((1, kin), lambda i: (0, 0)),
                  pl.BlockSpec((kin, n), lambda i: (0, 0))],
        out_specs=[pl.BlockSpec((tm, c1 - c0), lambda i: (i, 0)) for c0, c1 in splits],
        out_shape=[jax.ShapeDtypeStruct((tp, c1 - c0), F32) for c0, c1 in splits],
        compiler_params=_params(("parallel",)),
    )(x, g, w)


def _q_proj(q_lat, g, w_uq, cos_t, sin_t, heads, tm):
    tp, kin = q_lat.shape
    n = heads * HEAD_W

    def body(x_ref, g_ref, w_ref, cos_ref, sin_ref, q_ref):
        xv = x_ref[...]
        nrm = ((xv * _rms(xv)) * g_ref[...]).astype(BF16)
        y = _dot(nrm, w_ref[...])
        cos_v, sin_v = cos_ref[...], sin_ref[...]
        for h in range(heads):
            c0 = h * HEAD_W
            q_ref[:, c0:c0 + QK_NOPE] = y[:, c0:c0 + QK_NOPE].astype(BF16)
            q_ref[:, c0 + QK_NOPE:c0 + HEAD_W] = _rope(y[:, c0 + QK_NOPE:c0 + HEAD_W], cos_v, sin_v).astype(BF16)

    return pl.pallas_call(
        body, name="a_q_proj", grid=(tp // tm,),
        in_specs=[pl.BlockSpec((tm, kin), lambda i: (i, 0)),
                  pl.BlockSpec((1, kin), lambda i: (0, 0)),
                  pl.BlockSpec((kin, n), lambda i: (0, 0)),
                  pl.BlockSpec((tm, LANES), lambda i: (i, 0)),
                  pl.BlockSpec((tm, LANES), lambda i: (i, 0))],
        out_specs=pl.BlockSpec((tm, n), lambda i: (i, 0)),
        out_shape=jax.ShapeDtypeStruct((tp, n), BF16),
        compiler_params=_params(("parallel",)),
    )(q_lat, g, w_uq, cos_t, sin_t)


def _kv_proj(kv_lat, g, w_ukv, k_rope_raw, cos_t, sin_t, heads, tm):
    tp, kin = kv_lat.shape
    n = heads * (QK_NOPE + V_HEAD)

    def body(x_ref, g_ref, w_ref, kr_ref, cos_ref, sin_ref, k_ref, v_ref):
        xv = x_ref[...]
        nrm = ((xv * _rms(xv)) * g_ref[...]).astype(BF16)
        y = _dot(nrm, w_ref[...])
        kr = _rope(kr_ref[...], cos_ref[...], sin_ref[...]).astype(BF16)
        for h in range(heads):
            c0 = h * (QK_NOPE + V_HEAD)
            k_ref[:, h * HEAD_W:h * HEAD_W + QK_NOPE] = y[:, c0:c0 + QK_NOPE].astype(BF16)
            k_ref[:, h * HEAD_W + QK_NOPE:(h + 1) * HEAD_W] = kr
            v_ref[:, h * V_HEAD:(h + 1) * V_HEAD] = y[:, c0 + QK_NOPE:c0 + QK_NOPE + V_HEAD].astype(BF16)

    return pl.pallas_call(
        body, name="a_kv_proj", grid=(tp // tm,),
        in_specs=[pl.BlockSpec((tm, kin), lambda i: (i, 0)),
                  pl.BlockSpec((1, kin), lambda i: (0, 0)),
                  pl.BlockSpec((kin, n), lambda i: (0, 0)),
                  pl.BlockSpec((tm, LANES), lambda i: (i, 0)),
                  pl.BlockSpec((tm, LANES), lambda i: (i, 0)),
                  pl.BlockSpec((tm, LANES), lambda i: (i, 0))],
        out_specs=[pl.BlockSpec((tm, heads * HEAD_W), lambda i: (i, 0)),
                   pl.BlockSpec((tm, heads * V_HEAD), lambda i: (i, 0))],
        out_shape=[jax.ShapeDtypeStruct((tp, heads * HEAD_W), BF16),
                   jax.ShapeDtypeStruct((tp, heads * V_HEAD), BF16)],
        compiler_params=_params(("parallel",)),
    )(kv_lat, g, w_ukv, k_rope_raw, cos_t, sin_t)


def _as_rows(col):
    rows = col.shape[0]
    return jnp.transpose(jnp.broadcast_to(col, (rows, LANES)))[0:SUBLANES, :]


def _attn_mask(row0, col0, rows, cols, pad):
    row = row0 + lax.broadcasted_iota(jnp.int32, (rows, cols), 0)
    col = col0 + lax.broadcasted_iota(jnp.int32, (rows, cols), 1)
    return (col <= row) & (col >= pad)


LOG2E = 1.4426950408889634
FLASH_FWD_BLOCKS_PER_TRIP = (4, 2, 1)


def _flash_fwd(q, k, v, heads, pad, tq):
    tp = q.shape[0]
    c2 = (QK_NOPE + QK_ROPE) ** -0.5 * LOG2E

    def body(q_ref, k_ref, v_ref, o_ref, lse_ref):
        i = pl.program_id(1)

        def make_step(masked, blocks):
            keys = blocks * tq

            def step(j, carry):
                m, l, acc = carry
                off = pl.multiple_of(j * tq, tq)
                s = _dot_nt(q_ref[...], k_ref[pl.ds(off, keys), :]) * c2
                if masked:
                    s = jnp.where(_attn_mask(i * tq, j * tq, tq, keys, pad), s, MASK_VALUE)
                m_new = jnp.maximum(m, jnp.max(s, axis=-1, keepdims=True))
                p = jnp.exp2(s - m_new)
                alpha = jnp.exp2(m - m_new)
                l = alpha * l + jnp.sum(p, axis=-1, keepdims=True)
                acc = alpha * acc + _dot(p.astype(BF16), v_ref[pl.ds(off, keys), :])
                return m_new, l, acc
            return step

        init = (jnp.full((tq, 1), MASK_VALUE, F32), jnp.zeros((tq, 1), F32), jnp.zeros((tq, V_HEAD), F32))
        carry = make_step(True, 1)(0, init)
        first = 1
        for blocks in FLASH_FWD_BLOCKS_PER_TRIP:
            trips = jnp.maximum(i - first, 0) // blocks
            step_n = make_step(False, blocks)
            carry = lax.fori_loop(0, trips, lambda t, cr, f=first, b=blocks, s=step_n: s(f + b * t, cr), carry)
            first = first + blocks * trips
        m, l, acc = lax.fori_loop(jnp.maximum(i, 1), i + 1, make_step(True, 1), carry)
        o_ref[...] = acc / l
        lse_ref[...] = _as_rows(m + jnp.log(l) * LOG2E)

    return pl.pallas_call(
        body, name="a_flash_fwd", grid=(heads, tp // tq),
        in_specs=[pl.BlockSpec((tq, HEAD_W), lambda h, i: (i, h)),
                  pl.BlockSpec((tp, HEAD_W), lambda h, i: (0, h)),
                  pl.BlockSpec((tp, V_HEAD), lambda h, i: (0, h))],
        out_specs=[pl.BlockSpec((tq, V_HEAD), lambda h, i: (i, h)),
                   pl.BlockSpec((None, None, SUBLANES, tq), lambda h, i: (h, i, 0, 0))],
        out_shape=[jax.ShapeDtypeStruct((tp, heads * V_HEAD), F32),
                   jax.ShapeDtypeStruct((heads, tp // tq, SUBLANES, tq), F32)],
        compiler_params=_params(("parallel", "parallel")),
    )(q, k, v)


def _gated_out(name, a, gate, w, resid, tm):
    tp, wd = a.shape
    d = w.shape[1]

    def body(a_ref, gate_ref, w_ref, res_ref, o_ref):
        gv = gate_ref[...]
        y = (a_ref[...] * (gv * _sigmoid(gv))).astype(BF16)
        o_ref[...] = res_ref[...] + _dot(y, w_ref[...])

    return pl.pallas_call(
        body, name=name, grid=(tp // tm,),
        in_specs=[pl.BlockSpec((tm, wd), lambda i: (i, 0)),
                  pl.BlockSpec((tm, wd), lambda i: (i, 0)),
                  pl.BlockSpec((wd, d), lambda i: (0, 0)),
                  pl.BlockSpec((tm, d), lambda i: (i, 0))],
        out_specs=pl.BlockSpec((tm, d), lambda i: (i, 0)),
        out_shape=jax.ShapeDtypeStruct((tp, d), F32),
        compiler_params=_params(("parallel",)),
    )(a, gate, w, resid)


def _lru_decay(r, sp):
    log_a = -LRU_C * r * sp
    a = jnp.exp(log_a)
    e2 = a * a
    x2 = 2.0 * log_a
    series = x2 * (1.0 + x2 * (0.5 + x2 * (1.0 / 6.0)))
    em1 = jnp.where(x2 > -0.02, series, e2 - 1.0)
    return a, e2, jnp.sqrt(-em1)


def _softplus(x):
    return jnp.maximum(x, 0.0) + jnp.log1p(jnp.exp(-jnp.abs(x)))


def _rglru_fwd(u, conv_w, conv_b, w_rg, b_rg, w_ig, b_ig, lam, pad, tm):
    tp, w = u.shape
    groups, blk = w_rg.shape[0], w_rg.shape[1]

    def body(u_ref, cw_ref, cb_ref, wr_ref, br_ref, wi_ref, bi_ref, lam_ref,
             uc_ref, r_ref, ig_ref, hs_ref, uext, a_s, b_s, hc):
        i = pl.program_id(0)

        @pl.when(i == 0)
        def _():
            uext[0:SUBLANES, :] = jnp.zeros((SUBLANES, w), F32)
            hc[...] = jnp.zeros((SUBLANES, w), F32)

        uext[SUBLANES:SUBLANES + tm, :] = u_ref[...]
        cw = cw_ref[...]
        uc = cb_ref[...] + uext[pl.ds(SUBLANES - 3, tm), :] * cw[0:1, :]
        uc = uc + uext[pl.ds(SUBLANES - 2, tm), :] * cw[1:2, :]
        uc = uc + uext[pl.ds(SUBLANES - 1, tm), :] * cw[2:3, :]
        uc = uc + uext[pl.ds(SUBLANES, tm), :] * cw[3:4, :]
        uc_ref[...] = uc
        uext[0:SUBLANES, :] = uext[tm:tm + SUBLANES, :]

        sp = _softplus(-lam_ref[...])
        for g in range(groups):
            sl = slice(g * blk, (g + 1) * blk)
            ucg = uc_ref[:, sl]
            ucb = ucg.astype(BF16)
            r = _sigmoid(_dot(ucb, wr_ref[g]) + br_ref[:, sl])
            ig = _sigmoid(_dot(ucb, wi_ref[g]) + bi_ref[:, sl])
            r_ref[:, sl] = r
            ig_ref[:, sl] = ig
            a, _, mult = _lru_decay(r, sp[:, sl])
            a_s[:, sl] = a
            b_s[:, sl] = mult * (ig * ucg)

        @pl.when(i == 0)
        def _():
            row = lax.broadcasted_iota(jnp.int32, (Q_BLOCK, w), 0)
            start = ig_ref[0:Q_BLOCK, :] * uc_ref[0:Q_BLOCK, :]
            b_s[0:Q_BLOCK, :] = jnp.where(row < pad, 0.0, jnp.where(row == pad, start, b_s[0:Q_BLOCK, :]))

        row8 = lax.broadcasted_iota(jnp.int32, (SUBLANES, w), 0)

        def group(gi, h_in):
            off = pl.multiple_of(gi * SUBLANES, SUBLANES)
            av = a_s[pl.ds(off, SUBLANES), :]
            bv = b_s[pl.ds(off, SUBLANES), :]
            for k in (1, 2, 4):
                keep = row8 >= k
                bv = jnp.where(keep, av * pltpu.roll(bv, k, 0) + bv, bv)
                av = jnp.where(keep, av * pltpu.roll(av, k, 0), av)
            hv = av * h_in + bv
            hs_ref[pl.ds(off, SUBLANES), :] = hv
            return jnp.broadcast_to(hv[SUBLANES - 1:SUBLANES, :], (SUBLANES, w))

        hc[...] = lax.fori_loop(0, tm // SUBLANES, group, hc[...])

    row_spec = pl.BlockSpec((tm, w), lambda i: (i, 0))
    vec_spec = pl.BlockSpec((1, w), lambda i: (0, 0))
    mat_spec = pl.BlockSpec((groups, blk, blk), lambda i: (0, 0, 0))
    return pl.pallas_call(
        body, name="b_rglru_fwd", grid=(tp // tm,),
        in_specs=[row_spec, pl.BlockSpec((CONV_WIDTH, w), lambda i: (0, 0)), vec_spec,
                  mat_spec, vec_spec, mat_spec, vec_spec, vec_spec],
        out_specs=[row_spec, row_spec, row_spec, row_spec],
        out_shape=[jax.ShapeDtypeStruct((tp, w), F32)] * 4,
        scratch_shapes=[pltpu.VMEM((tm + SUBLANES, w), F32), pltpu.VMEM((tm, w), F32),
                        pltpu.VMEM((tm, w), F32), pltpu.VMEM((SUBLANES, w), F32)],
        compiler_params=_params(("arbitrary",)),
    )(u, conv_w, conv_b, w_rg, b_rg, w_ig, b_ig, lam)


def _final_loss(h, g, target, x0):
    tp, d = h.shape
    tm = Q_BLOCK
    assert x0 % tm == 0 and target.shape[0] == tp - x0
    lead = x0 // tm

    def body(h_ref, g_ref, t_ref, dh_ref, loss_ref, dg_ref):
        i = pl.program_id(0)

        @pl.when(i == 0)
        def _():
            loss_ref[...] = jnp.zeros_like(loss_ref)
            dg_ref[...] = jnp.zeros_like(dg_ref)

        xv = h_ref[...]
        gv = g_ref[...]
        r = _rms(xv)
        xh = xv * r
        err = jnp.where(i >= lead, xh * gv - t_ref[...], 0.0)
        loss_ref[...] += 0.5 * jnp.sum(jnp.mean(err * err, axis=-1, keepdims=True))
        dy = err / d
        dg_ref[...] += jnp.sum(dy * xh, axis=0, keepdims=True)
        dxh = dy * gv
        dh_ref[...] = r * (dxh - xh * jnp.mean(dxh * xh, axis=-1, keepdims=True))

    return pl.pallas_call(
        body, name="final_loss", grid=(tp // tm,),
        in_specs=[pl.BlockSpec((tm, d), lambda i: (i, 0)),
                  pl.BlockSpec((1, d), lambda i: (0, 0)),
                  pl.BlockSpec((tm, d), lambda i: (jnp.maximum(i - lead, 0), 0))],
        out_specs=[pl.BlockSpec((tm, d), lambda i: (i, 0)),
                   pl.BlockSpec((SUBLANES, LANES), lambda i: (0, 0)),
                   pl.BlockSpec((1, d), lambda i: (0, 0))],
        out_shape=[jax.ShapeDtypeStruct((tp, d), F32),
                   jax.ShapeDtypeStruct((SUBLANES, LANES), F32),
                   jax.ShapeDtypeStruct((1, d), F32)],
        compiler_params=_params(("arbitrary",)),
    )(h, g, target)


def _gated_out_bwd(name, dout, a, gate, w, tm, delta_heads=0):
    tp, wd = a.shape
    d = w.shape[1]

    def body(do_ref, a_ref, gate_ref, w_ref, da_ref, dgate_ref, dw_ref, *delta_ref):
        @pl.when(pl.program_id(0) == 0)
        def _():
            dw_ref[...] = jnp.zeros_like(dw_ref)

        gv = gate_ref[...]
        av = a_ref[...]
        sg = _sigmoid(gv)
        silu = gv * sg
        dob = do_ref[...].astype(BF16)
        dy = _dot_nt(dob, w_ref[...])
        da = dy * silu
        da_ref[...] = da
        dgate_ref[...] = dy * av * (sg * (1.0 + gv * (1.0 - sg)))
        dw_ref[...] += _dot_tn((av * silu).astype(BF16), dob)
        for h in range(delta_heads):
            sl = slice(h * V_HEAD, (h + 1) * V_HEAD)
            delta_ref[0][h] = _as_rows(jnp.sum(da[:, sl] * av[:, sl], axis=-1, keepdims=True))

    out_specs = [pl.BlockSpec((tm, wd), lambda i: (i, 0)),
                 pl.BlockSpec((tm, wd), lambda i: (i, 0)),
                 pl.BlockSpec((wd, d), lambda i: (0, 0))]
    out_shape = [jax.ShapeDtypeStruct((tp, wd), F32),
                 jax.ShapeDtypeStruct((tp, wd), F32),
                 jax.ShapeDtypeStruct((wd, d), F32)]
    if delta_heads:
        out_specs.append(pl.BlockSpec((delta_heads, None, SUBLANES, tm), lambda i: (0, i, 0, 0)))
        out_shape.append(jax.ShapeDtypeStruct((delta_heads, tp // tm, SUBLANES, tm), F32))
    return pl.pallas_call(
        body, name=name, grid=(tp // tm,),
        in_specs=[pl.BlockSpec((tm, d), lambda i: (i, 0)),
                  pl.BlockSpec((tm, wd), lambda i: (i, 0)),
                  pl.BlockSpec((tm, wd), lambda i: (i, 0)),
                  pl.BlockSpec((wd, d), lambda i: (0, 0))],
        out_specs=out_specs, out_shape=out_shape,
        compiler_params=_params(("arbitrary",)),
    )(dout, a, gate, w)


def _rglru_bwd(dhs, hs, r, ig, uc, u, conv_w, w_rg, w_ig, lam, pad, tm):
    tp, w = u.shape
    groups, blk = w_rg.shape[0], w_rg.shape[1]
    nt = tp // tm
    per8 = tm // SUBLANES

    def body(dhs_ref, hs_ref, hprev_ref, r_ref, ig_ref, uc_ref, u_ref, uprev_ref, cw_ref, wr_ref, wi_ref, lam_ref,
             du_ref, dcw_ref, dcb_ref, dwr_ref, dbr_ref, dwi_ref, dbi_ref, dlam_ref,
             aext, c_s, g_s, hext, uext, ducext, gc):
        step = pl.program_id(0)
        ti = nt - 1 - step

        @pl.when(step == 0)
        def _():
            for ref in (dcw_ref, dcb_ref, dwr_ref, dbr_ref, dwi_ref, dbi_ref, dlam_ref):
                ref[...] = jnp.zeros_like(ref)
            aext[tm:tm + SUBLANES, :] = jnp.zeros((SUBLANES, w), F32)
            ducext[tm:tm + SUBLANES, :] = jnp.zeros((SUBLANES, w), F32)
            gc[...] = jnp.zeros((SUBLANES, w), F32)

        lam_v = lam_ref[...]
        sp = _softplus(-lam_v)
        row = ti * tm + lax.broadcasted_iota(jnp.int32, (tm, w), 0)

        rv = r_ref[...]
        a, e2, mult = _lru_decay(rv, sp)
        aext[0:tm, :] = a
        c_s[...] = aext[pl.ds(1, tm), :]
        row8 = lax.broadcasted_iota(jnp.int32, (SUBLANES, w), 0)

        def group(gi, g_in):
            off = pl.multiple_of((per8 - 1 - gi) * SUBLANES, SUBLANES)
            cv = c_s[pl.ds(off, SUBLANES), :]
            dv = dhs_ref[pl.ds(off, SUBLANES), :]
            for k in (1, 2, 4):
                keep = row8 < SUBLANES - k
                dv = jnp.where(keep, cv * pltpu.roll(dv, SUBLANES - k, 0) + dv, dv)
                cv = jnp.where(keep, cv * pltpu.roll(cv, SUBLANES - k, 0), cv)
            gv = cv * g_in + dv
            g_s[pl.ds(off, SUBLANES), :] = gv
            return jnp.broadcast_to(gv[0:1, :], (SUBLANES, w))

        gc[...] = lax.fori_loop(0, per8, group, gc[...])
        aext[tm:tm + SUBLANES, :] = aext[0:SUBLANES, :]

        gsc = jnp.where(row < pad, 0.0, g_s[...])
        hext[0:SUBLANES, :] = hprev_ref[...]
        hext[SUBLANES:SUBLANES + tm, :] = hs_ref[...]
        hprev = jnp.where(row == 0, 0.0, hext[pl.ds(SUBLANES - 1, tm), :])
        igv = ig_ref[...]
        ucv = uc_ref[...]
        first = row == pad
        mult = jnp.where(first, 1.0, mult)
        dmult = gsc * (igv * ucv)
        dig = gsc * mult * ucv
        duc = gsc * mult * igv
        dlog_a = (gsc * hprev) * a + jnp.where(first, 0.0, dmult * (-e2 / mult))
        dlam_ref[...] += jnp.sum(dlog_a * rv, axis=0, keepdims=True) * (LRU_C * _sigmoid(-lam_v))
        dpre_r = dlog_a * (-LRU_C * sp) * (rv * (1.0 - rv))
        dpre_i = dig * (igv * (1.0 - igv))
        dbr_ref[...] += jnp.sum(dpre_r, axis=0, keepdims=True)
        dbi_ref[...] += jnp.sum(dpre_i, axis=0, keepdims=True)
        for g in range(groups):
            sl = slice(g * blk, (g + 1) * blk)
            ucb = ucv[:, sl].astype(BF16)
            drb = dpre_r[:, sl].astype(BF16)
            dib = dpre_i[:, sl].astype(BF16)
            dwr_ref[g] += _dot_tn(ucb, drb)
            dwi_ref[g] += _dot_tn(ucb, dib)
            ducext[0:tm, sl] = duc[:, sl] + _dot_nt(drb, wr_ref[g]) + _dot_nt(dib, wi_ref[g])

        ducv = ducext[0:tm, :]
        cw = cw_ref[...]
        dcb_ref[...] += jnp.sum(ducv, axis=0, keepdims=True)
        uext[0:SUBLANES, :] = jnp.where(ti == 0, 0.0, uprev_ref[...])
        uext[SUBLANES:SUBLANES + tm, :] = u_ref[...]
        for j in range(CONV_WIDTH):
            ush = uext[pl.ds(SUBLANES - (CONV_WIDTH - 1 - j), tm), :]
            dcw_ref[j:j + 1, :] += jnp.sum(ducv * ush, axis=0, keepdims=True)
        du = ducv * cw[3:4, :]
        for k in range(1, CONV_WIDTH):
            du = du + ducext[pl.ds(k, tm), :] * cw[3 - k:4 - k, :]
        du_ref[...] = du
        ducext[tm:tm + SUBLANES, :] = ducext[0:SUBLANES, :]

    rev = lambda s: (nt - 1 - s, 0)
    halo = lambda s: (jnp.maximum((nt - 1 - s) * per8 - 1, 0), 0)
    row_spec = pl.BlockSpec((tm, w), rev)
    halo_spec = pl.BlockSpec((SUBLANES, w), halo)
    vec_spec = pl.BlockSpec((1, w), lambda s: (0, 0))
    mat_spec = pl.BlockSpec((groups, blk, blk), lambda s: (0, 0, 0))
    cw_spec = pl.BlockSpec((CONV_WIDTH, w), lambda s: (0, 0))
    return pl.pallas_call(
        body, name="b_rglru_bwd", grid=(nt,),
        in_specs=[row_spec, row_spec, halo_spec, row_spec, row_spec, row_spec, row_spec, halo_spec,
                  cw_spec, mat_spec, mat_spec, vec_spec],
        out_specs=[row_spec, cw_spec, vec_spec, mat_spec, vec_spec, mat_spec, vec_spec, vec_spec],
        out_shape=[jax.ShapeDtypeStruct((tp, w), F32), jax.ShapeDtypeStruct((CONV_WIDTH, w), F32),
                   jax.ShapeDtypeStruct((1, w), F32), jax.ShapeDtypeStruct((groups, blk, blk), F32),
                   jax.ShapeDtypeStruct((1, w), F32), jax.ShapeDtypeStruct((groups, blk, blk), F32),
                   jax.ShapeDtypeStruct((1, w), F32), jax.ShapeDtypeStruct((1, w), F32)],
        scratch_shapes=[pltpu.VMEM((tm + SUBLANES, w), F32), pltpu.VMEM((tm, w), F32), pltpu.VMEM((tm, w), F32),
                        pltpu.VMEM((tm + SUBLANES, w), F32), pltpu.VMEM((tm + SUBLANES, w), F32),
                        pltpu.VMEM((tm + SUBLANES, w), F32), pltpu.VMEM((SUBLANES, w), F32)],
        compiler_params=_params(("arbitrary",)),
    )(dhs, hs, hs, r, ig, uc, u, u, conv_w, w_rg, w_ig, lam)


def _norm_matmul_bwd(name, x, g, w, dys, tm, resid=None, prologue=None, extra_out=None):
    tp, kin = x.shape
    n = w.shape[1]
    n_dy = len(dys)
    has_res = resid is not None
    has_extra = extra_out is not None

    def body(*refs):
        x_ref, g_ref, w_ref = refs[:3]
        dy_refs = refs[3:3 + n_dy]
        pos = 3 + n_dy
        res_ref = refs[pos] if has_res else None
        pos += int(has_res)
        dx_ref, dw_ref, dg_ref = refs[pos:pos + 3]
        pos += 3
        ex_ref = refs[pos] if has_extra else None
        pos += int(has_extra)
        dy_s = refs[pos]

        @pl.when(pl.program_id(0) == 0)
        def _():
            dw_ref[...] = jnp.zeros_like(dw_ref)
            dg_ref[...] = jnp.zeros_like(dg_ref)

        if prologue is None:
            c0 = 0
            for ref in dy_refs:
                dy_s[:, c0:c0 + ref.shape[1]] = ref[...].astype(BF16)
                c0 += ref.shape[1]
        else:
            prologue(dy_refs, dy_s, ex_ref)

        xv = x_ref[...]
        gv = g_ref[...]
        r = _rms(xv)
        xh = xv * r
        dyb = dy_s[...]
        dn = _dot_nt(dyb, w_ref[...])
        dw_ref[...] += _dot_tn((xh * gv).astype(BF16), dyb)
        dg_ref[...] += jnp.sum(dn * xh, axis=0, keepdims=True)
        dxh = dn * gv
        dx = r * (dxh - xh * jnp.mean(dxh * xh, axis=-1, keepdims=True))
        if has_res:
            dx = dx + res_ref[...]
        dx_ref[...] = dx

    row = lambda width: pl.BlockSpec((tm, width), lambda i: (i, 0))
    in_specs = [row(kin), pl.BlockSpec((1, kin), lambda i: (0, 0)), pl.BlockSpec((kin, n), lambda i: (0, 0))]
    in_specs += [row(a.shape[1]) for a in dys]
    args = [x, g, w, *dys]
    if has_res:
        in_specs.append(row(kin))
        args.append(resid)
    out_specs = [row(kin), pl.BlockSpec((kin, n), lambda i: (0, 0)), pl.BlockSpec((1, kin), lambda i: (0, 0))]
    out_shape = [jax.ShapeDtypeStruct((tp, kin), F32), jax.ShapeDtypeStruct((kin, n), F32),
                 jax.ShapeDtypeStruct((1, kin), F32)]
    if has_extra:
        out_specs.append(row(extra_out[0]))
        out_shape.append(jax.ShapeDtypeStruct((tp, extra_out[0]), extra_out[1]))
    return pl.pallas_call(
        body, name=name, grid=(tp // tm,),
        in_specs=in_specs, out_specs=out_specs, out_shape=out_shape,
        scratch_shapes=[pltpu.VMEM((tm, n), BF16)],
        compiler_params=_params(("arbitrary",)),
    )(*args)


def _flash_bwd(q, k, v, lse, delta, do, heads, pad, tq):
    tp = q.shape[0]
    nq = tp // tq
    scale = (QK_NOPE + QK_ROPE) ** -0.5
    c2 = scale * LOG2E

    def body(q_ref, k_ref, v_ref, lse_ref, delta_ref, do_ref, dq_ref, dk_ref, dv_ref):
        j = pl.program_id(1)

        @pl.when(j == 0)
        def _():
            dq_ref[...] = jnp.zeros_like(dq_ref)

        kv = k_ref[...]
        vv = v_ref[...]

        def rows_of(ref, i, blocks):
            parts = [ref[i + b][0:1, :] for b in range(blocks)]
            return parts[0] if blocks == 1 else jnp.concatenate(parts, axis=1)

        def make_step(masked, blocks):
            def step(i, carry):
                dk, dv = carry
                off = pl.multiple_of(i * tq, tq)
                qv = q_ref[pl.ds(off, blocks * tq), :]
                dob = do_ref[pl.ds(off, blocks * tq), :].astype(BF16)
                p = jnp.exp2(_dot_nt(kv, qv) * c2 - rows_of(lse_ref, i, blocks))
                if masked:
                    key = j * tq + lax.broadcasted_iota(jnp.int32, (tq, tq), 0)
                    qry = j * tq + lax.broadcasted_iota(jnp.int32, (tq, tq), 1)
                    first = jnp.where((key <= qry) & (key >= pad), p[:, :tq], 0.0)
                    p = first if blocks == 1 else jnp.concatenate([first, p[:, tq:]], axis=1)
                dv = dv + _dot(p.astype(BF16), dob)
                dp = _dot_nt(vv, dob)
                ds = (p * (dp - rows_of(delta_ref, i, blocks)) * scale).astype(BF16)
                dk = dk + _dot(ds, qv)
                dq_ref[pl.ds(off, blocks * tq), :] += _dot_tn(ds, kv)
                return dk, dv
            return step

        odd = (nq - j) % 2
        carry = (jnp.zeros((tq, HEAD_W), F32), jnp.zeros((tq, V_HEAD), F32))
        carry = lax.fori_loop(0, odd, lambda t, cr: make_step(True, 1)(j, cr), carry)
        carry = lax.fori_loop(0, 1 - odd, lambda t, cr: make_step(True, 2)(j, cr), carry)
        start = j + 2 - odd
        for blocks in (4, 2):
            trips = (nq - start) // blocks
            step_n = make_step(False, blocks)
            carry = lax.fori_loop(0, trips, lambda t, cr, s=start, b=blocks, f=step_n: f(s + b * t, cr), carry)
            start = start + blocks * trips
        dk, dv = carry
        dk_ref[...] = dk
        dv_ref[...] = dv

    return pl.pallas_call(
        body, name="a_flash_bwd", grid=(heads, nq),
        in_specs=[pl.BlockSpec((tp, HEAD_W), lambda h, j: (0, h)),
                  pl.BlockSpec((tq, HEAD_W), lambda h, j: (j, h)),
                  pl.BlockSpec((tq, V_HEAD), lambda h, j: (j, h)),
                  pl.BlockSpec((None, nq, SUBLANES, tq), lambda h, j: (h, 0, 0, 0)),
                  pl.BlockSpec((None, nq, SUBLANES, tq), lambda h, j: (h, 0, 0, 0)),
                  pl.BlockSpec((tp, V_HEAD), lambda h, j: (0, h))],
        out_specs=[pl.BlockSpec((tp, HEAD_W), lambda h, j: (0, h)),
                   pl.BlockSpec((tq, HEAD_W), lambda h, j: (j, h)),
                   pl.BlockSpec((tq, V_HEAD), lambda h, j: (j, h))],
        out_shape=[jax.ShapeDtypeStruct((tp, heads * HEAD_W), F32),
                   jax.ShapeDtypeStruct((tp, heads * HEAD_W), F32),
                   jax.ShapeDtypeStruct((tp, heads * V_HEAD), F32)],
        compiler_params=_params(("parallel", "arbitrary")),
    )(q, k, v, lse, delta, do)


def _position():
    return lax.axis_index("x"), lax.axis_index("y"), lax.axis_index("c")


def _other_chips(x, y):
    return [(1 - x, y), (x, 1 - y), (1 - x, 1 - y)]


def _block(ref, shard_axis, n, k, split_axis=None, m=None, h=None):
    idx = []
    for a in range(len(ref.shape)):
        start = 0
        size = None
        if a == shard_axis:
            start, size = k * n, n
        if a == split_axis:
            size = (n if a == shard_axis else m) // 2
            start = start + h * size
        idx.append(slice(None) if size is None else pl.ds(start, size))
    return ref.at[tuple(idx)]


def _gather_weights(split, whole_small):
    ns, nw = len(split), len(whole_small)
    n = ns + nw
    arrs = [s[0] for s in split] + [s[0] for s in whole_small]
    axes = [s[1] for s in split] + [s[1] for s in whole_small]

    def body(*refs):
        ins, outs = refs[:n], refs[n:2 * n]
        ici_send, ici_recv, d2d_send, d2d_recv, sib_send, sib_recv = refs[2 * n:]
        x, y, c = _position()
        me = 2 * x + y
        others = _other_chips(x, y)
        sent, local = [], []

        def remote(src, dst, sems, idx, to):
            return pltpu.make_async_remote_copy(src_ref=src, dst_ref=dst, send_sem=sems[0].at[idx],
                                                recv_sem=sems[1].at[idx], device_id=to, device_id_type=MESH)

        for a in range(n):
            width = ins[a].shape[axes[a]]
            mine = remote(ins[a], _block(outs[a], axes[a], width, me), (sib_send, sib_recv), a, (x, y, 1 - c))
            mine.start()
            local.append(mine)
            for j, (px, py) in enumerate(others):
                if a < ns:
                    sx = split[a][2]
                    src = _block(ins[a], None, None, None, sx, ins[a].shape[sx], c)
                    dst = _block(outs[a], axes[a], width, me, sx, outs[a].shape[sx], c)
                else:
                    src, dst = ins[a], _block(outs[a], axes[a], width, me)
                cp = remote(src, dst, (ici_send, ici_recv), 3 * a + j, (px, py, c))
                cp.start()
                sent.append(cp)
        for a in range(ns):
            width = ins[a].shape[axes[a]]
            sx = split[a][2]
            for j, (px, py) in enumerate(others):
                theirs = _block(outs[a], axes[a], width, 2 * px + py, sx, outs[a].shape[sx], c)
                remote(theirs, theirs, (ici_send, ici_recv), 3 * a + j, (px, py, c)).wait_recv()
                fwd = remote(theirs, theirs, (d2d_send, d2d_recv), 3 * a + j, (x, y, 1 - c))
                fwd.start()
                sent.append(fwd)
        for a in range(ns, n):
            width = ins[a].shape[axes[a]]
            for j, (px, py) in enumerate(others):
                theirs = _block(outs[a], axes[a], width, 2 * px + py)
                remote(theirs, theirs, (ici_send, ici_recv), 3 * a + j, (px, py, c)).wait_recv()
        for a in range(ns):
            width = ins[a].shape[axes[a]]
            sx = split[a][2]
            for j, (px, py) in enumerate(others):
                from_sibling = _block(outs[a], axes[a], width, 2 * px + py, sx, outs[a].shape[sx], 1 - c)
                remote(from_sibling, from_sibling, (d2d_send, d2d_recv), 3 * a + j, (x, y, 1 - c)).wait_recv()
        for cp in sent:
            cp.wait_send()
        for cp in local:
            cp.wait()

    def whole_shape(a, axis):
        shape = list(a.shape)
        shape[axis] *= N_CHIPS
        return jax.ShapeDtypeStruct(tuple(shape), a.dtype)

    any_spec = pl.BlockSpec(memory_space=pl.ANY)
    return pl.pallas_call(
        body, name="gather_weights",
        in_specs=[any_spec] * n, out_specs=[any_spec] * n,
        out_shape=[whole_shape(a, ax) for a, ax in zip(arrs, axes)],
        scratch_shapes=[pltpu.SemaphoreType.DMA((3 * n,)), pltpu.SemaphoreType.DMA((3 * n,)),
                        pltpu.SemaphoreType.DMA((3 * ns,)), pltpu.SemaphoreType.DMA((3 * ns,)),
                        pltpu.SemaphoreType.DMA((n,)), pltpu.SemaphoreType.DMA((n,))],
        compiler_params=pltpu.CompilerParams(has_side_effects=True),
    )(*arrs)


class _Grad:
    def __init__(self, name, g, kind, rh, cols, groups=None):
        self.name, self.g, self.kind, self.rh, self.cols, self.groups = name, g, kind, rh, cols, groups
        if kind == 'rows':
            self.tr = rh
        elif kind == 'gate':
            self.tr = rh // (groups // 2)
        else:
            self.tr = 256 if rh % 256 == 0 else rh
        self.nb = rh // self.tr

    def pieces(self, ref, k, h):
        rh, cols = self.rh, self.cols
        if self.kind == 'cols':
            return [(ref.at[pl.ds(h * rh, rh), pl.ds(k * cols, cols)], 0, rh)]
        if self.kind == 'rows':
            return [(ref.at[pl.ds((2 * k + h) * rh, rh), :], 0, rh)]
        if self.kind == 'lead':
            return [(ref.at[k, pl.ds(h * rh, rh), :], 0, rh)]
        per = self.groups // 2
        return [(ref.at[pl.ds((((h * per + gi) * N_CHIPS) + k) * self.tr, self.tr), :], gi * self.tr, self.tr)
                for gi in range(per)]

    def block_spec(self):
        tr, nb, cols = self.tr, self.nb, self.cols
        if self.kind == 'cols':
            return pl.BlockSpec((tr, cols), lambda k, i, c: (c[0] * nb + i, k))
        if self.kind == 'rows':
            return pl.BlockSpec((tr, cols), lambda k, i, c: (2 * k + c[0], 0))
        if self.kind == 'lead':
            return pl.BlockSpec((None, tr, cols), lambda k, i, c: (k, c[0] * nb + i, 0))
        return pl.BlockSpec((tr, cols), lambda k, i, c: ((c[0] * nb + i) * N_CHIPS + k, 0))


def _halves_to_sibling(grads):
    n = len(grads)
    total = sum(len(gr.pieces(gr.g, 0, 0)) * N_CHIPS for gr in grads)

    def body(*refs):
        ins, outs = refs[:n], refs[n:2 * n]
        send_sems, recv_sems = refs[2 * n:]
        x, y, c = _position()
        copies = []
        for gr, g_ref, got_ref in zip(grads, ins, outs):
            for k in range(N_CHIPS):
                for src, r0, nr in gr.pieces(g_ref, k, 1 - c):
                    idx = len(copies)
                    cp = pltpu.make_async_remote_copy(
                        src_ref=src, dst_ref=got_ref.at[k, pl.ds(r0, nr), :],
                        send_sem=send_sems.at[idx], recv_sem=recv_sems.at[idx],
                        device_id=(x, y, 1 - c), device_id_type=MESH)
                    cp.start()
                    copies.append(cp)
        for cp in copies:
            cp.wait()

    any_spec = pl.BlockSpec(memory_space=pl.ANY)
    return pl.pallas_call(
        body, name="grads_to_sibling",
        in_specs=[any_spec] * n, out_specs=[any_spec] * n,
        out_shape=[jax.ShapeDtypeStruct((N_CHIPS, gr.rh, gr.cols), F32) for gr in grads],
        scratch_shapes=[pltpu.SemaphoreType.DMA((total,)), pltpu.SemaphoreType.DMA((total,))],
        compiler_params=pltpu.CompilerParams(has_side_effects=True),
    )(*[gr.g for gr in grads])


def _chip_sum(gr, got, c):
    def body(c_ref, g_ref, got_ref, o_ref):
        o_ref[...] = (g_ref[...] + got_ref[...]).astype(BF16)

    tile = pl.BlockSpec((None, gr.tr, gr.cols), lambda k, i, c_ref: (k, i, 0))
    return pl.pallas_call(
        body, name="chip_sum_" + gr.name,
        grid_spec=pltpu.PrefetchScalarGridSpec(
            num_scalar_prefetch=1, grid=(N_CHIPS, gr.nb),
            in_specs=[gr.block_spec(), tile], out_specs=tile),
        out_shape=jax.ShapeDtypeStruct((N_CHIPS, gr.rh, gr.cols), BF16),
        compiler_params=_params(("parallel", "parallel")),
    )(c, gr.g, got)


def _blocks_to_chips(parts):
    n = len(parts)

    def body(*refs):
        ins, outs = refs[:n], refs[n:2 * n]
        send_sems, recv_sems = refs[2 * n:]
        x, y, c = _position()
        copies = []
        for a in range(n):
            for j, (px, py) in enumerate(_other_chips(x, y)):
                cp = pltpu.make_async_remote_copy(
                    src_ref=ins[a].at[2 * px + py], dst_ref=outs[a].at[j],
                    send_sem=send_sems.at[3 * a + j], recv_sem=recv_sems.at[3 * a + j],
                    device_id=(px, py, c), device_id_type=MESH)
                cp.start()
                copies.append(cp)
        for cp in copies:
            cp.wait()

    any_spec = pl.BlockSpec(memory_space=pl.ANY)
    return pl.pallas_call(
        body, name="grads_to_chips",
        in_specs=[any_spec] * n, out_specs=[any_spec] * n,
        out_shape=[jax.ShapeDtypeStruct((3,) + p.shape[1:], p.dtype) for p in parts],
        scratch_shapes=[pltpu.SemaphoreType.DMA((3 * n,)), pltpu.SemaphoreType.DMA((3 * n,))],
        compiler_params=pltpu.CompilerParams(has_side_effects=True),
    )(*parts)


def _sum_chips(name, part, got, me, c):
    nk, rh, cols = part.shape
    tr = 256 if rh % 256 == 0 else rh
    nb = rh // tr

    def body(me_ref, c_ref, own_ref, *rest):
        got_refs, o_ref = rest[:nk], rest[nk]
        own = own_ref[...].astype(F32)
        acc = None
        for k in range(nk):
            term = jnp.where(me_ref[0] == k, own, got_refs[k][...].astype(F32))
            acc = term if acc is None else acc + term
        o_ref[...] = acc

    def got_map(k):
        def index(i, me_ref, c_ref):
            xor = jnp.bitwise_xor(me_ref[0], k)
            slot = jnp.where(xor == 1, 1, jnp.where(xor == 3, 2, 0))
            return (slot, i, 0)
        return index

    return pl.pallas_call(
        body, name="sum_" + name,
        grid_spec=pltpu.PrefetchScalarGridSpec(
            num_scalar_prefetch=2, grid=(nb,),
            in_specs=[pl.BlockSpec((None, tr, cols), lambda i, me_ref, c_ref: (me_ref[0], i, 0))]
            + [pl.BlockSpec((None, tr, cols), got_map(k)) for k in range(nk)],
            out_specs=pl.BlockSpec((tr, cols), lambda i, me_ref, c_ref: (c_ref[0] * nb + i, 0))),
        out_shape=jax.ShapeDtypeStruct((2 * rh, cols), F32),
        compiler_params=_params(("parallel",)),
    )(me, c, part, *([got] * nk))


def _share_with_sibling(halves):
    n = len(halves)

    def body(*refs):
        outs = refs[n:2 * n]
        send_sems, recv_sems = refs[2 * n:]
        x, y, c = _position()
        copies = []
        for a in range(n):
            rh = outs[a].shape[0] // 2
            mine = outs[a].at[pl.ds(c * rh, rh), :]
            cp = pltpu.make_async_remote_copy(
                src_ref=mine, dst_ref=mine, send_sem=send_sems.at[a], recv_sem=recv_sems.at[a],
                device_id=(x, y, 1 - c), device_id_type=MESH)
            cp.start()
            copies.append(cp)
        for cp in copies:
            cp.wait()

    any_spec = pl.BlockSpec(memory_space=pl.ANY)
    return pl.pallas_call(
        body, name="grads_share",
        in_specs=[any_spec] * n, out_specs=[any_spec] * n,
        out_shape=[jax.ShapeDtypeStruct(h.shape, h.dtype) for h in halves],
        input_output_aliases={a: a for a in range(n)},
        scratch_shapes=[pltpu.SemaphoreType.DMA((n,)), pltpu.SemaphoreType.DMA((n,))],
        compiler_params=pltpu.CompilerParams(has_side_effects=True),
    )(*halves)


def _adamw(name, w, g, m, v):
    rows, cols = w.shape
    tr = 256 if rows % 256 == 0 else rows

    def body(w_ref, g_ref, m_ref, v_ref, d_ref, nm_ref, nv_ref):
        gv = g_ref[...]
        mn = ADAM_B1 * m_ref[...] + (1.0 - ADAM_B1) * gv
        vn = ADAM_B2 * v_ref[...] + (1.0 - ADAM_B2) * (gv * gv)
        m_hat = mn / (1.0 - ADAM_B1 ** ADAM_STEP)
        v_hat = vn / (1.0 - ADAM_B2 ** ADAM_STEP)
        d_ref[...] = -ADAM_LR * (m_hat / (jnp.sqrt(v_hat) + ADAM_EPS) + ADAM_WD * w_ref[...])
        nm_ref[...] = mn
        nv_ref[...] = vn

    spec = pl.BlockSpec((tr, cols), lambda i: (i, 0))
    return pl.pallas_call(
        body, name=name, grid=(rows // tr,),
        in_specs=[spec] * 4, out_specs=[spec] * 3,
        out_shape=[jax.ShapeDtypeStruct((rows, cols), F32)] * 3,
        compiler_params=_params(("parallel",)),
    )(w, g, m, v)


def _as2d(a):
    if a.ndim == 1:
        return a.reshape(1, -1)
    return a.reshape(-1, a.shape[-1])


def _unshard(gathered, axis):
    moved = jnp.moveaxis(gathered, 0, axis)
    shape = list(gathered.shape[1:])
    shape[axis] *= N_CHIPS
    return moved.reshape(shape)


def _rope_tables(tp, pad):
    pos = jnp.arange(tp, dtype=F32) - pad
    inv_freq = ROPE_BASE ** (-jnp.arange(0, QK_ROPE, 2, dtype=F32) / QK_ROPE)
    ang = pos[:, None] * inv_freq[None, :]
    cos, sin = jnp.cos(ang), jnp.sin(ang)
    zeros = jnp.zeros((tp, LANES - QK_ROPE), F32)
    return jnp.concatenate([cos, cos, zeros], axis=1), jnp.concatenate([-sin, sin, zeros], axis=1)


def _local_grads(x, target, wt, heads):
    seq, d = x.shape
    n_meta = wt['meta_tokens'].shape[0]
    t = seq + n_meta
    pad = (-t) % Q_BLOCK
    tp = t + pad
    x0 = pad + n_meta
    tm = _row_tile(tp)
    ql = wt['a_q_norm_g'].shape[1]
    kvl = wt['a_kv_norm_g'].shape[1]
    mla_w = heads * V_HEAD
    lru_w = wt['b_conv_w'].shape[1]

    h0 = jnp.concatenate([jnp.zeros((pad, d), F32), wt['meta_tokens'], x], axis=0)
    cos_t, sin_t = _rope_tables(tp, pad)

    w_in_a = wt['a_w_in']
    zcol = jnp.zeros((d, LANES - QK_ROPE), BF16)
    w_in_a = jnp.concatenate([w_in_a[:, :ql + kvl + QK_ROPE], zcol, w_in_a[:, ql + kvl + QK_ROPE:]], axis=1)
    c_kv, c_kr, c_gate = ql, ql + kvl, ql + kvl + LANES
    splits_a = [(0, c_kv), (c_kv, c_kr), (c_kr, c_gate), (c_gate, c_gate + mla_w)]
    w_uq = wt['a_w_uq'].reshape(ql, heads, QK_NOPE + QK_ROPE)
    w_uq = jnp.pad(w_uq, ((0, 0), (0, 0), (0, HEAD_W - QK_NOPE - QK_ROPE))).reshape(ql, heads * HEAD_W)
    w_ukv = wt['a_w_ukv']

    q_lat, kv_lat, kr_raw, gate_a = _norm_matmul("a_in_proj", h0, wt['a_norm_g'], w_in_a, splits_a, tm)
    q = _q_proj(q_lat, wt['a_q_norm_g'], w_uq, cos_t, sin_t, heads, tm)
    k, v = _kv_proj(kv_lat, wt['a_kv_norm_g'], w_ukv, kr_raw, cos_t, sin_t, heads, tm)
    attn, lse = _flash_fwd(q, k, v, heads, pad, tm)
    h1 = _gated_out("a_out_proj", attn, gate_a, wt['a_w_out'], h0, tm)

    u, gate_b = _norm_matmul("b_in_proj", h1, wt['b_norm_g'], wt['b_w_in'], [(0, lru_w), (lru_w, 2 * lru_w)], tm)
    uc, r, ig, hs = _rglru_fwd(u, wt['b_conv_w'], wt['b_conv_b'], wt['b_w_rg'], wt['b_b_rg'],
                               wt['b_w_ig'], wt['b_b_ig'], wt['b_lam'], pad, tm)
    h2 = _gated_out("b_out_proj", hs, gate_b, wt['b_w_out'], h1, tm)

    dh2, loss, d_final_g = _final_loss(h2, wt['final_norm_g'], target, x0)

    dhs, dgate_b, dw_out_b = _gated_out_bwd("b_out_proj_bwd", dh2, hs, gate_b, wt['b_w_out'], tm)
    du, dconv_w, dconv_b, dw_rg, db_rg, dw_ig, db_ig, dlam = _rglru_bwd(
        dhs, hs, r, ig, uc, u, wt['b_conv_w'], wt['b_w_rg'], wt['b_w_ig'], wt['b_lam'], pad, tm)
    dh1, dw_in_b, dg_b = _norm_matmul_bwd("b_in_proj_bwd", h1, wt['b_norm_g'], wt['b_w_in'], [du, dgate_b], tm, resid=dh2)

    dattn, dgate_a, dw_out_a, delta = _gated_out_bwd("a_out_proj_bwd", dh1, attn, gate_a, wt['a_w_out'], tm,
                                                     delta_heads=heads)
    dq, dk, dv = _flash_bwd(q, k, v, lse, delta, dattn, heads, pad, tm)

    def q_prologue(dy_refs, dy_s, ex_ref):
        (dq_ref,), cos_v, sin_v = dy_refs[:1], dy_refs[1][...], dy_refs[2][...]
        for h in range(heads):
            c0 = h * HEAD_W
            dy_s[:, c0:c0 + QK_NOPE] = dq_ref[:, c0:c0 + QK_NOPE].astype(BF16)
            dy_s[:, c0 + QK_NOPE:c0 + HEAD_W] = _unrope(dq_ref[:, c0 + QK_NOPE:c0 + HEAD_W], cos_v, sin_v).astype(BF16)

    dq_lat, dw_uq, dg_q = _norm_matmul_bwd("a_q_proj_bwd", q_lat, wt['a_q_norm_g'], w_uq, [dq, cos_t, sin_t], tm,
                                           prologue=q_prologue)

    def kv_prologue(dy_refs, dy_s, ex_ref):
        dk_ref, dv_ref = dy_refs[:2]
        cos_v, sin_v = dy_refs[2][...], dy_refs[3][...]
        dkr = jnp.zeros((dk_ref.shape[0], LANES), F32)
        for h in range(heads):
            c0 = h * (QK_NOPE + V_HEAD)
            dy_s[:, c0:c0 + QK_NOPE] = dk_ref[:, h * HEAD_W:h * HEAD_W + QK_NOPE].astype(BF16)
            dy_s[:, c0 + QK_NOPE:c0 + QK_NOPE + V_HEAD] = dv_ref[:, h * V_HEAD:(h + 1) * V_HEAD].astype(BF16)
            dkr = dkr + dk_ref[:, h * HEAD_W + QK_NOPE:(h + 1) * HEAD_W]
        ex_ref[...] = _unrope(dkr, cos_v, sin_v)

    dkv_lat, dw_ukv, dg_kv, dkr_raw = _norm_matmul_bwd(
        "a_kv_proj_bwd", kv_lat, wt['a_kv_norm_g'], w_ukv, [dk, dv, cos_t, sin_t], tm,
        prologue=kv_prologue, extra_out=(LANES, F32))

    dh0, dw_in_a, dg_a = _norm_matmul_bwd("a_in_proj_bwd", h0, wt['a_norm_g'], w_in_a,
                                          [dq_lat, dkv_lat, dkr_raw, dgate_a], tm, resid=dh1)

    dw_in_a = jnp.concatenate([dw_in_a[:, :c_kr + QK_ROPE], dw_in_a[:, c_gate:]], axis=1)
    grads = {
        'meta_tokens': dh0[pad:x0], 'a_norm_g': dg_a, 'a_w_in': dw_in_a, 'a_q_norm_g': dg_q, 'a_kv_norm_g': dg_kv,
        'a_w_uq': dw_uq, 'a_w_ukv': dw_ukv, 'a_w_out': dw_out_a, 'b_norm_g': dg_b, 'b_w_in': dw_in_b,
        'b_conv_w': dconv_w, 'b_conv_b': dconv_b, 'b_w_rg': dw_rg, 'b_b_rg': db_rg, 'b_w_ig': dw_ig,
        'b_b_ig': db_ig, 'b_lam': dlam, 'b_w_out': dw_out_b, 'final_norm_g': d_final_g,
    }
    return loss, dh0[x0:], grads


def _chip_major(whole, local_shape, axis):
    if axis is None:
        return jnp.broadcast_to(whole.reshape(1, -1), (N_CHIPS, whole.size))
    shape = list(local_shape)
    g = whole.reshape(shape[:axis] + [N_CHIPS, shape[axis]] + shape[axis + 1:])
    return jnp.moveaxis(g, axis, 0).reshape(N_CHIPS, -1)


def kernel(x, meta_tokens, a_norm_g, a_w_in, a_q_norm_g, a_kv_norm_g, a_w_uq, a_w_ukv, a_w_out, b_norm_g, b_w_in, b_conv_w, b_conv_b, b_w_rg, b_b_rg, b_w_ig, b_b_ig, b_lam, b_w_out, final_norm_g, loss_target, m_meta_tokens, m_a_norm_g, m_a_w_in, m_a_q_norm_g, m_a_kv_norm_g, m_a_w_uq, m_a_w_ukv, m_a_w_out, m_b_norm_g, m_b_w_in, m_b_conv_w, m_b_conv_b, m_b_w_rg, m_b_b_rg, m_b_w_ig, m_b_b_ig, m_b_lam, m_b_w_out, m_final_norm_g, v_meta_tokens, v_a_norm_g, v_a_w_in, v_a_q_norm_g, v_a_kv_norm_g, v_a_w_uq, v_a_w_ukv, v_a_w_out, v_b_norm_g, v_b_w_in, v_b_conv_w, v_b_conv_b, v_b_w_rg, v_b_b_rg, v_b_w_ig, v_b_b_ig, v_b_lam, v_b_w_out, v_final_norm_g):
    local_w = dict(zip(WEIGHTS, (meta_tokens, a_norm_g, a_w_in, a_q_norm_g, a_kv_norm_g, a_w_uq, a_w_ukv, a_w_out,
                                 b_norm_g, b_w_in, b_conv_w, b_conv_b, b_w_rg, b_b_rg, b_w_ig, b_b_ig, b_lam,
                                 b_w_out, final_norm_g)))
    local_m = dict(zip(WEIGHTS, (m_meta_tokens, m_a_norm_g, m_a_w_in, m_a_q_norm_g, m_a_kv_norm_g, m_a_w_uq,
                                 m_a_w_ukv, m_a_w_out, m_b_norm_g, m_b_w_in, m_b_conv_w, m_b_conv_b, m_b_w_rg,
                                 m_b_b_rg, m_b_w_ig, m_b_b_ig, m_b_lam, m_b_w_out, m_final_norm_g)))
    local_v = dict(zip(WEIGHTS, (v_meta_tokens, v_a_norm_g, v_a_w_in, v_a_q_norm_g, v_a_kv_norm_g, v_a_w_uq,
                                 v_a_w_ukv, v_a_w_out, v_b_norm_g, v_b_w_in, v_b_conv_w, v_b_conv_b, v_b_w_rg,
                                 v_b_b_rg, v_b_w_ig, v_b_b_ig, v_b_lam, v_b_w_out, v_final_norm_g)))
    matrices = ('a_w_in', 'a_w_uq', 'a_w_ukv', 'a_w_out', 'b_w_in', 'b_w_rg', 'b_w_ig', 'b_w_out')
    heads = a_w_uq.shape[-1] * N_CHIPS // (QK_NOPE + QK_ROPE)

    split, small = [], []
    for n in WEIGHTS:
        if SHARD_AXIS[n] is None:
            continue
        if n == 'a_w_in':
            split.append((n, local_w[n].astype(BF16)[None], 0, 2))
        elif n in matrices:
            split.append((n, local_w[n].astype(BF16), SHARD_AXIS[n], 1))
        else:
            small.append((n, local_w[n], SHARD_AXIS[n]))
    gathered = _gather_weights([s[1:] for s in split], [s[1:] for s in small])
    whole = dict(zip([s[0] for s in split + small], gathered))
    whole['a_w_in'] = _unshard(whole['a_w_in'], SHARD_AXIS['a_w_in'])
    wt = {}
    for n in WEIGHTS:
        w = whole.get(n, local_w[n])
        wt[n] = w[0] if n in ('b_w_rg', 'b_w_ig', 'b_conv_w') else _as2d(w)

    loss, grad_x, grads = _local_grads(x[0], loss_target[0], wt, heads)

    def shard2d(n):
        s = local_w[n].shape
        return (s[-3] * s[-2], s[-1]) if n in ('b_w_rg', 'b_w_ig') else (s[-2], s[-1])

    ext_uq = heads % N_CHIPS == 0
    exchange = []
    for n in matrices:
        rows, cols = shard2d(n)
        g = grads[n]
        if n == 'a_w_in' or (n == 'a_w_uq' and not ext_uq):
            if n == 'a_w_uq':
                g = g.reshape(g.shape[0], heads, HEAD_W)[:, :, :QK_NOPE + QK_ROPE].reshape(g.shape[0], -1)
            exchange.append(_Grad(n, _chip_major(g, local_w[n].shape, SHARD_AXIS[n]).reshape(N_CHIPS, rows, cols),
                                  'lead', rows // 2, cols))
        elif n == 'a_w_uq':
            exchange.append(_Grad(n, g, 'cols', rows // 2, g.shape[1] // N_CHIPS))
        elif n in ('b_w_rg', 'b_w_ig'):
            groups = g.shape[0]
            exchange.append(_Grad(n, g.reshape(-1, cols), 'gate', rows // 2, cols, groups))
        elif SHARD_AXIS[n] == 2:
            exchange.append(_Grad(n, g, 'cols', rows // 2, cols))
        else:
            exchange.append(_Grad(n, g, 'rows', rows // 2, cols))
    rest = [n for n in WEIGHTS if n not in matrices]
    pieces = [_chip_major(grads[n], local_w[n].shape, SHARD_AXIS[n]) for n in rest]
    length = sum(p.shape[1] for p in pieces)
    unit = 2 * SUBLANES * 1024
    padded = -(-length // unit) * unit
    flat = jnp.concatenate(pieces + [jnp.zeros((N_CHIPS, padded - length), F32)], axis=1)
    exchange.append(_Grad('small', flat.reshape(N_CHIPS, padded // 1024, 1024), 'lead', padded // 2048, 1024))

    c_idx = lax.axis_index("c").astype(jnp.int32).reshape(1)
    me_idx = (2 * lax.axis_index("x") + lax.axis_index("y")).astype(jnp.int32).reshape(1)
    got = _halves_to_sibling(exchange)
    chip_sums = [_chip_sum(gr, r, c_idx) for gr, r in zip(exchange, got)]
    from_chips = _blocks_to_chips(chip_sums)
    halves = [_sum_chips(gr.name, p, r, me_idx, c_idx) for gr, p, r in zip(exchange, chip_sums, from_chips)]
    summed = dict(zip([gr.name for gr in exchange], _share_with_sibling(halves)))
    if ext_uq:
        g = summed['a_w_uq']
        summed['a_w_uq'] = g.reshape(g.shape[0], -1, HEAD_W)[:, :, :QK_NOPE + QK_ROPE]
    total = summed['small'].reshape(-1)

    out_g, out_d, out_m, out_v = [], [], [], []
    off = 0
    for n in WEIGHTS:
        shape = local_w[n].shape
        if n in matrices:
            g = summed[n].reshape(shape)
        else:
            size = 1
            for s in shape:
                size *= s
            g = total[off:off + size].reshape(shape)
            off += size
        delta, new_m, new_v = _adamw("adamw_" + n, _as2d(local_w[n]), _as2d(g), _as2d(local_m[n]), _as2d(local_v[n]))
        out_g.append(g)
        out_d.append(delta.reshape(shape))
        out_m.append(new_m.reshape(shape))
        out_v.append(new_v.reshape(shape))

    loss = lax.psum(loss[0, 0], ("x", "y", "c"))
    return (loss, grad_x[None], *out_g, *out_d, *out_m, *out_v)
```

```python
import functools

import jax
import jax.numpy as jnp
from jax import lax
from jax.experimental import pallas as pl
from jax.experimental.pallas import tpu as pltpu

F32 = jnp.float32
BF16 = jnp.bfloat16
MESH = pl.DeviceIdType.MESH

RMS_EPS = 1e-6
QK_NOPE = 128
QK_ROPE = 64
V_HEAD = 128
HEAD_W = 256
ROPE_BASE = 10000.0
Q_BLOCK = 128
MASK_VALUE = -1e30
CONV_WIDTH = 4
LRU_C = 8.0
N_CHIPS = 4

ADAM_LR = 0.001
ADAM_B1 = 0.9
ADAM_B2 = 0.999
ADAM_EPS = 1e-08
ADAM_WD = 0.01
ADAM_STEP = 10

VMEM_LIMIT_V7X = 56 * 1024 * 1024
LANES = 128
SUBLANES = 8

WEIGHTS = ['meta_tokens', 'a_norm_g', 'a_w_in', 'a_q_norm_g', 'a_kv_norm_g', 'a_w_uq', 'a_w_ukv',
           'a_w_out', 'b_norm_g', 'b_w_in', 'b_conv_w', 'b_conv_b', 'b_w_rg', 'b_b_rg', 'b_w_ig',
           'b_b_ig', 'b_lam', 'b_w_out', 'final_norm_g']
SHARD_AXIS = {'meta_tokens': 1, 'a_norm_g': None, 'a_w_in': 2, 'a_q_norm_g': None, 'a_kv_norm_g': None,
              'a_w_uq': 2, 'a_w_ukv': 2, 'a_w_out': 1, 'b_norm_g': 1, 'b_w_in': 2, 'b_conv_w': 2,
              'b_conv_b': 1, 'b_w_rg': 2, 'b_b_rg': 1, 'b_w_ig': 2, 'b_b_ig': 1, 'b_lam': 1,
              'b_w_out': 1, 'final_norm_g': None}


def _params(sem=None):
    return pltpu.CompilerParams(dimension_semantics=sem, vmem_limit_bytes=VMEM_LIMIT_V7X)


def _row_tile(tp):
    return 384 if (tp % 384 == 0 and tp >= 1152) else 128


def _sigmoid(x):
    return 1.0 / (1.0 + jnp.exp(-x))


def _rms(x):
    return lax.rsqrt(jnp.mean(x * x, axis=-1, keepdims=True) + RMS_EPS)


def _swap32(x):
    lane = lax.broadcasted_iota(jnp.int32, x.shape, 1)
    return jnp.where(lane < 32, pltpu.roll(x, 96, 1), pltpu.roll(x, 32, 1))


def _rope(x, cos_t, sin_t):
    return x * cos_t + _swap32(x) * sin_t


def _unrope(d, cos_t, sin_t):
    lane = lax.broadcasted_iota(jnp.int32, d.shape, 1)
    return jnp.where(lane < QK_ROPE, d * cos_t + _swap32(d * sin_t), 0.0)


def _dot(a, b):
    return jnp.dot(a, b, preferred_element_type=F32)


def _dot_nt(a, b):
    return lax.dot_general(a, b, (((1,), (1,)), ((), ())), preferred_element_type=F32)


def _dot_tn(a, b):
    return lax.dot_general(a, b, (((0,), (0,)), ((), ())), preferred_element_type=F32)


def _norm_matmul(name, x, g, w, splits, tm):
    tp, kin = x.shape
    n = w.shape[1]

    def body(x_ref, g_ref, w_ref, *outs):
        xv = x_ref[...]
        nrm = ((xv * _rms(xv)) * g_ref[...]).astype(BF16)
        y = _dot(nrm, w_ref[...])
        for o_ref, (c0, c1) in zip(outs, splits):
            o_ref[...] = y[:, c0:c1]

    return pl.pallas_call(
        body, name=name, grid=(tp // tm,),
        in_specs=[pl.BlockSpec((tm, kin), lambda i: (i, 0)),
                  pl.BlockSpec((1, kin), lambda i: (0, 0)),
                  pl.BlockSpec((kin, n), lambda i: (0, 0))],
        out_specs=[pl.BlockSpec((tm, c1 - c0), lambda i: (i, 0)) for c0, c1 in splits],
        out_shape=[jax.ShapeDtypeStruct((tp, c1 - c0), F32) for c0, c1 in splits],
        compiler_params=_params(("parallel",)),
    )(x, g, w)


def _q_proj(q_lat, g, w_uq, cos_t, sin_t, heads, tm):
    tp, kin = q_lat.shape
    n = heads * HEAD_W

    def body(x_ref, g_ref, w_ref, cos_ref, sin_ref, q_ref):
        xv = x_ref[...]
        nrm = ((xv * _rms(xv)) * g_ref[...]).astype(BF16)
        y = _dot(nrm, w_ref[...])
        cos_v, sin_v = cos_ref[...], sin_ref[...]
        for h in range(heads):
            c0 = h * HEAD_W
            q_ref[:, c0:c0 + QK_NOPE] = y[:, c0:c0 + QK_NOPE].astype(BF16)
            q_ref[:, c0 + QK_NOPE:c0 + HEAD_W] = _rope(y[:, c0 + QK_NOPE:c0 + HEAD_W], cos_v, sin_v).astype(BF16)

    return pl.pallas_call(
        body, name="a_q_proj", grid=(tp // tm,),
        in_specs=[pl.BlockSpec((tm, kin), lambda i: (i, 0)),
                  pl.BlockSpec((1, kin), lambda i: (0, 0)),
                  pl.BlockSpec((kin, n), lambda i: (0, 0)),
                  pl.BlockSpec((tm, LANES), lambda i: (i, 0)),
                  pl.BlockSpec((tm, LANES), lambda i: (i, 0))],
        out_specs=pl.BlockSpec((tm, n), lambda i: (i, 0)),
        out_shape=jax.ShapeDtypeStruct((tp, n), BF16),
        compiler_params=_params(("parallel",)),
    )(q_lat, g, w_uq, cos_t, sin_t)


def _kv_proj(kv_lat, g, w_ukv, k_rope_raw, cos_t, sin_t, heads, tm):
    tp, kin = kv_lat.shape
    n = heads * (QK_NOPE + V_HEAD)

    def body(x_ref, g_ref, w_ref, kr_ref, cos_ref, sin_ref, k_ref, v_ref):
        xv = x_ref[...]
        nrm = ((xv * _rms(xv)) * g_ref[...]).astype(BF16)
        y = _dot(nrm, w_ref[...])
        kr = _rope(kr_ref[...], cos_ref[...], sin_ref[...]).astype(BF16)
        for h in range(heads):
            c0 = h * (QK_NOPE + V_HEAD)
            k_ref[:, h * HEAD_W:h * HEAD_W + QK_NOPE] = y[:, c0:c0 + QK_NOPE].astype(BF16)
            k_ref[:, h * HEAD_W + QK_NOPE:(h + 1) * HEAD_W] = kr
            v_ref[:, h * V_HEAD:(h + 1) * V_HEAD] = y[:, c0 + QK_NOPE:c0 + QK_NOPE + V_HEAD].astype(BF16)

    return pl.pallas_call(
        body, name="a_kv_proj", grid=(tp // tm,),
        in_specs=[pl.BlockSpec((tm, kin), lambda i: (i, 0)),
                  pl.BlockSpec((1, kin), lambda i: (0, 0)),
                  pl.BlockSpec((kin, n), lambda i: (0, 0)),
                  pl.BlockSpec((tm, LANES), lambda i: (i, 0)),
                  pl.BlockSpec((tm, LANES), lambda i: (i, 0)),
                  pl.BlockSpec((tm, LANES), lambda i: (i, 0))],
        out_specs=[pl.BlockSpec((tm, heads * HEAD_W), lambda i: (i, 0)),
                   pl.BlockSpec((tm, heads * V_HEAD), lambda i: (i, 0))],
        out_shape=[jax.ShapeDtypeStruct((tp, heads * HEAD_W), BF16),
                   jax.ShapeDtypeStruct((tp, heads * V_HEAD), BF16)],
        compiler_params=_params(("parallel",)),
    )(kv_lat, g, w_ukv, k_rope_raw, cos_t, sin_t)


def _as_rows(col):
    rows = col.shape[0]
    return jnp.transpose(jnp.broadcast_to(col, (rows, LANES)))[0:SUBLANES, :]


def _attn_mask(row0, col0, rows, cols, pad):
    row = row0 + lax.broadcasted_iota(jnp.int32, (rows, cols), 0)
    col = col0 + lax.broadcasted_iota(jnp.int32, (rows, cols), 1)
    return (col <= row) & (col >= pad)


LOG2E = 1.4426950408889634
FLASH_FWD_TRIPS = ((4, 2), (2, 2), (1, 1))


def _flash_fwd(q, k, v, heads, pad, tq, exchange=None):
    tp = q.shape[0]
    nq = tp // tq
    c2 = (QK_NOPE + QK_ROPE) ** -0.5 * LOG2E

    def compute(q_ref, k_ref, v_ref, o_ref, lse_ref):
        i = pl.program_id(1)

        def make_step(masked, blocks, parts=1):
            keys = blocks * tq // parts

            def step(j, carry):
                m, l, acc = carry
                offs = [pl.multiple_of(j * tq + part * keys, tq) for part in range(parts)]
                scores = [_dot_nt(q_ref[...], k_ref[pl.ds(off, keys), :]) for off in offs]
                for off, s in zip(offs, scores):
                    s = s * c2
                    if masked:
                        s = jnp.where(_attn_mask(i * tq, j * tq, tq, keys, pad), s, MASK_VALUE)
                    m_new = jnp.maximum(m, jnp.max(s, axis=-1, keepdims=True))
                    p = jnp.exp2(s - m_new)
                    alpha = jnp.exp2(m - m_new)
                    l = alpha * l + jnp.sum(p, axis=-1, keepdims=True)
                    acc = alpha * acc + _dot(p.astype(BF16), v_ref[pl.ds(off, keys), :])
                    m = m_new
                return m, l, acc
            return step

        init = (jnp.full((tq, 1), MASK_VALUE, F32), jnp.zeros((tq, 1), F32), jnp.zeros((tq, V_HEAD), F32))
        carry = make_step(True, 1)(0, init)
        first = 1
        for blocks, parts in FLASH_FWD_TRIPS:
            trips = jnp.maximum(i - first, 0) // blocks
            step_n = make_step(False, blocks, parts)
            carry = lax.fori_loop(0, trips, lambda t, cr, f=first, b=blocks, s=step_n: s(f + b * t, cr), carry)
            first = first + blocks * trips
        m, l, acc = lax.fori_loop(jnp.maximum(i, 1), i + 1, make_step(True, 1), carry)
        o_ref[...] = acc / l
        lse_ref[...] = _as_rows(m + jnp.log(l) * LOG2E)

    in_specs = [pl.BlockSpec((tq, HEAD_W), lambda h, i: (i, h)),
                pl.BlockSpec((tp, HEAD_W), lambda h, i: (0, h)),
                pl.BlockSpec((tp, V_HEAD), lambda h, i: (0, h))]
    out_specs = [pl.BlockSpec((tq, V_HEAD), lambda h, i: (i, h)),
                 pl.BlockSpec((None, None, SUBLANES, tq), lambda h, i: (h, i, 0, 0))]
    out_shape = [jax.ShapeDtypeStruct((tp, heads * V_HEAD), F32),
                 jax.ShapeDtypeStruct((heads, nq, SUBLANES, tq), F32)]
    args, scratch = [q, k, v], []
    body = _with_exchange(exchange, len(in_specs), len(out_specs),
                          lambda: (pl.program_id(0) == 0) & (pl.program_id(1) == 0),
                          lambda: (pl.program_id(0) == heads - 1) & (pl.program_id(1) == nq - 1), compute)
    if exchange is not None:
        in_specs = in_specs + exchange.specs(exchange.n_in)
        out_specs = out_specs + exchange.specs(exchange.n_out)
        out_shape = out_shape + exchange.out_shapes
        args, scratch = args + exchange.arrays, exchange.scratch()
    return pl.pallas_call(
        body, name="a_flash_fwd", grid=(heads, nq),
        in_specs=in_specs, out_specs=out_specs, out_shape=out_shape, scratch_shapes=scratch,
        compiler_params=_params(("arbitrary", "arbitrary")),
    )(*args)


def _gated_out(name, a, gate, w, resid, tm):
    tp, wd = a.shape
    d = w.shape[1]

    def body(a_ref, gate_ref, w_ref, res_ref, o_ref):
        gv = gate_ref[...]
        y = (a_ref[...] * (gv * _sigmoid(gv))).astype(BF16)
        o_ref[...] = res_ref[...] + _dot(y, w_ref[...])

    return pl.pallas_call(
        body, name=name, grid=(tp // tm,),
        in_specs=[pl.BlockSpec((tm, wd), lambda i: (i, 0)),
                  pl.BlockSpec((tm, wd), lambda i: (i, 0)),
                  pl.BlockSpec((wd, d), lambda i: (0, 0)),
                  pl.BlockSpec((tm, d), lambda i: (i, 0))],
        out_specs=pl.BlockSpec((tm, d), lambda i: (i, 0)),
        out_shape=jax.ShapeDtypeStruct((tp, d), F32),
        compiler_params=_params(("parallel",)),
    )(a, gate, w, resid)


def _lru_decay(r, sp):
    log_a = -LRU_C * r * sp
    a = jnp.exp(log_a)
    e2 = a * a
    x2 = 2.0 * log_a
    series = x2 * (1.0 + x2 * (0.5 + x2 * (1.0 / 6.0)))
    em1 = jnp.where(x2 > -0.02, series, e2 - 1.0)
    return a, e2, jnp.sqrt(-em1)


def _softplus(x):
    return jnp.maximum(x, 0.0) + jnp.log1p(jnp.exp(-jnp.abs(x)))


def _rglru_fwd(u, conv_w, conv_b, w_rg, b_rg, w_ig, b_ig, lam, pad, tm):
    tp, w = u.shape
    groups, blk = w_rg.shape[0], w_rg.shape[1]

    def body(u_ref, cw_ref, cb_ref, wr_ref, br_ref, wi_ref, bi_ref, lam_ref,
             uc_ref, r_ref, ig_ref, hs_ref, uext, a_s, b_s, hc):
        i = pl.program_id(0)

        @pl.when(i == 0)
        def _():
            uext[0:SUBLANES, :] = jnp.zeros((SUBLANES, w), F32)
            hc[...] = jnp.zeros((SUBLANES, w), F32)

        uext[SUBLANES:SUBLANES + tm, :] = u_ref[...]
        cw = cw_ref[...]
        uc = cb_ref[...] + uext[pl.ds(SUBLANES - 3, tm), :] * cw[0:1, :]
        uc = uc + uext[pl.ds(SUBLANES - 2, tm), :] * cw[1:2, :]
        uc = uc + uext[pl.ds(SUBLANES - 1, tm), :] * cw[2:3, :]
        uc = uc + uext[pl.ds(SUBLANES, tm), :] * cw[3:4, :]
        uc_ref[...] = uc
        uext[0:SUBLANES, :] = uext[tm:tm + SUBLANES, :]

        sp = _softplus(-lam_ref[...])
        for g in range(groups):
            sl = slice(g * blk, (g + 1) * blk)
            ucg = uc_ref[:, sl]
            ucb = ucg.astype(BF16)
            r = _sigmoid(_dot(ucb, wr_ref[g]) + br_ref[:, sl])
            ig = _sigmoid(_dot(ucb, wi_ref[g]) + bi_ref[:, sl])
            r_ref[:, sl] = r
            ig_ref[:, sl] = ig
            a, _, mult = _lru_decay(r, sp[:, sl])
            a_s[:, sl] = a
            b_s[:, sl] = mult * (ig * ucg)

        @pl.when(i == 0)
        def _():
            row = lax.broadcasted_iota(jnp.int32, (Q_BLOCK, w), 0)
            start = ig_ref[0:Q_BLOCK, :] * uc_ref[0:Q_BLOCK, :]
            b_s[0:Q_BLOCK, :] = jnp.where(row < pad, 0.0, jnp.where(row == pad, start, b_s[0:Q_BLOCK, :]))

        row8 = lax.broadcasted_iota(jnp.int32, (SUBLANES, w), 0)

        def group(gi, h_in):
            off = pl.multiple_of(gi * SUBLANES, SUBLANES)
            av = a_s[pl.ds(off, SUBLANES), :]
            bv = b_s[pl.ds(off, SUBLANES), :]
            for k in (1, 2, 4):
                keep = row8 >= k
                bv = jnp.where(keep, av * pltpu.roll(bv, k, 0) + bv, bv)
                av = jnp.where(keep, av * pltpu.roll(av, k, 0), av)
            hv = av * h_in + bv
            hs_ref[pl.ds(off, SUBLANES), :] = hv
            return jnp.broadcast_to(hv[SUBLANES - 1:SUBLANES, :], (SUBLANES, w))

        hc[...] = lax.fori_loop(0, tm // SUBLANES, group, hc[...])

    row_spec = pl.BlockSpec((tm, w), lambda i: (i, 0))
    vec_spec = pl.BlockSpec((1, w), lambda i: (0, 0))
    mat_spec = pl.BlockSpec((groups, blk, blk), lambda i: (0, 0, 0))
    return pl.pallas_call(
        body, name="b_rglru_fwd", grid=(tp // tm,),
        in_specs=[row_spec, pl.BlockSpec((CONV_WIDTH, w), lambda i: (0, 0)), vec_spec,
                  mat_spec, vec_spec, mat_spec, vec_spec, vec_spec],
        out_specs=[row_spec, row_spec, row_spec, row_spec],
        out_shape=[jax.ShapeDtypeStruct((tp, w), F32)] * 4,
        scratch_shapes=[pltpu.VMEM((tm + SUBLANES, w), F32), pltpu.VMEM((tm, w), F32),
                        pltpu.VMEM((tm, w), F32), pltpu.VMEM((SUBLANES, w), F32)],
        compiler_params=_params(("arbitrary",)),
    )(u, conv_w, conv_b, w_rg, b_rg, w_ig, b_ig, lam)


def _final_loss(h, g, target, x0):
    tp, d = h.shape
    tm = Q_BLOCK
    assert x0 % tm == 0 and target.shape[0] == tp - x0
    lead = x0 // tm

    def body(h_ref, g_ref, t_ref, dh_ref, loss_ref, dg_ref):
        i = pl.program_id(0)

        @pl.when(i == 0)
        def _():
            loss_ref[...] = jnp.zeros_like(loss_ref)
            dg_ref[...] = jnp.zeros_like(dg_ref)

        xv = h_ref[...]
        gv = g_ref[...]
        r = _rms(xv)
        xh = xv * r
        err = jnp.where(i >= lead, xh * gv - t_ref[...], 0.0)
        loss_ref[...] += 0.5 * jnp.sum(jnp.mean(err * err, axis=-1, keepdims=True))
        dy = err / d
        dg_ref[...] += jnp.sum(dy * xh, axis=0, keepdims=True)
        dxh = dy * gv
        dh_ref[...] = r * (dxh - xh * jnp.mean(dxh * xh, axis=-1, keepdims=True))

    return pl.pallas_call(
        body, name="final_loss", grid=(tp // tm,),
        in_specs=[pl.BlockSpec((tm, d), lambda i: (i, 0)),
                  pl.BlockSpec((1, d), lambda i: (0, 0)),
                  pl.BlockSpec((tm, d), lambda i: (jnp.maximum(i - lead, 0), 0))],
        out_specs=[pl.BlockSpec((tm, d), lambda i: (i, 0)),
                   pl.BlockSpec((SUBLANES, LANES), lambda i: (0, 0)),
                   pl.BlockSpec((1, d), lambda i: (0, 0))],
        out_shape=[jax.ShapeDtypeStruct((tp, d), F32),
                   jax.ShapeDtypeStruct((SUBLANES, LANES), F32),
                   jax.ShapeDtypeStruct((1, d), F32)],
        compiler_params=_params(("arbitrary",)),
    )(h, g, target)


def _with_exchange(exchange, n_in, n_out, first, last, compute):
    if exchange is None:
        return compute
    ex_in, ex_out = exchange.n_in, exchange.n_out

    def body(*refs):
        own_in, their_in = refs[:n_in], refs[n_in:n_in + ex_in]
        pos = n_in + ex_in
        own_out, their_out = refs[pos:pos + n_out], refs[pos + n_out:pos + n_out + ex_out]
        rest = refs[pos + n_out + ex_out:]
        own_scratch, sems = rest[:len(rest) - 2], rest[len(rest) - 2:]

        @pl.when(first())
        def _():
            exchange.start(their_in, their_out, sems)

        compute(*own_in, *own_out, *own_scratch)

        @pl.when(last())
        def _():
            exchange.finish(their_in, their_out, sems)

    return body


def _gated_out_bwd(name, dout, a, gate, w, tm, delta_heads=0, exchange=None):
    tp, wd = a.shape
    d = w.shape[1]
    nt = tp // tm

    def compute(do_ref, a_ref, gate_ref, w_ref, da_ref, dgate_ref, dw_ref, *delta_ref):
        @pl.when(pl.program_id(0) == 0)
        def _():
            dw_ref[...] = jnp.zeros_like(dw_ref)

        gv = gate_ref[...]
        av = a_ref[...]
        sg = _sigmoid(gv)
        silu = gv * sg
        dob = do_ref[...].astype(BF16)
        dy = _dot_nt(dob, w_ref[...])
        da = dy * silu
        da_ref[...] = da
        dgate_ref[...] = dy * av * (sg * (1.0 + gv * (1.0 - sg)))
        dw_ref[...] += _dot_tn((av * silu).astype(BF16), dob)
        for h in range(delta_heads):
            sl = slice(h * V_HEAD, (h + 1) * V_HEAD)
            delta_ref[0][h] = _as_rows(jnp.sum(da[:, sl] * av[:, sl], axis=-1, keepdims=True))

    out_specs = [pl.BlockSpec((tm, wd), lambda i: (i, 0)),
                 pl.BlockSpec((tm, wd), lambda i: (i, 0)),
                 pl.BlockSpec((wd, d), lambda i: (0, 0))]
    out_shape = [jax.ShapeDtypeStruct((tp, wd), F32),
                 jax.ShapeDtypeStruct((tp, wd), F32),
                 jax.ShapeDtypeStruct((wd, d), F32)]
    if delta_heads:
        out_specs.append(pl.BlockSpec((delta_heads, None, SUBLANES, tm), lambda i: (0, i, 0, 0)))
        out_shape.append(jax.ShapeDtypeStruct((delta_heads, tp // tm, SUBLANES, tm), F32))
    in_specs = [pl.BlockSpec((tm, d), lambda i: (i, 0)),
                pl.BlockSpec((tm, wd), lambda i: (i, 0)),
                pl.BlockSpec((tm, wd), lambda i: (i, 0)),
                pl.BlockSpec((wd, d), lambda i: (0, 0))]
    args, scratch = [dout, a, gate, w], []
    body = _with_exchange(exchange, len(in_specs), len(out_specs),
                          lambda: pl.program_id(0) == 0, lambda: pl.program_id(0) == nt - 1, compute)
    if exchange is not None:
        in_specs = in_specs + exchange.specs(exchange.n_in)
        out_specs = out_specs + exchange.specs(exchange.n_out)
        out_shape = out_shape + exchange.out_shapes
        args, scratch = args + exchange.arrays, exchange.scratch()
    return pl.pallas_call(
        body, name=name, grid=(nt,),
        in_specs=in_specs, out_specs=out_specs, out_shape=out_shape, scratch_shapes=scratch,
        compiler_params=_params(("arbitrary",)),
    )(*args)


def _rglru_bwd(dhs, hs, r, ig, uc, u, conv_w, w_rg, w_ig, lam, pad, tm):
    tp, w = u.shape
    groups, blk = w_rg.shape[0], w_rg.shape[1]
    nt = tp // tm
    per8 = tm // SUBLANES

    def body(dhs_ref, hs_ref, hprev_ref, r_ref, ig_ref, uc_ref, u_ref, uprev_ref, cw_ref, wr_ref, wi_ref, lam_ref,
             du_ref, dcw_ref, dcb_ref, dwr_ref, dbr_ref, dwi_ref, dbi_ref, dlam_ref,
             aext, c_s, g_s, hext, uext, ducext, gc):
        step = pl.program_id(0)
        ti = nt - 1 - step

        @pl.when(step == 0)
        def _():
            for ref in (dcw_ref, dcb_ref, dwr_ref, dbr_ref, dwi_ref, dbi_ref, dlam_ref):
                ref[...] = jnp.zeros_like(ref)
            aext[tm:tm + SUBLANES, :] = jnp.zeros((SUBLANES, w), F32)
            ducext[tm:tm + SUBLANES, :] = jnp.zeros((SUBLANES, w), F32)
            gc[...] = jnp.zeros((SUBLANES, w), F32)

        lam_v = lam_ref[...]
        sp = _softplus(-lam_v)
        row = ti * tm + lax.broadcasted_iota(jnp.int32, (tm, w), 0)

        rv = r_ref[...]
        a, e2, mult = _lru_decay(rv, sp)
        aext[0:tm, :] = a
        c_s[...] = aext[pl.ds(1, tm), :]
        row8 = lax.broadcasted_iota(jnp.int32, (SUBLANES, w), 0)

        def group(gi, g_in):
            off = pl.multiple_of((per8 - 1 - gi) * SUBLANES, SUBLANES)
            cv = c_s[pl.ds(off, SUBLANES), :]
            dv = dhs_ref[pl.ds(off, SUBLANES), :]
            for k in (1, 2, 4):
                keep = row8 < SUBLANES - k
                dv = jnp.where(keep, cv * pltpu.roll(dv, SUBLANES - k, 0) + dv, dv)
                cv = jnp.where(keep, cv * pltpu.roll(cv, SUBLANES - k, 0), cv)
            gv = cv * g_in + dv
            g_s[pl.ds(off, SUBLANES), :] = gv
            return jnp.broadcast_to(gv[0:1, :], (SUBLANES, w))

        gc[...] = lax.fori_loop(0, per8, group, gc[...])
        aext[tm:tm + SUBLANES, :] = aext[0:SUBLANES, :]

        gsc = jnp.where(row < pad, 0.0, g_s[...])
        hext[0:SUBLANES, :] = hprev_ref[...]
        hext[SUBLANES:SUBLANES + tm, :] = hs_ref[...]
        hprev = jnp.where(row == 0, 0.0, hext[pl.ds(SUBLANES - 1, tm), :])
        igv = ig_ref[...]
        ucv = uc_ref[...]
        first = row == pad
        mult = jnp.where(first, 1.0, mult)
        dmult = gsc * (igv * ucv)
        dig = gsc * mult * ucv
        duc = gsc * mult * igv
        dlog_a = (gsc * hprev) * a + jnp.where(first, 0.0, dmult * (-e2 / mult))
        dlam_ref[...] += jnp.sum(dlog_a * rv, axis=0, keepdims=True) * (LRU_C * _sigmoid(-lam_v))
        dpre_r = dlog_a * (-LRU_C * sp) * (rv * (1.0 - rv))
        dpre_i = dig * (igv * (1.0 - igv))
        dbr_ref[...] += jnp.sum(dpre_r, axis=0, keepdims=True)
        dbi_ref[...] += jnp.sum(dpre_i, axis=0, keepdims=True)
        for g in range(groups):
            sl = slice(g * blk, (g + 1) * blk)
            ucb = ucv[:, sl].astype(BF16)
            drb = dpre_r[:, sl].astype(BF16)
            dib = dpre_i[:, sl].astype(BF16)
            dwr_ref[g] += _dot_tn(ucb, drb)
            dwi_ref[g] += _dot_tn(ucb, dib)
            ducext[0:tm, sl] = duc[:, sl] + _dot_nt(drb, wr_ref[g]) + _dot_nt(dib, wi_ref[g])

        ducv = ducext[0:tm, :]
        cw = cw_ref[...]
        dcb_ref[...] += jnp.sum(ducv, axis=0, keepdims=True)
        uext[0:SUBLANES, :] = jnp.where(ti == 0, 0.0, uprev_ref[...])
        uext[SUBLANES:SUBLANES + tm, :] = u_ref[...]
        for j in range(CONV_WIDTH):
            ush = uext[pl.ds(SUBLANES - (CONV_WIDTH - 1 - j), tm), :]
            dcw_ref[j:j + 1, :] += jnp.sum(ducv * ush, axis=0, keepdims=True)
        du = ducv * cw[3:4, :]
        for k in range(1, CONV_WIDTH):
            du = du + ducext[pl.ds(k, tm), :] * cw[3 - k:4 - k, :]
        du_ref[...] = du
        ducext[tm:tm + SUBLANES, :] = ducext[0:SUBLANES, :]

    rev = lambda s: (nt - 1 - s, 0)
    halo = lambda s: (jnp.maximum((nt - 1 - s) * per8 - 1, 0), 0)
    row_spec = pl.BlockSpec((tm, w), rev)
    halo_spec = pl.BlockSpec((SUBLANES, w), halo)
    vec_spec = pl.BlockSpec((1, w), lambda s: (0, 0))
    mat_spec = pl.BlockSpec((groups, blk, blk), lambda s: (0, 0, 0))
    cw_spec = pl.BlockSpec((CONV_WIDTH, w), lambda s: (0, 0))
    return pl.pallas_call(
        body, name="b_rglru_bwd", grid=(nt,),
        in_specs=[row_spec, row_spec, halo_spec, row_spec, row_spec, row_spec, row_spec, halo_spec,
                  cw_spec, mat_spec, mat_spec, vec_spec],
        out_specs=[row_spec, cw_spec, vec_spec, mat_spec, vec_spec, mat_spec, vec_spec, vec_spec],
        out_shape=[jax.ShapeDtypeStruct((tp, w), F32), jax.ShapeDtypeStruct((CONV_WIDTH, w), F32),
                   jax.ShapeDtypeStruct((1, w), F32), jax.ShapeDtypeStruct((groups, blk, blk), F32),
                   jax.ShapeDtypeStruct((1, w), F32), jax.ShapeDtypeStruct((groups, blk, blk), F32),
                   jax.ShapeDtypeStruct((1, w), F32), jax.ShapeDtypeStruct((1, w), F32)],
        scratch_shapes=[pltpu.VMEM((tm + SUBLANES, w), F32), pltpu.VMEM((tm, w), F32), pltpu.VMEM((tm, w), F32),
                        pltpu.VMEM((tm + SUBLANES, w), F32), pltpu.VMEM((tm + SUBLANES, w), F32),
                        pltpu.VMEM((tm + SUBLANES, w), F32), pltpu.VMEM((SUBLANES, w), F32)],
        compiler_params=_params(("arbitrary",)),
    )(dhs, hs, hs, r, ig, uc, u, u, conv_w, w_rg, w_ig, lam)


def _norm_matmul_bwd(name, x, g, w, dys, tm, resid=None, prologue=None, extra_out=None):
    tp, kin = x.shape
    n = w.shape[1]
    n_dy = len(dys)
    has_res = resid is not None
    has_extra = extra_out is not None

    def body(*refs):
        x_ref, g_ref, w_ref = refs[:3]
        dy_refs = refs[3:3 + n_dy]
        pos = 3 + n_dy
        res_ref = refs[pos] if has_res else None
        pos += int(has_res)
        dx_ref, dw_ref, dg_ref = refs[pos:pos + 3]
        pos += 3
        ex_ref = refs[pos] if has_extra else None
        pos += int(has_extra)
        dy_s = refs[pos]

        @pl.when(pl.program_id(0) == 0)
        def _():
            dw_ref[...] = jnp.zeros_like(dw_ref)
            dg_ref[...] = jnp.zeros_like(dg_ref)

        if prologue is None:
            c0 = 0
            for ref in dy_refs:
                dy_s[:, c0:c0 + ref.shape[1]] = ref[...].astype(BF16)
                c0 += ref.shape[1]
        else:
            prologue(dy_refs, dy_s, ex_ref)

        xv = x_ref[...]
        gv = g_ref[...]
        r = _rms(xv)
        xh = xv * r
        dyb = dy_s[...]
        dn = _dot_nt(dyb, w_ref[...])
        dw_ref[...] += _dot_tn((xh * gv).astype(BF16), dyb)
        dg_ref[...] += jnp.sum(dn * xh, axis=0, keepdims=True)
        dxh = dn * gv
        dx = r * (dxh - xh * jnp.mean(dxh * xh, axis=-1, keepdims=True))
        if has_res:
            dx = dx + res_ref[...]
        dx_ref[...] = dx

    row = lambda width: pl.BlockSpec((tm, width), lambda i: (i, 0))
    in_specs = [row(kin), pl.BlockSpec((1, kin), lambda i: (0, 0)), pl.BlockSpec((kin, n), lambda i: (0, 0))]
    in_specs += [row(a.shape[1]) for a in dys]
    args = [x, g, w, *dys]
    if has_res:
        in_specs.append(row(kin))
        args.append(resid)
    out_specs = [row(kin), pl.BlockSpec((kin, n), lambda i: (0, 0)), pl.BlockSpec((1, kin), lambda i: (0, 0))]
    out_shape = [jax.ShapeDtypeStruct((tp, kin), F32), jax.ShapeDtypeStruct((kin, n), F32),
                 jax.ShapeDtypeStruct((1, kin), F32)]
    if has_extra:
        out_specs.append(row(extra_out[0]))
        out_shape.append(jax.ShapeDtypeStruct((tp, extra_out[0]), extra_out[1]))
    return pl.pallas_call(
        body, name=name, grid=(tp // tm,),
        in_specs=in_specs, out_specs=out_specs, out_shape=out_shape,
        scratch_shapes=[pltpu.VMEM((tm, n), BF16)],
        compiler_params=_params(("arbitrary",)),
    )(*args)


def _flash_bwd(q, k, v, lse, delta, do, heads, pad, tq, exchange=None):
    tp = q.shape[0]
    nq = tp // tq
    scale = (QK_NOPE + QK_ROPE) ** -0.5
    c2 = scale * LOG2E

    def compute(q_ref, k_ref, v_ref, lse_ref, delta_ref, do_ref, dq_ref, dk_ref, dv_ref):
        j = pl.program_id(1)

        @pl.when(j == 0)
        def _():
            dq_ref[...] = jnp.zeros_like(dq_ref)

        kv = k_ref[...]
        vv = v_ref[...]

        def rows_of(ref, i, blocks):
            parts = [ref[i + b][0:1, :] for b in range(blocks)]
            return parts[0] if blocks == 1 else jnp.concatenate(parts, axis=1)

        def make_step(masked, blocks):
            def step(i, carry):
                dk, dv = carry
                off = pl.multiple_of(i * tq, tq)
                qv = q_ref[pl.ds(off, blocks * tq), :]
                dob = do_ref[pl.ds(off, blocks * tq), :].astype(BF16)
                p = jnp.exp2(_dot_nt(kv, qv) * c2 - rows_of(lse_ref, i, blocks))
                if masked:
                    key = j * tq + lax.broadcasted_iota(jnp.int32, (tq, tq), 0)
                    qry = j * tq + lax.broadcasted_iota(jnp.int32, (tq, tq), 1)
                    first = jnp.where((key <= qry) & (key >= pad), p[:, :tq], 0.0)
                    p = first if blocks == 1 else jnp.concatenate([first, p[:, tq:]], axis=1)
                dv = dv + _dot(p.astype(BF16), dob)
                dp = _dot_nt(vv, dob)
                ds = (p * (dp - rows_of(delta_ref, i, blocks)) * scale).astype(BF16)
                dk = dk + _dot(ds, qv)
                dq_ref[pl.ds(off, blocks * tq), :] += _dot_tn(ds, kv)
                return dk, dv
            return step

        odd = (nq - j) % 2
        carry = (jnp.zeros((tq, HEAD_W), F32), jnp.zeros((tq, V_HEAD), F32))
        carry = lax.fori_loop(0, odd, lambda t, cr: make_step(True, 1)(j, cr), carry)
        carry = lax.fori_loop(0, 1 - odd, lambda t, cr: make_step(True, 2)(j, cr), carry)
        start = j + 2 - odd
        for blocks in (4, 2):
            trips = (nq - start) // blocks
            step_n = make_step(False, blocks)
            carry = lax.fori_loop(0, trips, lambda t, cr, s=start, b=blocks, f=step_n: f(s + b * t, cr), carry)
            start = start + blocks * trips
        dk, dv = carry
        dk_ref[...] = dk
        dv_ref[...] = dv

    in_specs = [pl.BlockSpec((tp, HEAD_W), lambda h, j: (0, h)),
                pl.BlockSpec((tq, HEAD_W), lambda h, j: (j, h)),
                pl.BlockSpec((tq, V_HEAD), lambda h, j: (j, h)),
                pl.BlockSpec((None, nq, SUBLANES, tq), lambda h, j: (h, 0, 0, 0)),
                pl.BlockSpec((None, nq, SUBLANES, tq), lambda h, j: (h, 0, 0, 0)),
                pl.BlockSpec((tp, V_HEAD), lambda h, j: (0, h))]
    out_specs = [pl.BlockSpec((tp, HEAD_W), lambda h, j: (0, h)),
                 pl.BlockSpec((tq, HEAD_W), lambda h, j: (j, h)),
                 pl.BlockSpec((tq, V_HEAD), lambda h, j: (j, h))]
    out_shape = [jax.ShapeDtypeStruct((tp, heads * HEAD_W), F32),
                 jax.ShapeDtypeStruct((tp, heads * HEAD_W), F32),
                 jax.ShapeDtypeStruct((tp, heads * V_HEAD), F32)]
    args, scratch = [q, k, v, lse, delta, do], []
    body = _with_exchange(exchange, len(in_specs), len(out_specs),
                          lambda: (pl.program_id(0) == 0) & (pl.program_id(1) == 0),
                          lambda: (pl.program_id(0) == heads - 1) & (pl.program_id(1) == nq - 1), compute)
    if exchange is not None:
        in_specs = in_specs + exchange.specs(exchange.n_in)
        out_specs = out_specs + exchange.specs(exchange.n_out)
        out_shape = out_shape + exchange.out_shapes
        args, scratch = args + exchange.arrays, exchange.scratch()
    return pl.pallas_call(
        body, name="a_flash_bwd", grid=(heads, nq),
        in_specs=in_specs, out_specs=out_specs, out_shape=out_shape, scratch_shapes=scratch,
        compiler_params=_params(("arbitrary", "arbitrary")),
    )(*args)


def _position():
    return lax.axis_index("x"), lax.axis_index("y"), lax.axis_index("c")


def _other_chips(x, y):
    return [(1 - x, y), (x, 1 - y), (1 - x, 1 - y)]


def _block(ref, shard_axis, n, k, split_axis=None, m=None, h=None):
    idx = []
    for a in range(len(ref.shape)):
        start = 0
        size = None
        if a == shard_axis:
            start, size = k * n, n
        if a == split_axis:
            size = (n if a == shard_axis else m) // 2
            start = start + h * size
        idx.append(slice(None) if size is None else pl.ds(start, size))
    return ref.at[tuple(idx)]


def _gather_weights(split, whole_small):
    ns, nw = len(split), len(whole_small)
    n = ns + nw
    arrs = [s[0] for s in split] + [s[0] for s in whole_small]
    axes = [s[1] for s in split] + [s[1] for s in whole_small]

    def body(*refs):
        ins, outs = refs[:n], refs[n:2 * n]
        ici_send, ici_recv, d2d_send, d2d_recv, sib_send, sib_recv = refs[2 * n:]
        x, y, c = _position()
        me = 2 * x + y
        others = _other_chips(x, y)
        sent, local = [], []

        def remote(src, dst, sems, idx, to):
            return pltpu.make_async_remote_copy(src_ref=src, dst_ref=dst, send_sem=sems[0].at[idx],
                                                recv_sem=sems[1].at[idx], device_id=to, device_id_type=MESH)

        for a in range(n):
            width = ins[a].shape[axes[a]]
            mine = remote(ins[a], _block(outs[a], axes[a], width, me), (sib_send, sib_recv), a, (x, y, 1 - c))
            mine.start()
            local.append(mine)
            for j, (px, py) in enumerate(others):
                if a < ns:
                    sx = split[a][2]
                    src = _block(ins[a], None, None, None, sx, ins[a].shape[sx], c)
                    dst = _block(outs[a], axes[a], width, me, sx, outs[a].shape[sx], c)
                else:
                    src, dst = ins[a], _block(outs[a], axes[a], width, me)
                cp = remote(src, dst, (ici_send, ici_recv), 3 * a + j, (px, py, c))
                cp.start()
                sent.append(cp)
        for a in range(ns):
            width = ins[a].shape[axes[a]]
            sx = split[a][2]
            for j, (px, py) in enumerate(others):
                theirs = _block(outs[a], axes[a], width, 2 * px + py, sx, outs[a].shape[sx], c)
                remote(theirs, theirs, (ici_send, ici_recv), 3 * a + j, (px, py, c)).wait_recv()
                fwd = remote(theirs, theirs, (d2d_send, d2d_recv), 3 * a + j, (x, y, 1 - c))
                fwd.start()
                sent.append(fwd)
        for a in range(ns, n):
            width = ins[a].shape[axes[a]]
            for j, (px, py) in enumerate(others):
                theirs = _block(outs[a], axes[a], width, 2 * px + py)
                remote(theirs, theirs, (ici_send, ici_recv), 3 * a + j, (px, py, c)).wait_recv()
        for a in range(ns):
            width = ins[a].shape[axes[a]]
            sx = split[a][2]
            for j, (px, py) in enumerate(others):
                from_sibling = _block(outs[a], axes[a], width, 2 * px + py, sx, outs[a].shape[sx], 1 - c)
                remote(from_sibling, from_sibling, (d2d_send, d2d_recv), 3 * a + j, (x, y, 1 - c)).wait_recv()
        for cp in sent:
            cp.wait_send()
        for cp in local:
            cp.wait()

    def whole_shape(a, axis):
        shape = list(a.shape)
        shape[axis] *= N_CHIPS
        return jax.ShapeDtypeStruct(tuple(shape), a.dtype)

    any_spec = pl.BlockSpec(memory_space=pl.ANY)
    return pl.pallas_call(
        body, name="gather_weights",
        in_specs=[any_spec] * n, out_specs=[any_spec] * n,
        out_shape=[whole_shape(a, ax) for a, ax in zip(arrs, axes)],
        scratch_shapes=[pltpu.SemaphoreType.DMA((3 * n,)), pltpu.SemaphoreType.DMA((3 * n,)),
                        pltpu.SemaphoreType.DMA((3 * ns,)), pltpu.SemaphoreType.DMA((3 * ns,)),
                        pltpu.SemaphoreType.DMA((n,)), pltpu.SemaphoreType.DMA((n,))],
        compiler_params=pltpu.CompilerParams(has_side_effects=True),
    )(*arrs)


class _Grad:
    def __init__(self, name, g, kind, rh, cols, groups=None):
        self.name, self.g, self.kind, self.rh, self.cols, self.groups = name, g, kind, rh, cols, groups
        if kind == 'rows':
            self.tr = rh
        elif kind == 'gate':
            self.tr = rh // (groups // 2)
        else:
            self.tr = 256 if rh % 256 == 0 else rh
        self.nb = rh // self.tr

    def pieces(self, ref, k, h):
        rh, cols = self.rh, self.cols
        if self.kind == 'cols':
            return [(ref.at[pl.ds(h * rh, rh), pl.ds(k * cols, cols)], 0, rh)]
        if self.kind == 'rows':
            return [(ref.at[pl.ds((2 * k + h) * rh, rh), :], 0, rh)]
        if self.kind == 'lead':
            return [(ref.at[k, pl.ds(h * rh, rh), :], 0, rh)]
        per = self.groups // 2
        return [(ref.at[pl.ds((((h * per + gi) * N_CHIPS) + k) * self.tr, self.tr), :], gi * self.tr, self.tr)
                for gi in range(per)]

    def block_spec(self):
        tr, nb, cols = self.tr, self.nb, self.cols
        if self.kind == 'cols':
            return pl.BlockSpec((tr, cols), lambda k, i, c: (c[0] * nb + i, k))
        if self.kind == 'rows':
            return pl.BlockSpec((tr, cols), lambda k, i, c: (2 * k + c[0], 0))
        if self.kind == 'lead':
            return pl.BlockSpec((None, tr, cols), lambda k, i, c: (k, c[0] * nb + i, 0))
        return pl.BlockSpec((tr, cols), lambda k, i, c: ((c[0] * nb + i) * N_CHIPS + k, 0))


class _Exchange:
    def __init__(self, name, arrays, out_shapes, n_copies, copies):
        self.name, self.arrays, self.out_shapes, self.n_copies, self.copies = name, arrays, out_shapes, n_copies, copies
        self.n_in, self.n_out = len(arrays), len(out_shapes)

    def specs(self, n):
        return [pl.BlockSpec(memory_space=pl.ANY)] * n

    def scratch(self):
        return [pltpu.SemaphoreType.DMA((self.n_copies,)), pltpu.SemaphoreType.DMA((self.n_copies,))]

    def _descriptors(self, in_refs, out_refs, sems):
        return self.copies(in_refs, out_refs, sems[0], sems[1])

    def start(self, in_refs, out_refs, sems):
        for cp in self._descriptors(in_refs, out_refs, sems):
            cp.start()

    def finish(self, in_refs, out_refs, sems):
        for cp in self._descriptors(in_refs, out_refs, sems):
            cp.wait()

    def run(self):
        def body(*refs):
            ins, outs, sems = refs[:self.n_in], refs[self.n_in:self.n_in + self.n_out], refs[self.n_in + self.n_out:]
            self.start(ins, outs, sems)
            self.finish(ins, outs, sems)

        return pl.pallas_call(
            body, name=self.name,
            in_specs=self.specs(self.n_in), out_specs=self.specs(self.n_out), out_shape=self.out_shapes,
            scratch_shapes=self.scratch(),
            compiler_params=pltpu.CompilerParams(has_side_effects=True),
        )(*self.arrays)


def _gather_whole(name, shards):
    def copies(ins, outs, send_sems, recv_sems):
        x, y, c = _position()
        me = 2 * x + y
        made = []
        for a, (_, axis) in enumerate(shards):
            dst = _block(outs[a], axis, ins[a].shape[axis], me)
            for j, to in enumerate([(x, y, 1 - c)] + [(px, py, c) for px, py in _other_chips(x, y)]):
                made.append(pltpu.make_async_remote_copy(
                    src_ref=ins[a], dst_ref=dst, send_sem=send_sems.at[4 * a + j], recv_sem=recv_sems.at[4 * a + j],
                    device_id=to, device_id_type=MESH))
        return made

    def whole_shape(a, axis):
        shape = list(a.shape)
        shape[axis] *= N_CHIPS
        return jax.ShapeDtypeStruct(tuple(shape), a.dtype)

    return _Exchange(name, [s[0] for s in shards], [whole_shape(*s) for s in shards], 4 * len(shards), copies)


def _halves_to_sibling(name, grads):
    total = sum(len(gr.pieces(gr.g, 0, 0)) * N_CHIPS for gr in grads)

    def copies(ins, outs, send_sems, recv_sems):
        x, y, c = _position()
        made = []
        for gr, g_ref, got_ref in zip(grads, ins, outs):
            for k in range(N_CHIPS):
                for src, r0, nr in gr.pieces(g_ref, k, 1 - c):
                    idx = len(made)
                    made.append(pltpu.make_async_remote_copy(
                        src_ref=src, dst_ref=got_ref.at[k, pl.ds(r0, nr), :],
                        send_sem=send_sems.at[idx], recv_sem=recv_sems.at[idx],
                        device_id=(x, y, 1 - c), device_id_type=MESH))
        return made

    return _Exchange(name, [gr.g for gr in grads],
                     [jax.ShapeDtypeStruct((N_CHIPS, gr.rh, gr.cols), F32) for gr in grads], total, copies)


def _chip_sum(gr, got, c):
    def body(c_ref, g_ref, got_ref, o_ref):
        o_ref[...] = (g_ref[...] + got_ref[...]).astype(BF16)

    tile = pl.BlockSpec((None, gr.tr, gr.cols), lambda k, i, c_ref: (k, i, 0))
    return pl.pallas_call(
        body, name="chip_sum_" + gr.name,
        grid_spec=pltpu.PrefetchScalarGridSpec(
            num_scalar_prefetch=1, grid=(N_CHIPS, gr.nb),
            in_specs=[gr.block_spec(), tile], out_specs=tile),
        out_shape=jax.ShapeDtypeStruct((N_CHIPS, gr.rh, gr.cols), BF16),
        compiler_params=_params(("parallel", "parallel")),
    )(c, gr.g, got)


def _blocks_to_chips(name, parts):
    n = len(parts)

    def copies(ins, outs, send_sems, recv_sems):
        x, y, c = _position()
        made = []
        for a in range(n):
            for j, (px, py) in enumerate(_other_chips(x, y)):
                made.append(pltpu.make_async_remote_copy(
                    src_ref=ins[a].at[2 * px + py], dst_ref=outs[a].at[j],
                    send_sem=send_sems.at[3 * a + j], recv_sem=recv_sems.at[3 * a + j],
                    device_id=(px, py, c), device_id_type=MESH))
        return made

    return _Exchange(name, parts, [jax.ShapeDtypeStruct((3,) + p.shape[1:], p.dtype) for p in parts], 3 * n, copies)


def _sum_chips(name, part, got, me, c):
    nk, rh, cols = part.shape
    tr = 256 if rh % 256 == 0 else rh
    nb = rh // tr

    def body(me_ref, c_ref, own_ref, *rest):
        got_refs, o_ref = rest[:nk], rest[nk]
        own = own_ref[...].astype(F32)
        acc = None
        for k in range(nk):
            term = jnp.where(me_ref[0] == k, own, got_refs[k][...].astype(F32))
            acc = term if acc is None else acc + term
        o_ref[...] = acc

    def got_map(k):
        def index(i, me_ref, c_ref):
            xor = jnp.bitwise_xor(me_ref[0], k)
            slot = jnp.where(xor == 1, 1, jnp.where(xor == 3, 2, 0))
            return (slot, i, 0)
        return index

    return pl.pallas_call(
        body, name="sum_" + name,
        grid_spec=pltpu.PrefetchScalarGridSpec(
            num_scalar_prefetch=2, grid=(nb,),
            in_specs=[pl.BlockSpec((None, tr, cols), lambda i, me_ref, c_ref: (me_ref[0], i, 0))]
            + [pl.BlockSpec((None, tr, cols), got_map(k)) for k in range(nk)],
            out_specs=pl.BlockSpec((tr, cols), lambda i, me_ref, c_ref: (c_ref[0] * nb + i, 0))),
        out_shape=jax.ShapeDtypeStruct((2 * rh, cols), F32),
        compiler_params=_params(("parallel",)),
    )(me, c, part, *([got] * nk))


def _share_with_sibling(halves):
    n = len(halves)

    def body(*refs):
        outs = refs[n:2 * n]
        send_sems, recv_sems = refs[2 * n:]
        x, y, c = _position()
        copies = []
        for a in range(n):
            rh = outs[a].shape[0] // 2
            mine = outs[a].at[pl.ds(c * rh, rh), :]
            cp = pltpu.make_async_remote_copy(
                src_ref=mine, dst_ref=mine, send_sem=send_sems.at[a], recv_sem=recv_sems.at[a],
                device_id=(x, y, 1 - c), device_id_type=MESH)
            cp.start()
            copies.append(cp)
        for cp in copies:
            cp.wait()

    any_spec = pl.BlockSpec(memory_space=pl.ANY)
    return pl.pallas_call(
        body, name="grads_share",
        in_specs=[any_spec] * n, out_specs=[any_spec] * n,
        out_shape=[jax.ShapeDtypeStruct(h.shape, h.dtype) for h in halves],
        input_output_aliases={a: a for a in range(n)},
        scratch_shapes=[pltpu.SemaphoreType.DMA((n,)), pltpu.SemaphoreType.DMA((n,))],
        compiler_params=pltpu.CompilerParams(has_side_effects=True),
    )(*halves)


def _adamw(name, w, g, m, v):
    rows, cols = w.shape
    tr = 256 if rows % 256 == 0 else rows

    def body(w_ref, g_ref, m_ref, v_ref, d_ref, nm_ref, nv_ref):
        gv = g_ref[...]
        mn = ADAM_B1 * m_ref[...] + (1.0 - ADAM_B1) * gv
        vn = ADAM_B2 * v_ref[...] + (1.0 - ADAM_B2) * (gv * gv)
        m_hat = mn / (1.0 - ADAM_B1 ** ADAM_STEP)
        v_hat = vn / (1.0 - ADAM_B2 ** ADAM_STEP)
        d_ref[...] = -ADAM_LR * (m_hat / (jnp.sqrt(v_hat) + ADAM_EPS) + ADAM_WD * w_ref[...])
        nm_ref[...] = mn
        nv_ref[...] = vn

    spec = pl.BlockSpec((tr, cols), lambda i: (i, 0))
    return pl.pallas_call(
        body, name=name, grid=(rows // tr,),
        in_specs=[spec] * 4, out_specs=[spec] * 3,
        out_shape=[jax.ShapeDtypeStruct((rows, cols), F32)] * 3,
        compiler_params=_params(("parallel",)),
    )(w, g, m, v)


def _as2d(a):
    if a.ndim == 1:
        return a.reshape(1, -1)
    return a.reshape(-1, a.shape[-1])


def _unshard(gathered, axis):
    moved = jnp.moveaxis(gathered, 0, axis)
    shape = list(gathered.shape[1:])
    shape[axis] *= N_CHIPS
    return moved.reshape(shape)


def _rope_tables(tp, pad):
    pos = jnp.arange(tp, dtype=F32) - pad
    inv_freq = ROPE_BASE ** (-jnp.arange(0, QK_ROPE, 2, dtype=F32) / QK_ROPE)
    ang = pos[:, None] * inv_freq[None, :]
    cos, sin = jnp.cos(ang), jnp.sin(ang)
    zeros = jnp.zeros((tp, LANES - QK_ROPE), F32)
    return jnp.concatenate([cos, cos, zeros], axis=1), jnp.concatenate([-sin, sin, zeros], axis=1)


def _matrix_grad(name, g, heads):
    if name in ('b_w_rg', 'b_w_ig'):
        groups, blk, cols = g.shape
        return _Grad(name, g.reshape(groups * blk, cols), 'gate', (groups // 2) * (blk // N_CHIPS), cols, groups)
    rows, cols = g.shape
    if name in ('a_w_out', 'b_w_out'):
        return _Grad(name, g, 'rows', rows // (2 * N_CHIPS), cols)
    if name == 'a_w_uq' and heads % N_CHIPS != 0:
        g = g.reshape(rows, heads, HEAD_W)[:, :, :QK_NOPE + QK_ROPE].reshape(rows, -1)
        cols = g.shape[1]
    if name == 'a_w_in' or (name == 'a_w_uq' and heads % N_CHIPS != 0):
        g = jnp.moveaxis(g.reshape(rows, N_CHIPS, cols // N_CHIPS), 1, 0)
        return _Grad(name, g, 'lead', rows // 2, cols // N_CHIPS)
    return _Grad(name, g, 'cols', rows // 2, cols // N_CHIPS)


def _kernel_form(name, w):
    return w[0] if name in ('b_w_rg', 'b_w_ig', 'b_conv_w') else _as2d(w)


def _local_grads(x, target, wt, heads, c_idx, late_names, late_gather):
    wt = dict(wt)
    seq, d = x.shape
    n_meta = wt['meta_tokens'].shape[0]
    t = seq + n_meta
    pad = (-t) % Q_BLOCK
    tp = t + pad
    x0 = pad + n_meta
    tm = _row_tile(tp)
    ql = wt['a_q_norm_g'].shape[1]
    kvl = wt['a_kv_norm_g'].shape[1]
    mla_w = heads * V_HEAD

    h0 = jnp.concatenate([jnp.zeros((pad, d), F32), wt['meta_tokens'], x], axis=0)
    cos_t, sin_t = _rope_tables(tp, pad)

    w_in_a = wt['a_w_in']
    zcol = jnp.zeros((d, LANES - QK_ROPE), BF16)
    w_in_a = jnp.concatenate([w_in_a[:, :ql + kvl + QK_ROPE], zcol, w_in_a[:, ql + kvl + QK_ROPE:]], axis=1)
    c_kv, c_kr, c_gate = ql, ql + kvl, ql + kvl + LANES
    splits_a = [(0, c_kv), (c_kv, c_kr), (c_kr, c_gate), (c_gate, c_gate + mla_w)]
    w_uq = wt['a_w_uq'].reshape(ql, heads, QK_NOPE + QK_ROPE)
    w_uq = jnp.pad(w_uq, ((0, 0), (0, 0), (0, HEAD_W - QK_NOPE - QK_ROPE))).reshape(ql, heads * HEAD_W)
    w_ukv = wt['a_w_ukv']

    q_lat, kv_lat, kr_raw, gate_a = _norm_matmul("a_in_proj", h0, wt['a_norm_g'], w_in_a, splits_a, tm)
    q = _q_proj(q_lat, wt['a_q_norm_g'], w_uq, cos_t, sin_t, heads, tm)
    k, v = _kv_proj(kv_lat, wt['a_kv_norm_g'], w_ukv, kr_raw, cos_t, sin_t, heads, tm)
    attn, lse, *late_whole = _flash_fwd(q, k, v, heads, pad, tm, exchange=late_gather)
    wt.update({n: _kernel_form(n, w) for n, w in zip(late_names, late_whole)})
    lru_w = wt['b_conv_w'].shape[1]
    h1 = _gated_out("a_out_proj", attn, gate_a, wt['a_w_out'], h0, tm)

    u, gate_b = _norm_matmul("b_in_proj", h1, wt['b_norm_g'], wt['b_w_in'], [(0, lru_w), (lru_w, 2 * lru_w)], tm)
    uc, r, ig, hs = _rglru_fwd(u, wt['b_conv_w'], wt['b_conv_b'], wt['b_w_rg'], wt['b_b_rg'],
                               wt['b_w_ig'], wt['b_b_ig'], wt['b_lam'], pad, tm)
    h2 = _gated_out("b_out_proj", hs, gate_b, wt['b_w_out'], h1, tm)

    dh2, loss, d_final_g = _final_loss(h2, wt['final_norm_g'], target, x0)

    dhs, dgate_b, dw_out_b = _gated_out_bwd("b_out_proj_bwd", dh2, hs, gate_b, wt['b_w_out'], tm)
    du, dconv_w, dconv_b, dw_rg, db_rg, dw_ig, db_ig, dlam = _rglru_bwd(
        dhs, hs, r, ig, uc, u, wt['b_conv_w'], wt['b_w_rg'], wt['b_w_ig'], wt['b_lam'], pad, tm)
    dh1, dw_in_b, dg_b = _norm_matmul_bwd("b_in_proj_bwd", h1, wt['b_norm_g'], wt['b_w_in'], [du, dgate_b], tm, resid=dh2)

    early = [_matrix_grad(n, g, heads) for n, g in
             (('b_w_in', dw_in_b), ('b_w_rg', dw_rg), ('b_w_ig', dw_ig), ('b_w_out', dw_out_b))]
    dattn, dgate_a, dw_out_a, delta, *got = _gated_out_bwd(
        "a_out_proj_bwd", dh1, attn, gate_a, wt['a_w_out'], tm, delta_heads=heads,
        exchange=_halves_to_sibling("grads_to_sibling_b", early))
    early_sums = [_chip_sum(gr, r, c_idx) for gr, r in zip(early, got)]
    dq, dk, dv, *early_from_chips = _flash_bwd(q, k, v, lse, delta, dattn, heads, pad, tm,
                                               exchange=_blocks_to_chips("grads_to_chips_b", early_sums))

    def q_prologue(dy_refs, dy_s, ex_ref):
        (dq_ref,), cos_v, sin_v = dy_refs[:1], dy_refs[1][...], dy_refs[2][...]
        for h in range(heads):
            c0 = h * HEAD_W
            dy_s[:, c0:c0 + QK_NOPE] = dq_ref[:, c0:c0 + QK_NOPE].astype(BF16)
            dy_s[:, c0 + QK_NOPE:c0 + HEAD_W] = _unrope(dq_ref[:, c0 + QK_NOPE:c0 + HEAD_W], cos_v, sin_v).astype(BF16)

    dq_lat, dw_uq, dg_q = _norm_matmul_bwd("a_q_proj_bwd", q_lat, wt['a_q_norm_g'], w_uq, [dq, cos_t, sin_t], tm,
                                           prologue=q_prologue)

    def kv_prologue(dy_refs, dy_s, ex_ref):
        dk_ref, dv_ref = dy_refs[:2]
        cos_v, sin_v = dy_refs[2][...], dy_refs[3][...]
        dkr = jnp.zeros((dk_ref.shape[0], LANES), F32)
        for h in range(heads):
            c0 = h * (QK_NOPE + V_HEAD)
            dy_s[:, c0:c0 + QK_NOPE] = dk_ref[:, h * HEAD_W:h * HEAD_W + QK_NOPE].astype(BF16)
            dy_s[:, c0 + QK_NOPE:c0 + QK_NOPE + V_HEAD] = dv_ref[:, h * V_HEAD:(h + 1) * V_HEAD].astype(BF16)
            dkr = dkr + dk_ref[:, h * HEAD_W + QK_NOPE:(h + 1) * HEAD_W]
        ex_ref[...] = _unrope(dkr, cos_v, sin_v)

    dkv_lat, dw_ukv, dg_kv, dkr_raw = _norm_matmul_bwd(
        "a_kv_proj_bwd", kv_lat, wt['a_kv_norm_g'], w_ukv, [dk, dv, cos_t, sin_t], tm,
        prologue=kv_prologue, extra_out=(LANES, F32))

    dh0, dw_in_a, dg_a = _norm_matmul_bwd("a_in_proj_bwd", h0, wt['a_norm_g'], w_in_a,
                                          [dq_lat, dkv_lat, dkr_raw, dgate_a], tm, resid=dh1)

    dw_in_a = jnp.concatenate([dw_in_a[:, :c_kr + QK_ROPE], dw_in_a[:, c_gate:]], axis=1)
    grads = {
        'meta_tokens': dh0[pad:x0], 'a_norm_g': dg_a, 'a_w_in': dw_in_a, 'a_q_norm_g': dg_q, 'a_kv_norm_g': dg_kv,
        'a_w_uq': dw_uq, 'a_w_ukv': dw_ukv, 'a_w_out': dw_out_a, 'b_norm_g': dg_b, 'b_w_in': dw_in_b,
        'b_conv_w': dconv_w, 'b_conv_b': dconv_b, 'b_w_rg': dw_rg, 'b_b_rg': db_rg, 'b_w_ig': dw_ig,
        'b_b_ig': db_ig, 'b_lam': dlam, 'b_w_out': dw_out_b, 'final_norm_g': d_final_g,
    }
    return loss, dh0[x0:], grads, list(zip(early, early_sums, early_from_chips))


def _chip_major(whole, local_shape, axis):
    if axis is None:
        return jnp.broadcast_to(whole.reshape(1, -1), (N_CHIPS, whole.size))
    shape = list(local_shape)
    g = whole.reshape(shape[:axis] + [N_CHIPS, shape[axis]] + shape[axis + 1:])
    return jnp.moveaxis(g, axis, 0).reshape(N_CHIPS, -1)


def kernel(x, meta_tokens, a_norm_g, a_w_in, a_q_norm_g, a_kv_norm_g, a_w_uq, a_w_ukv, a_w_out, b_norm_g, b_w_in, b_conv_w, b_conv_b, b_w_rg, b_b_rg, b_w_ig, b_b_ig, b_lam, b_w_out, final_norm_g, loss_target, m_meta_tokens, m_a_norm_g, m_a_w_in, m_a_q_norm_g, m_a_kv_norm_g, m_a_w_uq, m_a_w_ukv, m_a_w_out, m_b_norm_g, m_b_w_in, m_b_conv_w, m_b_conv_b, m_b_w_rg, m_b_b_rg, m_b_w_ig, m_b_b_ig, m_b_lam, m_b_w_out, m_final_norm_g, v_meta_tokens, v_a_norm_g, v_a_w_in, v_a_q_norm_g, v_a_kv_norm_g, v_a_w_uq, v_a_w_ukv, v_a_w_out, v_b_norm_g, v_b_w_in, v_b_conv_w, v_b_conv_b, v_b_w_rg, v_b_b_rg, v_b_w_ig, v_b_b_ig, v_b_lam, v_b_w_out, v_final_norm_g):
    local_w = dict(zip(WEIGHTS, (meta_tokens, a_norm_g, a_w_in, a_q_norm_g, a_kv_norm_g, a_w_uq, a_w_ukv, a_w_out,
                                 b_norm_g, b_w_in, b_conv_w, b_conv_b, b_w_rg, b_b_rg, b_w_ig, b_b_ig, b_lam,
                                 b_w_out, final_norm_g)))
    local_m = dict(zip(WEIGHTS, (m_meta_tokens, m_a_norm_g, m_a_w_in, m_a_q_norm_g, m_a_kv_norm_g, m_a_w_uq,
                                 m_a_w_ukv, m_a_w_out, m_b_norm_g, m_b_w_in, m_b_conv_w, m_b_conv_b, m_b_w_rg,
                                 m_b_b_rg, m_b_w_ig, m_b_b_ig, m_b_lam, m_b_w_out, m_final_norm_g)))
    local_v = dict(zip(WEIGHTS, (v_meta_tokens, v_a_norm_g, v_a_w_in, v_a_q_norm_g, v_a_kv_norm_g, v_a_w_uq,
                                 v_a_w_ukv, v_a_w_out, v_b_norm_g, v_b_w_in, v_b_conv_w, v_b_conv_b, v_b_w_rg,
                                 v_b_b_rg, v_b_w_ig, v_b_b_ig, v_b_lam, v_b_w_out, v_final_norm_g)))
    matrices = ('a_w_in', 'a_w_uq', 'a_w_ukv', 'a_w_out', 'b_w_in', 'b_w_rg', 'b_w_ig', 'b_w_out')
    heads = a_w_uq.shape[-1] * N_CHIPS // (QK_NOPE + QK_ROPE)

    split, small, late = [], [], []
    for n in WEIGHTS:
        if SHARD_AXIS[n] is None:
            continue
        if n.startswith('b_'):
            late.append((n, local_w[n].astype(BF16) if n in matrices else local_w[n], SHARD_AXIS[n]))
        elif n == 'a_w_in':
            split.append((n, local_w[n].astype(BF16)[None], 0, 2))
        elif n in matrices:
            split.append((n, local_w[n].astype(BF16), SHARD_AXIS[n], 1))
        else:
            small.append((n, local_w[n], SHARD_AXIS[n]))
    gathered = _gather_weights([s[1:] for s in split], [s[1:] for s in small])
    whole = dict(zip([s[0] for s in split + small], gathered))
    whole['a_w_in'] = _unshard(whole['a_w_in'], SHARD_AXIS['a_w_in'])
    late_names = [s[0] for s in late]
    wt = {n: _kernel_form(n, whole.get(n, local_w[n])) for n in WEIGHTS if n not in late_names}

    c_idx = lax.axis_index("c").astype(jnp.int32).reshape(1)
    me_idx = (2 * lax.axis_index("x") + lax.axis_index("y")).astype(jnp.int32).reshape(1)
    loss, grad_x, grads, early = _local_grads(x[0], loss_target[0], wt, heads, c_idx, late_names,
                                              _gather_whole("gather_weights_b", [s[1:] for s in late]))

    ext_uq = heads % N_CHIPS == 0
    late = [_matrix_grad(n, grads[n], heads) for n in matrices if n not in [e[0].name for e in early]]
    rest = [n for n in WEIGHTS if n not in matrices]
    pieces = [_chip_major(grads[n], local_w[n].shape, SHARD_AXIS[n]) for n in rest]
    length = sum(p.shape[1] for p in pieces)
    unit = 2 * SUBLANES * 1024
    padded = -(-length // unit) * unit
    flat = jnp.concatenate(pieces + [jnp.zeros((N_CHIPS, padded - length), F32)], axis=1)
    late.append(_Grad('small', flat.reshape(N_CHIPS, padded // 1024, 1024), 'lead', padded // 2048, 1024))

    got = _halves_to_sibling("grads_to_sibling", late).run()
    chip_sums = [_chip_sum(gr, r, c_idx) for gr, r in zip(late, got)]
    from_chips = _blocks_to_chips("grads_to_chips", chip_sums).run()
    through = early + list(zip(late, chip_sums, from_chips))
    halves = [_sum_chips(gr.name, p, r, me_idx, c_idx) for gr, p, r in through]
    summed = dict(zip([gr.name for gr, _, _ in through], _share_with_sibling(halves)))
    if ext_uq:
        g = summed['a_w_uq']
        summed['a_w_uq'] = g.reshape(g.shape[0], -1, HEAD_W)[:, :, :QK_NOPE + QK_ROPE]
    total = summed['small'].reshape(-1)

    out_g, out_d, out_m, out_v = [], [], [], []
    off = 0
    for n in WEIGHTS:
        shape = local_w[n].shape
        if n in matrices:
            g = summed[n].reshape(shape)
        else:
            size = 1
            for s in shape:
                size *= s
            g = total[off:off + size].reshape(shape)
            off += size
        delta, new_m, new_v = _adamw("adamw_" + n, _as2d(local_w[n]), _as2d(g), _as2d(local_m[n]), _as2d(local_v[n]))
        out_g.append(g)
        out_d.append(delta.reshape(shape))
        out_m.append(new_m.reshape(shape))
        out_v.append(new_v.reshape(shape))

    loss = lax.psum(loss[0, 0], ("x", "y", "c"))
    return (loss, grad_x[None], *out_g, *out_d, *out_m, *out_v)
```

```python
import functools

import jax
import jax.numpy as jnp
from jax import lax
from jax.experimental import pallas as pl
from jax.experimental.pallas import tpu as pltpu

F32 = jnp.float32
BF16 = jnp.bfloat16
MESH = pl.DeviceIdType.MESH

RMS_EPS = 1e-6
QK_NOPE = 128
QK_ROPE = 64
V_HEAD = 128
HEAD_W = 256
ROPE_BASE = 10000.0
Q_BLOCK = 128
MASK_VALUE = -1e30
CONV_WIDTH = 4
LRU_C = 8.0
N_CHIPS = 4

ADAM_LR = 0.001
ADAM_B1 = 0.9
ADAM_B2 = 0.999
ADAM_EPS = 1e-08
ADAM_WD = 0.01
ADAM_STEP = 10

VMEM_LIMIT_V7X = 56 * 1024 * 1024
LANES = 128
SUBLANES = 8

WEIGHTS = ['meta_tokens', 'a_norm_g', 'a_w_in', 'a_q_norm_g', 'a_kv_norm_g', 'a_w_uq', 'a_w_ukv',
           'a_w_out', 'b_norm_g', 'b_w_in', 'b_conv_w', 'b_conv_b', 'b_w_rg', 'b_b_rg', 'b_w_ig',
           'b_b_ig', 'b_lam', 'b_w_out', 'final_norm_g']
SHARD_AXIS = {'meta_tokens': 1, 'a_norm_g': None, 'a_w_in': 2, 'a_q_norm_g': None, 'a_kv_norm_g': None,
              'a_w_uq': 2, 'a_w_ukv': 2, 'a_w_out': 1, 'b_norm_g': 1, 'b_w_in': 2, 'b_conv_w': 2,
              'b_conv_b': 1, 'b_w_rg': 2, 'b_b_rg': 1, 'b_w_ig': 2, 'b_b_ig': 1, 'b_lam': 1,
              'b_w_out': 1, 'final_norm_g': None}


def _params(sem=None):
    return pltpu.CompilerParams(dimension_semantics=sem, vmem_limit_bytes=VMEM_LIMIT_V7X)


def _row_tile(tp):
    return 384 if (tp % 384 == 0 and tp >= 1152) else 128


def _sigmoid(x):
    return 1.0 / (1.0 + jnp.exp(-x))


def _rms(x):
    return lax.rsqrt(jnp.mean(x * x, axis=-1, keepdims=True) + RMS_EPS)


def _swap32(x):
    lane = lax.broadcasted_iota(jnp.int32, x.shape, 1)
    return jnp.where(lane < 32, pltpu.roll(x, 96, 1), pltpu.roll(x, 32, 1))


def _rope(x, cos_t, sin_t):
    return x * cos_t + _swap32(x) * sin_t


def _unrope(d, cos_t, sin_t):
    lane = lax.broadcasted_iota(jnp.int32, d.shape, 1)
    return jnp.where(lane < QK_ROPE, d * cos_t + _swap32(d * sin_t), 0.0)


def _dot(a, b):
    return jnp.dot(a, b, preferred_element_type=F32)


def _dot_nt(a, b):
    return lax.dot_general(a, b, (((1,), (1,)), ((), ())), preferred_element_type=F32)


def _dot_tn(a, b):
    return lax.dot_general(a, b, (((0,), (0,)), ((), ())), preferred_element_type=F32)


def _norm_matmul(name, x, g, w, splits, tm):
    tp, kin = x.shape
    n = w.shape[1]

    def body(x_ref, g_ref, w_ref, *outs):
        xv = x_ref[...]
        nrm = ((xv * _rms(xv)) * g_ref[...]).astype(BF16)
        y = _dot(nrm, w_ref[...])
        for o_ref, (c0, c1) in zip(outs, splits):
            o_ref[...] = y[:, c0:c1]

    return pl.pallas_call(
        body, name=name, grid=(tp // tm,),
        in_specs=[pl.BlockSpec((tm, kin), lambda i: (i, 0)),
                  pl.BlockSpec((1, kin), lambda i: (0, 0)),
                  pl.BlockSpec((kin, n), lambda i: (0, 0))],
        out_specs=[pl.BlockSpec((tm, c1 - c0), lambda i: (i, 0)) for c0, c1 in splits],
        out_shape=[jax.ShapeDtypeStruct((tp, c1 - c0), F32) for c0, c1 in splits],
        compiler_params=_params(("parallel",)),
    )(x, g, w)


def _q_proj(q_lat, g, w_uq, cos_t, sin_t, heads, tm):
    tp, kin = q_lat.shape
    n = heads * HEAD_W

    def body(x_ref, g_ref, w_ref, cos_ref, sin_ref, q_ref):
        xv = x_ref[...]
        nrm = ((xv * _rms(xv)) * g_ref[...]).astype(BF16)
        y = _dot(nrm, w_ref[...])
        cos_v, sin_v = cos_ref[...], sin_ref[...]
        for h in range(heads):
            c0 = h * HEAD_W
            q_ref[:, c0:c0 + QK_NOPE] = y[:, c0:c0 + QK_NOPE].astype(BF16)
            q_ref[:, c0 + QK_NOPE:c0 + HEAD_W] = _rope(y[:, c0 + QK_NOPE:c0 + HEAD_W], cos_v, sin_v).astype(BF16)

    return pl.pallas_call(
        body, name="a_q_proj", grid=(tp // tm,),
        in_specs=[pl.BlockSpec((tm, kin), lambda i: (i, 0)),
                  pl.BlockSpec((1, kin), lambda i: (0, 0)),
                  pl.BlockSpec((kin, n), lambda i: (0, 0)),
                  pl.BlockSpec((tm, LANES), lambda i: (i, 0)),
                  pl.BlockSpec((tm, LANES), lambda i: (i, 0))],
        out_specs=pl.BlockSpec((tm, n), lambda i: (i, 0)),
        out_shape=jax.ShapeDtypeStruct((tp, n), BF16),
        compiler_params=_params(("parallel",)),
    )(q_lat, g, w_uq, cos_t, sin_t)


def _kv_proj(kv_lat, g, w_ukv, k_rope_raw, cos_t, sin_t, heads, tm):
    tp, kin = kv_lat.shape
    n = heads * (QK_NOPE + V_HEAD)

    def body(x_ref, g_ref, w_ref, kr_ref, cos_ref, sin_ref, k_ref, v_ref):
        xv = x_ref[...]
        nrm = ((xv * _rms(xv)) * g_ref[...]).astype(BF16)
        y = _dot(nrm, w_ref[...])
        kr = _rope(kr_ref[...], cos_ref[...], sin_ref[...]).astype(BF16)
        for h in range(heads):
            c0 = h * (QK_NOPE + V_HEAD)
            k_ref[:, h * HEAD_W:h * HEAD_W + QK_NOPE] = y[:, c0:c0 + QK_NOPE].astype(BF16)
            k_ref[:, h * HEAD_W + QK_NOPE:(h + 1) * HEAD_W] = kr
            v_ref[:, h * V_HEAD:(h + 1) * V_HEAD] = y[:, c0 + QK_NOPE:c0 + QK_NOPE + V_HEAD].astype(BF16)

    return pl.pallas_call(
        body, name="a_kv_proj", grid=(tp // tm,),
        in_specs=[pl.BlockSpec((tm, kin), lambda i: (i, 0)),
                  pl.BlockSpec((1, kin), lambda i: (0, 0)),
                  pl.BlockSpec((kin, n), lambda i: (0, 0)),
                  pl.BlockSpec((tm, LANES), lambda i: (i, 0)),
                  pl.BlockSpec((tm, LANES), lambda i: (i, 0)),
                  pl.BlockSpec((tm, LANES), lambda i: (i, 0))],
        out_specs=[pl.BlockSpec((tm, heads * HEAD_W), lambda i: (i, 0)),
                   pl.BlockSpec((tm, heads * V_HEAD), lambda i: (i, 0))],
        out_shape=[jax.ShapeDtypeStruct((tp, heads * HEAD_W), BF16),
                   jax.ShapeDtypeStruct((tp, heads * V_HEAD), BF16)],
        compiler_params=_params(("parallel",)),
    )(kv_lat, g, w_ukv, k_rope_raw, cos_t, sin_t)


def _as_rows(col):
    rows = col.shape[0]
    return jnp.transpose(jnp.broadcast_to(col, (rows, LANES)))[0:SUBLANES, :]


def _attn_mask(row0, col0, rows, cols, pad):
    row = row0 + lax.broadcasted_iota(jnp.int32, (rows, cols), 0)
    col = col0 + lax.broadcasted_iota(jnp.int32, (rows, cols), 1)
    return (col <= row) & (col >= pad)


LOG2E = 1.4426950408889634
FLASH_FWD_TRIPS = ((4, 2), (2, 2), (1, 1))


def _flash_fwd(q, k, v, heads, pad, tq, exchange=None):
    tp = q.shape[0]
    nq = tp // tq
    c2 = (QK_NOPE + QK_ROPE) ** -0.5 * LOG2E

    def compute(q_ref, k_ref, v_ref, o_ref, lse_ref):
        i = pl.program_id(1)

        def make_step(masked, blocks, parts=1):
            keys = blocks * tq // parts

            def step(j, carry):
                m, l, acc = carry
                offs = [pl.multiple_of(j * tq + part * keys, tq) for part in range(parts)]
                scores = [_dot_nt(q_ref[...], k_ref[pl.ds(off, keys), :]) for off in offs]
                for off, s in zip(offs, scores):
                    s = s * c2
                    if masked:
                        s = jnp.where(_attn_mask(i * tq, j * tq, tq, keys, pad), s, MASK_VALUE)
                    m_new = jnp.maximum(m, jnp.max(s, axis=-1, keepdims=True))
                    p = jnp.exp2(s - m_new)
                    alpha = jnp.exp2(m - m_new)
                    l = alpha * l + jnp.sum(p, axis=-1, keepdims=True)
                    acc = alpha * acc + _dot(p.astype(BF16), v_ref[pl.ds(off, keys), :])
                    m = m_new
                return m, l, acc
            return step

        init = (jnp.full((tq, 1), MASK_VALUE, F32), jnp.zeros((tq, 1), F32), jnp.zeros((tq, V_HEAD), F32))
        carry = make_step(True, 1)(0, init)
        first = 1
        for blocks, parts in FLASH_FWD_TRIPS:
            trips = jnp.maximum(i - first, 0) // blocks
            step_n = make_step(False, blocks, parts)
            carry = lax.fori_loop(0, trips, lambda t, cr, f=first, b=blocks, s=step_n: s(f + b * t, cr), carry)
            first = first + blocks * trips
        m, l, acc = lax.fori_loop(jnp.maximum(i, 1), i + 1, make_step(True, 1), carry)
        o_ref[...] = acc / l
        lse_ref[...] = _as_rows(m + jnp.log(l) * LOG2E)

    in_specs = [pl.BlockSpec((tq, HEAD_W), lambda h, i: (i, h)),
                pl.BlockSpec((tp, HEAD_W), lambda h, i: (0, h)),
                pl.BlockSpec((tp, V_HEAD), lambda h, i: (0, h))]
    out_specs = [pl.BlockSpec((tq, V_HEAD), lambda h, i: (i, h)),
                 pl.BlockSpec((None, None, SUBLANES, tq), lambda h, i: (h, i, 0, 0))]
    out_shape = [jax.ShapeDtypeStruct((tp, heads * V_HEAD), F32),
                 jax.ShapeDtypeStruct((heads, nq, SUBLANES, tq), F32)]
    args, scratch = [q, k, v], []
    body = _with_exchange(exchange, len(in_specs), len(out_specs),
                          lambda: (pl.program_id(0) == 0) & (pl.program_id(1) == 0),
                          lambda: (pl.program_id(0) == heads - 1) & (pl.program_id(1) == nq - 1), compute)
    if exchange is not None:
        in_specs = in_specs + exchange.specs(exchange.n_in)
        out_specs = out_specs + exchange.specs(exchange.n_out)
        out_shape = out_shape + exchange.out_shapes
        args, scratch = args + exchange.arrays, exchange.scratch()
    return pl.pallas_call(
        body, name="a_flash_fwd", grid=(heads, nq),
        in_specs=in_specs, out_specs=out_specs, out_shape=out_shape, scratch_shapes=scratch,
        compiler_params=_params(("arbitrary", "arbitrary")),
    )(*args)


def _gated_out(name, a, gate, w, resid, tm):
    tp, wd = a.shape
    d = w.shape[1]

    def body(a_ref, gate_ref, w_ref, res_ref, o_ref):
        gv = gate_ref[...]
        y = (a_ref[...] * (gv * _sigmoid(gv))).astype(BF16)
        o_ref[...] = res_ref[...] + _dot(y, w_ref[...])

    return pl.pallas_call(
        body, name=name, grid=(tp // tm,),
        in_specs=[pl.BlockSpec((tm, wd), lambda i: (i, 0)),
                  pl.BlockSpec((tm, wd), lambda i: (i, 0)),
                  pl.BlockSpec((wd, d), lambda i: (0, 0)),
                  pl.BlockSpec((tm, d), lambda i: (i, 0))],
        out_specs=pl.BlockSpec((tm, d), lambda i: (i, 0)),
        out_shape=jax.ShapeDtypeStruct((tp, d), F32),
        compiler_params=_params(("parallel",)),
    )(a, gate, w, resid)


def _lru_decay(r, sp):
    log_a = -LRU_C * r * sp
    a = jnp.exp(log_a)
    e2 = a * a
    x2 = 2.0 * log_a
    series = x2 * (1.0 + x2 * (0.5 + x2 * (1.0 / 6.0)))
    em1 = jnp.where(x2 > -0.02, series, e2 - 1.0)
    return a, e2, jnp.sqrt(-em1)


def _softplus(x):
    return jnp.maximum(x, 0.0) + jnp.log1p(jnp.exp(-jnp.abs(x)))


def _rglru_fwd(u, conv_w, conv_b, w_rg, b_rg, w_ig, b_ig, lam, pad, tm):
    tp, w = u.shape
    groups, blk = w_rg.shape[0], w_rg.shape[1]

    def body(u_ref, cw_ref, cb_ref, wr_ref, br_ref, wi_ref, bi_ref, lam_ref,
             uc_ref, r_ref, ig_ref, hs_ref, a_s, mult_ref, uext, b_s, hc):
        i = pl.program_id(0)

        @pl.when(i == 0)
        def _():
            uext[0:SUBLANES, :] = jnp.zeros((SUBLANES, w), F32)
            hc[...] = jnp.zeros((SUBLANES, w), F32)

        uext[SUBLANES:SUBLANES + tm, :] = u_ref[...]
        cw = cw_ref[...]
        uc = cb_ref[...] + uext[pl.ds(SUBLANES - 3, tm), :] * cw[0:1, :]
        uc = uc + uext[pl.ds(SUBLANES - 2, tm), :] * cw[1:2, :]
        uc = uc + uext[pl.ds(SUBLANES - 1, tm), :] * cw[2:3, :]
        uc = uc + uext[pl.ds(SUBLANES, tm), :] * cw[3:4, :]
        uc_ref[...] = uc
        uext[0:SUBLANES, :] = uext[tm:tm + SUBLANES, :]

        sp = _softplus(-lam_ref[...])
        for g in range(groups):
            sl = slice(g * blk, (g + 1) * blk)
            ucg = uc_ref[:, sl]
            ucb = ucg.astype(BF16)
            r = _sigmoid(_dot(ucb, wr_ref[g]) + br_ref[:, sl])
            ig = _sigmoid(_dot(ucb, wi_ref[g]) + bi_ref[:, sl])
            r_ref[:, sl] = r
            ig_ref[:, sl] = ig
            a, _, mult = _lru_decay(r, sp[:, sl])
            a_s[:, sl] = a
            mult_ref[:, sl] = mult
            b_s[:, sl] = mult * (ig * ucg)

        @pl.when(i == 0)
        def _():
            row = lax.broadcasted_iota(jnp.int32, (Q_BLOCK, w), 0)
            start = ig_ref[0:Q_BLOCK, :] * uc_ref[0:Q_BLOCK, :]
            b_s[0:Q_BLOCK, :] = jnp.where(row < pad, 0.0, jnp.where(row == pad, start, b_s[0:Q_BLOCK, :]))
            mult_ref[0:Q_BLOCK, :] = jnp.where(row == pad, 1.0, mult_ref[0:Q_BLOCK, :])

        row8 = lax.broadcasted_iota(jnp.int32, (SUBLANES, w), 0)

        def group(gi, h_in):
            off = pl.multiple_of(gi * SUBLANES, SUBLANES)
            av = a_s[pl.ds(off, SUBLANES), :]
            bv = b_s[pl.ds(off, SUBLANES), :]
            for k in (1, 2, 4):
                keep = row8 >= k
                bv = jnp.where(keep, av * pltpu.roll(bv, k, 0) + bv, bv)
                av = jnp.where(keep, av * pltpu.roll(av, k, 0), av)
            hv = av * h_in + bv
            hs_ref[pl.ds(off, SUBLANES), :] = hv
            return jnp.broadcast_to(hv[SUBLANES - 1:SUBLANES, :], (SUBLANES, w))

        hc[...] = lax.fori_loop(0, tm // SUBLANES, group, hc[...])

    row_spec = pl.BlockSpec((tm, w), lambda i: (i, 0))
    vec_spec = pl.BlockSpec((1, w), lambda i: (0, 0))
    mat_spec = pl.BlockSpec((groups, blk, blk), lambda i: (0, 0, 0))
    return pl.pallas_call(
        body, name="b_rglru_fwd", grid=(tp // tm,),
        in_specs=[row_spec, pl.BlockSpec((CONV_WIDTH, w), lambda i: (0, 0)), vec_spec,
                  mat_spec, vec_spec, mat_spec, vec_spec, vec_spec],
        out_specs=[row_spec] * 6,
        out_shape=[jax.ShapeDtypeStruct((tp, w), F32)] * 6,
        scratch_shapes=[pltpu.VMEM((tm + SUBLANES, w), F32), pltpu.VMEM((tm, w), F32),
                        pltpu.VMEM((SUBLANES, w), F32)],
        compiler_params=_params(("arbitrary",)),
    )(u, conv_w, conv_b, w_rg, b_rg, w_ig, b_ig, lam)


def _final_loss(h, g, target, x0):
    tp, d = h.shape
    tm = Q_BLOCK
    assert x0 % tm == 0 and target.shape[0] == tp - x0
    lead = x0 // tm

    def body(h_ref, g_ref, t_ref, dh_ref, loss_ref, dg_ref):
        i = pl.program_id(0)

        @pl.when(i == 0)
        def _():
            loss_ref[...] = jnp.zeros_like(loss_ref)
            dg_ref[...] = jnp.zeros_like(dg_ref)

        xv = h_ref[...]
        gv = g_ref[...]
        r = _rms(xv)
        xh = xv * r
        err = jnp.where(i >= lead, xh * gv - t_ref[...], 0.0)
        loss_ref[...] += 0.5 * jnp.sum(jnp.mean(err * err, axis=-1, keepdims=True))
        dy = err / d
        dg_ref[...] += jnp.sum(dy * xh, axis=0, keepdims=True)
        dxh = dy * gv
        dh_ref[...] = r * (dxh - xh * jnp.mean(dxh * xh, axis=-1, keepdims=True))

    return pl.pallas_call(
        body, name="final_loss", grid=(tp // tm,),
        in_specs=[pl.BlockSpec((tm, d), lambda i: (i, 0)),
                  pl.BlockSpec((1, d), lambda i: (0, 0)),
                  pl.BlockSpec((tm, d), lambda i: (jnp.maximum(i - lead, 0), 0))],
        out_specs=[pl.BlockSpec((tm, d), lambda i: (i, 0)),
                   pl.BlockSpec((SUBLANES, LANES), lambda i: (0, 0)),
                   pl.BlockSpec((1, d), lambda i: (0, 0))],
        out_shape=[jax.ShapeDtypeStruct((tp, d), F32),
                   jax.ShapeDtypeStruct((SUBLANES, LANES), F32),
                   jax.ShapeDtypeStruct((1, d), F32)],
        compiler_params=_params(("arbitrary",)),
    )(h, g, target)


def _with_exchange(exchange, n_in, n_out, first, last, compute):
    if exchange is None:
        return compute
    ex_in, ex_out = exchange.n_in, exchange.n_out

    def body(*refs):
        own_in, their_in = refs[:n_in], refs[n_in:n_in + ex_in]
        pos = n_in + ex_in
        own_out, their_out = refs[pos:pos + n_out], refs[pos + n_out:pos + n_out + ex_out]
        rest = refs[pos + n_out + ex_out:]
        own_scratch, sems = rest[:len(rest) - 2], rest[len(rest) - 2:]

        @pl.when(first())
        def _():
            exchange.start(their_in, their_out, sems)

        compute(*own_in, *own_out, *own_scratch)

        @pl.when(last())
        def _():
            exchange.finish(their_in, their_out, sems)

    return body


def _gated_out_bwd(name, dout, a, gate, w, tm, delta_heads=0, exchange=None):
    tp, wd = a.shape
    d = w.shape[1]
    nt = tp // tm

    def compute(do_ref, a_ref, gate_ref, w_ref, da_ref, dgate_ref, dw_ref, *delta_ref):
        @pl.when(pl.program_id(0) == 0)
        def _():
            dw_ref[...] = jnp.zeros_like(dw_ref)

        gv = gate_ref[...]
        av = a_ref[...]
        sg = _sigmoid(gv)
        silu = gv * sg
        dob = do_ref[...].astype(BF16)
        dy = _dot_nt(dob, w_ref[...])
        da = dy * silu
        da_ref[...] = da
        dgate_ref[...] = dy * av * (sg * (1.0 + gv * (1.0 - sg)))
        dw_ref[...] += _dot_tn((av * silu).astype(BF16), dob)
        for h in range(delta_heads):
            sl = slice(h * V_HEAD, (h + 1) * V_HEAD)
            delta_ref[0][h] = _as_rows(jnp.sum(da[:, sl] * av[:, sl], axis=-1, keepdims=True))

    out_specs = [pl.BlockSpec((tm, wd), lambda i: (i, 0)),
                 pl.BlockSpec((tm, wd), lambda i: (i, 0)),
                 pl.BlockSpec((wd, d), lambda i: (0, 0))]
    out_shape = [jax.ShapeDtypeStruct((tp, wd), F32),
                 jax.ShapeDtypeStruct((tp, wd), F32),
                 jax.ShapeDtypeStruct((wd, d), F32)]
    if delta_heads:
        out_specs.append(pl.BlockSpec((delta_heads, None, SUBLANES, tm), lambda i: (0, i, 0, 0)))
        out_shape.append(jax.ShapeDtypeStruct((delta_heads, tp // tm, SUBLANES, tm), F32))
    in_specs = [pl.BlockSpec((tm, d), lambda i: (i, 0)),
                pl.BlockSpec((tm, wd), lambda i: (i, 0)),
                pl.BlockSpec((tm, wd), lambda i: (i, 0)),
                pl.BlockSpec((wd, d), lambda i: (0, 0))]
    args, scratch = [dout, a, gate, w], []
    body = _with_exchange(exchange, len(in_specs), len(out_specs),
                          lambda: pl.program_id(0) == 0, lambda: pl.program_id(0) == nt - 1, compute)
    if exchange is not None:
        in_specs = in_specs + exchange.specs(exchange.n_in)
        out_specs = out_specs + exchange.specs(exchange.n_out)
        out_shape = out_shape + exchange.out_shapes
        args, scratch = args + exchange.arrays, exchange.scratch()
    return pl.pallas_call(
        body, name=name, grid=(nt,),
        in_specs=in_specs, out_specs=out_specs, out_shape=out_shape, scratch_shapes=scratch,
        compiler_params=_params(("arbitrary",)),
    )(*args)


def _rglru_bwd(dhs, hs, r, ig, uc, u, a, mult, conv_w, w_rg, w_ig, lam, pad, tm):
    tp, w = u.shape
    groups, blk = w_rg.shape[0], w_rg.shape[1]
    nt = tp // tm
    per8 = tm // SUBLANES

    def body(dhs_ref, hs_ref, hprev_ref, r_ref, ig_ref, uc_ref, u_ref, uprev_ref, a_ref, mult_ref,
             cw_ref, wr_ref, wi_ref, lam_ref,
             du_ref, dcw_ref, dcb_ref, dwr_ref, dbr_ref, dwi_ref, dbi_ref, dlam_ref,
             aext, c_s, g_s, hext, uext, ducext, gc):
        step = pl.program_id(0)
        ti = nt - 1 - step

        @pl.when(step == 0)
        def _():
            for ref in (dcw_ref, dcb_ref, dwr_ref, dbr_ref, dwi_ref, dbi_ref, dlam_ref):
                ref[...] = jnp.zeros_like(ref)
            aext[tm:tm + SUBLANES, :] = jnp.zeros((SUBLANES, w), F32)
            ducext[tm:tm + SUBLANES, :] = jnp.zeros((SUBLANES, w), F32)
            gc[...] = jnp.zeros((SUBLANES, w), F32)

        lam_v = lam_ref[...]
        sp = _softplus(-lam_v)
        row = ti * tm + lax.broadcasted_iota(jnp.int32, (tm, w), 0)

        rv = r_ref[...]
        a = a_ref[...]
        mult = mult_ref[...]
        aext[0:tm, :] = a
        c_s[...] = aext[pl.ds(1, tm), :]
        row8 = lax.broadcasted_iota(jnp.int32, (SUBLANES, w), 0)

        def group(gi, g_in):
            off = pl.multiple_of((per8 - 1 - gi) * SUBLANES, SUBLANES)
            cv = c_s[pl.ds(off, SUBLANES), :]
            dv = dhs_ref[pl.ds(off, SUBLANES), :]
            for k in (1, 2, 4):
                keep = row8 < SUBLANES - k
                dv = jnp.where(keep, cv * pltpu.roll(dv, SUBLANES - k, 0) + dv, dv)
                cv = jnp.where(keep, cv * pltpu.roll(cv, SUBLANES - k, 0), cv)
            gv = cv * g_in + dv
            g_s[pl.ds(off, SUBLANES), :] = gv
            return jnp.broadcast_to(gv[0:1, :], (SUBLANES, w))

        gc[...] = lax.fori_loop(0, per8, group, gc[...])
        aext[tm:tm + SUBLANES, :] = aext[0:SUBLANES, :]

        gsc = jnp.where(row < pad, 0.0, g_s[...])
        hext[0:SUBLANES, :] = hprev_ref[...]
        hext[SUBLANES:SUBLANES + tm, :] = hs_ref[...]
        hprev = jnp.where(row == 0, 0.0, hext[pl.ds(SUBLANES - 1, tm), :])
        igv = ig_ref[...]
        ucv = uc_ref[...]
        first = row == pad
        dmult = gsc * (igv * ucv)
        dig = gsc * mult * ucv
        duc = gsc * mult * igv
        dlog_a = (gsc * hprev) * a + jnp.where(first, 0.0, dmult * (-(a * a) / mult))
        dlam_ref[...] += jnp.sum(dlog_a * rv, axis=0, keepdims=True) * (LRU_C * _sigmoid(-lam_v))
        dpre_r = dlog_a * (-LRU_C * sp) * (rv * (1.0 - rv))
        dpre_i = dig * (igv * (1.0 - igv))
        dbr_ref[...] += jnp.sum(dpre_r, axis=0, keepdims=True)
        dbi_ref[...] += jnp.sum(dpre_i, axis=0, keepdims=True)
        for g in range(groups):
            sl = slice(g * blk, (g + 1) * blk)
            ucb = ucv[:, sl].astype(BF16)
            drb = dpre_r[:, sl].astype(BF16)
            dib = dpre_i[:, sl].astype(BF16)
            dwr_ref[g] += _dot_tn(ucb, drb)
            dwi_ref[g] += _dot_tn(ucb, dib)
            ducext[0:tm, sl] = duc[:, sl] + _dot_nt(drb, wr_ref[g]) + _dot_nt(dib, wi_ref[g])

        ducv = ducext[0:tm, :]
        cw = cw_ref[...]
        dcb_ref[...] += jnp.sum(ducv, axis=0, keepdims=True)
        uext[0:SUBLANES, :] = jnp.where(ti == 0, 0.0, uprev_ref[...])
        uext[SUBLANES:SUBLANES + tm, :] = u_ref[...]
        for j in range(CONV_WIDTH):
            ush = uext[pl.ds(SUBLANES - (CONV_WIDTH - 1 - j), tm), :]
            dcw_ref[j:j + 1, :] += jnp.sum(ducv * ush, axis=0, keepdims=True)
        du = ducv * cw[3:4, :]
        for k in range(1, CONV_WIDTH):
            du = du + ducext[pl.ds(k, tm), :] * cw[3 - k:4 - k, :]
        du_ref[...] = du
        ducext[tm:tm + SUBLANES, :] = ducext[0:SUBLANES, :]

    rev = lambda s: (nt - 1 - s, 0)
    halo = lambda s: (jnp.maximum((nt - 1 - s) * per8 - 1, 0), 0)
    row_spec = pl.BlockSpec((tm, w), rev)
    halo_spec = pl.BlockSpec((SUBLANES, w), halo)
    vec_spec = pl.BlockSpec((1, w), lambda s: (0, 0))
    mat_spec = pl.BlockSpec((groups, blk, blk), lambda s: (0, 0, 0))
    cw_spec = pl.BlockSpec((CONV_WIDTH, w), lambda s: (0, 0))
    return pl.pallas_call(
        body, name="b_rglru_bwd", grid=(nt,),
        in_specs=[row_spec, row_spec, halo_spec, row_spec, row_spec, row_spec, row_spec, halo_spec, row_spec, row_spec,
                  cw_spec, mat_spec, mat_spec, vec_spec],
        out_specs=[row_spec, cw_spec, vec_spec, mat_spec, vec_spec, mat_spec, vec_spec, vec_spec],
        out_shape=[jax.ShapeDtypeStruct((tp, w), F32), jax.ShapeDtypeStruct((CONV_WIDTH, w), F32),
                   jax.ShapeDtypeStruct((1, w), F32), jax.ShapeDtypeStruct((groups, blk, blk), F32),
                   jax.ShapeDtypeStruct((1, w), F32), jax.ShapeDtypeStruct((groups, blk, blk), F32),
                   jax.ShapeDtypeStruct((1, w), F32), jax.ShapeDtypeStruct((1, w), F32)],
        scratch_shapes=[pltpu.VMEM((tm + SUBLANES, w), F32), pltpu.VMEM((tm, w), F32), pltpu.VMEM((tm, w), F32),
                        pltpu.VMEM((tm + SUBLANES, w), F32), pltpu.VMEM((tm + SUBLANES, w), F32),
                        pltpu.VMEM((tm + SUBLANES, w), F32), pltpu.VMEM((SUBLANES, w), F32)],
        compiler_params=_params(("arbitrary",)),
    )(dhs, hs, hs, r, ig, uc, u, u, a, mult, conv_w, w_rg, w_ig, lam)


def _norm_matmul_bwd(name, x, g, w, dys, tm, resid=None, prologue=None, extra_out=None, exchange=None):
    tp, kin = x.shape
    n = w.shape[1]
    nt = tp // tm
    n_dy = len(dys)
    has_res = resid is not None
    has_extra = extra_out is not None

    def compute(*refs):
        x_ref, g_ref, w_ref = refs[:3]
        dy_refs = refs[3:3 + n_dy]
        pos = 3 + n_dy
        res_ref = refs[pos] if has_res else None
        pos += int(has_res)
        dx_ref, dw_ref, dg_ref = refs[pos:pos + 3]
        pos += 3
        ex_ref = refs[pos] if has_extra else None
        pos += int(has_extra)
        dy_s = refs[pos]

        @pl.when(pl.program_id(0) == 0)
        def _():
            dw_ref[...] = jnp.zeros_like(dw_ref)
            dg_ref[...] = jnp.zeros_like(dg_ref)

        if prologue is None:
            c0 = 0
            for ref in dy_refs:
                dy_s[:, c0:c0 + ref.shape[1]] = ref[...].astype(BF16)
                c0 += ref.shape[1]
        else:
            prologue(dy_refs, dy_s, ex_ref)

        xv = x_ref[...]
        gv = g_ref[...]
        r = _rms(xv)
        xh = xv * r
        dyb = dy_s[...]
        dn = _dot_nt(dyb, w_ref[...])
        dw_ref[...] += _dot_tn((xh * gv).astype(BF16), dyb)
        dg_ref[...] += jnp.sum(dn * xh, axis=0, keepdims=True)
        dxh = dn * gv
        dx = r * (dxh - xh * jnp.mean(dxh * xh, axis=-1, keepdims=True))
        if has_res:
            dx = dx + res_ref[...]
        dx_ref[...] = dx

    row = lambda width: pl.BlockSpec((tm, width), lambda i: (i, 0))
    in_specs = [row(kin), pl.BlockSpec((1, kin), lambda i: (0, 0)), pl.BlockSpec((kin, n), lambda i: (0, 0))]
    in_specs += [row(a.shape[1]) for a in dys]
    args = [x, g, w, *dys]
    if has_res:
        in_specs.append(row(kin))
        args.append(resid)
    out_specs = [row(kin), pl.BlockSpec((kin, n), lambda i: (0, 0)), pl.BlockSpec((1, kin), lambda i: (0, 0))]
    out_shape = [jax.ShapeDtypeStruct((tp, kin), F32), jax.ShapeDtypeStruct((kin, n), F32),
                 jax.ShapeDtypeStruct((1, kin), F32)]
    if has_extra:
        out_specs.append(row(extra_out[0]))
        out_shape.append(jax.ShapeDtypeStruct((tp, extra_out[0]), extra_out[1]))
    scratch = [pltpu.VMEM((tm, n), BF16)]
    body = _with_exchange(exchange, len(in_specs), len(out_specs),
                          lambda: pl.program_id(0) == 0, lambda: pl.program_id(0) == nt - 1, compute)
    if exchange is not None:
        in_specs = in_specs + exchange.specs(exchange.n_in)
        out_specs = out_specs + exchange.specs(exchange.n_out)
        out_shape = out_shape + exchange.out_shapes
        args, scratch = args + exchange.arrays, scratch + exchange.scratch()
    return pl.pallas_call(
        body, name=name, grid=(nt,),
        in_specs=in_specs, out_specs=out_specs, out_shape=out_shape, scratch_shapes=scratch,
        compiler_params=_params(("arbitrary",)),
    )(*args)


def _flash_bwd(q, k, v, lse, delta, do, heads, pad, tq, exchange=None):
    tp = q.shape[0]
    nq = tp // tq
    scale = (QK_NOPE + QK_ROPE) ** -0.5
    c2 = scale * LOG2E

    def compute(q_ref, k_ref, v_ref, lse_ref, delta_ref, do_ref, dq_ref, dk_ref, dv_ref):
        j = pl.program_id(1)

        @pl.when(j == 0)
        def _():
            dq_ref[...] = jnp.zeros_like(dq_ref)

        kv = k_ref[...]
        vv = v_ref[...]

        def rows_of(ref, i, blocks):
            parts = [ref[i + b][0:1, :] for b in range(blocks)]
            return parts[0] if blocks == 1 else jnp.concatenate(parts, axis=1)

        def make_step(masked, blocks):
            def step(i, carry):
                dk, dv = carry
                off = pl.multiple_of(i * tq, tq)
                qv = q_ref[pl.ds(off, blocks * tq), :]
                dob = do_ref[pl.ds(off, blocks * tq), :].astype(BF16)
                p = jnp.exp2(_dot_nt(kv, qv) * c2 - rows_of(lse_ref, i, blocks))
                if masked:
                    key = j * tq + lax.broadcasted_iota(jnp.int32, (tq, tq), 0)
                    qry = j * tq + lax.broadcasted_iota(jnp.int32, (tq, tq), 1)
                    first = jnp.where((key <= qry) & (key >= pad), p[:, :tq], 0.0)
                    p = first if blocks == 1 else jnp.concatenate([first, p[:, tq:]], axis=1)
                dv = dv + _dot(p.astype(BF16), dob)
                dp = _dot_nt(vv, dob)
                ds = (p * (dp - rows_of(delta_ref, i, blocks)) * scale).astype(BF16)
                dk = dk + _dot(ds, qv)
                dq_ref[pl.ds(off, blocks * tq), :] += _dot_tn(ds, kv)
                return dk, dv
            return step

        odd = (nq - j) % 2
        carry = (jnp.zeros((tq, HEAD_W), F32), jnp.zeros((tq, V_HEAD), F32))
        carry = lax.fori_loop(0, odd, lambda t, cr: make_step(True, 1)(j, cr), carry)
        carry = lax.fori_loop(0, 1 - odd, lambda t, cr: make_step(True, 2)(j, cr), carry)
        start = j + 2 - odd
        for blocks in (4, 2):
            trips = (nq - start) // blocks
            step_n = make_step(False, blocks)
            carry = lax.fori_loop(0, trips, lambda t, cr, s=start, b=blocks, f=step_n: f(s + b * t, cr), carry)
            start = start + blocks * trips
        dk, dv = carry
        dk_ref[...] = dk
        dv_ref[...] = dv

    in_specs = [pl.BlockSpec((tp, HEAD_W), lambda h, j: (0, h)),
                pl.BlockSpec((tq, HEAD_W), lambda h, j: (j, h)),
                pl.BlockSpec((tq, V_HEAD), lambda h, j: (j, h)),
                pl.BlockSpec((None, nq, SUBLANES, tq), lambda h, j: (h, 0, 0, 0)),
                pl.BlockSpec((None, nq, SUBLANES, tq), lambda h, j: (h, 0, 0, 0)),
                pl.BlockSpec((tp, V_HEAD), lambda h, j: (0, h))]
    out_specs = [pl.BlockSpec((tp, HEAD_W), lambda h, j: (0, h)),
                 pl.BlockSpec((tq, HEAD_W), lambda h, j: (j, h)),
                 pl.BlockSpec((tq, V_HEAD), lambda h, j: (j, h))]
    out_shape = [jax.ShapeDtypeStruct((tp, heads * HEAD_W), F32),
                 jax.ShapeDtypeStruct((tp, heads * HEAD_W), F32),
                 jax.ShapeDtypeStruct((tp, heads * V_HEAD), F32)]
    args, scratch = [q, k, v, lse, delta, do], []
    body = _with_exchange(exchange, len(in_specs), len(out_specs),
                          lambda: (pl.program_id(0) == 0) & (pl.program_id(1) == 0),
                          lambda: (pl.program_id(0) == heads - 1) & (pl.program_id(1) == nq - 1), compute)
    if exchange is not None:
        in_specs = in_specs + exchange.specs(exchange.n_in)
        out_specs = out_specs + exchange.specs(exchange.n_out)
        out_shape = out_shape + exchange.out_shapes
        args, scratch = args + exchange.arrays, exchange.scratch()
    return pl.pallas_call(
        body, name="a_flash_bwd", grid=(heads, nq),
        in_specs=in_specs, out_specs=out_specs, out_shape=out_shape, scratch_shapes=scratch,
        compiler_params=_params(("arbitrary", "arbitrary")),
    )(*args)


def _position():
    return lax.axis_index("x"), lax.axis_index("y"), lax.axis_index("c")


def _other_chips(x, y):
    return [(1 - x, y), (x, 1 - y), (1 - x, 1 - y)]


def _block(ref, shard_axis, n, k, split_axis=None, m=None, h=None):
    idx = []
    for a in range(len(ref.shape)):
        start = 0
        size = None
        if a == shard_axis:
            start, size = k * n, n
        if a == split_axis:
            size = (n if a == shard_axis else m) // 2
            start = start + h * size
        idx.append(slice(None) if size is None else pl.ds(start, size))
    return ref.at[tuple(idx)]


def _gather_weights(split, whole_small):
    ns, nw = len(split), len(whole_small)
    n = ns + nw
    arrs = [s[0] for s in split] + [s[0] for s in whole_small]
    axes = [s[1] for s in split] + [s[1] for s in whole_small]

    def body(*refs):
        ins, outs = refs[:n], refs[n:2 * n]
        ici_send, ici_recv, d2d_send, d2d_recv, sib_send, sib_recv = refs[2 * n:]
        x, y, c = _position()
        me = 2 * x + y
        others = _other_chips(x, y)
        sent, local = [], []

        def remote(src, dst, sems, idx, to):
            return pltpu.make_async_remote_copy(src_ref=src, dst_ref=dst, send_sem=sems[0].at[idx],
                                                recv_sem=sems[1].at[idx], device_id=to, device_id_type=MESH)

        for a in range(n):
            width = ins[a].shape[axes[a]]
            mine = remote(ins[a], _block(outs[a], axes[a], width, me), (sib_send, sib_recv), a, (x, y, 1 - c))
            mine.start()
            local.append(mine)
            for j, (px, py) in enumerate(others):
                if a < ns:
                    sx = split[a][2]
                    src = _block(ins[a], None, None, None, sx, ins[a].shape[sx], c)
                    dst = _block(outs[a], axes[a], width, me, sx, outs[a].shape[sx], c)
                else:
                    src, dst = ins[a], _block(outs[a], axes[a], width, me)
                cp = remote(src, dst, (ici_send, ici_recv), 3 * a + j, (px, py, c))
                cp.start()
                sent.append(cp)
        for a in range(ns):
            width = ins[a].shape[axes[a]]
            sx = split[a][2]
            for j, (px, py) in enumerate(others):
                theirs = _block(outs[a], axes[a], width, 2 * px + py, sx, outs[a].shape[sx], c)
                remote(theirs, theirs, (ici_send, ici_recv), 3 * a + j, (px, py, c)).wait_recv()
                fwd = remote(theirs, theirs, (d2d_send, d2d_recv), 3 * a + j, (x, y, 1 - c))
                fwd.start()
                sent.append(fwd)
        for a in range(ns, n):
            width = ins[a].shape[axes[a]]
            for j, (px, py) in enumerate(others):
                theirs = _block(outs[a], axes[a], width, 2 * px + py)
                remote(theirs, theirs, (ici_send, ici_recv), 3 * a + j, (px, py, c)).wait_recv()
        for a in range(ns):
            width = ins[a].shape[axes[a]]
            sx = split[a][2]
            for j, (px, py) in enumerate(others):
                from_sibling = _block(outs[a], axes[a], width, 2 * px + py, sx, outs[a].shape[sx], 1 - c)
                remote(from_sibling, from_sibling, (d2d_send, d2d_recv), 3 * a + j, (x, y, 1 - c)).wait_recv()
        for cp in sent:
            cp.wait_send()
        for cp in local:
            cp.wait()

    def whole_shape(a, axis):
        shape = list(a.shape)
        shape[axis] *= N_CHIPS
        return jax.ShapeDtypeStruct(tuple(shape), a.dtype)

    any_spec = pl.BlockSpec(memory_space=pl.ANY)
    return pl.pallas_call(
        body, name="gather_weights",
        in_specs=[any_spec] * n, out_specs=[any_spec] * n,
        out_shape=[whole_shape(a, ax) for a, ax in zip(arrs, axes)],
        scratch_shapes=[pltpu.SemaphoreType.DMA((3 * n,)), pltpu.SemaphoreType.DMA((3 * n,)),
                        pltpu.SemaphoreType.DMA((3 * ns,)), pltpu.SemaphoreType.DMA((3 * ns,)),
                        pltpu.SemaphoreType.DMA((n,)), pltpu.SemaphoreType.DMA((n,))],
        compiler_params=pltpu.CompilerParams(has_side_effects=True),
    )(*arrs)


class _Grad:
    def __init__(self, name, g, kind, rh, cols, groups=None):
        self.name, self.g, self.kind, self.rh, self.cols, self.groups = name, g, kind, rh, cols, groups
        if kind == 'rows':
            self.tr = rh
        elif kind == 'gate':
            self.tr = rh // (groups // 2)
        else:
            self.tr = rh if rh <= 512 else 256
        self.nb = rh // self.tr

    def pieces(self, ref, k, h):
        rh, cols = self.rh, self.cols
        if self.kind == 'cols':
            return [(ref.at[pl.ds(h * rh, rh), pl.ds(k * cols, cols)], 0, rh)]
        if self.kind == 'rows':
            return [(ref.at[pl.ds((2 * k + h) * rh, rh), :], 0, rh)]
        if self.kind == 'lead':
            return [(ref.at[k, pl.ds(h * rh, rh), :], 0, rh)]
        per = self.groups // 2
        return [(ref.at[pl.ds((((h * per + gi) * N_CHIPS) + k) * self.tr, self.tr), :], gi * self.tr, self.tr)
                for gi in range(per)]

    def block_spec(self):
        tr, nb, cols = self.tr, self.nb, self.cols
        if self.kind == 'cols':
            return pl.BlockSpec((tr, cols), lambda k, i, c: (c[0] * nb + i, k))
        if self.kind == 'rows':
            return pl.BlockSpec((tr, cols), lambda k, i, c: (2 * k + c[0], 0))
        if self.kind == 'lead':
            return pl.BlockSpec((None, tr, cols), lambda k, i, c: (k, c[0] * nb + i, 0))
        return pl.BlockSpec((tr, cols), lambda k, i, c: ((c[0] * nb + i) * N_CHIPS + k, 0))


class _Exchange:
    def __init__(self, name, arrays, out_shapes, n_copies, copies):
        self.name, self.arrays, self.out_shapes, self.n_copies, self.copies = name, arrays, out_shapes, n_copies, copies
        self.n_in, self.n_out = len(arrays), len(out_shapes)

    def specs(self, n):
        return [pl.BlockSpec(memory_space=pl.ANY)] * n

    def scratch(self):
        return [pltpu.SemaphoreType.DMA((self.n_copies,)), pltpu.SemaphoreType.DMA((self.n_copies,))]

    def _descriptors(self, in_refs, out_refs, sems):
        return self.copies(in_refs, out_refs, sems[0], sems[1])

    def start(self, in_refs, out_refs, sems):
        for cp in self._descriptors(in_refs, out_refs, sems):
            cp.start()

    def finish(self, in_refs, out_refs, sems):
        for cp in self._descriptors(in_refs, out_refs, sems):
            cp.wait()

    def __add__(self, other):
        def copies(ins, outs, send_sems, recv_sems, base=0):
            return (self.copies(ins[:self.n_in], outs[:self.n_out], send_sems, recv_sems, base)
                    + other.copies(ins[self.n_in:], outs[self.n_out:], send_sems, recv_sems, base + self.n_copies))

        return _Exchange(self.name + "_" + other.name, self.arrays + other.arrays, self.out_shapes + other.out_shapes,
                         self.n_copies + other.n_copies, copies)

    def run(self):
        def body(*refs):
            ins, outs, sems = refs[:self.n_in], refs[self.n_in:self.n_in + self.n_out], refs[self.n_in + self.n_out:]
            self.start(ins, outs, sems)
            self.finish(ins, outs, sems)

        return pl.pallas_call(
            body, name=self.name,
            in_specs=self.specs(self.n_in), out_specs=self.specs(self.n_out), out_shape=self.out_shapes,
            scratch_shapes=self.scratch(),
            compiler_params=pltpu.CompilerParams(has_side_effects=True),
        )(*self.arrays)


def _gather_whole(name, shards):
    def copies(ins, outs, send_sems, recv_sems, base=0):
        x, y, c = _position()
        me = 2 * x + y
        made = []
        for a, (_, axis) in enumerate(shards):
            dst = _block(outs[a], axis, ins[a].shape[axis], me)
            for j, to in enumerate([(x, y, 1 - c)] + [(px, py, c) for px, py in _other_chips(x, y)]):
                idx = base + 4 * a + j
                made.append(pltpu.make_async_remote_copy(
                    src_ref=ins[a], dst_ref=dst, send_sem=send_sems.at[idx], recv_sem=recv_sems.at[idx],
                    device_id=to, device_id_type=MESH))
        return made

    def whole_shape(a, axis):
        shape = list(a.shape)
        shape[axis] *= N_CHIPS
        return jax.ShapeDtypeStruct(tuple(shape), a.dtype)

    return _Exchange(name, [s[0] for s in shards], [whole_shape(*s) for s in shards], 4 * len(shards), copies)


def _halves_to_sibling(name, grads):
    total = sum(len(gr.pieces(gr.g, 0, 0)) * N_CHIPS for gr in grads)

    def copies(ins, outs, send_sems, recv_sems, base=0):
        x, y, c = _position()
        made = []
        for gr, g_ref, got_ref in zip(grads, ins, outs):
            for k in range(N_CHIPS):
                for src, r0, nr in gr.pieces(g_ref, k, 1 - c):
                    idx = base + len(made)
                    made.append(pltpu.make_async_remote_copy(
                        src_ref=src, dst_ref=got_ref.at[k, pl.ds(r0, nr), :],
                        send_sem=send_sems.at[idx], recv_sem=recv_sems.at[idx],
                        device_id=(x, y, 1 - c), device_id_type=MESH))
        return made

    return _Exchange(name, [gr.g for gr in grads],
                     [jax.ShapeDtypeStruct((N_CHIPS, gr.rh, gr.cols), F32) for gr in grads], total, copies)


def _chip_sum(gr, got, c):
    def body(c_ref, g_ref, got_ref, o_ref):
        o_ref[...] = (g_ref[...] + got_ref[...]).astype(BF16)

    tile = pl.BlockSpec((None, gr.tr, gr.cols), lambda k, i, c_ref: (k, i, 0))
    return pl.pallas_call(
        body, name="chip_sum_" + gr.name,
        grid_spec=pltpu.PrefetchScalarGridSpec(
            num_scalar_prefetch=1, grid=(N_CHIPS, gr.nb),
            in_specs=[gr.block_spec(), tile], out_specs=tile),
        out_shape=jax.ShapeDtypeStruct((N_CHIPS, gr.rh, gr.cols), BF16),
        compiler_params=_params(("parallel", "parallel")),
    )(c, gr.g, got)


def _blocks_to_chips(name, parts):
    n = len(parts)

    def copies(ins, outs, send_sems, recv_sems, base=0):
        x, y, c = _position()
        made = []
        for a in range(n):
            for j, (px, py) in enumerate(_other_chips(x, y)):
                idx = base + 3 * a + j
                made.append(pltpu.make_async_remote_copy(
                    src_ref=ins[a].at[2 * px + py], dst_ref=outs[a].at[j],
                    send_sem=send_sems.at[idx], recv_sem=recv_sems.at[idx],
                    device_id=(px, py, c), device_id_type=MESH))
        return made

    return _Exchange(name, parts, [jax.ShapeDtypeStruct((3,) + p.shape[1:], p.dtype) for p in parts], 3 * n, copies)


def _sum_chips(name, part, got, me, c):
    nk, rh, cols = part.shape
    tr = rh if rh <= 512 else 256
    nb = rh // tr

    def body(me_ref, c_ref, own_ref, *rest):
        got_refs, o_ref = rest[:nk], rest[nk]
        own = own_ref[...].astype(F32)
        acc = None
        for k in range(nk):
            term = jnp.where(me_ref[0] == k, own, got_refs[k][...].astype(F32))
            acc = term if acc is None else acc + term
        o_ref[...] = acc

    def got_map(k):
        def index(i, me_ref, c_ref):
            xor = jnp.bitwise_xor(me_ref[0], k)
            slot = jnp.where(xor == 1, 1, jnp.where(xor == 3, 2, 0))
            return (slot, i, 0)
        return index

    return pl.pallas_call(
        body, name="sum_" + name,
        grid_spec=pltpu.PrefetchScalarGridSpec(
            num_scalar_prefetch=2, grid=(nb,),
            in_specs=[pl.BlockSpec((None, tr, cols), lambda i, me_ref, c_ref: (me_ref[0], i, 0))]
            + [pl.BlockSpec((None, tr, cols), got_map(k)) for k in range(nk)],
            out_specs=pl.BlockSpec((tr, cols), lambda i, me_ref, c_ref: (c_ref[0] * nb + i, 0))),
        out_shape=jax.ShapeDtypeStruct((2 * rh, cols), F32),
        compiler_params=_params(("parallel",)),
    )(me, c, part, *([got] * nk))


def _share_with_sibling(halves):
    n = len(halves)

    def body(*refs):
        outs = refs[n:2 * n]
        send_sems, recv_sems = refs[2 * n:]
        x, y, c = _position()
        copies = []
        for a in range(n):
            rh = outs[a].shape[0] // 2
            mine = outs[a].at[pl.ds(c * rh, rh), :]
            cp = pltpu.make_async_remote_copy(
                src_ref=mine, dst_ref=mine, send_sem=send_sems.at[a], recv_sem=recv_sems.at[a],
                device_id=(x, y, 1 - c), device_id_type=MESH)
            cp.start()
            copies.append(cp)
        for cp in copies:
            cp.wait()

    any_spec = pl.BlockSpec(memory_space=pl.ANY)
    return pl.pallas_call(
        body, name="grads_share",
        in_specs=[any_spec] * n, out_specs=[any_spec] * n,
        out_shape=[jax.ShapeDtypeStruct(h.shape, h.dtype) for h in halves],
        input_output_aliases={a: a for a in range(n)},
        scratch_shapes=[pltpu.SemaphoreType.DMA((n,)), pltpu.SemaphoreType.DMA((n,))],
        compiler_params=pltpu.CompilerParams(has_side_effects=True),
    )(*halves)


def _adamw(name, w, g, m, v):
    rows, cols = w.shape
    tr = 256 if rows % 256 == 0 else rows

    def body(w_ref, g_ref, m_ref, v_ref, d_ref, nm_ref, nv_ref):
        gv = g_ref[...]
        mn = ADAM_B1 * m_ref[...] + (1.0 - ADAM_B1) * gv
        vn = ADAM_B2 * v_ref[...] + (1.0 - ADAM_B2) * (gv * gv)
        m_hat = mn / (1.0 - ADAM_B1 ** ADAM_STEP)
        v_hat = vn / (1.0 - ADAM_B2 ** ADAM_STEP)
        d_ref[...] = -ADAM_LR * (m_hat / (jnp.sqrt(v_hat) + ADAM_EPS) + ADAM_WD * w_ref[...])
        nm_ref[...] = mn
        nv_ref[...] = vn

    spec = pl.BlockSpec((tr, cols), lambda i: (i, 0))
    return pl.pallas_call(
        body, name=name, grid=(rows // tr,),
        in_specs=[spec] * 4, out_specs=[spec] * 3,
        out_shape=[jax.ShapeDtypeStruct((rows, cols), F32)] * 3,
        compiler_params=_params(("parallel",)),
    )(w, g, m, v)


def _as2d(a):
    if a.ndim == 1:
        return a.reshape(1, -1)
    return a.reshape(-1, a.shape[-1])


def _unshard(gathered, axis):
    moved = jnp.moveaxis(gathered, 0, axis)
    shape = list(gathered.shape[1:])
    shape[axis] *= N_CHIPS
    return moved.reshape(shape)


def _rope_tables(tp, pad):
    pos = jnp.arange(tp, dtype=F32) - pad
    inv_freq = ROPE_BASE ** (-jnp.arange(0, QK_ROPE, 2, dtype=F32) / QK_ROPE)
    ang = pos[:, None] * inv_freq[None, :]
    cos, sin = jnp.cos(ang), jnp.sin(ang)
    zeros = jnp.zeros((tp, LANES - QK_ROPE), F32)
    return jnp.concatenate([cos, cos, zeros], axis=1), jnp.concatenate([-sin, sin, zeros], axis=1)


def _matrix_grad(name, g, heads):
    if name in ('b_w_rg', 'b_w_ig'):
        groups, blk, cols = g.shape
        return _Grad(name, g.reshape(groups * blk, cols), 'gate', (groups // 2) * (blk // N_CHIPS), cols, groups)
    rows, cols = g.shape
    if name in ('a_w_out', 'b_w_out'):
        return _Grad(name, g, 'rows', rows // (2 * N_CHIPS), cols)
    if name == 'a_w_uq' and heads % N_CHIPS != 0:
        g = g.reshape(rows, heads, HEAD_W)[:, :, :QK_NOPE + QK_ROPE].reshape(rows, -1)
        cols = g.shape[1]
    if name == 'a_w_in' or (name == 'a_w_uq' and heads % N_CHIPS != 0):
        g = jnp.moveaxis(g.reshape(rows, N_CHIPS, cols // N_CHIPS), 1, 0)
        return _Grad(name, g, 'lead', rows // 2, cols // N_CHIPS)
    return _Grad(name, g, 'cols', rows // 2, cols // N_CHIPS)


def _kernel_form(name, w):
    return w[0] if name in ('b_w_rg', 'b_w_ig', 'b_conv_w') else _as2d(w)


def _local_grads(x, target, wt, heads, c_idx, late_names, late_gather):
    wt = dict(wt)
    seq, d = x.shape
    n_meta = wt['meta_tokens'].shape[0]
    t = seq + n_meta
    pad = (-t) % Q_BLOCK
    tp = t + pad
    x0 = pad + n_meta
    tm = _row_tile(tp)
    ql = wt['a_q_norm_g'].shape[1]
    kvl = wt['a_kv_norm_g'].shape[1]
    mla_w = heads * V_HEAD

    h0 = jnp.concatenate([jnp.zeros((pad, d), F32), wt['meta_tokens'], x], axis=0)
    cos_t, sin_t = _rope_tables(tp, pad)

    w_in_a = wt['a_w_in']
    zcol = jnp.zeros((d, LANES - QK_ROPE), BF16)
    w_in_a = jnp.concatenate([w_in_a[:, :ql + kvl + QK_ROPE], zcol, w_in_a[:, ql + kvl + QK_ROPE:]], axis=1)
    c_kv, c_kr, c_gate = ql, ql + kvl, ql + kvl + LANES
    splits_a = [(0, c_kv), (c_kv, c_kr), (c_kr, c_gate), (c_gate, c_gate + mla_w)]
    w_uq = wt['a_w_uq'].reshape(ql, heads, QK_NOPE + QK_ROPE)
    w_uq = jnp.pad(w_uq, ((0, 0), (0, 0), (0, HEAD_W - QK_NOPE - QK_ROPE))).reshape(ql, heads * HEAD_W)
    w_ukv = wt['a_w_ukv']

    q_lat, kv_lat, kr_raw, gate_a = _norm_matmul("a_in_proj", h0, wt['a_norm_g'], w_in_a, splits_a, tm)
    q = _q_proj(q_lat, wt['a_q_norm_g'], w_uq, cos_t, sin_t, heads, tm)
    k, v = _kv_proj(kv_lat, wt['a_kv_norm_g'], w_ukv, kr_raw, cos_t, sin_t, heads, tm)
    attn, lse, *late_whole = _flash_fwd(q, k, v, heads, pad, tm, exchange=late_gather)
    wt.update({n: _kernel_form(n, w) for n, w in zip(late_names, late_whole)})
    lru_w = wt['b_conv_w'].shape[1]
    h1 = _gated_out("a_out_proj", attn, gate_a, wt['a_w_out'], h0, tm)

    u, gate_b = _norm_matmul("b_in_proj", h1, wt['b_norm_g'], wt['b_w_in'], [(0, lru_w), (lru_w, 2 * lru_w)], tm)
    uc, r, ig, hs, decay, mult = _rglru_fwd(u, wt['b_conv_w'], wt['b_conv_b'], wt['b_w_rg'], wt['b_b_rg'],
                                            wt['b_w_ig'], wt['b_b_ig'], wt['b_lam'], pad, tm)
    h2 = _gated_out("b_out_proj", hs, gate_b, wt['b_w_out'], h1, tm)

    dh2, loss, d_final_g = _final_loss(h2, wt['final_norm_g'], target, x0)

    dhs, dgate_b, dw_out_b = _gated_out_bwd("b_out_proj_bwd", dh2, hs, gate_b, wt['b_w_out'], tm)
    du, dconv_w, dconv_b, dw_rg, db_rg, dw_ig, db_ig, dlam = _rglru_bwd(
        dhs, hs, r, ig, uc, u, decay, mult, wt['b_conv_w'], wt['b_w_rg'], wt['b_w_ig'], wt['b_lam'], pad, tm)
    dh1, dw_in_b, dg_b = _norm_matmul_bwd("b_in_proj_bwd", h1, wt['b_norm_g'], wt['b_w_in'], [du, dgate_b], tm, resid=dh2)

    grads_b = [_matrix_grad(n, g, heads) for n, g in
               (('b_w_in', dw_in_b), ('b_w_rg', dw_rg), ('b_w_ig', dw_ig), ('b_w_out', dw_out_b))]
    dattn, dgate_a, dw_out_a, delta, *got = _gated_out_bwd(
        "a_out_proj_bwd", dh1, attn, gate_a, wt['a_w_out'], tm, delta_heads=heads,
        exchange=_halves_to_sibling("swap_b", grads_b))
    sums_b = [_chip_sum(gr, r, c_idx) for gr, r in zip(grads_b, got)]
    grad_out = _matrix_grad('a_w_out', dw_out_a, heads)
    dq, dk, dv, *landed = _flash_bwd(
        q, k, v, lse, delta, dattn, heads, pad, tm,
        exchange=_blocks_to_chips("chips_b", sums_b) + _halves_to_sibling("swap_out", [grad_out]))
    through = list(zip(grads_b, sums_b, landed[:len(grads_b)]))
    sum_out = _chip_sum(grad_out, landed[len(grads_b)], c_idx)

    def q_prologue(dy_refs, dy_s, ex_ref):
        (dq_ref,), cos_v, sin_v = dy_refs[:1], dy_refs[1][...], dy_refs[2][...]
        for h in range(heads):
            c0 = h * HEAD_W
            dy_s[:, c0:c0 + QK_NOPE] = dq_ref[:, c0:c0 + QK_NOPE].astype(BF16)
            dy_s[:, c0 + QK_NOPE:c0 + HEAD_W] = _unrope(dq_ref[:, c0 + QK_NOPE:c0 + HEAD_W], cos_v, sin_v).astype(BF16)

    dq_lat, dw_uq, dg_q, from_chips_out = _norm_matmul_bwd(
        "a_q_proj_bwd", q_lat, wt['a_q_norm_g'], w_uq, [dq, cos_t, sin_t], tm, prologue=q_prologue,
        exchange=_blocks_to_chips("chips_out", [sum_out]))
    through.append((grad_out, sum_out, from_chips_out))
    grad_uq = _matrix_grad('a_w_uq', dw_uq, heads)

    def kv_prologue(dy_refs, dy_s, ex_ref):
        dk_ref, dv_ref = dy_refs[:2]
        cos_v, sin_v = dy_refs[2][...], dy_refs[3][...]
        dkr = jnp.zeros((dk_ref.shape[0], LANES), F32)
        for h in range(heads):
            c0 = h * (QK_NOPE + V_HEAD)
            dy_s[:, c0:c0 + QK_NOPE] = dk_ref[:, h * HEAD_W:h * HEAD_W + QK_NOPE].astype(BF16)
            dy_s[:, c0 + QK_NOPE:c0 + QK_NOPE + V_HEAD] = dv_ref[:, h * V_HEAD:(h + 1) * V_HEAD].astype(BF16)
            dkr = dkr + dk_ref[:, h * HEAD_W + QK_NOPE:(h + 1) * HEAD_W]
        ex_ref[...] = _unrope(dkr, cos_v, sin_v)

    dkv_lat, dw_ukv, dg_kv, dkr_raw, got_uq = _norm_matmul_bwd(
        "a_kv_proj_bwd", kv_lat, wt['a_kv_norm_g'], w_ukv, [dk, dv, cos_t, sin_t], tm,
        prologue=kv_prologue, extra_out=(LANES, F32), exchange=_halves_to_sibling("swap_uq", [grad_uq]))
    sum_uq = _chip_sum(grad_uq, got_uq, c_idx)
    grad_ukv = _matrix_grad('a_w_ukv', dw_ukv, heads)

    dh0, dw_in_a, dg_a, from_chips_uq, got_ukv = _norm_matmul_bwd(
        "a_in_proj_bwd", h0, wt['a_norm_g'], w_in_a, [dq_lat, dkv_lat, dkr_raw, dgate_a], tm, resid=dh1,
        exchange=_blocks_to_chips("chips_uq", [sum_uq]) + _halves_to_sibling("swap_ukv", [grad_ukv]))
    through.append((grad_uq, sum_uq, from_chips_uq))
    swapped = [(grad_ukv, _chip_sum(grad_ukv, got_ukv, c_idx))]

    dw_in_a = jnp.concatenate([dw_in_a[:, :c_kr + QK_ROPE], dw_in_a[:, c_gate:]], axis=1)
    grads = {
        'meta_tokens': dh0[pad:x0], 'a_norm_g': dg_a, 'a_w_in': dw_in_a, 'a_q_norm_g': dg_q, 'a_kv_norm_g': dg_kv,
        'a_w_uq': dw_uq, 'a_w_ukv': dw_ukv, 'a_w_out': dw_out_a, 'b_norm_g': dg_b, 'b_w_in': dw_in_b,
        'b_conv_w': dconv_w, 'b_conv_b': dconv_b, 'b_w_rg': dw_rg, 'b_b_rg': db_rg, 'b_w_ig': dw_ig,
        'b_b_ig': db_ig, 'b_lam': dlam, 'b_w_out': dw_out_b, 'final_norm_g': d_final_g,
    }
    return loss, dh0[x0:], grads, through, swapped


def _chip_major(whole, local_shape, axis):
    if axis is None:
        return jnp.broadcast_to(whole.reshape(1, -1), (N_CHIPS, whole.size))
    shape = list(local_shape)
    g = whole.reshape(shape[:axis] + [N_CHIPS, shape[axis]] + shape[axis + 1:])
    return jnp.moveaxis(g, axis, 0).reshape(N_CHIPS, -1)


def kernel(x, meta_tokens, a_norm_g, a_w_in, a_q_norm_g, a_kv_norm_g, a_w_uq, a_w_ukv, a_w_out, b_norm_g, b_w_in, b_conv_w, b_conv_b, b_w_rg, b_b_rg, b_w_ig, b_b_ig, b_lam, b_w_out, final_norm_g, loss_target, m_meta_tokens, m_a_norm_g, m_a_w_in, m_a_q_norm_g, m_a_kv_norm_g, m_a_w_uq, m_a_w_ukv, m_a_w_out, m_b_norm_g, m_b_w_in, m_b_conv_w, m_b_conv_b, m_b_w_rg, m_b_b_rg, m_b_w_ig, m_b_b_ig, m_b_lam, m_b_w_out, m_final_norm_g, v_meta_tokens, v_a_norm_g, v_a_w_in, v_a_q_norm_g, v_a_kv_norm_g, v_a_w_uq, v_a_w_ukv, v_a_w_out, v_b_norm_g, v_b_w_in, v_b_conv_w, v_b_conv_b, v_b_w_rg, v_b_b_rg, v_b_w_ig, v_b_b_ig, v_b_lam, v_b_w_out, v_final_norm_g):
    local_w = dict(zip(WEIGHTS, (meta_tokens, a_norm_g, a_w_in, a_q_norm_g, a_kv_norm_g, a_w_uq, a_w_ukv, a_w_out,
                                 b_norm_g, b_w_in, b_conv_w, b_conv_b, b_w_rg, b_b_rg, b_w_ig, b_b_ig, b_lam,
                                 b_w_out, final_norm_g)))
    local_m = dict(zip(WEIGHTS, (m_meta_tokens, m_a_norm_g, m_a_w_in, m_a_q_norm_g, m_a_kv_norm_g, m_a_w_uq,
                                 m_a_w_ukv, m_a_w_out, m_b_norm_g, m_b_w_in, m_b_conv_w, m_b_conv_b, m_b_w_rg,
                                 m_b_b_rg, m_b_w_ig, m_b_b_ig, m_b_lam, m_b_w_out, m_final_norm_g)))
    local_v = dict(zip(WEIGHTS, (v_meta_tokens, v_a_norm_g, v_a_w_in, v_a_q_norm_g, v_a_kv_norm_g, v_a_w_uq,
                                 v_a_w_ukv, v_a_w_out, v_b_norm_g, v_b_w_in, v_b_conv_w, v_b_conv_b, v_b_w_rg,
                                 v_b_b_rg, v_b_w_ig, v_b_b_ig, v_b_lam, v_b_w_out, v_final_norm_g)))
    matrices = ('a_w_in', 'a_w_uq', 'a_w_ukv', 'a_w_out', 'b_w_in', 'b_w_rg', 'b_w_ig', 'b_w_out')
    heads = a_w_uq.shape[-1] * N_CHIPS // (QK_NOPE + QK_ROPE)

    split, small, late = [], [], []
    for n in WEIGHTS:
        if SHARD_AXIS[n] is None:
            continue
        if n.startswith('b_') or n == 'a_w_out':
            late.append((n, local_w[n].astype(BF16) if n in matrices else local_w[n], SHARD_AXIS[n]))
        elif n == 'a_w_in':
            split.append((n, local_w[n].astype(BF16)[None], 0, 2))
        elif n in matrices:
            split.append((n, local_w[n].astype(BF16), SHARD_AXIS[n], 1))
        else:
            small.append((n, local_w[n], SHARD_AXIS[n]))
    gathered = _gather_weights([s[1:] for s in split], [s[1:] for s in small])
    whole = dict(zip([s[0] for s in split + small], gathered))
    whole['a_w_in'] = _unshard(whole['a_w_in'], SHARD_AXIS['a_w_in'])
    late_names = [s[0] for s in late]
    wt = {n: _kernel_form(n, whole.get(n, local_w[n])) for n in WEIGHTS if n not in late_names}

    c_idx = lax.axis_index("c").astype(jnp.int32).reshape(1)
    me_idx = (2 * lax.axis_index("x") + lax.axis_index("y")).astype(jnp.int32).reshape(1)
    loss, grad_x, grads, through, swapped = _local_grads(
        x[0], loss_target[0], wt, heads, c_idx, late_names, _gather_whole("gather_weights_b", [s[1:] for s in late]))

    ext_uq = heads % N_CHIPS == 0
    started = [gr.name for gr, *_ in through + swapped]
    last = [_matrix_grad(n, grads[n], heads) for n in matrices if n not in started]
    rest = [n for n in WEIGHTS if n not in matrices]
    pieces = [_chip_major(grads[n], local_w[n].shape, SHARD_AXIS[n]) for n in rest]
    length = sum(p.shape[1] for p in pieces)
    unit = 2 * SUBLANES * 1024
    padded = -(-length // unit) * unit
    flat = jnp.concatenate(pieces + [jnp.zeros((N_CHIPS, padded - length), F32)], axis=1)
    last.append(_Grad('small', flat.reshape(N_CHIPS, padded // 1024, 1024), 'lead', padded // 2048, 1024))

    got = _halves_to_sibling("grads_to_sibling", last).run()
    swapped = swapped + [(gr, _chip_sum(gr, r, c_idx)) for gr, r in zip(last, got)]
    from_chips = _blocks_to_chips("grads_to_chips", [p for _, p in swapped]).run()
    through = through + [(gr, p, r) for (gr, p), r in zip(swapped, from_chips)]
    halves = [_sum_chips(gr.name, p, r, me_idx, c_idx) for gr, p, r in through]
    summed = dict(zip([gr.name for gr, _, _ in through], _share_with_sibling(halves)))
    if ext_uq:
        g = summed['a_w_uq']
        summed['a_w_uq'] = g.reshape(g.shape[0], -1, HEAD_W)[:, :, :QK_NOPE + QK_ROPE]
    total = summed['small'].reshape(-1)

    out_g, out_d, out_m, out_v = [], [], [], []
    off = 0
    for n in WEIGHTS:
        shape = local_w[n].shape
        if n in matrices:
            g = summed[n].reshape(shape)
        else:
            size = 1
            for s in shape:
                size *= s
            g = total[off:off + size].reshape(shape)
            off += size
        delta, new_m, new_v = _adamw("adamw_" + n, _as2d(local_w[n]), _as2d(g), _as2d(local_m[n]), _as2d(local_v[n]))
        out_g.append(g)
        out_d.append(delta.reshape(shape))
        out_m.append(new_m.reshape(shape))
        out_v.append(new_v.reshape(shape))

    loss = lax.psum(loss[0, 0], ("x", "y", "c"))
    return (loss, grad_x[None], *out_g, *out_d, *out_m, *out_v)
```

```python
import functools

import jax
import jax.numpy as jnp
from jax import lax
from jax.experimental import pallas as pl
from jax.experimental.pallas import tpu as pltpu

F32 = jnp.float32
BF16 = jnp.bfloat16
MESH = pl.DeviceIdType.MESH

RMS_EPS = 1e-6
QK_NOPE = 128
QK_ROPE = 64
V_HEAD = 128
HEAD_W = 256
ROPE_BASE = 10000.0
Q_BLOCK = 128
MASK_VALUE = -1e30
CONV_WIDTH = 4
LRU_C = 8.0
N_CHIPS = 4

ADAM_LR = 0.001
ADAM_B1 = 0.9
ADAM_B2 = 0.999
ADAM_EPS = 1e-08
ADAM_WD = 0.01
ADAM_STEP = 10

VMEM_LIMIT_V7X = 56 * 1024 * 1024
LANES = 128
SUBLANES = 8

WEIGHTS = ['meta_tokens', 'a_norm_g', 'a_w_in', 'a_q_norm_g', 'a_kv_norm_g', 'a_w_uq', 'a_w_ukv',
           'a_w_out', 'b_norm_g', 'b_w_in', 'b_conv_w', 'b_conv_b', 'b_w_rg', 'b_b_rg', 'b_w_ig',
           'b_b_ig', 'b_lam', 'b_w_out', 'final_norm_g']
SHARD_AXIS = {'meta_tokens': 1, 'a_norm_g': None, 'a_w_in': 2, 'a_q_norm_g': None, 'a_kv_norm_g': None,
              'a_w_uq': 2, 'a_w_ukv': 2, 'a_w_out': 1, 'b_norm_g': 1, 'b_w_in': 2, 'b_conv_w': 2,
              'b_conv_b': 1, 'b_w_rg': 2, 'b_b_rg': 1, 'b_w_ig': 2, 'b_b_ig': 1, 'b_lam': 1,
              'b_w_out': 1, 'final_norm_g': None}


def _params(sem=None):
    return pltpu.CompilerParams(dimension_semantics=sem, vmem_limit_bytes=VMEM_LIMIT_V7X)


def _row_tile(tp):
    return 384 if (tp % 384 == 0 and tp >= 1152) else 128


def _sigmoid(x):
    return 1.0 / (1.0 + jnp.exp(-x))


def _rms(x):
    return lax.rsqrt(jnp.mean(x * x, axis=-1, keepdims=True) + RMS_EPS)


def _swap32(x):
    lane = lax.broadcasted_iota(jnp.int32, x.shape, 1)
    return jnp.where(lane < 32, pltpu.roll(x, 96, 1), pltpu.roll(x, 32, 1))


def _rope(x, cos_t, sin_t):
    return x * cos_t + _swap32(x) * sin_t


def _unrope(d, cos_t, sin_t):
    lane = lax.broadcasted_iota(jnp.int32, d.shape, 1)
    return jnp.where(lane < QK_ROPE, d * cos_t + _swap32(d * sin_t), 0.0)


def _dot(a, b):
    return jnp.dot(a, b, preferred_element_type=F32)


def _dot_nt(a, b):
    return lax.dot_general(a, b, (((1,), (1,)), ((), ())), preferred_element_type=F32)


def _dot_tn(a, b):
    return lax.dot_general(a, b, (((0,), (0,)), ((), ())), preferred_element_type=F32)


def _norm_matmul(name, x, g, w, splits, tm, exchange=None):
    tp, kin = x.shape
    n = w.shape[1]
    nt = tp // tm

    def compute(x_ref, g_ref, w_ref, *outs):
        xv = x_ref[...]
        nrm = ((xv * _rms(xv)) * g_ref[...]).astype(BF16)
        y = _dot(nrm, w_ref[...])
        for o_ref, (c0, c1) in zip(outs, splits):
            o_ref[...] = y[:, c0:c1]

    in_specs = [pl.BlockSpec((tm, kin), lambda i: (i, 0)),
                pl.BlockSpec((1, kin), lambda i: (0, 0)),
                pl.BlockSpec((kin, n), lambda i: (0, 0))]
    out_specs = [pl.BlockSpec((tm, c1 - c0), lambda i: (i, 0)) for c0, c1 in splits]
    out_shape = [jax.ShapeDtypeStruct((tp, c1 - c0), F32) for c0, c1 in splits]
    args, scratch = [x, g, w], []
    body = _with_exchange(exchange, len(in_specs), len(out_specs),
                          lambda: pl.program_id(0) == 0, lambda: pl.program_id(0) == nt - 1, compute)
    if exchange is not None:
        in_specs = in_specs + exchange.specs(exchange.n_in)
        out_specs = out_specs + exchange.specs(exchange.n_out)
        out_shape = out_shape + exchange.out_shapes
        args, scratch = args + exchange.arrays, exchange.scratch()
    return pl.pallas_call(
        body, name=name, grid=(nt,),
        in_specs=in_specs, out_specs=out_specs, out_shape=out_shape, scratch_shapes=scratch,
        compiler_params=_params(("arbitrary",)),
    )(*args)


def _q_proj(q_lat, g, w_uq, cos_t, sin_t, heads, tm):
    tp, kin = q_lat.shape
    n = heads * HEAD_W

    def body(x_ref, g_ref, w_ref, cos_ref, sin_ref, q_ref):
        xv = x_ref[...]
        nrm = ((xv * _rms(xv)) * g_ref[...]).astype(BF16)
        y = _dot(nrm, w_ref[...])
        cos_v, sin_v = cos_ref[...], sin_ref[...]
        for h in range(heads):
            c0 = h * HEAD_W
            q_ref[:, c0:c0 + QK_NOPE] = y[:, c0:c0 + QK_NOPE].astype(BF16)
            q_ref[:, c0 + QK_NOPE:c0 + HEAD_W] = _rope(y[:, c0 + QK_NOPE:c0 + HEAD_W], cos_v, sin_v).astype(BF16)

    return pl.pallas_call(
        body, name="a_q_proj", grid=(tp // tm,),
        in_specs=[pl.BlockSpec((tm, kin), lambda i: (i, 0)),
                  pl.BlockSpec((1, kin), lambda i: (0, 0)),
                  pl.BlockSpec((kin, n), lambda i: (0, 0)),
                  pl.BlockSpec((tm, LANES), lambda i: (i, 0)),
                  pl.BlockSpec((tm, LANES), lambda i: (i, 0))],
        out_specs=pl.BlockSpec((tm, n), lambda i: (i, 0)),
        out_shape=jax.ShapeDtypeStruct((tp, n), BF16),
        compiler_params=_params(("parallel",)),
    )(q_lat, g, w_uq, cos_t, sin_t)


def _kv_proj(kv_lat, g, w_ukv, k_rope_raw, cos_t, sin_t, heads, tm):
    tp, kin = kv_lat.shape
    n = heads * (QK_NOPE + V_HEAD)

    def body(x_ref, g_ref, w_ref, kr_ref, cos_ref, sin_ref, k_ref, v_ref):
        xv = x_ref[...]
        nrm = ((xv * _rms(xv)) * g_ref[...]).astype(BF16)
        y = _dot(nrm, w_ref[...])
        kr = _rope(kr_ref[...], cos_ref[...], sin_ref[...]).astype(BF16)
        for h in range(heads):
            c0 = h * (QK_NOPE + V_HEAD)
            k_ref[:, h * HEAD_W:h * HEAD_W + QK_NOPE] = y[:, c0:c0 + QK_NOPE].astype(BF16)
            k_ref[:, h * HEAD_W + QK_NOPE:(h + 1) * HEAD_W] = kr
            v_ref[:, h * V_HEAD:(h + 1) * V_HEAD] = y[:, c0 + QK_NOPE:c0 + QK_NOPE + V_HEAD].astype(BF16)

    return pl.pallas_call(
        body, name="a_kv_proj", grid=(tp // tm,),
        in_specs=[pl.BlockSpec((tm, kin), lambda i: (i, 0)),
                  pl.BlockSpec((1, kin), lambda i: (0, 0)),
                  pl.BlockSpec((kin, n), lambda i: (0, 0)),
                  pl.BlockSpec((tm, LANES), lambda i: (i, 0)),
                  pl.BlockSpec((tm, LANES), lambda i: (i, 0)),
                  pl.BlockSpec((tm, LANES), lambda i: (i, 0))],
        out_specs=[pl.BlockSpec((tm, heads * HEAD_W), lambda i: (i, 0)),
                   pl.BlockSpec((tm, heads * V_HEAD), lambda i: (i, 0))],
        out_shape=[jax.ShapeDtypeStruct((tp, heads * HEAD_W), BF16),
                   jax.ShapeDtypeStruct((tp, heads * V_HEAD), BF16)],
        compiler_params=_params(("parallel",)),
    )(kv_lat, g, w_ukv, k_rope_raw, cos_t, sin_t)


def _as_rows(col):
    rows = col.shape[0]
    return jnp.transpose(jnp.broadcast_to(col, (rows, LANES)))[0:SUBLANES, :]


def _attn_mask(row0, col0, rows, cols, pad):
    row = row0 + lax.broadcasted_iota(jnp.int32, (rows, cols), 0)
    col = col0 + lax.broadcasted_iota(jnp.int32, (rows, cols), 1)
    return (col <= row) & (col >= pad)


LOG2E = 1.4426950408889634
FLASH_FWD_TRIPS = ((4, 2), (2, 2), (1, 1))


def _flash_fwd(q, k, v, heads, pad, tq, exchange=None):
    tp = q.shape[0]
    nq = tp // tq
    c2 = (QK_NOPE + QK_ROPE) ** -0.5 * LOG2E

    def compute(q_ref, k_ref, v_ref, o_ref, lse_ref):
        i = pl.program_id(1)

        def make_step(masked, blocks, parts=1):
            keys = blocks * tq // parts

            def step(j, carry):
                m, l, acc = carry
                offs = [pl.multiple_of(j * tq + part * keys, tq) for part in range(parts)]
                scores = [_dot_nt(q_ref[...], k_ref[pl.ds(off, keys), :]) for off in offs]
                for off, s in zip(offs, scores):
                    s = s * c2
                    if masked:
                        s = jnp.where(_attn_mask(i * tq, j * tq, tq, keys, pad), s, MASK_VALUE)
                    m_new = jnp.maximum(m, jnp.max(s, axis=-1, keepdims=True))
                    p = jnp.exp2(s - m_new)
                    alpha = jnp.exp2(m - m_new)
                    l = alpha * l + jnp.sum(p, axis=-1, keepdims=True)
                    acc = alpha * acc + _dot(p.astype(BF16), v_ref[pl.ds(off, keys), :])
                    m = m_new
                return m, l, acc
            return step

        init = (jnp.full((tq, 1), MASK_VALUE, F32), jnp.zeros((tq, 1), F32), jnp.zeros((tq, V_HEAD), F32))
        carry = make_step(True, 1)(0, init)
        first = 1
        for blocks, parts in FLASH_FWD_TRIPS:
            trips = jnp.maximum(i - first, 0) // blocks
            step_n = make_step(False, blocks, parts)
            carry = lax.fori_loop(0, trips, lambda t, cr, f=first, b=blocks, s=step_n: s(f + b * t, cr), carry)
            first = first + blocks * trips
        m, l, acc = lax.fori_loop(jnp.maximum(i, 1), i + 1, make_step(True, 1), carry)
        o_ref[...] = acc / l
        lse_ref[...] = _as_rows(m + jnp.log(l) * LOG2E)

    in_specs = [pl.BlockSpec((tq, HEAD_W), lambda h, i: (i, h)),
                pl.BlockSpec((tp, HEAD_W), lambda h, i: (0, h)),
                pl.BlockSpec((tp, V_HEAD), lambda h, i: (0, h))]
    out_specs = [pl.BlockSpec((tq, V_HEAD), lambda h, i: (i, h)),
                 pl.BlockSpec((None, None, SUBLANES, tq), lambda h, i: (h, i, 0, 0))]
    out_shape = [jax.ShapeDtypeStruct((tp, heads * V_HEAD), F32),
                 jax.ShapeDtypeStruct((heads, nq, SUBLANES, tq), F32)]
    args, scratch = [q, k, v], []
    body = _with_exchange(exchange, len(in_specs), len(out_specs),
                          lambda: (pl.program_id(0) == 0) & (pl.program_id(1) == 0),
                          lambda: (pl.program_id(0) == heads - 1) & (pl.program_id(1) == nq - 1), compute)
    if exchange is not None:
        in_specs = in_specs + exchange.specs(exchange.n_in)
        out_specs = out_specs + exchange.specs(exchange.n_out)
        out_shape = out_shape + exchange.out_shapes
        args, scratch = args + exchange.arrays, exchange.scratch()
    return pl.pallas_call(
        body, name="a_flash_fwd", grid=(heads, nq),
        in_specs=in_specs, out_specs=out_specs, out_shape=out_shape, scratch_shapes=scratch,
        compiler_params=_params(("arbitrary", "arbitrary")),
    )(*args)


def _gated_out(name, a, gate, w, resid, tm):
    tp, wd = a.shape
    d = w.shape[1]

    def body(a_ref, gate_ref, w_ref, res_ref, o_ref):
        gv = gate_ref[...]
        y = (a_ref[...] * (gv * _sigmoid(gv))).astype(BF16)
        o_ref[...] = res_ref[...] + _dot(y, w_ref[...])

    return pl.pallas_call(
        body, name=name, grid=(tp // tm,),
        in_specs=[pl.BlockSpec((tm, wd), lambda i: (i, 0)),
                  pl.BlockSpec((tm, wd), lambda i: (i, 0)),
                  pl.BlockSpec((wd, d), lambda i: (0, 0)),
                  pl.BlockSpec((tm, d), lambda i: (i, 0))],
        out_specs=pl.BlockSpec((tm, d), lambda i: (i, 0)),
        out_shape=jax.ShapeDtypeStruct((tp, d), F32),
        compiler_params=_params(("parallel",)),
    )(a, gate, w, resid)


def _lru_decay(r, sp):
    log_a = -LRU_C * r * sp
    a = jnp.exp(log_a)
    e2 = a * a
    x2 = 2.0 * log_a
    series = x2 * (1.0 + x2 * (0.5 + x2 * (1.0 / 6.0)))
    em1 = jnp.where(x2 > -0.02, series, e2 - 1.0)
    return a, e2, jnp.sqrt(-em1)


def _softplus(x):
    return jnp.maximum(x, 0.0) + jnp.log1p(jnp.exp(-jnp.abs(x)))


def _rglru_fwd(u, conv_w, conv_b, w_rg, b_rg, w_ig, b_ig, lam, pad, tm):
    tp, w = u.shape
    groups, blk = w_rg.shape[0], w_rg.shape[1]

    def body(u_ref, cw_ref, cb_ref, wr_ref, br_ref, wi_ref, bi_ref, lam_ref,
             uc_ref, r_ref, ig_ref, hs_ref, a_s, mult_ref, uext, b_s, hc):
        i = pl.program_id(0)

        @pl.when(i == 0)
        def _():
            uext[0:SUBLANES, :] = jnp.zeros((SUBLANES, w), F32)
            hc[...] = jnp.zeros((SUBLANES, w), F32)

        uext[SUBLANES:SUBLANES + tm, :] = u_ref[...]
        cw = cw_ref[...]
        uc = cb_ref[...] + uext[pl.ds(SUBLANES - 3, tm), :] * cw[0:1, :]
        uc = uc + uext[pl.ds(SUBLANES - 2, tm), :] * cw[1:2, :]
        uc = uc + uext[pl.ds(SUBLANES - 1, tm), :] * cw[2:3, :]
        uc = uc + uext[pl.ds(SUBLANES, tm), :] * cw[3:4, :]
        uc_ref[...] = uc
        uext[0:SUBLANES, :] = uext[tm:tm + SUBLANES, :]

        sp = _softplus(-lam_ref[...])
        for g in range(groups):
            sl = slice(g * blk, (g + 1) * blk)
            ucg = uc_ref[:, sl]
            ucb = ucg.astype(BF16)
            r = _sigmoid(_dot(ucb, wr_ref[g]) + br_ref[:, sl])
            ig = _sigmoid(_dot(ucb, wi_ref[g]) + bi_ref[:, sl])
            r_ref[:, sl] = r
            ig_ref[:, sl] = ig
            a, _, mult = _lru_decay(r, sp[:, sl])
            a_s[:, sl] = a
            mult_ref[:, sl] = mult
            b_s[:, sl] = mult * (ig * ucg)

        @pl.when(i == 0)
        def _():
            row = lax.broadcasted_iota(jnp.int32, (Q_BLOCK, w), 0)
            start = ig_ref[0:Q_BLOCK, :] * uc_ref[0:Q_BLOCK, :]
            b_s[0:Q_BLOCK, :] = jnp.where(row < pad, 0.0, jnp.where(row == pad, start, b_s[0:Q_BLOCK, :]))
            mult_ref[0:Q_BLOCK, :] = jnp.where(row == pad, 1.0, mult_ref[0:Q_BLOCK, :])

        row8 = lax.broadcasted_iota(jnp.int32, (SUBLANES, w), 0)

        def group(gi, h_in):
            off = pl.multiple_of(gi * SUBLANES, SUBLANES)
            av = a_s[pl.ds(off, SUBLANES), :]
            bv = b_s[pl.ds(off, SUBLANES), :]
            for k in (1, 2, 4):
                keep = row8 >= k
                bv = jnp.where(keep, av * pltpu.roll(bv, k, 0) + bv, bv)
                av = jnp.where(keep, av * pltpu.roll(av, k, 0), av)
            hv = av * h_in + bv
            hs_ref[pl.ds(off, SUBLANES), :] = hv
            return jnp.broadcast_to(hv[SUBLANES - 1:SUBLANES, :], (SUBLANES, w))

        hc[...] = lax.fori_loop(0, tm // SUBLANES, group, hc[...])

    row_spec = pl.BlockSpec((tm, w), lambda i: (i, 0))
    vec_spec = pl.BlockSpec((1, w), lambda i: (0, 0))
    mat_spec = pl.BlockSpec((groups, blk, blk), lambda i: (0, 0, 0))
    return pl.pallas_call(
        body, name="b_rglru_fwd", grid=(tp // tm,),
        in_specs=[row_spec, pl.BlockSpec((CONV_WIDTH, w), lambda i: (0, 0)), vec_spec,
                  mat_spec, vec_spec, mat_spec, vec_spec, vec_spec],
        out_specs=[row_spec] * 6,
        out_shape=[jax.ShapeDtypeStruct((tp, w), F32)] * 6,
        scratch_shapes=[pltpu.VMEM((tm + SUBLANES, w), F32), pltpu.VMEM((tm, w), F32),
                        pltpu.VMEM((SUBLANES, w), F32)],
        compiler_params=_params(("arbitrary",)),
    )(u, conv_w, conv_b, w_rg, b_rg, w_ig, b_ig, lam)


def _final_loss(h, g, target, x0, tm):
    tp, d = h.shape
    assert x0 % Q_BLOCK == 0 and tm % Q_BLOCK == 0 and target.shape[0] == tp - x0
    lead = x0 // Q_BLOCK
    per = tm // Q_BLOCK

    def body(h_ref, g_ref, *rest):
        t_refs, (dh_ref, loss_ref, dg_ref) = rest[:per], rest[per:]
        i = pl.program_id(0)

        @pl.when(i == 0)
        def _():
            loss_ref[...] = jnp.zeros_like(loss_ref)
            dg_ref[...] = jnp.zeros_like(dg_ref)

        gv = g_ref[...]
        for b in range(per):
            rows = slice(b * Q_BLOCK, (b + 1) * Q_BLOCK)
            xv = h_ref[rows, :]
            r = _rms(xv)
            xh = xv * r
            err = jnp.where(i * per + b >= lead, xh * gv - t_refs[b][...], 0.0)
            loss_ref[...] += 0.5 * jnp.sum(jnp.mean(err * err, axis=-1, keepdims=True))
            dy = err / d
            dg_ref[...] += jnp.sum(dy * xh, axis=0, keepdims=True)
            dxh = dy * gv
            dh_ref[rows, :] = r * (dxh - xh * jnp.mean(dxh * xh, axis=-1, keepdims=True))

    def piece(b):
        return pl.BlockSpec((Q_BLOCK, d), lambda i: (jnp.maximum(i * per + b - lead, 0), 0))

    return pl.pallas_call(
        body, name="final_loss", grid=(tp // tm,),
        in_specs=[pl.BlockSpec((tm, d), lambda i: (i, 0)),
                  pl.BlockSpec((1, d), lambda i: (0, 0))] + [piece(b) for b in range(per)],
        out_specs=[pl.BlockSpec((tm, d), lambda i: (i, 0)),
                   pl.BlockSpec((SUBLANES, LANES), lambda i: (0, 0)),
                   pl.BlockSpec((1, d), lambda i: (0, 0))],
        out_shape=[jax.ShapeDtypeStruct((tp, d), F32),
                   jax.ShapeDtypeStruct((SUBLANES, LANES), F32),
                   jax.ShapeDtypeStruct((1, d), F32)],
        compiler_params=_params(("arbitrary",)),
    )(h, g, *([target] * per))


def _with_exchange(exchange, n_in, n_out, first, last, compute):
    if exchange is None:
        return compute
    ex_in, ex_out = exchange.n_in, exchange.n_out

    def body(*refs):
        own_in, their_in = refs[:n_in], refs[n_in:n_in + ex_in]
        pos = n_in + ex_in
        own_out, their_out = refs[pos:pos + n_out], refs[pos + n_out:pos + n_out + ex_out]
        rest = refs[pos + n_out + ex_out:]
        own_scratch, sems = rest[:len(rest) - 2], rest[len(rest) - 2:]

        @pl.when(first())
        def _():
            exchange.start(their_in, their_out, sems)

        compute(*own_in, *own_out, *own_scratch)

        @pl.when(last())
        def _():
            exchange.finish(their_in, their_out, sems)

    return body


def _gated_out_bwd(name, dout, a, gate, w, tm, delta_heads=0, exchange=None):
    tp, wd = a.shape
    d = w.shape[1]
    nt = tp // tm

    def compute(do_ref, a_ref, gate_ref, w_ref, da_ref, dgate_ref, dw_ref, *delta_ref):
        @pl.when(pl.program_id(0) == 0)
        def _():
            dw_ref[...] = jnp.zeros_like(dw_ref)

        gv = gate_ref[...]
        av = a_ref[...]
        sg = _sigmoid(gv)
        silu = gv * sg
        dob = do_ref[...].astype(BF16)
        dy = _dot_nt(dob, w_ref[...])
        da = dy * silu
        da_ref[...] = da
        dgate_ref[...] = dy * av * (sg * (1.0 + gv * (1.0 - sg)))
        dw_ref[...] += _dot_tn((av * silu).astype(BF16), dob)
        for h in range(delta_heads):
            sl = slice(h * V_HEAD, (h + 1) * V_HEAD)
            delta_ref[0][h] = _as_rows(jnp.sum(da[:, sl] * av[:, sl], axis=-1, keepdims=True))

    out_specs = [pl.BlockSpec((tm, wd), lambda i: (i, 0)),
                 pl.BlockSpec((tm, wd), lambda i: (i, 0)),
                 pl.BlockSpec((wd, d), lambda i: (0, 0))]
    out_shape = [jax.ShapeDtypeStruct((tp, wd), F32),
                 jax.ShapeDtypeStruct((tp, wd), F32),
                 jax.ShapeDtypeStruct((wd, d), F32)]
    if delta_heads:
        out_specs.append(pl.BlockSpec((delta_heads, None, SUBLANES, tm), lambda i: (0, i, 0, 0)))
        out_shape.append(jax.ShapeDtypeStruct((delta_heads, tp // tm, SUBLANES, tm), F32))
    in_specs = [pl.BlockSpec((tm, d), lambda i: (i, 0)),
                pl.BlockSpec((tm, wd), lambda i: (i, 0)),
                pl.BlockSpec((tm, wd), lambda i: (i, 0)),
                pl.BlockSpec((wd, d), lambda i: (0, 0))]
    args, scratch = [dout, a, gate, w], []
    body = _with_exchange(exchange, len(in_specs), len(out_specs),
                          lambda: pl.program_id(0) == 0, lambda: pl.program_id(0) == nt - 1, compute)
    if exchange is not None:
        in_specs = in_specs + exchange.specs(exchange.n_in)
        out_specs = out_specs + exchange.specs(exchange.n_out)
        out_shape = out_shape + exchange.out_shapes
        args, scratch = args + exchange.arrays, exchange.scratch()
    return pl.pallas_call(
        body, name=name, grid=(nt,),
        in_specs=in_specs, out_specs=out_specs, out_shape=out_shape, scratch_shapes=scratch,
        compiler_params=_params(("arbitrary",)),
    )(*args)


def _rglru_bwd(dhs, hs, r, ig, uc, u, a, mult, conv_w, w_rg, w_ig, lam, pad, tm):
    tp, w = u.shape
    groups, blk = w_rg.shape[0], w_rg.shape[1]
    nt = tp // tm
    per8 = tm // SUBLANES

    def body(dhs_ref, hs_ref, hprev_ref, r_ref, ig_ref, uc_ref, u_ref, uprev_ref, a_ref, mult_ref,
             cw_ref, wr_ref, wi_ref, lam_ref,
             du_ref, dcw_ref, dcb_ref, dwr_ref, dbr_ref, dwi_ref, dbi_ref, dlam_ref,
             aext, c_s, g_s, hext, uext, ducext, gc):
        step = pl.program_id(0)
        ti = nt - 1 - step

        @pl.when(step == 0)
        def _():
            for ref in (dcw_ref, dcb_ref, dwr_ref, dbr_ref, dwi_ref, dbi_ref, dlam_ref):
                ref[...] = jnp.zeros_like(ref)
            aext[tm:tm + SUBLANES, :] = jnp.zeros((SUBLANES, w), F32)
            ducext[tm:tm + SUBLANES, :] = jnp.zeros((SUBLANES, w), F32)
            gc[...] = jnp.zeros((SUBLANES, w), F32)

        lam_v = lam_ref[...]
        sp = _softplus(-lam_v)
        row = ti * tm + lax.broadcasted_iota(jnp.int32, (tm, w), 0)

        rv = r_ref[...]
        a = a_ref[...]
        mult = mult_ref[...]
        aext[0:tm, :] = a
        c_s[...] = aext[pl.ds(1, tm), :]
        row8 = lax.broadcasted_iota(jnp.int32, (SUBLANES, w), 0)

        def group(gi, g_in):
            off = pl.multiple_of((per8 - 1 - gi) * SUBLANES, SUBLANES)
            cv = c_s[pl.ds(off, SUBLANES), :]
            dv = dhs_ref[pl.ds(off, SUBLANES), :]
            for k in (1, 2, 4):
                keep = row8 < SUBLANES - k
                dv = jnp.where(keep, cv * pltpu.roll(dv, SUBLANES - k, 0) + dv, dv)
                cv = jnp.where(keep, cv * pltpu.roll(cv, SUBLANES - k, 0), cv)
            gv = cv * g_in + dv
            g_s[pl.ds(off, SUBLANES), :] = gv
            return jnp.broadcast_to(gv[0:1, :], (SUBLANES, w))

        gc[...] = lax.fori_loop(0, per8, group, gc[...])
        aext[tm:tm + SUBLANES, :] = aext[0:SUBLANES, :]

        gsc = jnp.where(row < pad, 0.0, g_s[...])
        hext[0:SUBLANES, :] = hprev_ref[...]
        hext[SUBLANES:SUBLANES + tm, :] = hs_ref[...]
        hprev = jnp.where(row == 0, 0.0, hext[pl.ds(SUBLANES - 1, tm), :])
        igv = ig_ref[...]
        ucv = uc_ref[...]
        first = row == pad
        dmult = gsc * (igv * ucv)
        dig = gsc * mult * ucv
        duc = gsc * mult * igv
        dlog_a = (gsc * hprev) * a + jnp.where(first, 0.0, dmult * (-(a * a) / mult))
        dlam_ref[...] += jnp.sum(dlog_a * rv, axis=0, keepdims=True) * (LRU_C * _sigmoid(-lam_v))
        dpre_r = dlog_a * (-LRU_C * sp) * (rv * (1.0 - rv))
        dpre_i = dig * (igv * (1.0 - igv))
        dbr_ref[...] += jnp.sum(dpre_r, axis=0, keepdims=True)
        dbi_ref[...] += jnp.sum(dpre_i, axis=0, keepdims=True)
        for g in range(groups):
            sl = slice(g * blk, (g + 1) * blk)
            ucb = ucv[:, sl].astype(BF16)
            drb = dpre_r[:, sl].astype(BF16)
            dib = dpre_i[:, sl].astype(BF16)
            dwr_ref[g] += _dot_tn(ucb, drb)
            dwi_ref[g] += _dot_tn(ucb, dib)
            ducext[0:tm, sl] = duc[:, sl] + _dot_nt(drb, wr_ref[g]) + _dot_nt(dib, wi_ref[g])

        ducv = ducext[0:tm, :]
        cw = cw_ref[...]
        dcb_ref[...] += jnp.sum(ducv, axis=0, keepdims=True)
        uext[0:SUBLANES, :] = jnp.where(ti == 0, 0.0, uprev_ref[...])
        uext[SUBLANES:SUBLANES + tm, :] = u_ref[...]
        for j in range(CONV_WIDTH):
            ush = uext[pl.ds(SUBLANES - (CONV_WIDTH - 1 - j), tm), :]
            dcw_ref[j:j + 1, :] += jnp.sum(ducv * ush, axis=0, keepdims=True)
        du = ducv * cw[3:4, :]
        for k in range(1, CONV_WIDTH):
            du = du + ducext[pl.ds(k, tm), :] * cw[3 - k:4 - k, :]
        du_ref[...] = du
        ducext[tm:tm + SUBLANES, :] = ducext[0:SUBLANES, :]

    rev = lambda s: (nt - 1 - s, 0)
    halo = lambda s: (jnp.maximum((nt - 1 - s) * per8 - 1, 0), 0)
    row_spec = pl.BlockSpec((tm, w), rev)
    halo_spec = pl.BlockSpec((SUBLANES, w), halo)
    vec_spec = pl.BlockSpec((1, w), lambda s: (0, 0))
    mat_spec = pl.BlockSpec((groups, blk, blk), lambda s: (0, 0, 0))
    cw_spec = pl.BlockSpec((CONV_WIDTH, w), lambda s: (0, 0))
    return pl.pallas_call(
        body, name="b_rglru_bwd", grid=(nt,),
        in_specs=[row_spec, row_spec, halo_spec, row_spec, row_spec, row_spec, row_spec, halo_spec, row_spec, row_spec,
                  cw_spec, mat_spec, mat_spec, vec_spec],
        out_specs=[row_spec, cw_spec, vec_spec, mat_spec, vec_spec, mat_spec, vec_spec, vec_spec],
        out_shape=[jax.ShapeDtypeStruct((tp, w), F32), jax.ShapeDtypeStruct((CONV_WIDTH, w), F32),
                   jax.ShapeDtypeStruct((1, w), F32), jax.ShapeDtypeStruct((groups, blk, blk), F32),
                   jax.ShapeDtypeStruct((1, w), F32), jax.ShapeDtypeStruct((groups, blk, blk), F32),
                   jax.ShapeDtypeStruct((1, w), F32), jax.ShapeDtypeStruct((1, w), F32)],
        scratch_shapes=[pltpu.VMEM((tm + SUBLANES, w), F32), pltpu.VMEM((tm, w), F32), pltpu.VMEM((tm, w), F32),
                        pltpu.VMEM((tm + SUBLANES, w), F32), pltpu.VMEM((tm + SUBLANES, w), F32),
                        pltpu.VMEM((tm + SUBLANES, w), F32), pltpu.VMEM((SUBLANES, w), F32)],
        compiler_params=_params(("arbitrary",)),
    )(dhs, hs, hs, r, ig, uc, u, u, a, mult, conv_w, w_rg, w_ig, lam)


def _norm_matmul_bwd(name, x, g, w, dys, tm, resid=None, prologue=None, extra_out=None, exchange=None):
    tp, kin = x.shape
    n = w.shape[1]
    nt = tp // tm
    n_dy = len(dys)
    has_res = resid is not None
    has_extra = extra_out is not None

    def compute(*refs):
        x_ref, g_ref, w_ref = refs[:3]
        dy_refs = refs[3:3 + n_dy]
        pos = 3 + n_dy
        res_ref = refs[pos] if has_res else None
        pos += int(has_res)
        dx_ref, dw_ref, dg_ref = refs[pos:pos + 3]
        pos += 3
        ex_ref = refs[pos] if has_extra else None
        pos += int(has_extra)
        dy_s = refs[pos]

        @pl.when(pl.program_id(0) == 0)
        def _():
            dw_ref[...] = jnp.zeros_like(dw_ref)
            dg_ref[...] = jnp.zeros_like(dg_ref)

        if prologue is None:
            c0 = 0
            for ref in dy_refs:
                dy_s[:, c0:c0 + ref.shape[1]] = ref[...].astype(BF16)
                c0 += ref.shape[1]
        else:
            prologue(dy_refs, dy_s, ex_ref)

        xv = x_ref[...]
        gv = g_ref[...]
        r = _rms(xv)
        xh = xv * r
        dyb = dy_s[...]
        dn = _dot_nt(dyb, w_ref[...])
        dw_ref[...] += _dot_tn((xh * gv).astype(BF16), dyb)
        dg_ref[...] += jnp.sum(dn * xh, axis=0, keepdims=True)
        dxh = dn * gv
        dx = r * (dxh - xh * jnp.mean(dxh * xh, axis=-1, keepdims=True))
        if has_res:
            dx = dx + res_ref[...]
        dx_ref[...] = dx

    row = lambda width: pl.BlockSpec((tm, width), lambda i: (i, 0))
    in_specs = [row(kin), pl.BlockSpec((1, kin), lambda i: (0, 0)), pl.BlockSpec((kin, n), lambda i: (0, 0))]
    in_specs += [row(a.shape[1]) for a in dys]
    args = [x, g, w, *dys]
    if has_res:
        in_specs.append(row(kin))
        args.append(resid)
    out_specs = [row(kin), pl.BlockSpec((kin, n), lambda i: (0, 0)), pl.BlockSpec((1, kin), lambda i: (0, 0))]
    out_shape = [jax.ShapeDtypeStruct((tp, kin), F32), jax.ShapeDtypeStruct((kin, n), F32),
                 jax.ShapeDtypeStruct((1, kin), F32)]
    if has_extra:
        out_specs.append(row(extra_out[0]))
        out_shape.append(jax.ShapeDtypeStruct((tp, extra_out[0]), extra_out[1]))
    scratch = [pltpu.VMEM((tm, n), BF16)]
    body = _with_exchange(exchange, len(in_specs), len(out_specs),
                          lambda: pl.program_id(0) == 0, lambda: pl.program_id(0) == nt - 1, compute)
    if exchange is not None:
        in_specs = in_specs + exchange.specs(exchange.n_in)
        out_specs = out_specs + exchange.specs(exchange.n_out)
        out_shape = out_shape + exchange.out_shapes
        args, scratch = args + exchange.arrays, scratch + exchange.scratch()
    return pl.pallas_call(
        body, name=name, grid=(nt,),
        in_specs=in_specs, out_specs=out_specs, out_shape=out_shape, scratch_shapes=scratch,
        compiler_params=_params(("arbitrary",)),
    )(*args)


def _flash_bwd(q, k, v, lse, delta, do, heads, pad, tq, exchange=None):
    tp = q.shape[0]
    nq = tp // tq
    scale = (QK_NOPE + QK_ROPE) ** -0.5
    c2 = scale * LOG2E

    def compute(q_ref, k_ref, v_ref, lse_ref, delta_ref, do_ref, dq_ref, dk_ref, dv_ref):
        j = pl.program_id(1)

        @pl.when(j == 0)
        def _():
            dq_ref[...] = jnp.zeros_like(dq_ref)

        kv = k_ref[...]
        vv = v_ref[...]

        def rows_of(ref, i, blocks):
            parts = [ref[i + b][0:1, :] for b in range(blocks)]
            return parts[0] if blocks == 1 else jnp.concatenate(parts, axis=1)

        def make_step(masked, blocks):
            def step(i, carry):
                dk, dv = carry
                off = pl.multiple_of(i * tq, tq)
                qv = q_ref[pl.ds(off, blocks * tq), :]
                dob = do_ref[pl.ds(off, blocks * tq), :].astype(BF16)
                p = jnp.exp2(_dot_nt(kv, qv) * c2 - rows_of(lse_ref, i, blocks))
                if masked:
                    key = j * tq + lax.broadcasted_iota(jnp.int32, (tq, tq), 0)
                    qry = j * tq + lax.broadcasted_iota(jnp.int32, (tq, tq), 1)
                    first = jnp.where((key <= qry) & (key >= pad), p[:, :tq], 0.0)
                    p = first if blocks == 1 else jnp.concatenate([first, p[:, tq:]], axis=1)
                dv = dv + _dot(p.astype(BF16), dob)
                dp = _dot_nt(vv, dob)
                ds = (p * (dp - rows_of(delta_ref, i, blocks)) * scale).astype(BF16)
                dk = dk + _dot(ds, qv)
                dq_ref[pl.ds(off, blocks * tq), :] += _dot_tn(ds, kv)
                return dk, dv
            return step

        odd = (nq - j) % 2
        carry = (jnp.zeros((tq, HEAD_W), F32), jnp.zeros((tq, V_HEAD), F32))
        carry = lax.fori_loop(0, odd, lambda t, cr: make_step(True, 1)(j, cr), carry)
        carry = lax.fori_loop(0, 1 - odd, lambda t, cr: make_step(True, 2)(j, cr), carry)
        start = j + 2 - odd
        for blocks in (4, 2):
            trips = (nq - start) // blocks
            step_n = make_step(False, blocks)
            carry = lax.fori_loop(0, trips, lambda t, cr, s=start, b=blocks, f=step_n: f(s + b * t, cr), carry)
            start = start + blocks * trips
        dk, dv = carry
        dk_ref[...] = dk
        dv_ref[...] = dv

    in_specs = [pl.BlockSpec((tp, HEAD_W), lambda h, j: (0, h)),
                pl.BlockSpec((tq, HEAD_W), lambda h, j: (j, h)),
                pl.BlockSpec((tq, V_HEAD), lambda h, j: (j, h)),
                pl.BlockSpec((None, nq, SUBLANES, tq), lambda h, j: (h, 0, 0, 0)),
                pl.BlockSpec((None, nq, SUBLANES, tq), lambda h, j: (h, 0, 0, 0)),
                pl.BlockSpec((tp, V_HEAD), lambda h, j: (0, h))]
    out_specs = [pl.BlockSpec((tp, HEAD_W), lambda h, j: (0, h)),
                 pl.BlockSpec((tq, HEAD_W), lambda h, j: (j, h)),
                 pl.BlockSpec((tq, V_HEAD), lambda h, j: (j, h))]
    out_shape = [jax.ShapeDtypeStruct((tp, heads * HEAD_W), F32),
                 jax.ShapeDtypeStruct((tp, heads * HEAD_W), F32),
                 jax.ShapeDtypeStruct((tp, heads * V_HEAD), F32)]
    args, scratch = [q, k, v, lse, delta, do], []
    body = _with_exchange(exchange, len(in_specs), len(out_specs),
                          lambda: (pl.program_id(0) == 0) & (pl.program_id(1) == 0),
                          lambda: (pl.program_id(0) == heads - 1) & (pl.program_id(1) == nq - 1), compute)
    if exchange is not None:
        in_specs = in_specs + exchange.specs(exchange.n_in)
        out_specs = out_specs + exchange.specs(exchange.n_out)
        out_shape = out_shape + exchange.out_shapes
        args, scratch = args + exchange.arrays, exchange.scratch()
    return pl.pallas_call(
        body, name="a_flash_bwd", grid=(heads, nq),
        in_specs=in_specs, out_specs=out_specs, out_shape=out_shape, scratch_shapes=scratch,
        compiler_params=_params(("arbitrary", "arbitrary")),
    )(*args)


def _position():
    return lax.axis_index("x"), lax.axis_index("y"), lax.axis_index("c")


def _other_chips(x, y):
    return [(1 - x, y), (x, 1 - y), (1 - x, 1 - y)]


def _block(ref, shard_axis, n, k, split_axis=None, m=None, h=None):
    idx = []
    for a in range(len(ref.shape)):
        start = 0
        size = None
        if a == shard_axis:
            start, size = k * n, n
        if a == split_axis:
            size = (n if a == shard_axis else m) // 2
            start = start + h * size
        idx.append(slice(None) if size is None else pl.ds(start, size))
    return ref.at[tuple(idx)]


def _gather_weights(split, whole_small):
    ns, nw = len(split), len(whole_small)
    n = ns + nw
    arrs = [s[0] for s in split] + [s[0] for s in whole_small]
    axes = [s[1] for s in split] + [s[1] for s in whole_small]

    def body(*refs):
        ins, outs = refs[:n], refs[n:2 * n]
        ici_send, ici_recv, d2d_send, d2d_recv, sib_send, sib_recv = refs[2 * n:]
        x, y, c = _position()
        me = 2 * x + y
        others = _other_chips(x, y)
        sent, local = [], []

        def remote(src, dst, sems, idx, to):
            return pltpu.make_async_remote_copy(src_ref=src, dst_ref=dst, send_sem=sems[0].at[idx],
                                                recv_sem=sems[1].at[idx], device_id=to, device_id_type=MESH)

        for a in range(n):
            width = ins[a].shape[axes[a]]
            mine = remote(ins[a], _block(outs[a], axes[a], width, me), (sib_send, sib_recv), a, (x, y, 1 - c))
            mine.start()
            local.append(mine)
            for j, (px, py) in enumerate(others):
                if a < ns:
                    sx = split[a][2]
                    src = _block(ins[a], None, None, None, sx, ins[a].shape[sx], c)
                    dst = _block(outs[a], axes[a], width, me, sx, outs[a].shape[sx], c)
                else:
                    src, dst = ins[a], _block(outs[a], axes[a], width, me)
                cp = remote(src, dst, (ici_send, ici_recv), 3 * a + j, (px, py, c))
                cp.start()
                sent.append(cp)
        for a in range(ns):
            width = ins[a].shape[axes[a]]
            sx = split[a][2]
            for j, (px, py) in enumerate(others):
                theirs = _block(outs[a], axes[a], width, 2 * px + py, sx, outs[a].shape[sx], c)
                remote(theirs, theirs, (ici_send, ici_recv), 3 * a + j, (px, py, c)).wait_recv()
                fwd = remote(theirs, theirs, (d2d_send, d2d_recv), 3 * a + j, (x, y, 1 - c))
                fwd.start()
                sent.append(fwd)
        for a in range(ns, n):
            width = ins[a].shape[axes[a]]
            for j, (px, py) in enumerate(others):
                theirs = _block(outs[a], axes[a], width, 2 * px + py)
                remote(theirs, theirs, (ici_send, ici_recv), 3 * a + j, (px, py, c)).wait_recv()
        for a in range(ns):
            width = ins[a].shape[axes[a]]
            sx = split[a][2]
            for j, (px, py) in enumerate(others):
                from_sibling = _block(outs[a], axes[a], width, 2 * px + py, sx, outs[a].shape[sx], 1 - c)
                remote(from_sibling, from_sibling, (d2d_send, d2d_recv), 3 * a + j, (x, y, 1 - c)).wait_recv()
        for cp in sent:
            cp.wait_send()
        for cp in local:
            cp.wait()

    def whole_shape(a, axis):
        shape = list(a.shape)
        shape[axis] *= N_CHIPS
        return jax.ShapeDtypeStruct(tuple(shape), a.dtype)

    any_spec = pl.BlockSpec(memory_space=pl.ANY)
    return pl.pallas_call(
        body, name="gather_weights",
        in_specs=[any_spec] * n, out_specs=[any_spec] * n,
        out_shape=[whole_shape(a, ax) for a, ax in zip(arrs, axes)],
        scratch_shapes=[pltpu.SemaphoreType.DMA((3 * n,)), pltpu.SemaphoreType.DMA((3 * n,)),
                        pltpu.SemaphoreType.DMA((3 * ns,)), pltpu.SemaphoreType.DMA((3 * ns,)),
                        pltpu.SemaphoreType.DMA((n,)), pltpu.SemaphoreType.DMA((n,))],
        compiler_params=pltpu.CompilerParams(has_side_effects=True),
    )(*arrs)


class _Grad:
    def __init__(self, name, g, kind, rh, cols, groups=None):
        self.name, self.g, self.kind, self.rh, self.cols, self.groups = name, g, kind, rh, cols, groups
        if kind == 'rows':
            self.tr = rh
        elif kind == 'gate':
            self.tr = rh // (groups // 2)
        else:
            self.tr = rh if rh <= 512 else 256
        self.nb = rh // self.tr

    def pieces(self, ref, k, h):
        rh, cols = self.rh, self.cols
        if self.kind == 'cols':
            return [(ref.at[pl.ds(h * rh, rh), pl.ds(k * cols, cols)], 0, rh)]
        if self.kind == 'rows':
            return [(ref.at[pl.ds((2 * k + h) * rh, rh), :], 0, rh)]
        if self.kind == 'lead':
            return [(ref.at[k, pl.ds(h * rh, rh), :], 0, rh)]
        per = self.groups // 2
        return [(ref.at[pl.ds((((h * per + gi) * N_CHIPS) + k) * self.tr, self.tr), :], gi * self.tr, self.tr)
                for gi in range(per)]

    def block_spec(self):
        tr, nb, cols = self.tr, self.nb, self.cols
        if self.kind == 'cols':
            return pl.BlockSpec((tr, cols), lambda k, i, c: (c[0] * nb + i, k))
        if self.kind == 'rows':
            return pl.BlockSpec((tr, cols), lambda k, i, c: (2 * k + c[0], 0))
        if self.kind == 'lead':
            return pl.BlockSpec((None, tr, cols), lambda k, i, c: (k, c[0] * nb + i, 0))
        return pl.BlockSpec((tr, cols), lambda k, i, c: ((c[0] * nb + i) * N_CHIPS + k, 0))


class _Exchange:
    def __init__(self, name, arrays, out_shapes, n_copies, copies):
        self.name, self.arrays, self.out_shapes, self.n_copies, self.copies = name, arrays, out_shapes, n_copies, copies
        self.n_in, self.n_out = len(arrays), len(out_shapes)

    def specs(self, n):
        return [pl.BlockSpec(memory_space=pl.ANY)] * n

    def scratch(self):
        return [pltpu.SemaphoreType.DMA((self.n_copies,)), pltpu.SemaphoreType.DMA((self.n_copies,))]

    def _descriptors(self, in_refs, out_refs, sems):
        return self.copies(in_refs, out_refs, sems[0], sems[1])

    def start(self, in_refs, out_refs, sems):
        for cp in self._descriptors(in_refs, out_refs, sems):
            cp.start()

    def finish(self, in_refs, out_refs, sems):
        for cp in self._descriptors(in_refs, out_refs, sems):
            cp.wait()

    def __add__(self, other):
        def copies(ins, outs, send_sems, recv_sems, base=0):
            return (self.copies(ins[:self.n_in], outs[:self.n_out], send_sems, recv_sems, base)
                    + other.copies(ins[self.n_in:], outs[self.n_out:], send_sems, recv_sems, base + self.n_copies))

        return _Exchange(self.name + "_" + other.name, self.arrays + other.arrays, self.out_shapes + other.out_shapes,
                         self.n_copies + other.n_copies, copies)

    def run(self):
        def body(*refs):
            ins, outs, sems = refs[:self.n_in], refs[self.n_in:self.n_in + self.n_out], refs[self.n_in + self.n_out:]
            self.start(ins, outs, sems)
            self.finish(ins, outs, sems)

        return pl.pallas_call(
            body, name=self.name,
            in_specs=self.specs(self.n_in), out_specs=self.specs(self.n_out), out_shape=self.out_shapes,
            scratch_shapes=self.scratch(),
            compiler_params=pltpu.CompilerParams(has_side_effects=True),
        )(*self.arrays)


def _gather_whole(name, shards):
    def copies(ins, outs, send_sems, recv_sems, base=0):
        x, y, c = _position()
        me = 2 * x + y
        made = []
        for a, (_, axis) in enumerate(shards):
            dst = _block(outs[a], axis, ins[a].shape[axis], me)
            for j, to in enumerate([(x, y, 1 - c)] + [(px, py, c) for px, py in _other_chips(x, y)]):
                idx = base + 4 * a + j
                made.append(pltpu.make_async_remote_copy(
                    src_ref=ins[a], dst_ref=dst, send_sem=send_sems.at[idx], recv_sem=recv_sems.at[idx],
                    device_id=to, device_id_type=MESH))
        return made

    def whole_shape(a, axis):
        shape = list(a.shape)
        shape[axis] *= N_CHIPS
        return jax.ShapeDtypeStruct(tuple(shape), a.dtype)

    return _Exchange(name, [s[0] for s in shards], [whole_shape(*s) for s in shards], 4 * len(shards), copies)


def _halves_to_sibling(name, grads):
    total = sum(len(gr.pieces(gr.g, 0, 0)) * N_CHIPS for gr in grads)

    def copies(ins, outs, send_sems, recv_sems, base=0):
        x, y, c = _position()
        made = []
        for gr, g_ref, got_ref in zip(grads, ins, outs):
            for k in range(N_CHIPS):
                for src, r0, nr in gr.pieces(g_ref, k, 1 - c):
                    idx = base + len(made)
                    made.append(pltpu.make_async_remote_copy(
                        src_ref=src, dst_ref=got_ref.at[k, pl.ds(r0, nr), :],
                        send_sem=send_sems.at[idx], recv_sem=recv_sems.at[idx],
                        device_id=(x, y, 1 - c), device_id_type=MESH))
        return made

    return _Exchange(name, [gr.g for gr in grads],
                     [jax.ShapeDtypeStruct((N_CHIPS, gr.rh, gr.cols), F32) for gr in grads], total, copies)


def _chip_sum(gr, got, c, wire=BF16):
    def body(c_ref, g_ref, got_ref, o_ref):
        o_ref[...] = (g_ref[...] + got_ref[...]).astype(wire)

    tile = pl.BlockSpec((None, gr.tr, gr.cols), lambda k, i, c_ref: (k, i, 0))
    return pl.pallas_call(
        body, name="chip_sum_" + gr.name,
        grid_spec=pltpu.PrefetchScalarGridSpec(
            num_scalar_prefetch=1, grid=(N_CHIPS, gr.nb),
            in_specs=[gr.block_spec(), tile], out_specs=tile),
        out_shape=jax.ShapeDtypeStruct((N_CHIPS, gr.rh, gr.cols), wire),
        compiler_params=_params(("parallel", "parallel")),
    )(c, gr.g, got)


def _blocks_to_chips(name, parts):
    n = len(parts)

    def copies(ins, outs, send_sems, recv_sems, base=0):
        x, y, c = _position()
        made = []
        for a in range(n):
            for j, (px, py) in enumerate(_other_chips(x, y)):
                idx = base + 3 * a + j
                made.append(pltpu.make_async_remote_copy(
                    src_ref=ins[a].at[2 * px + py], dst_ref=outs[a].at[j],
                    send_sem=send_sems.at[idx], recv_sem=recv_sems.at[idx],
                    device_id=(px, py, c), device_id_type=MESH))
        return made

    return _Exchange(name, parts, [jax.ShapeDtypeStruct((3,) + p.shape[1:], p.dtype) for p in parts], 3 * n, copies)


def _sum_chips(name, part, got, me, c):
    nk, rh, cols = part.shape
    tr = rh if rh <= 512 else 256
    nb = rh // tr

    def body(me_ref, c_ref, own_ref, *rest):
        got_refs, o_ref = rest[:nk], rest[nk]
        own = own_ref[...].astype(F32)
        acc = None
        for k in range(nk):
            term = jnp.where(me_ref[0] == k, own, got_refs[k][...].astype(F32))
            acc = term if acc is None else acc + term
        o_ref[...] = acc

    def got_map(k):
        def index(i, me_ref, c_ref):
            xor = jnp.bitwise_xor(me_ref[0], k)
            slot = jnp.where(xor == 1, 1, jnp.where(xor == 3, 2, 0))
            return (slot, i, 0)
        return index

    return pl.pallas_call(
        body, name="sum_" + name,
        grid_spec=pltpu.PrefetchScalarGridSpec(
            num_scalar_prefetch=2, grid=(nb,),
            in_specs=[pl.BlockSpec((None, tr, cols), lambda i, me_ref, c_ref: (me_ref[0], i, 0))]
            + [pl.BlockSpec((None, tr, cols), got_map(k)) for k in range(nk)],
            out_specs=pl.BlockSpec((tr, cols), lambda i, me_ref, c_ref: (c_ref[0] * nb + i, 0))),
        out_shape=jax.ShapeDtypeStruct((2 * rh, cols), F32),
        compiler_params=_params(("parallel",)),
    )(me, c, part, *([got] * nk))


def _share_with_sibling(halves):
    n = len(halves)

    def body(*refs):
        outs = refs[n:2 * n]
        send_sems, recv_sems = refs[2 * n:]
        x, y, c = _position()
        copies = []
        for a in range(n):
            rh = outs[a].shape[0] // 2
            mine = outs[a].at[pl.ds(c * rh, rh), :]
            cp = pltpu.make_async_remote_copy(
                src_ref=mine, dst_ref=mine, send_sem=send_sems.at[a], recv_sem=recv_sems.at[a],
                device_id=(x, y, 1 - c), device_id_type=MESH)
            cp.start()
            copies.append(cp)
        for cp in copies:
            cp.wait()

    any_spec = pl.BlockSpec(memory_space=pl.ANY)
    return pl.pallas_call(
        body, name="grads_share",
        in_specs=[any_spec] * n, out_specs=[any_spec] * n,
        out_shape=[jax.ShapeDtypeStruct(h.shape, h.dtype) for h in halves],
        input_output_aliases={a: a for a in range(n)},
        scratch_shapes=[pltpu.SemaphoreType.DMA((n,)), pltpu.SemaphoreType.DMA((n,))],
        compiler_params=pltpu.CompilerParams(has_side_effects=True),
    )(*halves)


def _adamw(name, w, g, m, v):
    rows, cols = w.shape
    tr = 256 if rows % 256 == 0 else rows

    def body(w_ref, g_ref, m_ref, v_ref, d_ref, nm_ref, nv_ref):
        gv = g_ref[...]
        mn = ADAM_B1 * m_ref[...] + (1.0 - ADAM_B1) * gv
        vn = ADAM_B2 * v_ref[...] + (1.0 - ADAM_B2) * (gv * gv)
        m_hat = mn / (1.0 - ADAM_B1 ** ADAM_STEP)
        v_hat = vn / (1.0 - ADAM_B2 ** ADAM_STEP)
        d_ref[...] = -ADAM_LR * (m_hat / (jnp.sqrt(v_hat) + ADAM_EPS) + ADAM_WD * w_ref[...])
        nm_ref[...] = mn
        nv_ref[...] = vn

    spec = pl.BlockSpec((tr, cols), lambda i: (i, 0))
    return pl.pallas_call(
        body, name=name, grid=(rows // tr,),
        in_specs=[spec] * 4, out_specs=[spec] * 3,
        out_shape=[jax.ShapeDtypeStruct((rows, cols), F32)] * 3,
        compiler_params=_params(("parallel",)),
    )(w, g, m, v)


def _as2d(a):
    if a.ndim == 1:
        return a.reshape(1, -1)
    return a.reshape(-1, a.shape[-1])


def _unshard(gathered, axis):
    moved = jnp.moveaxis(gathered, 0, axis)
    shape = list(gathered.shape[1:])
    shape[axis] *= N_CHIPS
    return moved.reshape(shape)


def _rope_tables(tp, pad):
    pos = jnp.arange(tp, dtype=F32) - pad
    inv_freq = ROPE_BASE ** (-jnp.arange(0, QK_ROPE, 2, dtype=F32) / QK_ROPE)
    ang = pos[:, None] * inv_freq[None, :]
    cos, sin = jnp.cos(ang), jnp.sin(ang)
    zeros = jnp.zeros((tp, LANES - QK_ROPE), F32)
    return jnp.concatenate([cos, cos, zeros], axis=1), jnp.concatenate([-sin, sin, zeros], axis=1)


def _matrix_grad(name, g, heads):
    if name in ('b_w_rg', 'b_w_ig'):
        groups, blk, cols = g.shape
        return _Grad(name, g.reshape(groups * blk, cols), 'gate', (groups // 2) * (blk // N_CHIPS), cols, groups)
    rows, cols = g.shape
    if name in ('a_w_out', 'b_w_out'):
        return _Grad(name, g, 'rows', rows // (2 * N_CHIPS), cols)
    if name == 'a_w_uq' and heads % N_CHIPS != 0:
        g = g.reshape(rows, heads, HEAD_W)[:, :, :QK_NOPE + QK_ROPE].reshape(rows, -1)
        cols = g.shape[1]
    if name == 'a_w_in' or (name == 'a_w_uq' and heads % N_CHIPS != 0):
        g = jnp.moveaxis(g.reshape(rows, N_CHIPS, cols // N_CHIPS), 1, 0)
        return _Grad(name, g, 'lead', rows // 2, cols // N_CHIPS)
    return _Grad(name, g, 'cols', rows // 2, cols // N_CHIPS)


def _kernel_form(name, w):
    return w[0] if name in ('b_w_rg', 'b_w_ig', 'b_conv_w') else _as2d(w)


def _local_grads(x, target, wt, heads, c_idx, mid_names, mid_gather, late_names, late_gather):
    wt = dict(wt)
    seq, d = x.shape
    n_meta = wt['meta_tokens'].shape[0]
    t = seq + n_meta
    pad = (-t) % Q_BLOCK
    tp = t + pad
    x0 = pad + n_meta
    tm = _row_tile(tp)
    ql = wt['a_q_norm_g'].shape[1]
    kvl = wt['a_kv_norm_g'].shape[1]
    mla_w = heads * V_HEAD

    h0 = jnp.concatenate([jnp.zeros((pad, d), F32), wt['meta_tokens'], x], axis=0)
    cos_t, sin_t = _rope_tables(tp, pad)

    w_in_a = wt['a_w_in']
    zcol = jnp.zeros((d, LANES - QK_ROPE), BF16)
    w_in_a = jnp.concatenate([w_in_a[:, :ql + kvl + QK_ROPE], zcol, w_in_a[:, ql + kvl + QK_ROPE:]], axis=1)
    c_kv, c_kr, c_gate = ql, ql + kvl, ql + kvl + LANES
    splits_a = [(0, c_kv), (c_kv, c_kr), (c_kr, c_gate), (c_gate, c_gate + mla_w)]

    q_lat, kv_lat, kr_raw, gate_a, *mid_whole = _norm_matmul("a_in_proj", h0, wt['a_norm_g'], w_in_a, splits_a, tm,
                                                             exchange=mid_gather)
    wt.update({n: _kernel_form(n, w) for n, w in zip(mid_names, mid_whole)})
    w_uq = wt['a_w_uq'].reshape(ql, heads, QK_NOPE + QK_ROPE)
    w_uq = jnp.pad(w_uq, ((0, 0), (0, 0), (0, HEAD_W - QK_NOPE - QK_ROPE))).reshape(ql, heads * HEAD_W)
    w_ukv = wt['a_w_ukv']
    q = _q_proj(q_lat, wt['a_q_norm_g'], w_uq, cos_t, sin_t, heads, tm)
    k, v = _kv_proj(kv_lat, wt['a_kv_norm_g'], w_ukv, kr_raw, cos_t, sin_t, heads, tm)
    attn, lse, *late_whole = _flash_fwd(q, k, v, heads, pad, tm, exchange=late_gather)
    wt.update({n: _kernel_form(n, w) for n, w in zip(late_names, late_whole)})
    lru_w = wt['b_conv_w'].shape[1]
    h1 = _gated_out("a_out_proj", attn, gate_a, wt['a_w_out'], h0, tm)

    u, gate_b = _norm_matmul("b_in_proj", h1, wt['b_norm_g'], wt['b_w_in'], [(0, lru_w), (lru_w, 2 * lru_w)], tm)
    uc, r, ig, hs, decay, mult = _rglru_fwd(u, wt['b_conv_w'], wt['b_conv_b'], wt['b_w_rg'], wt['b_b_rg'],
                                            wt['b_w_ig'], wt['b_b_ig'], wt['b_lam'], pad, tm)
    h2 = _gated_out("b_out_proj", hs, gate_b, wt['b_w_out'], h1, tm)

    dh2, loss, d_final_g = _final_loss(h2, wt['final_norm_g'], target, x0, tm)

    dhs, dgate_b, dw_out_b = _gated_out_bwd("b_out_proj_bwd", dh2, hs, gate_b, wt['b_w_out'], tm)
    du, dconv_w, dconv_b, dw_rg, db_rg, dw_ig, db_ig, dlam = _rglru_bwd(
        dhs, hs, r, ig, uc, u, decay, mult, wt['b_conv_w'], wt['b_w_rg'], wt['b_w_ig'], wt['b_lam'], pad, tm)
    dh1, dw_in_b, dg_b = _norm_matmul_bwd("b_in_proj_bwd", h1, wt['b_norm_g'], wt['b_w_in'], [du, dgate_b], tm, resid=dh2)

    grads_b = [_matrix_grad(n, g, heads) for n, g in
               (('b_w_in', dw_in_b), ('b_w_rg', dw_rg), ('b_w_ig', dw_ig), ('b_w_out', dw_out_b))]
    dattn, dgate_a, dw_out_a, delta, *got = _gated_out_bwd(
        "a_out_proj_bwd", dh1, attn, gate_a, wt['a_w_out'], tm, delta_heads=heads,
        exchange=_halves_to_sibling("swap_b", grads_b))
    sums_b = [_chip_sum(gr, r, c_idx) for gr, r in zip(grads_b, got)]
    grad_out = _matrix_grad('a_w_out', dw_out_a, heads)
    dq, dk, dv, *landed = _flash_bwd(
        q, k, v, lse, delta, dattn, heads, pad, tm,
        exchange=_blocks_to_chips("chips_b", sums_b) + _halves_to_sibling("swap_out", [grad_out]))
    through = list(zip(grads_b, sums_b, landed[:len(grads_b)]))
    sum_out = _chip_sum(grad_out, landed[len(grads_b)], c_idx)

    def q_prologue(dy_refs, dy_s, ex_ref):
        (dq_ref,), cos_v, sin_v = dy_refs[:1], dy_refs[1][...], dy_refs[2][...]
        for h in range(heads):
            c0 = h * HEAD_W
            dy_s[:, c0:c0 + QK_NOPE] = dq_ref[:, c0:c0 + QK_NOPE].astype(BF16)
            dy_s[:, c0 + QK_NOPE:c0 + HEAD_W] = _unrope(dq_ref[:, c0 + QK_NOPE:c0 + HEAD_W], cos_v, sin_v).astype(BF16)

    dq_lat, dw_uq, dg_q, from_chips_out = _norm_matmul_bwd(
        "a_q_proj_bwd", q_lat, wt['a_q_norm_g'], w_uq, [dq, cos_t, sin_t], tm, prologue=q_prologue,
        exchange=_blocks_to_chips("chips_out", [sum_out]))
    through.append((grad_out, sum_out, from_chips_out))
    grad_uq = _matrix_grad('a_w_uq', dw_uq, heads)

    def kv_prologue(dy_refs, dy_s, ex_ref):
        dk_ref, dv_ref = dy_refs[:2]
        cos_v, sin_v = dy_refs[2][...], dy_refs[3][...]
        dkr = jnp.zeros((dk_ref.shape[0], LANES), F32)
        for h in range(heads):
            c0 = h * (QK_NOPE + V_HEAD)
            dy_s[:, c0:c0 + QK_NOPE] = dk_ref[:, h * HEAD_W:h * HEAD_W + QK_NOPE].astype(BF16)
            dy_s[:, c0 + QK_NOPE:c0 + QK_NOPE + V_HEAD] = dv_ref[:, h * V_HEAD:(h + 1) * V_HEAD].astype(BF16)
            dkr = dkr + dk_ref[:, h * HEAD_W + QK_NOPE:(h + 1) * HEAD_W]
        ex_ref[...] = _unrope(dkr, cos_v, sin_v)

    dkv_lat, dw_ukv, dg_kv, dkr_raw, got_uq = _norm_matmul_bwd(
        "a_kv_proj_bwd", kv_lat, wt['a_kv_norm_g'], w_ukv, [dk, dv, cos_t, sin_t], tm,
        prologue=kv_prologue, extra_out=(LANES, F32), exchange=_halves_to_sibling("swap_uq", [grad_uq]))
    sum_uq = _chip_sum(grad_uq, got_uq, c_idx)
    grad_ukv = _matrix_grad('a_w_ukv', dw_ukv, heads)

    dh0, dw_in_a, dg_a, from_chips_uq, got_ukv = _norm_matmul_bwd(
        "a_in_proj_bwd", h0, wt['a_norm_g'], w_in_a, [dq_lat, dkv_lat, dkr_raw, dgate_a], tm, resid=dh1,
        exchange=_blocks_to_chips("chips_uq", [sum_uq]) + _halves_to_sibling("swap_ukv", [grad_ukv]))
    through.append((grad_uq, sum_uq, from_chips_uq))
    swapped = [(grad_ukv, _chip_sum(grad_ukv, got_ukv, c_idx))]

    dw_in_a = jnp.concatenate([dw_in_a[:, :c_kr + QK_ROPE], dw_in_a[:, c_gate:]], axis=1)
    grads = {
        'meta_tokens': dh0[pad:x0], 'a_norm_g': dg_a, 'a_w_in': dw_in_a, 'a_q_norm_g': dg_q, 'a_kv_norm_g': dg_kv,
        'a_w_uq': dw_uq, 'a_w_ukv': dw_ukv, 'a_w_out': dw_out_a, 'b_norm_g': dg_b, 'b_w_in': dw_in_b,
        'b_conv_w': dconv_w, 'b_conv_b': dconv_b, 'b_w_rg': dw_rg, 'b_b_rg': db_rg, 'b_w_ig': dw_ig,
        'b_b_ig': db_ig, 'b_lam': dlam, 'b_w_out': dw_out_b, 'final_norm_g': d_final_g,
    }
    return loss, dh0[x0:], grads, through, swapped


def _chip_major(whole, local_shape, axis):
    if axis is None:
        return jnp.broadcast_to(whole.reshape(1, -1), (N_CHIPS, whole.size))
    shape = list(local_shape)
    g = whole.reshape(shape[:axis] + [N_CHIPS, shape[axis]] + shape[axis + 1:])
    return jnp.moveaxis(g, axis, 0).reshape(N_CHIPS, -1)


def kernel(x, meta_tokens, a_norm_g, a_w_in, a_q_norm_g, a_kv_norm_g, a_w_uq, a_w_ukv, a_w_out, b_norm_g, b_w_in, b_conv_w, b_conv_b, b_w_rg, b_b_rg, b_w_ig, b_b_ig, b_lam, b_w_out, final_norm_g, loss_target, m_meta_tokens, m_a_norm_g, m_a_w_in, m_a_q_norm_g, m_a_kv_norm_g, m_a_w_uq, m_a_w_ukv, m_a_w_out, m_b_norm_g, m_b_w_in, m_b_conv_w, m_b_conv_b, m_b_w_rg, m_b_b_rg, m_b_w_ig, m_b_b_ig, m_b_lam, m_b_w_out, m_final_norm_g, v_meta_tokens, v_a_norm_g, v_a_w_in, v_a_q_norm_g, v_a_kv_norm_g, v_a_w_uq, v_a_w_ukv, v_a_w_out, v_b_norm_g, v_b_w_in, v_b_conv_w, v_b_conv_b, v_b_w_rg, v_b_b_rg, v_b_w_ig, v_b_b_ig, v_b_lam, v_b_w_out, v_final_norm_g):
    local_w = dict(zip(WEIGHTS, (meta_tokens, a_norm_g, a_w_in, a_q_norm_g, a_kv_norm_g, a_w_uq, a_w_ukv, a_w_out,
                                 b_norm_g, b_w_in, b_conv_w, b_conv_b, b_w_rg, b_b_rg, b_w_ig, b_b_ig, b_lam,
                                 b_w_out, final_norm_g)))
    local_m = dict(zip(WEIGHTS, (m_meta_tokens, m_a_norm_g, m_a_w_in, m_a_q_norm_g, m_a_kv_norm_g, m_a_w_uq,
                                 m_a_w_ukv, m_a_w_out, m_b_norm_g, m_b_w_in, m_b_conv_w, m_b_conv_b, m_b_w_rg,
                                 m_b_b_rg, m_b_w_ig, m_b_b_ig, m_b_lam, m_b_w_out, m_final_norm_g)))
    local_v = dict(zip(WEIGHTS, (v_meta_tokens, v_a_norm_g, v_a_w_in, v_a_q_norm_g, v_a_kv_norm_g, v_a_w_uq,
                                 v_a_w_ukv, v_a_w_out, v_b_norm_g, v_b_w_in, v_b_conv_w, v_b_conv_b, v_b_w_rg,
                                 v_b_b_rg, v_b_w_ig, v_b_b_ig, v_b_lam, v_b_w_out, v_final_norm_g)))
    matrices = ('a_w_in', 'a_w_uq', 'a_w_ukv', 'a_w_out', 'b_w_in', 'b_w_rg', 'b_w_ig', 'b_w_out')
    heads = a_w_uq.shape[-1] * N_CHIPS // (QK_NOPE + QK_ROPE)

    split, small, mid, late = [], [], [], []
    for n in WEIGHTS:
        if SHARD_AXIS[n] is None:
            continue
        if n.startswith('b_') or n == 'a_w_out':
            late.append((n, local_w[n].astype(BF16) if n in matrices else local_w[n], SHARD_AXIS[n]))
        elif n == 'a_w_in':
            split.append((n, local_w[n].astype(BF16)[None], 0, 2))
        elif n in matrices:
            mid.append((n, local_w[n].astype(BF16), SHARD_AXIS[n]))
        else:
            small.append((n, local_w[n], SHARD_AXIS[n]))
    gathered = _gather_weights([s[1:] for s in split], [s[1:] for s in small])
    whole = dict(zip([s[0] for s in split + small], gathered))
    whole['a_w_in'] = _unshard(whole['a_w_in'], SHARD_AXIS['a_w_in'])
    mid_names, late_names = [s[0] for s in mid], [s[0] for s in late]
    wt = {n: _kernel_form(n, whole.get(n, local_w[n])) for n in WEIGHTS if n not in mid_names + late_names}

    c_idx = lax.axis_index("c").astype(jnp.int32).reshape(1)
    me_idx = (2 * lax.axis_index("x") + lax.axis_index("y")).astype(jnp.int32).reshape(1)
    loss, grad_x, grads, through, swapped = _local_grads(
        x[0], loss_target[0], wt, heads, c_idx,
        mid_names, _gather_whole("gather_weights_a", [s[1:] for s in mid]),
        late_names, _gather_whole("gather_weights_b", [s[1:] for s in late]))

    ext_uq = heads % N_CHIPS == 0
    started = [gr.name for gr, *_ in through + swapped]
    last = [_matrix_grad(n, grads[n], heads) for n in matrices if n not in started]
    rest = [n for n in WEIGHTS if n not in matrices]
    pieces = [_chip_major(grads[n], local_w[n].shape, SHARD_AXIS[n]) for n in rest]
    pieces.append(jnp.broadcast_to(loss[0:1, 0:1], (N_CHIPS, 1)))
    length = sum(p.shape[1] for p in pieces)
    unit = 2 * SUBLANES * 1024
    padded = -(-length // unit) * unit
    flat = jnp.concatenate(pieces + [jnp.zeros((N_CHIPS, padded - length), F32)], axis=1)
    last.append(_Grad('small', flat.reshape(N_CHIPS, padded // 1024, 1024), 'lead', padded // 2048, 1024))

    got = _halves_to_sibling("grads_to_sibling", last).run()
    swapped = swapped + [(gr, _chip_sum(gr, r, c_idx, F32 if gr.name == 'small' else BF16))
                         for gr, r in zip(last, got)]
    from_chips = _blocks_to_chips("grads_to_chips", [p for _, p in swapped]).run()
    through = through + [(gr, p, r) for (gr, p), r in zip(swapped, from_chips)]
    halves = [_sum_chips(gr.name, p, r, me_idx, c_idx) for gr, p, r in through]
    summed = dict(zip([gr.name for gr, _, _ in through], _share_with_sibling(halves)))
    if ext_uq:
        g = summed['a_w_uq']
        summed['a_w_uq'] = g.reshape(g.shape[0], -1, HEAD_W)[:, :, :QK_NOPE + QK_ROPE]
    total = summed['small'].reshape(-1)

    out_g, out_d, out_m, out_v = [], [], [], []
    off = 0
    for n in WEIGHTS:
        shape = local_w[n].shape
        if n in matrices:
            g = summed[n].reshape(shape)
        else:
            size = 1
            for s in shape:
                size *= s
            g = total[off:off + size].reshape(shape)
            off += size
        delta, new_m, new_v = _adamw("adamw_" + n, _as2d(local_w[n]), _as2d(g), _as2d(local_m[n]), _as2d(local_v[n]))
        out_g.append(g)
        out_d.append(delta.reshape(shape))
        out_m.append(new_m.reshape(shape))
        out_v.append(new_v.reshape(shape))

    return (total[off], grad_x[None], *out_g, *out_d, *out_m, *out_v)
```

```python
import functools

import jax
import jax.numpy as jnp
from jax import lax
from jax.experimental import pallas as pl
from jax.experimental.pallas import tpu as pltpu

F32 = jnp.float32
BF16 = jnp.bfloat16
MESH = pl.DeviceIdType.MESH

RMS_EPS = 1e-6
QK_NOPE = 128
QK_ROPE = 64
V_HEAD = 128
HEAD_W = 256
ROPE_BASE = 10000.0
Q_BLOCK = 128
MASK_VALUE = -1e30
CONV_WIDTH = 4
LRU_C = 8.0
N_CHIPS = 4

ADAM_LR = 0.001
ADAM_B1 = 0.9
ADAM_B2 = 0.999
ADAM_EPS = 1e-08
ADAM_WD = 0.01
ADAM_STEP = 10

VMEM_LIMIT_V7X = 56 * 1024 * 1024
LANES = 128
SUBLANES = 8

WEIGHTS = ['meta_tokens', 'a_norm_g', 'a_w_in', 'a_q_norm_g', 'a_kv_norm_g', 'a_w_uq', 'a_w_ukv',
           'a_w_out', 'b_norm_g', 'b_w_in', 'b_conv_w', 'b_conv_b', 'b_w_rg', 'b_b_rg', 'b_w_ig',
           'b_b_ig', 'b_lam', 'b_w_out', 'final_norm_g']
SHARD_AXIS = {'meta_tokens': 1, 'a_norm_g': None, 'a_w_in': 2, 'a_q_norm_g': None, 'a_kv_norm_g': None,
              'a_w_uq': 2, 'a_w_ukv': 2, 'a_w_out': 1, 'b_norm_g': 1, 'b_w_in': 2, 'b_conv_w': 2,
              'b_conv_b': 1, 'b_w_rg': 2, 'b_b_rg': 1, 'b_w_ig': 2, 'b_b_ig': 1, 'b_lam': 1,
              'b_w_out': 1, 'final_norm_g': None}


def _params(sem=None):
    return pltpu.CompilerParams(dimension_semantics=sem, vmem_limit_bytes=VMEM_LIMIT_V7X)


def _row_tile(tp):
    return 384 if (tp % 384 == 0 and tp >= 1152) else 128


def _sigmoid(x):
    return 1.0 / (1.0 + jnp.exp(-x))


def _rms(x):
    return lax.rsqrt(jnp.mean(x * x, axis=-1, keepdims=True) + RMS_EPS)


def _swap32(x):
    lane = lax.broadcasted_iota(jnp.int32, x.shape, 1)
    return jnp.where(lane < 32, pltpu.roll(x, 96, 1), pltpu.roll(x, 32, 1))


def _rope(x, cos_t, sin_t):
    return x * cos_t + _swap32(x) * sin_t


def _unrope(d, cos_t, sin_t):
    lane = lax.broadcasted_iota(jnp.int32, d.shape, 1)
    return jnp.where(lane < QK_ROPE, d * cos_t + _swap32(d * sin_t), 0.0)


def _dot(a, b):
    return jnp.dot(a, b, preferred_element_type=F32)


def _dot_nt(a, b):
    return lax.dot_general(a, b, (((1,), (1,)), ((), ())), preferred_element_type=F32)


def _dot_tn(a, b):
    return lax.dot_general(a, b, (((0,), (0,)), ((), ())), preferred_element_type=F32)


def _norm_matmul(name, x, g, w, splits, tm, exchange=None):
    tp, kin = x.shape
    n = w.shape[1]
    nt = tp // tm

    def compute(x_ref, g_ref, w_ref, *outs):
        xv = x_ref[...]
        nrm = ((xv * _rms(xv)) * g_ref[...]).astype(BF16)
        y = _dot(nrm, w_ref[...])
        for o_ref, (c0, c1) in zip(outs, splits):
            o_ref[...] = y[:, c0:c1]

    in_specs = [pl.BlockSpec((tm, kin), lambda i: (i, 0)),
                pl.BlockSpec((1, kin), lambda i: (0, 0)),
                pl.BlockSpec((kin, n), lambda i: (0, 0))]
    out_specs = [pl.BlockSpec((tm, c1 - c0), lambda i: (i, 0)) for c0, c1 in splits]
    out_shape = [jax.ShapeDtypeStruct((tp, c1 - c0), F32) for c0, c1 in splits]
    args, scratch = [x, g, w], []
    body = _with_exchange(exchange, len(in_specs), len(out_specs),
                          lambda: pl.program_id(0) == 0, lambda: pl.program_id(0) == nt - 1, compute)
    if exchange is not None:
        in_specs = in_specs + exchange.specs(exchange.n_in)
        out_specs = out_specs + exchange.specs(exchange.n_out)
        out_shape = out_shape + exchange.out_shapes
        args, scratch = args + exchange.arrays, exchange.scratch()
    return pl.pallas_call(
        body, name=name, grid=(nt,),
        in_specs=in_specs, out_specs=out_specs, out_shape=out_shape, scratch_shapes=scratch,
        compiler_params=_params(("arbitrary",)),
    )(*args)


def _q_proj(q_lat, g, w_uq, cos_t, sin_t, heads, tm):
    tp, kin = q_lat.shape
    n = heads * HEAD_W

    def body(x_ref, g_ref, w_ref, cos_ref, sin_ref, q_ref):
        xv = x_ref[...]
        nrm = ((xv * _rms(xv)) * g_ref[...]).astype(BF16)
        y = _dot(nrm, w_ref[...])
        cos_v, sin_v = cos_ref[...], sin_ref[...]
        for h in range(heads):
            c0 = h * HEAD_W
            q_ref[:, c0:c0 + QK_NOPE] = y[:, c0:c0 + QK_NOPE].astype(BF16)
            q_ref[:, c0 + QK_NOPE:c0 + HEAD_W] = _rope(y[:, c0 + QK_NOPE:c0 + HEAD_W], cos_v, sin_v).astype(BF16)

    return pl.pallas_call(
        body, name="a_q_proj", grid=(tp // tm,),
        in_specs=[pl.BlockSpec((tm, kin), lambda i: (i, 0)),
                  pl.BlockSpec((1, kin), lambda i: (0, 0)),
                  pl.BlockSpec((kin, n), lambda i: (0, 0)),
                  pl.BlockSpec((tm, LANES), lambda i: (i, 0)),
                  pl.BlockSpec((tm, LANES), lambda i: (i, 0))],
        out_specs=pl.BlockSpec((tm, n), lambda i: (i, 0)),
        out_shape=jax.ShapeDtypeStruct((tp, n), BF16),
        compiler_params=_params(("parallel",)),
    )(q_lat, g, w_uq, cos_t, sin_t)


def _kv_proj(kv_lat, g, w_ukv, k_rope_raw, cos_t, sin_t, heads, tm):
    tp, kin = kv_lat.shape
    n = heads * (QK_NOPE + V_HEAD)

    def body(x_ref, g_ref, w_ref, kr_ref, cos_ref, sin_ref, k_ref, v_ref):
        xv = x_ref[...]
        nrm = ((xv * _rms(xv)) * g_ref[...]).astype(BF16)
        y = _dot(nrm, w_ref[...])
        kr = _rope(kr_ref[...], cos_ref[...], sin_ref[...]).astype(BF16)
        for h in range(heads):
            c0 = h * (QK_NOPE + V_HEAD)
            k_ref[:, h * HEAD_W:h * HEAD_W + QK_NOPE] = y[:, c0:c0 + QK_NOPE].astype(BF16)
            k_ref[:, h * HEAD_W + QK_NOPE:(h + 1) * HEAD_W] = kr
            v_ref[:, h * V_HEAD:(h + 1) * V_HEAD] = y[:, c0 + QK_NOPE:c0 + QK_NOPE + V_HEAD].astype(BF16)

    return pl.pallas_call(
        body, name="a_kv_proj", grid=(tp // tm,),
        in_specs=[pl.BlockSpec((tm, kin), lambda i: (i, 0)),
                  pl.BlockSpec((1, kin), lambda i: (0, 0)),
                  pl.BlockSpec((kin, n), lambda i: (0, 0)),
                  pl.BlockSpec((tm, LANES), lambda i: (i, 0)),
                  pl.BlockSpec((tm, LANES), lambda i: (i, 0)),
                  pl.BlockSpec((tm, LANES), lambda i: (i, 0))],
        out_specs=[pl.BlockSpec((tm, heads * HEAD_W), lambda i: (i, 0)),
                   pl.BlockSpec((tm, heads * V_HEAD), lambda i: (i, 0))],
        out_shape=[jax.ShapeDtypeStruct((tp, heads * HEAD_W), BF16),
                   jax.ShapeDtypeStruct((tp, heads * V_HEAD), BF16)],
        compiler_params=_params(("parallel",)),
    )(kv_lat, g, w_ukv, k_rope_raw, cos_t, sin_t)


def _as_rows(col):
    rows = col.shape[0]
    return jnp.transpose(jnp.broadcast_to(col, (rows, LANES)))[0:SUBLANES, :]


def _attn_mask(row0, col0, rows, cols, pad):
    row = row0 + lax.broadcasted_iota(jnp.int32, (rows, cols), 0)
    col = col0 + lax.broadcasted_iota(jnp.int32, (rows, cols), 1)
    return (col <= row) & (col >= pad)


LOG2E = 1.4426950408889634
FLASH_FWD_TRIPS = ((4, 2), (2, 2), (1, 1))


def _flash_fwd(q, k, v, heads, pad, tq, exchange=None):
    tp = q.shape[0]
    nq = tp // tq
    c2 = (QK_NOPE + QK_ROPE) ** -0.5 * LOG2E

    def compute(q_ref, k_ref, v_ref, o_ref, lse_ref):
        i = pl.program_id(1)

        def make_step(masked, blocks, parts=1):
            keys = blocks * tq // parts

            def step(j, carry):
                m, l, acc = carry
                offs = [pl.multiple_of(j * tq + part * keys, tq) for part in range(parts)]
                scores = [_dot_nt(q_ref[...], k_ref[pl.ds(off, keys), :]) for off in offs]
                for off, s in zip(offs, scores):
                    s = s * c2
                    if masked:
                        s = jnp.where(_attn_mask(i * tq, j * tq, tq, keys, pad), s, MASK_VALUE)
                    m_new = jnp.maximum(m, jnp.max(s, axis=-1, keepdims=True))
                    p = jnp.exp2(s - m_new)
                    alpha = jnp.exp2(m - m_new)
                    l = alpha * l + jnp.sum(p, axis=-1, keepdims=True)
                    acc = alpha * acc + _dot(p.astype(BF16), v_ref[pl.ds(off, keys), :])
                    m = m_new
                return m, l, acc
            return step

        init = (jnp.full((tq, 1), MASK_VALUE, F32), jnp.zeros((tq, 1), F32), jnp.zeros((tq, V_HEAD), F32))
        carry = make_step(True, 1)(0, init)
        first = 1
        for blocks, parts in FLASH_FWD_TRIPS:
            trips = jnp.maximum(i - first, 0) // blocks
            step_n = make_step(False, blocks, parts)
            carry = lax.fori_loop(0, trips, lambda t, cr, f=first, b=blocks, s=step_n: s(f + b * t, cr), carry)
            first = first + blocks * trips
        m, l, acc = lax.fori_loop(jnp.maximum(i, 1), i + 1, make_step(True, 1), carry)
        o_ref[...] = acc / l
        lse_ref[...] = _as_rows(m + jnp.log(l) * LOG2E)

    in_specs = [pl.BlockSpec((tq, HEAD_W), lambda h, i: (i, h)),
                pl.BlockSpec((tp, HEAD_W), lambda h, i: (0, h)),
                pl.BlockSpec((tp, V_HEAD), lambda h, i: (0, h))]
    out_specs = [pl.BlockSpec((tq, V_HEAD), lambda h, i: (i, h)),
                 pl.BlockSpec((None, None, SUBLANES, tq), lambda h, i: (h, i, 0, 0))]
    out_shape = [jax.ShapeDtypeStruct((tp, heads * V_HEAD), F32),
                 jax.ShapeDtypeStruct((heads, nq, SUBLANES, tq), F32)]
    args, scratch = [q, k, v], []
    body = _with_exchange(exchange, len(in_specs), len(out_specs),
                          lambda: (pl.program_id(0) == 0) & (pl.program_id(1) == 0),
                          lambda: (pl.program_id(0) == heads - 1) & (pl.program_id(1) == nq - 1), compute)
    if exchange is not None:
        in_specs = in_specs + exchange.specs(exchange.n_in)
        out_specs = out_specs + exchange.specs(exchange.n_out)
        out_shape = out_shape + exchange.out_shapes
        args, scratch = args + exchange.arrays, exchange.scratch()
    return pl.pallas_call(
        body, name="a_flash_fwd", grid=(heads, nq),
        in_specs=in_specs, out_specs=out_specs, out_shape=out_shape, scratch_shapes=scratch,
        compiler_params=_params(("arbitrary", "arbitrary")),
    )(*args)


def _gated_out(name, a, gate, w, resid, tm):
    tp, wd = a.shape
    d = w.shape[1]

    def body(a_ref, gate_ref, w_ref, res_ref, o_ref):
        gv = gate_ref[...]
        y = (a_ref[...] * (gv * _sigmoid(gv))).astype(BF16)
        o_ref[...] = res_ref[...] + _dot(y, w_ref[...])

    return pl.pallas_call(
        body, name=name, grid=(tp // tm,),
        in_specs=[pl.BlockSpec((tm, wd), lambda i: (i, 0)),
                  pl.BlockSpec((tm, wd), lambda i: (i, 0)),
                  pl.BlockSpec((wd, d), lambda i: (0, 0)),
                  pl.BlockSpec((tm, d), lambda i: (i, 0))],
        out_specs=pl.BlockSpec((tm, d), lambda i: (i, 0)),
        out_shape=jax.ShapeDtypeStruct((tp, d), F32),
        compiler_params=_params(("parallel",)),
    )(a, gate, w, resid)


def _lru_decay(r, sp):
    log_a = -LRU_C * r * sp
    a = jnp.exp(log_a)
    e2 = a * a
    x2 = 2.0 * log_a
    series = x2 * (1.0 + x2 * (0.5 + x2 * (1.0 / 6.0)))
    em1 = jnp.where(x2 > -0.02, series, e2 - 1.0)
    return a, e2, jnp.sqrt(-em1)


def _softplus(x):
    return jnp.maximum(x, 0.0) + jnp.log1p(jnp.exp(-jnp.abs(x)))


def _rglru_fwd(u, conv_w, conv_b, w_rg, b_rg, w_ig, b_ig, lam, pad, tm):
    tp, w = u.shape
    groups, blk = w_rg.shape[0], w_rg.shape[1]

    def body(u_ref, cw_ref, cb_ref, wr_ref, br_ref, wi_ref, bi_ref, lam_ref,
             uc_ref, r_ref, ig_ref, hs_ref, a_s, mult_ref, uext, b_s, hc):
        i = pl.program_id(0)

        @pl.when(i == 0)
        def _():
            uext[0:SUBLANES, :] = jnp.zeros((SUBLANES, w), F32)
            hc[...] = jnp.zeros((SUBLANES, w), F32)

        uext[SUBLANES:SUBLANES + tm, :] = u_ref[...]
        cw = cw_ref[...]
        uc = cb_ref[...] + uext[pl.ds(SUBLANES - 3, tm), :] * cw[0:1, :]
        uc = uc + uext[pl.ds(SUBLANES - 2, tm), :] * cw[1:2, :]
        uc = uc + uext[pl.ds(SUBLANES - 1, tm), :] * cw[2:3, :]
        uc = uc + uext[pl.ds(SUBLANES, tm), :] * cw[3:4, :]
        uc_ref[...] = uc
        uext[0:SUBLANES, :] = uext[tm:tm + SUBLANES, :]

        sp = _softplus(-lam_ref[...])
        for g in range(groups):
            sl = slice(g * blk, (g + 1) * blk)
            ucg = uc_ref[:, sl]
            ucb = ucg.astype(BF16)
            r = _sigmoid(_dot(ucb, wr_ref[g]) + br_ref[:, sl])
            ig = _sigmoid(_dot(ucb, wi_ref[g]) + bi_ref[:, sl])
            r_ref[:, sl] = r
            ig_ref[:, sl] = ig
            a, _, mult = _lru_decay(r, sp[:, sl])
            a_s[:, sl] = a
            mult_ref[:, sl] = mult
            b_s[:, sl] = mult * (ig * ucg)

        @pl.when(i == 0)
        def _():
            row = lax.broadcasted_iota(jnp.int32, (Q_BLOCK, w), 0)
            start = ig_ref[0:Q_BLOCK, :] * uc_ref[0:Q_BLOCK, :]
            b_s[0:Q_BLOCK, :] = jnp.where(row < pad, 0.0, jnp.where(row == pad, start, b_s[0:Q_BLOCK, :]))
            mult_ref[0:Q_BLOCK, :] = jnp.where(row == pad, 1.0, mult_ref[0:Q_BLOCK, :])

        row8 = lax.broadcasted_iota(jnp.int32, (SUBLANES, w), 0)

        def group(gi, h_in):
            off = pl.multiple_of(gi * SUBLANES, SUBLANES)
            av = a_s[pl.ds(off, SUBLANES), :]
            bv = b_s[pl.ds(off, SUBLANES), :]
            for k in (1, 2, 4):
                keep = row8 >= k
                bv = jnp.where(keep, av * pltpu.roll(bv, k, 0) + bv, bv)
                av = jnp.where(keep, av * pltpu.roll(av, k, 0), av)
            hv = av * h_in + bv
            hs_ref[pl.ds(off, SUBLANES), :] = hv
            return jnp.broadcast_to(hv[SUBLANES - 1:SUBLANES, :], (SUBLANES, w))

        hc[...] = lax.fori_loop(0, tm // SUBLANES, group, hc[...])

    row_spec = pl.BlockSpec((tm, w), lambda i: (i, 0))
    vec_spec = pl.BlockSpec((1, w), lambda i: (0, 0))
    mat_spec = pl.BlockSpec((groups, blk, blk), lambda i: (0, 0, 0))
    return pl.pallas_call(
        body, name="b_rglru_fwd", grid=(tp // tm,),
        in_specs=[row_spec, pl.BlockSpec((CONV_WIDTH, w), lambda i: (0, 0)), vec_spec,
                  mat_spec, vec_spec, mat_spec, vec_spec, vec_spec],
        out_specs=[row_spec] * 6,
        out_shape=[jax.ShapeDtypeStruct((tp, w), F32)] * 6,
        scratch_shapes=[pltpu.VMEM((tm + SUBLANES, w), F32), pltpu.VMEM((tm, w), F32),
                        pltpu.VMEM((SUBLANES, w), F32)],
        compiler_params=_params(("arbitrary",)),
    )(u, conv_w, conv_b, w_rg, b_rg, w_ig, b_ig, lam)


def _final_loss(h, g, target, x0, tm):
    tp, d = h.shape
    assert x0 % Q_BLOCK == 0 and tm % Q_BLOCK == 0 and target.shape[0] == tp - x0
    lead = x0 // Q_BLOCK
    per = tm // Q_BLOCK

    def body(h_ref, g_ref, *rest):
        t_refs, (dh_ref, loss_ref, dg_ref) = rest[:per], rest[per:]
        i = pl.program_id(0)

        @pl.when(i == 0)
        def _():
            loss_ref[...] = jnp.zeros_like(loss_ref)
            dg_ref[...] = jnp.zeros_like(dg_ref)

        gv = g_ref[...]
        for b in range(per):
            rows = slice(b * Q_BLOCK, (b + 1) * Q_BLOCK)
            xv = h_ref[rows, :]
            r = _rms(xv)
            xh = xv * r
            err = jnp.where(i * per + b >= lead, xh * gv - t_refs[b][...], 0.0)
            loss_ref[...] += 0.5 * jnp.sum(jnp.mean(err * err, axis=-1, keepdims=True))
            dy = err / d
            dg_ref[...] += jnp.sum(dy * xh, axis=0, keepdims=True)
            dxh = dy * gv
            dh_ref[rows, :] = r * (dxh - xh * jnp.mean(dxh * xh, axis=-1, keepdims=True))

    def piece(b):
        return pl.BlockSpec((Q_BLOCK, d), lambda i: (jnp.maximum(i * per + b - lead, 0), 0))

    return pl.pallas_call(
        body, name="final_loss", grid=(tp // tm,),
        in_specs=[pl.BlockSpec((tm, d), lambda i: (i, 0)),
                  pl.BlockSpec((1, d), lambda i: (0, 0))] + [piece(b) for b in range(per)],
        out_specs=[pl.BlockSpec((tm, d), lambda i: (i, 0)),
                   pl.BlockSpec((SUBLANES, LANES), lambda i: (0, 0)),
                   pl.BlockSpec((1, d), lambda i: (0, 0))],
        out_shape=[jax.ShapeDtypeStruct((tp, d), F32),
                   jax.ShapeDtypeStruct((SUBLANES, LANES), F32),
                   jax.ShapeDtypeStruct((1, d), F32)],
        compiler_params=_params(("arbitrary",)),
    )(h, g, *([target] * per))


def _with_exchange(exchange, n_in, n_out, first, last, compute):
    if exchange is None:
        return compute
    ex_in, ex_out = exchange.n_in, exchange.n_out

    def body(*refs):
        own_in, their_in = refs[:n_in], refs[n_in:n_in + ex_in]
        pos = n_in + ex_in
        own_out, their_out = refs[pos:pos + n_out], refs[pos + n_out:pos + n_out + ex_out]
        rest = refs[pos + n_out + ex_out:]
        own_scratch, sems = rest[:len(rest) - 2], rest[len(rest) - 2:]

        @pl.when(first())
        def _():
            exchange.start(their_in, their_out, sems)

        compute(*own_in, *own_out, *own_scratch)

        @pl.when(last())
        def _():
            exchange.finish(their_in, their_out, sems)

    return body


def _gated_out_bwd(name, dout, a, gate, w, tm, delta_heads=0, exchange=None):
    tp, wd = a.shape
    d = w.shape[1]
    nt = tp // tm

    def compute(do_ref, a_ref, gate_ref, w_ref, da_ref, dgate_ref, dw_ref, *delta_ref):
        @pl.when(pl.program_id(0) == 0)
        def _():
            dw_ref[...] = jnp.zeros_like(dw_ref)

        gv = gate_ref[...]
        av = a_ref[...]
        sg = _sigmoid(gv)
        silu = gv * sg
        dob = do_ref[...].astype(BF16)
        dy = _dot_nt(dob, w_ref[...])
        da = dy * silu
        da_ref[...] = da
        dgate_ref[...] = dy * av * (sg * (1.0 + gv * (1.0 - sg)))
        dw_ref[...] += _dot_tn((av * silu).astype(BF16), dob)
        for h in range(delta_heads):
            sl = slice(h * V_HEAD, (h + 1) * V_HEAD)
            delta_ref[0][h] = _as_rows(jnp.sum(da[:, sl] * av[:, sl], axis=-1, keepdims=True))

    out_specs = [pl.BlockSpec((tm, wd), lambda i: (i, 0)),
                 pl.BlockSpec((tm, wd), lambda i: (i, 0)),
                 pl.BlockSpec((wd, d), lambda i: (0, 0))]
    out_shape = [jax.ShapeDtypeStruct((tp, wd), F32),
                 jax.ShapeDtypeStruct((tp, wd), F32),
                 jax.ShapeDtypeStruct((wd, d), F32)]
    if delta_heads:
        out_specs.append(pl.BlockSpec((delta_heads, None, SUBLANES, tm), lambda i: (0, i, 0, 0)))
        out_shape.append(jax.ShapeDtypeStruct((delta_heads, tp // tm, SUBLANES, tm), F32))
    in_specs = [pl.BlockSpec((tm, d), lambda i: (i, 0)),
                pl.BlockSpec((tm, wd), lambda i: (i, 0)),
                pl.BlockSpec((tm, wd), lambda i: (i, 0)),
                pl.BlockSpec((wd, d), lambda i: (0, 0))]
    args, scratch = [dout, a, gate, w], []
    body = _with_exchange(exchange, len(in_specs), len(out_specs),
                          lambda: pl.program_id(0) == 0, lambda: pl.program_id(0) == nt - 1, compute)
    if exchange is not None:
        in_specs = in_specs + exchange.specs(exchange.n_in)
        out_specs = out_specs + exchange.specs(exchange.n_out)
        out_shape = out_shape + exchange.out_shapes
        args, scratch = args + exchange.arrays, exchange.scratch()
    return pl.pallas_call(
        body, name=name, grid=(nt,),
        in_specs=in_specs, out_specs=out_specs, out_shape=out_shape, scratch_shapes=scratch,
        compiler_params=_params(("arbitrary",)),
    )(*args)


def _rglru_bwd(dhs, hs, r, ig, uc, u, a, mult, conv_w, w_rg, w_ig, lam, pad, tm):
    tp, w = u.shape
    groups, blk = w_rg.shape[0], w_rg.shape[1]
    nt = tp // tm
    per8 = tm // SUBLANES

    def body(dhs_ref, hs_ref, hprev_ref, r_ref, ig_ref, uc_ref, u_ref, uprev_ref, a_ref, mult_ref,
             cw_ref, wr_ref, wi_ref, lam_ref,
             du_ref, dcw_ref, dcb_ref, dwr_ref, dbr_ref, dwi_ref, dbi_ref, dlam_ref,
             aext, c_s, g_s, hext, uext, ducext, gc):
        step = pl.program_id(0)
        ti = nt - 1 - step

        @pl.when(step == 0)
        def _():
            for ref in (dcw_ref, dcb_ref, dwr_ref, dbr_ref, dwi_ref, dbi_ref, dlam_ref):
                ref[...] = jnp.zeros_like(ref)
            aext[tm:tm + SUBLANES, :] = jnp.zeros((SUBLANES, w), F32)
            ducext[tm:tm + SUBLANES, :] = jnp.zeros((SUBLANES, w), F32)
            gc[...] = jnp.zeros((SUBLANES, w), F32)

        lam_v = lam_ref[...]
        sp = _softplus(-lam_v)
        row = ti * tm + lax.broadcasted_iota(jnp.int32, (tm, w), 0)

        rv = r_ref[...]
        a = a_ref[...]
        mult = mult_ref[...]
        aext[0:tm, :] = a
        c_s[...] = aext[pl.ds(1, tm), :]
        row8 = lax.broadcasted_iota(jnp.int32, (SUBLANES, w), 0)

        def group(gi, g_in):
            off = pl.multiple_of((per8 - 1 - gi) * SUBLANES, SUBLANES)
            cv = c_s[pl.ds(off, SUBLANES), :]
            dv = dhs_ref[pl.ds(off, SUBLANES), :]
            for k in (1, 2, 4):
                keep = row8 < SUBLANES - k
                dv = jnp.where(keep, cv * pltpu.roll(dv, SUBLANES - k, 0) + dv, dv)
                cv = jnp.where(keep, cv * pltpu.roll(cv, SUBLANES - k, 0), cv)
            gv = cv * g_in + dv
            g_s[pl.ds(off, SUBLANES), :] = gv
            return jnp.broadcast_to(gv[0:1, :], (SUBLANES, w))

        gc[...] = lax.fori_loop(0, per8, group, gc[...])
        aext[tm:tm + SUBLANES, :] = aext[0:SUBLANES, :]

        gsc = jnp.where(row < pad, 0.0, g_s[...])
        hext[0:SUBLANES, :] = hprev_ref[...]
        hext[SUBLANES:SUBLANES + tm, :] = hs_ref[...]
        hprev = jnp.where(row == 0, 0.0, hext[pl.ds(SUBLANES - 1, tm), :])
        igv = ig_ref[...]
        ucv = uc_ref[...]
        first = row == pad
        dmult = gsc * (igv * ucv)
        dig = gsc * mult * ucv
        duc = gsc * mult * igv
        dlog_a = (gsc * hprev) * a + jnp.where(first, 0.0, dmult * (-(a * a) / mult))
        dlam_ref[...] += jnp.sum(dlog_a * rv, axis=0, keepdims=True) * (LRU_C * _sigmoid(-lam_v))
        dpre_r = dlog_a * (-LRU_C * sp) * (rv * (1.0 - rv))
        dpre_i = dig * (igv * (1.0 - igv))
        dbr_ref[...] += jnp.sum(dpre_r, axis=0, keepdims=True)
        dbi_ref[...] += jnp.sum(dpre_i, axis=0, keepdims=True)
        for g in range(groups):
            sl = slice(g * blk, (g + 1) * blk)
            ucb = ucv[:, sl].astype(BF16)
            drb = dpre_r[:, sl].astype(BF16)
            dib = dpre_i[:, sl].astype(BF16)
            dwr_ref[g] += _dot_tn(ucb, drb)
            dwi_ref[g] += _dot_tn(ucb, dib)
            ducext[0:tm, sl] = duc[:, sl] + _dot_nt(drb, wr_ref[g]) + _dot_nt(dib, wi_ref[g])

        ducv = ducext[0:tm, :]
        cw = cw_ref[...]
        dcb_ref[...] += jnp.sum(ducv, axis=0, keepdims=True)
        uext[0:SUBLANES, :] = jnp.where(ti == 0, 0.0, uprev_ref[...])
        uext[SUBLANES:SUBLANES + tm, :] = u_ref[...]
        for j in range(CONV_WIDTH):
            ush = uext[pl.ds(SUBLANES - (CONV_WIDTH - 1 - j), tm), :]
            dcw_ref[j:j + 1, :] += jnp.sum(ducv * ush, axis=0, keepdims=True)
        du = ducv * cw[3:4, :]
        for k in range(1, CONV_WIDTH):
            du = du + ducext[pl.ds(k, tm), :] * cw[3 - k:4 - k, :]
        du_ref[...] = du
        ducext[tm:tm + SUBLANES, :] = ducext[0:SUBLANES, :]

    rev = lambda s: (nt - 1 - s, 0)
    halo = lambda s: (jnp.maximum((nt - 1 - s) * per8 - 1, 0), 0)
    row_spec = pl.BlockSpec((tm, w), rev)
    halo_spec = pl.BlockSpec((SUBLANES, w), halo)
    vec_spec = pl.BlockSpec((1, w), lambda s: (0, 0))
    mat_spec = pl.BlockSpec((groups, blk, blk), lambda s: (0, 0, 0))
    cw_spec = pl.BlockSpec((CONV_WIDTH, w), lambda s: (0, 0))
    return pl.pallas_call(
        body, name="b_rglru_bwd", grid=(nt,),
        in_specs=[row_spec, row_spec, halo_spec, row_spec, row_spec, row_spec, row_spec, halo_spec, row_spec, row_spec,
                  cw_spec, mat_spec, mat_spec, vec_spec],
        out_specs=[row_spec, cw_spec, vec_spec, mat_spec, vec_spec, mat_spec, vec_spec, vec_spec],
        out_shape=[jax.ShapeDtypeStruct((tp, w), F32), jax.ShapeDtypeStruct((CONV_WIDTH, w), F32),
                   jax.ShapeDtypeStruct((1, w), F32), jax.ShapeDtypeStruct((groups, blk, blk), F32),
                   jax.ShapeDtypeStruct((1, w), F32), jax.ShapeDtypeStruct((groups, blk, blk), F32),
                   jax.ShapeDtypeStruct((1, w), F32), jax.ShapeDtypeStruct((1, w), F32)],
        scratch_shapes=[pltpu.VMEM((tm + SUBLANES, w), F32), pltpu.VMEM((tm, w), F32), pltpu.VMEM((tm, w), F32),
                        pltpu.VMEM((tm + SUBLANES, w), F32), pltpu.VMEM((tm + SUBLANES, w), F32),
                        pltpu.VMEM((tm + SUBLANES, w), F32), pltpu.VMEM((SUBLANES, w), F32)],
        compiler_params=_params(("arbitrary",)),
    )(dhs, hs, hs, r, ig, uc, u, u, a, mult, conv_w, w_rg, w_ig, lam)


def _norm_matmul_bwd(name, x, g, w, dys, tm, resid=None, prologue=None, extra_out=None, exchange=None, split=None):
    tp, kin = x.shape
    n = w.shape[1]
    nt = tp // tm
    n_dy = len(dys)
    has_res = resid is not None
    has_extra = extra_out is not None
    n_dx = 1 if split is None else 2
    if split is not None:
        pad, x0 = split
        assert pad % SUBLANES == 0 and x0 % SUBLANES == 0 and pad < x0 <= tm and nt >= 2

    def compute(*refs):
        x_ref, g_ref, w_ref = refs[:3]
        dy_refs = refs[3:3 + n_dy]
        pos = 3 + n_dy
        res_ref = refs[pos] if has_res else None
        pos += int(has_res)
        dx_refs = refs[pos:pos + n_dx]
        dw_ref, dg_ref = refs[pos + n_dx:pos + n_dx + 2]
        pos += n_dx + 2
        ex_ref = refs[pos] if has_extra else None
        pos += int(has_extra)
        dy_s = refs[pos]

        @pl.when(pl.program_id(0) == 0)
        def _():
            dw_ref[...] = jnp.zeros_like(dw_ref)
            dg_ref[...] = jnp.zeros_like(dg_ref)

        if prologue is None:
            c0 = 0
            for ref in dy_refs:
                dy_s[:, c0:c0 + ref.shape[1]] = ref[...].astype(BF16)
                c0 += ref.shape[1]
        else:
            prologue(dy_refs, dy_s, ex_ref)

        xv = x_ref[...]
        gv = g_ref[...]
        r = _rms(xv)
        xh = xv * r
        dyb = dy_s[...]
        dn = _dot_nt(dyb, w_ref[...])
        dw_ref[...] += _dot_tn((xh * gv).astype(BF16), dyb)
        dg_ref[...] += jnp.sum(dn * xh, axis=0, keepdims=True)
        dxh = dn * gv
        dx = r * (dxh - xh * jnp.mean(dxh * xh, axis=-1, keepdims=True))
        if has_res:
            dx = dx + res_ref[...]
        if split is None:
            dx_refs[0][...] = dx
            return

        head_ref, tail_ref = dx_refs
        tile_s, sems = refs[pos + 1:pos + 3]
        i = pl.program_id(0)

        def tail_copy(t):
            return pltpu.make_async_copy(tile_s.at[t % 2], tail_ref.at[pl.ds(pl.multiple_of(t * tm - x0, SUBLANES), tm), :],
                                         sems.at[t % 2])

        def first_copy():
            return pltpu.make_async_copy(tile_s.at[0, pl.ds(x0, tm - x0), :], tail_ref.at[pl.ds(0, tm - x0), :], sems.at[0])

        def wait_tile(t):
            if x0 < tm:
                pl.when(t == 0)(lambda: first_copy().wait())
            pl.when(t > 0)(lambda: tail_copy(t).wait())

        pl.when(i >= 2)(lambda: wait_tile(i - 2))
        tile_s[i % 2] = dx

        @pl.when(i == 0)
        def _():
            head = pltpu.make_async_copy(tile_s.at[0, pl.ds(pad, x0 - pad), :], head_ref, sems.at[2])
            head.start()
            if x0 < tm:
                first_copy().start()
            head.wait()

        pl.when(i > 0)(lambda: tail_copy(i).start())

        @pl.when(i == nt - 1)
        def _():
            wait_tile(i - 1)
            tail_copy(i).wait()

    row = lambda width: pl.BlockSpec((tm, width), lambda i: (i, 0))
    in_specs = [row(kin), pl.BlockSpec((1, kin), lambda i: (0, 0)), pl.BlockSpec((kin, n), lambda i: (0, 0))]
    in_specs += [row(a.shape[1]) for a in dys]
    args = [x, g, w, *dys]
    if has_res:
        in_specs.append(row(kin))
        args.append(resid)
    if split is None:
        out_specs = [row(kin)]
        out_shape = [jax.ShapeDtypeStruct((tp, kin), F32)]
    else:
        out_specs = [pl.BlockSpec(memory_space=pl.ANY)] * 2
        out_shape = [jax.ShapeDtypeStruct((x0 - pad, kin), F32), jax.ShapeDtypeStruct((tp - x0, kin), F32)]
    out_specs += [pl.BlockSpec((kin, n), lambda i: (0, 0)), pl.BlockSpec((1, kin), lambda i: (0, 0))]
    out_shape += [jax.ShapeDtypeStruct((kin, n), F32), jax.ShapeDtypeStruct((1, kin), F32)]
    if has_extra:
        out_specs.append(row(extra_out[0]))
        out_shape.append(jax.ShapeDtypeStruct((tp, extra_out[0]), extra_out[1]))
    scratch = [pltpu.VMEM((tm, n), BF16)]
    if split is not None:
        scratch += [pltpu.VMEM((2, tm, kin), F32), pltpu.SemaphoreType.DMA((3,))]
    body = _with_exchange(exchange, len(in_specs), len(out_specs),
                          lambda: pl.program_id(0) == 0, lambda: pl.program_id(0) == nt - 1, compute)
    if exchange is not None:
        in_specs = in_specs + exchange.specs(exchange.n_in)
        out_specs = out_specs + exchange.specs(exchange.n_out)
        out_shape = out_shape + exchange.out_shapes
        args, scratch = args + exchange.arrays, scratch + exchange.scratch()
    return pl.pallas_call(
        body, name=name, grid=(nt,),
        in_specs=in_specs, out_specs=out_specs, out_shape=out_shape, scratch_shapes=scratch,
        compiler_params=_params(("arbitrary",)),
    )(*args)


def _flash_bwd(q, k, v, lse, delta, do, heads, pad, tq, exchange=None):
    tp = q.shape[0]
    nq = tp // tq
    scale = (QK_NOPE + QK_ROPE) ** -0.5
    c2 = scale * LOG2E

    def compute(q_ref, k_ref, v_ref, lse_ref, delta_ref, do_ref, dq_ref, dk_ref, dv_ref):
        j = pl.program_id(1)

        @pl.when(j == 0)
        def _():
            dq_ref[...] = jnp.zeros_like(dq_ref)

        kv = k_ref[...]
        vv = v_ref[...]

        def rows_of(ref, i, blocks):
            parts = [ref[i + b][0:1, :] for b in range(blocks)]
            return parts[0] if blocks == 1 else jnp.concatenate(parts, axis=1)

        def make_step(masked, blocks):
            def step(i, carry):
                dk, dv = carry
                off = pl.multiple_of(i * tq, tq)
                qv = q_ref[pl.ds(off, blocks * tq), :]
                dob = do_ref[pl.ds(off, blocks * tq), :].astype(BF16)
                p = jnp.exp2(_dot_nt(kv, qv) * c2 - rows_of(lse_ref, i, blocks))
                if masked:
                    key = j * tq + lax.broadcasted_iota(jnp.int32, (tq, tq), 0)
                    qry = j * tq + lax.broadcasted_iota(jnp.int32, (tq, tq), 1)
                    first = jnp.where((key <= qry) & (key >= pad), p[:, :tq], 0.0)
                    p = first if blocks == 1 else jnp.concatenate([first, p[:, tq:]], axis=1)
                dv = dv + _dot(p.astype(BF16), dob)
                dp = _dot_nt(vv, dob)
                ds = (p * (dp - rows_of(delta_ref, i, blocks)) * scale).astype(BF16)
                dk = dk + _dot(ds, qv)
                dq_ref[pl.ds(off, blocks * tq), :] += _dot_tn(ds, kv)
                return dk, dv
            return step

        odd = (nq - j) % 2
        carry = (jnp.zeros((tq, HEAD_W), F32), jnp.zeros((tq, V_HEAD), F32))
        carry = lax.fori_loop(0, odd, lambda t, cr: make_step(True, 1)(j, cr), carry)
        carry = lax.fori_loop(0, 1 - odd, lambda t, cr: make_step(True, 2)(j, cr), carry)
        start = j + 2 - odd
        for blocks in (4, 2):
            trips = (nq - start) // blocks
            step_n = make_step(False, blocks)
            carry = lax.fori_loop(0, trips, lambda t, cr, s=start, b=blocks, f=step_n: f(s + b * t, cr), carry)
            start = start + blocks * trips
        dk, dv = carry
        dk_ref[...] = dk
        dv_ref[...] = dv

    in_specs = [pl.BlockSpec((tp, HEAD_W), lambda h, j: (0, h)),
                pl.BlockSpec((tq, HEAD_W), lambda h, j: (j, h)),
                pl.BlockSpec((tq, V_HEAD), lambda h, j: (j, h)),
                pl.BlockSpec((None, nq, SUBLANES, tq), lambda h, j: (h, 0, 0, 0)),
                pl.BlockSpec((None, nq, SUBLANES, tq), lambda h, j: (h, 0, 0, 0)),
                pl.BlockSpec((tp, V_HEAD), lambda h, j: (0, h))]
    out_specs = [pl.BlockSpec((tp, HEAD_W), lambda h, j: (0, h)),
                 pl.BlockSpec((tq, HEAD_W), lambda h, j: (j, h)),
                 pl.BlockSpec((tq, V_HEAD), lambda h, j: (j, h))]
    out_shape = [jax.ShapeDtypeStruct((tp, heads * HEAD_W), F32),
                 jax.ShapeDtypeStruct((tp, heads * HEAD_W), F32),
                 jax.ShapeDtypeStruct((tp, heads * V_HEAD), F32)]
    args, scratch = [q, k, v, lse, delta, do], []
    body = _with_exchange(exchange, len(in_specs), len(out_specs),
                          lambda: (pl.program_id(0) == 0) & (pl.program_id(1) == 0),
                          lambda: (pl.program_id(0) == heads - 1) & (pl.program_id(1) == nq - 1), compute)
    if exchange is not None:
        in_specs = in_specs + exchange.specs(exchange.n_in)
        out_specs = out_specs + exchange.specs(exchange.n_out)
        out_shape = out_shape + exchange.out_shapes
        args, scratch = args + exchange.arrays, exchange.scratch()
    return pl.pallas_call(
        body, name="a_flash_bwd", grid=(heads, nq),
        in_specs=in_specs, out_specs=out_specs, out_shape=out_shape, scratch_shapes=scratch,
        compiler_params=_params(("arbitrary", "arbitrary")),
    )(*args)


def _position():
    return lax.axis_index("x"), lax.axis_index("y"), lax.axis_index("c")


def _other_chips(x, y):
    return [(1 - x, y), (x, 1 - y), (1 - x, 1 - y)]


def _block(ref, shard_axis, n, k, split_axis=None, m=None, h=None):
    idx = []
    for a in range(len(ref.shape)):
        start = 0
        size = None
        if a == shard_axis:
            start, size = k * n, n
        if a == split_axis:
            size = (n if a == shard_axis else m) // 2
            start = start + h * size
        idx.append(slice(None) if size is None else pl.ds(start, size))
    return ref.at[tuple(idx)]


def _gather_weights(split, whole_small):
    ns, nw = len(split), len(whole_small)
    n = ns + nw
    arrs = [s[0] for s in split] + [s[0] for s in whole_small]
    axes = [s[1] for s in split] + [s[1] for s in whole_small]

    def body(*refs):
        ins, outs = refs[:n], refs[n:2 * n]
        ici_send, ici_recv, d2d_send, d2d_recv, sib_send, sib_recv = refs[2 * n:]
        x, y, c = _position()
        me = 2 * x + y
        others = _other_chips(x, y)
        sent, local = [], []

        def remote(src, dst, sems, idx, to):
            return pltpu.make_async_remote_copy(src_ref=src, dst_ref=dst, send_sem=sems[0].at[idx],
                                                recv_sem=sems[1].at[idx], device_id=to, device_id_type=MESH)

        for a in range(n):
            width = ins[a].shape[axes[a]]
            mine = remote(ins[a], _block(outs[a], axes[a], width, me), (sib_send, sib_recv), a, (x, y, 1 - c))
            mine.start()
            local.append(mine)
            for j, (px, py) in enumerate(others):
                if a < ns:
                    sx = split[a][2]
                    src = _block(ins[a], None, None, None, sx, ins[a].shape[sx], c)
                    dst = _block(outs[a], axes[a], width, me, sx, outs[a].shape[sx], c)
                else:
                    src, dst = ins[a], _block(outs[a], axes[a], width, me)
                cp = remote(src, dst, (ici_send, ici_recv), 3 * a + j, (px, py, c))
                cp.start()
                sent.append(cp)
        for a in range(ns):
            width = ins[a].shape[axes[a]]
            sx = split[a][2]
            for j, (px, py) in enumerate(others):
                theirs = _block(outs[a], axes[a], width, 2 * px + py, sx, outs[a].shape[sx], c)
                remote(theirs, theirs, (ici_send, ici_recv), 3 * a + j, (px, py, c)).wait_recv()
                fwd = remote(theirs, theirs, (d2d_send, d2d_recv), 3 * a + j, (x, y, 1 - c))
                fwd.start()
                sent.append(fwd)
        for a in range(ns, n):
            width = ins[a].shape[axes[a]]
            for j, (px, py) in enumerate(others):
                theirs = _block(outs[a], axes[a], width, 2 * px + py)
                remote(theirs, theirs, (ici_send, ici_recv), 3 * a + j, (px, py, c)).wait_recv()
        for a in range(ns):
            width = ins[a].shape[axes[a]]
            sx = split[a][2]
            for j, (px, py) in enumerate(others):
                from_sibling = _block(outs[a], axes[a], width, 2 * px + py, sx, outs[a].shape[sx], 1 - c)
                remote(from_sibling, from_sibling, (d2d_send, d2d_recv), 3 * a + j, (x, y, 1 - c)).wait_recv()
        for cp in sent:
            cp.wait_send()
        for cp in local:
            cp.wait()

    def whole_shape(a, axis):
        shape = list(a.shape)
        shape[axis] *= N_CHIPS
        return jax.ShapeDtypeStruct(tuple(shape), a.dtype)

    any_spec = pl.BlockSpec(memory_space=pl.ANY)
    return pl.pallas_call(
        body, name="gather_weights",
        in_specs=[any_spec] * n, out_specs=[any_spec] * n,
        out_shape=[whole_shape(a, ax) for a, ax in zip(arrs, axes)],
        scratch_shapes=[pltpu.SemaphoreType.DMA((3 * n,)), pltpu.SemaphoreType.DMA((3 * n,)),
                        pltpu.SemaphoreType.DMA((3 * ns,)), pltpu.SemaphoreType.DMA((3 * ns,)),
                        pltpu.SemaphoreType.DMA((n,)), pltpu.SemaphoreType.DMA((n,))],
        compiler_params=pltpu.CompilerParams(has_side_effects=True),
    )(*arrs)


class _Grad:
    def __init__(self, name, g, kind, rh, cols, groups=None):
        self.name, self.g, self.kind, self.rh, self.cols, self.groups = name, g, kind, rh, cols, groups
        if kind == 'rows':
            self.tr = rh
        elif kind == 'gate':
            self.tr = rh // (groups // 2)
        else:
            self.tr = rh if rh <= 512 else 256
        self.nb = rh // self.tr

    def pieces(self, ref, k, h):
        rh, cols = self.rh, self.cols
        if self.kind == 'cols':
            return [(ref.at[pl.ds(h * rh, rh), pl.ds(k * cols, cols)], 0, rh)]
        if self.kind == 'rows':
            return [(ref.at[pl.ds((2 * k + h) * rh, rh), :], 0, rh)]
        if self.kind == 'lead':
            return [(ref.at[k, pl.ds(h * rh, rh), :], 0, rh)]
        per = self.groups // 2
        return [(ref.at[pl.ds((((h * per + gi) * N_CHIPS) + k) * self.tr, self.tr), :], gi * self.tr, self.tr)
                for gi in range(per)]

    def block_spec(self):
        tr, nb, cols = self.tr, self.nb, self.cols
        if self.kind == 'cols':
            return pl.BlockSpec((tr, cols), lambda k, i, c: (c[0] * nb + i, k))
        if self.kind == 'rows':
            return pl.BlockSpec((tr, cols), lambda k, i, c: (2 * k + c[0], 0))
        if self.kind == 'lead':
            return pl.BlockSpec((None, tr, cols), lambda k, i, c: (k, c[0] * nb + i, 0))
        return pl.BlockSpec((tr, cols), lambda k, i, c: ((c[0] * nb + i) * N_CHIPS + k, 0))


class _Exchange:
    def __init__(self, name, arrays, out_shapes, n_copies, copies):
        self.name, self.arrays, self.out_shapes, self.n_copies, self.copies = name, arrays, out_shapes, n_copies, copies
        self.n_in, self.n_out = len(arrays), len(out_shapes)

    def specs(self, n):
        return [pl.BlockSpec(memory_space=pl.ANY)] * n

    def scratch(self):
        return [pltpu.SemaphoreType.DMA((self.n_copies,)), pltpu.SemaphoreType.DMA((self.n_copies,))]

    def _descriptors(self, in_refs, out_refs, sems):
        return self.copies(in_refs, out_refs, sems[0], sems[1])

    def start(self, in_refs, out_refs, sems):
        for cp in self._descriptors(in_refs, out_refs, sems):
            cp.start()

    def finish(self, in_refs, out_refs, sems):
        for cp in self._descriptors(in_refs, out_refs, sems):
            cp.wait()

    def __add__(self, other):
        def copies(ins, outs, send_sems, recv_sems, base=0):
            return (self.copies(ins[:self.n_in], outs[:self.n_out], send_sems, recv_sems, base)
                    + other.copies(ins[self.n_in:], outs[self.n_out:], send_sems, recv_sems, base + self.n_copies))

        return _Exchange(self.name + "_" + other.name, self.arrays + other.arrays, self.out_shapes + other.out_shapes,
                         self.n_copies + other.n_copies, copies)

    def run(self):
        def body(*refs):
            ins, outs, sems = refs[:self.n_in], refs[self.n_in:self.n_in + self.n_out], refs[self.n_in + self.n_out:]
            self.start(ins, outs, sems)
            self.finish(ins, outs, sems)

        return pl.pallas_call(
            body, name=self.name,
            in_specs=self.specs(self.n_in), out_specs=self.specs(self.n_out), out_shape=self.out_shapes,
            scratch_shapes=self.scratch(),
            compiler_params=pltpu.CompilerParams(has_side_effects=True),
        )(*self.arrays)


def _gather_whole(name, shards):
    def copies(ins, outs, send_sems, recv_sems, base=0):
        x, y, c = _position()
        me = 2 * x + y
        made = []
        for a, (_, axis) in enumerate(shards):
            dst = _block(outs[a], axis, ins[a].shape[axis], me)
            for j, to in enumerate([(x, y, 1 - c)] + [(px, py, c) for px, py in _other_chips(x, y)]):
                idx = base + 4 * a + j
                made.append(pltpu.make_async_remote_copy(
                    src_ref=ins[a], dst_ref=dst, send_sem=send_sems.at[idx], recv_sem=recv_sems.at[idx],
                    device_id=to, device_id_type=MESH))
        return made

    def whole_shape(a, axis):
        shape = list(a.shape)
        shape[axis] *= N_CHIPS
        return jax.ShapeDtypeStruct(tuple(shape), a.dtype)

    return _Exchange(name, [s[0] for s in shards], [whole_shape(*s) for s in shards], 4 * len(shards), copies)


def _halves_to_sibling(name, grads):
    total = sum(len(gr.pieces(gr.g, 0, 0)) * N_CHIPS for gr in grads)

    def copies(ins, outs, send_sems, recv_sems, base=0):
        x, y, c = _position()
        made = []
        for gr, g_ref, got_ref in zip(grads, ins, outs):
            for k in range(N_CHIPS):
                for src, r0, nr in gr.pieces(g_ref, k, 1 - c):
                    idx = base + len(made)
                    made.append(pltpu.make_async_remote_copy(
                        src_ref=src, dst_ref=got_ref.at[k, pl.ds(r0, nr), :],
                        send_sem=send_sems.at[idx], recv_sem=recv_sems.at[idx],
                        device_id=(x, y, 1 - c), device_id_type=MESH))
        return made

    return _Exchange(name, [gr.g for gr in grads],
                     [jax.ShapeDtypeStruct((N_CHIPS, gr.rh, gr.cols), F32) for gr in grads], total, copies)


def _chip_sum(gr, got, c, wire=BF16):
    def body(c_ref, g_ref, got_ref, o_ref):
        o_ref[...] = (g_ref[...] + got_ref[...]).astype(wire)

    tile = pl.BlockSpec((None, gr.tr, gr.cols), lambda k, i, c_ref: (k, i, 0))
    return pl.pallas_call(
        body, name="chip_sum_" + gr.name,
        grid_spec=pltpu.PrefetchScalarGridSpec(
            num_scalar_prefetch=1, grid=(N_CHIPS, gr.nb),
            in_specs=[gr.block_spec(), tile], out_specs=tile),
        out_shape=jax.ShapeDtypeStruct((N_CHIPS, gr.rh, gr.cols), wire),
        compiler_params=_params(("parallel", "parallel")),
    )(c, gr.g, got)


def _blocks_to_chips(name, parts):
    n = len(parts)

    def copies(ins, outs, send_sems, recv_sems, base=0):
        x, y, c = _position()
        made = []
        for a in range(n):
            for j, (px, py) in enumerate(_other_chips(x, y)):
                idx = base + 3 * a + j
                made.append(pltpu.make_async_remote_copy(
                    src_ref=ins[a].at[2 * px + py], dst_ref=outs[a].at[j],
                    send_sem=send_sems.at[idx], recv_sem=recv_sems.at[idx],
                    device_id=(px, py, c), device_id_type=MESH))
        return made

    return _Exchange(name, parts, [jax.ShapeDtypeStruct((3,) + p.shape[1:], p.dtype) for p in parts], 3 * n, copies)


def _sum_chips(name, part, got, me, c):
    nk, rh, cols = part.shape
    tr = rh if rh <= 512 else 256
    nb = rh // tr

    def body(me_ref, c_ref, own_ref, *rest):
        got_refs, o_ref = rest[:nk], rest[nk]
        own = own_ref[...].astype(F32)
        acc = None
        for k in range(nk):
            term = jnp.where(me_ref[0] == k, own, got_refs[k][...].astype(F32))
            acc = term if acc is None else acc + term
        o_ref[...] = acc

    def got_map(k):
        def index(i, me_ref, c_ref):
            xor = jnp.bitwise_xor(me_ref[0], k)
            slot = jnp.where(xor == 1, 1, jnp.where(xor == 3, 2, 0))
            return (slot, i, 0)
        return index

    return pl.pallas_call(
        body, name="sum_" + name,
        grid_spec=pltpu.PrefetchScalarGridSpec(
            num_scalar_prefetch=2, grid=(nb,),
            in_specs=[pl.BlockSpec((None, tr, cols), lambda i, me_ref, c_ref: (me_ref[0], i, 0))]
            + [pl.BlockSpec((None, tr, cols), got_map(k)) for k in range(nk)],
            out_specs=pl.BlockSpec((tr, cols), lambda i, me_ref, c_ref: (c_ref[0] * nb + i, 0))),
        out_shape=jax.ShapeDtypeStruct((2 * rh, cols), F32),
        compiler_params=_params(("parallel",)),
    )(me, c, part, *([got] * nk))


def _share_with_sibling(halves):
    n = len(halves)

    def body(*refs):
        outs = refs[n:2 * n]
        send_sems, recv_sems = refs[2 * n:]
        x, y, c = _position()
        copies = []
        for a in range(n):
            rh = outs[a].shape[0] // 2
            mine = outs[a].at[pl.ds(c * rh, rh), :]
            cp = pltpu.make_async_remote_copy(
                src_ref=mine, dst_ref=mine, send_sem=send_sems.at[a], recv_sem=recv_sems.at[a],
                device_id=(x, y, 1 - c), device_id_type=MESH)
            cp.start()
            copies.append(cp)
        for cp in copies:
            cp.wait()

    any_spec = pl.BlockSpec(memory_space=pl.ANY)
    return pl.pallas_call(
        body, name="grads_share",
        in_specs=[any_spec] * n, out_specs=[any_spec] * n,
        out_shape=[jax.ShapeDtypeStruct(h.shape, h.dtype) for h in halves],
        input_output_aliases={a: a for a in range(n)},
        scratch_shapes=[pltpu.SemaphoreType.DMA((n,)), pltpu.SemaphoreType.DMA((n,))],
        compiler_params=pltpu.CompilerParams(has_side_effects=True),
    )(*halves)


def _adamw(name, w, g, m, v):
    rows, cols = w.shape
    tr = 256 if rows % 256 == 0 else rows

    def body(w_ref, g_ref, m_ref, v_ref, d_ref, nm_ref, nv_ref):
        gv = g_ref[...]
        mn = ADAM_B1 * m_ref[...] + (1.0 - ADAM_B1) * gv
        vn = ADAM_B2 * v_ref[...] + (1.0 - ADAM_B2) * (gv * gv)
        m_hat = mn / (1.0 - ADAM_B1 ** ADAM_STEP)
        v_hat = vn / (1.0 - ADAM_B2 ** ADAM_STEP)
        d_ref[...] = -ADAM_LR * (m_hat / (jnp.sqrt(v_hat) + ADAM_EPS) + ADAM_WD * w_ref[...])
        nm_ref[...] = mn
        nv_ref[...] = vn

    spec = pl.BlockSpec((tr, cols), lambda i: (i, 0))
    return pl.pallas_call(
        body, name=name, grid=(rows // tr,),
        in_specs=[spec] * 4, out_specs=[spec] * 3,
        out_shape=[jax.ShapeDtypeStruct((rows, cols), F32)] * 3,
        compiler_params=_params(("parallel",)),
    )(w, g, m, v)


def _as2d(a):
    if a.ndim == 1:
        return a.reshape(1, -1)
    return a.reshape(-1, a.shape[-1])


def _unshard(gathered, axis):
    moved = jnp.moveaxis(gathered, 0, axis)
    shape = list(gathered.shape[1:])
    shape[axis] *= N_CHIPS
    return moved.reshape(shape)


def _rope_tables(tp, pad):
    pos = jnp.arange(tp, dtype=F32) - pad
    inv_freq = ROPE_BASE ** (-jnp.arange(0, QK_ROPE, 2, dtype=F32) / QK_ROPE)
    ang = pos[:, None] * inv_freq[None, :]
    cos, sin = jnp.cos(ang), jnp.sin(ang)
    zeros = jnp.zeros((tp, LANES - QK_ROPE), F32)
    return jnp.concatenate([cos, cos, zeros], axis=1), jnp.concatenate([-sin, sin, zeros], axis=1)


def _matrix_grad(name, g, heads):
    if name in ('b_w_rg', 'b_w_ig'):
        groups, blk, cols = g.shape
        return _Grad(name, g.reshape(groups * blk, cols), 'gate', (groups // 2) * (blk // N_CHIPS), cols, groups)
    rows, cols = g.shape
    if name in ('a_w_out', 'b_w_out'):
        return _Grad(name, g, 'rows', rows // (2 * N_CHIPS), cols)
    if name == 'a_w_uq' and heads % N_CHIPS != 0:
        g = g.reshape(rows, heads, HEAD_W)[:, :, :QK_NOPE + QK_ROPE].reshape(rows, -1)
        cols = g.shape[1]
    if name == 'a_w_in' or (name == 'a_w_uq' and heads % N_CHIPS != 0):
        g = jnp.moveaxis(g.reshape(rows, N_CHIPS, cols // N_CHIPS), 1, 0)
        return _Grad(name, g, 'lead', rows // 2, cols // N_CHIPS)
    return _Grad(name, g, 'cols', rows // 2, cols // N_CHIPS)


def _kernel_form(name, w):
    return w[0] if name in ('b_w_rg', 'b_w_ig', 'b_conv_w') else _as2d(w)


def _local_grads(x, target, wt, heads, c_idx, mid_names, mid_gather, late_names, late_gather):
    wt = dict(wt)
    seq, d = x.shape
    n_meta = wt['meta_tokens'].shape[0]
    t = seq + n_meta
    pad = (-t) % Q_BLOCK
    tp = t + pad
    x0 = pad + n_meta
    tm = _row_tile(tp)
    ql = wt['a_q_norm_g'].shape[1]
    kvl = wt['a_kv_norm_g'].shape[1]
    mla_w = heads * V_HEAD

    h0 = jnp.concatenate([jnp.zeros((pad, d), F32), wt['meta_tokens'], x], axis=0)
    cos_t, sin_t = _rope_tables(tp, pad)

    w_in_a = wt['a_w_in']
    zcol = jnp.zeros((d, LANES - QK_ROPE), BF16)
    w_in_a = jnp.concatenate([w_in_a[:, :ql + kvl + QK_ROPE], zcol, w_in_a[:, ql + kvl + QK_ROPE:]], axis=1)
    c_kv, c_kr, c_gate = ql, ql + kvl, ql + kvl + LANES
    splits_a = [(0, c_kv), (c_kv, c_kr), (c_kr, c_gate), (c_gate, c_gate + mla_w)]

    q_lat, kv_lat, kr_raw, gate_a, *mid_whole = _norm_matmul("a_in_proj", h0, wt['a_norm_g'], w_in_a, splits_a, tm,
                                                             exchange=mid_gather)
    wt.update({n: _kernel_form(n, w) for n, w in zip(mid_names, mid_whole)})
    w_uq = wt['a_w_uq'].reshape(ql, heads, QK_NOPE + QK_ROPE)
    w_uq = jnp.pad(w_uq, ((0, 0), (0, 0), (0, HEAD_W - QK_NOPE - QK_ROPE))).reshape(ql, heads * HEAD_W)
    w_ukv = wt['a_w_ukv']
    q = _q_proj(q_lat, wt['a_q_norm_g'], w_uq, cos_t, sin_t, heads, tm)
    k, v = _kv_proj(kv_lat, wt['a_kv_norm_g'], w_ukv, kr_raw, cos_t, sin_t, heads, tm)
    attn, lse, *late_whole = _flash_fwd(q, k, v, heads, pad, tm, exchange=late_gather)
    wt.update({n: _kernel_form(n, w) for n, w in zip(late_names, late_whole)})
    lru_w = wt['b_conv_w'].shape[1]
    h1 = _gated_out("a_out_proj", attn, gate_a, wt['a_w_out'], h0, tm)

    u, gate_b = _norm_matmul("b_in_proj", h1, wt['b_norm_g'], wt['b_w_in'], [(0, lru_w), (lru_w, 2 * lru_w)], tm)
    uc, r, ig, hs, decay, mult = _rglru_fwd(u, wt['b_conv_w'], wt['b_conv_b'], wt['b_w_rg'], wt['b_b_rg'],
                                            wt['b_w_ig'], wt['b_b_ig'], wt['b_lam'], pad, tm)
    h2 = _gated_out("b_out_proj", hs, gate_b, wt['b_w_out'], h1, tm)

    dh2, loss, d_final_g = _final_loss(h2, wt['final_norm_g'], target, x0, tm)

    dhs, dgate_b, dw_out_b = _gated_out_bwd("b_out_proj_bwd", dh2, hs, gate_b, wt['b_w_out'], tm)
    du, dconv_w, dconv_b, dw_rg, db_rg, dw_ig, db_ig, dlam = _rglru_bwd(
        dhs, hs, r, ig, uc, u, decay, mult, wt['b_conv_w'], wt['b_w_rg'], wt['b_w_ig'], wt['b_lam'], pad, tm)
    dh1, dw_in_b, dg_b = _norm_matmul_bwd("b_in_proj_bwd", h1, wt['b_norm_g'], wt['b_w_in'], [du, dgate_b], tm, resid=dh2)

    grads_b = [_matrix_grad(n, g, heads) for n, g in
               (('b_w_in', dw_in_b), ('b_w_rg', dw_rg), ('b_w_ig', dw_ig), ('b_w_out', dw_out_b))]
    dattn, dgate_a, dw_out_a, delta, *got = _gated_out_bwd(
        "a_out_proj_bwd", dh1, attn, gate_a, wt['a_w_out'], tm, delta_heads=heads,
        exchange=_halves_to_sibling("swap_b", grads_b))
    sums_b = [_chip_sum(gr, r, c_idx) for gr, r in zip(grads_b, got)]
    grad_out = _matrix_grad('a_w_out', dw_out_a, heads)
    dq, dk, dv, *landed = _flash_bwd(
        q, k, v, lse, delta, dattn, heads, pad, tm,
        exchange=_blocks_to_chips("chips_b", sums_b) + _halves_to_sibling("swap_out", [grad_out]))
    through = list(zip(grads_b, sums_b, landed[:len(grads_b)]))
    sum_out = _chip_sum(grad_out, landed[len(grads_b)], c_idx)

    def q_prologue(dy_refs, dy_s, ex_ref):
        (dq_ref,), cos_v, sin_v = dy_refs[:1], dy_refs[1][...], dy_refs[2][...]
        for h in range(heads):
            c0 = h * HEAD_W
            dy_s[:, c0:c0 + QK_NOPE] = dq_ref[:, c0:c0 + QK_NOPE].astype(BF16)
            dy_s[:, c0 + QK_NOPE:c0 + HEAD_W] = _unrope(dq_ref[:, c0 + QK_NOPE:c0 + HEAD_W], cos_v, sin_v).astype(BF16)

    dq_lat, dw_uq, dg_q, from_chips_out = _norm_matmul_bwd(
        "a_q_proj_bwd", q_lat, wt['a_q_norm_g'], w_uq, [dq, cos_t, sin_t], tm, prologue=q_prologue,
        exchange=_blocks_to_chips("chips_out", [sum_out]))
    through.append((grad_out, sum_out, from_chips_out))
    grad_uq = _matrix_grad('a_w_uq', dw_uq, heads)

    def kv_prologue(dy_refs, dy_s, ex_ref):
        dk_ref, dv_ref = dy_refs[:2]
        cos_v, sin_v = dy_refs[2][...], dy_refs[3][...]
        dkr = jnp.zeros((dk_ref.shape[0], LANES), F32)
        for h in range(heads):
            c0 = h * (QK_NOPE + V_HEAD)
            dy_s[:, c0:c0 + QK_NOPE] = dk_ref[:, h * HEAD_W:h * HEAD_W + QK_NOPE].astype(BF16)
            dy_s[:, c0 + QK_NOPE:c0 + QK_NOPE + V_HEAD] = dv_ref[:, h * V_HEAD:(h + 1) * V_HEAD].astype(BF16)
            dkr = dkr + dk_ref[:, h * HEAD_W + QK_NOPE:(h + 1) * HEAD_W]
        ex_ref[...] = _unrope(dkr, cos_v, sin_v)

    dkv_lat, dw_ukv, dg_kv, dkr_raw, got_uq = _norm_matmul_bwd(
        "a_kv_proj_bwd", kv_lat, wt['a_kv_norm_g'], w_ukv, [dk, dv, cos_t, sin_t], tm,
        prologue=kv_prologue, extra_out=(LANES, F32), exchange=_halves_to_sibling("swap_uq", [grad_uq]))
    sum_uq = _chip_sum(grad_uq, got_uq, c_idx)
    grad_ukv = _matrix_grad('a_w_ukv', dw_ukv, heads)

    d_meta, d_x, dw_in_a, dg_a, from_chips_uq, got_ukv = _norm_matmul_bwd(
        "a_in_proj_bwd", h0, wt['a_norm_g'], w_in_a, [dq_lat, dkv_lat, dkr_raw, dgate_a], tm, resid=dh1,
        split=(pad, x0),
        exchange=_blocks_to_chips("chips_uq", [sum_uq]) + _halves_to_sibling("swap_ukv", [grad_ukv]))
    through.append((grad_uq, sum_uq, from_chips_uq))
    swapped = [(grad_ukv, _chip_sum(grad_ukv, got_ukv, c_idx))]

    dw_in_a = jnp.concatenate([dw_in_a[:, :c_kr + QK_ROPE], dw_in_a[:, c_gate:]], axis=1)
    grads = {
        'meta_tokens': d_meta, 'a_norm_g': dg_a, 'a_w_in': dw_in_a, 'a_q_norm_g': dg_q, 'a_kv_norm_g': dg_kv,
        'a_w_uq': dw_uq, 'a_w_ukv': dw_ukv, 'a_w_out': dw_out_a, 'b_norm_g': dg_b, 'b_w_in': dw_in_b,
        'b_conv_w': dconv_w, 'b_conv_b': dconv_b, 'b_w_rg': dw_rg, 'b_b_rg': db_rg, 'b_w_ig': dw_ig,
        'b_b_ig': db_ig, 'b_lam': dlam, 'b_w_out': dw_out_b, 'final_norm_g': d_final_g,
    }
    return loss, d_x, grads, through, swapped


def _chip_major(whole, local_shape, axis):
    if axis is None:
        return jnp.broadcast_to(whole.reshape(1, -1), (N_CHIPS, whole.size))
    shape = list(local_shape)
    g = whole.reshape(shape[:axis] + [N_CHIPS, shape[axis]] + shape[axis + 1:])
    return jnp.moveaxis(g, axis, 0).reshape(N_CHIPS, -1)


def kernel(x, meta_tokens, a_norm_g, a_w_in, a_q_norm_g, a_kv_norm_g, a_w_uq, a_w_ukv, a_w_out, b_norm_g, b_w_in, b_conv_w, b_conv_b, b_w_rg, b_b_rg, b_w_ig, b_b_ig, b_lam, b_w_out, final_norm_g, loss_target, m_meta_tokens, m_a_norm_g, m_a_w_in, m_a_q_norm_g, m_a_kv_norm_g, m_a_w_uq, m_a_w_ukv, m_a_w_out, m_b_norm_g, m_b_w_in, m_b_conv_w, m_b_conv_b, m_b_w_rg, m_b_b_rg, m_b_w_ig, m_b_b_ig, m_b_lam, m_b_w_out, m_final_norm_g, v_meta_tokens, v_a_norm_g, v_a_w_in, v_a_q_norm_g, v_a_kv_norm_g, v_a_w_uq, v_a_w_ukv, v_a_w_out, v_b_norm_g, v_b_w_in, v_b_conv_w, v_b_conv_b, v_b_w_rg, v_b_b_rg, v_b_w_ig, v_b_b_ig, v_b_lam, v_b_w_out, v_final_norm_g):
    local_w = dict(zip(WEIGHTS, (meta_tokens, a_norm_g, a_w_in, a_q_norm_g, a_kv_norm_g, a_w_uq, a_w_ukv, a_w_out,
                                 b_norm_g, b_w_in, b_conv_w, b_conv_b, b_w_rg, b_b_rg, b_w_ig, b_b_ig, b_lam,
                                 b_w_out, final_norm_g)))
    local_m = dict(zip(WEIGHTS, (m_meta_tokens, m_a_norm_g, m_a_w_in, m_a_q_norm_g, m_a_kv_norm_g, m_a_w_uq,
                                 m_a_w_ukv, m_a_w_out, m_b_norm_g, m_b_w_in, m_b_conv_w, m_b_conv_b, m_b_w_rg,
                                 m_b_b_rg, m_b_w_ig, m_b_b_ig, m_b_lam, m_b_w_out, m_final_norm_g)))
    local_v = dict(zip(WEIGHTS, (v_meta_tokens, v_a_norm_g, v_a_w_in, v_a_q_norm_g, v_a_kv_norm_g, v_a_w_uq,
                                 v_a_w_ukv, v_a_w_out, v_b_norm_g, v_b_w_in, v_b_conv_w, v_b_conv_b, v_b_w_rg,
                                 v_b_b_rg, v_b_w_ig, v_b_b_ig, v_b_lam, v_b_w_out, v_final_norm_g)))
    matrices = ('a_w_in', 'a_w_uq', 'a_w_ukv', 'a_w_out', 'b_w_in', 'b_w_rg', 'b_w_ig', 'b_w_out')
    heads = a_w_uq.shape[-1] * N_CHIPS // (QK_NOPE + QK_ROPE)

    split, small, mid, late = [], [], [], []
    for n in WEIGHTS:
        if SHARD_AXIS[n] is None:
            continue
        if n.startswith('b_') or n == 'a_w_out':
            late.append((n, local_w[n].astype(BF16) if n in matrices else local_w[n], SHARD_AXIS[n]))
        elif n == 'a_w_in':
            split.append((n, local_w[n].astype(BF16)[None], 0, 2))
        elif n in matrices:
            mid.append((n, local_w[n].astype(BF16), SHARD_AXIS[n]))
        else:
            small.append((n, local_w[n], SHARD_AXIS[n]))
    gathered = _gather_weights([s[1:] for s in split], [s[1:] for s in small])
    whole = dict(zip([s[0] for s in split + small], gathered))
    whole['a_w_in'] = _unshard(whole['a_w_in'], SHARD_AXIS['a_w_in'])
    mid_names, late_names = [s[0] for s in mid], [s[0] for s in late]
    wt = {n: _kernel_form(n, whole.get(n, local_w[n])) for n in WEIGHTS if n not in mid_names + late_names}

    c_idx = lax.axis_index("c").astype(jnp.int32).reshape(1)
    me_idx = (2 * lax.axis_index("x") + lax.axis_index("y")).astype(jnp.int32).reshape(1)
    loss, grad_x, grads, through, swapped = _local_grads(
        x[0], loss_target[0], wt, heads, c_idx,
        mid_names, _gather_whole("gather_weights_a", [s[1:] for s in mid]),
        late_names, _gather_whole("gather_weights_b", [s[1:] for s in late]))

    ext_uq = heads % N_CHIPS == 0
    started = [gr.name for gr, *_ in through + swapped]
    last = [_matrix_grad(n, grads[n], heads) for n in matrices if n not in started]
    rest = [n for n in WEIGHTS if n not in matrices]
    pieces = [_chip_major(grads[n], local_w[n].shape, SHARD_AXIS[n]) for n in rest]
    pieces.append(jnp.broadcast_to(loss[0:1, 0:1], (N_CHIPS, 1)))
    length = sum(p.shape[1] for p in pieces)
    unit = 2 * SUBLANES * 1024
    padded = -(-length // unit) * unit
    flat = jnp.concatenate(pieces + [jnp.zeros((N_CHIPS, padded - length), F32)], axis=1)
    last.append(_Grad('small', flat.reshape(N_CHIPS, padded // 1024, 1024), 'lead', padded // 2048, 1024))

    got = _halves_to_sibling("grads_to_sibling", last).run()
    swapped = swapped + [(gr, _chip_sum(gr, r, c_idx, F32 if gr.name == 'small' else BF16))
                         for gr, r in zip(last, got)]
    from_chips = _blocks_to_chips("grads_to_chips", [p for _, p in swapped]).run()
    through = through + [(gr, p, r) for (gr, p), r in zip(swapped, from_chips)]
    halves = [_sum_chips(gr.name, p, r, me_idx, c_idx) for gr, p, r in through]
    summed = dict(zip([gr.name for gr, _, _ in through], _share_with_sibling(halves)))
    if ext_uq:
        g = summed['a_w_uq']
        summed['a_w_uq'] = g.reshape(g.shape[0], -1, HEAD_W)[:, :, :QK_NOPE + QK_ROPE]
    total = summed['small'].reshape(-1)

    out_g, out_d, out_m, out_v = [], [], [], []
    off = 0
    for n in WEIGHTS:
        shape = local_w[n].shape
        if n in matrices:
            g = summed[n].reshape(shape)
        else:
            size = 1
            for s in shape:
                size *= s
            g = total[off:off + size].reshape(shape)
            off += size
        delta, new_m, new_v = _adamw("adamw_" + n, _as2d(local_w[n]), _as2d(g), _as2d(local_m[n]), _as2d(local_v[n]))
        out_g.append(g)
        out_d.append(delta.reshape(shape))
        out_m.append(new_m.reshape(shape))
        out_v.append(new_v.reshape(shape))

    return (total[off], grad_x[None], *out_g, *out_d, *out_m, *out_v)
```

```python
import functools

import jax
import jax.numpy as jnp
from jax import lax
from jax.experimental import pallas as pl
from jax.experimental.pallas import tpu as pltpu

F32 = jnp.float32
BF16 = jnp.bfloat16
MESH = pl.DeviceIdType.MESH

RMS_EPS = 1e-6
QK_NOPE = 128
QK_ROPE = 64
V_HEAD = 128
HEAD_W = 256
ROPE_BASE = 10000.0
Q_BLOCK = 128
MASK_VALUE = -1e30
CONV_WIDTH = 4
LRU_C = 8.0
N_CHIPS = 4

ADAM_LR = 0.001
ADAM_B1 = 0.9
ADAM_B2 = 0.999
ADAM_EPS = 1e-08
ADAM_WD = 0.01
ADAM_STEP = 10

VMEM_LIMIT_V7X = 56 * 1024 * 1024
LANES = 128
SUBLANES = 8

WEIGHTS = ['meta_tokens', 'a_norm_g', 'a_w_in', 'a_q_norm_g', 'a_kv_norm_g', 'a_w_uq', 'a_w_ukv',
           'a_w_out', 'b_norm_g', 'b_w_in', 'b_conv_w', 'b_conv_b', 'b_w_rg', 'b_b_rg', 'b_w_ig',
           'b_b_ig', 'b_lam', 'b_w_out', 'final_norm_g']
SHARD_AXIS = {'meta_tokens': 1, 'a_norm_g': None, 'a_w_in': 2, 'a_q_norm_g': None, 'a_kv_norm_g': None,
              'a_w_uq': 2, 'a_w_ukv': 2, 'a_w_out': 1, 'b_norm_g': 1, 'b_w_in': 2, 'b_conv_w': 2,
              'b_conv_b': 1, 'b_w_rg': 2, 'b_b_rg': 1, 'b_w_ig': 2, 'b_b_ig': 1, 'b_lam': 1,
              'b_w_out': 1, 'final_norm_g': None}


def _params(sem=None):
    return pltpu.CompilerParams(dimension_semantics=sem, vmem_limit_bytes=VMEM_LIMIT_V7X)


def _row_tile(tp):
    return 384 if (tp % 384 == 0 and tp >= 1152) else 128


def _sigmoid(x):
    return 1.0 / (1.0 + jnp.exp(-x))


def _rms(x):
    return lax.rsqrt(jnp.mean(x * x, axis=-1, keepdims=True) + RMS_EPS)


def _swap32(x):
    lane = lax.broadcasted_iota(jnp.int32, x.shape, 1)
    return jnp.where(lane < 32, pltpu.roll(x, 96, 1), pltpu.roll(x, 32, 1))


def _rope(x, cos_t, sin_t):
    return x * cos_t + _swap32(x) * sin_t


def _unrope(d, cos_t, sin_t):
    lane = lax.broadcasted_iota(jnp.int32, d.shape, 1)
    return jnp.where(lane < QK_ROPE, d * cos_t + _swap32(d * sin_t), 0.0)


def _dot(a, b):
    return jnp.dot(a, b, preferred_element_type=F32)


def _dot_nt(a, b):
    return lax.dot_general(a, b, (((1,), (1,)), ((), ())), preferred_element_type=F32)


def _dot_tn(a, b):
    return lax.dot_general(a, b, (((0,), (0,)), ((), ())), preferred_element_type=F32)


def _norm_matmul(name, x, g, w, splits, tm, exchange=None):
    tp, kin = x.shape
    n = w.shape[1]
    nt = tp // tm

    def compute(x_ref, g_ref, w_ref, *outs):
        xv = x_ref[...]
        nrm = ((xv * _rms(xv)) * g_ref[...]).astype(BF16)
        y = _dot(nrm, w_ref[...])
        for o_ref, (c0, c1) in zip(outs, splits):
            o_ref[...] = y[:, c0:c1]

    in_specs = [pl.BlockSpec((tm, kin), lambda i: (i, 0)),
                pl.BlockSpec((1, kin), lambda i: (0, 0)),
                pl.BlockSpec((kin, n), lambda i: (0, 0))]
    out_specs = [pl.BlockSpec((tm, c1 - c0), lambda i: (i, 0)) for c0, c1 in splits]
    out_shape = [jax.ShapeDtypeStruct((tp, c1 - c0), F32) for c0, c1 in splits]
    args, scratch = [x, g, w], []
    body = _with_exchange(exchange, len(in_specs), len(out_specs),
                          lambda: pl.program_id(0) == 0, lambda: pl.program_id(0) == nt - 1, compute)
    if exchange is not None:
        in_specs = in_specs + exchange.specs(exchange.n_in)
        out_specs = out_specs + exchange.specs(exchange.n_out)
        out_shape = out_shape + exchange.out_shapes
        args, scratch = args + exchange.arrays, exchange.scratch()
    return pl.pallas_call(
        body, name=name, grid=(nt,),
        in_specs=in_specs, out_specs=out_specs, out_shape=out_shape, scratch_shapes=scratch,
        compiler_params=_params(("arbitrary",)),
    )(*args)


def _q_proj(q_lat, g, w_uq, cos_t, sin_t, heads, tm):
    tp, kin = q_lat.shape
    n = heads * HEAD_W

    def body(x_ref, g_ref, w_ref, cos_ref, sin_ref, q_ref):
        xv = x_ref[...]
        nrm = ((xv * _rms(xv)) * g_ref[...]).astype(BF16)
        y = _dot(nrm, w_ref[...])
        cos_v, sin_v = cos_ref[...], sin_ref[...]
        for h in range(heads):
            c0 = h * HEAD_W
            q_ref[:, c0:c0 + QK_NOPE] = y[:, c0:c0 + QK_NOPE].astype(BF16)
            q_ref[:, c0 + QK_NOPE:c0 + HEAD_W] = _rope(y[:, c0 + QK_NOPE:c0 + HEAD_W], cos_v, sin_v).astype(BF16)

    return pl.pallas_call(
        body, name="a_q_proj", grid=(tp // tm,),
        in_specs=[pl.BlockSpec((tm, kin), lambda i: (i, 0)),
                  pl.BlockSpec((1, kin), lambda i: (0, 0)),
                  pl.BlockSpec((kin, n), lambda i: (0, 0)),
                  pl.BlockSpec((tm, LANES), lambda i: (i, 0)),
                  pl.BlockSpec((tm, LANES), lambda i: (i, 0))],
        out_specs=pl.BlockSpec((tm, n), lambda i: (i, 0)),
        out_shape=jax.ShapeDtypeStruct((tp, n), BF16),
        compiler_params=_params(("parallel",)),
    )(q_lat, g, w_uq, cos_t, sin_t)


def _kv_proj(kv_lat, g, w_ukv, k_rope_raw, cos_t, sin_t, heads, tm):
    tp, kin = kv_lat.shape
    n = heads * (QK_NOPE + V_HEAD)

    def body(x_ref, g_ref, w_ref, kr_ref, cos_ref, sin_ref, k_ref, v_ref):
        xv = x_ref[...]
        nrm = ((xv * _rms(xv)) * g_ref[...]).astype(BF16)
        y = _dot(nrm, w_ref[...])
        kr = _rope(kr_ref[...], cos_ref[...], sin_ref[...]).astype(BF16)
        for h in range(heads):
            c0 = h * (QK_NOPE + V_HEAD)
            k_ref[:, h * HEAD_W:h * HEAD_W + QK_NOPE] = y[:, c0:c0 + QK_NOPE].astype(BF16)
            k_ref[:, h * HEAD_W + QK_NOPE:(h + 1) * HEAD_W] = kr
            v_ref[:, h * V_HEAD:(h + 1) * V_HEAD] = y[:, c0 + QK_NOPE:c0 + QK_NOPE + V_HEAD].astype(BF16)

    return pl.pallas_call(
        body, name="a_kv_proj", grid=(tp // tm,),
        in_specs=[pl.BlockSpec((tm, kin), lambda i: (i, 0)),
                  pl.BlockSpec((1, kin), lambda i: (0, 0)),
                  pl.BlockSpec((kin, n), lambda i: (0, 0)),
                  pl.BlockSpec((tm, LANES), lambda i: (i, 0)),
                  pl.BlockSpec((tm, LANES), lambda i: (i, 0)),
                  pl.BlockSpec((tm, LANES), lambda i: (i, 0))],
        out_specs=[pl.BlockSpec((tm, heads * HEAD_W), lambda i: (i, 0)),
                   pl.BlockSpec((tm, heads * V_HEAD), lambda i: (i, 0))],
        out_shape=[jax.ShapeDtypeStruct((tp, heads * HEAD_W), BF16),
                   jax.ShapeDtypeStruct((tp, heads * V_HEAD), BF16)],
        compiler_params=_params(("parallel",)),
    )(kv_lat, g, w_ukv, k_rope_raw, cos_t, sin_t)


def _as_rows(col):
    rows = col.shape[0]
    return jnp.transpose(jnp.broadcast_to(col, (rows, LANES)))[0:SUBLANES, :]


def _attn_mask(row0, col0, rows, cols, pad):
    row = row0 + lax.broadcasted_iota(jnp.int32, (rows, cols), 0)
    col = col0 + lax.broadcasted_iota(jnp.int32, (rows, cols), 1)
    return (col <= row) & (col >= pad)


LOG2E = 1.4426950408889634
FLASH_FWD_TRIPS = ((4, 2), (2, 2), (1, 1))


def _flash_fwd(q, k, v, heads, pad, tq, exchange=None):
    tp = q.shape[0]
    nq = tp // tq
    c2 = (QK_NOPE + QK_ROPE) ** -0.5 * LOG2E

    def compute(q_ref, k_ref, v_ref, o_ref, lse_ref):
        i = pl.program_id(1)

        def make_step(masked, blocks, parts=1):
            keys = blocks * tq // parts

            def step(j, carry):
                m, l, acc = carry
                offs = [pl.multiple_of(j * tq + part * keys, tq) for part in range(parts)]
                scores = [_dot_nt(q_ref[...], k_ref[pl.ds(off, keys), :]) for off in offs]
                for off, s in zip(offs, scores):
                    s = s * c2
                    if masked:
                        s = jnp.where(_attn_mask(i * tq, j * tq, tq, keys, pad), s, MASK_VALUE)
                    m_new = jnp.maximum(m, jnp.max(s, axis=-1, keepdims=True))
                    p = jnp.exp2(s - m_new)
                    alpha = jnp.exp2(m - m_new)
                    l = alpha * l + jnp.sum(p, axis=-1, keepdims=True)
                    acc = alpha * acc + _dot(p.astype(BF16), v_ref[pl.ds(off, keys), :])
                    m = m_new
                return m, l, acc
            return step

        init = (jnp.full((tq, 1), MASK_VALUE, F32), jnp.zeros((tq, 1), F32), jnp.zeros((tq, V_HEAD), F32))
        carry = make_step(True, 1)(0, init)
        first = 1
        for blocks, parts in FLASH_FWD_TRIPS:
            trips = jnp.maximum(i - first, 0) // blocks
            step_n = make_step(False, blocks, parts)
            carry = lax.fori_loop(0, trips, lambda t, cr, f=first, b=blocks, s=step_n: s(f + b * t, cr), carry)
            first = first + blocks * trips
        m, l, acc = lax.fori_loop(jnp.maximum(i, 1), i + 1, make_step(True, 1), carry)
        o_ref[...] = acc / l
        lse_ref[...] = _as_rows(m + jnp.log(l) * LOG2E)

    in_specs = [pl.BlockSpec((tq, HEAD_W), lambda h, i: (i, h)),
                pl.BlockSpec((tp, HEAD_W), lambda h, i: (0, h)),
                pl.BlockSpec((tp, V_HEAD), lambda h, i: (0, h))]
    out_specs = [pl.BlockSpec((tq, V_HEAD), lambda h, i: (i, h)),
                 pl.BlockSpec((None, None, SUBLANES, tq), lambda h, i: (h, i, 0, 0))]
    out_shape = [jax.ShapeDtypeStruct((tp, heads * V_HEAD), F32),
                 jax.ShapeDtypeStruct((heads, nq, SUBLANES, tq), F32)]
    args, scratch = [q, k, v], []
    body = _with_exchange(exchange, len(in_specs), len(out_specs),
                          lambda: (pl.program_id(0) == 0) & (pl.program_id(1) == 0),
                          lambda: (pl.program_id(0) == heads - 1) & (pl.program_id(1) == nq - 1), compute)
    if exchange is not None:
        in_specs = in_specs + exchange.specs(exchange.n_in)
        out_specs = out_specs + exchange.specs(exchange.n_out)
        out_shape = out_shape + exchange.out_shapes
        args, scratch = args + exchange.arrays, exchange.scratch()
    return pl.pallas_call(
        body, name="a_flash_fwd", grid=(heads, nq),
        in_specs=in_specs, out_specs=out_specs, out_shape=out_shape, scratch_shapes=scratch,
        compiler_params=_params(("arbitrary", "arbitrary")),
    )(*args)


def _gated_out(name, a, gate, w, resid, tm):
    tp, wd = a.shape
    d = w.shape[1]

    def body(a_ref, gate_ref, w_ref, res_ref, o_ref):
        gv = gate_ref[...]
        y = (a_ref[...] * (gv * _sigmoid(gv))).astype(BF16)
        o_ref[...] = res_ref[...] + _dot(y, w_ref[...])

    return pl.pallas_call(
        body, name=name, grid=(tp // tm,),
        in_specs=[pl.BlockSpec((tm, wd), lambda i: (i, 0)),
                  pl.BlockSpec((tm, wd), lambda i: (i, 0)),
                  pl.BlockSpec((wd, d), lambda i: (0, 0)),
                  pl.BlockSpec((tm, d), lambda i: (i, 0))],
        out_specs=pl.BlockSpec((tm, d), lambda i: (i, 0)),
        out_shape=jax.ShapeDtypeStruct((tp, d), F32),
        compiler_params=_params(("parallel",)),
    )(a, gate, w, resid)


def _lru_decay(r, sp):
    log_a = -LRU_C * r * sp
    a = jnp.exp(log_a)
    e2 = a * a
    x2 = 2.0 * log_a
    series = x2 * (1.0 + x2 * (0.5 + x2 * (1.0 / 6.0)))
    em1 = jnp.where(x2 > -0.02, series, e2 - 1.0)
    return a, e2, jnp.sqrt(-em1)


def _softplus(x):
    return jnp.maximum(x, 0.0) + jnp.log1p(jnp.exp(-jnp.abs(x)))


def _rglru_fwd(u, conv_w, conv_b, w_rg, b_rg, w_ig, b_ig, lam, pad, tm):
    tp, w = u.shape
    groups, blk = w_rg.shape[0], w_rg.shape[1]

    def body(u_ref, cw_ref, cb_ref, wr_ref, br_ref, wi_ref, bi_ref, lam_ref,
             uc_ref, r_ref, ig_ref, hs_ref, a_s, mult_ref, uext, b_s, hc):
        i = pl.program_id(0)

        @pl.when(i == 0)
        def _():
            uext[0:SUBLANES, :] = jnp.zeros((SUBLANES, w), F32)
            hc[...] = jnp.zeros((SUBLANES, w), F32)

        uext[SUBLANES:SUBLANES + tm, :] = u_ref[...]
        cw = cw_ref[...]
        uc = cb_ref[...] + uext[pl.ds(SUBLANES - 3, tm), :] * cw[0:1, :]
        uc = uc + uext[pl.ds(SUBLANES - 2, tm), :] * cw[1:2, :]
        uc = uc + uext[pl.ds(SUBLANES - 1, tm), :] * cw[2:3, :]
        uc = uc + uext[pl.ds(SUBLANES, tm), :] * cw[3:4, :]
        uc_ref[...] = uc
        uext[0:SUBLANES, :] = uext[tm:tm + SUBLANES, :]

        sp = _softplus(-lam_ref[...])
        for g in range(groups):
            sl = slice(g * blk, (g + 1) * blk)
            ucg = uc_ref[:, sl]
            ucb = ucg.astype(BF16)
            r = _sigmoid(_dot(ucb, wr_ref[g]) + br_ref[:, sl])
            ig = _sigmoid(_dot(ucb, wi_ref[g]) + bi_ref[:, sl])
            r_ref[:, sl] = r
            ig_ref[:, sl] = ig
            a, _, mult = _lru_decay(r, sp[:, sl])
            a_s[:, sl] = a
            mult_ref[:, sl] = mult
            b_s[:, sl] = mult * (ig * ucg)

        @pl.when(i == 0)
        def _():
            row = lax.broadcasted_iota(jnp.int32, (Q_BLOCK, w), 0)
            start = ig_ref[0:Q_BLOCK, :] * uc_ref[0:Q_BLOCK, :]
            b_s[0:Q_BLOCK, :] = jnp.where(row < pad, 0.0, jnp.where(row == pad, start, b_s[0:Q_BLOCK, :]))
            mult_ref[0:Q_BLOCK, :] = jnp.where(row == pad, 1.0, mult_ref[0:Q_BLOCK, :])

        row8 = lax.broadcasted_iota(jnp.int32, (SUBLANES, w), 0)

        def group(gi, h_in):
            off = pl.multiple_of(gi * SUBLANES, SUBLANES)
            av = a_s[pl.ds(off, SUBLANES), :]
            bv = b_s[pl.ds(off, SUBLANES), :]
            for k in (1, 2, 4):
                keep = row8 >= k
                bv = jnp.where(keep, av * pltpu.roll(bv, k, 0) + bv, bv)
                av = jnp.where(keep, av * pltpu.roll(av, k, 0), av)
            hv = av * h_in + bv
            hs_ref[pl.ds(off, SUBLANES), :] = hv
            return jnp.broadcast_to(hv[SUBLANES - 1:SUBLANES, :], (SUBLANES, w))

        hc[...] = lax.fori_loop(0, tm // SUBLANES, group, hc[...])

    row_spec = pl.BlockSpec((tm, w), lambda i: (i, 0))
    vec_spec = pl.BlockSpec((1, w), lambda i: (0, 0))
    mat_spec = pl.BlockSpec((groups, blk, blk), lambda i: (0, 0, 0))
    return pl.pallas_call(
        body, name="b_rglru_fwd", grid=(tp // tm,),
        in_specs=[row_spec, pl.BlockSpec((CONV_WIDTH, w), lambda i: (0, 0)), vec_spec,
                  mat_spec, vec_spec, mat_spec, vec_spec, vec_spec],
        out_specs=[row_spec] * 6,
        out_shape=[jax.ShapeDtypeStruct((tp, w), F32)] * 6,
        scratch_shapes=[pltpu.VMEM((tm + SUBLANES, w), F32), pltpu.VMEM((tm, w), F32),
                        pltpu.VMEM((SUBLANES, w), F32)],
        compiler_params=_params(("arbitrary",)),
    )(u, conv_w, conv_b, w_rg, b_rg, w_ig, b_ig, lam)


def _final_loss(h, g, target, x0, tm):
    tp, d = h.shape
    assert x0 % Q_BLOCK == 0 and tm % Q_BLOCK == 0 and target.shape[0] == tp - x0
    lead = x0 // Q_BLOCK
    per = tm // Q_BLOCK

    def body(h_ref, g_ref, *rest):
        t_refs, (dh_ref, loss_ref, dg_ref) = rest[:per], rest[per:]
        i = pl.program_id(0)

        @pl.when(i == 0)
        def _():
            loss_ref[...] = jnp.zeros_like(loss_ref)
            dg_ref[...] = jnp.zeros_like(dg_ref)

        gv = g_ref[...]
        for b in range(per):
            rows = slice(b * Q_BLOCK, (b + 1) * Q_BLOCK)
            xv = h_ref[rows, :]
            r = _rms(xv)
            xh = xv * r
            err = jnp.where(i * per + b >= lead, xh * gv - t_refs[b][...], 0.0)
            loss_ref[...] += 0.5 * jnp.sum(jnp.mean(err * err, axis=-1, keepdims=True))
            dy = err / d
            dg_ref[...] += jnp.sum(dy * xh, axis=0, keepdims=True)
            dxh = dy * gv
            dh_ref[rows, :] = r * (dxh - xh * jnp.mean(dxh * xh, axis=-1, keepdims=True))

    def piece(b):
        return pl.BlockSpec((Q_BLOCK, d), lambda i: (jnp.maximum(i * per + b - lead, 0), 0))

    return pl.pallas_call(
        body, name="final_loss", grid=(tp // tm,),
        in_specs=[pl.BlockSpec((tm, d), lambda i: (i, 0)),
                  pl.BlockSpec((1, d), lambda i: (0, 0))] + [piece(b) for b in range(per)],
        out_specs=[pl.BlockSpec((tm, d), lambda i: (i, 0)),
                   pl.BlockSpec((SUBLANES, LANES), lambda i: (0, 0)),
                   pl.BlockSpec((1, d), lambda i: (0, 0))],
        out_shape=[jax.ShapeDtypeStruct((tp, d), F32),
                   jax.ShapeDtypeStruct((SUBLANES, LANES), F32),
                   jax.ShapeDtypeStruct((1, d), F32)],
        compiler_params=_params(("arbitrary",)),
    )(h, g, *([target] * per))


def _with_exchange(exchange, n_in, n_out, first, last, compute):
    if exchange is None:
        return compute
    ex_in, ex_out = exchange.n_in, exchange.n_out

    def body(*refs):
        own_in, their_in = refs[:n_in], refs[n_in:n_in + ex_in]
        pos = n_in + ex_in
        own_out, their_out = refs[pos:pos + n_out], refs[pos + n_out:pos + n_out + ex_out]
        rest = refs[pos + n_out + ex_out:]
        own_scratch, sems = rest[:len(rest) - 2], rest[len(rest) - 2:]

        @pl.when(first())
        def _():
            exchange.start(their_in, their_out, sems)

        compute(*own_in, *own_out, *own_scratch)

        @pl.when(last())
        def _():
            exchange.finish(their_in, their_out, sems)

    return body


def _gated_out_bwd(name, dout, a, gate, w, tm, delta_heads=0, exchange=None):
    tp, wd = a.shape
    d = w.shape[1]
    nt = tp // tm

    def compute(do_ref, a_ref, gate_ref, w_ref, da_ref, dgate_ref, dw_ref, *delta_ref):
        @pl.when(pl.program_id(0) == 0)
        def _():
            dw_ref[...] = jnp.zeros_like(dw_ref)

        gv = gate_ref[...]
        av = a_ref[...]
        sg = _sigmoid(gv)
        silu = gv * sg
        dob = do_ref[...].astype(BF16)
        dy = _dot_nt(dob, w_ref[...])
        da = dy * silu
        da_ref[...] = da
        dgate_ref[...] = dy * av * (sg * (1.0 + gv * (1.0 - sg)))
        dw_ref[...] += _dot_tn((av * silu).astype(BF16), dob)
        for h in range(delta_heads):
            sl = slice(h * V_HEAD, (h + 1) * V_HEAD)
            delta_ref[0][h] = _as_rows(jnp.sum(da[:, sl] * av[:, sl], axis=-1, keepdims=True))

    out_specs = [pl.BlockSpec((tm, wd), lambda i: (i, 0)),
                 pl.BlockSpec((tm, wd), lambda i: (i, 0)),
                 pl.BlockSpec((wd, d), lambda i: (0, 0))]
    out_shape = [jax.ShapeDtypeStruct((tp, wd), F32),
                 jax.ShapeDtypeStruct((tp, wd), F32),
                 jax.ShapeDtypeStruct((wd, d), F32)]
    if delta_heads:
        out_specs.append(pl.BlockSpec((delta_heads, None, SUBLANES, tm), lambda i: (0, i, 0, 0)))
        out_shape.append(jax.ShapeDtypeStruct((delta_heads, tp // tm, SUBLANES, tm), F32))
    in_specs = [pl.BlockSpec((tm, d), lambda i: (i, 0)),
                pl.BlockSpec((tm, wd), lambda i: (i, 0)),
                pl.BlockSpec((tm, wd), lambda i: (i, 0)),
                pl.BlockSpec((wd, d), lambda i: (0, 0))]
    args, scratch = [dout, a, gate, w], []
    body = _with_exchange(exchange, len(in_specs), len(out_specs),
                          lambda: pl.program_id(0) == 0, lambda: pl.program_id(0) == nt - 1, compute)
    if exchange is not None:
        in_specs = in_specs + exchange.specs(exchange.n_in)
        out_specs = out_specs + exchange.specs(exchange.n_out)
        out_shape = out_shape + exchange.out_shapes
        args, scratch = args + exchange.arrays, exchange.scratch()
    return pl.pallas_call(
        body, name=name, grid=(nt,),
        in_specs=in_specs, out_specs=out_specs, out_shape=out_shape, scratch_shapes=scratch,
        compiler_params=_params(("arbitrary",)),
    )(*args)


def _rglru_bwd(dhs, hs, r, ig, uc, u, a, mult, conv_w, w_rg, w_ig, lam, pad, tm):
    tp, w = u.shape
    groups, blk = w_rg.shape[0], w_rg.shape[1]
    nt = tp // tm
    per8 = tm // SUBLANES

    def body(dhs_ref, hs_ref, hprev_ref, r_ref, ig_ref, uc_ref, u_ref, uprev_ref, a_ref, mult_ref,
             cw_ref, wr_ref, wi_ref, lam_ref,
             du_ref, dcw_ref, dcb_ref, dwr_ref, dbr_ref, dwi_ref, dbi_ref, dlam_ref,
             aext, c_s, g_s, hext, uext, ducext, gc):
        step = pl.program_id(0)
        ti = nt - 1 - step

        @pl.when(step == 0)
        def _():
            for ref in (dcw_ref, dcb_ref, dwr_ref, dbr_ref, dwi_ref, dbi_ref, dlam_ref):
                ref[...] = jnp.zeros_like(ref)
            aext[tm:tm + SUBLANES, :] = jnp.zeros((SUBLANES, w), F32)
            ducext[tm:tm + SUBLANES, :] = jnp.zeros((SUBLANES, w), F32)
            gc[...] = jnp.zeros((SUBLANES, w), F32)

        lam_v = lam_ref[...]
        sp = _softplus(-lam_v)
        row = ti * tm + lax.broadcasted_iota(jnp.int32, (tm, w), 0)

        rv = r_ref[...]
        a = a_ref[...]
        mult = mult_ref[...]
        aext[0:tm, :] = a
        c_s[...] = aext[pl.ds(1, tm), :]
        row8 = lax.broadcasted_iota(jnp.int32, (SUBLANES, w), 0)

        def group(gi, g_in):
            off = pl.multiple_of((per8 - 1 - gi) * SUBLANES, SUBLANES)
            cv = c_s[pl.ds(off, SUBLANES), :]
            dv = dhs_ref[pl.ds(off, SUBLANES), :]
            for k in (1, 2, 4):
                keep = row8 < SUBLANES - k
                dv = jnp.where(keep, cv * pltpu.roll(dv, SUBLANES - k, 0) + dv, dv)
                cv = jnp.where(keep, cv * pltpu.roll(cv, SUBLANES - k, 0), cv)
            gv = cv * g_in + dv
            g_s[pl.ds(off, SUBLANES), :] = gv
            return jnp.broadcast_to(gv[0:1, :], (SUBLANES, w))

        gc[...] = lax.fori_loop(0, per8, group, gc[...])
        aext[tm:tm + SUBLANES, :] = aext[0:SUBLANES, :]

        gsc = jnp.where(row < pad, 0.0, g_s[...])
        hext[0:SUBLANES, :] = hprev_ref[...]
        hext[SUBLANES:SUBLANES + tm, :] = hs_ref[...]
        hprev = jnp.where(row == 0, 0.0, hext[pl.ds(SUBLANES - 1, tm), :])
        igv = ig_ref[...]
        ucv = uc_ref[...]
        first = row == pad
        dmult = gsc * (igv * ucv)
        dig = gsc * mult * ucv
        duc = gsc * mult * igv
        dlog_a = (gsc * hprev) * a + jnp.where(first, 0.0, dmult * (-(a * a) / mult))
        dlam_ref[...] += jnp.sum(dlog_a * rv, axis=0, keepdims=True) * (LRU_C * _sigmoid(-lam_v))
        dpre_r = dlog_a * (-LRU_C * sp) * (rv * (1.0 - rv))
        dpre_i = dig * (igv * (1.0 - igv))
        dbr_ref[...] += jnp.sum(dpre_r, axis=0, keepdims=True)
        dbi_ref[...] += jnp.sum(dpre_i, axis=0, keepdims=True)
        for g in range(groups):
            sl = slice(g * blk, (g + 1) * blk)
            ucb = ucv[:, sl].astype(BF16)
            drb = dpre_r[:, sl].astype(BF16)
            dib = dpre_i[:, sl].astype(BF16)
            dwr_ref[g] += _dot_tn(ucb, drb)
            dwi_ref[g] += _dot_tn(ucb, dib)
            ducext[0:tm, sl] = duc[:, sl] + _dot_nt(drb, wr_ref[g]) + _dot_nt(dib, wi_ref[g])

        ducv = ducext[0:tm, :]
        cw = cw_ref[...]
        dcb_ref[...] += jnp.sum(ducv, axis=0, keepdims=True)
        uext[0:SUBLANES, :] = jnp.where(ti == 0, 0.0, uprev_ref[...])
        uext[SUBLANES:SUBLANES + tm, :] = u_ref[...]
        for j in range(CONV_WIDTH):
            ush = uext[pl.ds(SUBLANES - (CONV_WIDTH - 1 - j), tm), :]
            dcw_ref[j:j + 1, :] += jnp.sum(ducv * ush, axis=0, keepdims=True)
        du = ducv * cw[3:4, :]
        for k in range(1, CONV_WIDTH):
            du = du + ducext[pl.ds(k, tm), :] * cw[3 - k:4 - k, :]
        du_ref[...] = du
        ducext[tm:tm + SUBLANES, :] = ducext[0:SUBLANES, :]

    rev = lambda s: (nt - 1 - s, 0)
    halo = lambda s: (jnp.maximum((nt - 1 - s) * per8 - 1, 0), 0)
    row_spec = pl.BlockSpec((tm, w), rev)
    halo_spec = pl.BlockSpec((SUBLANES, w), halo)
    vec_spec = pl.BlockSpec((1, w), lambda s: (0, 0))
    mat_spec = pl.BlockSpec((groups, blk, blk), lambda s: (0, 0, 0))
    cw_spec = pl.BlockSpec((CONV_WIDTH, w), lambda s: (0, 0))
    return pl.pallas_call(
        body, name="b_rglru_bwd", grid=(nt,),
        in_specs=[row_spec, row_spec, halo_spec, row_spec, row_spec, row_spec, row_spec, halo_spec, row_spec, row_spec,
                  cw_spec, mat_spec, mat_spec, vec_spec],
        out_specs=[row_spec, cw_spec, vec_spec, mat_spec, vec_spec, mat_spec, vec_spec, vec_spec],
        out_shape=[jax.ShapeDtypeStruct((tp, w), F32), jax.ShapeDtypeStruct((CONV_WIDTH, w), F32),
                   jax.ShapeDtypeStruct((1, w), F32), jax.ShapeDtypeStruct((groups, blk, blk), F32),
                   jax.ShapeDtypeStruct((1, w), F32), jax.ShapeDtypeStruct((groups, blk, blk), F32),
                   jax.ShapeDtypeStruct((1, w), F32), jax.ShapeDtypeStruct((1, w), F32)],
        scratch_shapes=[pltpu.VMEM((tm + SUBLANES, w), F32), pltpu.VMEM((tm, w), F32), pltpu.VMEM((tm, w), F32),
                        pltpu.VMEM((tm + SUBLANES, w), F32), pltpu.VMEM((tm + SUBLANES, w), F32),
                        pltpu.VMEM((tm + SUBLANES, w), F32), pltpu.VMEM((SUBLANES, w), F32)],
        compiler_params=_params(("arbitrary",)),
    )(dhs, hs, hs, r, ig, uc, u, u, a, mult, conv_w, w_rg, w_ig, lam)


def _norm_matmul_bwd(name, x, g, w, dys, tm, resid=None, prologue=None, extra_out=None, exchange=None):
    tp, kin = x.shape
    n = w.shape[1]
    nt = tp // tm
    n_dy = len(dys)
    has_res = resid is not None
    has_extra = extra_out is not None

    def compute(*refs):
        x_ref, g_ref, w_ref = refs[:3]
        dy_refs = refs[3:3 + n_dy]
        pos = 3 + n_dy
        res_ref = refs[pos] if has_res else None
        pos += int(has_res)
        dx_ref, dw_ref, dg_ref = refs[pos:pos + 3]
        pos += 3
        ex_ref = refs[pos] if has_extra else None
        pos += int(has_extra)
        dy_s = refs[pos]

        @pl.when(pl.program_id(0) == 0)
        def _():
            dw_ref[...] = jnp.zeros_like(dw_ref)
            dg_ref[...] = jnp.zeros_like(dg_ref)

        if prologue is None:
            c0 = 0
            for ref in dy_refs:
                dy_s[:, c0:c0 + ref.shape[1]] = ref[...].astype(BF16)
                c0 += ref.shape[1]
        else:
            prologue(dy_refs, dy_s, ex_ref)

        xv = x_ref[...]
        gv = g_ref[...]
        r = _rms(xv)
        xh = xv * r
        dyb = dy_s[...]
        dn = _dot_nt(dyb, w_ref[...])
        dw_ref[...] += _dot_tn((xh * gv).astype(BF16), dyb)
        dg_ref[...] += jnp.sum(dn * xh, axis=0, keepdims=True)
        dxh = dn * gv
        dx = r * (dxh - xh * jnp.mean(dxh * xh, axis=-1, keepdims=True))
        if has_res:
            dx = dx + res_ref[...]
        dx_ref[...] = dx

    row = lambda width: pl.BlockSpec((tm, width), lambda i: (i, 0))
    in_specs = [row(kin), pl.BlockSpec((1, kin), lambda i: (0, 0)), pl.BlockSpec((kin, n), lambda i: (0, 0))]
    in_specs += [row(a.shape[1]) for a in dys]
    args = [x, g, w, *dys]
    if has_res:
        in_specs.append(row(kin))
        args.append(resid)
    out_specs = [row(kin), pl.BlockSpec((kin, n), lambda i: (0, 0)), pl.BlockSpec((1, kin), lambda i: (0, 0))]
    out_shape = [jax.ShapeDtypeStruct((tp, kin), F32), jax.ShapeDtypeStruct((kin, n), F32),
                 jax.ShapeDtypeStruct((1, kin), F32)]
    if has_extra:
        out_specs.append(row(extra_out[0]))
        out_shape.append(jax.ShapeDtypeStruct((tp, extra_out[0]), extra_out[1]))
    scratch = [pltpu.VMEM((tm, n), BF16)]
    body = _with_exchange(exchange, len(in_specs), len(out_specs),
                          lambda: pl.program_id(0) == 0, lambda: pl.program_id(0) == nt - 1, compute)
    if exchange is not None:
        in_specs = in_specs + exchange.specs(exchange.n_in)
        out_specs = out_specs + exchange.specs(exchange.n_out)
        out_shape = out_shape + exchange.out_shapes
        args, scratch = args + exchange.arrays, scratch + exchange.scratch()
    return pl.pallas_call(
        body, name=name, grid=(nt,),
        in_specs=in_specs, out_specs=out_specs, out_shape=out_shape, scratch_shapes=scratch,
        compiler_params=_params(("arbitrary",)),
    )(*args)


def _flash_bwd(q, k, v, lse, delta, do, heads, pad, tq, exchange=None):
    tp = q.shape[0]
    nq = tp // tq
    scale = (QK_NOPE + QK_ROPE) ** -0.5
    c2 = scale * LOG2E

    def compute(q_ref, k_ref, v_ref, lse_ref, delta_ref, do_ref, dq_ref, dk_ref, dv_ref):
        j = pl.program_id(1)

        @pl.when(j == 0)
        def _():
            dq_ref[...] = jnp.zeros_like(dq_ref)

        kv = k_ref[...]
        vv = v_ref[...]

        def rows_of(ref, i, blocks):
            parts = [ref[i + b][0:1, :] for b in range(blocks)]
            return parts[0] if blocks == 1 else jnp.concatenate(parts, axis=1)

        def make_step(masked, blocks):
            def step(i, carry):
                off = pl.multiple_of(i * tq, tq)
                qv = q_ref[pl.ds(off, blocks * tq), :]
                dob = do_ref[pl.ds(off, blocks * tq), :].astype(BF16)
                p = jnp.exp2(_dot_nt(kv, qv) * c2 - rows_of(lse_ref, i, blocks))
                if masked:
                    key = j * tq + lax.broadcasted_iota(jnp.int32, (tq, tq), 0)
                    qry = j * tq + lax.broadcasted_iota(jnp.int32, (tq, tq), 1)
                    first = jnp.where((key <= qry) & (key >= pad), p[:, :tq], 0.0)
                    p = first if blocks == 1 else jnp.concatenate([first, p[:, tq:]], axis=1)
                dv_ref[...] += _dot(p.astype(BF16), dob)
                dp = _dot_nt(vv, dob)
                ds = (p * (dp - rows_of(delta_ref, i, blocks)) * scale).astype(BF16)
                dk_ref[...] += _dot(ds, qv)
                dq_ref[pl.ds(off, blocks * tq), :] += _dot_tn(ds, kv)
                return carry
            return step

        dk_ref[...] = jnp.zeros_like(dk_ref)
        dv_ref[...] = jnp.zeros_like(dv_ref)
        odd = (nq - j) % 2
        lax.fori_loop(0, odd, lambda t, cr: make_step(True, 1)(j, cr), 0)
        lax.fori_loop(0, 1 - odd, lambda t, cr: make_step(True, 2)(j, cr), 0)
        start = j + 2 - odd
        for blocks in (4, 2):
            trips = (nq - start) // blocks
            step_n = make_step(False, blocks)
            lax.fori_loop(0, trips, lambda t, cr, s=start, b=blocks, f=step_n: f(s + b * t, cr), 0)
            start = start + blocks * trips

    in_specs = [pl.BlockSpec((tp, HEAD_W), lambda h, j: (0, h)),
                pl.BlockSpec((tq, HEAD_W), lambda h, j: (j, h)),
                pl.BlockSpec((tq, V_HEAD), lambda h, j: (j, h)),
                pl.BlockSpec((None, nq, SUBLANES, tq), lambda h, j: (h, 0, 0, 0)),
                pl.BlockSpec((None, nq, SUBLANES, tq), lambda h, j: (h, 0, 0, 0)),
                pl.BlockSpec((tp, V_HEAD), lambda h, j: (0, h))]
    out_specs = [pl.BlockSpec((tp, HEAD_W), lambda h, j: (0, h)),
                 pl.BlockSpec((tq, HEAD_W), lambda h, j: (j, h)),
                 pl.BlockSpec((tq, V_HEAD), lambda h, j: (j, h))]
    out_shape = [jax.ShapeDtypeStruct((tp, heads * HEAD_W), F32),
                 jax.ShapeDtypeStruct((tp, heads * HEAD_W), F32),
                 jax.ShapeDtypeStruct((tp, heads * V_HEAD), F32)]
    args, scratch = [q, k, v, lse, delta, do], []
    body = _with_exchange(exchange, len(in_specs), len(out_specs),
                          lambda: (pl.program_id(0) == 0) & (pl.program_id(1) == 0),
                          lambda: (pl.program_id(0) == heads - 1) & (pl.program_id(1) == nq - 1), compute)
    if exchange is not None:
        in_specs = in_specs + exchange.specs(exchange.n_in)
        out_specs = out_specs + exchange.specs(exchange.n_out)
        out_shape = out_shape + exchange.out_shapes
        args, scratch = args + exchange.arrays, exchange.scratch()
    return pl.pallas_call(
        body, name="a_flash_bwd", grid=(heads, nq),
        in_specs=in_specs, out_specs=out_specs, out_shape=out_shape, scratch_shapes=scratch,
        compiler_params=_params(("arbitrary", "arbitrary")),
    )(*args)


def _position():
    return lax.axis_index("x"), lax.axis_index("y"), lax.axis_index("c")


def _other_chips(x, y):
    return [(1 - x, y), (x, 1 - y), (1 - x, 1 - y)]


def _block(ref, shard_axis, n, k, split_axis=None, m=None, h=None):
    idx = []
    for a in range(len(ref.shape)):
        start = 0
        size = None
        if a == shard_axis:
            start, size = k * n, n
        if a == split_axis:
            size = (n if a == shard_axis else m) // 2
            start = start + h * size
        idx.append(slice(None) if size is None else pl.ds(start, size))
    return ref.at[tuple(idx)]


def _gather_weights(split, whole_small):
    ns, nw = len(split), len(whole_small)
    n = ns + nw
    arrs = [s[0] for s in split] + [s[0] for s in whole_small]
    axes = [s[1] for s in split] + [s[1] for s in whole_small]

    def body(*refs):
        ins, outs = refs[:n], refs[n:2 * n]
        ici_send, ici_recv, d2d_send, d2d_recv, sib_send, sib_recv = refs[2 * n:]
        x, y, c = _position()
        me = 2 * x + y
        others = _other_chips(x, y)
        sent, local = [], []

        def remote(src, dst, sems, idx, to):
            return pltpu.make_async_remote_copy(src_ref=src, dst_ref=dst, send_sem=sems[0].at[idx],
                                                recv_sem=sems[1].at[idx], device_id=to, device_id_type=MESH)

        for a in range(n):
            width = ins[a].shape[axes[a]]
            mine = remote(ins[a], _block(outs[a], axes[a], width, me), (sib_send, sib_recv), a, (x, y, 1 - c))
            mine.start()
            local.append(mine)
            for j, (px, py) in enumerate(others):
                if a < ns:
                    sx = split[a][2]
                    src = _block(ins[a], None, None, None, sx, ins[a].shape[sx], c)
                    dst = _block(outs[a], axes[a], width, me, sx, outs[a].shape[sx], c)
                else:
                    src, dst = ins[a], _block(outs[a], axes[a], width, me)
                cp = remote(src, dst, (ici_send, ici_recv), 3 * a + j, (px, py, c))
                cp.start()
                sent.append(cp)
        for a in range(ns):
            width = ins[a].shape[axes[a]]
            sx = split[a][2]
            for j, (px, py) in enumerate(others):
                theirs = _block(outs[a], axes[a], width, 2 * px + py, sx, outs[a].shape[sx], c)
                remote(theirs, theirs, (ici_send, ici_recv), 3 * a + j, (px, py, c)).wait_recv()
                fwd = remote(theirs, theirs, (d2d_send, d2d_recv), 3 * a + j, (x, y, 1 - c))
                fwd.start()
                sent.append(fwd)
        for a in range(ns, n):
            width = ins[a].shape[axes[a]]
            for j, (px, py) in enumerate(others):
                theirs = _block(outs[a], axes[a], width, 2 * px + py)
                remote(theirs, theirs, (ici_send, ici_recv), 3 * a + j, (px, py, c)).wait_recv()
        for a in range(ns):
            width = ins[a].shape[axes[a]]
            sx = split[a][2]
            for j, (px, py) in enumerate(others):
                from_sibling = _block(outs[a], axes[a], width, 2 * px + py, sx, outs[a].shape[sx], 1 - c)
                remote(from_sibling, from_sibling, (d2d_send, d2d_recv), 3 * a + j, (x, y, 1 - c)).wait_recv()
        for cp in sent:
            cp.wait_send()
        for cp in local:
            cp.wait()

    def whole_shape(a, axis):
        shape = list(a.shape)
        shape[axis] *= N_CHIPS
        return jax.ShapeDtypeStruct(tuple(shape), a.dtype)

    any_spec = pl.BlockSpec(memory_space=pl.ANY)
    return pl.pallas_call(
        body, name="gather_weights",
        in_specs=[any_spec] * n, out_specs=[any_spec] * n,
        out_shape=[whole_shape(a, ax) for a, ax in zip(arrs, axes)],
        scratch_shapes=[pltpu.SemaphoreType.DMA((3 * n,)), pltpu.SemaphoreType.DMA((3 * n,)),
                        pltpu.SemaphoreType.DMA((3 * ns,)), pltpu.SemaphoreType.DMA((3 * ns,)),
                        pltpu.SemaphoreType.DMA((n,)), pltpu.SemaphoreType.DMA((n,))],
        compiler_params=pltpu.CompilerParams(has_side_effects=True),
    )(*arrs)


class _Grad:
    def __init__(self, name, g, kind, rh, cols, groups=None):
        self.name, self.g, self.kind, self.rh, self.cols, self.groups = name, g, kind, rh, cols, groups
        if kind == 'rows':
            self.tr = rh
        elif kind == 'gate':
            self.tr = rh // (groups // 2)
        else:
            self.tr = rh if rh <= 512 else 256
        self.nb = rh // self.tr

    def pieces(self, ref, k, h):
        rh, cols = self.rh, self.cols
        if self.kind == 'cols':
            return [(ref.at[pl.ds(h * rh, rh), pl.ds(k * cols, cols)], 0, rh)]
        if self.kind == 'rows':
            return [(ref.at[pl.ds((2 * k + h) * rh, rh), :], 0, rh)]
        if self.kind == 'lead':
            return [(ref.at[k, pl.ds(h * rh, rh), :], 0, rh)]
        per = self.groups // 2
        return [(ref.at[pl.ds((((h * per + gi) * N_CHIPS) + k) * self.tr, self.tr), :], gi * self.tr, self.tr)
                for gi in range(per)]

    def block_spec(self):
        tr, nb, cols = self.tr, self.nb, self.cols
        if self.kind == 'cols':
            return pl.BlockSpec((tr, cols), lambda k, i, c: (c[0] * nb + i, k))
        if self.kind == 'rows':
            return pl.BlockSpec((tr, cols), lambda k, i, c: (2 * k + c[0], 0))
        if self.kind == 'lead':
            return pl.BlockSpec((None, tr, cols), lambda k, i, c: (k, c[0] * nb + i, 0))
        return pl.BlockSpec((tr, cols), lambda k, i, c: ((c[0] * nb + i) * N_CHIPS + k, 0))


class _Exchange:
    def __init__(self, name, arrays, out_shapes, n_copies, copies):
        self.name, self.arrays, self.out_shapes, self.n_copies, self.copies = name, arrays, out_shapes, n_copies, copies
        self.n_in, self.n_out = len(arrays), len(out_shapes)

    def specs(self, n):
        return [pl.BlockSpec(memory_space=pl.ANY)] * n

    def scratch(self):
        return [pltpu.SemaphoreType.DMA((self.n_copies,)), pltpu.SemaphoreType.DMA((self.n_copies,))]

    def _descriptors(self, in_refs, out_refs, sems):
        return self.copies(in_refs, out_refs, sems[0], sems[1])

    def start(self, in_refs, out_refs, sems):
        for cp in self._descriptors(in_refs, out_refs, sems):
            cp.start()

    def finish(self, in_refs, out_refs, sems):
        for cp in self._descriptors(in_refs, out_refs, sems):
            cp.wait()

    def __add__(self, other):
        def copies(ins, outs, send_sems, recv_sems, base=0):
            return (self.copies(ins[:self.n_in], outs[:self.n_out], send_sems, recv_sems, base)
                    + other.copies(ins[self.n_in:], outs[self.n_out:], send_sems, recv_sems, base + self.n_copies))

        return _Exchange(self.name + "_" + other.name, self.arrays + other.arrays, self.out_shapes + other.out_shapes,
                         self.n_copies + other.n_copies, copies)

    def run(self):
        def body(*refs):
            ins, outs, sems = refs[:self.n_in], refs[self.n_in:self.n_in + self.n_out], refs[self.n_in + self.n_out:]
            self.start(ins, outs, sems)
            self.finish(ins, outs, sems)

        return pl.pallas_call(
            body, name=self.name,
            in_specs=self.specs(self.n_in), out_specs=self.specs(self.n_out), out_shape=self.out_shapes,
            scratch_shapes=self.scratch(),
            compiler_params=pltpu.CompilerParams(has_side_effects=True),
        )(*self.arrays)


def _gather_whole(name, shards):
    def copies(ins, outs, send_sems, recv_sems, base=0):
        x, y, c = _position()
        me = 2 * x + y
        made = []
        for a, (_, axis) in enumerate(shards):
            dst = _block(outs[a], axis, ins[a].shape[axis], me)
            for j, to in enumerate([(x, y, 1 - c)] + [(px, py, c) for px, py in _other_chips(x, y)]):
                idx = base + 4 * a + j
                made.append(pltpu.make_async_remote_copy(
                    src_ref=ins[a], dst_ref=dst, send_sem=send_sems.at[idx], recv_sem=recv_sems.at[idx],
                    device_id=to, device_id_type=MESH))
        return made

    def whole_shape(a, axis):
        shape = list(a.shape)
        shape[axis] *= N_CHIPS
        return jax.ShapeDtypeStruct(tuple(shape), a.dtype)

    return _Exchange(name, [s[0] for s in shards], [whole_shape(*s) for s in shards], 4 * len(shards), copies)


def _halves_to_sibling(name, grads):
    total = sum(len(gr.pieces(gr.g, 0, 0)) * N_CHIPS for gr in grads)

    def copies(ins, outs, send_sems, recv_sems, base=0):
        x, y, c = _position()
        made = []
        for gr, g_ref, got_ref in zip(grads, ins, outs):
            for k in range(N_CHIPS):
                for src, r0, nr in gr.pieces(g_ref, k, 1 - c):
                    idx = base + len(made)
                    made.append(pltpu.make_async_remote_copy(
                        src_ref=src, dst_ref=got_ref.at[k, pl.ds(r0, nr), :],
                        send_sem=send_sems.at[idx], recv_sem=recv_sems.at[idx],
                        device_id=(x, y, 1 - c), device_id_type=MESH))
        return made

    return _Exchange(name, [gr.g for gr in grads],
                     [jax.ShapeDtypeStruct((N_CHIPS, gr.rh, gr.cols), F32) for gr in grads], total, copies)


def _chip_sum(gr, got, c, wire=BF16):
    def body(c_ref, g_ref, got_ref, o_ref):
        o_ref[...] = (g_ref[...] + got_ref[...]).astype(wire)

    tile = pl.BlockSpec((None, gr.tr, gr.cols), lambda k, i, c_ref: (k, i, 0))
    return pl.pallas_call(
        body, name="chip_sum_" + gr.name,
        grid_spec=pltpu.PrefetchScalarGridSpec(
            num_scalar_prefetch=1, grid=(N_CHIPS, gr.nb),
            in_specs=[gr.block_spec(), tile], out_specs=tile),
        out_shape=jax.ShapeDtypeStruct((N_CHIPS, gr.rh, gr.cols), wire),
        compiler_params=_params(("parallel", "parallel")),
    )(c, gr.g, got)


def _blocks_to_chips(name, parts):
    n = len(parts)

    def copies(ins, outs, send_sems, recv_sems, base=0):
        x, y, c = _position()
        made = []
        for a in range(n):
            for j, (px, py) in enumerate(_other_chips(x, y)):
                idx = base + 3 * a + j
                made.append(pltpu.make_async_remote_copy(
                    src_ref=ins[a].at[2 * px + py], dst_ref=outs[a].at[j],
                    send_sem=send_sems.at[idx], recv_sem=recv_sems.at[idx],
                    device_id=(px, py, c), device_id_type=MESH))
        return made

    return _Exchange(name, parts, [jax.ShapeDtypeStruct((3,) + p.shape[1:], p.dtype) for p in parts], 3 * n, copies)


def _sum_chips(name, part, got, me, c):
    nk, rh, cols = part.shape
    tr = rh if rh <= 512 else 256
    nb = rh // tr

    def body(me_ref, c_ref, own_ref, *rest):
        got_refs, o_ref = rest[:nk], rest[nk]
        own = own_ref[...].astype(F32)
        acc = None
        for k in range(nk):
            term = jnp.where(me_ref[0] == k, own, got_refs[k][...].astype(F32))
            acc = term if acc is None else acc + term
        o_ref[...] = acc

    def got_map(k):
        def index(i, me_ref, c_ref):
            xor = jnp.bitwise_xor(me_ref[0], k)
            slot = jnp.where(xor == 1, 1, jnp.where(xor == 3, 2, 0))
            return (slot, i, 0)
        return index

    return pl.pallas_call(
        body, name="sum_" + name,
        grid_spec=pltpu.PrefetchScalarGridSpec(
            num_scalar_prefetch=2, grid=(nb,),
            in_specs=[pl.BlockSpec((None, tr, cols), lambda i, me_ref, c_ref: (me_ref[0], i, 0))]
            + [pl.BlockSpec((None, tr, cols), got_map(k)) for k in range(nk)],
            out_specs=pl.BlockSpec((tr, cols), lambda i, me_ref, c_ref: (c_ref[0] * nb + i, 0))),
        out_shape=jax.ShapeDtypeStruct((2 * rh, cols), F32),
        compiler_params=_params(("parallel",)),
    )(me, c, part, *([got] * nk))


def _share_with_sibling(halves):
    n = len(halves)

    def body(*refs):
        outs = refs[n:2 * n]
        send_sems, recv_sems = refs[2 * n:]
        x, y, c = _position()
        copies = []
        for a in range(n):
            rh = outs[a].shape[0] // 2
            mine = outs[a].at[pl.ds(c * rh, rh), :]
            cp = pltpu.make_async_remote_copy(
                src_ref=mine, dst_ref=mine, send_sem=send_sems.at[a], recv_sem=recv_sems.at[a],
                device_id=(x, y, 1 - c), device_id_type=MESH)
            cp.start()
            copies.append(cp)
        for cp in copies:
            cp.wait()

    any_spec = pl.BlockSpec(memory_space=pl.ANY)
    return pl.pallas_call(
        body, name="grads_share",
        in_specs=[any_spec] * n, out_specs=[any_spec] * n,
        out_shape=[jax.ShapeDtypeStruct(h.shape, h.dtype) for h in halves],
        input_output_aliases={a: a for a in range(n)},
        scratch_shapes=[pltpu.SemaphoreType.DMA((n,)), pltpu.SemaphoreType.DMA((n,))],
        compiler_params=pltpu.CompilerParams(has_side_effects=True),
    )(*halves)


def _adamw(name, w, g, m, v):
    rows, cols = w.shape
    tr = 256 if rows % 256 == 0 else rows

    def body(w_ref, g_ref, m_ref, v_ref, d_ref, nm_ref, nv_ref):
        gv = g_ref[...]
        mn = ADAM_B1 * m_ref[...] + (1.0 - ADAM_B1) * gv
        vn = ADAM_B2 * v_ref[...] + (1.0 - ADAM_B2) * (gv * gv)
        m_hat = mn / (1.0 - ADAM_B1 ** ADAM_STEP)
        v_hat = vn / (1.0 - ADAM_B2 ** ADAM_STEP)
        d_ref[...] = -ADAM_LR * (m_hat / (jnp.sqrt(v_hat) + ADAM_EPS) + ADAM_WD * w_ref[...])
        nm_ref[...] = mn
        nv_ref[...] = vn

    spec = pl.BlockSpec((tr, cols), lambda i: (i, 0))
    return pl.pallas_call(
        body, name=name, grid=(rows // tr,),
        in_specs=[spec] * 4, out_specs=[spec] * 3,
        out_shape=[jax.ShapeDtypeStruct((rows, cols), F32)] * 3,
        compiler_params=_params(("parallel",)),
    )(w, g, m, v)


def _as2d(a):
    if a.ndim == 1:
        return a.reshape(1, -1)
    return a.reshape(-1, a.shape[-1])


def _unshard(gathered, axis):
    moved = jnp.moveaxis(gathered, 0, axis)
    shape = list(gathered.shape[1:])
    shape[axis] *= N_CHIPS
    return moved.reshape(shape)


def _rope_tables(tp, pad):
    pos = jnp.arange(tp, dtype=F32) - pad
    inv_freq = ROPE_BASE ** (-jnp.arange(0, QK_ROPE, 2, dtype=F32) / QK_ROPE)
    ang = pos[:, None] * inv_freq[None, :]
    cos, sin = jnp.cos(ang), jnp.sin(ang)
    zeros = jnp.zeros((tp, LANES - QK_ROPE), F32)
    return jnp.concatenate([cos, cos, zeros], axis=1), jnp.concatenate([-sin, sin, zeros], axis=1)


def _matrix_grad(name, g, heads):
    if name in ('b_w_rg', 'b_w_ig'):
        groups, blk, cols = g.shape
        return _Grad(name, g.reshape(groups * blk, cols), 'gate', (groups // 2) * (blk // N_CHIPS), cols, groups)
    rows, cols = g.shape
    if name in ('a_w_out', 'b_w_out'):
        return _Grad(name, g, 'rows', rows // (2 * N_CHIPS), cols)
    if name == 'a_w_uq' and heads % N_CHIPS != 0:
        g = g.reshape(rows, heads, HEAD_W)[:, :, :QK_NOPE + QK_ROPE].reshape(rows, -1)
        cols = g.shape[1]
    if name == 'a_w_in' or (name == 'a_w_uq' and heads % N_CHIPS != 0):
        g = jnp.moveaxis(g.reshape(rows, N_CHIPS, cols // N_CHIPS), 1, 0)
        return _Grad(name, g, 'lead', rows // 2, cols // N_CHIPS)
    return _Grad(name, g, 'cols', rows // 2, cols // N_CHIPS)


def _kernel_form(name, w):
    return w[0] if name in ('b_w_rg', 'b_w_ig', 'b_conv_w') else _as2d(w)


def _local_grads(x, target, wt, heads, c_idx, mid_names, mid_gather, late_names, late_gather):
    wt = dict(wt)
    seq, d = x.shape
    n_meta = wt['meta_tokens'].shape[0]
    t = seq + n_meta
    pad = (-t) % Q_BLOCK
    tp = t + pad
    x0 = pad + n_meta
    tm = _row_tile(tp)
    ql = wt['a_q_norm_g'].shape[1]
    kvl = wt['a_kv_norm_g'].shape[1]
    mla_w = heads * V_HEAD

    h0 = jnp.concatenate([jnp.zeros((pad, d), F32), wt['meta_tokens'], x], axis=0)
    cos_t, sin_t = _rope_tables(tp, pad)

    w_in_a = wt['a_w_in']
    zcol = jnp.zeros((d, LANES - QK_ROPE), BF16)
    w_in_a = jnp.concatenate([w_in_a[:, :ql + kvl + QK_ROPE], zcol, w_in_a[:, ql + kvl + QK_ROPE:]], axis=1)
    c_kv, c_kr, c_gate = ql, ql + kvl, ql + kvl + LANES
    splits_a = [(0, c_kv), (c_kv, c_kr), (c_kr, c_gate), (c_gate, c_gate + mla_w)]

    q_lat, kv_lat, kr_raw, gate_a, *mid_whole = _norm_matmul("a_in_proj", h0, wt['a_norm_g'], w_in_a, splits_a, tm,
                                                             exchange=mid_gather)
    wt.update({n: _kernel_form(n, w) for n, w in zip(mid_names, mid_whole)})
    w_uq = wt['a_w_uq'].reshape(ql, heads, QK_NOPE + QK_ROPE)
    w_uq = jnp.pad(w_uq, ((0, 0), (0, 0), (0, HEAD_W - QK_NOPE - QK_ROPE))).reshape(ql, heads * HEAD_W)
    w_ukv = wt['a_w_ukv']
    q = _q_proj(q_lat, wt['a_q_norm_g'], w_uq, cos_t, sin_t, heads, tm)
    k, v = _kv_proj(kv_lat, wt['a_kv_norm_g'], w_ukv, kr_raw, cos_t, sin_t, heads, tm)
    attn, lse, *late_whole = _flash_fwd(q, k, v, heads, pad, tm, exchange=late_gather)
    wt.update({n: _kernel_form(n, w) for n, w in zip(late_names, late_whole)})
    lru_w = wt['b_conv_w'].shape[1]
    h1 = _gated_out("a_out_proj", attn, gate_a, wt['a_w_out'], h0, tm)

    u, gate_b = _norm_matmul("b_in_proj", h1, wt['b_norm_g'], wt['b_w_in'], [(0, lru_w), (lru_w, 2 * lru_w)], tm)
    uc, r, ig, hs, decay, mult = _rglru_fwd(u, wt['b_conv_w'], wt['b_conv_b'], wt['b_w_rg'], wt['b_b_rg'],
                                            wt['b_w_ig'], wt['b_b_ig'], wt['b_lam'], pad, tm)
    h2 = _gated_out("b_out_proj", hs, gate_b, wt['b_w_out'], h1, tm)

    dh2, loss, d_final_g = _final_loss(h2, wt['final_norm_g'], target, x0, tm)

    dhs, dgate_b, dw_out_b = _gated_out_bwd("b_out_proj_bwd", dh2, hs, gate_b, wt['b_w_out'], tm)
    du, dconv_w, dconv_b, dw_rg, db_rg, dw_ig, db_ig, dlam = _rglru_bwd(
        dhs, hs, r, ig, uc, u, decay, mult, wt['b_conv_w'], wt['b_w_rg'], wt['b_w_ig'], wt['b_lam'], pad, tm)
    dh1, dw_in_b, dg_b = _norm_matmul_bwd("b_in_proj_bwd", h1, wt['b_norm_g'], wt['b_w_in'], [du, dgate_b], tm, resid=dh2)

    grads_b = [_matrix_grad(n, g, heads) for n, g in
               (('b_w_in', dw_in_b), ('b_w_rg', dw_rg), ('b_w_ig', dw_ig), ('b_w_out', dw_out_b))]
    dattn, dgate_a, dw_out_a, delta, *got = _gated_out_bwd(
        "a_out_proj_bwd", dh1, attn, gate_a, wt['a_w_out'], tm, delta_heads=heads,
        exchange=_halves_to_sibling("swap_b", grads_b))
    sums_b = [_chip_sum(gr, r, c_idx) for gr, r in zip(grads_b, got)]
    grad_out = _matrix_grad('a_w_out', dw_out_a, heads)
    dq, dk, dv, *landed = _flash_bwd(
        q, k, v, lse, delta, dattn, heads, pad, tm,
        exchange=_blocks_to_chips("chips_b", sums_b) + _halves_to_sibling("swap_out", [grad_out]))
    through = list(zip(grads_b, sums_b, landed[:len(grads_b)]))
    sum_out = _chip_sum(grad_out, landed[len(grads_b)], c_idx)

    def q_prologue(dy_refs, dy_s, ex_ref):
        (dq_ref,), cos_v, sin_v = dy_refs[:1], dy_refs[1][...], dy_refs[2][...]
        for h in range(heads):
            c0 = h * HEAD_W
            dy_s[:, c0:c0 + QK_NOPE] = dq_ref[:, c0:c0 + QK_NOPE].astype(BF16)
            dy_s[:, c0 + QK_NOPE:c0 + HEAD_W] = _unrope(dq_ref[:, c0 + QK_NOPE:c0 + HEAD_W], cos_v, sin_v).astype(BF16)

    dq_lat, dw_uq, dg_q, from_chips_out = _norm_matmul_bwd(
        "a_q_proj_bwd", q_lat, wt['a_q_norm_g'], w_uq, [dq, cos_t, sin_t], tm, prologue=q_prologue,
        exchange=_blocks_to_chips("chips_out", [sum_out]))
    through.append((grad_out, sum_out, from_chips_out))
    grad_uq = _matrix_grad('a_w_uq', dw_uq, heads)

    def kv_prologue(dy_refs, dy_s, ex_ref):
        dk_ref, dv_ref = dy_refs[:2]
        cos_v, sin_v = dy_refs[2][...], dy_refs[3][...]
        dkr = jnp.zeros((dk_ref.shape[0], LANES), F32)
        for h in range(heads):
            c0 = h * (QK_NOPE + V_HEAD)
            dy_s[:, c0:c0 + QK_NOPE] = dk_ref[:, h * HEAD_W:h * HEAD_W + QK_NOPE].astype(BF16)
            dy_s[:, c0 + QK_NOPE:c0 + QK_NOPE + V_HEAD] = dv_ref[:, h * V_HEAD:(h + 1) * V_HEAD].astype(BF16)
            dkr = dkr + dk_ref[:, h * HEAD_W + QK_NOPE:(h + 1) * HEAD_W]
        ex_ref[...] = _unrope(dkr, cos_v, sin_v)

    dkv_lat, dw_ukv, dg_kv, dkr_raw, got_uq = _norm_matmul_bwd(
        "a_kv_proj_bwd", kv_lat, wt['a_kv_norm_g'], w_ukv, [dk, dv, cos_t, sin_t], tm,
        prologue=kv_prologue, extra_out=(LANES, F32), exchange=_halves_to_sibling("swap_uq", [grad_uq]))
    sum_uq = _chip_sum(grad_uq, got_uq, c_idx)
    grad_ukv = _matrix_grad('a_w_ukv', dw_ukv, heads)

    dh0, dw_in_a, dg_a, from_chips_uq, got_ukv = _norm_matmul_bwd(
        "a_in_proj_bwd", h0, wt['a_norm_g'], w_in_a, [dq_lat, dkv_lat, dkr_raw, dgate_a], tm, resid=dh1,
        exchange=_blocks_to_chips("chips_uq", [sum_uq]) + _halves_to_sibling("swap_ukv", [grad_ukv]))
    through.append((grad_uq, sum_uq, from_chips_uq))
    swapped = [(grad_ukv, _chip_sum(grad_ukv, got_ukv, c_idx))]

    dw_in_a = jnp.concatenate([dw_in_a[:, :c_kr + QK_ROPE], dw_in_a[:, c_gate:]], axis=1)
    grads = {
        'meta_tokens': dh0[pad:x0], 'a_norm_g': dg_a, 'a_w_in': dw_in_a, 'a_q_norm_g': dg_q, 'a_kv_norm_g': dg_kv,
        'a_w_uq': dw_uq, 'a_w_ukv': dw_ukv, 'a_w_out': dw_out_a, 'b_norm_g': dg_b, 'b_w_in': dw_in_b,
        'b_conv_w': dconv_w, 'b_conv_b': dconv_b, 'b_w_rg': dw_rg, 'b_b_rg': db_rg, 'b_w_ig': dw_ig,
        'b_b_ig': db_ig, 'b_lam': dlam, 'b_w_out': dw_out_b, 'final_norm_g': d_final_g,
    }
    return loss, dh0[x0:], grads, through, swapped


def _chip_major(whole, local_shape, axis):
    if axis is None:
        return jnp.broadcast_to(whole.reshape(1, -1), (N_CHIPS, whole.size))
    shape = list(local_shape)
    g = whole.reshape(shape[:axis] + [N_CHIPS, shape[axis]] + shape[axis + 1:])
    return jnp.moveaxis(g, axis, 0).reshape(N_CHIPS, -1)


def kernel(x, meta_tokens, a_norm_g, a_w_in, a_q_norm_g, a_kv_norm_g, a_w_uq, a_w_ukv, a_w_out, b_norm_g, b_w_in, b_conv_w, b_conv_b, b_w_rg, b_b_rg, b_w_ig, b_b_ig, b_lam, b_w_out, final_norm_g, loss_target, m_meta_tokens, m_a_norm_g, m_a_w_in, m_a_q_norm_g, m_a_kv_norm_g, m_a_w_uq, m_a_w_ukv, m_a_w_out, m_b_norm_g, m_b_w_in, m_b_conv_w, m_b_conv_b, m_b_w_rg, m_b_b_rg, m_b_w_ig, m_b_b_ig, m_b_lam, m_b_w_out, m_final_norm_g, v_meta_tokens, v_a_norm_g, v_a_w_in, v_a_q_norm_g, v_a_kv_norm_g, v_a_w_uq, v_a_w_ukv, v_a_w_out, v_b_norm_g, v_b_w_in, v_b_conv_w, v_b_conv_b, v_b_w_rg, v_b_b_rg, v_b_w_ig, v_b_b_ig, v_b_lam, v_b_w_out, v_final_norm_g):
    local_w = dict(zip(WEIGHTS, (meta_tokens, a_norm_g, a_w_in, a_q_norm_g, a_kv_norm_g, a_w_uq, a_w_ukv, a_w_out,
                                 b_norm_g, b_w_in, b_conv_w, b_conv_b, b_w_rg, b_b_rg, b_w_ig, b_b_ig, b_lam,
                                 b_w_out, final_norm_g)))
    local_m = dict(zip(WEIGHTS, (m_meta_tokens, m_a_norm_g, m_a_w_in, m_a_q_norm_g, m_a_kv_norm_g, m_a_w_uq,
                                 m_a_w_ukv, m_a_w_out, m_b_norm_g, m_b_w_in, m_b_conv_w, m_b_conv_b, m_b_w_rg,
                                 m_b_b_rg, m_b_w_ig, m_b_b_ig, m_b_lam, m_b_w_out, m_final_norm_g)))
    local_v = dict(zip(WEIGHTS, (v_meta_tokens, v_a_norm_g, v_a_w_in, v_a_q_norm_g, v_a_kv_norm_g, v_a_w_uq,
                                 v_a_w_ukv, v_a_w_out, v_b_norm_g, v_b_w_in, v_b_conv_w, v_b_conv_b, v_b_w_rg,
                                 v_b_b_rg, v_b_w_ig, v_b_b_ig, v_b_lam, v_b_w_out, v_final_norm_g)))
    matrices = ('a_w_in', 'a_w_uq', 'a_w_ukv', 'a_w_out', 'b_w_in', 'b_w_rg', 'b_w_ig', 'b_w_out')
    heads = a_w_uq.shape[-1] * N_CHIPS // (QK_NOPE + QK_ROPE)

    split, small, mid, late = [], [], [], []
    for n in WEIGHTS:
        if SHARD_AXIS[n] is None:
            continue
        if n.startswith('b_') or n == 'a_w_out':
            late.append((n, local_w[n].astype(BF16) if n in matrices else local_w[n], SHARD_AXIS[n]))
        elif n == 'a_w_in':
            split.append((n, local_w[n].astype(BF16)[None], 0, 2))
        elif n in matrices:
            mid.append((n, local_w[n].astype(BF16), SHARD_AXIS[n]))
        else:
            small.append((n, local_w[n], SHARD_AXIS[n]))
    gathered = _gather_weights([s[1:] for s in split], [s[1:] for s in small])
    whole = dict(zip([s[0] for s in split + small], gathered))
    whole['a_w_in'] = _unshard(whole['a_w_in'], SHARD_AXIS['a_w_in'])
    mid_names, late_names = [s[0] for s in mid], [s[0] for s in late]
    wt = {n: _kernel_form(n, whole.get(n, local_w[n])) for n in WEIGHTS if n not in mid_names + late_names}

    c_idx = lax.axis_index("c").astype(jnp.int32).reshape(1)
    me_idx = (2 * lax.axis_index("x") + lax.axis_index("y")).astype(jnp.int32).reshape(1)
    loss, grad_x, grads, through, swapped = _local_grads(
        x[0], loss_target[0], wt, heads, c_idx,
        mid_names, _gather_whole("gather_weights_a", [s[1:] for s in mid]),
        late_names, _gather_whole("gather_weights_b", [s[1:] for s in late]))

    ext_uq = heads % N_CHIPS == 0
    started = [gr.name for gr, *_ in through + swapped]
    last = [_matrix_grad(n, grads[n], heads) for n in matrices if n not in started]
    rest = [n for n in WEIGHTS if n not in matrices]
    pieces = [_chip_major(grads[n], local_w[n].shape, SHARD_AXIS[n]) for n in rest]
    pieces.append(jnp.broadcast_to(loss[0:1, 0:1], (N_CHIPS, 1)))
    length = sum(p.shape[1] for p in pieces)
    unit = 2 * SUBLANES * 1024
    padded = -(-length // unit) * unit
    flat = jnp.concatenate(pieces + [jnp.zeros((N_CHIPS, padded - length), F32)], axis=1)
    last.append(_Grad('small', flat.reshape(N_CHIPS, padded // 1024, 1024), 'lead', padded // 2048, 1024))

    got = _halves_to_sibling("grads_to_sibling", last).run()
    swapped = swapped + [(gr, _chip_sum(gr, r, c_idx, F32 if gr.name == 'small' else BF16))
                         for gr, r in zip(last, got)]
    from_chips = _blocks_to_chips("grads_to_chips", [p for _, p in swapped]).run()
    through = through + [(gr, p, r) for (gr, p), r in zip(swapped, from_chips)]
    halves = [_sum_chips(gr.name, p, r, me_idx, c_idx) for gr, p, r in through]
    summed = dict(zip([gr.name for gr, _, _ in through], _share_with_sibling(halves)))
    if ext_uq:
        g = summed['a_w_uq']
        summed['a_w_uq'] = g.reshape(g.shape[0], -1, HEAD_W)[:, :, :QK_NOPE + QK_ROPE]
    total = summed['small'].reshape(-1)

    out_g, out_d, out_m, out_v = [], [], [], []
    off = 0
    for n in WEIGHTS:
        shape = local_w[n].shape
        if n in matrices:
            g = summed[n].reshape(shape)
        else:
            size = 1
            for s in shape:
                size *= s
            g = total[off:off + size].reshape(shape)
            off += size
        delta, new_m, new_v = _adamw("adamw_" + n, _as2d(local_w[n]), _as2d(g), _as2d(local_m[n]), _as2d(local_v[n]))
        out_g.append(g)
        out_d.append(delta.reshape(shape))
        out_m.append(new_m.reshape(shape))
        out_v.append(new_v.reshape(shape))

    return (total[off], grad_x[None], *out_g, *out_d, *out_m, *out_v)
```

```python
import functools

import jax
import jax.numpy as jnp
from jax import lax
from jax.experimental import pallas as pl
from jax.experimental.pallas import tpu as pltpu

F32 = jnp.float32
BF16 = jnp.bfloat16
MESH = pl.DeviceIdType.MESH

RMS_EPS = 1e-6
QK_NOPE = 128
QK_ROPE = 64
V_HEAD = 128
HEAD_W = 256
ROPE_BASE = 10000.0
Q_BLOCK = 128
MASK_VALUE = -1e30
CONV_WIDTH = 4
LRU_C = 8.0
N_CHIPS = 4

ADAM_LR = 0.001
ADAM_B1 = 0.9
ADAM_B2 = 0.999
ADAM_EPS = 1e-08
ADAM_WD = 0.01
ADAM_STEP = 10

VMEM_LIMIT_V7X = 56 * 1024 * 1024
LANES = 128
SUBLANES = 8

WEIGHTS = ['meta_tokens', 'a_norm_g', 'a_w_in', 'a_q_norm_g', 'a_kv_norm_g', 'a_w_uq', 'a_w_ukv',
           'a_w_out', 'b_norm_g', 'b_w_in', 'b_conv_w', 'b_conv_b', 'b_w_rg', 'b_b_rg', 'b_w_ig',
           'b_b_ig', 'b_lam', 'b_w_out', 'final_norm_g']
SHARD_AXIS = {'meta_tokens': 1, 'a_norm_g': None, 'a_w_in': 2, 'a_q_norm_g': None, 'a_kv_norm_g': None,
              'a_w_uq': 2, 'a_w_ukv': 2, 'a_w_out': 1, 'b_norm_g': 1, 'b_w_in': 2, 'b_conv_w': 2,
              'b_conv_b': 1, 'b_w_rg': 2, 'b_b_rg': 1, 'b_w_ig': 2, 'b_b_ig': 1, 'b_lam': 1,
              'b_w_out': 1, 'final_norm_g': None}


def _params(sem=None):
    return pltpu.CompilerParams(dimension_semantics=sem, vmem_limit_bytes=VMEM_LIMIT_V7X)


def _row_tile(tp):
    return 384 if (tp % 384 == 0 and tp >= 1152) else 128


def _sigmoid(x):
    return 1.0 / (1.0 + jnp.exp(-x))


def _rms(x):
    return lax.rsqrt(jnp.mean(x * x, axis=-1, keepdims=True) + RMS_EPS)


def _swap32(x):
    lane = lax.broadcasted_iota(jnp.int32, x.shape, 1)
    return jnp.where(lane < 32, pltpu.roll(x, 96, 1), pltpu.roll(x, 32, 1))


def _rope(x, cos_t, sin_t):
    return x * cos_t + _swap32(x) * sin_t


def _unrope(d, cos_t, sin_t):
    lane = lax.broadcasted_iota(jnp.int32, d.shape, 1)
    return jnp.where(lane < QK_ROPE, d * cos_t + _swap32(d * sin_t), 0.0)


def _dot(a, b):
    return jnp.dot(a, b, preferred_element_type=F32)


def _dot_nt(a, b):
    return lax.dot_general(a, b, (((1,), (1,)), ((), ())), preferred_element_type=F32)


def _dot_tn(a, b):
    return lax.dot_general(a, b, (((0,), (0,)), ((), ())), preferred_element_type=F32)


def _norm_matmul(name, x, g, w, splits, tm, exchange=None):
    tp, kin = x.shape
    n = w.shape[1]
    nt = tp // tm

    def compute(x_ref, g_ref, w_ref, *outs):
        xv = x_ref[...]
        nrm = ((xv * _rms(xv)) * g_ref[...]).astype(BF16)
        y = _dot(nrm, w_ref[...])
        for o_ref, (c0, c1) in zip(outs, splits):
            o_ref[...] = y[:, c0:c1]

    in_specs = [pl.BlockSpec((tm, kin), lambda i: (i, 0)),
                pl.BlockSpec((1, kin), lambda i: (0, 0)),
                pl.BlockSpec((kin, n), lambda i: (0, 0))]
    out_specs = [pl.BlockSpec((tm, c1 - c0), lambda i: (i, 0)) for c0, c1 in splits]
    out_shape = [jax.ShapeDtypeStruct((tp, c1 - c0), F32) for c0, c1 in splits]
    args, scratch = [x, g, w], []
    body = _with_exchange(exchange, len(in_specs), len(out_specs),
                          lambda: pl.program_id(0) == 0, lambda: pl.program_id(0) == nt - 1, compute)
    if exchange is not None:
        in_specs = in_specs + exchange.specs(exchange.n_in)
        out_specs = out_specs + exchange.specs(exchange.n_out)
        out_shape = out_shape + exchange.out_shapes
        args, scratch = args + exchange.arrays, exchange.scratch()
    return pl.pallas_call(
        body, name=name, grid=(nt,),
        in_specs=in_specs, out_specs=out_specs, out_shape=out_shape, scratch_shapes=scratch,
        compiler_params=_params(("arbitrary",)),
    )(*args)


def _q_proj(q_lat, g, w_uq, cos_t, sin_t, heads, tm):
    tp, kin = q_lat.shape
    n = heads * HEAD_W

    def body(x_ref, g_ref, w_ref, cos_ref, sin_ref, q_ref):
        xv = x_ref[...]
        nrm = ((xv * _rms(xv)) * g_ref[...]).astype(BF16)
        y = _dot(nrm, w_ref[...])
        cos_v, sin_v = cos_ref[...], sin_ref[...]
        for h in range(heads):
            c0 = h * HEAD_W
            q_ref[:, c0:c0 + QK_NOPE] = y[:, c0:c0 + QK_NOPE].astype(BF16)
            q_ref[:, c0 + QK_NOPE:c0 + HEAD_W] = _rope(y[:, c0 + QK_NOPE:c0 + HEAD_W], cos_v, sin_v).astype(BF16)

    return pl.pallas_call(
        body, name="a_q_proj", grid=(tp // tm,),
        in_specs=[pl.BlockSpec((tm, kin), lambda i: (i, 0)),
                  pl.BlockSpec((1, kin), lambda i: (0, 0)),
                  pl.BlockSpec((kin, n), lambda i: (0, 0)),
                  pl.BlockSpec((tm, LANES), lambda i: (i, 0)),
                  pl.BlockSpec((tm, LANES), lambda i: (i, 0))],
        out_specs=pl.BlockSpec((tm, n), lambda i: (i, 0)),
        out_shape=jax.ShapeDtypeStruct((tp, n), BF16),
        compiler_params=_params(("parallel",)),
    )(q_lat, g, w_uq, cos_t, sin_t)


def _kv_proj(kv_lat, g, w_ukv, k_rope_raw, cos_t, sin_t, heads, tm):
    tp, kin = kv_lat.shape
    n = heads * (QK_NOPE + V_HEAD)

    def body(x_ref, g_ref, w_ref, kr_ref, cos_ref, sin_ref, k_ref, v_ref):
        xv = x_ref[...]
        nrm = ((xv * _rms(xv)) * g_ref[...]).astype(BF16)
        y = _dot(nrm, w_ref[...])
        kr = _rope(kr_ref[...], cos_ref[...], sin_ref[...]).astype(BF16)
        for h in range(heads):
            c0 = h * (QK_NOPE + V_HEAD)
            k_ref[:, h * HEAD_W:h * HEAD_W + QK_NOPE] = y[:, c0:c0 + QK_NOPE].astype(BF16)
            k_ref[:, h * HEAD_W + QK_NOPE:(h + 1) * HEAD_W] = kr
            v_ref[:, h * V_HEAD:(h + 1) * V_HEAD] = y[:, c0 + QK_NOPE:c0 + QK_NOPE + V_HEAD].astype(BF16)

    return pl.pallas_call(
        body, name="a_kv_proj", grid=(tp // tm,),
        in_specs=[pl.BlockSpec((tm, kin), lambda i: (i, 0)),
                  pl.BlockSpec((1, kin), lambda i: (0, 0)),
                  pl.BlockSpec((kin, n), lambda i: (0, 0)),
                  pl.BlockSpec((tm, LANES), lambda i: (i, 0)),
                  pl.BlockSpec((tm, LANES), lambda i: (i, 0)),
                  pl.BlockSpec((tm, LANES), lambda i: (i, 0))],
        out_specs=[pl.BlockSpec((tm, heads * HEAD_W), lambda i: (i, 0)),
                   pl.BlockSpec((tm, heads * V_HEAD), lambda i: (i, 0))],
        out_shape=[jax.ShapeDtypeStruct((tp, heads * HEAD_W), BF16),
                   jax.ShapeDtypeStruct((tp, heads * V_HEAD), BF16)],
        compiler_params=_params(("parallel",)),
    )(kv_lat, g, w_ukv, k_rope_raw, cos_t, sin_t)


def _as_rows(col):
    rows = col.shape[0]
    return jnp.transpose(jnp.broadcast_to(col, (rows, LANES)))[0:SUBLANES, :]


def _attn_mask(row0, col0, rows, cols, pad):
    row = row0 + lax.broadcasted_iota(jnp.int32, (rows, cols), 0)
    col = col0 + lax.broadcasted_iota(jnp.int32, (rows, cols), 1)
    return (col <= row) & (col >= pad)


LOG2E = 1.4426950408889634
FLASH_FWD_TRIPS = ((4, 2), (2, 2), (1, 1))


def _flash_fwd(q, k, v, heads, pad, tq, exchange=None):
    tp = q.shape[0]
    nq = tp // tq
    c2 = (QK_NOPE + QK_ROPE) ** -0.5 * LOG2E

    def compute(q_ref, k_ref, v_ref, o_ref, lse_ref):
        i = pl.program_id(1)

        def make_step(masked, blocks, parts=1):
            keys = blocks * tq // parts

            def step(j, carry):
                m, l, acc = carry
                offs = [pl.multiple_of(j * tq + part * keys, tq) for part in range(parts)]
                scores = [_dot_nt(q_ref[...], k_ref[pl.ds(off, keys), :]) for off in offs]
                for off, s in zip(offs, scores):
                    s = s * c2
                    if masked:
                        s = jnp.where(_attn_mask(i * tq, j * tq, tq, keys, pad), s, MASK_VALUE)
                    m_new = jnp.maximum(m, jnp.max(s, axis=-1, keepdims=True))
                    p = jnp.exp2(s - m_new)
                    alpha = jnp.exp2(m - m_new)
                    l = alpha * l + jnp.sum(p, axis=-1, keepdims=True)
                    acc = alpha * acc + _dot(p.astype(BF16), v_ref[pl.ds(off, keys), :])
                    m = m_new
                return m, l, acc
            return step

        init = (jnp.full((tq, 1), MASK_VALUE, F32), jnp.zeros((tq, 1), F32), jnp.zeros((tq, V_HEAD), F32))
        carry = make_step(True, 1)(0, init)
        first = 1
        for blocks, parts in FLASH_FWD_TRIPS:
            trips = jnp.maximum(i - first, 0) // blocks
            step_n = make_step(False, blocks, parts)
            carry = lax.fori_loop(0, trips, lambda t, cr, f=first, b=blocks, s=step_n: s(f + b * t, cr), carry)
            first = first + blocks * trips
        m, l, acc = lax.fori_loop(jnp.maximum(i, 1), i + 1, make_step(True, 1), carry)
        o_ref[...] = acc / l
        lse_ref[...] = _as_rows(m + jnp.log(l) * LOG2E)

    in_specs = [pl.BlockSpec((tq, HEAD_W), lambda h, i: (i, h)),
                pl.BlockSpec((tp, HEAD_W), lambda h, i: (0, h)),
                pl.BlockSpec((tp, V_HEAD), lambda h, i: (0, h))]
    out_specs = [pl.BlockSpec((tq, V_HEAD), lambda h, i: (i, h)),
                 pl.BlockSpec((None, None, SUBLANES, tq), lambda h, i: (h, i, 0, 0))]
    out_shape = [jax.ShapeDtypeStruct((tp, heads * V_HEAD), F32),
                 jax.ShapeDtypeStruct((heads, nq, SUBLANES, tq), F32)]
    args, scratch = [q, k, v], []
    body = _with_exchange(exchange, len(in_specs), len(out_specs),
                          lambda: (pl.program_id(0) == 0) & (pl.program_id(1) == 0),
                          lambda: (pl.program_id(0) == heads - 1) & (pl.program_id(1) == nq - 1), compute)
    if exchange is not None:
        in_specs = in_specs + exchange.specs(exchange.n_in)
        out_specs = out_specs + exchange.specs(exchange.n_out)
        out_shape = out_shape + exchange.out_shapes
        args, scratch = args + exchange.arrays, exchange.scratch()
    return pl.pallas_call(
        body, name="a_flash_fwd", grid=(heads, nq),
        in_specs=in_specs, out_specs=out_specs, out_shape=out_shape, scratch_shapes=scratch,
        compiler_params=_params(("arbitrary", "arbitrary")),
    )(*args)


def _gated_out(name, a, gate, w, resid, tm):
    tp, wd = a.shape
    d = w.shape[1]

    def body(a_ref, gate_ref, w_ref, res_ref, o_ref):
        gv = gate_ref[...]
        y = (a_ref[...] * (gv * _sigmoid(gv))).astype(BF16)
        o_ref[...] = res_ref[...] + _dot(y, w_ref[...])

    return pl.pallas_call(
        body, name=name, grid=(tp // tm,),
        in_specs=[pl.BlockSpec((tm, wd), lambda i: (i, 0)),
                  pl.BlockSpec((tm, wd), lambda i: (i, 0)),
                  pl.BlockSpec((wd, d), lambda i: (0, 0)),
                  pl.BlockSpec((tm, d), lambda i: (i, 0))],
        out_specs=pl.BlockSpec((tm, d), lambda i: (i, 0)),
        out_shape=jax.ShapeDtypeStruct((tp, d), F32),
        compiler_params=_params(("parallel",)),
    )(a, gate, w, resid)


def _lru_decay(r, sp):
    log_a = -LRU_C * r * sp
    a = jnp.exp(log_a)
    e2 = a * a
    x2 = 2.0 * log_a
    series = x2 * (1.0 + x2 * (0.5 + x2 * (1.0 / 6.0)))
    em1 = jnp.where(x2 > -0.02, series, e2 - 1.0)
    return a, e2, jnp.sqrt(-em1)


def _softplus(x):
    return jnp.maximum(x, 0.0) + jnp.log1p(jnp.exp(-jnp.abs(x)))


def _rglru_fwd(u, conv_w, conv_b, w_rg, b_rg, w_ig, b_ig, lam, pad, tm):
    tp, w = u.shape
    groups, blk = w_rg.shape[0], w_rg.shape[1]

    def body(u_ref, cw_ref, cb_ref, wr_ref, br_ref, wi_ref, bi_ref, lam_ref,
             uc_ref, r_ref, ig_ref, hs_ref, a_s, mult_ref, uext, b_s, hc):
        i = pl.program_id(0)

        @pl.when(i == 0)
        def _():
            uext[0:SUBLANES, :] = jnp.zeros((SUBLANES, w), F32)
            hc[...] = jnp.zeros((SUBLANES, w), F32)

        uext[SUBLANES:SUBLANES + tm, :] = u_ref[...]
        cw = cw_ref[...]
        uc = cb_ref[...] + uext[pl.ds(SUBLANES - 3, tm), :] * cw[0:1, :]
        uc = uc + uext[pl.ds(SUBLANES - 2, tm), :] * cw[1:2, :]
        uc = uc + uext[pl.ds(SUBLANES - 1, tm), :] * cw[2:3, :]
        uc = uc + uext[pl.ds(SUBLANES, tm), :] * cw[3:4, :]
        uc_ref[...] = uc
        uext[0:SUBLANES, :] = uext[tm:tm + SUBLANES, :]

        sp = _softplus(-lam_ref[...])
        for g in range(groups):
            sl = slice(g * blk, (g + 1) * blk)
            ucg = uc_ref[:, sl]
            ucb = ucg.astype(BF16)
            r = _sigmoid(_dot(ucb, wr_ref[g]) + br_ref[:, sl])
            ig = _sigmoid(_dot(ucb, wi_ref[g]) + bi_ref[:, sl])
            r_ref[:, sl] = r
            ig_ref[:, sl] = ig
            a, _, mult = _lru_decay(r, sp[:, sl])
            a_s[:, sl] = a
            mult_ref[:, sl] = mult
            b_s[:, sl] = mult * (ig * ucg)

        @pl.when(i == 0)
        def _():
            row = lax.broadcasted_iota(jnp.int32, (Q_BLOCK, w), 0)
            start = ig_ref[0:Q_BLOCK, :] * uc_ref[0:Q_BLOCK, :]
            b_s[0:Q_BLOCK, :] = jnp.where(row < pad, 0.0, jnp.where(row == pad, start, b_s[0:Q_BLOCK, :]))
            mult_ref[0:Q_BLOCK, :] = jnp.where(row == pad, 1.0, mult_ref[0:Q_BLOCK, :])

        row8 = lax.broadcasted_iota(jnp.int32, (SUBLANES, w), 0)

        def group(gi, h_in):
            off = pl.multiple_of(gi * SUBLANES, SUBLANES)
            av = a_s[pl.ds(off, SUBLANES), :]
            bv = b_s[pl.ds(off, SUBLANES), :]
            for k in (1, 2, 4):
                keep = row8 >= k
                bv = jnp.where(keep, av * pltpu.roll(bv, k, 0) + bv, bv)
                av = jnp.where(keep, av * pltpu.roll(av, k, 0), av)
            hv = av * h_in + bv
            hs_ref[pl.ds(off, SUBLANES), :] = hv
            return jnp.broadcast_to(hv[SUBLANES - 1:SUBLANES, :], (SUBLANES, w))

        hc[...] = lax.fori_loop(0, tm // SUBLANES, group, hc[...])

    row_spec = pl.BlockSpec((tm, w), lambda i: (i, 0))
    vec_spec = pl.BlockSpec((1, w), lambda i: (0, 0))
    mat_spec = pl.BlockSpec((groups, blk, blk), lambda i: (0, 0, 0))
    return pl.pallas_call(
        body, name="b_rglru_fwd", grid=(tp // tm,),
        in_specs=[row_spec, pl.BlockSpec((CONV_WIDTH, w), lambda i: (0, 0)), vec_spec,
                  mat_spec, vec_spec, mat_spec, vec_spec, vec_spec],
        out_specs=[row_spec] * 6,
        out_shape=[jax.ShapeDtypeStruct((tp, w), F32)] * 6,
        scratch_shapes=[pltpu.VMEM((tm + SUBLANES, w), F32), pltpu.VMEM((tm, w), F32),
                        pltpu.VMEM((SUBLANES, w), F32)],
        compiler_params=_params(("arbitrary",)),
    )(u, conv_w, conv_b, w_rg, b_rg, w_ig, b_ig, lam)


def _out_proj_loss(a, gate, w, resid, g, target, x0, tm):
    tp, wd = a.shape
    d = w.shape[1]
    assert x0 % Q_BLOCK == 0 and tm % Q_BLOCK == 0 and target.shape[0] == tp - x0
    lead = x0 // Q_BLOCK
    per = tm // Q_BLOCK

    def body(a_ref, gate_ref, w_ref, res_ref, g_ref, *rest):
        t_refs, (dh_ref, loss_ref, dg_ref, da_ref, dgate_ref, dw_ref) = rest[:per], rest[per:]
        i = pl.program_id(0)

        @pl.when(i == 0)
        def _():
            loss_ref[...] = jnp.zeros_like(loss_ref)
            dg_ref[...] = jnp.zeros_like(dg_ref)
            dw_ref[...] = jnp.zeros_like(dw_ref)

        gate_v = gate_ref[...]
        av = a_ref[...]
        sg = _sigmoid(gate_v)
        silu = gate_v * sg
        y = (av * silu).astype(BF16)
        h = res_ref[...] + _dot(y, w_ref[...])
        gv = g_ref[...]
        for b in range(per):
            rows = slice(b * Q_BLOCK, (b + 1) * Q_BLOCK)
            xv = h[rows, :]
            r = _rms(xv)
            xh = xv * r
            err = jnp.where(i * per + b >= lead, xh * gv - t_refs[b][...], 0.0)
            loss_ref[...] += 0.5 * jnp.sum(jnp.mean(err * err, axis=-1, keepdims=True))
            dy = err / d
            dg_ref[...] += jnp.sum(dy * xh, axis=0, keepdims=True)
            dxh = dy * gv
            dh_ref[rows, :] = r * (dxh - xh * jnp.mean(dxh * xh, axis=-1, keepdims=True))

        dob = dh_ref[...].astype(BF16)
        dyv = _dot_nt(dob, w_ref[...])
        da_ref[...] = dyv * silu
        dgate_ref[...] = dyv * av * (sg * (1.0 + gate_v * (1.0 - sg)))
        dw_ref[...] += _dot_tn(y, dob)

    def piece(b):
        return pl.BlockSpec((Q_BLOCK, d), lambda i: (jnp.maximum(i * per + b - lead, 0), 0))

    return pl.pallas_call(
        body, name="b_out_proj_loss", grid=(tp // tm,),
        in_specs=[pl.BlockSpec((tm, wd), lambda i: (i, 0)),
                  pl.BlockSpec((tm, wd), lambda i: (i, 0)),
                  pl.BlockSpec((wd, d), lambda i: (0, 0)),
                  pl.BlockSpec((tm, d), lambda i: (i, 0)),
                  pl.BlockSpec((1, d), lambda i: (0, 0))] + [piece(b) for b in range(per)],
        out_specs=[pl.BlockSpec((tm, d), lambda i: (i, 0)),
                   pl.BlockSpec((SUBLANES, LANES), lambda i: (0, 0)),
                   pl.BlockSpec((1, d), lambda i: (0, 0)),
                   pl.BlockSpec((tm, wd), lambda i: (i, 0)),
                   pl.BlockSpec((tm, wd), lambda i: (i, 0)),
                   pl.BlockSpec((wd, d), lambda i: (0, 0))],
        out_shape=[jax.ShapeDtypeStruct((tp, d), F32),
                   jax.ShapeDtypeStruct((SUBLANES, LANES), F32),
                   jax.ShapeDtypeStruct((1, d), F32),
                   jax.ShapeDtypeStruct((tp, wd), F32),
                   jax.ShapeDtypeStruct((tp, wd), F32),
                   jax.ShapeDtypeStruct((wd, d), F32)],
        compiler_params=_params(("arbitrary",)),
    )(a, gate, w, resid, g, *([target] * per))


def _with_exchange(exchange, n_in, n_out, first, last, compute):
    if exchange is None:
        return compute
    ex_in, ex_out = exchange.n_in, exchange.n_out

    def body(*refs):
        own_in, their_in = refs[:n_in], refs[n_in:n_in + ex_in]
        pos = n_in + ex_in
        own_out, their_out = refs[pos:pos + n_out], refs[pos + n_out:pos + n_out + ex_out]
        rest = refs[pos + n_out + ex_out:]
        own_scratch, sems = rest[:len(rest) - 2], rest[len(rest) - 2:]

        @pl.when(first())
        def _():
            exchange.start(their_in, their_out, sems)

        compute(*own_in, *own_out, *own_scratch)

        @pl.when(last())
        def _():
            exchange.finish(their_in, their_out, sems)

    return body


def _gated_out_bwd(name, dout, a, gate, w, tm, delta_heads=0, exchange=None):
    tp, wd = a.shape
    d = w.shape[1]
    nt = tp // tm

    def compute(do_ref, a_ref, gate_ref, w_ref, da_ref, dgate_ref, dw_ref, *delta_ref):
        @pl.when(pl.program_id(0) == 0)
        def _():
            dw_ref[...] = jnp.zeros_like(dw_ref)

        gv = gate_ref[...]
        av = a_ref[...]
        sg = _sigmoid(gv)
        silu = gv * sg
        dob = do_ref[...].astype(BF16)
        dy = _dot_nt(dob, w_ref[...])
        da = dy * silu
        da_ref[...] = da
        dgate_ref[...] = dy * av * (sg * (1.0 + gv * (1.0 - sg)))
        dw_ref[...] += _dot_tn((av * silu).astype(BF16), dob)
        for h in range(delta_heads):
            sl = slice(h * V_HEAD, (h + 1) * V_HEAD)
            delta_ref[0][h] = _as_rows(jnp.sum(da[:, sl] * av[:, sl], axis=-1, keepdims=True))

    out_specs = [pl.BlockSpec((tm, wd), lambda i: (i, 0)),
                 pl.BlockSpec((tm, wd), lambda i: (i, 0)),
                 pl.BlockSpec((wd, d), lambda i: (0, 0))]
    out_shape = [jax.ShapeDtypeStruct((tp, wd), F32),
                 jax.ShapeDtypeStruct((tp, wd), F32),
                 jax.ShapeDtypeStruct((wd, d), F32)]
    if delta_heads:
        out_specs.append(pl.BlockSpec((delta_heads, None, SUBLANES, tm), lambda i: (0, i, 0, 0)))
        out_shape.append(jax.ShapeDtypeStruct((delta_heads, tp // tm, SUBLANES, tm), F32))
    in_specs = [pl.BlockSpec((tm, d), lambda i: (i, 0)),
                pl.BlockSpec((tm, wd), lambda i: (i, 0)),
                pl.BlockSpec((tm, wd), lambda i: (i, 0)),
                pl.BlockSpec((wd, d), lambda i: (0, 0))]
    args, scratch = [dout, a, gate, w], []
    body = _with_exchange(exchange, len(in_specs), len(out_specs),
                          lambda: pl.program_id(0) == 0, lambda: pl.program_id(0) == nt - 1, compute)
    if exchange is not None:
        in_specs = in_specs + exchange.specs(exchange.n_in)
        out_specs = out_specs + exchange.specs(exchange.n_out)
        out_shape = out_shape + exchange.out_shapes
        args, scratch = args + exchange.arrays, exchange.scratch()
    return pl.pallas_call(
        body, name=name, grid=(nt,),
        in_specs=in_specs, out_specs=out_specs, out_shape=out_shape, scratch_shapes=scratch,
        compiler_params=_params(("arbitrary",)),
    )(*args)


def _rglru_bwd(dhs, hs, r, ig, uc, u, a, mult, conv_w, w_rg, w_ig, lam, pad, tm):
    tp, w = u.shape
    groups, blk = w_rg.shape[0], w_rg.shape[1]
    nt = tp // tm
    per8 = tm // SUBLANES

    def body(dhs_ref, hs_ref, hprev_ref, r_ref, ig_ref, uc_ref, u_ref, uprev_ref, a_ref, mult_ref,
             cw_ref, wr_ref, wi_ref, lam_ref,
             du_ref, dcw_ref, dcb_ref, dwr_ref, dbr_ref, dwi_ref, dbi_ref, dlam_ref,
             aext, c_s, g_s, hext, uext, ducext, gc):
        step = pl.program_id(0)
        ti = nt - 1 - step

        @pl.when(step == 0)
        def _():
            for ref in (dcw_ref, dcb_ref, dwr_ref, dbr_ref, dwi_ref, dbi_ref, dlam_ref):
                ref[...] = jnp.zeros_like(ref)
            aext[tm:tm + SUBLANES, :] = jnp.zeros((SUBLANES, w), F32)
            ducext[tm:tm + SUBLANES, :] = jnp.zeros((SUBLANES, w), F32)
            gc[...] = jnp.zeros((SUBLANES, w), F32)

        lam_v = lam_ref[...]
        sp = _softplus(-lam_v)
        row = ti * tm + lax.broadcasted_iota(jnp.int32, (tm, w), 0)

        rv = r_ref[...]
        a = a_ref[...]
        mult = mult_ref[...]
        aext[0:tm, :] = a
        c_s[...] = aext[pl.ds(1, tm), :]
        row8 = lax.broadcasted_iota(jnp.int32, (SUBLANES, w), 0)

        def group(gi, g_in):
            off = pl.multiple_of((per8 - 1 - gi) * SUBLANES, SUBLANES)
            cv = c_s[pl.ds(off, SUBLANES), :]
            dv = dhs_ref[pl.ds(off, SUBLANES), :]
            for k in (1, 2, 4):
                keep = row8 < SUBLANES - k
                dv = jnp.where(keep, cv * pltpu.roll(dv, SUBLANES - k, 0) + dv, dv)
                cv = jnp.where(keep, cv * pltpu.roll(cv, SUBLANES - k, 0), cv)
            gv = cv * g_in + dv
            g_s[pl.ds(off, SUBLANES), :] = gv
            return jnp.broadcast_to(gv[0:1, :], (SUBLANES, w))

        gc[...] = lax.fori_loop(0, per8, group, gc[...])
        aext[tm:tm + SUBLANES, :] = aext[0:SUBLANES, :]

        gsc = jnp.where(row < pad, 0.0, g_s[...])
        hext[0:SUBLANES, :] = hprev_ref[...]
        hext[SUBLANES:SUBLANES + tm, :] = hs_ref[...]
        hprev = jnp.where(row == 0, 0.0, hext[pl.ds(SUBLANES - 1, tm), :])
        igv = ig_ref[...]
        ucv = uc_ref[...]
        first = row == pad
        dmult = gsc * (igv * ucv)
        dig = gsc * mult * ucv
        duc = gsc * mult * igv
        dlog_a = (gsc * hprev) * a + jnp.where(first, 0.0, dmult * (-(a * a) / mult))
        dlam_ref[...] += jnp.sum(dlog_a * rv, axis=0, keepdims=True) * (LRU_C * _sigmoid(-lam_v))
        dpre_r = dlog_a * (-LRU_C * sp) * (rv * (1.0 - rv))
        dpre_i = dig * (igv * (1.0 - igv))
        dbr_ref[...] += jnp.sum(dpre_r, axis=0, keepdims=True)
        dbi_ref[...] += jnp.sum(dpre_i, axis=0, keepdims=True)
        for g in range(groups):
            sl = slice(g * blk, (g + 1) * blk)
            ucb = ucv[:, sl].astype(BF16)
            drb = dpre_r[:, sl].astype(BF16)
            dib = dpre_i[:, sl].astype(BF16)
            dwr_ref[g] += _dot_tn(ucb, drb)
            dwi_ref[g] += _dot_tn(ucb, dib)
            ducext[0:tm, sl] = duc[:, sl] + _dot_nt(drb, wr_ref[g]) + _dot_nt(dib, wi_ref[g])

        ducv = ducext[0:tm, :]
        cw = cw_ref[...]
        dcb_ref[...] += jnp.sum(ducv, axis=0, keepdims=True)
        uext[0:SUBLANES, :] = jnp.where(ti == 0, 0.0, uprev_ref[...])
        uext[SUBLANES:SUBLANES + tm, :] = u_ref[...]
        for j in range(CONV_WIDTH):
            ush = uext[pl.ds(SUBLANES - (CONV_WIDTH - 1 - j), tm), :]
            dcw_ref[j:j + 1, :] += jnp.sum(ducv * ush, axis=0, keepdims=True)
        du = ducv * cw[3:4, :]
        for k in range(1, CONV_WIDTH):
            du = du + ducext[pl.ds(k, tm), :] * cw[3 - k:4 - k, :]
        du_ref[...] = du
        ducext[tm:tm + SUBLANES, :] = ducext[0:SUBLANES, :]

    rev = lambda s: (nt - 1 - s, 0)
    halo = lambda s: (jnp.maximum((nt - 1 - s) * per8 - 1, 0), 0)
    row_spec = pl.BlockSpec((tm, w), rev)
    halo_spec = pl.BlockSpec((SUBLANES, w), halo)
    vec_spec = pl.BlockSpec((1, w), lambda s: (0, 0))
    mat_spec = pl.BlockSpec((groups, blk, blk), lambda s: (0, 0, 0))
    cw_spec = pl.BlockSpec((CONV_WIDTH, w), lambda s: (0, 0))
    return pl.pallas_call(
        body, name="b_rglru_bwd", grid=(nt,),
        in_specs=[row_spec, row_spec, halo_spec, row_spec, row_spec, row_spec, row_spec, halo_spec, row_spec, row_spec,
                  cw_spec, mat_spec, mat_spec, vec_spec],
        out_specs=[row_spec, cw_spec, vec_spec, mat_spec, vec_spec, mat_spec, vec_spec, vec_spec],
        out_shape=[jax.ShapeDtypeStruct((tp, w), F32), jax.ShapeDtypeStruct((CONV_WIDTH, w), F32),
                   jax.ShapeDtypeStruct((1, w), F32), jax.ShapeDtypeStruct((groups, blk, blk), F32),
                   jax.ShapeDtypeStruct((1, w), F32), jax.ShapeDtypeStruct((groups, blk, blk), F32),
                   jax.ShapeDtypeStruct((1, w), F32), jax.ShapeDtypeStruct((1, w), F32)],
        scratch_shapes=[pltpu.VMEM((tm + SUBLANES, w), F32), pltpu.VMEM((tm, w), F32), pltpu.VMEM((tm, w), F32),
                        pltpu.VMEM((tm + SUBLANES, w), F32), pltpu.VMEM((tm + SUBLANES, w), F32),
                        pltpu.VMEM((tm + SUBLANES, w), F32), pltpu.VMEM((SUBLANES, w), F32)],
        compiler_params=_params(("arbitrary",)),
    )(dhs, hs, hs, r, ig, uc, u, u, a, mult, conv_w, w_rg, w_ig, lam)


def _norm_matmul_bwd(name, x, g, w, dys, tm, resid=None, prologue=None, extra_out=None, exchange=None):
    tp, kin = x.shape
    n = w.shape[1]
    nt = tp // tm
    n_dy = len(dys)
    has_res = resid is not None
    has_extra = extra_out is not None

    def compute(*refs):
        x_ref, g_ref, w_ref = refs[:3]
        dy_refs = refs[3:3 + n_dy]
        pos = 3 + n_dy
        res_ref = refs[pos] if has_res else None
        pos += int(has_res)
        dx_ref, dw_ref, dg_ref = refs[pos:pos + 3]
        pos += 3
        ex_ref = refs[pos] if has_extra else None
        pos += int(has_extra)
        dy_s = refs[pos]

        @pl.when(pl.program_id(0) == 0)
        def _():
            dw_ref[...] = jnp.zeros_like(dw_ref)
            dg_ref[...] = jnp.zeros_like(dg_ref)

        if prologue is None:
            c0 = 0
            for ref in dy_refs:
                dy_s[:, c0:c0 + ref.shape[1]] = ref[...].astype(BF16)
                c0 += ref.shape[1]
        else:
            prologue(dy_refs, dy_s, ex_ref)

        xv = x_ref[...]
        gv = g_ref[...]
        r = _rms(xv)
        xh = xv * r
        dyb = dy_s[...]
        dn = _dot_nt(dyb, w_ref[...])
        dw_ref[...] += _dot_tn((xh * gv).astype(BF16), dyb)
        dg_ref[...] += jnp.sum(dn * xh, axis=0, keepdims=True)
        dxh = dn * gv
        dx = r * (dxh - xh * jnp.mean(dxh * xh, axis=-1, keepdims=True))
        if has_res:
            dx = dx + res_ref[...]
        dx_ref[...] = dx

    row = lambda width: pl.BlockSpec((tm, width), lambda i: (i, 0))
    in_specs = [row(kin), pl.BlockSpec((1, kin), lambda i: (0, 0)), pl.BlockSpec((kin, n), lambda i: (0, 0))]
    in_specs += [row(a.shape[1]) for a in dys]
    args = [x, g, w, *dys]
    if has_res:
        in_specs.append(row(kin))
        args.append(resid)
    out_specs = [row(kin), pl.BlockSpec((kin, n), lambda i: (0, 0)), pl.BlockSpec((1, kin), lambda i: (0, 0))]
    out_shape = [jax.ShapeDtypeStruct((tp, kin), F32), jax.ShapeDtypeStruct((kin, n), F32),
                 jax.ShapeDtypeStruct((1, kin), F32)]
    if has_extra:
        out_specs.append(row(extra_out[0]))
        out_shape.append(jax.ShapeDtypeStruct((tp, extra_out[0]), extra_out[1]))
    scratch = [pltpu.VMEM((tm, n), BF16)]
    body = _with_exchange(exchange, len(in_specs), len(out_specs),
                          lambda: pl.program_id(0) == 0, lambda: pl.program_id(0) == nt - 1, compute)
    if exchange is not None:
        in_specs = in_specs + exchange.specs(exchange.n_in)
        out_specs = out_specs + exchange.specs(exchange.n_out)
        out_shape = out_shape + exchange.out_shapes
        args, scratch = args + exchange.arrays, scratch + exchange.scratch()
    return pl.pallas_call(
        body, name=name, grid=(nt,),
        in_specs=in_specs, out_specs=out_specs, out_shape=out_shape, scratch_shapes=scratch,
        compiler_params=_params(("arbitrary",)),
    )(*args)


def _flash_bwd(q, k, v, lse, delta, do, heads, pad, tq, exchange=None):
    tp = q.shape[0]
    nq = tp // tq
    scale = (QK_NOPE + QK_ROPE) ** -0.5
    c2 = scale * LOG2E

    def compute(q_ref, k_ref, v_ref, lse_ref, delta_ref, do_ref, dq_ref, dk_ref, dv_ref):
        j = pl.program_id(1)

        @pl.when(j == 0)
        def _():
            dq_ref[...] = jnp.zeros_like(dq_ref)

        kv = k_ref[...]
        vv = v_ref[...]

        def rows_of(ref, i, blocks):
            parts = [ref[i + b][0:1, :] for b in range(blocks)]
            return parts[0] if blocks == 1 else jnp.concatenate(parts, axis=1)

        def make_step(masked, blocks):
            def step(i, carry):
                off = pl.multiple_of(i * tq, tq)
                qv = q_ref[pl.ds(off, blocks * tq), :]
                dob = do_ref[pl.ds(off, blocks * tq), :].astype(BF16)
                p = jnp.exp2(_dot_nt(kv, qv) * c2 - rows_of(lse_ref, i, blocks))
                if masked:
                    key = j * tq + lax.broadcasted_iota(jnp.int32, (tq, tq), 0)
                    qry = j * tq + lax.broadcasted_iota(jnp.int32, (tq, tq), 1)
                    first = jnp.where((key <= qry) & (key >= pad), p[:, :tq], 0.0)
                    p = first if blocks == 1 else jnp.concatenate([first, p[:, tq:]], axis=1)
                dv_ref[...] += _dot(p.astype(BF16), dob)
                dp = _dot_nt(vv, dob)
                ds = (p * (dp - rows_of(delta_ref, i, blocks)) * scale).astype(BF16)
                dk_ref[...] += _dot(ds, qv)
                dq_ref[pl.ds(off, blocks * tq), :] += _dot_tn(ds, kv)
                return carry
            return step

        dk_ref[...] = jnp.zeros_like(dk_ref)
        dv_ref[...] = jnp.zeros_like(dv_ref)
        odd = (nq - j) % 2
        lax.fori_loop(0, odd, lambda t, cr: make_step(True, 1)(j, cr), 0)
        lax.fori_loop(0, 1 - odd, lambda t, cr: make_step(True, 2)(j, cr), 0)
        start = j + 2 - odd
        for blocks in (4, 2):
            trips = (nq - start) // blocks
            step_n = make_step(False, blocks)
            lax.fori_loop(0, trips, lambda t, cr, s=start, b=blocks, f=step_n: f(s + b * t, cr), 0)
            start = start + blocks * trips

    in_specs = [pl.BlockSpec((tp, HEAD_W), lambda h, j: (0, h)),
                pl.BlockSpec((tq, HEAD_W), lambda h, j: (j, h)),
                pl.BlockSpec((tq, V_HEAD), lambda h, j: (j, h)),
                pl.BlockSpec((None, nq, SUBLANES, tq), lambda h, j: (h, 0, 0, 0)),
                pl.BlockSpec((None, nq, SUBLANES, tq), lambda h, j: (h, 0, 0, 0)),
                pl.BlockSpec((tp, V_HEAD), lambda h, j: (0, h))]
    out_specs = [pl.BlockSpec((tp, HEAD_W), lambda h, j: (0, h)),
                 pl.BlockSpec((tq, HEAD_W), lambda h, j: (j, h)),
                 pl.BlockSpec((tq, V_HEAD), lambda h, j: (j, h))]
    out_shape = [jax.ShapeDtypeStruct((tp, heads * HEAD_W), F32),
                 jax.ShapeDtypeStruct((tp, heads * HEAD_W), F32),
                 jax.ShapeDtypeStruct((tp, heads * V_HEAD), F32)]
    args, scratch = [q, k, v, lse, delta, do], []
    body = _with_exchange(exchange, len(in_specs), len(out_specs),
                          lambda: (pl.program_id(0) == 0) & (pl.program_id(1) == 0),
                          lambda: (pl.program_id(0) == heads - 1) & (pl.program_id(1) == nq - 1), compute)
    if exchange is not None:
        in_specs = in_specs + exchange.specs(exchange.n_in)
        out_specs = out_specs + exchange.specs(exchange.n_out)
        out_shape = out_shape + exchange.out_shapes
        args, scratch = args + exchange.arrays, exchange.scratch()
    return pl.pallas_call(
        body, name="a_flash_bwd", grid=(heads, nq),
        in_specs=in_specs, out_specs=out_specs, out_shape=out_shape, scratch_shapes=scratch,
        compiler_params=_params(("arbitrary", "arbitrary")),
    )(*args)


def _position():
    return lax.axis_index("x"), lax.axis_index("y"), lax.axis_index("c")


def _other_chips(x, y):
    return [(1 - x, y), (x, 1 - y), (1 - x, 1 - y)]


def _block(ref, shard_axis, n, k, split_axis=None, m=None, h=None):
    idx = []
    for a in range(len(ref.shape)):
        start = 0
        size = None
        if a == shard_axis:
            start, size = k * n, n
        if a == split_axis:
            size = (n if a == shard_axis else m) // 2
            start = start + h * size
        idx.append(slice(None) if size is None else pl.ds(start, size))
    return ref.at[tuple(idx)]


def _gather_weights(split, whole_small):
    ns, nw = len(split), len(whole_small)
    n = ns + nw
    arrs = [s[0] for s in split] + [s[0] for s in whole_small]
    axes = [s[1] for s in split] + [s[1] for s in whole_small]

    def body(*refs):
        ins, outs = refs[:n], refs[n:2 * n]
        ici_send, ici_recv, d2d_send, d2d_recv, sib_send, sib_recv = refs[2 * n:]
        x, y, c = _position()
        me = 2 * x + y
        others = _other_chips(x, y)
        sent, local = [], []

        def remote(src, dst, sems, idx, to):
            return pltpu.make_async_remote_copy(src_ref=src, dst_ref=dst, send_sem=sems[0].at[idx],
                                                recv_sem=sems[1].at[idx], device_id=to, device_id_type=MESH)

        for a in range(n):
            width = ins[a].shape[axes[a]]
            mine = remote(ins[a], _block(outs[a], axes[a], width, me), (sib_send, sib_recv), a, (x, y, 1 - c))
            mine.start()
            local.append(mine)
            for j, (px, py) in enumerate(others):
                if a < ns:
                    sx = split[a][2]
                    src = _block(ins[a], None, None, None, sx, ins[a].shape[sx], c)
                    dst = _block(outs[a], axes[a], width, me, sx, outs[a].shape[sx], c)
                else:
                    src, dst = ins[a], _block(outs[a], axes[a], width, me)
                cp = remote(src, dst, (ici_send, ici_recv), 3 * a + j, (px, py, c))
                cp.start()
                sent.append(cp)
        for a in range(ns):
            width = ins[a].shape[axes[a]]
            sx = split[a][2]
            for j, (px, py) in enumerate(others):
                theirs = _block(outs[a], axes[a], width, 2 * px + py, sx, outs[a].shape[sx], c)
                remote(theirs, theirs, (ici_send, ici_recv), 3 * a + j, (px, py, c)).wait_recv()
                fwd = remote(theirs, theirs, (d2d_send, d2d_recv), 3 * a + j, (x, y, 1 - c))
                fwd.start()
                sent.append(fwd)
        for a in range(ns, n):
            width = ins[a].shape[axes[a]]
            for j, (px, py) in enumerate(others):
                theirs = _block(outs[a], axes[a], width, 2 * px + py)
                remote(theirs, theirs, (ici_send, ici_recv), 3 * a + j, (px, py, c)).wait_recv()
        for a in range(ns):
            width = ins[a].shape[axes[a]]
            sx = split[a][2]
            for j, (px, py) in enumerate(others):
                from_sibling = _block(outs[a], axes[a], width, 2 * px + py, sx, outs[a].shape[sx], 1 - c)
                remote(from_sibling, from_sibling, (d2d_send, d2d_recv), 3 * a + j, (x, y, 1 - c)).wait_recv()
        for cp in sent:
            cp.wait_send()
        for cp in local:
            cp.wait()

    def whole_shape(a, axis):
        shape = list(a.shape)
        shape[axis] *= N_CHIPS
        return jax.ShapeDtypeStruct(tuple(shape), a.dtype)

    any_spec = pl.BlockSpec(memory_space=pl.ANY)
    return pl.pallas_call(
        body, name="gather_weights",
        in_specs=[any_spec] * n, out_specs=[any_spec] * n,
        out_shape=[whole_shape(a, ax) for a, ax in zip(arrs, axes)],
        scratch_shapes=[pltpu.SemaphoreType.DMA((3 * n,)), pltpu.SemaphoreType.DMA((3 * n,)),
                        pltpu.SemaphoreType.DMA((3 * ns,)), pltpu.SemaphoreType.DMA((3 * ns,)),
                        pltpu.SemaphoreType.DMA((n,)), pltpu.SemaphoreType.DMA((n,))],
        compiler_params=pltpu.CompilerParams(has_side_effects=True),
    )(*arrs)


class _Grad:
    def __init__(self, name, g, kind, rh, cols, groups=None):
        self.name, self.g, self.kind, self.rh, self.cols, self.groups = name, g, kind, rh, cols, groups
        if kind == 'rows':
            self.tr = rh
        elif kind == 'gate':
            self.tr = rh // (groups // 2)
        else:
            self.tr = rh if rh <= 512 else 256
        self.nb = rh // self.tr

    def pieces(self, ref, k, h):
        rh, cols = self.rh, self.cols
        if self.kind == 'cols':
            return [(ref.at[pl.ds(h * rh, rh), pl.ds(k * cols, cols)], 0, rh)]
        if self.kind == 'rows':
            return [(ref.at[pl.ds((2 * k + h) * rh, rh), :], 0, rh)]
        if self.kind == 'lead':
            return [(ref.at[k, pl.ds(h * rh, rh), :], 0, rh)]
        per = self.groups // 2
        return [(ref.at[pl.ds((((h * per + gi) * N_CHIPS) + k) * self.tr, self.tr), :], gi * self.tr, self.tr)
                for gi in range(per)]

    def block_spec(self):
        tr, nb, cols = self.tr, self.nb, self.cols
        if self.kind == 'cols':
            return pl.BlockSpec((tr, cols), lambda k, i, c: (c[0] * nb + i, k))
        if self.kind == 'rows':
            return pl.BlockSpec((tr, cols), lambda k, i, c: (2 * k + c[0], 0))
        if self.kind == 'lead':
            return pl.BlockSpec((None, tr, cols), lambda k, i, c: (k, c[0] * nb + i, 0))
        return pl.BlockSpec((tr, cols), lambda k, i, c: ((c[0] * nb + i) * N_CHIPS + k, 0))


class _Exchange:
    def __init__(self, name, arrays, out_shapes, n_copies, copies):
        self.name, self.arrays, self.out_shapes, self.n_copies, self.copies = name, arrays, out_shapes, n_copies, copies
        self.n_in, self.n_out = len(arrays), len(out_shapes)

    def specs(self, n):
        return [pl.BlockSpec(memory_space=pl.ANY)] * n

    def scratch(self):
        return [pltpu.SemaphoreType.DMA((self.n_copies,)), pltpu.SemaphoreType.DMA((self.n_copies,))]

    def _descriptors(self, in_refs, out_refs, sems):
        return self.copies(in_refs, out_refs, sems[0], sems[1])

    def start(self, in_refs, out_refs, sems):
        for cp in self._descriptors(in_refs, out_refs, sems):
            cp.start()

    def finish(self, in_refs, out_refs, sems):
        for cp in self._descriptors(in_refs, out_refs, sems):
            cp.wait()

    def __add__(self, other):
        def copies(ins, outs, send_sems, recv_sems, base=0):
            return (self.copies(ins[:self.n_in], outs[:self.n_out], send_sems, recv_sems, base)
                    + other.copies(ins[self.n_in:], outs[self.n_out:], send_sems, recv_sems, base + self.n_copies))

        return _Exchange(self.name + "_" + other.name, self.arrays + other.arrays, self.out_shapes + other.out_shapes,
                         self.n_copies + other.n_copies, copies)

    def run(self):
        def body(*refs):
            ins, outs, sems = refs[:self.n_in], refs[self.n_in:self.n_in + self.n_out], refs[self.n_in + self.n_out:]
            self.start(ins, outs, sems)
            self.finish(ins, outs, sems)

        return pl.pallas_call(
            body, name=self.name,
            in_specs=self.specs(self.n_in), out_specs=self.specs(self.n_out), out_shape=self.out_shapes,
            scratch_shapes=self.scratch(),
            compiler_params=pltpu.CompilerParams(has_side_effects=True),
        )(*self.arrays)


def _gather_whole(name, shards):
    def copies(ins, outs, send_sems, recv_sems, base=0):
        x, y, c = _position()
        me = 2 * x + y
        made = []
        for a, (_, axis) in enumerate(shards):
            dst = _block(outs[a], axis, ins[a].shape[axis], me)
            for j, to in enumerate([(x, y, 1 - c)] + [(px, py, c) for px, py in _other_chips(x, y)]):
                idx = base + 4 * a + j
                made.append(pltpu.make_async_remote_copy(
                    src_ref=ins[a], dst_ref=dst, send_sem=send_sems.at[idx], recv_sem=recv_sems.at[idx],
                    device_id=to, device_id_type=MESH))
        return made

    def whole_shape(a, axis):
        shape = list(a.shape)
        shape[axis] *= N_CHIPS
        return jax.ShapeDtypeStruct(tuple(shape), a.dtype)

    return _Exchange(name, [s[0] for s in shards], [whole_shape(*s) for s in shards], 4 * len(shards), copies)


def _halves_to_sibling(name, grads):
    total = sum(len(gr.pieces(gr.g, 0, 0)) * N_CHIPS for gr in grads)

    def copies(ins, outs, send_sems, recv_sems, base=0):
        x, y, c = _position()
        made = []
        for gr, g_ref, got_ref in zip(grads, ins, outs):
            for k in range(N_CHIPS):
                for src, r0, nr in gr.pieces(g_ref, k, 1 - c):
                    idx = base + len(made)
                    made.append(pltpu.make_async_remote_copy(
                        src_ref=src, dst_ref=got_ref.at[k, pl.ds(r0, nr), :],
                        send_sem=send_sems.at[idx], recv_sem=recv_sems.at[idx],
                        device_id=(x, y, 1 - c), device_id_type=MESH))
        return made

    return _Exchange(name, [gr.g for gr in grads],
                     [jax.ShapeDtypeStruct((N_CHIPS, gr.rh, gr.cols), F32) for gr in grads], total, copies)


def _chip_sum(gr, got, c, wire=BF16):
    def body(c_ref, g_ref, got_ref, o_ref):
        o_ref[...] = (g_ref[...] + got_ref[...]).astype(wire)

    tile = pl.BlockSpec((None, gr.tr, gr.cols), lambda k, i, c_ref: (k, i, 0))
    return pl.pallas_call(
        body, name="chip_sum_" + gr.name,
        grid_spec=pltpu.PrefetchScalarGridSpec(
            num_scalar_prefetch=1, grid=(N_CHIPS, gr.nb),
            in_specs=[gr.block_spec(), tile], out_specs=tile),
        out_shape=jax.ShapeDtypeStruct((N_CHIPS, gr.rh, gr.cols), wire),
        compiler_params=_params(("parallel", "parallel")),
    )(c, gr.g, got)


def _blocks_to_chips(name, parts):
    n = len(parts)

    def copies(ins, outs, send_sems, recv_sems, base=0):
        x, y, c = _position()
        made = []
        for a in range(n):
            for j, (px, py) in enumerate(_other_chips(x, y)):
                idx = base + 3 * a + j
                made.append(pltpu.make_async_remote_copy(
                    src_ref=ins[a].at[2 * px + py], dst_ref=outs[a].at[j],
                    send_sem=send_sems.at[idx], recv_sem=recv_sems.at[idx],
                    device_id=(px, py, c), device_id_type=MESH))
        return made

    return _Exchange(name, parts, [jax.ShapeDtypeStruct((3,) + p.shape[1:], p.dtype) for p in parts], 3 * n, copies)


def _sum_chips(name, part, got, me, c):
    nk, rh, cols = part.shape
    tr = rh if rh <= 512 else 256
    nb = rh // tr

    def body(me_ref, c_ref, own_ref, *rest):
        got_refs, o_ref = rest[:nk], rest[nk]
        own = own_ref[...].astype(F32)
        acc = None
        for k in range(nk):
            term = jnp.where(me_ref[0] == k, own, got_refs[k][...].astype(F32))
            acc = term if acc is None else acc + term
        o_ref[...] = acc

    def got_map(k):
        def index(i, me_ref, c_ref):
            xor = jnp.bitwise_xor(me_ref[0], k)
            slot = jnp.where(xor == 1, 1, jnp.where(xor == 3, 2, 0))
            return (slot, i, 0)
        return index

    return pl.pallas_call(
        body, name="sum_" + name,
        grid_spec=pltpu.PrefetchScalarGridSpec(
            num_scalar_prefetch=2, grid=(nb,),
            in_specs=[pl.BlockSpec((None, tr, cols), lambda i, me_ref, c_ref: (me_ref[0], i, 0))]
            + [pl.BlockSpec((None, tr, cols), got_map(k)) for k in range(nk)],
            out_specs=pl.BlockSpec((tr, cols), lambda i, me_ref, c_ref: (c_ref[0] * nb + i, 0))),
        out_shape=jax.ShapeDtypeStruct((2 * rh, cols), F32),
        compiler_params=_params(("parallel",)),
    )(me, c, part, *([got] * nk))


def _share_with_sibling(halves):
    n = len(halves)

    def body(*refs):
        outs = refs[n:2 * n]
        send_sems, recv_sems = refs[2 * n:]
        x, y, c = _position()
        copies = []
        for a in range(n):
            rh = outs[a].shape[0] // 2
            mine = outs[a].at[pl.ds(c * rh, rh), :]
            cp = pltpu.make_async_remote_copy(
                src_ref=mine, dst_ref=mine, send_sem=send_sems.at[a], recv_sem=recv_sems.at[a],
                device_id=(x, y, 1 - c), device_id_type=MESH)
            cp.start()
            copies.append(cp)
        for cp in copies:
            cp.wait()

    any_spec = pl.BlockSpec(memory_space=pl.ANY)
    return pl.pallas_call(
        body, name="grads_share",
        in_specs=[any_spec] * n, out_specs=[any_spec] * n,
        out_shape=[jax.ShapeDtypeStruct(h.shape, h.dtype) for h in halves],
        input_output_aliases={a: a for a in range(n)},
        scratch_shapes=[pltpu.SemaphoreType.DMA((n,)), pltpu.SemaphoreType.DMA((n,))],
        compiler_params=pltpu.CompilerParams(has_side_effects=True),
    )(*halves)


def _adamw(name, w, g, m, v):
    rows, cols = w.shape
    tr = 256 if rows % 256 == 0 else rows

    def body(w_ref, g_ref, m_ref, v_ref, d_ref, nm_ref, nv_ref):
        gv = g_ref[...]
        mn = ADAM_B1 * m_ref[...] + (1.0 - ADAM_B1) * gv
        vn = ADAM_B2 * v_ref[...] + (1.0 - ADAM_B2) * (gv * gv)
        m_hat = mn / (1.0 - ADAM_B1 ** ADAM_STEP)
        v_hat = vn / (1.0 - ADAM_B2 ** ADAM_STEP)
        d_ref[...] = -ADAM_LR * (m_hat / (jnp.sqrt(v_hat) + ADAM_EPS) + ADAM_WD * w_ref[...])
        nm_ref[...] = mn
        nv_ref[...] = vn

    spec = pl.BlockSpec((tr, cols), lambda i: (i, 0))
    return pl.pallas_call(
        body, name=name, grid=(rows // tr,),
        in_specs=[spec] * 4, out_specs=[spec] * 3,
        out_shape=[jax.ShapeDtypeStruct((rows, cols), F32)] * 3,
        compiler_params=_params(("parallel",)),
    )(w, g, m, v)


def _as2d(a):
    if a.ndim == 1:
        return a.reshape(1, -1)
    return a.reshape(-1, a.shape[-1])


def _unshard(gathered, axis):
    moved = jnp.moveaxis(gathered, 0, axis)
    shape = list(gathered.shape[1:])
    shape[axis] *= N_CHIPS
    return moved.reshape(shape)


def _rope_tables(tp, pad):
    pos = jnp.arange(tp, dtype=F32) - pad
    inv_freq = ROPE_BASE ** (-jnp.arange(0, QK_ROPE, 2, dtype=F32) / QK_ROPE)
    ang = pos[:, None] * inv_freq[None, :]
    cos, sin = jnp.cos(ang), jnp.sin(ang)
    zeros = jnp.zeros((tp, LANES - QK_ROPE), F32)
    return jnp.concatenate([cos, cos, zeros], axis=1), jnp.concatenate([-sin, sin, zeros], axis=1)


def _matrix_grad(name, g, heads):
    if name in ('b_w_rg', 'b_w_ig'):
        groups, blk, cols = g.shape
        return _Grad(name, g.reshape(groups * blk, cols), 'gate', (groups // 2) * (blk // N_CHIPS), cols, groups)
    rows, cols = g.shape
    if name in ('a_w_out', 'b_w_out'):
        return _Grad(name, g, 'rows', rows // (2 * N_CHIPS), cols)
    if name == 'a_w_uq' and heads % N_CHIPS != 0:
        g = g.reshape(rows, heads, HEAD_W)[:, :, :QK_NOPE + QK_ROPE].reshape(rows, -1)
        cols = g.shape[1]
    if name == 'a_w_in' or (name == 'a_w_uq' and heads % N_CHIPS != 0):
        g = jnp.moveaxis(g.reshape(rows, N_CHIPS, cols // N_CHIPS), 1, 0)
        return _Grad(name, g, 'lead', rows // 2, cols // N_CHIPS)
    return _Grad(name, g, 'cols', rows // 2, cols // N_CHIPS)


def _kernel_form(name, w):
    return w[0] if name in ('b_w_rg', 'b_w_ig', 'b_conv_w') else _as2d(w)


def _local_grads(x, target, wt, heads, c_idx, mid_names, mid_gather, late_names, late_gather):
    wt = dict(wt)
    seq, d = x.shape
    n_meta = wt['meta_tokens'].shape[0]
    t = seq + n_meta
    pad = (-t) % Q_BLOCK
    tp = t + pad
    x0 = pad + n_meta
    tm = _row_tile(tp)
    ql = wt['a_q_norm_g'].shape[1]
    kvl = wt['a_kv_norm_g'].shape[1]
    mla_w = heads * V_HEAD

    h0 = jnp.concatenate([jnp.zeros((pad, d), F32), wt['meta_tokens'], x], axis=0)
    cos_t, sin_t = _rope_tables(tp, pad)

    w_in_a = wt['a_w_in']
    zcol = jnp.zeros((d, LANES - QK_ROPE), BF16)
    w_in_a = jnp.concatenate([w_in_a[:, :ql + kvl + QK_ROPE], zcol, w_in_a[:, ql + kvl + QK_ROPE:]], axis=1)
    c_kv, c_kr, c_gate = ql, ql + kvl, ql + kvl + LANES
    splits_a = [(0, c_kv), (c_kv, c_kr), (c_kr, c_gate), (c_gate, c_gate + mla_w)]

    q_lat, kv_lat, kr_raw, gate_a, *mid_whole = _norm_matmul("a_in_proj", h0, wt['a_norm_g'], w_in_a, splits_a, tm,
                                                             exchange=mid_gather)
    wt.update({n: _kernel_form(n, w) for n, w in zip(mid_names, mid_whole)})
    w_uq = wt['a_w_uq'].reshape(ql, heads, QK_NOPE + QK_ROPE)
    w_uq = jnp.pad(w_uq, ((0, 0), (0, 0), (0, HEAD_W - QK_NOPE - QK_ROPE))).reshape(ql, heads * HEAD_W)
    w_ukv = wt['a_w_ukv']
    q = _q_proj(q_lat, wt['a_q_norm_g'], w_uq, cos_t, sin_t, heads, tm)
    k, v = _kv_proj(kv_lat, wt['a_kv_norm_g'], w_ukv, kr_raw, cos_t, sin_t, heads, tm)
    attn, lse, *late_whole = _flash_fwd(q, k, v, heads, pad, tm, exchange=late_gather)
    wt.update({n: _kernel_form(n, w) for n, w in zip(late_names, late_whole)})
    lru_w = wt['b_conv_w'].shape[1]
    h1 = _gated_out("a_out_proj", attn, gate_a, wt['a_w_out'], h0, tm)

    u, gate_b = _norm_matmul("b_in_proj", h1, wt['b_norm_g'], wt['b_w_in'], [(0, lru_w), (lru_w, 2 * lru_w)], tm)
    uc, r, ig, hs, decay, mult = _rglru_fwd(u, wt['b_conv_w'], wt['b_conv_b'], wt['b_w_rg'], wt['b_b_rg'],
                                            wt['b_w_ig'], wt['b_b_ig'], wt['b_lam'], pad, tm)

    dh2, loss, d_final_g, dhs, dgate_b, dw_out_b = _out_proj_loss(
        hs, gate_b, wt['b_w_out'], h1, wt['final_norm_g'], target, x0, tm)
    du, dconv_w, dconv_b, dw_rg, db_rg, dw_ig, db_ig, dlam = _rglru_bwd(
        dhs, hs, r, ig, uc, u, decay, mult, wt['b_conv_w'], wt['b_w_rg'], wt['b_w_ig'], wt['b_lam'], pad, tm)
    dh1, dw_in_b, dg_b = _norm_matmul_bwd("b_in_proj_bwd", h1, wt['b_norm_g'], wt['b_w_in'], [du, dgate_b], tm, resid=dh2)

    grads_b = [_matrix_grad(n, g, heads) for n, g in
               (('b_w_in', dw_in_b), ('b_w_rg', dw_rg), ('b_w_ig', dw_ig), ('b_w_out', dw_out_b))]
    dattn, dgate_a, dw_out_a, delta, *got = _gated_out_bwd(
        "a_out_proj_bwd", dh1, attn, gate_a, wt['a_w_out'], tm, delta_heads=heads,
        exchange=_halves_to_sibling("swap_b", grads_b))
    sums_b = [_chip_sum(gr, r, c_idx) for gr, r in zip(grads_b, got)]
    grad_out = _matrix_grad('a_w_out', dw_out_a, heads)
    dq, dk, dv, *landed = _flash_bwd(
        q, k, v, lse, delta, dattn, heads, pad, tm,
        exchange=_blocks_to_chips("chips_b", sums_b) + _halves_to_sibling("swap_out", [grad_out]))
    through = list(zip(grads_b, sums_b, landed[:len(grads_b)]))
    sum_out = _chip_sum(grad_out, landed[len(grads_b)], c_idx)

    def q_prologue(dy_refs, dy_s, ex_ref):
        (dq_ref,), cos_v, sin_v = dy_refs[:1], dy_refs[1][...], dy_refs[2][...]
        for h in range(heads):
            c0 = h * HEAD_W
            dy_s[:, c0:c0 + QK_NOPE] = dq_ref[:, c0:c0 + QK_NOPE].astype(BF16)
            dy_s[:, c0 + QK_NOPE:c0 + HEAD_W] = _unrope(dq_ref[:, c0 + QK_NOPE:c0 + HEAD_W], cos_v, sin_v).astype(BF16)

    dq_lat, dw_uq, dg_q, from_chips_out = _norm_matmul_bwd(
        "a_q_proj_bwd", q_lat, wt['a_q_norm_g'], w_uq, [dq, cos_t, sin_t], tm, prologue=q_prologue,
        exchange=_blocks_to_chips("chips_out", [sum_out]))
    through.append((grad_out, sum_out, from_chips_out))
    grad_uq = _matrix_grad('a_w_uq', dw_uq, heads)

    def kv_prologue(dy_refs, dy_s, ex_ref):
        dk_ref, dv_ref = dy_refs[:2]
        cos_v, sin_v = dy_refs[2][...], dy_refs[3][...]
        dkr = jnp.zeros((dk_ref.shape[0], LANES), F32)
        for h in range(heads):
            c0 = h * (QK_NOPE + V_HEAD)
            dy_s[:, c0:c0 + QK_NOPE] = dk_ref[:, h * HEAD_W:h * HEAD_W + QK_NOPE].astype(BF16)
            dy_s[:, c0 + QK_NOPE:c0 + QK_NOPE + V_HEAD] = dv_ref[:, h * V_HEAD:(h + 1) * V_HEAD].astype(BF16)
            dkr = dkr + dk_ref[:, h * HEAD_W + QK_NOPE:(h + 1) * HEAD_W]
        ex_ref[...] = _unrope(dkr, cos_v, sin_v)

    dkv_lat, dw_ukv, dg_kv, dkr_raw, got_uq = _norm_matmul_bwd(
        "a_kv_proj_bwd", kv_lat, wt['a_kv_norm_g'], w_ukv, [dk, dv, cos_t, sin_t], tm,
        prologue=kv_prologue, extra_out=(LANES, F32), exchange=_halves_to_sibling("swap_uq", [grad_uq]))
    sum_uq = _chip_sum(grad_uq, got_uq, c_idx)
    grad_ukv = _matrix_grad('a_w_ukv', dw_ukv, heads)

    dh0, dw_in_a, dg_a, from_chips_uq, got_ukv = _norm_matmul_bwd(
        "a_in_proj_bwd", h0, wt['a_norm_g'], w_in_a, [dq_lat, dkv_lat, dkr_raw, dgate_a], tm, resid=dh1,
        exchange=_blocks_to_chips("chips_uq", [sum_uq]) + _halves_to_sibling("swap_ukv", [grad_ukv]))
    through.append((grad_uq, sum_uq, from_chips_uq))
    swapped = [(grad_ukv, _chip_sum(grad_ukv, got_ukv, c_idx))]

    dw_in_a = jnp.concatenate([dw_in_a[:, :c_kr + QK_ROPE], dw_in_a[:, c_gate:]], axis=1)
    grads = {
        'meta_tokens': dh0[pad:x0], 'a_norm_g': dg_a, 'a_w_in': dw_in_a, 'a_q_norm_g': dg_q, 'a_kv_norm_g': dg_kv,
        'a_w_uq': dw_uq, 'a_w_ukv': dw_ukv, 'a_w_out': dw_out_a, 'b_norm_g': dg_b, 'b_w_in': dw_in_b,
        'b_conv_w': dconv_w, 'b_conv_b': dconv_b, 'b_w_rg': dw_rg, 'b_b_rg': db_rg, 'b_w_ig': dw_ig,
        'b_b_ig': db_ig, 'b_lam': dlam, 'b_w_out': dw_out_b, 'final_norm_g': d_final_g,
    }
    return loss, dh0[x0:], grads, through, swapped


def _chip_major(whole, local_shape, axis):
    if axis is None:
        return jnp.broadcast_to(whole.reshape(1, -1), (N_CHIPS, whole.size))
    shape = list(local_shape)
    g = whole.reshape(shape[:axis] + [N_CHIPS, shape[axis]] + shape[axis + 1:])
    return jnp.moveaxis(g, axis, 0).reshape(N_CHIPS, -1)


def kernel(x, meta_tokens, a_norm_g, a_w_in, a_q_norm_g, a_kv_norm_g, a_w_uq, a_w_ukv, a_w_out, b_norm_g, b_w_in, b_conv_w, b_conv_b, b_w_rg, b_b_rg, b_w_ig, b_b_ig, b_lam, b_w_out, final_norm_g, loss_target, m_meta_tokens, m_a_norm_g, m_a_w_in, m_a_q_norm_g, m_a_kv_norm_g, m_a_w_uq, m_a_w_ukv, m_a_w_out, m_b_norm_g, m_b_w_in, m_b_conv_w, m_b_conv_b, m_b_w_rg, m_b_b_rg, m_b_w_ig, m_b_b_ig, m_b_lam, m_b_w_out, m_final_norm_g, v_meta_tokens, v_a_norm_g, v_a_w_in, v_a_q_norm_g, v_a_kv_norm_g, v_a_w_uq, v_a_w_ukv, v_a_w_out, v_b_norm_g, v_b_w_in, v_b_conv_w, v_b_conv_b, v_b_w_rg, v_b_b_rg, v_b_w_ig, v_b_b_ig, v_b_lam, v_b_w_out, v_final_norm_g):
    local_w = dict(zip(WEIGHTS, (meta_tokens, a_norm_g, a_w_in, a_q_norm_g, a_kv_norm_g, a_w_uq, a_w_ukv, a_w_out,
                                 b_norm_g, b_w_in, b_conv_w, b_conv_b, b_w_rg, b_b_rg, b_w_ig, b_b_ig, b_lam,
                                 b_w_out, final_norm_g)))
    local_m = dict(zip(WEIGHTS, (m_meta_tokens, m_a_norm_g, m_a_w_in, m_a_q_norm_g, m_a_kv_norm_g, m_a_w_uq,
                                 m_a_w_ukv, m_a_w_out, m_b_norm_g, m_b_w_in, m_b_conv_w, m_b_conv_b, m_b_w_rg,
                                 m_b_b_rg, m_b_w_ig, m_b_b_ig, m_b_lam, m_b_w_out, m_final_norm_g)))
    local_v = dict(zip(WEIGHTS, (v_meta_tokens, v_a_norm_g, v_a_w_in, v_a_q_norm_g, v_a_kv_norm_g, v_a_w_uq,
                                 v_a_w_ukv, v_a_w_out, v_b_norm_g, v_b_w_in, v_b_conv_w, v_b_conv_b, v_b_w_rg,
                                 v_b_b_rg, v_b_w_ig, v_b_b_ig, v_b_lam, v_b_w_out, v_final_norm_g)))
    matrices = ('a_w_in', 'a_w_uq', 'a_w_ukv', 'a_w_out', 'b_w_in', 'b_w_rg', 'b_w_ig', 'b_w_out')
    heads = a_w_uq.shape[-1] * N_CHIPS // (QK_NOPE + QK_ROPE)

    split, small, mid, late = [], [], [], []
    for n in WEIGHTS:
        if SHARD_AXIS[n] is None:
            continue
        if n.startswith('b_') or n == 'a_w_out':
            late.append((n, local_w[n].astype(BF16) if n in matrices else local_w[n], SHARD_AXIS[n]))
        elif n == 'a_w_in':
            split.append((n, local_w[n].astype(BF16)[None], 0, 2))
        elif n in matrices:
            mid.append((n, local_w[n].astype(BF16), SHARD_AXIS[n]))
        else:
            small.append((n, local_w[n], SHARD_AXIS[n]))
    gathered = _gather_weights([s[1:] for s in split], [s[1:] for s in small])
    whole = dict(zip([s[0] for s in split + small], gathered))
    whole['a_w_in'] = _unshard(whole['a_w_in'], SHARD_AXIS['a_w_in'])
    mid_names, late_names = [s[0] for s in mid], [s[0] for s in late]
    wt = {n: _kernel_form(n, whole.get(n, local_w[n])) for n in WEIGHTS if n not in mid_names + late_names}

    c_idx = lax.axis_index("c").astype(jnp.int32).reshape(1)
    me_idx = (2 * lax.axis_index("x") + lax.axis_index("y")).astype(jnp.int32).reshape(1)
    loss, grad_x, grads, through, swapped = _local_grads(
        x[0], loss_target[0], wt, heads, c_idx,
        mid_names, _gather_whole("gather_weights_a", [s[1:] for s in mid]),
        late_names, _gather_whole("gather_weights_b", [s[1:] for s in late]))

    ext_uq = heads % N_CHIPS == 0
    started = [gr.name for gr, *_ in through + swapped]
    last = [_matrix_grad(n, grads[n], heads) for n in matrices if n not in started]
    rest = [n for n in WEIGHTS if n not in matrices]
    pieces = [_chip_major(grads[n], local_w[n].shape, SHARD_AXIS[n]) for n in rest]
    pieces.append(jnp.broadcast_to(loss[0:1, 0:1], (N_CHIPS, 1)))
    length = sum(p.shape[1] for p in pieces)
    unit = 2 * SUBLANES * 1024
    padded = -(-length // unit) * unit
    flat = jnp.concatenate(pieces + [jnp.zeros((N_CHIPS, padded - length), F32)], axis=1)
    last.append(_Grad('small', flat.reshape(N_CHIPS, padded // 1024, 1024), 'lead', padded // 2048, 1024))

    got = _halves_to_sibling("grads_to_sibling", last).run()
    swapped = swapped + [(gr, _chip_sum(gr, r, c_idx, F32 if gr.name == 'small' else BF16))
                         for gr, r in zip(last, got)]
    from_chips = _blocks_to_chips("grads_to_chips", [p for _, p in swapped]).run()
    through = through + [(gr, p, r) for (gr, p), r in zip(swapped, from_chips)]
    halves = [_sum_chips(gr.name, p, r, me_idx, c_idx) for gr, p, r in through]
    summed = dict(zip([gr.name for gr, _, _ in through], _share_with_sibling(halves)))
    if ext_uq:
        g = summed['a_w_uq']
        summed['a_w_uq'] = g.reshape(g.shape[0], -1, HEAD_W)[:, :, :QK_NOPE + QK_ROPE]
    total = summed['small'].reshape(-1)

    out_g, out_d, out_m, out_v = [], [], [], []
    off = 0
    for n in WEIGHTS:
        shape = local_w[n].shape
        if n in matrices:
            g = summed[n].reshape(shape)
        else:
            size = 1
            for s in shape:
                size *= s
            g = total[off:off + size].reshape(shape)
            off += size
        delta, new_m, new_v = _adamw("adamw_" + n, _as2d(local_w[n]), _as2d(g), _as2d(local_m[n]), _as2d(local_v[n]))
        out_g.append(g)
        out_d.append(delta.reshape(shape))
        out_m.append(new_m.reshape(shape))
        out_v.append(new_v.reshape(shape))

    return (total[off], grad_x[None], *out_g, *out_d, *out_m, *out_v)
```

```python
import functools

import jax
import jax.numpy as jnp
from jax import lax
from jax.experimental import pallas as pl
from jax.experimental.pallas import tpu as pltpu

F32 = jnp.float32
BF16 = jnp.bfloat16
MESH = pl.DeviceIdType.MESH

RMS_EPS = 1e-6
QK_NOPE = 128
QK_ROPE = 64
V_HEAD = 128
HEAD_W = 256
ROPE_BASE = 10000.0
Q_BLOCK = 128
MASK_VALUE = -1e30
CONV_WIDTH = 4
LRU_C = 8.0
N_CHIPS = 4

ADAM_LR = 0.001
ADAM_B1 = 0.9
ADAM_B2 = 0.999
ADAM_EPS = 1e-08
ADAM_WD = 0.01
ADAM_STEP = 10

VMEM_LIMIT_V7X = 56 * 1024 * 1024
LANES = 128
SUBLANES = 8

WEIGHTS = ['meta_tokens', 'a_norm_g', 'a_w_in', 'a_q_norm_g', 'a_kv_norm_g', 'a_w_uq', 'a_w_ukv',
           'a_w_out', 'b_norm_g', 'b_w_in', 'b_conv_w', 'b_conv_b', 'b_w_rg', 'b_b_rg', 'b_w_ig',
           'b_b_ig', 'b_lam', 'b_w_out', 'final_norm_g']
SHARD_AXIS = {'meta_tokens': 1, 'a_norm_g': None, 'a_w_in': 2, 'a_q_norm_g': None, 'a_kv_norm_g': None,
              'a_w_uq': 2, 'a_w_ukv': 2, 'a_w_out': 1, 'b_norm_g': 1, 'b_w_in': 2, 'b_conv_w': 2,
              'b_conv_b': 1, 'b_w_rg': 2, 'b_b_rg': 1, 'b_w_ig': 2, 'b_b_ig': 1, 'b_lam': 1,
              'b_w_out': 1, 'final_norm_g': None}


def _params(sem=None):
    return pltpu.CompilerParams(dimension_semantics=sem, vmem_limit_bytes=VMEM_LIMIT_V7X)


def _row_tile(tp):
    return 384 if (tp % 384 == 0 and tp >= 1152) else 128


def _sigmoid(x):
    return 1.0 / (1.0 + jnp.exp(-x))


def _rms(x):
    return lax.rsqrt(jnp.mean(x * x, axis=-1, keepdims=True) + RMS_EPS)


def _swap32(x):
    lane = lax.broadcasted_iota(jnp.int32, x.shape, 1)
    return jnp.where(lane < 32, pltpu.roll(x, 96, 1), pltpu.roll(x, 32, 1))


def _rope(x, cos_t, sin_t):
    return x * cos_t + _swap32(x) * sin_t


def _unrope(d, cos_t, sin_t):
    lane = lax.broadcasted_iota(jnp.int32, d.shape, 1)
    return jnp.where(lane < QK_ROPE, d * cos_t + _swap32(d * sin_t), 0.0)


def _dot(a, b):
    return jnp.dot(a, b, preferred_element_type=F32)


def _dot_nt(a, b):
    return lax.dot_general(a, b, (((1,), (1,)), ((), ())), preferred_element_type=F32)


def _dot_tn(a, b):
    return lax.dot_general(a, b, (((0,), (0,)), ((), ())), preferred_element_type=F32)


def _norm_matmul(name, x, g, w, splits, tm, exchange=None):
    tp, kin = x.shape
    n = w.shape[1]
    nt = tp // tm

    def compute(x_ref, g_ref, w_ref, *outs):
        xv = x_ref[...]
        nrm = ((xv * _rms(xv)) * g_ref[...]).astype(BF16)
        y = _dot(nrm, w_ref[...])
        for o_ref, (c0, c1) in zip(outs, splits):
            o_ref[...] = y[:, c0:c1]

    in_specs = [pl.BlockSpec((tm, kin), lambda i: (i, 0)),
                pl.BlockSpec((1, kin), lambda i: (0, 0)),
                pl.BlockSpec((kin, n), lambda i: (0, 0))]
    out_specs = [pl.BlockSpec((tm, c1 - c0), lambda i: (i, 0)) for c0, c1 in splits]
    out_shape = [jax.ShapeDtypeStruct((tp, c1 - c0), F32) for c0, c1 in splits]
    args, scratch = [x, g, w], []
    body = _with_exchange(exchange, len(in_specs), len(out_specs),
                          lambda: pl.program_id(0) == 0, lambda: pl.program_id(0) == nt - 1, compute)
    if exchange is not None:
        in_specs = in_specs + exchange.specs(exchange.n_in)
        out_specs = out_specs + exchange.specs(exchange.n_out)
        out_shape = out_shape + exchange.out_shapes
        args, scratch = args + exchange.arrays, exchange.scratch()
    return pl.pallas_call(
        body, name=name, grid=(nt,),
        in_specs=in_specs, out_specs=out_specs, out_shape=out_shape, scratch_shapes=scratch,
        compiler_params=_params(("arbitrary",)),
    )(*args)


def _q_proj(q_lat, g, w_uq, cos_t, sin_t, heads, tm):
    tp, kin = q_lat.shape
    n = heads * HEAD_W

    def body(x_ref, g_ref, w_ref, cos_ref, sin_ref, q_ref):
        xv = x_ref[...]
        nrm = ((xv * _rms(xv)) * g_ref[...]).astype(BF16)
        y = _dot(nrm, w_ref[...])
        cos_v, sin_v = cos_ref[...], sin_ref[...]
        for h in range(heads):
            c0 = h * HEAD_W
            q_ref[:, c0:c0 + QK_NOPE] = y[:, c0:c0 + QK_NOPE].astype(BF16)
            q_ref[:, c0 + QK_NOPE:c0 + HEAD_W] = _rope(y[:, c0 + QK_NOPE:c0 + HEAD_W], cos_v, sin_v).astype(BF16)

    return pl.pallas_call(
        body, name="a_q_proj", grid=(tp // tm,),
        in_specs=[pl.BlockSpec((tm, kin), lambda i: (i, 0)),
                  pl.BlockSpec((1, kin), lambda i: (0, 0)),
                  pl.BlockSpec((kin, n), lambda i: (0, 0)),
                  pl.BlockSpec((tm, LANES), lambda i: (i, 0)),
                  pl.BlockSpec((tm, LANES), lambda i: (i, 0))],
        out_specs=pl.BlockSpec((tm, n), lambda i: (i, 0)),
        out_shape=jax.ShapeDtypeStruct((tp, n), BF16),
        compiler_params=_params(("parallel",)),
    )(q_lat, g, w_uq, cos_t, sin_t)


def _kv_proj(kv_lat, g, w_ukv, k_rope_raw, cos_t, sin_t, heads, tm):
    tp, kin = kv_lat.shape
    n = heads * (QK_NOPE + V_HEAD)

    def body(x_ref, g_ref, w_ref, kr_ref, cos_ref, sin_ref, k_ref, v_ref):
        xv = x_ref[...]
        nrm = ((xv * _rms(xv)) * g_ref[...]).astype(BF16)
        y = _dot(nrm, w_ref[...])
        kr = _rope(kr_ref[...], cos_ref[...], sin_ref[...]).astype(BF16)
        for h in range(heads):
            c0 = h * (QK_NOPE + V_HEAD)
            k_ref[:, h * HEAD_W:h * HEAD_W + QK_NOPE] = y[:, c0:c0 + QK_NOPE].astype(BF16)
            k_ref[:, h * HEAD_W + QK_NOPE:(h + 1) * HEAD_W] = kr
            v_ref[:, h * V_HEAD:(h + 1) * V_HEAD] = y[:, c0 + QK_NOPE:c0 + QK_NOPE + V_HEAD].astype(BF16)

    return pl.pallas_call(
        body, name="a_kv_proj", grid=(tp // tm,),
        in_specs=[pl.BlockSpec((tm, kin), lambda i: (i, 0)),
                  pl.BlockSpec((1, kin), lambda i: (0, 0)),
                  pl.BlockSpec((kin, n), lambda i: (0, 0)),
                  pl.BlockSpec((tm, LANES), lambda i: (i, 0)),
                  pl.BlockSpec((tm, LANES), lambda i: (i, 0)),
                  pl.BlockSpec((tm, LANES), lambda i: (i, 0))],
        out_specs=[pl.BlockSpec((tm, heads * HEAD_W), lambda i: (i, 0)),
                   pl.BlockSpec((tm, heads * V_HEAD), lambda i: (i, 0))],
        out_shape=[jax.ShapeDtypeStruct((tp, heads * HEAD_W), BF16),
                   jax.ShapeDtypeStruct((tp, heads * V_HEAD), BF16)],
        compiler_params=_params(("parallel",)),
    )(kv_lat, g, w_ukv, k_rope_raw, cos_t, sin_t)


def _as_rows(col):
    rows = col.shape[0]
    return jnp.transpose(jnp.broadcast_to(col, (rows, LANES)))[0:SUBLANES, :]


def _attn_mask(row0, col0, rows, cols, pad):
    row = row0 + lax.broadcasted_iota(jnp.int32, (rows, cols), 0)
    col = col0 + lax.broadcasted_iota(jnp.int32, (rows, cols), 1)
    return (col <= row) & (col >= pad)


LOG2E = 1.4426950408889634
FLASH_FWD_TRIPS = ((4, 2), (2, 2), (1, 1))


def _flash_fwd(q, k, v, heads, pad, tq, exchange=None):
    tp = q.shape[0]
    nq = tp // tq
    c2 = (QK_NOPE + QK_ROPE) ** -0.5 * LOG2E

    def compute(q_ref, k_ref, v_ref, o_ref, lse_ref):
        i = pl.program_id(1)

        def make_step(masked, blocks, parts=1):
            keys = blocks * tq // parts

            def step(j, carry):
                m, l, acc = carry
                offs = [pl.multiple_of(j * tq + part * keys, tq) for part in range(parts)]
                scores = [_dot_nt(q_ref[...], k_ref[pl.ds(off, keys), :]) for off in offs]
                for off, s in zip(offs, scores):
                    s = s * c2
                    if masked:
                        s = jnp.where(_attn_mask(i * tq, j * tq, tq, keys, pad), s, MASK_VALUE)
                    m_new = jnp.maximum(m, jnp.max(s, axis=-1, keepdims=True))
                    p = jnp.exp2(s - m_new)
                    alpha = jnp.exp2(m - m_new)
                    l = alpha * l + jnp.sum(p, axis=-1, keepdims=True)
                    acc = alpha * acc + _dot(p.astype(BF16), v_ref[pl.ds(off, keys), :])
                    m = m_new
                return m, l, acc
            return step

        init = (jnp.full((tq, 1), MASK_VALUE, F32), jnp.zeros((tq, 1), F32), jnp.zeros((tq, V_HEAD), F32))
        carry = make_step(True, 1)(0, init)
        first = 1
        for blocks, parts in FLASH_FWD_TRIPS:
            trips = jnp.maximum(i - first, 0) // blocks
            step_n = make_step(False, blocks, parts)
            carry = lax.fori_loop(0, trips, lambda t, cr, f=first, b=blocks, s=step_n: s(f + b * t, cr), carry)
            first = first + blocks * trips
        m, l, acc = lax.fori_loop(jnp.maximum(i, 1), i + 1, make_step(True, 1), carry)
        o_ref[...] = acc / l
        lse_ref[...] = _as_rows(m + jnp.log(l) * LOG2E)

    in_specs = [pl.BlockSpec((tq, HEAD_W), lambda h, i: (i, h)),
                pl.BlockSpec((tp, HEAD_W), lambda h, i: (0, h)),
                pl.BlockSpec((tp, V_HEAD), lambda h, i: (0, h))]
    out_specs = [pl.BlockSpec((tq, V_HEAD), lambda h, i: (i, h)),
                 pl.BlockSpec((None, None, SUBLANES, tq), lambda h, i: (h, i, 0, 0))]
    out_shape = [jax.ShapeDtypeStruct((tp, heads * V_HEAD), F32),
                 jax.ShapeDtypeStruct((heads, nq, SUBLANES, tq), F32)]
    args, scratch = [q, k, v], []
    body = _with_exchange(exchange, len(in_specs), len(out_specs),
                          lambda: (pl.program_id(0) == 0) & (pl.program_id(1) == 0),
                          lambda: (pl.program_id(0) == heads - 1) & (pl.program_id(1) == nq - 1), compute)
    if exchange is not None:
        in_specs = in_specs + exchange.specs(exchange.n_in)
        out_specs = out_specs + exchange.specs(exchange.n_out)
        out_shape = out_shape + exchange.out_shapes
        args, scratch = args + exchange.arrays, exchange.scratch()
    return pl.pallas_call(
        body, name="a_flash_fwd", grid=(heads, nq),
        in_specs=in_specs, out_specs=out_specs, out_shape=out_shape, scratch_shapes=scratch,
        compiler_params=_params(("arbitrary", "arbitrary")),
    )(*args)


def _out_proj_in_proj(name, a, gate, w_out, resid, g, w_in, splits, tm):
    tp, wd = a.shape
    d = w_out.shape[1]
    n = w_in.shape[1]

    def body(a_ref, gate_ref, wo_ref, res_ref, g_ref, wi_ref, h_ref, *outs):
        gv = gate_ref[...]
        y = (a_ref[...] * (gv * _sigmoid(gv))).astype(BF16)
        h = res_ref[...] + _dot(y, wo_ref[...])
        h_ref[...] = h
        nrm = ((h * _rms(h)) * g_ref[...]).astype(BF16)
        z = _dot(nrm, wi_ref[...])
        for o_ref, (c0, c1) in zip(outs, splits):
            o_ref[...] = z[:, c0:c1]

    return pl.pallas_call(
        body, name=name, grid=(tp // tm,),
        in_specs=[pl.BlockSpec((tm, wd), lambda i: (i, 0)),
                  pl.BlockSpec((tm, wd), lambda i: (i, 0)),
                  pl.BlockSpec((wd, d), lambda i: (0, 0)),
                  pl.BlockSpec((tm, d), lambda i: (i, 0)),
                  pl.BlockSpec((1, d), lambda i: (0, 0)),
                  pl.BlockSpec((d, n), lambda i: (0, 0))],
        out_specs=[pl.BlockSpec((tm, d), lambda i: (i, 0))]
        + [pl.BlockSpec((tm, c1 - c0), lambda i: (i, 0)) for c0, c1 in splits],
        out_shape=[jax.ShapeDtypeStruct((tp, d), F32)]
        + [jax.ShapeDtypeStruct((tp, c1 - c0), F32) for c0, c1 in splits],
        compiler_params=_params(("parallel",)),
    )(a, gate, w_out, resid, g, w_in)


def _lru_decay(r, sp):
    log_a = -LRU_C * r * sp
    a = jnp.exp(log_a)
    e2 = a * a
    x2 = 2.0 * log_a
    series = x2 * (1.0 + x2 * (0.5 + x2 * (1.0 / 6.0)))
    em1 = jnp.where(x2 > -0.02, series, e2 - 1.0)
    return a, e2, jnp.sqrt(-em1)


def _softplus(x):
    return jnp.maximum(x, 0.0) + jnp.log1p(jnp.exp(-jnp.abs(x)))


def _rglru_fwd(u, conv_w, conv_b, w_rg, b_rg, w_ig, b_ig, lam, pad, tm):
    tp, w = u.shape
    groups, blk = w_rg.shape[0], w_rg.shape[1]

    def body(u_ref, cw_ref, cb_ref, wr_ref, br_ref, wi_ref, bi_ref, lam_ref,
             uc_ref, r_ref, ig_ref, hs_ref, a_s, mult_ref, uext, b_s, hc):
        i = pl.program_id(0)

        @pl.when(i == 0)
        def _():
            uext[0:SUBLANES, :] = jnp.zeros((SUBLANES, w), F32)
            hc[...] = jnp.zeros((SUBLANES, w), F32)

        uext[SUBLANES:SUBLANES + tm, :] = u_ref[...]
        cw = cw_ref[...]
        uc = cb_ref[...] + uext[pl.ds(SUBLANES - 3, tm), :] * cw[0:1, :]
        uc = uc + uext[pl.ds(SUBLANES - 2, tm), :] * cw[1:2, :]
        uc = uc + uext[pl.ds(SUBLANES - 1, tm), :] * cw[2:3, :]
        uc = uc + uext[pl.ds(SUBLANES, tm), :] * cw[3:4, :]
        uc_ref[...] = uc
        uext[0:SUBLANES, :] = uext[tm:tm + SUBLANES, :]

        sp = _softplus(-lam_ref[...])
        for g in range(groups):
            sl = slice(g * blk, (g + 1) * blk)
            ucg = uc_ref[:, sl]
            ucb = ucg.astype(BF16)
            r = _sigmoid(_dot(ucb, wr_ref[g]) + br_ref[:, sl])
            ig = _sigmoid(_dot(ucb, wi_ref[g]) + bi_ref[:, sl])
            r_ref[:, sl] = r
            ig_ref[:, sl] = ig
            a, _, mult = _lru_decay(r, sp[:, sl])
            a_s[:, sl] = a
            mult_ref[:, sl] = mult
            b_s[:, sl] = mult * (ig * ucg)

        @pl.when(i == 0)
        def _():
            row = lax.broadcasted_iota(jnp.int32, (Q_BLOCK, w), 0)
            start = ig_ref[0:Q_BLOCK, :] * uc_ref[0:Q_BLOCK, :]
            b_s[0:Q_BLOCK, :] = jnp.where(row < pad, 0.0, jnp.where(row == pad, start, b_s[0:Q_BLOCK, :]))
            mult_ref[0:Q_BLOCK, :] = jnp.where(row == pad, 1.0, mult_ref[0:Q_BLOCK, :])

        row8 = lax.broadcasted_iota(jnp.int32, (SUBLANES, w), 0)

        def group(gi, h_in):
            off = pl.multiple_of(gi * SUBLANES, SUBLANES)
            av = a_s[pl.ds(off, SUBLANES), :]
            bv = b_s[pl.ds(off, SUBLANES), :]
            for k in (1, 2, 4):
                keep = row8 >= k
                bv = jnp.where(keep, av * pltpu.roll(bv, k, 0) + bv, bv)
                av = jnp.where(keep, av * pltpu.roll(av, k, 0), av)
            hv = av * h_in + bv
            hs_ref[pl.ds(off, SUBLANES), :] = hv
            return jnp.broadcast_to(hv[SUBLANES - 1:SUBLANES, :], (SUBLANES, w))

        hc[...] = lax.fori_loop(0, tm // SUBLANES, group, hc[...])

    row_spec = pl.BlockSpec((tm, w), lambda i: (i, 0))
    vec_spec = pl.BlockSpec((1, w), lambda i: (0, 0))
    mat_spec = pl.BlockSpec((groups, blk, blk), lambda i: (0, 0, 0))
    return pl.pallas_call(
        body, name="b_rglru_fwd", grid=(tp // tm,),
        in_specs=[row_spec, pl.BlockSpec((CONV_WIDTH, w), lambda i: (0, 0)), vec_spec,
                  mat_spec, vec_spec, mat_spec, vec_spec, vec_spec],
        out_specs=[row_spec] * 6,
        out_shape=[jax.ShapeDtypeStruct((tp, w), F32)] * 6,
        scratch_shapes=[pltpu.VMEM((tm + SUBLANES, w), F32), pltpu.VMEM((tm, w), F32),
                        pltpu.VMEM((SUBLANES, w), F32)],
        compiler_params=_params(("arbitrary",)),
    )(u, conv_w, conv_b, w_rg, b_rg, w_ig, b_ig, lam)


def _out_proj_loss(a, gate, w, resid, g, target, x0, tm):
    tp, wd = a.shape
    d = w.shape[1]
    assert x0 % Q_BLOCK == 0 and tm % Q_BLOCK == 0 and target.shape[0] == tp - x0
    lead = x0 // Q_BLOCK
    per = tm // Q_BLOCK

    def body(a_ref, gate_ref, w_ref, res_ref, g_ref, *rest):
        t_refs, (dh_ref, loss_ref, dg_ref, da_ref, dgate_ref, dw_ref) = rest[:per], rest[per:]
        i = pl.program_id(0)

        @pl.when(i == 0)
        def _():
            loss_ref[...] = jnp.zeros_like(loss_ref)
            dg_ref[...] = jnp.zeros_like(dg_ref)
            dw_ref[...] = jnp.zeros_like(dw_ref)

        gate_v = gate_ref[...]
        av = a_ref[...]
        sg = _sigmoid(gate_v)
        silu = gate_v * sg
        y = (av * silu).astype(BF16)
        h = res_ref[...] + _dot(y, w_ref[...])
        gv = g_ref[...]
        for b in range(per):
            rows = slice(b * Q_BLOCK, (b + 1) * Q_BLOCK)
            xv = h[rows, :]
            r = _rms(xv)
            xh = xv * r
            err = jnp.where(i * per + b >= lead, xh * gv - t_refs[b][...], 0.0)
            loss_ref[...] += 0.5 * jnp.sum(jnp.mean(err * err, axis=-1, keepdims=True))
            dy = err / d
            dg_ref[...] += jnp.sum(dy * xh, axis=0, keepdims=True)
            dxh = dy * gv
            dh_ref[rows, :] = r * (dxh - xh * jnp.mean(dxh * xh, axis=-1, keepdims=True))

        dob = dh_ref[...].astype(BF16)
        dyv = _dot_nt(dob, w_ref[...])
        da_ref[...] = dyv * silu
        dgate_ref[...] = dyv * av * (sg * (1.0 + gate_v * (1.0 - sg)))
        dw_ref[...] += _dot_tn(y, dob)

    def piece(b):
        return pl.BlockSpec((Q_BLOCK, d), lambda i: (jnp.maximum(i * per + b - lead, 0), 0))

    return pl.pallas_call(
        body, name="b_out_proj_loss", grid=(tp // tm,),
        in_specs=[pl.BlockSpec((tm, wd), lambda i: (i, 0)),
                  pl.BlockSpec((tm, wd), lambda i: (i, 0)),
                  pl.BlockSpec((wd, d), lambda i: (0, 0)),
                  pl.BlockSpec((tm, d), lambda i: (i, 0)),
                  pl.BlockSpec((1, d), lambda i: (0, 0))] + [piece(b) for b in range(per)],
        out_specs=[pl.BlockSpec((tm, d), lambda i: (i, 0)),
                   pl.BlockSpec((SUBLANES, LANES), lambda i: (0, 0)),
                   pl.BlockSpec((1, d), lambda i: (0, 0)),
                   pl.BlockSpec((tm, wd), lambda i: (i, 0)),
                   pl.BlockSpec((tm, wd), lambda i: (i, 0)),
                   pl.BlockSpec((wd, d), lambda i: (0, 0))],
        out_shape=[jax.ShapeDtypeStruct((tp, d), F32),
                   jax.ShapeDtypeStruct((SUBLANES, LANES), F32),
                   jax.ShapeDtypeStruct((1, d), F32),
                   jax.ShapeDtypeStruct((tp, wd), F32),
                   jax.ShapeDtypeStruct((tp, wd), F32),
                   jax.ShapeDtypeStruct((wd, d), F32)],
        compiler_params=_params(("arbitrary",)),
    )(a, gate, w, resid, g, *([target] * per))


def _with_exchange(exchange, n_in, n_out, first, last, compute):
    if exchange is None:
        return compute
    ex_in, ex_out = exchange.n_in, exchange.n_out

    def body(*refs):
        own_in, their_in = refs[:n_in], refs[n_in:n_in + ex_in]
        pos = n_in + ex_in
        own_out, their_out = refs[pos:pos + n_out], refs[pos + n_out:pos + n_out + ex_out]
        rest = refs[pos + n_out + ex_out:]
        own_scratch, sems = rest[:len(rest) - 2], rest[len(rest) - 2:]

        @pl.when(first())
        def _():
            exchange.start(their_in, their_out, sems)

        compute(*own_in, *own_out, *own_scratch)

        @pl.when(last())
        def _():
            exchange.finish(their_in, their_out, sems)

    return body


def _gated_out_bwd(name, dout, a, gate, w, tm, delta_heads=0, exchange=None):
    tp, wd = a.shape
    d = w.shape[1]
    nt = tp // tm

    def compute(do_ref, a_ref, gate_ref, w_ref, da_ref, dgate_ref, dw_ref, *delta_ref):
        @pl.when(pl.program_id(0) == 0)
        def _():
            dw_ref[...] = jnp.zeros_like(dw_ref)

        gv = gate_ref[...]
        av = a_ref[...]
        sg = _sigmoid(gv)
        silu = gv * sg
        dob = do_ref[...].astype(BF16)
        dy = _dot_nt(dob, w_ref[...])
        da = dy * silu
        da_ref[...] = da
        dgate_ref[...] = dy * av * (sg * (1.0 + gv * (1.0 - sg)))
        dw_ref[...] += _dot_tn((av * silu).astype(BF16), dob)
        for h in range(delta_heads):
            sl = slice(h * V_HEAD, (h + 1) * V_HEAD)
            delta_ref[0][h] = _as_rows(jnp.sum(da[:, sl] * av[:, sl], axis=-1, keepdims=True))

    out_specs = [pl.BlockSpec((tm, wd), lambda i: (i, 0)),
                 pl.BlockSpec((tm, wd), lambda i: (i, 0)),
                 pl.BlockSpec((wd, d), lambda i: (0, 0))]
    out_shape = [jax.ShapeDtypeStruct((tp, wd), F32),
                 jax.ShapeDtypeStruct((tp, wd), F32),
                 jax.ShapeDtypeStruct((wd, d), F32)]
    if delta_heads:
        out_specs.append(pl.BlockSpec((delta_heads, None, SUBLANES, tm), lambda i: (0, i, 0, 0)))
        out_shape.append(jax.ShapeDtypeStruct((delta_heads, tp // tm, SUBLANES, tm), F32))
    in_specs = [pl.BlockSpec((tm, d), lambda i: (i, 0)),
                pl.BlockSpec((tm, wd), lambda i: (i, 0)),
                pl.BlockSpec((tm, wd), lambda i: (i, 0)),
                pl.BlockSpec((wd, d), lambda i: (0, 0))]
    args, scratch = [dout, a, gate, w], []
    body = _with_exchange(exchange, len(in_specs), len(out_specs),
                          lambda: pl.program_id(0) == 0, lambda: pl.program_id(0) == nt - 1, compute)
    if exchange is not None:
        in_specs = in_specs + exchange.specs(exchange.n_in)
        out_specs = out_specs + exchange.specs(exchange.n_out)
        out_shape = out_shape + exchange.out_shapes
        args, scratch = args + exchange.arrays, exchange.scratch()
    return pl.pallas_call(
        body, name=name, grid=(nt,),
        in_specs=in_specs, out_specs=out_specs, out_shape=out_shape, scratch_shapes=scratch,
        compiler_params=_params(("arbitrary",)),
    )(*args)


def _rglru_bwd(dhs, hs, r, ig, uc, u, a, mult, conv_w, w_rg, w_ig, lam, pad, tm):
    tp, w = u.shape
    groups, blk = w_rg.shape[0], w_rg.shape[1]
    nt = tp // tm
    per8 = tm // SUBLANES

    def body(dhs_ref, hs_ref, hprev_ref, r_ref, ig_ref, uc_ref, u_ref, uprev_ref, a_ref, mult_ref,
             cw_ref, wr_ref, wi_ref, lam_ref,
             du_ref, dcw_ref, dcb_ref, dwr_ref, dbr_ref, dwi_ref, dbi_ref, dlam_ref,
             aext, c_s, g_s, hext, uext, ducext, gc):
        step = pl.program_id(0)
        ti = nt - 1 - step

        @pl.when(step == 0)
        def _():
            for ref in (dcw_ref, dcb_ref, dwr_ref, dbr_ref, dwi_ref, dbi_ref, dlam_ref):
                ref[...] = jnp.zeros_like(ref)
            aext[tm:tm + SUBLANES, :] = jnp.zeros((SUBLANES, w), F32)
            ducext[tm:tm + SUBLANES, :] = jnp.zeros((SUBLANES, w), F32)
            gc[...] = jnp.zeros((SUBLANES, w), F32)

        lam_v = lam_ref[...]
        sp = _softplus(-lam_v)
        row = ti * tm + lax.broadcasted_iota(jnp.int32, (tm, w), 0)

        rv = r_ref[...]
        a = a_ref[...]
        mult = mult_ref[...]
        aext[0:tm, :] = a
        c_s[...] = aext[pl.ds(1, tm), :]
        row8 = lax.broadcasted_iota(jnp.int32, (SUBLANES, w), 0)

        def group(gi, g_in):
            off = pl.multiple_of((per8 - 1 - gi) * SUBLANES, SUBLANES)
            cv = c_s[pl.ds(off, SUBLANES), :]
            dv = dhs_ref[pl.ds(off, SUBLANES), :]
            for k in (1, 2, 4):
                keep = row8 < SUBLANES - k
                dv = jnp.where(keep, cv * pltpu.roll(dv, SUBLANES - k, 0) + dv, dv)
                cv = jnp.where(keep, cv * pltpu.roll(cv, SUBLANES - k, 0), cv)
            gv = cv * g_in + dv
            g_s[pl.ds(off, SUBLANES), :] = gv
            return jnp.broadcast_to(gv[0:1, :], (SUBLANES, w))

        gc[...] = lax.fori_loop(0, per8, group, gc[...])
        aext[tm:tm + SUBLANES, :] = aext[0:SUBLANES, :]

        gsc = jnp.where(row < pad, 0.0, g_s[...])
        hext[0:SUBLANES, :] = hprev_ref[...]
        hext[SUBLANES:SUBLANES + tm, :] = hs_ref[...]
        hprev = jnp.where(row == 0, 0.0, hext[pl.ds(SUBLANES - 1, tm), :])
        igv = ig_ref[...]
        ucv = uc_ref[...]
        first = row == pad
        dmult = gsc * (igv * ucv)
        dig = gsc * mult * ucv
        duc = gsc * mult * igv
        dlog_a = (gsc * hprev) * a + jnp.where(first, 0.0, dmult * (-(a * a) / mult))
        dlam_ref[...] += jnp.sum(dlog_a * rv, axis=0, keepdims=True) * (LRU_C * _sigmoid(-lam_v))
        dpre_r = dlog_a * (-LRU_C * sp) * (rv * (1.0 - rv))
        dpre_i = dig * (igv * (1.0 - igv))
        dbr_ref[...] += jnp.sum(dpre_r, axis=0, keepdims=True)
        dbi_ref[...] += jnp.sum(dpre_i, axis=0, keepdims=True)
        for g in range(groups):
            sl = slice(g * blk, (g + 1) * blk)
            ucb = ucv[:, sl].astype(BF16)
            drb = dpre_r[:, sl].astype(BF16)
            dib = dpre_i[:, sl].astype(BF16)
            dwr_ref[g] += _dot_tn(ucb, drb)
            dwi_ref[g] += _dot_tn(ucb, dib)
            ducext[0:tm, sl] = duc[:, sl] + _dot_nt(drb, wr_ref[g]) + _dot_nt(dib, wi_ref[g])

        ducv = ducext[0:tm, :]
        cw = cw_ref[...]
        dcb_ref[...] += jnp.sum(ducv, axis=0, keepdims=True)
        uext[0:SUBLANES, :] = jnp.where(ti == 0, 0.0, uprev_ref[...])
        uext[SUBLANES:SUBLANES + tm, :] = u_ref[...]
        for j in range(CONV_WIDTH):
            ush = uext[pl.ds(SUBLANES - (CONV_WIDTH - 1 - j), tm), :]
            dcw_ref[j:j + 1, :] += jnp.sum(ducv * ush, axis=0, keepdims=True)
        du = ducv * cw[3:4, :]
        for k in range(1, CONV_WIDTH):
            du = du + ducext[pl.ds(k, tm), :] * cw[3 - k:4 - k, :]
        du_ref[...] = du
        ducext[tm:tm + SUBLANES, :] = ducext[0:SUBLANES, :]

    rev = lambda s: (nt - 1 - s, 0)
    halo = lambda s: (jnp.maximum((nt - 1 - s) * per8 - 1, 0), 0)
    row_spec = pl.BlockSpec((tm, w), rev)
    halo_spec = pl.BlockSpec((SUBLANES, w), halo)
    vec_spec = pl.BlockSpec((1, w), lambda s: (0, 0))
    mat_spec = pl.BlockSpec((groups, blk, blk), lambda s: (0, 0, 0))
    cw_spec = pl.BlockSpec((CONV_WIDTH, w), lambda s: (0, 0))
    return pl.pallas_call(
        body, name="b_rglru_bwd", grid=(nt,),
        in_specs=[row_spec, row_spec, halo_spec, row_spec, row_spec, row_spec, row_spec, halo_spec, row_spec, row_spec,
                  cw_spec, mat_spec, mat_spec, vec_spec],
        out_specs=[row_spec, cw_spec, vec_spec, mat_spec, vec_spec, mat_spec, vec_spec, vec_spec],
        out_shape=[jax.ShapeDtypeStruct((tp, w), F32), jax.ShapeDtypeStruct((CONV_WIDTH, w), F32),
                   jax.ShapeDtypeStruct((1, w), F32), jax.ShapeDtypeStruct((groups, blk, blk), F32),
                   jax.ShapeDtypeStruct((1, w), F32), jax.ShapeDtypeStruct((groups, blk, blk), F32),
                   jax.ShapeDtypeStruct((1, w), F32), jax.ShapeDtypeStruct((1, w), F32)],
        scratch_shapes=[pltpu.VMEM((tm + SUBLANES, w), F32), pltpu.VMEM((tm, w), F32), pltpu.VMEM((tm, w), F32),
                        pltpu.VMEM((tm + SUBLANES, w), F32), pltpu.VMEM((tm + SUBLANES, w), F32),
                        pltpu.VMEM((tm + SUBLANES, w), F32), pltpu.VMEM((SUBLANES, w), F32)],
        compiler_params=_params(("arbitrary",)),
    )(dhs, hs, hs, r, ig, uc, u, u, a, mult, conv_w, w_rg, w_ig, lam)


def _norm_matmul_bwd(name, x, g, w, dys, tm, resid=None, prologue=None, extra_out=None, exchange=None):
    tp, kin = x.shape
    n = w.shape[1]
    nt = tp // tm
    n_dy = len(dys)
    has_res = resid is not None
    has_extra = extra_out is not None

    def compute(*refs):
        x_ref, g_ref, w_ref = refs[:3]
        dy_refs = refs[3:3 + n_dy]
        pos = 3 + n_dy
        res_ref = refs[pos] if has_res else None
        pos += int(has_res)
        dx_ref, dw_ref, dg_ref = refs[pos:pos + 3]
        pos += 3
        ex_ref = refs[pos] if has_extra else None
        pos += int(has_extra)
        dy_s = refs[pos]

        @pl.when(pl.program_id(0) == 0)
        def _():
            dw_ref[...] = jnp.zeros_like(dw_ref)
            dg_ref[...] = jnp.zeros_like(dg_ref)

        if prologue is None:
            c0 = 0
            for ref in dy_refs:
                dy_s[:, c0:c0 + ref.shape[1]] = ref[...].astype(BF16)
                c0 += ref.shape[1]
        else:
            prologue(dy_refs, dy_s, ex_ref)

        xv = x_ref[...]
        gv = g_ref[...]
        r = _rms(xv)
        xh = xv * r
        dyb = dy_s[...]
        dn = _dot_nt(dyb, w_ref[...])
        dw_ref[...] += _dot_tn((xh * gv).astype(BF16), dyb)
        dg_ref[...] += jnp.sum(dn * xh, axis=0, keepdims=True)
        dxh = dn * gv
        dx = r * (dxh - xh * jnp.mean(dxh * xh, axis=-1, keepdims=True))
        if has_res:
            dx = dx + res_ref[...]
        dx_ref[...] = dx

    row = lambda width: pl.BlockSpec((tm, width), lambda i: (i, 0))
    in_specs = [row(kin), pl.BlockSpec((1, kin), lambda i: (0, 0)), pl.BlockSpec((kin, n), lambda i: (0, 0))]
    in_specs += [row(a.shape[1]) for a in dys]
    args = [x, g, w, *dys]
    if has_res:
        in_specs.append(row(kin))
        args.append(resid)
    out_specs = [row(kin), pl.BlockSpec((kin, n), lambda i: (0, 0)), pl.BlockSpec((1, kin), lambda i: (0, 0))]
    out_shape = [jax.ShapeDtypeStruct((tp, kin), F32), jax.ShapeDtypeStruct((kin, n), F32),
                 jax.ShapeDtypeStruct((1, kin), F32)]
    if has_extra:
        out_specs.append(row(extra_out[0]))
        out_shape.append(jax.ShapeDtypeStruct((tp, extra_out[0]), extra_out[1]))
    scratch = [pltpu.VMEM((tm, n), BF16)]
    body = _with_exchange(exchange, len(in_specs), len(out_specs),
                          lambda: pl.program_id(0) == 0, lambda: pl.program_id(0) == nt - 1, compute)
    if exchange is not None:
        in_specs = in_specs + exchange.specs(exchange.n_in)
        out_specs = out_specs + exchange.specs(exchange.n_out)
        out_shape = out_shape + exchange.out_shapes
        args, scratch = args + exchange.arrays, scratch + exchange.scratch()
    return pl.pallas_call(
        body, name=name, grid=(nt,),
        in_specs=in_specs, out_specs=out_specs, out_shape=out_shape, scratch_shapes=scratch,
        compiler_params=_params(("arbitrary",)),
    )(*args)


def _flash_bwd(q, k, v, lse, delta, do, heads, pad, tq, exchange=None):
    tp = q.shape[0]
    nq = tp // tq
    scale = (QK_NOPE + QK_ROPE) ** -0.5
    c2 = scale * LOG2E

    def compute(q_ref, k_ref, v_ref, lse_ref, delta_ref, do_ref, dq_ref, dk_ref, dv_ref):
        j = pl.program_id(1)

        @pl.when(j == 0)
        def _():
            dq_ref[...] = jnp.zeros_like(dq_ref)

        kv = k_ref[...]
        vv = v_ref[...]

        def rows_of(ref, i, blocks):
            parts = [ref[i + b][0:1, :] for b in range(blocks)]
            return parts[0] if blocks == 1 else jnp.concatenate(parts, axis=1)

        def make_step(masked, blocks):
            def step(i, carry):
                off = pl.multiple_of(i * tq, tq)
                qv = q_ref[pl.ds(off, blocks * tq), :]
                dob = do_ref[pl.ds(off, blocks * tq), :].astype(BF16)
                p = jnp.exp2(_dot_nt(kv, qv) * c2 - rows_of(lse_ref, i, blocks))
                if masked:
                    key = j * tq + lax.broadcasted_iota(jnp.int32, (tq, tq), 0)
                    qry = j * tq + lax.broadcasted_iota(jnp.int32, (tq, tq), 1)
                    first = jnp.where((key <= qry) & (key >= pad), p[:, :tq], 0.0)
                    p = first if blocks == 1 else jnp.concatenate([first, p[:, tq:]], axis=1)
                dv_ref[...] += _dot(p.astype(BF16), dob)
                dp = _dot_nt(vv, dob)
                ds = (p * (dp - rows_of(delta_ref, i, blocks)) * scale).astype(BF16)
                dk_ref[...] += _dot(ds, qv)
                dq_ref[pl.ds(off, blocks * tq), :] += _dot_tn(ds, kv)
                return carry
            return step

        dk_ref[...] = jnp.zeros_like(dk_ref)
        dv_ref[...] = jnp.zeros_like(dv_ref)
        odd = (nq - j) % 2
        lax.fori_loop(0, odd, lambda t, cr: make_step(True, 1)(j, cr), 0)
        lax.fori_loop(0, 1 - odd, lambda t, cr: make_step(True, 2)(j, cr), 0)
        start = j + 2 - odd
        for blocks in (4, 2):
            trips = (nq - start) // blocks
            step_n = make_step(False, blocks)
            lax.fori_loop(0, trips, lambda t, cr, s=start, b=blocks, f=step_n: f(s + b * t, cr), 0)
            start = start + blocks * trips

    in_specs = [pl.BlockSpec((tp, HEAD_W), lambda h, j: (0, h)),
                pl.BlockSpec((tq, HEAD_W), lambda h, j: (j, h)),
                pl.BlockSpec((tq, V_HEAD), lambda h, j: (j, h)),
                pl.BlockSpec((None, nq, SUBLANES, tq), lambda h, j: (h, 0, 0, 0)),
                pl.BlockSpec((None, nq, SUBLANES, tq), lambda h, j: (h, 0, 0, 0)),
                pl.BlockSpec((tp, V_HEAD), lambda h, j: (0, h))]
    out_specs = [pl.BlockSpec((tp, HEAD_W), lambda h, j: (0, h)),
                 pl.BlockSpec((tq, HEAD_W), lambda h, j: (j, h)),
                 pl.BlockSpec((tq, V_HEAD), lambda h, j: (j, h))]
    out_shape = [jax.ShapeDtypeStruct((tp, heads * HEAD_W), F32),
                 jax.ShapeDtypeStruct((tp, heads * HEAD_W), F32),
                 jax.ShapeDtypeStruct((tp, heads * V_HEAD), F32)]
    args, scratch = [q, k, v, lse, delta, do], []
    body = _with_exchange(exchange, len(in_specs), len(out_specs),
                          lambda: (pl.program_id(0) == 0) & (pl.program_id(1) == 0),
                          lambda: (pl.program_id(0) == heads - 1) & (pl.program_id(1) == nq - 1), compute)
    if exchange is not None:
        in_specs = in_specs + exchange.specs(exchange.n_in)
        out_specs = out_specs + exchange.specs(exchange.n_out)
        out_shape = out_shape + exchange.out_shapes
        args, scratch = args + exchange.arrays, exchange.scratch()
    return pl.pallas_call(
        body, name="a_flash_bwd", grid=(heads, nq),
        in_specs=in_specs, out_specs=out_specs, out_shape=out_shape, scratch_shapes=scratch,
        compiler_params=_params(("arbitrary", "arbitrary")),
    )(*args)


def _position():
    return lax.axis_index("x"), lax.axis_index("y"), lax.axis_index("c")


def _other_chips(x, y):
    return [(1 - x, y), (x, 1 - y), (1 - x, 1 - y)]


def _block(ref, shard_axis, n, k, split_axis=None, m=None, h=None):
    idx = []
    for a in range(len(ref.shape)):
        start = 0
        size = None
        if a == shard_axis:
            start, size = k * n, n
        if a == split_axis:
            size = (n if a == shard_axis else m) // 2
            start = start + h * size
        idx.append(slice(None) if size is None else pl.ds(start, size))
    return ref.at[tuple(idx)]


def _gather_weights(split, whole_small):
    ns, nw = len(split), len(whole_small)
    n = ns + nw
    arrs = [s[0] for s in split] + [s[0] for s in whole_small]
    axes = [s[1] for s in split] + [s[1] for s in whole_small]

    def body(*refs):
        ins, outs = refs[:n], refs[n:2 * n]
        ici_send, ici_recv, d2d_send, d2d_recv, sib_send, sib_recv = refs[2 * n:]
        x, y, c = _position()
        me = 2 * x + y
        others = _other_chips(x, y)
        sent, local = [], []

        def remote(src, dst, sems, idx, to):
            return pltpu.make_async_remote_copy(src_ref=src, dst_ref=dst, send_sem=sems[0].at[idx],
                                                recv_sem=sems[1].at[idx], device_id=to, device_id_type=MESH)

        for a in range(n):
            width = ins[a].shape[axes[a]]
            mine = remote(ins[a], _block(outs[a], axes[a], width, me), (sib_send, sib_recv), a, (x, y, 1 - c))
            mine.start()
            local.append(mine)
            for j, (px, py) in enumerate(others):
                if a < ns:
                    sx = split[a][2]
                    src = _block(ins[a], None, None, None, sx, ins[a].shape[sx], c)
                    dst = _block(outs[a], axes[a], width, me, sx, outs[a].shape[sx], c)
                else:
                    src, dst = ins[a], _block(outs[a], axes[a], width, me)
                cp = remote(src, dst, (ici_send, ici_recv), 3 * a + j, (px, py, c))
                cp.start()
                sent.append(cp)
        for a in range(ns):
            width = ins[a].shape[axes[a]]
            sx = split[a][2]
            for j, (px, py) in enumerate(others):
                theirs = _block(outs[a], axes[a], width, 2 * px + py, sx, outs[a].shape[sx], c)
                remote(theirs, theirs, (ici_send, ici_recv), 3 * a + j, (px, py, c)).wait_recv()
                fwd = remote(theirs, theirs, (d2d_send, d2d_recv), 3 * a + j, (x, y, 1 - c))
                fwd.start()
                sent.append(fwd)
        for a in range(ns, n):
            width = ins[a].shape[axes[a]]
            for j, (px, py) in enumerate(others):
                theirs = _block(outs[a], axes[a], width, 2 * px + py)
                remote(theirs, theirs, (ici_send, ici_recv), 3 * a + j, (px, py, c)).wait_recv()
        for a in range(ns):
            width = ins[a].shape[axes[a]]
            sx = split[a][2]
            for j, (px, py) in enumerate(others):
                from_sibling = _block(outs[a], axes[a], width, 2 * px + py, sx, outs[a].shape[sx], 1 - c)
                remote(from_sibling, from_sibling, (d2d_send, d2d_recv), 3 * a + j, (x, y, 1 - c)).wait_recv()
        for cp in sent:
            cp.wait_send()
        for cp in local:
            cp.wait()

    def whole_shape(a, axis):
        shape = list(a.shape)
        shape[axis] *= N_CHIPS
        return jax.ShapeDtypeStruct(tuple(shape), a.dtype)

    any_spec = pl.BlockSpec(memory_space=pl.ANY)
    return pl.pallas_call(
        body, name="gather_weights",
        in_specs=[any_spec] * n, out_specs=[any_spec] * n,
        out_shape=[whole_shape(a, ax) for a, ax in zip(arrs, axes)],
        scratch_shapes=[pltpu.SemaphoreType.DMA((3 * n,)), pltpu.SemaphoreType.DMA((3 * n,)),
                        pltpu.SemaphoreType.DMA((3 * ns,)), pltpu.SemaphoreType.DMA((3 * ns,)),
                        pltpu.SemaphoreType.DMA((n,)), pltpu.SemaphoreType.DMA((n,))],
        compiler_params=pltpu.CompilerParams(has_side_effects=True),
    )(*arrs)


class _Grad:
    def __init__(self, name, g, kind, rh, cols, groups=None):
        self.name, self.g, self.kind, self.rh, self.cols, self.groups = name, g, kind, rh, cols, groups
        if kind == 'rows':
            self.tr = rh
        elif kind == 'gate':
            self.tr = rh // (groups // 2)
        else:
            self.tr = rh if rh <= 512 else 256
        self.nb = rh // self.tr

    def pieces(self, ref, k, h):
        rh, cols = self.rh, self.cols
        if self.kind == 'cols':
            return [(ref.at[pl.ds(h * rh, rh), pl.ds(k * cols, cols)], 0, rh)]
        if self.kind == 'rows':
            return [(ref.at[pl.ds((2 * k + h) * rh, rh), :], 0, rh)]
        if self.kind == 'lead':
            return [(ref.at[k, pl.ds(h * rh, rh), :], 0, rh)]
        per = self.groups // 2
        return [(ref.at[pl.ds((((h * per + gi) * N_CHIPS) + k) * self.tr, self.tr), :], gi * self.tr, self.tr)
                for gi in range(per)]

    def block_spec(self):
        tr, nb, cols = self.tr, self.nb, self.cols
        if self.kind == 'cols':
            return pl.BlockSpec((tr, cols), lambda k, i, c: (c[0] * nb + i, k))
        if self.kind == 'rows':
            return pl.BlockSpec((tr, cols), lambda k, i, c: (2 * k + c[0], 0))
        if self.kind == 'lead':
            return pl.BlockSpec((None, tr, cols), lambda k, i, c: (k, c[0] * nb + i, 0))
        return pl.BlockSpec((tr, cols), lambda k, i, c: ((c[0] * nb + i) * N_CHIPS + k, 0))


class _Exchange:
    def __init__(self, name, arrays, out_shapes, n_copies, copies):
        self.name, self.arrays, self.out_shapes, self.n_copies, self.copies = name, arrays, out_shapes, n_copies, copies
        self.n_in, self.n_out = len(arrays), len(out_shapes)

    def specs(self, n):
        return [pl.BlockSpec(memory_space=pl.ANY)] * n

    def scratch(self):
        return [pltpu.SemaphoreType.DMA((self.n_copies,)), pltpu.SemaphoreType.DMA((self.n_copies,))]

    def _descriptors(self, in_refs, out_refs, sems):
        return self.copies(in_refs, out_refs, sems[0], sems[1])

    def start(self, in_refs, out_refs, sems):
        for cp in self._descriptors(in_refs, out_refs, sems):
            cp.start()

    def finish(self, in_refs, out_refs, sems):
        for cp in self._descriptors(in_refs, out_refs, sems):
            cp.wait()

    def __add__(self, other):
        def copies(ins, outs, send_sems, recv_sems, base=0):
            return (self.copies(ins[:self.n_in], outs[:self.n_out], send_sems, recv_sems, base)
                    + other.copies(ins[self.n_in:], outs[self.n_out:], send_sems, recv_sems, base + self.n_copies))

        return _Exchange(self.name + "_" + other.name, self.arrays + other.arrays, self.out_shapes + other.out_shapes,
                         self.n_copies + other.n_copies, copies)

    def run(self):
        def body(*refs):
            ins, outs, sems = refs[:self.n_in], refs[self.n_in:self.n_in + self.n_out], refs[self.n_in + self.n_out:]
            self.start(ins, outs, sems)
            self.finish(ins, outs, sems)

        return pl.pallas_call(
            body, name=self.name,
            in_specs=self.specs(self.n_in), out_specs=self.specs(self.n_out), out_shape=self.out_shapes,
            scratch_shapes=self.scratch(),
            compiler_params=pltpu.CompilerParams(has_side_effects=True),
        )(*self.arrays)


def _gather_whole(name, shards):
    def copies(ins, outs, send_sems, recv_sems, base=0):
        x, y, c = _position()
        me = 2 * x + y
        made = []
        for a, (_, axis) in enumerate(shards):
            dst = _block(outs[a], axis, ins[a].shape[axis], me)
            for j, to in enumerate([(x, y, 1 - c)] + [(px, py, c) for px, py in _other_chips(x, y)]):
                idx = base + 4 * a + j
                made.append(pltpu.make_async_remote_copy(
                    src_ref=ins[a], dst_ref=dst, send_sem=send_sems.at[idx], recv_sem=recv_sems.at[idx],
                    device_id=to, device_id_type=MESH))
        return made

    def whole_shape(a, axis):
        shape = list(a.shape)
        shape[axis] *= N_CHIPS
        return jax.ShapeDtypeStruct(tuple(shape), a.dtype)

    return _Exchange(name, [s[0] for s in shards], [whole_shape(*s) for s in shards], 4 * len(shards), copies)


def _halves_to_sibling(name, grads):
    total = sum(len(gr.pieces(gr.g, 0, 0)) * N_CHIPS for gr in grads)

    def copies(ins, outs, send_sems, recv_sems, base=0):
        x, y, c = _position()
        made = []
        for gr, g_ref, got_ref in zip(grads, ins, outs):
            for k in range(N_CHIPS):
                for src, r0, nr in gr.pieces(g_ref, k, 1 - c):
                    idx = base + len(made)
                    made.append(pltpu.make_async_remote_copy(
                        src_ref=src, dst_ref=got_ref.at[k, pl.ds(r0, nr), :],
                        send_sem=send_sems.at[idx], recv_sem=recv_sems.at[idx],
                        device_id=(x, y, 1 - c), device_id_type=MESH))
        return made

    return _Exchange(name, [gr.g for gr in grads],
                     [jax.ShapeDtypeStruct((N_CHIPS, gr.rh, gr.cols), F32) for gr in grads], total, copies)


def _chip_sum(gr, got, c, wire=BF16):
    def body(c_ref, g_ref, got_ref, o_ref):
        o_ref[...] = (g_ref[...] + got_ref[...]).astype(wire)

    tile = pl.BlockSpec((None, gr.tr, gr.cols), lambda k, i, c_ref: (k, i, 0))
    return pl.pallas_call(
        body, name="chip_sum_" + gr.name,
        grid_spec=pltpu.PrefetchScalarGridSpec(
            num_scalar_prefetch=1, grid=(N_CHIPS, gr.nb),
            in_specs=[gr.block_spec(), tile], out_specs=tile),
        out_shape=jax.ShapeDtypeStruct((N_CHIPS, gr.rh, gr.cols), wire),
        compiler_params=_params(("parallel", "parallel")),
    )(c, gr.g, got)


def _blocks_to_chips(name, parts):
    n = len(parts)

    def copies(ins, outs, send_sems, recv_sems, base=0):
        x, y, c = _position()
        made = []
        for a in range(n):
            for j, (px, py) in enumerate(_other_chips(x, y)):
                idx = base + 3 * a + j
                made.append(pltpu.make_async_remote_copy(
                    src_ref=ins[a].at[2 * px + py], dst_ref=outs[a].at[j],
                    send_sem=send_sems.at[idx], recv_sem=recv_sems.at[idx],
                    device_id=(px, py, c), device_id_type=MESH))
        return made

    return _Exchange(name, parts, [jax.ShapeDtypeStruct((3,) + p.shape[1:], p.dtype) for p in parts], 3 * n, copies)


def _sum_chips(name, part, got, me, c):
    nk, rh, cols = part.shape
    tr = rh if rh <= 512 else 256
    nb = rh // tr

    def body(me_ref, c_ref, own_ref, *rest):
        got_refs, o_ref = rest[:nk], rest[nk]
        own = own_ref[...].astype(F32)
        acc = None
        for k in range(nk):
            term = jnp.where(me_ref[0] == k, own, got_refs[k][...].astype(F32))
            acc = term if acc is None else acc + term
        o_ref[...] = acc

    def got_map(k):
        def index(i, me_ref, c_ref):
            xor = jnp.bitwise_xor(me_ref[0], k)
            slot = jnp.where(xor == 1, 1, jnp.where(xor == 3, 2, 0))
            return (slot, i, 0)
        return index

    return pl.pallas_call(
        body, name="sum_" + name,
        grid_spec=pltpu.PrefetchScalarGridSpec(
            num_scalar_prefetch=2, grid=(nb,),
            in_specs=[pl.BlockSpec((None, tr, cols), lambda i, me_ref, c_ref: (me_ref[0], i, 0))]
            + [pl.BlockSpec((None, tr, cols), got_map(k)) for k in range(nk)],
            out_specs=pl.BlockSpec((tr, cols), lambda i, me_ref, c_ref: (c_ref[0] * nb + i, 0))),
        out_shape=jax.ShapeDtypeStruct((2 * rh, cols), F32),
        compiler_params=_params(("parallel",)),
    )(me, c, part, *([got] * nk))


def _share_with_sibling(halves):
    n = len(halves)

    def body(*refs):
        outs = refs[n:2 * n]
        send_sems, recv_sems = refs[2 * n:]
        x, y, c = _position()
        copies = []
        for a in range(n):
            rh = outs[a].shape[0] // 2
            mine = outs[a].at[pl.ds(c * rh, rh), :]
            cp = pltpu.make_async_remote_copy(
                src_ref=mine, dst_ref=mine, send_sem=send_sems.at[a], recv_sem=recv_sems.at[a],
                device_id=(x, y, 1 - c), device_id_type=MESH)
            cp.start()
            copies.append(cp)
        for cp in copies:
            cp.wait()

    any_spec = pl.BlockSpec(memory_space=pl.ANY)
    return pl.pallas_call(
        body, name="grads_share",
        in_specs=[any_spec] * n, out_specs=[any_spec] * n,
        out_shape=[jax.ShapeDtypeStruct(h.shape, h.dtype) for h in halves],
        input_output_aliases={a: a for a in range(n)},
        scratch_shapes=[pltpu.SemaphoreType.DMA((n,)), pltpu.SemaphoreType.DMA((n,))],
        compiler_params=pltpu.CompilerParams(has_side_effects=True),
    )(*halves)


def _adamw(name, w, g, m, v):
    rows, cols = w.shape
    tr = 256 if rows % 256 == 0 else rows

    def body(w_ref, g_ref, m_ref, v_ref, g_out_ref, d_ref, nm_ref, nv_ref):
        gv = g_ref[...]
        g_out_ref[...] = gv
        mn = ADAM_B1 * m_ref[...] + (1.0 - ADAM_B1) * gv
        vn = ADAM_B2 * v_ref[...] + (1.0 - ADAM_B2) * (gv * gv)
        m_hat = mn / (1.0 - ADAM_B1 ** ADAM_STEP)
        v_hat = vn / (1.0 - ADAM_B2 ** ADAM_STEP)
        d_ref[...] = -ADAM_LR * (m_hat / (jnp.sqrt(v_hat) + ADAM_EPS) + ADAM_WD * w_ref[...])
        nm_ref[...] = mn
        nv_ref[...] = vn

    spec = pl.BlockSpec((tr, cols), lambda i: (i, 0))
    return pl.pallas_call(
        body, name=name, grid=(rows // tr,),
        in_specs=[spec] * 4, out_specs=[spec] * 4,
        out_shape=[jax.ShapeDtypeStruct((rows, cols), F32)] * 4,
        compiler_params=_params(("parallel",)),
    )(w, g, m, v)


def _as2d(a):
    if a.ndim == 1:
        return a.reshape(1, -1)
    return a.reshape(-1, a.shape[-1])


def _unshard(gathered, axis):
    moved = jnp.moveaxis(gathered, 0, axis)
    shape = list(gathered.shape[1:])
    shape[axis] *= N_CHIPS
    return moved.reshape(shape)


def _rope_tables(tp, pad):
    pos = jnp.arange(tp, dtype=F32) - pad
    inv_freq = ROPE_BASE ** (-jnp.arange(0, QK_ROPE, 2, dtype=F32) / QK_ROPE)
    ang = pos[:, None] * inv_freq[None, :]
    cos, sin = jnp.cos(ang), jnp.sin(ang)
    zeros = jnp.zeros((tp, LANES - QK_ROPE), F32)
    return jnp.concatenate([cos, cos, zeros], axis=1), jnp.concatenate([-sin, sin, zeros], axis=1)


def _matrix_grad(name, g, heads):
    if name in ('b_w_rg', 'b_w_ig'):
        groups, blk, cols = g.shape
        return _Grad(name, g.reshape(groups * blk, cols), 'gate', (groups // 2) * (blk // N_CHIPS), cols, groups)
    rows, cols = g.shape
    if name in ('a_w_out', 'b_w_out'):
        return _Grad(name, g, 'rows', rows // (2 * N_CHIPS), cols)
    if name == 'a_w_uq' and heads % N_CHIPS != 0:
        g = g.reshape(rows, heads, HEAD_W)[:, :, :QK_NOPE + QK_ROPE].reshape(rows, -1)
        cols = g.shape[1]
    if name == 'a_w_in' or (name == 'a_w_uq' and heads % N_CHIPS != 0):
        g = jnp.moveaxis(g.reshape(rows, N_CHIPS, cols // N_CHIPS), 1, 0)
        return _Grad(name, g, 'lead', rows // 2, cols // N_CHIPS)
    return _Grad(name, g, 'cols', rows // 2, cols // N_CHIPS)


def _kernel_form(name, w):
    return w[0] if name in ('b_w_rg', 'b_w_ig', 'b_conv_w') else _as2d(w)


def _local_grads(x, target, wt, heads, c_idx, mid_names, mid_gather, late_names, late_gather):
    wt = dict(wt)
    seq, d = x.shape
    n_meta = wt['meta_tokens'].shape[0]
    t = seq + n_meta
    pad = (-t) % Q_BLOCK
    tp = t + pad
    x0 = pad + n_meta
    tm = _row_tile(tp)
    ql = wt['a_q_norm_g'].shape[1]
    kvl = wt['a_kv_norm_g'].shape[1]
    mla_w = heads * V_HEAD

    h0 = jnp.concatenate([jnp.zeros((pad, d), F32), wt['meta_tokens'], x], axis=0)
    cos_t, sin_t = _rope_tables(tp, pad)

    w_in_a = wt['a_w_in']
    zcol = jnp.zeros((d, LANES - QK_ROPE), BF16)
    w_in_a = jnp.concatenate([w_in_a[:, :ql + kvl + QK_ROPE], zcol, w_in_a[:, ql + kvl + QK_ROPE:]], axis=1)
    c_kv, c_kr, c_gate = ql, ql + kvl, ql + kvl + LANES
    splits_a = [(0, c_kv), (c_kv, c_kr), (c_kr, c_gate), (c_gate, c_gate + mla_w)]

    q_lat, kv_lat, kr_raw, gate_a, *mid_whole = _norm_matmul("a_in_proj", h0, wt['a_norm_g'], w_in_a, splits_a, tm,
                                                             exchange=mid_gather)
    wt.update({n: _kernel_form(n, w) for n, w in zip(mid_names, mid_whole)})
    w_uq = wt['a_w_uq'].reshape(ql, heads, QK_NOPE + QK_ROPE)
    w_uq = jnp.pad(w_uq, ((0, 0), (0, 0), (0, HEAD_W - QK_NOPE - QK_ROPE))).reshape(ql, heads * HEAD_W)
    w_ukv = wt['a_w_ukv']
    q = _q_proj(q_lat, wt['a_q_norm_g'], w_uq, cos_t, sin_t, heads, tm)
    k, v = _kv_proj(kv_lat, wt['a_kv_norm_g'], w_ukv, kr_raw, cos_t, sin_t, heads, tm)
    attn, lse, *late_whole = _flash_fwd(q, k, v, heads, pad, tm, exchange=late_gather)
    wt.update({n: _kernel_form(n, w) for n, w in zip(late_names, late_whole)})
    lru_w = wt['b_conv_w'].shape[1]

    h1, u, gate_b = _out_proj_in_proj("a_out_b_in_proj", attn, gate_a, wt['a_w_out'], h0, wt['b_norm_g'], wt['b_w_in'],
                                      [(0, lru_w), (lru_w, 2 * lru_w)], tm)
    uc, r, ig, hs, decay, mult = _rglru_fwd(u, wt['b_conv_w'], wt['b_conv_b'], wt['b_w_rg'], wt['b_b_rg'],
                                            wt['b_w_ig'], wt['b_b_ig'], wt['b_lam'], pad, tm)

    dh2, loss, d_final_g, dhs, dgate_b, dw_out_b = _out_proj_loss(
        hs, gate_b, wt['b_w_out'], h1, wt['final_norm_g'], target, x0, tm)
    du, dconv_w, dconv_b, dw_rg, db_rg, dw_ig, db_ig, dlam = _rglru_bwd(
        dhs, hs, r, ig, uc, u, decay, mult, wt['b_conv_w'], wt['b_w_rg'], wt['b_w_ig'], wt['b_lam'], pad, tm)
    dh1, dw_in_b, dg_b = _norm_matmul_bwd("b_in_proj_bwd", h1, wt['b_norm_g'], wt['b_w_in'], [du, dgate_b], tm, resid=dh2)

    grads_b = [_matrix_grad(n, g, heads) for n, g in
               (('b_w_in', dw_in_b), ('b_w_rg', dw_rg), ('b_w_ig', dw_ig), ('b_w_out', dw_out_b))]
    dattn, dgate_a, dw_out_a, delta, *got = _gated_out_bwd(
        "a_out_proj_bwd", dh1, attn, gate_a, wt['a_w_out'], tm, delta_heads=heads,
        exchange=_halves_to_sibling("swap_b", grads_b))
    sums_b = [_chip_sum(gr, r, c_idx) for gr, r in zip(grads_b, got)]
    grad_out = _matrix_grad('a_w_out', dw_out_a, heads)
    dq, dk, dv, *landed = _flash_bwd(
        q, k, v, lse, delta, dattn, heads, pad, tm,
        exchange=_blocks_to_chips("chips_b", sums_b) + _halves_to_sibling("swap_out", [grad_out]))
    through = list(zip(grads_b, sums_b, landed[:len(grads_b)]))
    sum_out = _chip_sum(grad_out, landed[len(grads_b)], c_idx)

    def q_prologue(dy_refs, dy_s, ex_ref):
        (dq_ref,), cos_v, sin_v = dy_refs[:1], dy_refs[1][...], dy_refs[2][...]
        for h in range(heads):
            c0 = h * HEAD_W
            dy_s[:, c0:c0 + QK_NOPE] = dq_ref[:, c0:c0 + QK_NOPE].astype(BF16)
            dy_s[:, c0 + QK_NOPE:c0 + HEAD_W] = _unrope(dq_ref[:, c0 + QK_NOPE:c0 + HEAD_W], cos_v, sin_v).astype(BF16)

    dq_lat, dw_uq, dg_q, from_chips_out = _norm_matmul_bwd(
        "a_q_proj_bwd", q_lat, wt['a_q_norm_g'], w_uq, [dq, cos_t, sin_t], tm, prologue=q_prologue,
        exchange=_blocks_to_chips("chips_out", [sum_out]))
    through.append((grad_out, sum_out, from_chips_out))
    grad_uq = _matrix_grad('a_w_uq', dw_uq, heads)

    def kv_prologue(dy_refs, dy_s, ex_ref):
        dk_ref, dv_ref = dy_refs[:2]
        cos_v, sin_v = dy_refs[2][...], dy_refs[3][...]
        dkr = jnp.zeros((dk_ref.shape[0], LANES), F32)
        for h in range(heads):
            c0 = h * (QK_NOPE + V_HEAD)
            dy_s[:, c0:c0 + QK_NOPE] = dk_ref[:, h * HEAD_W:h * HEAD_W + QK_NOPE].astype(BF16)
            dy_s[:, c0 + QK_NOPE:c0 + QK_NOPE + V_HEAD] = dv_ref[:, h * V_HEAD:(h + 1) * V_HEAD].astype(BF16)
            dkr = dkr + dk_ref[:, h * HEAD_W + QK_NOPE:(h + 1) * HEAD_W]
        ex_ref[...] = _unrope(dkr, cos_v, sin_v)

    dkv_lat, dw_ukv, dg_kv, dkr_raw, got_uq = _norm_matmul_bwd(
        "a_kv_proj_bwd", kv_lat, wt['a_kv_norm_g'], w_ukv, [dk, dv, cos_t, sin_t], tm,
        prologue=kv_prologue, extra_out=(LANES, F32), exchange=_halves_to_sibling("swap_uq", [grad_uq]))
    sum_uq = _chip_sum(grad_uq, got_uq, c_idx)
    grad_ukv = _matrix_grad('a_w_ukv', dw_ukv, heads)

    dh0, dw_in_a, dg_a, from_chips_uq, got_ukv = _norm_matmul_bwd(
        "a_in_proj_bwd", h0, wt['a_norm_g'], w_in_a, [dq_lat, dkv_lat, dkr_raw, dgate_a], tm, resid=dh1,
        exchange=_blocks_to_chips("chips_uq", [sum_uq]) + _halves_to_sibling("swap_ukv", [grad_ukv]))
    through.append((grad_uq, sum_uq, from_chips_uq))
    swapped = [(grad_ukv, _chip_sum(grad_ukv, got_ukv, c_idx))]

    dw_in_a = jnp.concatenate([dw_in_a[:, :c_kr + QK_ROPE], dw_in_a[:, c_gate:]], axis=1)
    grads = {
        'meta_tokens': dh0[pad:x0], 'a_norm_g': dg_a, 'a_w_in': dw_in_a, 'a_q_norm_g': dg_q, 'a_kv_norm_g': dg_kv,
        'a_w_uq': dw_uq, 'a_w_ukv': dw_ukv, 'a_w_out': dw_out_a, 'b_norm_g': dg_b, 'b_w_in': dw_in_b,
        'b_conv_w': dconv_w, 'b_conv_b': dconv_b, 'b_w_rg': dw_rg, 'b_b_rg': db_rg, 'b_w_ig': dw_ig,
        'b_b_ig': db_ig, 'b_lam': dlam, 'b_w_out': dw_out_b, 'final_norm_g': d_final_g,
    }
    return loss, dh0[x0:], grads, through, swapped


def _chip_major(whole, local_shape, axis):
    if axis is None:
        return jnp.broadcast_to(whole.reshape(1, -1), (N_CHIPS, whole.size))
    shape = list(local_shape)
    g = whole.reshape(shape[:axis] + [N_CHIPS, shape[axis]] + shape[axis + 1:])
    return jnp.moveaxis(g, axis, 0).reshape(N_CHIPS, -1)


def kernel(x, meta_tokens, a_norm_g, a_w_in, a_q_norm_g, a_kv_norm_g, a_w_uq, a_w_ukv, a_w_out, b_norm_g, b_w_in, b_conv_w, b_conv_b, b_w_rg, b_b_rg, b_w_ig, b_b_ig, b_lam, b_w_out, final_norm_g, loss_target, m_meta_tokens, m_a_norm_g, m_a_w_in, m_a_q_norm_g, m_a_kv_norm_g, m_a_w_uq, m_a_w_ukv, m_a_w_out, m_b_norm_g, m_b_w_in, m_b_conv_w, m_b_conv_b, m_b_w_rg, m_b_b_rg, m_b_w_ig, m_b_b_ig, m_b_lam, m_b_w_out, m_final_norm_g, v_meta_tokens, v_a_norm_g, v_a_w_in, v_a_q_norm_g, v_a_kv_norm_g, v_a_w_uq, v_a_w_ukv, v_a_w_out, v_b_norm_g, v_b_w_in, v_b_conv_w, v_b_conv_b, v_b_w_rg, v_b_b_rg, v_b_w_ig, v_b_b_ig, v_b_lam, v_b_w_out, v_final_norm_g):
    local_w = dict(zip(WEIGHTS, (meta_tokens, a_norm_g, a_w_in, a_q_norm_g, a_kv_norm_g, a_w_uq, a_w_ukv, a_w_out,
                                 b_norm_g, b_w_in, b_conv_w, b_conv_b, b_w_rg, b_b_rg, b_w_ig, b_b_ig, b_lam,
                                 b_w_out, final_norm_g)))
    local_m = dict(zip(WEIGHTS, (m_meta_tokens, m_a_norm_g, m_a_w_in, m_a_q_norm_g, m_a_kv_norm_g, m_a_w_uq,
                                 m_a_w_ukv, m_a_w_out, m_b_norm_g, m_b_w_in, m_b_conv_w, m_b_conv_b, m_b_w_rg,
                                 m_b_b_rg, m_b_w_ig, m_b_b_ig, m_b_lam, m_b_w_out, m_final_norm_g)))
    local_v = dict(zip(WEIGHTS, (v_meta_tokens, v_a_norm_g, v_a_w_in, v_a_q_norm_g, v_a_kv_norm_g, v_a_w_uq,
                                 v_a_w_ukv, v_a_w_out, v_b_norm_g, v_b_w_in, v_b_conv_w, v_b_conv_b, v_b_w_rg,
                                 v_b_b_rg, v_b_w_ig, v_b_b_ig, v_b_lam, v_b_w_out, v_final_norm_g)))
    matrices = ('a_w_in', 'a_w_uq', 'a_w_ukv', 'a_w_out', 'b_w_in', 'b_w_rg', 'b_w_ig', 'b_w_out')
    heads = a_w_uq.shape[-1] * N_CHIPS // (QK_NOPE + QK_ROPE)

    split, small, mid, late = [], [], [], []
    for n in WEIGHTS:
        if SHARD_AXIS[n] is None:
            continue
        if n.startswith('b_') or n == 'a_w_out':
            late.append((n, local_w[n].astype(BF16) if n in matrices else local_w[n], SHARD_AXIS[n]))
        elif n == 'a_w_in':
            split.append((n, local_w[n].astype(BF16)[None], 0, 2))
        elif n in matrices:
            mid.append((n, local_w[n].astype(BF16), SHARD_AXIS[n]))
        else:
            small.append((n, local_w[n], SHARD_AXIS[n]))
    gathered = _gather_weights([s[1:] for s in split], [s[1:] for s in small])
    whole = dict(zip([s[0] for s in split + small], gathered))
    whole['a_w_in'] = _unshard(whole['a_w_in'], SHARD_AXIS['a_w_in'])
    mid_names, late_names = [s[0] for s in mid], [s[0] for s in late]
    wt = {n: _kernel_form(n, whole.get(n, local_w[n])) for n in WEIGHTS if n not in mid_names + late_names}

    c_idx = lax.axis_index("c").astype(jnp.int32).reshape(1)
    me_idx = (2 * lax.axis_index("x") + lax.axis_index("y")).astype(jnp.int32).reshape(1)
    loss, grad_x, grads, through, swapped = _local_grads(
        x[0], loss_target[0], wt, heads, c_idx,
        mid_names, _gather_whole("gather_weights_a", [s[1:] for s in mid]),
        late_names, _gather_whole("gather_weights_b", [s[1:] for s in late]))

    ext_uq = heads % N_CHIPS == 0
    started = [gr.name for gr, *_ in through + swapped]
    last = [_matrix_grad(n, grads[n], heads) for n in matrices if n not in started]
    rest = [n for n in WEIGHTS if n not in matrices]
    pieces = [_chip_major(grads[n], local_w[n].shape, SHARD_AXIS[n]) for n in rest]
    pieces.append(jnp.broadcast_to(loss[0:1, 0:1], (N_CHIPS, 1)))
    length = sum(p.shape[1] for p in pieces)
    unit = 2 * SUBLANES * 1024
    padded = -(-length // unit) * unit
    flat = jnp.concatenate(pieces + [jnp.zeros((N_CHIPS, padded - length), F32)], axis=1)
    last.append(_Grad('small', flat.reshape(N_CHIPS, padded // 1024, 1024), 'lead', padded // 2048, 1024))

    got = _halves_to_sibling("grads_to_sibling", last).run()
    swapped = swapped + [(gr, _chip_sum(gr, r, c_idx, F32 if gr.name == 'small' else BF16))
                         for gr, r in zip(last, got)]
    from_chips = _blocks_to_chips("grads_to_chips", [p for _, p in swapped]).run()
    through = through + [(gr, p, r) for (gr, p), r in zip(swapped, from_chips)]
    halves = [_sum_chips(gr.name, p, r, me_idx, c_idx) for gr, p, r in through]
    summed = dict(zip([gr.name for gr, _, _ in through], _share_with_sibling(halves)))
    if ext_uq:
        g = summed['a_w_uq']
        summed['a_w_uq'] = g.reshape(g.shape[0], -1, HEAD_W)[:, :, :QK_NOPE + QK_ROPE]
    total = summed['small'].reshape(-1)

    out_g, out_d, out_m, out_v = [], [], [], []
    off = 0
    for n in WEIGHTS:
        shape = local_w[n].shape
        if n in matrices:
            g = summed[n].reshape(shape)
        else:
            size = 1
            for s in shape:
                size *= s
            g = total[off:off + size].reshape(shape)
            off += size
        g, delta, new_m, new_v = _adamw("adamw_" + n, _as2d(local_w[n]), _as2d(g), _as2d(local_m[n]), _as2d(local_v[n]))
        out_g.append(g.reshape(shape))
        out_d.append(delta.reshape(shape))
        out_m.append(new_m.reshape(shape))
        out_v.append(new_v.reshape(shape))

    return (total[off], grad_x[None], *out_g, *out_d, *out_m, *out_v)
```

```python
import functools

import jax
import jax.numpy as jnp
from jax import lax
from jax.experimental import pallas as pl
from jax.experimental.pallas import tpu as pltpu

F32 = jnp.float32
BF16 = jnp.bfloat16
MESH = pl.DeviceIdType.MESH

RMS_EPS = 1e-6
QK_NOPE = 128
QK_ROPE = 64
V_HEAD = 128
HEAD_W = 256
ROPE_BASE = 10000.0
Q_BLOCK = 128
MASK_VALUE = -1e30
CONV_WIDTH = 4
LRU_C = 8.0
N_CHIPS = 4

ADAM_LR = 0.001
ADAM_B1 = 0.9
ADAM_B2 = 0.999
ADAM_EPS = 1e-08
ADAM_WD = 0.01
ADAM_STEP = 10

VMEM_LIMIT_V7X = 56 * 1024 * 1024
LANES = 128
SUBLANES = 8

WEIGHTS = ['meta_tokens', 'a_norm_g', 'a_w_in', 'a_q_norm_g', 'a_kv_norm_g', 'a_w_uq', 'a_w_ukv',
           'a_w_out', 'b_norm_g', 'b_w_in', 'b_conv_w', 'b_conv_b', 'b_w_rg', 'b_b_rg', 'b_w_ig',
           'b_b_ig', 'b_lam', 'b_w_out', 'final_norm_g']
SHARD_AXIS = {'meta_tokens': 1, 'a_norm_g': None, 'a_w_in': 2, 'a_q_norm_g': None, 'a_kv_norm_g': None,
              'a_w_uq': 2, 'a_w_ukv': 2, 'a_w_out': 1, 'b_norm_g': 1, 'b_w_in': 2, 'b_conv_w': 2,
              'b_conv_b': 1, 'b_w_rg': 2, 'b_b_rg': 1, 'b_w_ig': 2, 'b_b_ig': 1, 'b_lam': 1,
              'b_w_out': 1, 'final_norm_g': None}


def _params(sem=None):
    return pltpu.CompilerParams(dimension_semantics=sem, vmem_limit_bytes=VMEM_LIMIT_V7X)


def _row_tile(tp):
    return 384 if (tp % 384 == 0 and tp >= 1152) else 128


def _sigmoid(x):
    return 1.0 / (1.0 + jnp.exp(-x))


def _rms(x):
    return lax.rsqrt(jnp.mean(x * x, axis=-1, keepdims=True) + RMS_EPS)


def _swap32(x):
    lane = lax.broadcasted_iota(jnp.int32, x.shape, 1)
    return jnp.where(lane < 32, pltpu.roll(x, 96, 1), pltpu.roll(x, 32, 1))


def _rope(x, cos_t, sin_t):
    return x * cos_t + _swap32(x) * sin_t


def _unrope(d, cos_t, sin_t):
    lane = lax.broadcasted_iota(jnp.int32, d.shape, 1)
    return jnp.where(lane < QK_ROPE, d * cos_t + _swap32(d * sin_t), 0.0)


def _dot(a, b):
    return jnp.dot(a, b, preferred_element_type=F32)


def _dot_nt(a, b):
    return lax.dot_general(a, b, (((1,), (1,)), ((), ())), preferred_element_type=F32)


def _dot_tn(a, b):
    return lax.dot_general(a, b, (((0,), (0,)), ((), ())), preferred_element_type=F32)


def _norm_matmul(name, x, g, wt, splits, tm, exchange=None):
    tp, kin = x.shape
    n = wt.shape[0]
    nt = tp // tm

    def compute(x_ref, g_ref, w_ref, *outs):
        xv = x_ref[...]
        nrm = ((xv * _rms(xv)) * g_ref[...]).astype(BF16)
        y = _dot_nt(nrm, w_ref[...])
        for o_ref, (c0, c1) in zip(outs, splits):
            o_ref[...] = y[:, c0:c1]

    in_specs = [pl.BlockSpec((tm, kin), lambda i: (i, 0)),
                pl.BlockSpec((1, kin), lambda i: (0, 0)),
                pl.BlockSpec((n, kin), lambda i: (0, 0))]
    out_specs = [pl.BlockSpec((tm, c1 - c0), lambda i: (i, 0)) for c0, c1 in splits]
    out_shape = [jax.ShapeDtypeStruct((tp, c1 - c0), F32) for c0, c1 in splits]
    args, scratch = [x, g, wt], []
    body = _with_exchange(exchange, len(in_specs), len(out_specs),
                          lambda: pl.program_id(0) == 0, lambda: pl.program_id(0) == nt - 1, compute)
    if exchange is not None:
        in_specs = in_specs + exchange.specs(exchange.n_in)
        out_specs = out_specs + exchange.specs(exchange.n_out)
        out_shape = out_shape + exchange.out_shapes
        args, scratch = args + exchange.arrays, exchange.scratch()
    return pl.pallas_call(
        body, name=name, grid=(nt,),
        in_specs=in_specs, out_specs=out_specs, out_shape=out_shape, scratch_shapes=scratch,
        compiler_params=_params(("arbitrary",)),
    )(*args)


def _q_proj(q_lat, g, w_uq, cos_t, sin_t, heads, tm):
    tp, kin = q_lat.shape
    n = heads * HEAD_W

    def body(x_ref, g_ref, w_ref, cos_ref, sin_ref, q_ref):
        xv = x_ref[...]
        nrm = ((xv * _rms(xv)) * g_ref[...]).astype(BF16)
        y = _dot(nrm, w_ref[...])
        cos_v, sin_v = cos_ref[...], sin_ref[...]
        for h in range(heads):
            c0 = h * HEAD_W
            q_ref[:, c0:c0 + QK_NOPE] = y[:, c0:c0 + QK_NOPE].astype(BF16)
            q_ref[:, c0 + QK_NOPE:c0 + HEAD_W] = _rope(y[:, c0 + QK_NOPE:c0 + HEAD_W], cos_v, sin_v).astype(BF16)

    return pl.pallas_call(
        body, name="a_q_proj", grid=(tp // tm,),
        in_specs=[pl.BlockSpec((tm, kin), lambda i: (i, 0)),
                  pl.BlockSpec((1, kin), lambda i: (0, 0)),
                  pl.BlockSpec((kin, n), lambda i: (0, 0)),
                  pl.BlockSpec((tm, LANES), lambda i: (i, 0)),
                  pl.BlockSpec((tm, LANES), lambda i: (i, 0))],
        out_specs=pl.BlockSpec((tm, n), lambda i: (i, 0)),
        out_shape=jax.ShapeDtypeStruct((tp, n), BF16),
        compiler_params=_params(("parallel",)),
    )(q_lat, g, w_uq, cos_t, sin_t)


def _kv_proj(kv_lat, g, w_ukv, k_rope_raw, cos_t, sin_t, heads, tm):
    tp, kin = kv_lat.shape
    n = heads * (QK_NOPE + V_HEAD)

    def body(x_ref, g_ref, w_ref, kr_ref, cos_ref, sin_ref, k_ref, v_ref):
        xv = x_ref[...]
        nrm = ((xv * _rms(xv)) * g_ref[...]).astype(BF16)
        y = _dot(nrm, w_ref[...])
        kr = _rope(kr_ref[...], cos_ref[...], sin_ref[...]).astype(BF16)
        for h in range(heads):
            c0 = h * (QK_NOPE + V_HEAD)
            k_ref[:, h * HEAD_W:h * HEAD_W + QK_NOPE] = y[:, c0:c0 + QK_NOPE].astype(BF16)
            k_ref[:, h * HEAD_W + QK_NOPE:(h + 1) * HEAD_W] = kr
            v_ref[:, h * V_HEAD:(h + 1) * V_HEAD] = y[:, c0 + QK_NOPE:c0 + QK_NOPE + V_HEAD].astype(BF16)

    return pl.pallas_call(
        body, name="a_kv_proj", grid=(tp // tm,),
        in_specs=[pl.BlockSpec((tm, kin), lambda i: (i, 0)),
                  pl.BlockSpec((1, kin), lambda i: (0, 0)),
                  pl.BlockSpec((kin, n), lambda i: (0, 0)),
                  pl.BlockSpec((tm, LANES), lambda i: (i, 0)),
                  pl.BlockSpec((tm, LANES), lambda i: (i, 0)),
                  pl.BlockSpec((tm, LANES), lambda i: (i, 0))],
        out_specs=[pl.BlockSpec((tm, heads * HEAD_W), lambda i: (i, 0)),
                   pl.BlockSpec((tm, heads * V_HEAD), lambda i: (i, 0))],
        out_shape=[jax.ShapeDtypeStruct((tp, heads * HEAD_W), BF16),
                   jax.ShapeDtypeStruct((tp, heads * V_HEAD), BF16)],
        compiler_params=_params(("parallel",)),
    )(kv_lat, g, w_ukv, k_rope_raw, cos_t, sin_t)


def _as_rows(col):
    rows = col.shape[0]
    return jnp.transpose(jnp.broadcast_to(col, (rows, LANES)))[0:SUBLANES, :]


def _attn_mask(row0, col0, rows, cols, pad):
    row = row0 + lax.broadcasted_iota(jnp.int32, (rows, cols), 0)
    col = col0 + lax.broadcasted_iota(jnp.int32, (rows, cols), 1)
    return (col <= row) & (col >= pad)


LOG2E = 1.4426950408889634
FLASH_FWD_TRIPS = ((4, 2), (2, 2), (1, 1))


def _flash_fwd(q, k, v, heads, pad, tq, exchange=None):
    tp = q.shape[0]
    nq = tp // tq
    c2 = (QK_NOPE + QK_ROPE) ** -0.5 * LOG2E

    def compute(q_ref, k_ref, v_ref, o_ref, lse_ref):
        i = pl.program_id(1)

        def make_step(masked, blocks, parts=1):
            keys = blocks * tq // parts

            def step(j, carry):
                m, l, acc = carry
                offs = [pl.multiple_of(j * tq + part * keys, tq) for part in range(parts)]
                scores = [_dot_nt(q_ref[...], k_ref[pl.ds(off, keys), :]) for off in offs]
                for off, s in zip(offs, scores):
                    s = s * c2
                    if masked:
                        s = jnp.where(_attn_mask(i * tq, j * tq, tq, keys, pad), s, MASK_VALUE)
                    m_new = jnp.maximum(m, jnp.max(s, axis=-1, keepdims=True))
                    p = jnp.exp2(s - m_new)
                    alpha = jnp.exp2(m - m_new)
                    l = alpha * l + jnp.sum(p, axis=-1, keepdims=True)
                    acc = alpha * acc + _dot(p.astype(BF16), v_ref[pl.ds(off, keys), :])
                    m = m_new
                return m, l, acc
            return step

        init = (jnp.full((tq, 1), MASK_VALUE, F32), jnp.zeros((tq, 1), F32), jnp.zeros((tq, V_HEAD), F32))
        carry = make_step(True, 1)(0, init)
        first = 1
        for blocks, parts in FLASH_FWD_TRIPS:
            trips = jnp.maximum(i - first, 0) // blocks
            step_n = make_step(False, blocks, parts)
            carry = lax.fori_loop(0, trips, lambda t, cr, f=first, b=blocks, s=step_n: s(f + b * t, cr), carry)
            first = first + blocks * trips
        m, l, acc = lax.fori_loop(jnp.maximum(i, 1), i + 1, make_step(True, 1), carry)
        o_ref[...] = acc / l
        lse_ref[...] = _as_rows(m + jnp.log(l) * LOG2E)

    in_specs = [pl.BlockSpec((tq, HEAD_W), lambda h, i: (i, h)),
                pl.BlockSpec((tp, HEAD_W), lambda h, i: (0, h)),
                pl.BlockSpec((tp, V_HEAD), lambda h, i: (0, h))]
    out_specs = [pl.BlockSpec((tq, V_HEAD), lambda h, i: (i, h)),
                 pl.BlockSpec((None, None, SUBLANES, tq), lambda h, i: (h, i, 0, 0))]
    out_shape = [jax.ShapeDtypeStruct((tp, heads * V_HEAD), F32),
                 jax.ShapeDtypeStruct((heads, nq, SUBLANES, tq), F32)]
    args, scratch = [q, k, v], []
    body = _with_exchange(exchange, len(in_specs), len(out_specs),
                          lambda: (pl.program_id(0) == 0) & (pl.program_id(1) == 0),
                          lambda: (pl.program_id(0) == heads - 1) & (pl.program_id(1) == nq - 1), compute)
    if exchange is not None:
        in_specs = in_specs + exchange.specs(exchange.n_in)
        out_specs = out_specs + exchange.specs(exchange.n_out)
        out_shape = out_shape + exchange.out_shapes
        args, scratch = args + exchange.arrays, exchange.scratch()
    return pl.pallas_call(
        body, name="a_flash_fwd", grid=(heads, nq),
        in_specs=in_specs, out_specs=out_specs, out_shape=out_shape, scratch_shapes=scratch,
        compiler_params=_params(("arbitrary", "arbitrary")),
    )(*args)


def _out_proj_in_proj(name, a, gate, w_out, resid, g, w_in, splits, tm):
    tp, wd = a.shape
    d = w_out.shape[1]
    n = w_in.shape[1]

    def body(a_ref, gate_ref, wo_ref, res_ref, g_ref, wi_ref, h_ref, *outs):
        gv = gate_ref[...]
        y = (a_ref[...] * (gv * _sigmoid(gv))).astype(BF16)
        h = res_ref[...] + _dot(y, wo_ref[...])
        h_ref[...] = h
        nrm = ((h * _rms(h)) * g_ref[...]).astype(BF16)
        z = _dot(nrm, wi_ref[...])
        for o_ref, (c0, c1) in zip(outs, splits):
            o_ref[...] = z[:, c0:c1]

    return pl.pallas_call(
        body, name=name, grid=(tp // tm,),
        in_specs=[pl.BlockSpec((tm, wd), lambda i: (i, 0)),
                  pl.BlockSpec((tm, wd), lambda i: (i, 0)),
                  pl.BlockSpec((wd, d), lambda i: (0, 0)),
                  pl.BlockSpec((tm, d), lambda i: (i, 0)),
                  pl.BlockSpec((1, d), lambda i: (0, 0)),
                  pl.BlockSpec((d, n), lambda i: (0, 0))],
        out_specs=[pl.BlockSpec((tm, d), lambda i: (i, 0))]
        + [pl.BlockSpec((tm, c1 - c0), lambda i: (i, 0)) for c0, c1 in splits],
        out_shape=[jax.ShapeDtypeStruct((tp, d), F32)]
        + [jax.ShapeDtypeStruct((tp, c1 - c0), F32) for c0, c1 in splits],
        compiler_params=_params(("parallel",)),
    )(a, gate, w_out, resid, g, w_in)


def _lru_decay(r, sp):
    log_a = -LRU_C * r * sp
    a = jnp.exp(log_a)
    e2 = a * a
    x2 = 2.0 * log_a
    series = x2 * (1.0 + x2 * (0.5 + x2 * (1.0 / 6.0)))
    em1 = jnp.where(x2 > -0.02, series, e2 - 1.0)
    return a, e2, jnp.sqrt(-em1)


def _softplus(x):
    return jnp.maximum(x, 0.0) + jnp.log1p(jnp.exp(-jnp.abs(x)))


def _rglru_fwd(u, conv_w, conv_b, w_rg, b_rg, w_ig, b_ig, lam, pad, tm):
    tp, w = u.shape
    groups, blk = w_rg.shape[0], w_rg.shape[1]

    def body(u_ref, cw_ref, cb_ref, wr_ref, br_ref, wi_ref, bi_ref, lam_ref,
             uc_ref, r_ref, ig_ref, hs_ref, a_s, mult_ref, uext, b_s, hc):
        i = pl.program_id(0)

        @pl.when(i == 0)
        def _():
            uext[0:SUBLANES, :] = jnp.zeros((SUBLANES, w), F32)
            hc[...] = jnp.zeros((SUBLANES, w), F32)

        uext[SUBLANES:SUBLANES + tm, :] = u_ref[...]
        cw = cw_ref[...]
        uc = cb_ref[...] + uext[pl.ds(SUBLANES - 3, tm), :] * cw[0:1, :]
        uc = uc + uext[pl.ds(SUBLANES - 2, tm), :] * cw[1:2, :]
        uc = uc + uext[pl.ds(SUBLANES - 1, tm), :] * cw[2:3, :]
        uc = uc + uext[pl.ds(SUBLANES, tm), :] * cw[3:4, :]
        uc_ref[...] = uc
        uext[0:SUBLANES, :] = uext[tm:tm + SUBLANES, :]

        sp = _softplus(-lam_ref[...])
        for g in range(groups):
            sl = slice(g * blk, (g + 1) * blk)
            ucg = uc_ref[:, sl]
            ucb = ucg.astype(BF16)
            r = _sigmoid(_dot(ucb, wr_ref[g]) + br_ref[:, sl])
            ig = _sigmoid(_dot(ucb, wi_ref[g]) + bi_ref[:, sl])
            r_ref[:, sl] = r
            ig_ref[:, sl] = ig
            a, _, mult = _lru_decay(r, sp[:, sl])
            a_s[:, sl] = a
            mult_ref[:, sl] = mult
            b_s[:, sl] = mult * (ig * ucg)

        @pl.when(i == 0)
        def _():
            row = lax.broadcasted_iota(jnp.int32, (Q_BLOCK, w), 0)
            start = ig_ref[0:Q_BLOCK, :] * uc_ref[0:Q_BLOCK, :]
            b_s[0:Q_BLOCK, :] = jnp.where(row < pad, 0.0, jnp.where(row == pad, start, b_s[0:Q_BLOCK, :]))
            mult_ref[0:Q_BLOCK, :] = jnp.where(row == pad, 1.0, mult_ref[0:Q_BLOCK, :])

        row8 = lax.broadcasted_iota(jnp.int32, (SUBLANES, w), 0)

        def group(gi, h_in):
            off = pl.multiple_of(gi * SUBLANES, SUBLANES)
            av = a_s[pl.ds(off, SUBLANES), :]
            bv = b_s[pl.ds(off, SUBLANES), :]
            for k in (1, 2, 4):
                keep = row8 >= k
                bv = jnp.where(keep, av * pltpu.roll(bv, k, 0) + bv, bv)
                av = jnp.where(keep, av * pltpu.roll(av, k, 0), av)
            hv = av * h_in + bv
            hs_ref[pl.ds(off, SUBLANES), :] = hv
            return jnp.broadcast_to(hv[SUBLANES - 1:SUBLANES, :], (SUBLANES, w))

        hc[...] = lax.fori_loop(0, tm // SUBLANES, group, hc[...])

    row_spec = pl.BlockSpec((tm, w), lambda i: (i, 0))
    vec_spec = pl.BlockSpec((1, w), lambda i: (0, 0))
    mat_spec = pl.BlockSpec((groups, blk, blk), lambda i: (0, 0, 0))
    return pl.pallas_call(
        body, name="b_rglru_fwd", grid=(tp // tm,),
        in_specs=[row_spec, pl.BlockSpec((CONV_WIDTH, w), lambda i: (0, 0)), vec_spec,
                  mat_spec, vec_spec, mat_spec, vec_spec, vec_spec],
        out_specs=[row_spec] * 6,
        out_shape=[jax.ShapeDtypeStruct((tp, w), F32)] * 6,
        scratch_shapes=[pltpu.VMEM((tm + SUBLANES, w), F32), pltpu.VMEM((tm, w), F32),
                        pltpu.VMEM((SUBLANES, w), F32)],
        compiler_params=_params(("arbitrary",)),
    )(u, conv_w, conv_b, w_rg, b_rg, w_ig, b_ig, lam)


def _out_proj_loss(a, gate, w, resid, g, target, x0, tm):
    tp, wd = a.shape
    d = w.shape[1]
    assert x0 % Q_BLOCK == 0 and tm % Q_BLOCK == 0 and target.shape[0] == tp - x0
    lead = x0 // Q_BLOCK
    per = tm // Q_BLOCK

    def body(a_ref, gate_ref, w_ref, res_ref, g_ref, *rest):
        t_refs, (dh_ref, loss_ref, dg_ref, da_ref, dgate_ref, dw_ref) = rest[:per], rest[per:]
        i = pl.program_id(0)

        @pl.when(i == 0)
        def _():
            loss_ref[...] = jnp.zeros_like(loss_ref)
            dg_ref[...] = jnp.zeros_like(dg_ref)
            dw_ref[...] = jnp.zeros_like(dw_ref)

        gate_v = gate_ref[...]
        av = a_ref[...]
        sg = _sigmoid(gate_v)
        silu = gate_v * sg
        y = (av * silu).astype(BF16)
        h = res_ref[...] + _dot(y, w_ref[...])
        gv = g_ref[...]
        for b in range(per):
            rows = slice(b * Q_BLOCK, (b + 1) * Q_BLOCK)
            xv = h[rows, :]
            r = _rms(xv)
            xh = xv * r
            err = jnp.where(i * per + b >= lead, xh * gv - t_refs[b][...], 0.0)
            loss_ref[...] += 0.5 * jnp.sum(jnp.mean(err * err, axis=-1, keepdims=True))
            dy = err / d
            dg_ref[...] += jnp.sum(dy * xh, axis=0, keepdims=True)
            dxh = dy * gv
            dh_ref[rows, :] = r * (dxh - xh * jnp.mean(dxh * xh, axis=-1, keepdims=True))

        dob = dh_ref[...].astype(BF16)
        dyv = _dot_nt(dob, w_ref[...])
        da_ref[...] = dyv * silu
        dgate_ref[...] = dyv * av * (sg * (1.0 + gate_v * (1.0 - sg)))
        dw_ref[...] += _dot_tn(y, dob)

    def piece(b):
        return pl.BlockSpec((Q_BLOCK, d), lambda i: (jnp.maximum(i * per + b - lead, 0), 0))

    return pl.pallas_call(
        body, name="b_out_proj_loss", grid=(tp // tm,),
        in_specs=[pl.BlockSpec((tm, wd), lambda i: (i, 0)),
                  pl.BlockSpec((tm, wd), lambda i: (i, 0)),
                  pl.BlockSpec((wd, d), lambda i: (0, 0)),
                  pl.BlockSpec((tm, d), lambda i: (i, 0)),
                  pl.BlockSpec((1, d), lambda i: (0, 0))] + [piece(b) for b in range(per)],
        out_specs=[pl.BlockSpec((tm, d), lambda i: (i, 0)),
                   pl.BlockSpec((SUBLANES, LANES), lambda i: (0, 0)),
                   pl.BlockSpec((1, d), lambda i: (0, 0)),
                   pl.BlockSpec((tm, wd), lambda i: (i, 0)),
                   pl.BlockSpec((tm, wd), lambda i: (i, 0)),
                   pl.BlockSpec((wd, d), lambda i: (0, 0))],
        out_shape=[jax.ShapeDtypeStruct((tp, d), F32),
                   jax.ShapeDtypeStruct((SUBLANES, LANES), F32),
                   jax.ShapeDtypeStruct((1, d), F32),
                   jax.ShapeDtypeStruct((tp, wd), F32),
                   jax.ShapeDtypeStruct((tp, wd), F32),
                   jax.ShapeDtypeStruct((wd, d), F32)],
        compiler_params=_params(("arbitrary",)),
    )(a, gate, w, resid, g, *([target] * per))


def _with_exchange(exchange, n_in, n_out, first, last, compute):
    if exchange is None:
        return compute
    ex_in, ex_out = exchange.n_in, exchange.n_out

    def body(*refs):
        own_in, their_in = refs[:n_in], refs[n_in:n_in + ex_in]
        pos = n_in + ex_in
        own_out, their_out = refs[pos:pos + n_out], refs[pos + n_out:pos + n_out + ex_out]
        rest = refs[pos + n_out + ex_out:]
        own_scratch, sems = rest[:len(rest) - 2], rest[len(rest) - 2:]

        @pl.when(first())
        def _():
            exchange.start(their_in, their_out, sems)

        compute(*own_in, *own_out, *own_scratch)

        @pl.when(last())
        def _():
            exchange.finish(their_in, their_out, sems)

    return body


def _gated_out_bwd(name, dout, a, gate, w, tm, delta_heads=0, exchange=None):
    tp, wd = a.shape
    d = w.shape[1]
    nt = tp // tm

    def compute(do_ref, a_ref, gate_ref, w_ref, da_ref, dgate_ref, dw_ref, *delta_ref):
        @pl.when(pl.program_id(0) == 0)
        def _():
            dw_ref[...] = jnp.zeros_like(dw_ref)

        gv = gate_ref[...]
        av = a_ref[...]
        sg = _sigmoid(gv)
        silu = gv * sg
        dob = do_ref[...].astype(BF16)
        dy = _dot_nt(dob, w_ref[...])
        da = dy * silu
        da_ref[...] = da
        dgate_ref[...] = dy * av * (sg * (1.0 + gv * (1.0 - sg)))
        dw_ref[...] += _dot_tn((av * silu).astype(BF16), dob)
        for h in range(delta_heads):
            sl = slice(h * V_HEAD, (h + 1) * V_HEAD)
            delta_ref[0][h] = _as_rows(jnp.sum(da[:, sl] * av[:, sl], axis=-1, keepdims=True))

    out_specs = [pl.BlockSpec((tm, wd), lambda i: (i, 0)),
                 pl.BlockSpec((tm, wd), lambda i: (i, 0)),
                 pl.BlockSpec((wd, d), lambda i: (0, 0))]
    out_shape = [jax.ShapeDtypeStruct((tp, wd), F32),
                 jax.ShapeDtypeStruct((tp, wd), F32),
                 jax.ShapeDtypeStruct((wd, d), F32)]
    if delta_heads:
        out_specs.append(pl.BlockSpec((delta_heads, None, SUBLANES, tm), lambda i: (0, i, 0, 0)))
        out_shape.append(jax.ShapeDtypeStruct((delta_heads, tp // tm, SUBLANES, tm), F32))
    in_specs = [pl.BlockSpec((tm, d), lambda i: (i, 0)),
                pl.BlockSpec((tm, wd), lambda i: (i, 0)),
                pl.BlockSpec((tm, wd), lambda i: (i, 0)),
                pl.BlockSpec((wd, d), lambda i: (0, 0))]
    args, scratch = [dout, a, gate, w], []
    body = _with_exchange(exchange, len(in_specs), len(out_specs),
                          lambda: pl.program_id(0) == 0, lambda: pl.program_id(0) == nt - 1, compute)
    if exchange is not None:
        in_specs = in_specs + exchange.specs(exchange.n_in)
        out_specs = out_specs + exchange.specs(exchange.n_out)
        out_shape = out_shape + exchange.out_shapes
        args, scratch = args + exchange.arrays, exchange.scratch()
    return pl.pallas_call(
        body, name=name, grid=(nt,),
        in_specs=in_specs, out_specs=out_specs, out_shape=out_shape, scratch_shapes=scratch,
        compiler_params=_params(("arbitrary",)),
    )(*args)


def _rglru_bwd(dhs, hs, r, ig, uc, u, a, mult, conv_w, w_rg, w_ig, lam, pad, tm):
    tp, w = u.shape
    groups, blk = w_rg.shape[0], w_rg.shape[1]
    nt = tp // tm
    per8 = tm // SUBLANES

    def body(dhs_ref, hs_ref, hprev_ref, r_ref, ig_ref, uc_ref, u_ref, uprev_ref, a_ref, mult_ref,
             cw_ref, wr_ref, wi_ref, lam_ref,
             du_ref, dcw_ref, dcb_ref, dwr_ref, dbr_ref, dwi_ref, dbi_ref, dlam_ref,
             aext, c_s, g_s, hext, uext, ducext, gc):
        step = pl.program_id(0)
        ti = nt - 1 - step

        @pl.when(step == 0)
        def _():
            for ref in (dcw_ref, dcb_ref, dwr_ref, dbr_ref, dwi_ref, dbi_ref, dlam_ref):
                ref[...] = jnp.zeros_like(ref)
            aext[tm:tm + SUBLANES, :] = jnp.zeros((SUBLANES, w), F32)
            ducext[tm:tm + SUBLANES, :] = jnp.zeros((SUBLANES, w), F32)
            gc[...] = jnp.zeros((SUBLANES, w), F32)

        lam_v = lam_ref[...]
        sp = _softplus(-lam_v)
        row = ti * tm + lax.broadcasted_iota(jnp.int32, (tm, w), 0)

        rv = r_ref[...]
        a = a_ref[...]
        mult = mult_ref[...]
        aext[0:tm, :] = a
        c_s[...] = aext[pl.ds(1, tm), :]
        row8 = lax.broadcasted_iota(jnp.int32, (SUBLANES, w), 0)

        def group(gi, g_in):
            off = pl.multiple_of((per8 - 1 - gi) * SUBLANES, SUBLANES)
            cv = c_s[pl.ds(off, SUBLANES), :]
            dv = dhs_ref[pl.ds(off, SUBLANES), :]
            for k in (1, 2, 4):
                keep = row8 < SUBLANES - k
                dv = jnp.where(keep, cv * pltpu.roll(dv, SUBLANES - k, 0) + dv, dv)
                cv = jnp.where(keep, cv * pltpu.roll(cv, SUBLANES - k, 0), cv)
            gv = cv * g_in + dv
            g_s[pl.ds(off, SUBLANES), :] = gv
            return jnp.broadcast_to(gv[0:1, :], (SUBLANES, w))

        gc[...] = lax.fori_loop(0, per8, group, gc[...])
        aext[tm:tm + SUBLANES, :] = aext[0:SUBLANES, :]

        gsc = jnp.where(row < pad, 0.0, g_s[...])
        hext[0:SUBLANES, :] = hprev_ref[...]
        hext[SUBLANES:SUBLANES + tm, :] = hs_ref[...]
        hprev = jnp.where(row == 0, 0.0, hext[pl.ds(SUBLANES - 1, tm), :])
        igv = ig_ref[...]
        ucv = uc_ref[...]
        first = row == pad
        dmult = gsc * (igv * ucv)
        dig = gsc * mult * ucv
        duc = gsc * mult * igv
        dlog_a = (gsc * hprev) * a + jnp.where(first, 0.0, dmult * (-(a * a) / mult))
        dlam_ref[...] += jnp.sum(dlog_a * rv, axis=0, keepdims=True) * (LRU_C * _sigmoid(-lam_v))
        dpre_r = dlog_a * (-LRU_C * sp) * (rv * (1.0 - rv))
        dpre_i = dig * (igv * (1.0 - igv))
        dbr_ref[...] += jnp.sum(dpre_r, axis=0, keepdims=True)
        dbi_ref[...] += jnp.sum(dpre_i, axis=0, keepdims=True)
        for g in range(groups):
            sl = slice(g * blk, (g + 1) * blk)
            ucb = ucv[:, sl].astype(BF16)
            drb = dpre_r[:, sl].astype(BF16)
            dib = dpre_i[:, sl].astype(BF16)
            dwr_ref[g] += _dot_tn(ucb, drb)
            dwi_ref[g] += _dot_tn(ucb, dib)
            ducext[0:tm, sl] = duc[:, sl] + _dot_nt(drb, wr_ref[g]) + _dot_nt(dib, wi_ref[g])

        ducv = ducext[0:tm, :]
        cw = cw_ref[...]
        dcb_ref[...] += jnp.sum(ducv, axis=0, keepdims=True)
        uext[0:SUBLANES, :] = jnp.where(ti == 0, 0.0, uprev_ref[...])
        uext[SUBLANES:SUBLANES + tm, :] = u_ref[...]
        for j in range(CONV_WIDTH):
            ush = uext[pl.ds(SUBLANES - (CONV_WIDTH - 1 - j), tm), :]
            dcw_ref[j:j + 1, :] += jnp.sum(ducv * ush, axis=0, keepdims=True)
        du = ducv * cw[3:4, :]
        for k in range(1, CONV_WIDTH):
            du = du + ducext[pl.ds(k, tm), :] * cw[3 - k:4 - k, :]
        du_ref[...] = du
        ducext[tm:tm + SUBLANES, :] = ducext[0:SUBLANES, :]

    rev = lambda s: (nt - 1 - s, 0)
    halo = lambda s: (jnp.maximum((nt - 1 - s) * per8 - 1, 0), 0)
    row_spec = pl.BlockSpec((tm, w), rev)
    halo_spec = pl.BlockSpec((SUBLANES, w), halo)
    vec_spec = pl.BlockSpec((1, w), lambda s: (0, 0))
    mat_spec = pl.BlockSpec((groups, blk, blk), lambda s: (0, 0, 0))
    cw_spec = pl.BlockSpec((CONV_WIDTH, w), lambda s: (0, 0))
    return pl.pallas_call(
        body, name="b_rglru_bwd", grid=(nt,),
        in_specs=[row_spec, row_spec, halo_spec, row_spec, row_spec, row_spec, row_spec, halo_spec, row_spec, row_spec,
                  cw_spec, mat_spec, mat_spec, vec_spec],
        out_specs=[row_spec, cw_spec, vec_spec, mat_spec, vec_spec, mat_spec, vec_spec, vec_spec],
        out_shape=[jax.ShapeDtypeStruct((tp, w), F32), jax.ShapeDtypeStruct((CONV_WIDTH, w), F32),
                   jax.ShapeDtypeStruct((1, w), F32), jax.ShapeDtypeStruct((groups, blk, blk), F32),
                   jax.ShapeDtypeStruct((1, w), F32), jax.ShapeDtypeStruct((groups, blk, blk), F32),
                   jax.ShapeDtypeStruct((1, w), F32), jax.ShapeDtypeStruct((1, w), F32)],
        scratch_shapes=[pltpu.VMEM((tm + SUBLANES, w), F32), pltpu.VMEM((tm, w), F32), pltpu.VMEM((tm, w), F32),
                        pltpu.VMEM((tm + SUBLANES, w), F32), pltpu.VMEM((tm + SUBLANES, w), F32),
                        pltpu.VMEM((tm + SUBLANES, w), F32), pltpu.VMEM((SUBLANES, w), F32)],
        compiler_params=_params(("arbitrary",)),
    )(dhs, hs, hs, r, ig, uc, u, u, a, mult, conv_w, w_rg, w_ig, lam)


def _norm_matmul_bwd(name, x, g, w, dys, tm, resid=None, prologue=None, extra_out=None, exchange=None,
                     transposed=False):
    tp, kin = x.shape
    n = w.shape[0] if transposed else w.shape[1]
    w_shape = (n, kin) if transposed else (kin, n)
    nt = tp // tm
    n_dy = len(dys)
    has_res = resid is not None
    has_extra = extra_out is not None

    def compute(*refs):
        x_ref, g_ref, w_ref = refs[:3]
        dy_refs = refs[3:3 + n_dy]
        pos = 3 + n_dy
        res_ref = refs[pos] if has_res else None
        pos += int(has_res)
        dx_ref, dw_ref, dg_ref = refs[pos:pos + 3]
        pos += 3
        ex_ref = refs[pos] if has_extra else None
        pos += int(has_extra)
        dy_s = refs[pos]

        @pl.when(pl.program_id(0) == 0)
        def _():
            dw_ref[...] = jnp.zeros_like(dw_ref)
            dg_ref[...] = jnp.zeros_like(dg_ref)

        if prologue is None:
            c0 = 0
            for ref in dy_refs:
                dy_s[:, c0:c0 + ref.shape[1]] = ref[...].astype(BF16)
                c0 += ref.shape[1]
        else:
            prologue(dy_refs, dy_s, ex_ref)

        xv = x_ref[...]
        gv = g_ref[...]
        r = _rms(xv)
        xh = xv * r
        dyb = dy_s[...]
        nb = (xh * gv).astype(BF16)
        if transposed:
            dn = _dot(dyb, w_ref[...])
            dw_ref[...] += _dot_tn(dyb, nb)
        else:
            dn = _dot_nt(dyb, w_ref[...])
            dw_ref[...] += _dot_tn(nb, dyb)
        dg_ref[...] += jnp.sum(dn * xh, axis=0, keepdims=True)
        dxh = dn * gv
        dx = r * (dxh - xh * jnp.mean(dxh * xh, axis=-1, keepdims=True))
        if has_res:
            dx = dx + res_ref[...]
        dx_ref[...] = dx

    row = lambda width: pl.BlockSpec((tm, width), lambda i: (i, 0))
    in_specs = [row(kin), pl.BlockSpec((1, kin), lambda i: (0, 0)), pl.BlockSpec(w_shape, lambda i: (0, 0))]
    in_specs += [row(a.shape[1]) for a in dys]
    args = [x, g, w, *dys]
    if has_res:
        in_specs.append(row(kin))
        args.append(resid)
    out_specs = [row(kin), pl.BlockSpec(w_shape, lambda i: (0, 0)), pl.BlockSpec((1, kin), lambda i: (0, 0))]
    out_shape = [jax.ShapeDtypeStruct((tp, kin), F32), jax.ShapeDtypeStruct(w_shape, F32),
                 jax.ShapeDtypeStruct((1, kin), F32)]
    if has_extra:
        out_specs.append(row(extra_out[0]))
        out_shape.append(jax.ShapeDtypeStruct((tp, extra_out[0]), extra_out[1]))
    scratch = [pltpu.VMEM((tm, n), BF16)]
    body = _with_exchange(exchange, len(in_specs), len(out_specs),
                          lambda: pl.program_id(0) == 0, lambda: pl.program_id(0) == nt - 1, compute)
    if exchange is not None:
        in_specs = in_specs + exchange.specs(exchange.n_in)
        out_specs = out_specs + exchange.specs(exchange.n_out)
        out_shape = out_shape + exchange.out_shapes
        args, scratch = args + exchange.arrays, scratch + exchange.scratch()
    return pl.pallas_call(
        body, name=name, grid=(nt,),
        in_specs=in_specs, out_specs=out_specs, out_shape=out_shape, scratch_shapes=scratch,
        compiler_params=_params(("arbitrary",)),
    )(*args)


def _flash_bwd(q, k, v, lse, delta, do, heads, pad, tq, exchange=None):
    tp = q.shape[0]
    nq = tp // tq
    scale = (QK_NOPE + QK_ROPE) ** -0.5
    c2 = scale * LOG2E

    def compute(q_ref, k_ref, v_ref, lse_ref, delta_ref, do_ref, dq_ref, dk_ref, dv_ref):
        j = pl.program_id(1)

        @pl.when(j == 0)
        def _():
            dq_ref[...] = jnp.zeros_like(dq_ref)

        kv = k_ref[...]
        vv = v_ref[...]

        def rows_of(ref, i, blocks):
            parts = [ref[i + b][0:1, :] for b in range(blocks)]
            return parts[0] if blocks == 1 else jnp.concatenate(parts, axis=1)

        def make_step(masked, blocks):
            def step(i, carry):
                off = pl.multiple_of(i * tq, tq)
                qv = q_ref[pl.ds(off, blocks * tq), :]
                dob = do_ref[pl.ds(off, blocks * tq), :].astype(BF16)
                p = jnp.exp2(_dot_nt(kv, qv) * c2 - rows_of(lse_ref, i, blocks))
                if masked:
                    key = j * tq + lax.broadcasted_iota(jnp.int32, (tq, tq), 0)
                    qry = j * tq + lax.broadcasted_iota(jnp.int32, (tq, tq), 1)
                    first = jnp.where((key <= qry) & (key >= pad), p[:, :tq], 0.0)
                    p = first if blocks == 1 else jnp.concatenate([first, p[:, tq:]], axis=1)
                dv_ref[...] += _dot(p.astype(BF16), dob)
                dp = _dot_nt(vv, dob)
                ds = (p * (dp - rows_of(delta_ref, i, blocks)) * scale).astype(BF16)
                dk_ref[...] += _dot(ds, qv)
                dq_ref[pl.ds(off, blocks * tq), :] += _dot_tn(ds, kv)
                return carry
            return step

        dk_ref[...] = jnp.zeros_like(dk_ref)
        dv_ref[...] = jnp.zeros_like(dv_ref)
        odd = (nq - j) % 2
        lax.fori_loop(0, odd, lambda t, cr: make_step(True, 1)(j, cr), 0)
        lax.fori_loop(0, 1 - odd, lambda t, cr: make_step(True, 2)(j, cr), 0)
        start = j + 2 - odd
        for blocks in (4, 2):
            trips = (nq - start) // blocks
            step_n = make_step(False, blocks)
            lax.fori_loop(0, trips, lambda t, cr, s=start, b=blocks, f=step_n: f(s + b * t, cr), 0)
            start = start + blocks * trips

    in_specs = [pl.BlockSpec((tp, HEAD_W), lambda h, j: (0, h)),
                pl.BlockSpec((tq, HEAD_W), lambda h, j: (j, h)),
                pl.BlockSpec((tq, V_HEAD), lambda h, j: (j, h)),
                pl.BlockSpec((None, nq, SUBLANES, tq), lambda h, j: (h, 0, 0, 0)),
                pl.BlockSpec((None, nq, SUBLANES, tq), lambda h, j: (h, 0, 0, 0)),
                pl.BlockSpec((tp, V_HEAD), lambda h, j: (0, h))]
    out_specs = [pl.BlockSpec((tp, HEAD_W), lambda h, j: (0, h)),
                 pl.BlockSpec((tq, HEAD_W), lambda h, j: (j, h)),
                 pl.BlockSpec((tq, V_HEAD), lambda h, j: (j, h))]
    out_shape = [jax.ShapeDtypeStruct((tp, heads * HEAD_W), F32),
                 jax.ShapeDtypeStruct((tp, heads * HEAD_W), F32),
                 jax.ShapeDtypeStruct((tp, heads * V_HEAD), F32)]
    args, scratch = [q, k, v, lse, delta, do], []
    body = _with_exchange(exchange, len(in_specs), len(out_specs),
                          lambda: (pl.program_id(0) == 0) & (pl.program_id(1) == 0),
                          lambda: (pl.program_id(0) == heads - 1) & (pl.program_id(1) == nq - 1), compute)
    if exchange is not None:
        in_specs = in_specs + exchange.specs(exchange.n_in)
        out_specs = out_specs + exchange.specs(exchange.n_out)
        out_shape = out_shape + exchange.out_shapes
        args, scratch = args + exchange.arrays, exchange.scratch()
    return pl.pallas_call(
        body, name="a_flash_bwd", grid=(heads, nq),
        in_specs=in_specs, out_specs=out_specs, out_shape=out_shape, scratch_shapes=scratch,
        compiler_params=_params(("arbitrary", "arbitrary")),
    )(*args)


def _position():
    return lax.axis_index("x"), lax.axis_index("y"), lax.axis_index("c")


def _other_chips(x, y):
    return [(1 - x, y), (x, 1 - y), (1 - x, 1 - y)]


def _block(ref, shard_axis, n, k, split_axis=None, m=None, h=None):
    idx = []
    for a in range(len(ref.shape)):
        start = 0
        size = None
        if a == shard_axis:
            start, size = k * n, n
        if a == split_axis:
            size = (n if a == shard_axis else m) // 2
            start = start + h * size
        idx.append(slice(None) if size is None else pl.ds(start, size))
    return ref.at[tuple(idx)]


def _gather_weights(split, whole_small):
    ns, nw = len(split), len(whole_small)
    n = ns + nw
    arrs = [s[0] for s in split] + [s[0] for s in whole_small]
    axes = [s[1] for s in split] + [s[1] for s in whole_small]

    def body(*refs):
        ins, outs = refs[:n], refs[n:2 * n]
        ici_send, ici_recv, d2d_send, d2d_recv, sib_send, sib_recv = refs[2 * n:]
        x, y, c = _position()
        me = 2 * x + y
        others = _other_chips(x, y)
        sent, local = [], []

        def remote(src, dst, sems, idx, to):
            return pltpu.make_async_remote_copy(src_ref=src, dst_ref=dst, send_sem=sems[0].at[idx],
                                                recv_sem=sems[1].at[idx], device_id=to, device_id_type=MESH)

        for a in range(n):
            width = ins[a].shape[axes[a]]
            mine = remote(ins[a], _block(outs[a], axes[a], width, me), (sib_send, sib_recv), a, (x, y, 1 - c))
            mine.start()
            local.append(mine)
            for j, (px, py) in enumerate(others):
                if a < ns:
                    sx = split[a][2]
                    src = _block(ins[a], None, None, None, sx, ins[a].shape[sx], c)
                    dst = _block(outs[a], axes[a], width, me, sx, outs[a].shape[sx], c)
                else:
                    src, dst = ins[a], _block(outs[a], axes[a], width, me)
                cp = remote(src, dst, (ici_send, ici_recv), 3 * a + j, (px, py, c))
                cp.start()
                sent.append(cp)
        for a in range(ns):
            width = ins[a].shape[axes[a]]
            sx = split[a][2]
            for j, (px, py) in enumerate(others):
                theirs = _block(outs[a], axes[a], width, 2 * px + py, sx, outs[a].shape[sx], c)
                remote(theirs, theirs, (ici_send, ici_recv), 3 * a + j, (px, py, c)).wait_recv()
                fwd = remote(theirs, theirs, (d2d_send, d2d_recv), 3 * a + j, (x, y, 1 - c))
                fwd.start()
                sent.append(fwd)
        for a in range(ns, n):
            width = ins[a].shape[axes[a]]
            for j, (px, py) in enumerate(others):
                theirs = _block(outs[a], axes[a], width, 2 * px + py)
                remote(theirs, theirs, (ici_send, ici_recv), 3 * a + j, (px, py, c)).wait_recv()
        for a in range(ns):
            width = ins[a].shape[axes[a]]
            sx = split[a][2]
            for j, (px, py) in enumerate(others):
                from_sibling = _block(outs[a], axes[a], width, 2 * px + py, sx, outs[a].shape[sx], 1 - c)
                remote(from_sibling, from_sibling, (d2d_send, d2d_recv), 3 * a + j, (x, y, 1 - c)).wait_recv()
        for cp in sent:
            cp.wait_send()
        for cp in local:
            cp.wait()

    def whole_shape(a, axis):
        shape = list(a.shape)
        shape[axis] *= N_CHIPS
        return jax.ShapeDtypeStruct(tuple(shape), a.dtype)

    any_spec = pl.BlockSpec(memory_space=pl.ANY)
    return pl.pallas_call(
        body, name="gather_weights",
        in_specs=[any_spec] * n, out_specs=[any_spec] * n,
        out_shape=[whole_shape(a, ax) for a, ax in zip(arrs, axes)],
        scratch_shapes=[pltpu.SemaphoreType.DMA((3 * n,)), pltpu.SemaphoreType.DMA((3 * n,)),
                        pltpu.SemaphoreType.DMA((3 * ns,)), pltpu.SemaphoreType.DMA((3 * ns,)),
                        pltpu.SemaphoreType.DMA((n,)), pltpu.SemaphoreType.DMA((n,))],
        compiler_params=pltpu.CompilerParams(has_side_effects=True),
    )(*arrs)


class _Grad:
    def __init__(self, name, g, kind, rh, cols, groups=None):
        self.name, self.g, self.kind, self.rh, self.cols, self.groups = name, g, kind, rh, cols, groups
        if kind == 'rows':
            self.tr = rh
        elif kind == 'gate':
            self.tr = rh // (groups // 2)
        else:
            self.tr = rh if rh <= 512 else 256
        self.nb = rh // self.tr

    def pieces(self, ref, k, h):
        rh, cols = self.rh, self.cols
        if self.kind == 'cols':
            return [(ref.at[pl.ds(h * rh, rh), pl.ds(k * cols, cols)], 0, rh)]
        if self.kind == 'rows':
            return [(ref.at[pl.ds((2 * k + h) * rh, rh), :], 0, rh)]
        if self.kind == 'lead':
            return [(ref.at[k, pl.ds(h * rh, rh), :], 0, rh)]
        per = self.groups // 2
        return [(ref.at[pl.ds((((h * per + gi) * N_CHIPS) + k) * self.tr, self.tr), :], gi * self.tr, self.tr)
                for gi in range(per)]

    def block_spec(self):
        tr, nb, cols = self.tr, self.nb, self.cols
        if self.kind == 'cols':
            return pl.BlockSpec((tr, cols), lambda k, i, c: (c[0] * nb + i, k))
        if self.kind == 'rows':
            return pl.BlockSpec((tr, cols), lambda k, i, c: (2 * k + c[0], 0))
        if self.kind == 'lead':
            return pl.BlockSpec((None, tr, cols), lambda k, i, c: (k, c[0] * nb + i, 0))
        return pl.BlockSpec((tr, cols), lambda k, i, c: ((c[0] * nb + i) * N_CHIPS + k, 0))


class _Exchange:
    def __init__(self, name, arrays, out_shapes, n_copies, copies):
        self.name, self.arrays, self.out_shapes, self.n_copies, self.copies = name, arrays, out_shapes, n_copies, copies
        self.n_in, self.n_out = len(arrays), len(out_shapes)

    def specs(self, n):
        return [pl.BlockSpec(memory_space=pl.ANY)] * n

    def scratch(self):
        return [pltpu.SemaphoreType.DMA((self.n_copies,)), pltpu.SemaphoreType.DMA((self.n_copies,))]

    def _descriptors(self, in_refs, out_refs, sems):
        return self.copies(in_refs, out_refs, sems[0], sems[1])

    def start(self, in_refs, out_refs, sems):
        for cp in self._descriptors(in_refs, out_refs, sems):
            cp.start()

    def finish(self, in_refs, out_refs, sems):
        for cp in self._descriptors(in_refs, out_refs, sems):
            cp.wait()

    def __add__(self, other):
        def copies(ins, outs, send_sems, recv_sems, base=0):
            return (self.copies(ins[:self.n_in], outs[:self.n_out], send_sems, recv_sems, base)
                    + other.copies(ins[self.n_in:], outs[self.n_out:], send_sems, recv_sems, base + self.n_copies))

        return _Exchange(self.name + "_" + other.name, self.arrays + other.arrays, self.out_shapes + other.out_shapes,
                         self.n_copies + other.n_copies, copies)

    def run(self):
        def body(*refs):
            ins, outs, sems = refs[:self.n_in], refs[self.n_in:self.n_in + self.n_out], refs[self.n_in + self.n_out:]
            self.start(ins, outs, sems)
            self.finish(ins, outs, sems)

        return pl.pallas_call(
            body, name=self.name,
            in_specs=self.specs(self.n_in), out_specs=self.specs(self.n_out), out_shape=self.out_shapes,
            scratch_shapes=self.scratch(),
            compiler_params=pltpu.CompilerParams(has_side_effects=True),
        )(*self.arrays)


def _gather_whole(name, shards):
    def copies(ins, outs, send_sems, recv_sems, base=0):
        x, y, c = _position()
        me = 2 * x + y
        made = []
        for a, (_, axis) in enumerate(shards):
            dst = _block(outs[a], axis, ins[a].shape[axis], me)
            for j, to in enumerate([(x, y, 1 - c)] + [(px, py, c) for px, py in _other_chips(x, y)]):
                idx = base + 4 * a + j
                made.append(pltpu.make_async_remote_copy(
                    src_ref=ins[a], dst_ref=dst, send_sem=send_sems.at[idx], recv_sem=recv_sems.at[idx],
                    device_id=to, device_id_type=MESH))
        return made

    def whole_shape(a, axis):
        shape = list(a.shape)
        shape[axis] *= N_CHIPS
        return jax.ShapeDtypeStruct(tuple(shape), a.dtype)

    return _Exchange(name, [s[0] for s in shards], [whole_shape(*s) for s in shards], 4 * len(shards), copies)


def _halves_to_sibling(name, grads):
    total = sum(len(gr.pieces(gr.g, 0, 0)) * N_CHIPS for gr in grads)

    def copies(ins, outs, send_sems, recv_sems, base=0):
        x, y, c = _position()
        made = []
        for gr, g_ref, got_ref in zip(grads, ins, outs):
            for k in range(N_CHIPS):
                for src, r0, nr in gr.pieces(g_ref, k, 1 - c):
                    idx = base + len(made)
                    made.append(pltpu.make_async_remote_copy(
                        src_ref=src, dst_ref=got_ref.at[k, pl.ds(r0, nr), :],
                        send_sem=send_sems.at[idx], recv_sem=recv_sems.at[idx],
                        device_id=(x, y, 1 - c), device_id_type=MESH))
        return made

    return _Exchange(name, [gr.g for gr in grads],
                     [jax.ShapeDtypeStruct((N_CHIPS, gr.rh, gr.cols), F32) for gr in grads], total, copies)


def _chip_sum(gr, got, c, wire=BF16):
    def body(c_ref, g_ref, got_ref, o_ref):
        o_ref[...] = (g_ref[...] + got_ref[...]).astype(wire)

    tile = pl.BlockSpec((None, gr.tr, gr.cols), lambda k, i, c_ref: (k, i, 0))
    return pl.pallas_call(
        body, name="chip_sum_" + gr.name,
        grid_spec=pltpu.PrefetchScalarGridSpec(
            num_scalar_prefetch=1, grid=(N_CHIPS, gr.nb),
            in_specs=[gr.block_spec(), tile], out_specs=tile),
        out_shape=jax.ShapeDtypeStruct((N_CHIPS, gr.rh, gr.cols), wire),
        compiler_params=_params(("parallel", "parallel")),
    )(c, gr.g, got)


def _blocks_to_chips(name, parts):
    n = len(parts)

    def copies(ins, outs, send_sems, recv_sems, base=0):
        x, y, c = _position()
        made = []
        for a in range(n):
            for j, (px, py) in enumerate(_other_chips(x, y)):
                idx = base + 3 * a + j
                made.append(pltpu.make_async_remote_copy(
                    src_ref=ins[a].at[2 * px + py], dst_ref=outs[a].at[j],
                    send_sem=send_sems.at[idx], recv_sem=recv_sems.at[idx],
                    device_id=(px, py, c), device_id_type=MESH))
        return made

    return _Exchange(name, parts, [jax.ShapeDtypeStruct((3,) + p.shape[1:], p.dtype) for p in parts], 3 * n, copies)


def _sum_chips(name, part, got, me, c):
    nk, rh, cols = part.shape
    tr = rh if rh <= 512 else 256
    nb = rh // tr

    def body(me_ref, c_ref, own_ref, *rest):
        got_refs, o_ref = rest[:nk], rest[nk]
        own = own_ref[...].astype(F32)
        acc = None
        for k in range(nk):
            term = jnp.where(me_ref[0] == k, own, got_refs[k][...].astype(F32))
            acc = term if acc is None else acc + term
        o_ref[...] = acc

    def got_map(k):
        def index(i, me_ref, c_ref):
            xor = jnp.bitwise_xor(me_ref[0], k)
            slot = jnp.where(xor == 1, 1, jnp.where(xor == 3, 2, 0))
            return (slot, i, 0)
        return index

    return pl.pallas_call(
        body, name="sum_" + name,
        grid_spec=pltpu.PrefetchScalarGridSpec(
            num_scalar_prefetch=2, grid=(nb,),
            in_specs=[pl.BlockSpec((None, tr, cols), lambda i, me_ref, c_ref: (me_ref[0], i, 0))]
            + [pl.BlockSpec((None, tr, cols), got_map(k)) for k in range(nk)],
            out_specs=pl.BlockSpec((tr, cols), lambda i, me_ref, c_ref: (c_ref[0] * nb + i, 0))),
        out_shape=jax.ShapeDtypeStruct((2 * rh, cols), F32),
        compiler_params=_params(("parallel",)),
    )(me, c, part, *([got] * nk))


def _share_with_sibling(halves):
    n = len(halves)

    def body(*refs):
        outs = refs[n:2 * n]
        send_sems, recv_sems = refs[2 * n:]
        x, y, c = _position()
        copies = []
        for a in range(n):
            rh = outs[a].shape[0] // 2
            mine = outs[a].at[pl.ds(c * rh, rh), :]
            cp = pltpu.make_async_remote_copy(
                src_ref=mine, dst_ref=mine, send_sem=send_sems.at[a], recv_sem=recv_sems.at[a],
                device_id=(x, y, 1 - c), device_id_type=MESH)
            cp.start()
            copies.append(cp)
        for cp in copies:
            cp.wait()

    any_spec = pl.BlockSpec(memory_space=pl.ANY)
    return pl.pallas_call(
        body, name="grads_share",
        in_specs=[any_spec] * n, out_specs=[any_spec] * n,
        out_shape=[jax.ShapeDtypeStruct(h.shape, h.dtype) for h in halves],
        input_output_aliases={a: a for a in range(n)},
        scratch_shapes=[pltpu.SemaphoreType.DMA((n,)), pltpu.SemaphoreType.DMA((n,))],
        compiler_params=pltpu.CompilerParams(has_side_effects=True),
    )(*halves)


def _adamw(name, w, g, m, v):
    rows, cols = w.shape
    tr = 256 if rows % 256 == 0 else rows

    def body(w_ref, g_ref, m_ref, v_ref, g_out_ref, d_ref, nm_ref, nv_ref):
        gv = g_ref[...]
        g_out_ref[...] = gv
        mn = ADAM_B1 * m_ref[...] + (1.0 - ADAM_B1) * gv
        vn = ADAM_B2 * v_ref[...] + (1.0 - ADAM_B2) * (gv * gv)
        m_hat = mn / (1.0 - ADAM_B1 ** ADAM_STEP)
        v_hat = vn / (1.0 - ADAM_B2 ** ADAM_STEP)
        d_ref[...] = -ADAM_LR * (m_hat / (jnp.sqrt(v_hat) + ADAM_EPS) + ADAM_WD * w_ref[...])
        nm_ref[...] = mn
        nv_ref[...] = vn

    spec = pl.BlockSpec((tr, cols), lambda i: (i, 0))
    return pl.pallas_call(
        body, name=name, grid=(rows // tr,),
        in_specs=[spec] * 4, out_specs=[spec] * 4,
        out_shape=[jax.ShapeDtypeStruct((rows, cols), F32)] * 4,
        compiler_params=_params(("parallel",)),
    )(w, g, m, v)


def _as2d(a):
    if a.ndim == 1:
        return a.reshape(1, -1)
    return a.reshape(-1, a.shape[-1])


def _rope_tables(tp, pad):
    pos = jnp.arange(tp, dtype=F32) - pad
    inv_freq = ROPE_BASE ** (-jnp.arange(0, QK_ROPE, 2, dtype=F32) / QK_ROPE)
    ang = pos[:, None] * inv_freq[None, :]
    cos, sin = jnp.cos(ang), jnp.sin(ang)
    zeros = jnp.zeros((tp, LANES - QK_ROPE), F32)
    return jnp.concatenate([cos, cos, zeros], axis=1), jnp.concatenate([-sin, sin, zeros], axis=1)


def _matrix_grad(name, g, heads):
    if name in ('b_w_rg', 'b_w_ig'):
        groups, blk, cols = g.shape
        return _Grad(name, g.reshape(groups * blk, cols), 'gate', (groups // 2) * (blk // N_CHIPS), cols, groups)
    rows, cols = g.shape
    if name in ('a_w_in', 'a_w_out', 'b_w_out'):
        return _Grad(name, g, 'rows', rows // (2 * N_CHIPS), cols)
    if name == 'a_w_uq' and heads % N_CHIPS != 0:
        g = g.reshape(rows, heads, HEAD_W)[:, :, :QK_NOPE + QK_ROPE].reshape(rows, -1)
        cols = g.shape[1]
        g = jnp.moveaxis(g.reshape(rows, N_CHIPS, cols // N_CHIPS), 1, 0)
        return _Grad(name, g, 'lead', rows // 2, cols // N_CHIPS)
    return _Grad(name, g, 'cols', rows // 2, cols // N_CHIPS)


def _kernel_form(name, w):
    return w[0] if name in ('b_w_rg', 'b_w_ig', 'b_conv_w') else _as2d(w)


def _local_grads(x, target, wt, heads, c_idx, mid_names, mid_gather, late_names, late_gather):
    wt = dict(wt)
    seq, d = x.shape
    n_meta = wt['meta_tokens'].shape[0]
    t = seq + n_meta
    pad = (-t) % Q_BLOCK
    tp = t + pad
    x0 = pad + n_meta
    tm = _row_tile(tp)
    ql = wt['a_q_norm_g'].shape[1]
    kvl = wt['a_kv_norm_g'].shape[1]
    mla_w = heads * V_HEAD

    h0 = jnp.concatenate([jnp.zeros((pad, d), F32), wt['meta_tokens'], x], axis=0)
    cos_t, sin_t = _rope_tables(tp, pad)

    w_in_a = wt['a_w_in']
    zrow = jnp.zeros((LANES - QK_ROPE, d), BF16)
    w_in_a = jnp.concatenate([w_in_a[:ql + kvl + QK_ROPE], zrow, w_in_a[ql + kvl + QK_ROPE:]], axis=0)
    c_kv, c_kr, c_gate = ql, ql + kvl, ql + kvl + LANES
    splits_a = [(0, c_kv), (c_kv, c_kr), (c_kr, c_gate), (c_gate, c_gate + mla_w)]

    q_lat, kv_lat, kr_raw, gate_a, *mid_whole = _norm_matmul("a_in_proj", h0, wt['a_norm_g'], w_in_a, splits_a, tm,
                                                             exchange=mid_gather)
    wt.update({n: _kernel_form(n, w) for n, w in zip(mid_names, mid_whole)})
    w_uq = wt['a_w_uq'].reshape(ql, heads, QK_NOPE + QK_ROPE)
    w_uq = jnp.pad(w_uq, ((0, 0), (0, 0), (0, HEAD_W - QK_NOPE - QK_ROPE))).reshape(ql, heads * HEAD_W)
    w_ukv = wt['a_w_ukv']
    q = _q_proj(q_lat, wt['a_q_norm_g'], w_uq, cos_t, sin_t, heads, tm)
    k, v = _kv_proj(kv_lat, wt['a_kv_norm_g'], w_ukv, kr_raw, cos_t, sin_t, heads, tm)
    attn, lse, *late_whole = _flash_fwd(q, k, v, heads, pad, tm, exchange=late_gather)
    wt.update({n: _kernel_form(n, w) for n, w in zip(late_names, late_whole)})
    lru_w = wt['b_conv_w'].shape[1]

    h1, u, gate_b = _out_proj_in_proj("a_out_b_in_proj", attn, gate_a, wt['a_w_out'], h0, wt['b_norm_g'], wt['b_w_in'],
                                      [(0, lru_w), (lru_w, 2 * lru_w)], tm)
    uc, r, ig, hs, decay, mult = _rglru_fwd(u, wt['b_conv_w'], wt['b_conv_b'], wt['b_w_rg'], wt['b_b_rg'],
                                            wt['b_w_ig'], wt['b_b_ig'], wt['b_lam'], pad, tm)

    dh2, loss, d_final_g, dhs, dgate_b, dw_out_b = _out_proj_loss(
        hs, gate_b, wt['b_w_out'], h1, wt['final_norm_g'], target, x0, tm)
    du, dconv_w, dconv_b, dw_rg, db_rg, dw_ig, db_ig, dlam = _rglru_bwd(
        dhs, hs, r, ig, uc, u, decay, mult, wt['b_conv_w'], wt['b_w_rg'], wt['b_w_ig'], wt['b_lam'], pad, tm)
    dh1, dw_in_b, dg_b = _norm_matmul_bwd("b_in_proj_bwd", h1, wt['b_norm_g'], wt['b_w_in'], [du, dgate_b], tm, resid=dh2)

    grads_b = [_matrix_grad(n, g, heads) for n, g in
               (('b_w_in', dw_in_b), ('b_w_rg', dw_rg), ('b_w_ig', dw_ig), ('b_w_out', dw_out_b))]
    dattn, dgate_a, dw_out_a, delta, *got = _gated_out_bwd(
        "a_out_proj_bwd", dh1, attn, gate_a, wt['a_w_out'], tm, delta_heads=heads,
        exchange=_halves_to_sibling("swap_b", grads_b))
    sums_b = [_chip_sum(gr, r, c_idx) for gr, r in zip(grads_b, got)]
    grad_out = _matrix_grad('a_w_out', dw_out_a, heads)
    dq, dk, dv, *landed = _flash_bwd(
        q, k, v, lse, delta, dattn, heads, pad, tm,
        exchange=_blocks_to_chips("chips_b", sums_b) + _halves_to_sibling("swap_out", [grad_out]))
    through = list(zip(grads_b, sums_b, landed[:len(grads_b)]))
    sum_out = _chip_sum(grad_out, landed[len(grads_b)], c_idx)

    def q_prologue(dy_refs, dy_s, ex_ref):
        (dq_ref,), cos_v, sin_v = dy_refs[:1], dy_refs[1][...], dy_refs[2][...]
        for h in range(heads):
            c0 = h * HEAD_W
            dy_s[:, c0:c0 + QK_NOPE] = dq_ref[:, c0:c0 + QK_NOPE].astype(BF16)
            dy_s[:, c0 + QK_NOPE:c0 + HEAD_W] = _unrope(dq_ref[:, c0 + QK_NOPE:c0 + HEAD_W], cos_v, sin_v).astype(BF16)

    dq_lat, dw_uq, dg_q, from_chips_out = _norm_matmul_bwd(
        "a_q_proj_bwd", q_lat, wt['a_q_norm_g'], w_uq, [dq, cos_t, sin_t], tm, prologue=q_prologue,
        exchange=_blocks_to_chips("chips_out", [sum_out]))
    through.append((grad_out, sum_out, from_chips_out))
    grad_uq = _matrix_grad('a_w_uq', dw_uq, heads)

    def kv_prologue(dy_refs, dy_s, ex_ref):
        dk_ref, dv_ref = dy_refs[:2]
        cos_v, sin_v = dy_refs[2][...], dy_refs[3][...]
        dkr = jnp.zeros((dk_ref.shape[0], LANES), F32)
        for h in range(heads):
            c0 = h * (QK_NOPE + V_HEAD)
            dy_s[:, c0:c0 + QK_NOPE] = dk_ref[:, h * HEAD_W:h * HEAD_W + QK_NOPE].astype(BF16)
            dy_s[:, c0 + QK_NOPE:c0 + QK_NOPE + V_HEAD] = dv_ref[:, h * V_HEAD:(h + 1) * V_HEAD].astype(BF16)
            dkr = dkr + dk_ref[:, h * HEAD_W + QK_NOPE:(h + 1) * HEAD_W]
        ex_ref[...] = _unrope(dkr, cos_v, sin_v)

    dkv_lat, dw_ukv, dg_kv, dkr_raw, got_uq = _norm_matmul_bwd(
        "a_kv_proj_bwd", kv_lat, wt['a_kv_norm_g'], w_ukv, [dk, dv, cos_t, sin_t], tm,
        prologue=kv_prologue, extra_out=(LANES, F32), exchange=_halves_to_sibling("swap_uq", [grad_uq]))
    sum_uq = _chip_sum(grad_uq, got_uq, c_idx)
    grad_ukv = _matrix_grad('a_w_ukv', dw_ukv, heads)

    dh0, dw_in_a, dg_a, from_chips_uq, got_ukv = _norm_matmul_bwd(
        "a_in_proj_bwd", h0, wt['a_norm_g'], w_in_a, [dq_lat, dkv_lat, dkr_raw, dgate_a], tm, resid=dh1, transposed=True,
        exchange=_blocks_to_chips("chips_uq", [sum_uq]) + _halves_to_sibling("swap_ukv", [grad_ukv]))
    through.append((grad_uq, sum_uq, from_chips_uq))
    swapped = [(grad_ukv, _chip_sum(grad_ukv, got_ukv, c_idx))]

    dw_in_a = jnp.concatenate([dw_in_a[:c_kr + QK_ROPE], dw_in_a[c_gate:]], axis=0)
    grads = {
        'meta_tokens': dh0[pad:x0], 'a_norm_g': dg_a, 'a_w_in': dw_in_a, 'a_q_norm_g': dg_q, 'a_kv_norm_g': dg_kv,
        'a_w_uq': dw_uq, 'a_w_ukv': dw_ukv, 'a_w_out': dw_out_a, 'b_norm_g': dg_b, 'b_w_in': dw_in_b,
        'b_conv_w': dconv_w, 'b_conv_b': dconv_b, 'b_w_rg': dw_rg, 'b_b_rg': db_rg, 'b_w_ig': dw_ig,
        'b_b_ig': db_ig, 'b_lam': dlam, 'b_w_out': dw_out_b, 'final_norm_g': d_final_g,
    }
    return loss, dh0[x0:], grads, through, swapped


def _chip_major(whole, local_shape, axis):
    if axis is None:
        return jnp.broadcast_to(whole.reshape(1, -1), (N_CHIPS, whole.size))
    shape = list(local_shape)
    g = whole.reshape(shape[:axis] + [N_CHIPS, shape[axis]] + shape[axis + 1:])
    return jnp.moveaxis(g, axis, 0).reshape(N_CHIPS, -1)


def kernel(x, meta_tokens, a_norm_g, a_w_in, a_q_norm_g, a_kv_norm_g, a_w_uq, a_w_ukv, a_w_out, b_norm_g, b_w_in, b_conv_w, b_conv_b, b_w_rg, b_b_rg, b_w_ig, b_b_ig, b_lam, b_w_out, final_norm_g, loss_target, m_meta_tokens, m_a_norm_g, m_a_w_in, m_a_q_norm_g, m_a_kv_norm_g, m_a_w_uq, m_a_w_ukv, m_a_w_out, m_b_norm_g, m_b_w_in, m_b_conv_w, m_b_conv_b, m_b_w_rg, m_b_b_rg, m_b_w_ig, m_b_b_ig, m_b_lam, m_b_w_out, m_final_norm_g, v_meta_tokens, v_a_norm_g, v_a_w_in, v_a_q_norm_g, v_a_kv_norm_g, v_a_w_uq, v_a_w_ukv, v_a_w_out, v_b_norm_g, v_b_w_in, v_b_conv_w, v_b_conv_b, v_b_w_rg, v_b_b_rg, v_b_w_ig, v_b_b_ig, v_b_lam, v_b_w_out, v_final_norm_g):
    local_w = dict(zip(WEIGHTS, (meta_tokens, a_norm_g, a_w_in, a_q_norm_g, a_kv_norm_g, a_w_uq, a_w_ukv, a_w_out,
                                 b_norm_g, b_w_in, b_conv_w, b_conv_b, b_w_rg, b_b_rg, b_w_ig, b_b_ig, b_lam,
                                 b_w_out, final_norm_g)))
    local_m = dict(zip(WEIGHTS, (m_meta_tokens, m_a_norm_g, m_a_w_in, m_a_q_norm_g, m_a_kv_norm_g, m_a_w_uq,
                                 m_a_w_ukv, m_a_w_out, m_b_norm_g, m_b_w_in, m_b_conv_w, m_b_conv_b, m_b_w_rg,
                                 m_b_b_rg, m_b_w_ig, m_b_b_ig, m_b_lam, m_b_w_out, m_final_norm_g)))
    local_v = dict(zip(WEIGHTS, (v_meta_tokens, v_a_norm_g, v_a_w_in, v_a_q_norm_g, v_a_kv_norm_g, v_a_w_uq,
                                 v_a_w_ukv, v_a_w_out, v_b_norm_g, v_b_w_in, v_b_conv_w, v_b_conv_b, v_b_w_rg,
                                 v_b_b_rg, v_b_w_ig, v_b_b_ig, v_b_lam, v_b_w_out, v_final_norm_g)))
    matrices = ('a_w_in', 'a_w_uq', 'a_w_ukv', 'a_w_out', 'b_w_in', 'b_w_rg', 'b_w_ig', 'b_w_out')
    heads = a_w_uq.shape[-1] * N_CHIPS // (QK_NOPE + QK_ROPE)

    def transposed(a):
        return jnp.swapaxes(a, 1, 2)

    split, small, mid, late = [], [], [], []
    for n in WEIGHTS:
        if SHARD_AXIS[n] is None:
            continue
        if n.startswith('b_') or n == 'a_w_out':
            late.append((n, local_w[n].astype(BF16) if n in matrices else local_w[n], SHARD_AXIS[n]))
        elif n == 'a_w_in':
            split.append((n, transposed(local_w[n]).astype(BF16), 1, 2))
        elif n in matrices:
            mid.append((n, local_w[n].astype(BF16), SHARD_AXIS[n]))
        else:
            small.append((n, local_w[n], SHARD_AXIS[n]))
    gathered = _gather_weights([s[1:] for s in split], [s[1:] for s in small])
    whole = dict(zip([s[0] for s in split + small], gathered))
    mid_names, late_names = [s[0] for s in mid], [s[0] for s in late]
    wt = {n: _kernel_form(n, whole.get(n, local_w[n])) for n in WEIGHTS if n not in mid_names + late_names}

    c_idx = lax.axis_index("c").astype(jnp.int32).reshape(1)
    me_idx = (2 * lax.axis_index("x") + lax.axis_index("y")).astype(jnp.int32).reshape(1)
    loss, grad_x, grads, through, swapped = _local_grads(
        x[0], loss_target[0], wt, heads, c_idx,
        mid_names, _gather_whole("gather_weights_a", [s[1:] for s in mid]),
        late_names, _gather_whole("gather_weights_b", [s[1:] for s in late]))

    ext_uq = heads % N_CHIPS == 0
    started = [gr.name for gr, *_ in through + swapped]
    last = [_matrix_grad(n, grads[n], heads) for n in matrices if n not in started]
    rest = [n for n in WEIGHTS if n not in matrices]
    pieces = [_chip_major(grads[n], local_w[n].shape, SHARD_AXIS[n]) for n in rest]
    pieces.append(jnp.broadcast_to(loss[0:1, 0:1], (N_CHIPS, 1)))
    length = sum(p.shape[1] for p in pieces)
    unit = 2 * SUBLANES * 1024
    padded = -(-length // unit) * unit
    flat = jnp.concatenate(pieces + [jnp.zeros((N_CHIPS, padded - length), F32)], axis=1)
    last.append(_Grad('small', flat.reshape(N_CHIPS, padded // 1024, 1024), 'lead', padded // 2048, 1024))

    got = _halves_to_sibling("grads_to_sibling", last).run()
    swapped = swapped + [(gr, _chip_sum(gr, r, c_idx, F32 if gr.name == 'small' else BF16))
                         for gr, r in zip(last, got)]
    from_chips = _blocks_to_chips("grads_to_chips", [p for _, p in swapped]).run()
    through = through + [(gr, p, r) for (gr, p), r in zip(swapped, from_chips)]
    halves = [_sum_chips(gr.name, p, r, me_idx, c_idx) for gr, p, r in through]
    summed = dict(zip([gr.name for gr, _, _ in through], _share_with_sibling(halves)))
    if ext_uq:
        g = summed['a_w_uq']
        summed['a_w_uq'] = g.reshape(g.shape[0], -1, HEAD_W)[:, :, :QK_NOPE + QK_ROPE]
    total = summed['small'].reshape(-1)

    out_g, out_d, out_m, out_v = [], [], [], []
    off = 0
    for n in WEIGHTS:
        shape = local_w[n].shape
        view = transposed if n == 'a_w_in' else (lambda a: a)
        if n in matrices:
            g = summed[n].reshape(view(local_w[n]).shape)
        else:
            size = 1
            for s in shape:
                size *= s
            g = total[off:off + size].reshape(shape)
            off += size
        results = _adamw("adamw_" + n, _as2d(view(local_w[n])), _as2d(g), _as2d(view(local_m[n])), _as2d(view(local_v[n])))
        for out, r in zip((out_g, out_d, out_m, out_v), results):
            out.append(view(r.reshape(view(local_w[n]).shape)))

    return (total[off], grad_x[None], *out_g, *out_d, *out_m, *out_v)
```

```python
import functools

import jax
import jax.numpy as jnp
from jax import lax
from jax.experimental import pallas as pl
from jax.experimental.pallas import tpu as pltpu

F32 = jnp.float32
BF16 = jnp.bfloat16
MESH = pl.DeviceIdType.MESH

RMS_EPS = 1e-6
QK_NOPE = 128
QK_ROPE = 64
V_HEAD = 128
HEAD_W = 256
ROPE_BASE = 10000.0
Q_BLOCK = 128
MASK_VALUE = -1e30
CONV_WIDTH = 4
LRU_C = 8.0
N_CHIPS = 4

ADAM_LR = 0.001
ADAM_B1 = 0.9
ADAM_B2 = 0.999
ADAM_EPS = 1e-08
ADAM_WD = 0.01
ADAM_STEP = 10

VMEM_LIMIT_V7X = 56 * 1024 * 1024
LANES = 128
SUBLANES = 8

WEIGHTS = ['meta_tokens', 'a_norm_g', 'a_w_in', 'a_q_norm_g', 'a_kv_norm_g', 'a_w_uq', 'a_w_ukv',
           'a_w_out', 'b_norm_g', 'b_w_in', 'b_conv_w', 'b_conv_b', 'b_w_rg', 'b_b_rg', 'b_w_ig',
           'b_b_ig', 'b_lam', 'b_w_out', 'final_norm_g']
SHARD_AXIS = {'meta_tokens': 1, 'a_norm_g': None, 'a_w_in': 2, 'a_q_norm_g': None, 'a_kv_norm_g': None,
              'a_w_uq': 2, 'a_w_ukv': 2, 'a_w_out': 1, 'b_norm_g': 1, 'b_w_in': 2, 'b_conv_w': 2,
              'b_conv_b': 1, 'b_w_rg': 2, 'b_b_rg': 1, 'b_w_ig': 2, 'b_b_ig': 1, 'b_lam': 1,
              'b_w_out': 1, 'final_norm_g': None}


def _params(sem=None):
    return pltpu.CompilerParams(dimension_semantics=sem, vmem_limit_bytes=VMEM_LIMIT_V7X)


def _row_tile(tp):
    return 384 if (tp % 384 == 0 and tp >= 1152) else 128


def _sigmoid(x):
    return 1.0 / (1.0 + jnp.exp(-x))


def _rms(x):
    return lax.rsqrt(jnp.mean(x * x, axis=-1, keepdims=True) + RMS_EPS)


def _swap32(x):
    lane = lax.broadcasted_iota(jnp.int32, x.shape, 1)
    return jnp.where(lane < 32, pltpu.roll(x, 96, 1), pltpu.roll(x, 32, 1))


def _rope(x, cos_t, sin_t):
    return x * cos_t + _swap32(x) * sin_t


def _unrope(d, cos_t, sin_t):
    lane = lax.broadcasted_iota(jnp.int32, d.shape, 1)
    return jnp.where(lane < QK_ROPE, d * cos_t + _swap32(d * sin_t), 0.0)


def _dot(a, b):
    return jnp.dot(a, b, preferred_element_type=F32)


def _dot_nt(a, b):
    return lax.dot_general(a, b, (((1,), (1,)), ((), ())), preferred_element_type=F32)


def _dot_tn(a, b):
    return lax.dot_general(a, b, (((0,), (0,)), ((), ())), preferred_element_type=F32)


def _tokens_in_proj(name, tokens, meta, g, wt, splits, tm, pad, exchange=None):
    seq, kin = tokens.shape
    x0 = pad + meta.shape[0]
    tp = x0 + seq
    n = wt.shape[0]
    nt = tp // tm
    assert x0 <= tm and x0 % SUBLANES == 0 and pad % SUBLANES == 0 and nt >= 2

    def compute(x_hbm, meta_ref, g_ref, w_ref, h_ref, *rest):
        outs, (tile_s, sems) = rest[:len(splits)], rest[len(splits):]
        i = pl.program_id(0)

        def first_rows():
            return pltpu.make_async_copy(x_hbm.at[pl.ds(0, tm - x0), :], tile_s.at[0, pl.ds(x0, tm - x0), :], sems.at[0])

        def rows_of(t):
            return pltpu.make_async_copy(x_hbm.at[pl.ds(pl.multiple_of(t * tm - x0, SUBLANES), tm), :],
                                         tile_s.at[t % 2], sems.at[t % 2])

        @pl.when(i == 0)
        def _():
            tile_s[0, 0:pad, :] = jnp.zeros((pad, kin), F32)
            tile_s[0, pad:x0, :] = meta_ref[...]
            if x0 < tm:
                first_rows().start()
                first_rows().wait()

        pl.when(i > 0)(lambda: rows_of(i).wait())
        pl.when(i + 1 < nt)(lambda: rows_of(i + 1).start())

        xv = tile_s[i % 2]
        h_ref[...] = xv
        nrm = ((xv * _rms(xv)) * g_ref[...]).astype(BF16)
        y = _dot_nt(nrm, w_ref[...])
        for o_ref, (c0, c1) in zip(outs, splits):
            o_ref[...] = y[:, c0:c1]

    in_specs = [pl.BlockSpec(memory_space=pl.ANY),
                pl.BlockSpec(meta.shape, lambda i: (0, 0)),
                pl.BlockSpec((1, kin), lambda i: (0, 0)),
                pl.BlockSpec((n, kin), lambda i: (0, 0))]
    out_specs = [pl.BlockSpec((tm, kin), lambda i: (i, 0))]
    out_specs += [pl.BlockSpec((tm, c1 - c0), lambda i: (i, 0)) for c0, c1 in splits]
    out_shape = [jax.ShapeDtypeStruct((tp, kin), F32)]
    out_shape += [jax.ShapeDtypeStruct((tp, c1 - c0), F32) for c0, c1 in splits]
    args, scratch = [tokens, meta, g, wt], [pltpu.VMEM((2, tm, kin), F32), pltpu.SemaphoreType.DMA((2,))]
    body = _with_exchange(exchange, len(in_specs), len(out_specs),
                          lambda: pl.program_id(0) == 0, lambda: pl.program_id(0) == nt - 1, compute)
    if exchange is not None:
        in_specs = in_specs + exchange.specs(exchange.n_in)
        out_specs = out_specs + exchange.specs(exchange.n_out)
        out_shape = out_shape + exchange.out_shapes
        args, scratch = args + exchange.arrays, scratch + exchange.scratch()
    return pl.pallas_call(
        body, name=name, grid=(nt,),
        in_specs=in_specs, out_specs=out_specs, out_shape=out_shape, scratch_shapes=scratch,
        compiler_params=_params(("arbitrary",)),
    )(*args)


def _q_proj(q_lat, g, w_uq, cos_t, sin_t, heads, tm):
    tp, kin = q_lat.shape
    n = heads * HEAD_W

    def body(x_ref, g_ref, w_ref, cos_ref, sin_ref, q_ref):
        xv = x_ref[...]
        nrm = ((xv * _rms(xv)) * g_ref[...]).astype(BF16)
        y = _dot(nrm, w_ref[...])
        cos_v, sin_v = cos_ref[...], sin_ref[...]
        for h in range(heads):
            c0 = h * HEAD_W
            q_ref[:, c0:c0 + QK_NOPE] = y[:, c0:c0 + QK_NOPE].astype(BF16)
            q_ref[:, c0 + QK_NOPE:c0 + HEAD_W] = _rope(y[:, c0 + QK_NOPE:c0 + HEAD_W], cos_v, sin_v).astype(BF16)

    return pl.pallas_call(
        body, name="a_q_proj", grid=(tp // tm,),
        in_specs=[pl.BlockSpec((tm, kin), lambda i: (i, 0)),
                  pl.BlockSpec((1, kin), lambda i: (0, 0)),
                  pl.BlockSpec((kin, n), lambda i: (0, 0)),
                  pl.BlockSpec((tm, LANES), lambda i: (i, 0)),
                  pl.BlockSpec((tm, LANES), lambda i: (i, 0))],
        out_specs=pl.BlockSpec((tm, n), lambda i: (i, 0)),
        out_shape=jax.ShapeDtypeStruct((tp, n), BF16),
        compiler_params=_params(("parallel",)),
    )(q_lat, g, w_uq, cos_t, sin_t)


def _kv_proj(kv_lat, g, w_ukv, k_rope_raw, cos_t, sin_t, heads, tm):
    tp, kin = kv_lat.shape
    n = heads * (QK_NOPE + V_HEAD)

    def body(x_ref, g_ref, w_ref, kr_ref, cos_ref, sin_ref, k_ref, v_ref):
        xv = x_ref[...]
        nrm = ((xv * _rms(xv)) * g_ref[...]).astype(BF16)
        y = _dot(nrm, w_ref[...])
        kr = _rope(kr_ref[...], cos_ref[...], sin_ref[...]).astype(BF16)
        for h in range(heads):
            c0 = h * (QK_NOPE + V_HEAD)
            k_ref[:, h * HEAD_W:h * HEAD_W + QK_NOPE] = y[:, c0:c0 + QK_NOPE].astype(BF16)
            k_ref[:, h * HEAD_W + QK_NOPE:(h + 1) * HEAD_W] = kr
            v_ref[:, h * V_HEAD:(h + 1) * V_HEAD] = y[:, c0 + QK_NOPE:c0 + QK_NOPE + V_HEAD].astype(BF16)

    return pl.pallas_call(
        body, name="a_kv_proj", grid=(tp // tm,),
        in_specs=[pl.BlockSpec((tm, kin), lambda i: (i, 0)),
                  pl.BlockSpec((1, kin), lambda i: (0, 0)),
                  pl.BlockSpec((kin, n), lambda i: (0, 0)),
                  pl.BlockSpec((tm, LANES), lambda i: (i, 0)),
                  pl.BlockSpec((tm, LANES), lambda i: (i, 0)),
                  pl.BlockSpec((tm, LANES), lambda i: (i, 0))],
        out_specs=[pl.BlockSpec((tm, heads * HEAD_W), lambda i: (i, 0)),
                   pl.BlockSpec((tm, heads * V_HEAD), lambda i: (i, 0))],
        out_shape=[jax.ShapeDtypeStruct((tp, heads * HEAD_W), BF16),
                   jax.ShapeDtypeStruct((tp, heads * V_HEAD), BF16)],
        compiler_params=_params(("parallel",)),
    )(kv_lat, g, w_ukv, k_rope_raw, cos_t, sin_t)


def _as_rows(col):
    rows = col.shape[0]
    return jnp.transpose(jnp.broadcast_to(col, (rows, LANES)))[0:SUBLANES, :]


def _attn_mask(row0, col0, rows, cols, pad):
    row = row0 + lax.broadcasted_iota(jnp.int32, (rows, cols), 0)
    col = col0 + lax.broadcasted_iota(jnp.int32, (rows, cols), 1)
    return (col <= row) & (col >= pad)


LOG2E = 1.4426950408889634
FLASH_FWD_TRIPS = ((4, 2), (2, 2), (1, 1))


def _flash_fwd(q, k, v, heads, pad, tq, exchange=None):
    tp = q.shape[0]
    nq = tp // tq
    c2 = (QK_NOPE + QK_ROPE) ** -0.5 * LOG2E

    def compute(q_ref, k_ref, v_ref, o_ref, lse_ref):
        i = pl.program_id(1)

        def make_step(masked, blocks, parts=1):
            keys = blocks * tq // parts

            def step(j, carry):
                m, l, acc = carry
                offs = [pl.multiple_of(j * tq + part * keys, tq) for part in range(parts)]
                scores = [_dot_nt(q_ref[...], k_ref[pl.ds(off, keys), :]) for off in offs]
                for off, s in zip(offs, scores):
                    s = s * c2
                    if masked:
                        s = jnp.where(_attn_mask(i * tq, j * tq, tq, keys, pad), s, MASK_VALUE)
                    m_new = jnp.maximum(m, jnp.max(s, axis=-1, keepdims=True))
                    p = jnp.exp2(s - m_new)
                    alpha = jnp.exp2(m - m_new)
                    l = alpha * l + jnp.sum(p, axis=-1, keepdims=True)
                    acc = alpha * acc + _dot(p.astype(BF16), v_ref[pl.ds(off, keys), :])
                    m = m_new
                return m, l, acc
            return step

        init = (jnp.full((tq, 1), MASK_VALUE, F32), jnp.zeros((tq, 1), F32), jnp.zeros((tq, V_HEAD), F32))
        carry = make_step(True, 1)(0, init)
        first = 1
        for blocks, parts in FLASH_FWD_TRIPS:
            trips = jnp.maximum(i - first, 0) // blocks
            step_n = make_step(False, blocks, parts)
            carry = lax.fori_loop(0, trips, lambda t, cr, f=first, b=blocks, s=step_n: s(f + b * t, cr), carry)
            first = first + blocks * trips
        m, l, acc = lax.fori_loop(jnp.maximum(i, 1), i + 1, make_step(True, 1), carry)
        o_ref[...] = acc / l
        lse_ref[...] = _as_rows(m + jnp.log(l) * LOG2E)

    in_specs = [pl.BlockSpec((tq, HEAD_W), lambda h, i: (i, h)),
                pl.BlockSpec((tp, HEAD_W), lambda h, i: (0, h)),
                pl.BlockSpec((tp, V_HEAD), lambda h, i: (0, h))]
    out_specs = [pl.BlockSpec((tq, V_HEAD), lambda h, i: (i, h)),
                 pl.BlockSpec((None, None, SUBLANES, tq), lambda h, i: (h, i, 0, 0))]
    out_shape = [jax.ShapeDtypeStruct((tp, heads * V_HEAD), F32),
                 jax.ShapeDtypeStruct((heads, nq, SUBLANES, tq), F32)]
    args, scratch = [q, k, v], []
    body = _with_exchange(exchange, len(in_specs), len(out_specs),
                          lambda: (pl.program_id(0) == 0) & (pl.program_id(1) == 0),
                          lambda: (pl.program_id(0) == heads - 1) & (pl.program_id(1) == nq - 1), compute)
    if exchange is not None:
        in_specs = in_specs + exchange.specs(exchange.n_in)
        out_specs = out_specs + exchange.specs(exchange.n_out)
        out_shape = out_shape + exchange.out_shapes
        args, scratch = args + exchange.arrays, exchange.scratch()
    return pl.pallas_call(
        body, name="a_flash_fwd", grid=(heads, nq),
        in_specs=in_specs, out_specs=out_specs, out_shape=out_shape, scratch_shapes=scratch,
        compiler_params=_params(("arbitrary", "arbitrary")),
    )(*args)


def _out_proj_in_proj(name, a, gate, w_out, resid, g, w_in, splits, tm):
    tp, wd = a.shape
    d = w_out.shape[1]
    n = w_in.shape[1]

    def body(a_ref, gate_ref, wo_ref, res_ref, g_ref, wi_ref, h_ref, *outs):
        gv = gate_ref[...]
        y = (a_ref[...] * (gv * _sigmoid(gv))).astype(BF16)
        h = res_ref[...] + _dot(y, wo_ref[...])
        h_ref[...] = h
        nrm = ((h * _rms(h)) * g_ref[...]).astype(BF16)
        z = _dot(nrm, wi_ref[...])
        for o_ref, (c0, c1) in zip(outs, splits):
            o_ref[...] = z[:, c0:c1]

    return pl.pallas_call(
        body, name=name, grid=(tp // tm,),
        in_specs=[pl.BlockSpec((tm, wd), lambda i: (i, 0)),
                  pl.BlockSpec((tm, wd), lambda i: (i, 0)),
                  pl.BlockSpec((wd, d), lambda i: (0, 0)),
                  pl.BlockSpec((tm, d), lambda i: (i, 0)),
                  pl.BlockSpec((1, d), lambda i: (0, 0)),
                  pl.BlockSpec((d, n), lambda i: (0, 0))],
        out_specs=[pl.BlockSpec((tm, d), lambda i: (i, 0))]
        + [pl.BlockSpec((tm, c1 - c0), lambda i: (i, 0)) for c0, c1 in splits],
        out_shape=[jax.ShapeDtypeStruct((tp, d), F32)]
        + [jax.ShapeDtypeStruct((tp, c1 - c0), F32) for c0, c1 in splits],
        compiler_params=_params(("parallel",)),
    )(a, gate, w_out, resid, g, w_in)


def _lru_decay(r, sp):
    log_a = -LRU_C * r * sp
    a = jnp.exp(log_a)
    e2 = a * a
    x2 = 2.0 * log_a
    series = x2 * (1.0 + x2 * (0.5 + x2 * (1.0 / 6.0)))
    em1 = jnp.where(x2 > -0.02, series, e2 - 1.0)
    return a, e2, jnp.sqrt(-em1)


def _softplus(x):
    return jnp.maximum(x, 0.0) + jnp.log1p(jnp.exp(-jnp.abs(x)))


def _rglru_fwd(u, conv_w, conv_b, w_rg, b_rg, w_ig, b_ig, lam, pad, tm):
    tp, w = u.shape
    groups, blk = w_rg.shape[0], w_rg.shape[1]

    def body(u_ref, cw_ref, cb_ref, wr_ref, br_ref, wi_ref, bi_ref, lam_ref,
             uc_ref, r_ref, ig_ref, hs_ref, a_s, mult_ref, uext, b_s, hc):
        i = pl.program_id(0)

        @pl.when(i == 0)
        def _():
            uext[0:SUBLANES, :] = jnp.zeros((SUBLANES, w), F32)
            hc[...] = jnp.zeros((SUBLANES, w), F32)

        uext[SUBLANES:SUBLANES + tm, :] = u_ref[...]
        cw = cw_ref[...]
        uc = cb_ref[...] + uext[pl.ds(SUBLANES - 3, tm), :] * cw[0:1, :]
        uc = uc + uext[pl.ds(SUBLANES - 2, tm), :] * cw[1:2, :]
        uc = uc + uext[pl.ds(SUBLANES - 1, tm), :] * cw[2:3, :]
        uc = uc + uext[pl.ds(SUBLANES, tm), :] * cw[3:4, :]
        uc_ref[...] = uc
        uext[0:SUBLANES, :] = uext[tm:tm + SUBLANES, :]

        sp = _softplus(-lam_ref[...])
        for g in range(groups):
            sl = slice(g * blk, (g + 1) * blk)
            ucg = uc_ref[:, sl]
            ucb = ucg.astype(BF16)
            r = _sigmoid(_dot(ucb, wr_ref[g]) + br_ref[:, sl])
            ig = _sigmoid(_dot(ucb, wi_ref[g]) + bi_ref[:, sl])
            r_ref[:, sl] = r
            ig_ref[:, sl] = ig
            a, _, mult = _lru_decay(r, sp[:, sl])
            a_s[:, sl] = a
            mult_ref[:, sl] = mult
            b_s[:, sl] = mult * (ig * ucg)

        @pl.when(i == 0)
        def _():
            row = lax.broadcasted_iota(jnp.int32, (Q_BLOCK, w), 0)
            start = ig_ref[0:Q_BLOCK, :] * uc_ref[0:Q_BLOCK, :]
            b_s[0:Q_BLOCK, :] = jnp.where(row < pad, 0.0, jnp.where(row == pad, start, b_s[0:Q_BLOCK, :]))
            mult_ref[0:Q_BLOCK, :] = jnp.where(row == pad, 1.0, mult_ref[0:Q_BLOCK, :])

        row8 = lax.broadcasted_iota(jnp.int32, (SUBLANES, w), 0)

        def group(gi, h_in):
            off = pl.multiple_of(gi * SUBLANES, SUBLANES)
            av = a_s[pl.ds(off, SUBLANES), :]
            bv = b_s[pl.ds(off, SUBLANES), :]
            for k in (1, 2, 4):
                keep = row8 >= k
                bv = jnp.where(keep, av * pltpu.roll(bv, k, 0) + bv, bv)
                av = jnp.where(keep, av * pltpu.roll(av, k, 0), av)
            hv = av * h_in + bv
            hs_ref[pl.ds(off, SUBLANES), :] = hv
            return jnp.broadcast_to(hv[SUBLANES - 1:SUBLANES, :], (SUBLANES, w))

        hc[...] = lax.fori_loop(0, tm // SUBLANES, group, hc[...])

    row_spec = pl.BlockSpec((tm, w), lambda i: (i, 0))
    vec_spec = pl.BlockSpec((1, w), lambda i: (0, 0))
    mat_spec = pl.BlockSpec((groups, blk, blk), lambda i: (0, 0, 0))
    return pl.pallas_call(
        body, name="b_rglru_fwd", grid=(tp // tm,),
        in_specs=[row_spec, pl.BlockSpec((CONV_WIDTH, w), lambda i: (0, 0)), vec_spec,
                  mat_spec, vec_spec, mat_spec, vec_spec, vec_spec],
        out_specs=[row_spec] * 6,
        out_shape=[jax.ShapeDtypeStruct((tp, w), F32)] * 6,
        scratch_shapes=[pltpu.VMEM((tm + SUBLANES, w), F32), pltpu.VMEM((tm, w), F32),
                        pltpu.VMEM((SUBLANES, w), F32)],
        compiler_params=_params(("arbitrary",)),
    )(u, conv_w, conv_b, w_rg, b_rg, w_ig, b_ig, lam)


def _out_proj_loss(a, gate, w, resid, g, target, x0, tm):
    tp, wd = a.shape
    d = w.shape[1]
    assert x0 % Q_BLOCK == 0 and tm % Q_BLOCK == 0 and target.shape[0] == tp - x0
    lead = x0 // Q_BLOCK
    per = tm // Q_BLOCK

    def body(a_ref, gate_ref, w_ref, res_ref, g_ref, *rest):
        t_refs, (dh_ref, loss_ref, dg_ref, da_ref, dgate_ref, dw_ref) = rest[:per], rest[per:]
        i = pl.program_id(0)

        @pl.when(i == 0)
        def _():
            loss_ref[...] = jnp.zeros_like(loss_ref)
            dg_ref[...] = jnp.zeros_like(dg_ref)
            dw_ref[...] = jnp.zeros_like(dw_ref)

        gate_v = gate_ref[...]
        av = a_ref[...]
        sg = _sigmoid(gate_v)
        silu = gate_v * sg
        y = (av * silu).astype(BF16)
        h = res_ref[...] + _dot(y, w_ref[...])
        gv = g_ref[...]
        for b in range(per):
            rows = slice(b * Q_BLOCK, (b + 1) * Q_BLOCK)
            xv = h[rows, :]
            r = _rms(xv)
            xh = xv * r
            err = jnp.where(i * per + b >= lead, xh * gv - t_refs[b][...], 0.0)
            loss_ref[...] += 0.5 * jnp.sum(jnp.mean(err * err, axis=-1, keepdims=True))
            dy = err / d
            dg_ref[...] += jnp.sum(dy * xh, axis=0, keepdims=True)
            dxh = dy * gv
            dh_ref[rows, :] = r * (dxh - xh * jnp.mean(dxh * xh, axis=-1, keepdims=True))

        dob = dh_ref[...].astype(BF16)
        dyv = _dot_nt(dob, w_ref[...])
        da_ref[...] = dyv * silu
        dgate_ref[...] = dyv * av * (sg * (1.0 + gate_v * (1.0 - sg)))
        dw_ref[...] += _dot_tn(y, dob)

    def piece(b):
        return pl.BlockSpec((Q_BLOCK, d), lambda i: (jnp.maximum(i * per + b - lead, 0), 0))

    return pl.pallas_call(
        body, name="b_out_proj_loss", grid=(tp // tm,),
        in_specs=[pl.BlockSpec((tm, wd), lambda i: (i, 0)),
                  pl.BlockSpec((tm, wd), lambda i: (i, 0)),
                  pl.BlockSpec((wd, d), lambda i: (0, 0)),
                  pl.BlockSpec((tm, d), lambda i: (i, 0)),
                  pl.BlockSpec((1, d), lambda i: (0, 0))] + [piece(b) for b in range(per)],
        out_specs=[pl.BlockSpec((tm, d), lambda i: (i, 0)),
                   pl.BlockSpec((SUBLANES, LANES), lambda i: (0, 0)),
                   pl.BlockSpec((1, d), lambda i: (0, 0)),
                   pl.BlockSpec((tm, wd), lambda i: (i, 0)),
                   pl.BlockSpec((tm, wd), lambda i: (i, 0)),
                   pl.BlockSpec((wd, d), lambda i: (0, 0))],
        out_shape=[jax.ShapeDtypeStruct((tp, d), F32),
                   jax.ShapeDtypeStruct((SUBLANES, LANES), F32),
                   jax.ShapeDtypeStruct((1, d), F32),
                   jax.ShapeDtypeStruct((tp, wd), F32),
                   jax.ShapeDtypeStruct((tp, wd), F32),
                   jax.ShapeDtypeStruct((wd, d), F32)],
        compiler_params=_params(("arbitrary",)),
    )(a, gate, w, resid, g, *([target] * per))


def _with_exchange(exchange, n_in, n_out, first, last, compute):
    if exchange is None:
        return compute
    ex_in, ex_out = exchange.n_in, exchange.n_out

    def body(*refs):
        own_in, their_in = refs[:n_in], refs[n_in:n_in + ex_in]
        pos = n_in + ex_in
        own_out, their_out = refs[pos:pos + n_out], refs[pos + n_out:pos + n_out + ex_out]
        rest = refs[pos + n_out + ex_out:]
        own_scratch, sems = rest[:len(rest) - 2], rest[len(rest) - 2:]

        @pl.when(first())
        def _():
            exchange.start(their_in, their_out, sems)

        compute(*own_in, *own_out, *own_scratch)

        @pl.when(last())
        def _():
            exchange.finish(their_in, their_out, sems)

    return body


def _gated_out_bwd(name, dout, a, gate, w, tm, delta_heads=0, exchange=None):
    tp, wd = a.shape
    d = w.shape[1]
    nt = tp // tm

    def compute(do_ref, a_ref, gate_ref, w_ref, da_ref, dgate_ref, dw_ref, *delta_ref):
        @pl.when(pl.program_id(0) == 0)
        def _():
            dw_ref[...] = jnp.zeros_like(dw_ref)

        gv = gate_ref[...]
        av = a_ref[...]
        sg = _sigmoid(gv)
        silu = gv * sg
        dob = do_ref[...].astype(BF16)
        dy = _dot_nt(dob, w_ref[...])
        da = dy * silu
        da_ref[...] = da
        dgate_ref[...] = dy * av * (sg * (1.0 + gv * (1.0 - sg)))
        dw_ref[...] += _dot_tn((av * silu).astype(BF16), dob)
        for h in range(delta_heads):
            sl = slice(h * V_HEAD, (h + 1) * V_HEAD)
            delta_ref[0][h] = _as_rows(jnp.sum(da[:, sl] * av[:, sl], axis=-1, keepdims=True))

    out_specs = [pl.BlockSpec((tm, wd), lambda i: (i, 0)),
                 pl.BlockSpec((tm, wd), lambda i: (i, 0)),
                 pl.BlockSpec((wd, d), lambda i: (0, 0))]
    out_shape = [jax.ShapeDtypeStruct((tp, wd), F32),
                 jax.ShapeDtypeStruct((tp, wd), F32),
                 jax.ShapeDtypeStruct((wd, d), F32)]
    if delta_heads:
        out_specs.append(pl.BlockSpec((delta_heads, None, SUBLANES, tm), lambda i: (0, i, 0, 0)))
        out_shape.append(jax.ShapeDtypeStruct((delta_heads, tp // tm, SUBLANES, tm), F32))
    in_specs = [pl.BlockSpec((tm, d), lambda i: (i, 0)),
                pl.BlockSpec((tm, wd), lambda i: (i, 0)),
                pl.BlockSpec((tm, wd), lambda i: (i, 0)),
                pl.BlockSpec((wd, d), lambda i: (0, 0))]
    args, scratch = [dout, a, gate, w], []
    body = _with_exchange(exchange, len(in_specs), len(out_specs),
                          lambda: pl.program_id(0) == 0, lambda: pl.program_id(0) == nt - 1, compute)
    if exchange is not None:
        in_specs = in_specs + exchange.specs(exchange.n_in)
        out_specs = out_specs + exchange.specs(exchange.n_out)
        out_shape = out_shape + exchange.out_shapes
        args, scratch = args + exchange.arrays, exchange.scratch()
    return pl.pallas_call(
        body, name=name, grid=(nt,),
        in_specs=in_specs, out_specs=out_specs, out_shape=out_shape, scratch_shapes=scratch,
        compiler_params=_params(("arbitrary",)),
    )(*args)


def _rglru_bwd(dhs, hs, r, ig, uc, u, a, mult, conv_w, w_rg, w_ig, lam, pad, tm):
    tp, w = u.shape
    groups, blk = w_rg.shape[0], w_rg.shape[1]
    nt = tp // tm
    per8 = tm // SUBLANES

    def body(dhs_ref, hs_ref, hprev_ref, r_ref, ig_ref, uc_ref, u_ref, uprev_ref, a_ref, mult_ref,
             cw_ref, wr_ref, wi_ref, lam_ref,
             du_ref, dcw_ref, dcb_ref, dwr_ref, dbr_ref, dwi_ref, dbi_ref, dlam_ref,
             aext, c_s, g_s, hext, uext, ducext, gc):
        step = pl.program_id(0)
        ti = nt - 1 - step

        @pl.when(step == 0)
        def _():
            for ref in (dcw_ref, dcb_ref, dwr_ref, dbr_ref, dwi_ref, dbi_ref, dlam_ref):
                ref[...] = jnp.zeros_like(ref)
            aext[tm:tm + SUBLANES, :] = jnp.zeros((SUBLANES, w), F32)
            ducext[tm:tm + SUBLANES, :] = jnp.zeros((SUBLANES, w), F32)
            gc[...] = jnp.zeros((SUBLANES, w), F32)

        lam_v = lam_ref[...]
        sp = _softplus(-lam_v)
        row = ti * tm + lax.broadcasted_iota(jnp.int32, (tm, w), 0)

        rv = r_ref[...]
        a = a_ref[...]
        mult = mult_ref[...]
        aext[0:tm, :] = a
        c_s[...] = aext[pl.ds(1, tm), :]
        row8 = lax.broadcasted_iota(jnp.int32, (SUBLANES, w), 0)

        def group(gi, g_in):
            off = pl.multiple_of((per8 - 1 - gi) * SUBLANES, SUBLANES)
            cv = c_s[pl.ds(off, SUBLANES), :]
            dv = dhs_ref[pl.ds(off, SUBLANES), :]
            for k in (1, 2, 4):
                keep = row8 < SUBLANES - k
                dv = jnp.where(keep, cv * pltpu.roll(dv, SUBLANES - k, 0) + dv, dv)
                cv = jnp.where(keep, cv * pltpu.roll(cv, SUBLANES - k, 0), cv)
            gv = cv * g_in + dv
            g_s[pl.ds(off, SUBLANES), :] = gv
            return jnp.broadcast_to(gv[0:1, :], (SUBLANES, w))

        gc[...] = lax.fori_loop(0, per8, group, gc[...])
        aext[tm:tm + SUBLANES, :] = aext[0:SUBLANES, :]

        gsc = jnp.where(row < pad, 0.0, g_s[...])
        hext[0:SUBLANES, :] = hprev_ref[...]
        hext[SUBLANES:SUBLANES + tm, :] = hs_ref[...]
        hprev = jnp.where(row == 0, 0.0, hext[pl.ds(SUBLANES - 1, tm), :])
        igv = ig_ref[...]
        ucv = uc_ref[...]
        first = row == pad
        dmult = gsc * (igv * ucv)
        dig = gsc * mult * ucv
        duc = gsc * mult * igv
        dlog_a = (gsc * hprev) * a + jnp.where(first, 0.0, dmult * (-(a * a) / mult))
        dlam_ref[...] += jnp.sum(dlog_a * rv, axis=0, keepdims=True) * (LRU_C * _sigmoid(-lam_v))
        dpre_r = dlog_a * (-LRU_C * sp) * (rv * (1.0 - rv))
        dpre_i = dig * (igv * (1.0 - igv))
        dbr_ref[...] += jnp.sum(dpre_r, axis=0, keepdims=True)
        dbi_ref[...] += jnp.sum(dpre_i, axis=0, keepdims=True)
        for g in range(groups):
            sl = slice(g * blk, (g + 1) * blk)
            ucb = ucv[:, sl].astype(BF16)
            drb = dpre_r[:, sl].astype(BF16)
            dib = dpre_i[:, sl].astype(BF16)
            dwr_ref[g] += _dot_tn(ucb, drb)
            dwi_ref[g] += _dot_tn(ucb, dib)
            ducext[0:tm, sl] = duc[:, sl] + _dot_nt(drb, wr_ref[g]) + _dot_nt(dib, wi_ref[g])

        ducv = ducext[0:tm, :]
        cw = cw_ref[...]
        dcb_ref[...] += jnp.sum(ducv, axis=0, keepdims=True)
        uext[0:SUBLANES, :] = jnp.where(ti == 0, 0.0, uprev_ref[...])
        uext[SUBLANES:SUBLANES + tm, :] = u_ref[...]
        for j in range(CONV_WIDTH):
            ush = uext[pl.ds(SUBLANES - (CONV_WIDTH - 1 - j), tm), :]
            dcw_ref[j:j + 1, :] += jnp.sum(ducv * ush, axis=0, keepdims=True)
        du = ducv * cw[3:4, :]
        for k in range(1, CONV_WIDTH):
            du = du + ducext[pl.ds(k, tm), :] * cw[3 - k:4 - k, :]
        du_ref[...] = du
        ducext[tm:tm + SUBLANES, :] = ducext[0:SUBLANES, :]

    rev = lambda s: (nt - 1 - s, 0)
    halo = lambda s: (jnp.maximum((nt - 1 - s) * per8 - 1, 0), 0)
    row_spec = pl.BlockSpec((tm, w), rev)
    halo_spec = pl.BlockSpec((SUBLANES, w), halo)
    vec_spec = pl.BlockSpec((1, w), lambda s: (0, 0))
    mat_spec = pl.BlockSpec((groups, blk, blk), lambda s: (0, 0, 0))
    cw_spec = pl.BlockSpec((CONV_WIDTH, w), lambda s: (0, 0))
    return pl.pallas_call(
        body, name="b_rglru_bwd", grid=(nt,),
        in_specs=[row_spec, row_spec, halo_spec, row_spec, row_spec, row_spec, row_spec, halo_spec, row_spec, row_spec,
                  cw_spec, mat_spec, mat_spec, vec_spec],
        out_specs=[row_spec, cw_spec, vec_spec, mat_spec, vec_spec, mat_spec, vec_spec, vec_spec],
        out_shape=[jax.ShapeDtypeStruct((tp, w), F32), jax.ShapeDtypeStruct((CONV_WIDTH, w), F32),
                   jax.ShapeDtypeStruct((1, w), F32), jax.ShapeDtypeStruct((groups, blk, blk), F32),
                   jax.ShapeDtypeStruct((1, w), F32), jax.ShapeDtypeStruct((groups, blk, blk), F32),
                   jax.ShapeDtypeStruct((1, w), F32), jax.ShapeDtypeStruct((1, w), F32)],
        scratch_shapes=[pltpu.VMEM((tm + SUBLANES, w), F32), pltpu.VMEM((tm, w), F32), pltpu.VMEM((tm, w), F32),
                        pltpu.VMEM((tm + SUBLANES, w), F32), pltpu.VMEM((tm + SUBLANES, w), F32),
                        pltpu.VMEM((tm + SUBLANES, w), F32), pltpu.VMEM((SUBLANES, w), F32)],
        compiler_params=_params(("arbitrary",)),
    )(dhs, hs, hs, r, ig, uc, u, u, a, mult, conv_w, w_rg, w_ig, lam)


def _norm_matmul_bwd(name, x, g, w, dys, tm, resid=None, prologue=None, extra_out=None, exchange=None,
                     transposed=False):
    tp, kin = x.shape
    n = w.shape[0] if transposed else w.shape[1]
    w_shape = (n, kin) if transposed else (kin, n)
    nt = tp // tm
    n_dy = len(dys)
    has_res = resid is not None
    has_extra = extra_out is not None

    def compute(*refs):
        x_ref, g_ref, w_ref = refs[:3]
        dy_refs = refs[3:3 + n_dy]
        pos = 3 + n_dy
        res_ref = refs[pos] if has_res else None
        pos += int(has_res)
        dx_ref, dw_ref, dg_ref = refs[pos:pos + 3]
        pos += 3
        ex_ref = refs[pos] if has_extra else None
        pos += int(has_extra)
        dy_s = refs[pos]

        @pl.when(pl.program_id(0) == 0)
        def _():
            dw_ref[...] = jnp.zeros_like(dw_ref)
            dg_ref[...] = jnp.zeros_like(dg_ref)

        if prologue is None:
            c0 = 0
            for ref in dy_refs:
                dy_s[:, c0:c0 + ref.shape[1]] = ref[...].astype(BF16)
                c0 += ref.shape[1]
        else:
            prologue(dy_refs, dy_s, ex_ref)

        xv = x_ref[...]
        gv = g_ref[...]
        r = _rms(xv)
        xh = xv * r
        dyb = dy_s[...]
        nb = (xh * gv).astype(BF16)
        if transposed:
            dn = _dot(dyb, w_ref[...])
            dw_ref[...] += _dot_tn(dyb, nb)
        else:
            dn = _dot_nt(dyb, w_ref[...])
            dw_ref[...] += _dot_tn(nb, dyb)
        dg_ref[...] += jnp.sum(dn * xh, axis=0, keepdims=True)
        dxh = dn * gv
        dx = r * (dxh - xh * jnp.mean(dxh * xh, axis=-1, keepdims=True))
        if has_res:
            dx = dx + res_ref[...]
        dx_ref[...] = dx

    row = lambda width: pl.BlockSpec((tm, width), lambda i: (i, 0))
    in_specs = [row(kin), pl.BlockSpec((1, kin), lambda i: (0, 0)), pl.BlockSpec(w_shape, lambda i: (0, 0))]
    in_specs += [row(a.shape[1]) for a in dys]
    args = [x, g, w, *dys]
    if has_res:
        in_specs.append(row(kin))
        args.append(resid)
    out_specs = [row(kin), pl.BlockSpec(w_shape, lambda i: (0, 0)), pl.BlockSpec((1, kin), lambda i: (0, 0))]
    out_shape = [jax.ShapeDtypeStruct((tp, kin), F32), jax.ShapeDtypeStruct(w_shape, F32),
                 jax.ShapeDtypeStruct((1, kin), F32)]
    if has_extra:
        out_specs.append(row(extra_out[0]))
        out_shape.append(jax.ShapeDtypeStruct((tp, extra_out[0]), extra_out[1]))
    scratch = [pltpu.VMEM((tm, n), BF16)]
    body = _with_exchange(exchange, len(in_specs), len(out_specs),
                          lambda: pl.program_id(0) == 0, lambda: pl.program_id(0) == nt - 1, compute)
    if exchange is not None:
        in_specs = in_specs + exchange.specs(exchange.n_in)
        out_specs = out_specs + exchange.specs(exchange.n_out)
        out_shape = out_shape + exchange.out_shapes
        args, scratch = args + exchange.arrays, scratch + exchange.scratch()
    return pl.pallas_call(
        body, name=name, grid=(nt,),
        in_specs=in_specs, out_specs=out_specs, out_shape=out_shape, scratch_shapes=scratch,
        compiler_params=_params(("arbitrary",)),
    )(*args)


def _flash_bwd(q, k, v, lse, delta, do, heads, pad, tq, exchange=None):
    tp = q.shape[0]
    nq = tp // tq
    scale = (QK_NOPE + QK_ROPE) ** -0.5
    c2 = scale * LOG2E

    def compute(q_ref, k_ref, v_ref, lse_ref, delta_ref, do_ref, dq_ref, dk_ref, dv_ref):
        j = pl.program_id(1)

        @pl.when(j == 0)
        def _():
            dq_ref[...] = jnp.zeros_like(dq_ref)

        kv = k_ref[...]
        vv = v_ref[...]

        def rows_of(ref, i, blocks):
            parts = [ref[i + b][0:1, :] for b in range(blocks)]
            return parts[0] if blocks == 1 else jnp.concatenate(parts, axis=1)

        def make_step(masked, blocks):
            def step(i, carry):
                off = pl.multiple_of(i * tq, tq)
                qv = q_ref[pl.ds(off, blocks * tq), :]
                dob = do_ref[pl.ds(off, blocks * tq), :].astype(BF16)
                p = jnp.exp2(_dot_nt(kv, qv) * c2 - rows_of(lse_ref, i, blocks))
                if masked:
                    key = j * tq + lax.broadcasted_iota(jnp.int32, (tq, tq), 0)
                    qry = j * tq + lax.broadcasted_iota(jnp.int32, (tq, tq), 1)
                    first = jnp.where((key <= qry) & (key >= pad), p[:, :tq], 0.0)
                    p = first if blocks == 1 else jnp.concatenate([first, p[:, tq:]], axis=1)
                dv_ref[...] += _dot(p.astype(BF16), dob)
                dp = _dot_nt(vv, dob)
                ds = (p * (dp - rows_of(delta_ref, i, blocks)) * scale).astype(BF16)
                dk_ref[...] += _dot(ds, qv)
                dq_ref[pl.ds(off, blocks * tq), :] += _dot_tn(ds, kv)
                return carry
            return step

        dk_ref[...] = jnp.zeros_like(dk_ref)
        dv_ref[...] = jnp.zeros_like(dv_ref)
        odd = (nq - j) % 2
        lax.fori_loop(0, odd, lambda t, cr: make_step(True, 1)(j, cr), 0)
        lax.fori_loop(0, 1 - odd, lambda t, cr: make_step(True, 2)(j, cr), 0)
        start = j + 2 - odd
        for blocks in (4, 2):
            trips = (nq - start) // blocks
            step_n = make_step(False, blocks)
            lax.fori_loop(0, trips, lambda t, cr, s=start, b=blocks, f=step_n: f(s + b * t, cr), 0)
            start = start + blocks * trips

    in_specs = [pl.BlockSpec((tp, HEAD_W), lambda h, j: (0, h)),
                pl.BlockSpec((tq, HEAD_W), lambda h, j: (j, h)),
                pl.BlockSpec((tq, V_HEAD), lambda h, j: (j, h)),
                pl.BlockSpec((None, nq, SUBLANES, tq), lambda h, j: (h, 0, 0, 0)),
                pl.BlockSpec((None, nq, SUBLANES, tq), lambda h, j: (h, 0, 0, 0)),
                pl.BlockSpec((tp, V_HEAD), lambda h, j: (0, h))]
    out_specs = [pl.BlockSpec((tp, HEAD_W), lambda h, j: (0, h)),
                 pl.BlockSpec((tq, HEAD_W), lambda h, j: (j, h)),
                 pl.BlockSpec((tq, V_HEAD), lambda h, j: (j, h))]
    out_shape = [jax.ShapeDtypeStruct((tp, heads * HEAD_W), F32),
                 jax.ShapeDtypeStruct((tp, heads * HEAD_W), F32),
                 jax.ShapeDtypeStruct((tp, heads * V_HEAD), F32)]
    args, scratch = [q, k, v, lse, delta, do], []
    body = _with_exchange(exchange, len(in_specs), len(out_specs),
                          lambda: (pl.program_id(0) == 0) & (pl.program_id(1) == 0),
                          lambda: (pl.program_id(0) == heads - 1) & (pl.program_id(1) == nq - 1), compute)
    if exchange is not None:
        in_specs = in_specs + exchange.specs(exchange.n_in)
        out_specs = out_specs + exchange.specs(exchange.n_out)
        out_shape = out_shape + exchange.out_shapes
        args, scratch = args + exchange.arrays, exchange.scratch()
    return pl.pallas_call(
        body, name="a_flash_bwd", grid=(heads, nq),
        in_specs=in_specs, out_specs=out_specs, out_shape=out_shape, scratch_shapes=scratch,
        compiler_params=_params(("arbitrary", "arbitrary")),
    )(*args)


def _position():
    return lax.axis_index("x"), lax.axis_index("y"), lax.axis_index("c")


def _other_chips(x, y):
    return [(1 - x, y), (x, 1 - y), (1 - x, 1 - y)]


def _block(ref, shard_axis, n, k, split_axis=None, m=None, h=None):
    idx = []
    for a in range(len(ref.shape)):
        start = 0
        size = None
        if a == shard_axis:
            start, size = k * n, n
        if a == split_axis:
            size = (n if a == shard_axis else m) // 2
            start = start + h * size
        idx.append(slice(None) if size is None else pl.ds(start, size))
    return ref.at[tuple(idx)]


def _gather_weights(split, whole_small):
    ns, nw = len(split), len(whole_small)
    n = ns + nw
    arrs = [s[0] for s in split] + [s[0] for s in whole_small]
    axes = [s[1] for s in split] + [s[1] for s in whole_small]

    def body(*refs):
        ins, outs = refs[:n], refs[n:2 * n]
        ici_send, ici_recv, d2d_send, d2d_recv, sib_send, sib_recv = refs[2 * n:]
        x, y, c = _position()
        me = 2 * x + y
        others = _other_chips(x, y)
        sent, local = [], []

        def remote(src, dst, sems, idx, to):
            return pltpu.make_async_remote_copy(src_ref=src, dst_ref=dst, send_sem=sems[0].at[idx],
                                                recv_sem=sems[1].at[idx], device_id=to, device_id_type=MESH)

        for a in range(n):
            width = ins[a].shape[axes[a]]
            mine = remote(ins[a], _block(outs[a], axes[a], width, me), (sib_send, sib_recv), a, (x, y, 1 - c))
            mine.start()
            local.append(mine)
            for j, (px, py) in enumerate(others):
                if a < ns:
                    sx = split[a][2]
                    src = _block(ins[a], None, None, None, sx, ins[a].shape[sx], c)
                    dst = _block(outs[a], axes[a], width, me, sx, outs[a].shape[sx], c)
                else:
                    src, dst = ins[a], _block(outs[a], axes[a], width, me)
                cp = remote(src, dst, (ici_send, ici_recv), 3 * a + j, (px, py, c))
                cp.start()
                sent.append(cp)
        for a in range(ns):
            width = ins[a].shape[axes[a]]
            sx = split[a][2]
            for j, (px, py) in enumerate(others):
                theirs = _block(outs[a], axes[a], width, 2 * px + py, sx, outs[a].shape[sx], c)
                remote(theirs, theirs, (ici_send, ici_recv), 3 * a + j, (px, py, c)).wait_recv()
                fwd = remote(theirs, theirs, (d2d_send, d2d_recv), 3 * a + j, (x, y, 1 - c))
                fwd.start()
                sent.append(fwd)
        for a in range(ns, n):
            width = ins[a].shape[axes[a]]
            for j, (px, py) in enumerate(others):
                theirs = _block(outs[a], axes[a], width, 2 * px + py)
                remote(theirs, theirs, (ici_send, ici_recv), 3 * a + j, (px, py, c)).wait_recv()
        for a in range(ns):
            width = ins[a].shape[axes[a]]
            sx = split[a][2]
            for j, (px, py) in enumerate(others):
                from_sibling = _block(outs[a], axes[a], width, 2 * px + py, sx, outs[a].shape[sx], 1 - c)
                remote(from_sibling, from_sibling, (d2d_send, d2d_recv), 3 * a + j, (x, y, 1 - c)).wait_recv()
        for cp in sent:
            cp.wait_send()
        for cp in local:
            cp.wait()

    def whole_shape(a, axis):
        shape = list(a.shape)
        shape[axis] *= N_CHIPS
        return jax.ShapeDtypeStruct(tuple(shape), a.dtype)

    any_spec = pl.BlockSpec(memory_space=pl.ANY)
    return pl.pallas_call(
        body, name="gather_weights",
        in_specs=[any_spec] * n, out_specs=[any_spec] * n,
        out_shape=[whole_shape(a, ax) for a, ax in zip(arrs, axes)],
        scratch_shapes=[pltpu.SemaphoreType.DMA((3 * n,)), pltpu.SemaphoreType.DMA((3 * n,)),
                        pltpu.SemaphoreType.DMA((3 * ns,)), pltpu.SemaphoreType.DMA((3 * ns,)),
                        pltpu.SemaphoreType.DMA((n,)), pltpu.SemaphoreType.DMA((n,))],
        compiler_params=pltpu.CompilerParams(has_side_effects=True),
    )(*arrs)


class _Grad:
    def __init__(self, name, g, kind, rh, cols, groups=None):
        self.name, self.g, self.kind, self.rh, self.cols, self.groups = name, g, kind, rh, cols, groups
        if kind == 'rows':
            self.tr = rh
        elif kind == 'gate':
            self.tr = rh // (groups // 2)
        else:
            self.tr = rh if rh <= 512 else 256
        self.nb = rh // self.tr

    def pieces(self, ref, k, h):
        rh, cols = self.rh, self.cols
        if self.kind == 'cols':
            return [(ref.at[pl.ds(h * rh, rh), pl.ds(k * cols, cols)], 0, rh)]
        if self.kind == 'rows':
            return [(ref.at[pl.ds((2 * k + h) * rh, rh), :], 0, rh)]
        if self.kind == 'lead':
            return [(ref.at[k, pl.ds(h * rh, rh), :], 0, rh)]
        per = self.groups // 2
        return [(ref.at[pl.ds((((h * per + gi) * N_CHIPS) + k) * self.tr, self.tr), :], gi * self.tr, self.tr)
                for gi in range(per)]

    def block_spec(self):
        tr, nb, cols = self.tr, self.nb, self.cols
        if self.kind == 'cols':
            return pl.BlockSpec((tr, cols), lambda k, i, c: (c[0] * nb + i, k))
        if self.kind == 'rows':
            return pl.BlockSpec((tr, cols), lambda k, i, c: (2 * k + c[0], 0))
        if self.kind == 'lead':
            return pl.BlockSpec((None, tr, cols), lambda k, i, c: (k, c[0] * nb + i, 0))
        return pl.BlockSpec((tr, cols), lambda k, i, c: ((c[0] * nb + i) * N_CHIPS + k, 0))


class _Exchange:
    def __init__(self, name, arrays, out_shapes, n_copies, copies):
        self.name, self.arrays, self.out_shapes, self.n_copies, self.copies = name, arrays, out_shapes, n_copies, copies
        self.n_in, self.n_out = len(arrays), len(out_shapes)

    def specs(self, n):
        return [pl.BlockSpec(memory_space=pl.ANY)] * n

    def scratch(self):
        return [pltpu.SemaphoreType.DMA((self.n_copies,)), pltpu.SemaphoreType.DMA((self.n_copies,))]

    def _descriptors(self, in_refs, out_refs, sems):
        return self.copies(in_refs, out_refs, sems[0], sems[1])

    def start(self, in_refs, out_refs, sems):
        for cp in self._descriptors(in_refs, out_refs, sems):
            cp.start()

    def finish(self, in_refs, out_refs, sems):
        for cp in self._descriptors(in_refs, out_refs, sems):
            cp.wait()

    def __add__(self, other):
        def copies(ins, outs, send_sems, recv_sems, base=0):
            return (self.copies(ins[:self.n_in], outs[:self.n_out], send_sems, recv_sems, base)
                    + other.copies(ins[self.n_in:], outs[self.n_out:], send_sems, recv_sems, base + self.n_copies))

        return _Exchange(self.name + "_" + other.name, self.arrays + other.arrays, self.out_shapes + other.out_shapes,
                         self.n_copies + other.n_copies, copies)

    def run(self):
        def body(*refs):
            ins, outs, sems = refs[:self.n_in], refs[self.n_in:self.n_in + self.n_out], refs[self.n_in + self.n_out:]
            self.start(ins, outs, sems)
            self.finish(ins, outs, sems)

        return pl.pallas_call(
            body, name=self.name,
            in_specs=self.specs(self.n_in), out_specs=self.specs(self.n_out), out_shape=self.out_shapes,
            scratch_shapes=self.scratch(),
            compiler_params=pltpu.CompilerParams(has_side_effects=True),
        )(*self.arrays)


def _gather_whole(name, shards):
    def copies(ins, outs, send_sems, recv_sems, base=0):
        x, y, c = _position()
        me = 2 * x + y
        made = []
        for a, (_, axis) in enumerate(shards):
            dst = _block(outs[a], axis, ins[a].shape[axis], me)
            for j, to in enumerate([(x, y, 1 - c)] + [(px, py, c) for px, py in _other_chips(x, y)]):
                idx = base + 4 * a + j
                made.append(pltpu.make_async_remote_copy(
                    src_ref=ins[a], dst_ref=dst, send_sem=send_sems.at[idx], recv_sem=recv_sems.at[idx],
                    device_id=to, device_id_type=MESH))
        return made

    def whole_shape(a, axis):
        shape = list(a.shape)
        shape[axis] *= N_CHIPS
        return jax.ShapeDtypeStruct(tuple(shape), a.dtype)

    return _Exchange(name, [s[0] for s in shards], [whole_shape(*s) for s in shards], 4 * len(shards), copies)


def _halves_to_sibling(name, grads):
    total = sum(len(gr.pieces(gr.g, 0, 0)) * N_CHIPS for gr in grads)

    def copies(ins, outs, send_sems, recv_sems, base=0):
        x, y, c = _position()
        made = []
        for gr, g_ref, got_ref in zip(grads, ins, outs):
            for k in range(N_CHIPS):
                for src, r0, nr in gr.pieces(g_ref, k, 1 - c):
                    idx = base + len(made)
                    made.append(pltpu.make_async_remote_copy(
                        src_ref=src, dst_ref=got_ref.at[k, pl.ds(r0, nr), :],
                        send_sem=send_sems.at[idx], recv_sem=recv_sems.at[idx],
                        device_id=(x, y, 1 - c), device_id_type=MESH))
        return made

    return _Exchange(name, [gr.g for gr in grads],
                     [jax.ShapeDtypeStruct((N_CHIPS, gr.rh, gr.cols), F32) for gr in grads], total, copies)


def _chip_sum(gr, got, c, wire=BF16):
    def body(c_ref, g_ref, got_ref, o_ref):
        o_ref[...] = (g_ref[...] + got_ref[...]).astype(wire)

    tile = pl.BlockSpec((None, gr.tr, gr.cols), lambda k, i, c_ref: (k, i, 0))
    return pl.pallas_call(
        body, name="chip_sum_" + gr.name,
        grid_spec=pltpu.PrefetchScalarGridSpec(
            num_scalar_prefetch=1, grid=(N_CHIPS, gr.nb),
            in_specs=[gr.block_spec(), tile], out_specs=tile),
        out_shape=jax.ShapeDtypeStruct((N_CHIPS, gr.rh, gr.cols), wire),
        compiler_params=_params(("parallel", "parallel")),
    )(c, gr.g, got)


def _blocks_to_chips(name, parts):
    n = len(parts)

    def copies(ins, outs, send_sems, recv_sems, base=0):
        x, y, c = _position()
        made = []
        for a in range(n):
            for j, (px, py) in enumerate(_other_chips(x, y)):
                idx = base + 3 * a + j
                made.append(pltpu.make_async_remote_copy(
                    src_ref=ins[a].at[2 * px + py], dst_ref=outs[a].at[j],
                    send_sem=send_sems.at[idx], recv_sem=recv_sems.at[idx],
                    device_id=(px, py, c), device_id_type=MESH))
        return made

    return _Exchange(name, parts, [jax.ShapeDtypeStruct((3,) + p.shape[1:], p.dtype) for p in parts], 3 * n, copies)


def _sum_chips(name, part, got, me, c):
    nk, rh, cols = part.shape
    tr = rh if rh <= 512 else 256
    nb = rh // tr

    def body(me_ref, c_ref, own_ref, *rest):
        got_refs, o_ref = rest[:nk], rest[nk]
        own = own_ref[...].astype(F32)
        acc = None
        for k in range(nk):
            term = jnp.where(me_ref[0] == k, own, got_refs[k][...].astype(F32))
            acc = term if acc is None else acc + term
        o_ref[...] = acc

    def got_map(k):
        def index(i, me_ref, c_ref):
            xor = jnp.bitwise_xor(me_ref[0], k)
            slot = jnp.where(xor == 1, 1, jnp.where(xor == 3, 2, 0))
            return (slot, i, 0)
        return index

    return pl.pallas_call(
        body, name="sum_" + name,
        grid_spec=pltpu.PrefetchScalarGridSpec(
            num_scalar_prefetch=2, grid=(nb,),
            in_specs=[pl.BlockSpec((None, tr, cols), lambda i, me_ref, c_ref: (me_ref[0], i, 0))]
            + [pl.BlockSpec((None, tr, cols), got_map(k)) for k in range(nk)],
            out_specs=pl.BlockSpec((tr, cols), lambda i, me_ref, c_ref: (c_ref[0] * nb + i, 0))),
        out_shape=jax.ShapeDtypeStruct((2 * rh, cols), F32),
        compiler_params=_params(("parallel",)),
    )(me, c, part, *([got] * nk))


def _share_with_sibling(halves):
    n = len(halves)

    def body(*refs):
        outs = refs[n:2 * n]
        send_sems, recv_sems = refs[2 * n:]
        x, y, c = _position()
        copies = []
        for a in range(n):
            rh = outs[a].shape[0] // 2
            mine = outs[a].at[pl.ds(c * rh, rh), :]
            cp = pltpu.make_async_remote_copy(
                src_ref=mine, dst_ref=mine, send_sem=send_sems.at[a], recv_sem=recv_sems.at[a],
                device_id=(x, y, 1 - c), device_id_type=MESH)
            cp.start()
            copies.append(cp)
        for cp in copies:
            cp.wait()

    any_spec = pl.BlockSpec(memory_space=pl.ANY)
    return pl.pallas_call(
        body, name="grads_share",
        in_specs=[any_spec] * n, out_specs=[any_spec] * n,
        out_shape=[jax.ShapeDtypeStruct(h.shape, h.dtype) for h in halves],
        input_output_aliases={a: a for a in range(n)},
        scratch_shapes=[pltpu.SemaphoreType.DMA((n,)), pltpu.SemaphoreType.DMA((n,))],
        compiler_params=pltpu.CompilerParams(has_side_effects=True),
    )(*halves)


def _adamw(name, w, g, m, v):
    rows, cols = w.shape
    tr = 256 if rows % 256 == 0 else rows

    def body(w_ref, g_ref, m_ref, v_ref, g_out_ref, d_ref, nm_ref, nv_ref):
        gv = g_ref[...]
        g_out_ref[...] = gv
        mn = ADAM_B1 * m_ref[...] + (1.0 - ADAM_B1) * gv
        vn = ADAM_B2 * v_ref[...] + (1.0 - ADAM_B2) * (gv * gv)
        m_hat = mn / (1.0 - ADAM_B1 ** ADAM_STEP)
        v_hat = vn / (1.0 - ADAM_B2 ** ADAM_STEP)
        d_ref[...] = -ADAM_LR * (m_hat / (jnp.sqrt(v_hat) + ADAM_EPS) + ADAM_WD * w_ref[...])
        nm_ref[...] = mn
        nv_ref[...] = vn

    spec = pl.BlockSpec((tr, cols), lambda i: (i, 0))
    return pl.pallas_call(
        body, name=name, grid=(rows // tr,),
        in_specs=[spec] * 4, out_specs=[spec] * 4,
        out_shape=[jax.ShapeDtypeStruct((rows, cols), F32)] * 4,
        compiler_params=_params(("parallel",)),
    )(w, g, m, v)


def _as2d(a):
    if a.ndim == 1:
        return a.reshape(1, -1)
    return a.reshape(-1, a.shape[-1])


def _rope_tables(tp, pad):
    pos = jnp.arange(tp, dtype=F32) - pad
    inv_freq = ROPE_BASE ** (-jnp.arange(0, QK_ROPE, 2, dtype=F32) / QK_ROPE)
    ang = pos[:, None] * inv_freq[None, :]
    cos, sin = jnp.cos(ang), jnp.sin(ang)
    zeros = jnp.zeros((tp, LANES - QK_ROPE), F32)
    return jnp.concatenate([cos, cos, zeros], axis=1), jnp.concatenate([-sin, sin, zeros], axis=1)


def _matrix_grad(name, g, heads):
    if name in ('b_w_rg', 'b_w_ig'):
        groups, blk, cols = g.shape
        return _Grad(name, g.reshape(groups * blk, cols), 'gate', (groups // 2) * (blk // N_CHIPS), cols, groups)
    rows, cols = g.shape
    if name in ('a_w_in', 'a_w_out', 'b_w_out'):
        return _Grad(name, g, 'rows', rows // (2 * N_CHIPS), cols)
    if name == 'a_w_uq' and heads % N_CHIPS != 0:
        g = g.reshape(rows, heads, HEAD_W)[:, :, :QK_NOPE + QK_ROPE].reshape(rows, -1)
        cols = g.shape[1]
        g = jnp.moveaxis(g.reshape(rows, N_CHIPS, cols // N_CHIPS), 1, 0)
        return _Grad(name, g, 'lead', rows // 2, cols // N_CHIPS)
    return _Grad(name, g, 'cols', rows // 2, cols // N_CHIPS)


def _kernel_form(name, w):
    return w[0] if name in ('b_w_rg', 'b_w_ig', 'b_conv_w') else _as2d(w)


def _local_grads(x, target, wt, heads, c_idx, mid_names, mid_gather, late_names, late_gather):
    wt = dict(wt)
    seq, d = x.shape
    n_meta = wt['meta_tokens'].shape[0]
    t = seq + n_meta
    pad = (-t) % Q_BLOCK
    tp = t + pad
    x0 = pad + n_meta
    tm = _row_tile(tp)
    ql = wt['a_q_norm_g'].shape[1]
    kvl = wt['a_kv_norm_g'].shape[1]
    mla_w = heads * V_HEAD

    cos_t, sin_t = _rope_tables(tp, pad)

    w_in_a = wt['a_w_in']
    zrow = jnp.zeros((LANES - QK_ROPE, d), BF16)
    w_in_a = jnp.concatenate([w_in_a[:ql + kvl + QK_ROPE], zrow, w_in_a[ql + kvl + QK_ROPE:]], axis=0)
    c_kv, c_kr, c_gate = ql, ql + kvl, ql + kvl + LANES
    splits_a = [(0, c_kv), (c_kv, c_kr), (c_kr, c_gate), (c_gate, c_gate + mla_w)]

    h0, q_lat, kv_lat, kr_raw, gate_a, *mid_whole = _tokens_in_proj(
        "a_in_proj", x, wt['meta_tokens'], wt['a_norm_g'], w_in_a, splits_a, tm, pad, exchange=mid_gather)
    wt.update({n: _kernel_form(n, w) for n, w in zip(mid_names, mid_whole)})
    w_uq = wt['a_w_uq'].reshape(ql, heads, QK_NOPE + QK_ROPE)
    w_uq = jnp.pad(w_uq, ((0, 0), (0, 0), (0, HEAD_W - QK_NOPE - QK_ROPE))).reshape(ql, heads * HEAD_W)
    w_ukv = wt['a_w_ukv']
    q = _q_proj(q_lat, wt['a_q_norm_g'], w_uq, cos_t, sin_t, heads, tm)
    k, v = _kv_proj(kv_lat, wt['a_kv_norm_g'], w_ukv, kr_raw, cos_t, sin_t, heads, tm)
    attn, lse, *late_whole = _flash_fwd(q, k, v, heads, pad, tm, exchange=late_gather)
    wt.update({n: _kernel_form(n, w) for n, w in zip(late_names, late_whole)})
    lru_w = wt['b_conv_w'].shape[1]

    h1, u, gate_b = _out_proj_in_proj("a_out_b_in_proj", attn, gate_a, wt['a_w_out'], h0, wt['b_norm_g'], wt['b_w_in'],
                                      [(0, lru_w), (lru_w, 2 * lru_w)], tm)
    uc, r, ig, hs, decay, mult = _rglru_fwd(u, wt['b_conv_w'], wt['b_conv_b'], wt['b_w_rg'], wt['b_b_rg'],
                                            wt['b_w_ig'], wt['b_b_ig'], wt['b_lam'], pad, tm)

    dh2, loss, d_final_g, dhs, dgate_b, dw_out_b = _out_proj_loss(
        hs, gate_b, wt['b_w_out'], h1, wt['final_norm_g'], target, x0, tm)
    du, dconv_w, dconv_b, dw_rg, db_rg, dw_ig, db_ig, dlam = _rglru_bwd(
        dhs, hs, r, ig, uc, u, decay, mult, wt['b_conv_w'], wt['b_w_rg'], wt['b_w_ig'], wt['b_lam'], pad, tm)
    dh1, dw_in_b, dg_b = _norm_matmul_bwd("b_in_proj_bwd", h1, wt['b_norm_g'], wt['b_w_in'], [du, dgate_b], tm, resid=dh2)

    grads_b = [_matrix_grad(n, g, heads) for n, g in
               (('b_w_in', dw_in_b), ('b_w_rg', dw_rg), ('b_w_ig', dw_ig), ('b_w_out', dw_out_b))]
    dattn, dgate_a, dw_out_a, delta, *got = _gated_out_bwd(
        "a_out_proj_bwd", dh1, attn, gate_a, wt['a_w_out'], tm, delta_heads=heads,
        exchange=_halves_to_sibling("swap_b", grads_b))
    sums_b = [_chip_sum(gr, r, c_idx) for gr, r in zip(grads_b, got)]
    grad_out = _matrix_grad('a_w_out', dw_out_a, heads)
    dq, dk, dv, *landed = _flash_bwd(
        q, k, v, lse, delta, dattn, heads, pad, tm,
        exchange=_blocks_to_chips("chips_b", sums_b) + _halves_to_sibling("swap_out", [grad_out]))
    through = list(zip(grads_b, sums_b, landed[:len(grads_b)]))
    sum_out = _chip_sum(grad_out, landed[len(grads_b)], c_idx)

    def q_prologue(dy_refs, dy_s, ex_ref):
        (dq_ref,), cos_v, sin_v = dy_refs[:1], dy_refs[1][...], dy_refs[2][...]
        for h in range(heads):
            c0 = h * HEAD_W
            dy_s[:, c0:c0 + QK_NOPE] = dq_ref[:, c0:c0 + QK_NOPE].astype(BF16)
            dy_s[:, c0 + QK_NOPE:c0 + HEAD_W] = _unrope(dq_ref[:, c0 + QK_NOPE:c0 + HEAD_W], cos_v, sin_v).astype(BF16)

    dq_lat, dw_uq, dg_q, from_chips_out = _norm_matmul_bwd(
        "a_q_proj_bwd", q_lat, wt['a_q_norm_g'], w_uq, [dq, cos_t, sin_t], tm, prologue=q_prologue,
        exchange=_blocks_to_chips("chips_out", [sum_out]))
    through.append((grad_out, sum_out, from_chips_out))
    grad_uq = _matrix_grad('a_w_uq', dw_uq, heads)

    def kv_prologue(dy_refs, dy_s, ex_ref):
        dk_ref, dv_ref = dy_refs[:2]
        cos_v, sin_v = dy_refs[2][...], dy_refs[3][...]
        dkr = jnp.zeros((dk_ref.shape[0], LANES), F32)
        for h in range(heads):
            c0 = h * (QK_NOPE + V_HEAD)
            dy_s[:, c0:c0 + QK_NOPE] = dk_ref[:, h * HEAD_W:h * HEAD_W + QK_NOPE].astype(BF16)
            dy_s[:, c0 + QK_NOPE:c0 + QK_NOPE + V_HEAD] = dv_ref[:, h * V_HEAD:(h + 1) * V_HEAD].astype(BF16)
            dkr = dkr + dk_ref[:, h * HEAD_W + QK_NOPE:(h + 1) * HEAD_W]
        ex_ref[...] = _unrope(dkr, cos_v, sin_v)

    dkv_lat, dw_ukv, dg_kv, dkr_raw, got_uq = _norm_matmul_bwd(
        "a_kv_proj_bwd", kv_lat, wt['a_kv_norm_g'], w_ukv, [dk, dv, cos_t, sin_t], tm,
        prologue=kv_prologue, extra_out=(LANES, F32), exchange=_halves_to_sibling("swap_uq", [grad_uq]))
    sum_uq = _chip_sum(grad_uq, got_uq, c_idx)
    grad_ukv = _matrix_grad('a_w_ukv', dw_ukv, heads)

    dh0, dw_in_a, dg_a, from_chips_uq, got_ukv = _norm_matmul_bwd(
        "a_in_proj_bwd", h0, wt['a_norm_g'], w_in_a, [dq_lat, dkv_lat, dkr_raw, dgate_a], tm, resid=dh1, transposed=True,
        exchange=_blocks_to_chips("chips_uq", [sum_uq]) + _halves_to_sibling("swap_ukv", [grad_ukv]))
    through.append((grad_uq, sum_uq, from_chips_uq))
    swapped = [(grad_ukv, _chip_sum(grad_ukv, got_ukv, c_idx))]

    dw_in_a = jnp.concatenate([dw_in_a[:c_kr + QK_ROPE], dw_in_a[c_gate:]], axis=0)
    grads = {
        'meta_tokens': dh0[pad:x0], 'a_norm_g': dg_a, 'a_w_in': dw_in_a, 'a_q_norm_g': dg_q, 'a_kv_norm_g': dg_kv,
        'a_w_uq': dw_uq, 'a_w_ukv': dw_ukv, 'a_w_out': dw_out_a, 'b_norm_g': dg_b, 'b_w_in': dw_in_b,
        'b_conv_w': dconv_w, 'b_conv_b': dconv_b, 'b_w_rg': dw_rg, 'b_b_rg': db_rg, 'b_w_ig': dw_ig,
        'b_b_ig': db_ig, 'b_lam': dlam, 'b_w_out': dw_out_b, 'final_norm_g': d_final_g,
    }
    return loss, dh0[x0:], grads, through, swapped


def _chip_major(whole, local_shape, axis):
    if axis is None:
        return jnp.broadcast_to(whole.reshape(1, -1), (N_CHIPS, whole.size))
    shape = list(local_shape)
    g = whole.reshape(shape[:axis] + [N_CHIPS, shape[axis]] + shape[axis + 1:])
    return jnp.moveaxis(g, axis, 0).reshape(N_CHIPS, -1)


def kernel(x, meta_tokens, a_norm_g, a_w_in, a_q_norm_g, a_kv_norm_g, a_w_uq, a_w_ukv, a_w_out, b_norm_g, b_w_in, b_conv_w, b_conv_b, b_w_rg, b_b_rg, b_w_ig, b_b_ig, b_lam, b_w_out, final_norm_g, loss_target, m_meta_tokens, m_a_norm_g, m_a_w_in, m_a_q_norm_g, m_a_kv_norm_g, m_a_w_uq, m_a_w_ukv, m_a_w_out, m_b_norm_g, m_b_w_in, m_b_conv_w, m_b_conv_b, m_b_w_rg, m_b_b_rg, m_b_w_ig, m_b_b_ig, m_b_lam, m_b_w_out, m_final_norm_g, v_meta_tokens, v_a_norm_g, v_a_w_in, v_a_q_norm_g, v_a_kv_norm_g, v_a_w_uq, v_a_w_ukv, v_a_w_out, v_b_norm_g, v_b_w_in, v_b_conv_w, v_b_conv_b, v_b_w_rg, v_b_b_rg, v_b_w_ig, v_b_b_ig, v_b_lam, v_b_w_out, v_final_norm_g):
    local_w = dict(zip(WEIGHTS, (meta_tokens, a_norm_g, a_w_in, a_q_norm_g, a_kv_norm_g, a_w_uq, a_w_ukv, a_w_out,
                                 b_norm_g, b_w_in, b_conv_w, b_conv_b, b_w_rg, b_b_rg, b_w_ig, b_b_ig, b_lam,
                                 b_w_out, final_norm_g)))
    local_m = dict(zip(WEIGHTS, (m_meta_tokens, m_a_norm_g, m_a_w_in, m_a_q_norm_g, m_a_kv_norm_g, m_a_w_uq,
                                 m_a_w_ukv, m_a_w_out, m_b_norm_g, m_b_w_in, m_b_conv_w, m_b_conv_b, m_b_w_rg,
                                 m_b_b_rg, m_b_w_ig, m_b_b_ig, m_b_lam, m_b_w_out, m_final_norm_g)))
    local_v = dict(zip(WEIGHTS, (v_meta_tokens, v_a_norm_g, v_a_w_in, v_a_q_norm_g, v_a_kv_norm_g, v_a_w_uq,
                                 v_a_w_ukv, v_a_w_out, v_b_norm_g, v_b_w_in, v_b_conv_w, v_b_conv_b, v_b_w_rg,
                                 v_b_b_rg, v_b_w_ig, v_b_b_ig, v_b_lam, v_b_w_out, v_final_norm_g)))
    matrices = ('a_w_in', 'a_w_uq', 'a_w_ukv', 'a_w_out', 'b_w_in', 'b_w_rg', 'b_w_ig', 'b_w_out')
    heads = a_w_uq.shape[-1] * N_CHIPS // (QK_NOPE + QK_ROPE)

    def transposed(a):
        return jnp.swapaxes(a, 1, 2)

    split, small, mid, late = [], [], [], []
    for n in WEIGHTS:
        if SHARD_AXIS[n] is None:
            continue
        if n.startswith('b_') or n == 'a_w_out':
            late.append((n, local_w[n].astype(BF16) if n in matrices else local_w[n], SHARD_AXIS[n]))
        elif n == 'a_w_in':
            split.append((n, transposed(local_w[n]).astype(BF16), 1, 2))
        elif n in matrices:
            mid.append((n, local_w[n].astype(BF16), SHARD_AXIS[n]))
        else:
            small.append((n, local_w[n], SHARD_AXIS[n]))
    gathered = _gather_weights([s[1:] for s in split], [s[1:] for s in small])
    whole = dict(zip([s[0] for s in split + small], gathered))
    mid_names, late_names = [s[0] for s in mid], [s[0] for s in late]
    wt = {n: _kernel_form(n, whole.get(n, local_w[n])) for n in WEIGHTS if n not in mid_names + late_names}

    c_idx = lax.axis_index("c").astype(jnp.int32).reshape(1)
    me_idx = (2 * lax.axis_index("x") + lax.axis_index("y")).astype(jnp.int32).reshape(1)
    loss, grad_x, grads, through, swapped = _local_grads(
        x[0], loss_target[0], wt, heads, c_idx,
        mid_names, _gather_whole("gather_weights_a", [s[1:] for s in mid]),
        late_names, _gather_whole("gather_weights_b", [s[1:] for s in late]))

    ext_uq = heads % N_CHIPS == 0
    started = [gr.name for gr, *_ in through + swapped]
    last = [_matrix_grad(n, grads[n], heads) for n in matrices if n not in started]
    rest = [n for n in WEIGHTS if n not in matrices]
    pieces = [_chip_major(grads[n], local_w[n].shape, SHARD_AXIS[n]) for n in rest]
    pieces.append(jnp.broadcast_to(loss[0:1, 0:1], (N_CHIPS, 1)))
    length = sum(p.shape[1] for p in pieces)
    unit = 2 * SUBLANES * 1024
    padded = -(-length // unit) * unit
    flat = jnp.concatenate(pieces + [jnp.zeros((N_CHIPS, padded - length), F32)], axis=1)
    last.append(_Grad('small', flat.reshape(N_CHIPS, padded // 1024, 1024), 'lead', padded // 2048, 1024))

    got = _halves_to_sibling("grads_to_sibling", last).run()
    swapped = swapped + [(gr, _chip_sum(gr, r, c_idx, F32 if gr.name == 'small' else BF16))
                         for gr, r in zip(last, got)]
    from_chips = _blocks_to_chips("grads_to_chips", [p for _, p in swapped]).run()
    through = through + [(gr, p, r) for (gr, p), r in zip(swapped, from_chips)]
    halves = [_sum_chips(gr.name, p, r, me_idx, c_idx) for gr, p, r in through]
    summed = dict(zip([gr.name for gr, _, _ in through], _share_with_sibling(halves)))
    if ext_uq:
        g = summed['a_w_uq']
        summed['a_w_uq'] = g.reshape(g.shape[0], -1, HEAD_W)[:, :, :QK_NOPE + QK_ROPE]
    total = summed['small'].reshape(-1)

    out_g, out_d, out_m, out_v = [], [], [], []
    off = 0
    for n in WEIGHTS:
        shape = local_w[n].shape
        view = transposed if n == 'a_w_in' else (lambda a: a)
        if n in matrices:
            g = summed[n].reshape(view(local_w[n]).shape)
        else:
            size = 1
            for s in shape:
                size *= s
            g = total[off:off + size].reshape(shape)
            off += size
        results = _adamw("adamw_" + n, _as2d(view(local_w[n])), _as2d(g), _as2d(view(local_m[n])), _as2d(view(local_v[n])))
        for out, r in zip((out_g, out_d, out_m, out_v), results):
            out.append(view(r.reshape(view(local_w[n]).shape)))

    return (total[off], grad_x[None], *out_g, *out_d, *out_m, *out_v)
```

```python
import functools

import jax
import jax.numpy as jnp
from jax import lax
from jax.experimental import pallas as pl
from jax.experimental.pallas import tpu as pltpu

F32 = jnp.float32
BF16 = jnp.bfloat16
MESH = pl.DeviceIdType.MESH

RMS_EPS = 1e-6
QK_NOPE = 128
QK_ROPE = 64
V_HEAD = 128
HEAD_W = 256
ROPE_BASE = 10000.0
Q_BLOCK = 128
MASK_VALUE = -1e30
CONV_WIDTH = 4
LRU_C = 8.0
N_CHIPS = 4

ADAM_LR = 0.001
ADAM_B1 = 0.9
ADAM_B2 = 0.999
ADAM_EPS = 1e-08
ADAM_WD = 0.01
ADAM_STEP = 10

VMEM_LIMIT_V7X = 56 * 1024 * 1024
LANES = 128
SUBLANES = 8

WEIGHTS = ['meta_tokens', 'a_norm_g', 'a_w_in', 'a_q_norm_g', 'a_kv_norm_g', 'a_w_uq', 'a_w_ukv',
           'a_w_out', 'b_norm_g', 'b_w_in', 'b_conv_w', 'b_conv_b', 'b_w_rg', 'b_b_rg', 'b_w_ig',
           'b_b_ig', 'b_lam', 'b_w_out', 'final_norm_g']
SHARD_AXIS = {'meta_tokens': 1, 'a_norm_g': None, 'a_w_in': 2, 'a_q_norm_g': None, 'a_kv_norm_g': None,
              'a_w_uq': 2, 'a_w_ukv': 2, 'a_w_out': 1, 'b_norm_g': 1, 'b_w_in': 2, 'b_conv_w': 2,
              'b_conv_b': 1, 'b_w_rg': 2, 'b_b_rg': 1, 'b_w_ig': 2, 'b_b_ig': 1, 'b_lam': 1,
              'b_w_out': 1, 'final_norm_g': None}


def _params(sem=None):
    return pltpu.CompilerParams(dimension_semantics=sem, vmem_limit_bytes=VMEM_LIMIT_V7X)


def _row_tile(tp):
    return 384 if (tp % 384 == 0 and tp >= 1152) else 128


def _sigmoid(x):
    return 1.0 / (1.0 + jnp.exp(-x))


def _rms(x):
    return lax.rsqrt(jnp.mean(x * x, axis=-1, keepdims=True) + RMS_EPS)


def _swap32(x):
    lane = lax.broadcasted_iota(jnp.int32, x.shape, 1)
    return jnp.where(lane < 32, pltpu.roll(x, 96, 1), pltpu.roll(x, 32, 1))


def _rope(x, cos_t, sin_t):
    return x * cos_t + _swap32(x) * sin_t


def _unrope(d, cos_t, sin_t):
    lane = lax.broadcasted_iota(jnp.int32, d.shape, 1)
    return jnp.where(lane < QK_ROPE, d * cos_t + _swap32(d * sin_t), 0.0)


def _dot(a, b):
    return jnp.dot(a, b, preferred_element_type=F32)


def _dot_nt(a, b):
    return lax.dot_general(a, b, (((1,), (1,)), ((), ())), preferred_element_type=F32)


def _dot_tn(a, b):
    return lax.dot_general(a, b, (((0,), (0,)), ((), ())), preferred_element_type=F32)


def _tokens_in_proj(name, tokens, meta, g, wt, splits, tm, pad, exchange=None):
    seq, kin = tokens.shape
    x0 = pad + meta.shape[0]
    tp = x0 + seq
    n = wt.shape[0]
    nt = tp // tm
    assert x0 <= tm and x0 % SUBLANES == 0 and pad % SUBLANES == 0 and nt >= 2

    def compute(x_hbm, meta_ref, g_ref, w_ref, h_ref, *rest):
        outs, (tile_s, sems) = rest[:len(splits)], rest[len(splits):]
        i = pl.program_id(0)

        def first_rows():
            return pltpu.make_async_copy(x_hbm.at[pl.ds(0, tm - x0), :], tile_s.at[0, pl.ds(x0, tm - x0), :], sems.at[0])

        def rows_of(t):
            return pltpu.make_async_copy(x_hbm.at[pl.ds(pl.multiple_of(t * tm - x0, SUBLANES), tm), :],
                                         tile_s.at[t % 2], sems.at[t % 2])

        @pl.when(i == 0)
        def _():
            tile_s[0, 0:pad, :] = jnp.zeros((pad, kin), F32)
            tile_s[0, pad:x0, :] = meta_ref[...]
            if x0 < tm:
                first_rows().start()
                first_rows().wait()

        pl.when(i > 0)(lambda: rows_of(i).wait())
        pl.when(i + 1 < nt)(lambda: rows_of(i + 1).start())

        xv = tile_s[i % 2]
        h_ref[...] = xv
        nrm = ((xv * _rms(xv)) * g_ref[...]).astype(BF16)
        y = _dot_nt(nrm, w_ref[...])
        for o_ref, (c0, c1) in zip(outs, splits):
            o_ref[...] = y[:, c0:c1]

    in_specs = [pl.BlockSpec(memory_space=pl.ANY),
                pl.BlockSpec(meta.shape, lambda i: (0, 0)),
                pl.BlockSpec((1, kin), lambda i: (0, 0)),
                pl.BlockSpec((n, kin), lambda i: (0, 0))]
    out_specs = [pl.BlockSpec((tm, kin), lambda i: (i, 0))]
    out_specs += [pl.BlockSpec((tm, c1 - c0), lambda i: (i, 0)) for c0, c1 in splits]
    out_shape = [jax.ShapeDtypeStruct((tp, kin), F32)]
    out_shape += [jax.ShapeDtypeStruct((tp, c1 - c0), F32) for c0, c1 in splits]
    args, scratch = [tokens, meta, g, wt], [pltpu.VMEM((2, tm, kin), F32), pltpu.SemaphoreType.DMA((2,))]
    body = _with_exchange(exchange, len(in_specs), len(out_specs),
                          lambda: pl.program_id(0) == 0, lambda: pl.program_id(0) == nt - 1, compute)
    if exchange is not None:
        in_specs = in_specs + exchange.specs(exchange.n_in)
        out_specs = out_specs + exchange.specs(exchange.n_out)
        out_shape = out_shape + exchange.out_shapes
        args, scratch = args + exchange.arrays, scratch + exchange.scratch()
    return pl.pallas_call(
        body, name=name, grid=(nt,),
        in_specs=in_specs, out_specs=out_specs, out_shape=out_shape, scratch_shapes=scratch,
        compiler_params=_params(("arbitrary",)),
    )(*args)


def _q_proj(q_lat, g, w_uq, cos_t, sin_t, heads, tm):
    tp, kin = q_lat.shape
    n = heads * HEAD_W

    def body(x_ref, g_ref, w_ref, cos_ref, sin_ref, q_ref):
        xv = x_ref[...]
        nrm = ((xv * _rms(xv)) * g_ref[...]).astype(BF16)
        y = _dot(nrm, w_ref[...])
        cos_v, sin_v = cos_ref[...], sin_ref[...]
        for h in range(heads):
            c0 = h * HEAD_W
            q_ref[:, c0:c0 + QK_NOPE] = y[:, c0:c0 + QK_NOPE].astype(BF16)
            q_ref[:, c0 + QK_NOPE:c0 + HEAD_W] = _rope(y[:, c0 + QK_NOPE:c0 + HEAD_W], cos_v, sin_v).astype(BF16)

    return pl.pallas_call(
        body, name="a_q_proj", grid=(tp // tm,),
        in_specs=[pl.BlockSpec((tm, kin), lambda i: (i, 0)),
                  pl.BlockSpec((1, kin), lambda i: (0, 0)),
                  pl.BlockSpec((kin, n), lambda i: (0, 0)),
                  pl.BlockSpec((tm, LANES), lambda i: (i, 0)),
                  pl.BlockSpec((tm, LANES), lambda i: (i, 0))],
        out_specs=pl.BlockSpec((tm, n), lambda i: (i, 0)),
        out_shape=jax.ShapeDtypeStruct((tp, n), BF16),
        compiler_params=_params(("parallel",)),
    )(q_lat, g, w_uq, cos_t, sin_t)


def _kv_proj(kv_lat, g, w_ukv, k_rope_raw, cos_t, sin_t, heads, tm):
    tp, kin = kv_lat.shape
    n = heads * (QK_NOPE + V_HEAD)

    def body(x_ref, g_ref, w_ref, kr_ref, cos_ref, sin_ref, k_ref, v_ref):
        xv = x_ref[...]
        nrm = ((xv * _rms(xv)) * g_ref[...]).astype(BF16)
        y = _dot(nrm, w_ref[...])
        kr = _rope(kr_ref[...], cos_ref[...], sin_ref[...]).astype(BF16)
        for h in range(heads):
            c0 = h * (QK_NOPE + V_HEAD)
            k_ref[:, h * HEAD_W:h * HEAD_W + QK_NOPE] = y[:, c0:c0 + QK_NOPE].astype(BF16)
            k_ref[:, h * HEAD_W + QK_NOPE:(h + 1) * HEAD_W] = kr
            v_ref[:, h * V_HEAD:(h + 1) * V_HEAD] = y[:, c0 + QK_NOPE:c0 + QK_NOPE + V_HEAD].astype(BF16)

    return pl.pallas_call(
        body, name="a_kv_proj", grid=(tp // tm,),
        in_specs=[pl.BlockSpec((tm, kin), lambda i: (i, 0)),
                  pl.BlockSpec((1, kin), lambda i: (0, 0)),
                  pl.BlockSpec((kin, n), lambda i: (0, 0)),
                  pl.BlockSpec((tm, LANES), lambda i: (i, 0)),
                  pl.BlockSpec((tm, LANES), lambda i: (i, 0)),
                  pl.BlockSpec((tm, LANES), lambda i: (i, 0))],
        out_specs=[pl.BlockSpec((tm, heads * HEAD_W), lambda i: (i, 0)),
                   pl.BlockSpec((tm, heads * V_HEAD), lambda i: (i, 0))],
        out_shape=[jax.ShapeDtypeStruct((tp, heads * HEAD_W), BF16),
                   jax.ShapeDtypeStruct((tp, heads * V_HEAD), BF16)],
        compiler_params=_params(("parallel",)),
    )(kv_lat, g, w_ukv, k_rope_raw, cos_t, sin_t)


def _as_rows(col):
    rows = col.shape[0]
    return jnp.transpose(jnp.broadcast_to(col, (rows, LANES)))[0:SUBLANES, :]


def _attn_mask(row0, col0, rows, cols, pad):
    row = row0 + lax.broadcasted_iota(jnp.int32, (rows, cols), 0)
    col = col0 + lax.broadcasted_iota(jnp.int32, (rows, cols), 1)
    return (col <= row) & (col >= pad)


LOG2E = 1.4426950408889634
FLASH_FWD_TRIPS = ((4, 2), (2, 2), (1, 1))


def _flash_fwd(q, k, v, heads, pad, tq, exchange=None):
    tp = q.shape[0]
    nq = tp // tq
    c2 = (QK_NOPE + QK_ROPE) ** -0.5 * LOG2E

    def compute(q_ref, k_ref, v_ref, o_ref, lse_ref):
        i = pl.program_id(1)

        def make_step(masked, blocks, parts=1):
            keys = blocks * tq // parts

            def step(j, carry):
                m, l, acc = carry
                offs = [pl.multiple_of(j * tq + part * keys, tq) for part in range(parts)]
                scores = [_dot_nt(q_ref[...], k_ref[pl.ds(off, keys), :]) for off in offs]
                for off, s in zip(offs, scores):
                    s = s * c2
                    if masked:
                        s = jnp.where(_attn_mask(i * tq, j * tq, tq, keys, pad), s, MASK_VALUE)
                    m_new = jnp.maximum(m, jnp.max(s, axis=-1, keepdims=True))
                    p = jnp.exp2(s - m_new)
                    alpha = jnp.exp2(m - m_new)
                    l = alpha * l + jnp.sum(p, axis=-1, keepdims=True)
                    acc = alpha * acc + _dot(p.astype(BF16), v_ref[pl.ds(off, keys), :])
                    m = m_new
                return m, l, acc
            return step

        init = (jnp.full((tq, 1), MASK_VALUE, F32), jnp.zeros((tq, 1), F32), jnp.zeros((tq, V_HEAD), F32))
        carry = make_step(True, 1)(0, init)
        first = 1
        for blocks, parts in FLASH_FWD_TRIPS:
            trips = jnp.maximum(i - first, 0) // blocks
            step_n = make_step(False, blocks, parts)
            carry = lax.fori_loop(0, trips, lambda t, cr, f=first, b=blocks, s=step_n: s(f + b * t, cr), carry)
            first = first + blocks * trips
        m, l, acc = lax.fori_loop(jnp.maximum(i, 1), i + 1, make_step(True, 1), carry)
        o_ref[...] = acc / l
        lse_ref[...] = _as_rows(m + jnp.log(l) * LOG2E)

    in_specs = [pl.BlockSpec((tq, HEAD_W), lambda h, i: (i, h)),
                pl.BlockSpec((tp, HEAD_W), lambda h, i: (0, h)),
                pl.BlockSpec((tp, V_HEAD), lambda h, i: (0, h))]
    out_specs = [pl.BlockSpec((tq, V_HEAD), lambda h, i: (i, h)),
                 pl.BlockSpec((None, None, SUBLANES, tq), lambda h, i: (h, i, 0, 0))]
    out_shape = [jax.ShapeDtypeStruct((tp, heads * V_HEAD), F32),
                 jax.ShapeDtypeStruct((heads, nq, SUBLANES, tq), F32)]
    args, scratch = [q, k, v], []
    body = _with_exchange(exchange, len(in_specs), len(out_specs),
                          lambda: (pl.program_id(0) == 0) & (pl.program_id(1) == 0),
                          lambda: (pl.program_id(0) == heads - 1) & (pl.program_id(1) == nq - 1), compute)
    if exchange is not None:
        in_specs = in_specs + exchange.specs(exchange.n_in)
        out_specs = out_specs + exchange.specs(exchange.n_out)
        out_shape = out_shape + exchange.out_shapes
        args, scratch = args + exchange.arrays, exchange.scratch()
    return pl.pallas_call(
        body, name="a_flash_fwd", grid=(heads, nq),
        in_specs=in_specs, out_specs=out_specs, out_shape=out_shape, scratch_shapes=scratch,
        compiler_params=_params(("arbitrary", "arbitrary")),
    )(*args)


def _out_proj_in_proj(name, a, gate, w_out, resid, g, w_in, splits, tm):
    tp, wd = a.shape
    d = w_out.shape[1]
    n = w_in.shape[1]

    def body(a_ref, gate_ref, wo_ref, res_ref, g_ref, wi_ref, h_ref, *outs):
        gv = gate_ref[...]
        y = (a_ref[...] * (gv * _sigmoid(gv))).astype(BF16)
        h = res_ref[...] + _dot(y, wo_ref[...])
        h_ref[...] = h
        nrm = ((h * _rms(h)) * g_ref[...]).astype(BF16)
        z = _dot(nrm, wi_ref[...])
        for o_ref, (c0, c1) in zip(outs, splits):
            o_ref[...] = z[:, c0:c1]

    return pl.pallas_call(
        body, name=name, grid=(tp // tm,),
        in_specs=[pl.BlockSpec((tm, wd), lambda i: (i, 0)),
                  pl.BlockSpec((tm, wd), lambda i: (i, 0)),
                  pl.BlockSpec((wd, d), lambda i: (0, 0)),
                  pl.BlockSpec((tm, d), lambda i: (i, 0)),
                  pl.BlockSpec((1, d), lambda i: (0, 0)),
                  pl.BlockSpec((d, n), lambda i: (0, 0))],
        out_specs=[pl.BlockSpec((tm, d), lambda i: (i, 0))]
        + [pl.BlockSpec((tm, c1 - c0), lambda i: (i, 0)) for c0, c1 in splits],
        out_shape=[jax.ShapeDtypeStruct((tp, d), F32)]
        + [jax.ShapeDtypeStruct((tp, c1 - c0), F32) for c0, c1 in splits],
        compiler_params=_params(("parallel",)),
    )(a, gate, w_out, resid, g, w_in)


def _lru_decay(r, sp):
    log_a = -LRU_C * r * sp
    a = jnp.exp(log_a)
    e2 = a * a
    x2 = 2.0 * log_a
    series = x2 * (1.0 + x2 * (0.5 + x2 * (1.0 / 6.0)))
    em1 = jnp.where(x2 > -0.02, series, e2 - 1.0)
    return a, e2, jnp.sqrt(-em1)


def _softplus(x):
    return jnp.maximum(x, 0.0) + jnp.log1p(jnp.exp(-jnp.abs(x)))


def _rglru_fwd(u, conv_w, conv_b, w_rg, b_rg, w_ig, b_ig, lam, pad, tm):
    tp, w = u.shape
    groups, blk = w_rg.shape[0], w_rg.shape[1]

    def body(u_ref, cw_ref, cb_ref, wr_ref, br_ref, wi_ref, bi_ref, lam_ref,
             uc_ref, r_ref, ig_ref, hs_ref, a_s, mult_ref, uext, b_s, hc):
        i = pl.program_id(0)

        @pl.when(i == 0)
        def _():
            uext[0:SUBLANES, :] = jnp.zeros((SUBLANES, w), F32)
            hc[...] = jnp.zeros((SUBLANES, w), F32)

        uext[SUBLANES:SUBLANES + tm, :] = u_ref[...]
        cw = cw_ref[...]
        uc = cb_ref[...] + uext[pl.ds(SUBLANES - 3, tm), :] * cw[0:1, :]
        uc = uc + uext[pl.ds(SUBLANES - 2, tm), :] * cw[1:2, :]
        uc = uc + uext[pl.ds(SUBLANES - 1, tm), :] * cw[2:3, :]
        uc = uc + uext[pl.ds(SUBLANES, tm), :] * cw[3:4, :]
        uc_ref[...] = uc
        uext[0:SUBLANES, :] = uext[tm:tm + SUBLANES, :]

        sp = _softplus(-lam_ref[...])
        for g in range(groups):
            sl = slice(g * blk, (g + 1) * blk)
            ucg = uc_ref[:, sl]
            ucb = ucg.astype(BF16)
            r = _sigmoid(_dot(ucb, wr_ref[g]) + br_ref[:, sl])
            ig = _sigmoid(_dot(ucb, wi_ref[g]) + bi_ref[:, sl])
            r_ref[:, sl] = r
            ig_ref[:, sl] = ig
            a, _, mult = _lru_decay(r, sp[:, sl])
            a_s[:, sl] = a
            mult_ref[:, sl] = mult
            b_s[:, sl] = mult * (ig * ucg)

        @pl.when(i == 0)
        def _():
            row = lax.broadcasted_iota(jnp.int32, (Q_BLOCK, w), 0)
            start = ig_ref[0:Q_BLOCK, :] * uc_ref[0:Q_BLOCK, :]
            b_s[0:Q_BLOCK, :] = jnp.where(row < pad, 0.0, jnp.where(row == pad, start, b_s[0:Q_BLOCK, :]))
            mult_ref[0:Q_BLOCK, :] = jnp.where(row == pad, 1.0, mult_ref[0:Q_BLOCK, :])

        row8 = lax.broadcasted_iota(jnp.int32, (SUBLANES, w), 0)

        def group(gi, h_in):
            off = pl.multiple_of(gi * SUBLANES, SUBLANES)
            av = a_s[pl.ds(off, SUBLANES), :]
            bv = b_s[pl.ds(off, SUBLANES), :]
            for k in (1, 2, 4):
                keep = row8 >= k
                bv = jnp.where(keep, av * pltpu.roll(bv, k, 0) + bv, bv)
                av = jnp.where(keep, av * pltpu.roll(av, k, 0), av)
            hv = av * h_in + bv
            hs_ref[pl.ds(off, SUBLANES), :] = hv
            return jnp.broadcast_to(hv[SUBLANES - 1:SUBLANES, :], (SUBLANES, w))

        hc[...] = lax.fori_loop(0, tm // SUBLANES, group, hc[...])

    row_spec = pl.BlockSpec((tm, w), lambda i: (i, 0))
    vec_spec = pl.BlockSpec((1, w), lambda i: (0, 0))
    mat_spec = pl.BlockSpec((groups, blk, blk), lambda i: (0, 0, 0))
    return pl.pallas_call(
        body, name="b_rglru_fwd", grid=(tp // tm,),
        in_specs=[row_spec, pl.BlockSpec((CONV_WIDTH, w), lambda i: (0, 0)), vec_spec,
                  mat_spec, vec_spec, mat_spec, vec_spec, vec_spec],
        out_specs=[row_spec] * 6,
        out_shape=[jax.ShapeDtypeStruct((tp, w), F32)] * 6,
        scratch_shapes=[pltpu.VMEM((tm + SUBLANES, w), F32), pltpu.VMEM((tm, w), F32),
                        pltpu.VMEM((SUBLANES, w), F32)],
        compiler_params=_params(("arbitrary",)),
    )(u, conv_w, conv_b, w_rg, b_rg, w_ig, b_ig, lam)


def _out_proj_loss(a, gate, w, resid, g, target, x0, tm):
    tp, wd = a.shape
    d = w.shape[1]
    assert x0 % Q_BLOCK == 0 and tm % Q_BLOCK == 0 and target.shape[0] == tp - x0
    lead = x0 // Q_BLOCK
    per = tm // Q_BLOCK

    def body(a_ref, gate_ref, w_ref, res_ref, g_ref, *rest):
        t_refs, (dh_ref, loss_ref, dg_ref, da_ref, dgate_ref, dw_ref) = rest[:per], rest[per:]
        i = pl.program_id(0)

        @pl.when(i == 0)
        def _():
            loss_ref[...] = jnp.zeros_like(loss_ref)
            dg_ref[...] = jnp.zeros_like(dg_ref)
            dw_ref[...] = jnp.zeros_like(dw_ref)

        gate_v = gate_ref[...]
        av = a_ref[...]
        sg = _sigmoid(gate_v)
        silu = gate_v * sg
        y = (av * silu).astype(BF16)
        h = res_ref[...] + _dot(y, w_ref[...])
        gv = g_ref[...]
        for b in range(per):
            rows = slice(b * Q_BLOCK, (b + 1) * Q_BLOCK)
            xv = h[rows, :]
            r = _rms(xv)
            xh = xv * r
            err = jnp.where(i * per + b >= lead, xh * gv - t_refs[b][...], 0.0)
            loss_ref[...] += 0.5 * jnp.sum(jnp.mean(err * err, axis=-1, keepdims=True))
            dy = err / d
            dg_ref[...] += jnp.sum(dy * xh, axis=0, keepdims=True)
            dxh = dy * gv
            dh_ref[rows, :] = r * (dxh - xh * jnp.mean(dxh * xh, axis=-1, keepdims=True))

        dob = dh_ref[...].astype(BF16)
        dyv = _dot_nt(dob, w_ref[...])
        da_ref[...] = dyv * silu
        dgate_ref[...] = dyv * av * (sg * (1.0 + gate_v * (1.0 - sg)))
        dw_ref[...] += _dot_tn(y, dob)

    def piece(b):
        return pl.BlockSpec((Q_BLOCK, d), lambda i: (jnp.maximum(i * per + b - lead, 0), 0))

    return pl.pallas_call(
        body, name="b_out_proj_loss", grid=(tp // tm,),
        in_specs=[pl.BlockSpec((tm, wd), lambda i: (i, 0)),
                  pl.BlockSpec((tm, wd), lambda i: (i, 0)),
                  pl.BlockSpec((wd, d), lambda i: (0, 0)),
                  pl.BlockSpec((tm, d), lambda i: (i, 0)),
                  pl.BlockSpec((1, d), lambda i: (0, 0))] + [piece(b) for b in range(per)],
        out_specs=[pl.BlockSpec((tm, d), lambda i: (i, 0)),
                   pl.BlockSpec((SUBLANES, LANES), lambda i: (0, 0)),
                   pl.BlockSpec((1, d), lambda i: (0, 0)),
                   pl.BlockSpec((tm, wd), lambda i: (i, 0)),
                   pl.BlockSpec((tm, wd), lambda i: (i, 0)),
                   pl.BlockSpec((wd, d), lambda i: (0, 0))],
        out_shape=[jax.ShapeDtypeStruct((tp, d), F32),
                   jax.ShapeDtypeStruct((SUBLANES, LANES), F32),
                   jax.ShapeDtypeStruct((1, d), F32),
                   jax.ShapeDtypeStruct((tp, wd), F32),
                   jax.ShapeDtypeStruct((tp, wd), F32),
                   jax.ShapeDtypeStruct((wd, d), F32)],
        compiler_params=_params(("arbitrary",)),
    )(a, gate, w, resid, g, *([target] * per))


def _with_exchange(exchange, n_in, n_out, first, last, compute):
    if exchange is None:
        return compute
    ex_in, ex_out = exchange.n_in, exchange.n_out

    def body(*refs):
        own_in, their_in = refs[:n_in], refs[n_in:n_in + ex_in]
        pos = n_in + ex_in
        own_out, their_out = refs[pos:pos + n_out], refs[pos + n_out:pos + n_out + ex_out]
        rest = refs[pos + n_out + ex_out:]
        own_scratch, sems = rest[:len(rest) - 2], rest[len(rest) - 2:]

        @pl.when(first())
        def _():
            exchange.start(their_in, their_out, sems)

        compute(*own_in, *own_out, *own_scratch)

        @pl.when(last())
        def _():
            exchange.finish(their_in, their_out, sems)

    return body


def _gated_out_bwd(name, dout, a, gate, w, tm, delta_heads=0, exchange=None):
    tp, wd = a.shape
    d = w.shape[1]
    nt = tp // tm

    def compute(do_ref, a_ref, gate_ref, w_ref, da_ref, dgate_ref, dw_ref, *delta_ref):
        @pl.when(pl.program_id(0) == 0)
        def _():
            dw_ref[...] = jnp.zeros_like(dw_ref)

        gv = gate_ref[...]
        av = a_ref[...]
        sg = _sigmoid(gv)
        silu = gv * sg
        dob = do_ref[...].astype(BF16)
        dy = _dot_nt(dob, w_ref[...])
        da = dy * silu
        da_ref[...] = da.astype(BF16)
        dgate_ref[...] = dy * av * (sg * (1.0 + gv * (1.0 - sg)))
        dw_ref[...] += _dot_tn((av * silu).astype(BF16), dob)
        for h in range(delta_heads):
            sl = slice(h * V_HEAD, (h + 1) * V_HEAD)
            delta_ref[0][h] = _as_rows(jnp.sum(da[:, sl] * av[:, sl], axis=-1, keepdims=True))

    out_specs = [pl.BlockSpec((tm, wd), lambda i: (i, 0)),
                 pl.BlockSpec((tm, wd), lambda i: (i, 0)),
                 pl.BlockSpec((wd, d), lambda i: (0, 0))]
    out_shape = [jax.ShapeDtypeStruct((tp, wd), BF16),
                 jax.ShapeDtypeStruct((tp, wd), F32),
                 jax.ShapeDtypeStruct((wd, d), F32)]
    if delta_heads:
        out_specs.append(pl.BlockSpec((delta_heads, None, SUBLANES, tm), lambda i: (0, i, 0, 0)))
        out_shape.append(jax.ShapeDtypeStruct((delta_heads, tp // tm, SUBLANES, tm), F32))
    in_specs = [pl.BlockSpec((tm, d), lambda i: (i, 0)),
                pl.BlockSpec((tm, wd), lambda i: (i, 0)),
                pl.BlockSpec((tm, wd), lambda i: (i, 0)),
                pl.BlockSpec((wd, d), lambda i: (0, 0))]
    args, scratch = [dout, a, gate, w], []
    body = _with_exchange(exchange, len(in_specs), len(out_specs),
                          lambda: pl.program_id(0) == 0, lambda: pl.program_id(0) == nt - 1, compute)
    if exchange is not None:
        in_specs = in_specs + exchange.specs(exchange.n_in)
        out_specs = out_specs + exchange.specs(exchange.n_out)
        out_shape = out_shape + exchange.out_shapes
        args, scratch = args + exchange.arrays, exchange.scratch()
    return pl.pallas_call(
        body, name=name, grid=(nt,),
        in_specs=in_specs, out_specs=out_specs, out_shape=out_shape, scratch_shapes=scratch,
        compiler_params=_params(("arbitrary",)),
    )(*args)


def _rglru_bwd(dhs, hs, r, ig, uc, u, a, mult, conv_w, w_rg, w_ig, lam, pad, tm):
    tp, w = u.shape
    groups, blk = w_rg.shape[0], w_rg.shape[1]
    nt = tp // tm
    per8 = tm // SUBLANES

    def body(dhs_ref, hs_ref, hprev_ref, r_ref, ig_ref, uc_ref, u_ref, uprev_ref, a_ref, mult_ref,
             cw_ref, wr_ref, wi_ref, lam_ref,
             du_ref, dcw_ref, dcb_ref, dwr_ref, dbr_ref, dwi_ref, dbi_ref, dlam_ref,
             aext, c_s, g_s, hext, uext, ducext, gc):
        step = pl.program_id(0)
        ti = nt - 1 - step

        @pl.when(step == 0)
        def _():
            for ref in (dcw_ref, dcb_ref, dwr_ref, dbr_ref, dwi_ref, dbi_ref, dlam_ref):
                ref[...] = jnp.zeros_like(ref)
            aext[tm:tm + SUBLANES, :] = jnp.zeros((SUBLANES, w), F32)
            ducext[tm:tm + SUBLANES, :] = jnp.zeros((SUBLANES, w), F32)
            gc[...] = jnp.zeros((SUBLANES, w), F32)

        lam_v = lam_ref[...]
        sp = _softplus(-lam_v)
        row = ti * tm + lax.broadcasted_iota(jnp.int32, (tm, w), 0)

        rv = r_ref[...]
        a = a_ref[...]
        mult = mult_ref[...]
        aext[0:tm, :] = a
        c_s[...] = aext[pl.ds(1, tm), :]
        row8 = lax.broadcasted_iota(jnp.int32, (SUBLANES, w), 0)

        def group(gi, g_in):
            off = pl.multiple_of((per8 - 1 - gi) * SUBLANES, SUBLANES)
            cv = c_s[pl.ds(off, SUBLANES), :]
            dv = dhs_ref[pl.ds(off, SUBLANES), :]
            for k in (1, 2, 4):
                keep = row8 < SUBLANES - k
                dv = jnp.where(keep, cv * pltpu.roll(dv, SUBLANES - k, 0) + dv, dv)
                cv = jnp.where(keep, cv * pltpu.roll(cv, SUBLANES - k, 0), cv)
            gv = cv * g_in + dv
            g_s[pl.ds(off, SUBLANES), :] = gv
            return jnp.broadcast_to(gv[0:1, :], (SUBLANES, w))

        gc[...] = lax.fori_loop(0, per8, group, gc[...])
        aext[tm:tm + SUBLANES, :] = aext[0:SUBLANES, :]

        gsc = jnp.where(row < pad, 0.0, g_s[...])
        hext[0:SUBLANES, :] = hprev_ref[...]
        hext[SUBLANES:SUBLANES + tm, :] = hs_ref[...]
        hprev = jnp.where(row == 0, 0.0, hext[pl.ds(SUBLANES - 1, tm), :])
        igv = ig_ref[...]
        ucv = uc_ref[...]
        first = row == pad
        dmult = gsc * (igv * ucv)
        dig = gsc * mult * ucv
        duc = gsc * mult * igv
        dlog_a = (gsc * hprev) * a + jnp.where(first, 0.0, dmult * (-(a * a) / mult))
        dlam_ref[...] += jnp.sum(dlog_a * rv, axis=0, keepdims=True) * (LRU_C * _sigmoid(-lam_v))
        dpre_r = dlog_a * (-LRU_C * sp) * (rv * (1.0 - rv))
        dpre_i = dig * (igv * (1.0 - igv))
        dbr_ref[...] += jnp.sum(dpre_r, axis=0, keepdims=True)
        dbi_ref[...] += jnp.sum(dpre_i, axis=0, keepdims=True)
        for g in range(groups):
            sl = slice(g * blk, (g + 1) * blk)
            ucb = ucv[:, sl].astype(BF16)
            drb = dpre_r[:, sl].astype(BF16)
            dib = dpre_i[:, sl].astype(BF16)
            dwr_ref[g] += _dot_tn(ucb, drb)
            dwi_ref[g] += _dot_tn(ucb, dib)
            ducext[0:tm, sl] = duc[:, sl] + _dot_nt(drb, wr_ref[g]) + _dot_nt(dib, wi_ref[g])

        ducv = ducext[0:tm, :]
        cw = cw_ref[...]
        dcb_ref[...] += jnp.sum(ducv, axis=0, keepdims=True)
        uext[0:SUBLANES, :] = jnp.where(ti == 0, 0.0, uprev_ref[...])
        uext[SUBLANES:SUBLANES + tm, :] = u_ref[...]
        for j in range(CONV_WIDTH):
            ush = uext[pl.ds(SUBLANES - (CONV_WIDTH - 1 - j), tm), :]
            dcw_ref[j:j + 1, :] += jnp.sum(ducv * ush, axis=0, keepdims=True)
        du = ducv * cw[3:4, :]
        for k in range(1, CONV_WIDTH):
            du = du + ducext[pl.ds(k, tm), :] * cw[3 - k:4 - k, :]
        du_ref[...] = du
        ducext[tm:tm + SUBLANES, :] = ducext[0:SUBLANES, :]

    rev = lambda s: (nt - 1 - s, 0)
    halo = lambda s: (jnp.maximum((nt - 1 - s) * per8 - 1, 0), 0)
    row_spec = pl.BlockSpec((tm, w), rev)
    halo_spec = pl.BlockSpec((SUBLANES, w), halo)
    vec_spec = pl.BlockSpec((1, w), lambda s: (0, 0))
    mat_spec = pl.BlockSpec((groups, blk, blk), lambda s: (0, 0, 0))
    cw_spec = pl.BlockSpec((CONV_WIDTH, w), lambda s: (0, 0))
    return pl.pallas_call(
        body, name="b_rglru_bwd", grid=(nt,),
        in_specs=[row_spec, row_spec, halo_spec, row_spec, row_spec, row_spec, row_spec, halo_spec, row_spec, row_spec,
                  cw_spec, mat_spec, mat_spec, vec_spec],
        out_specs=[row_spec, cw_spec, vec_spec, mat_spec, vec_spec, mat_spec, vec_spec, vec_spec],
        out_shape=[jax.ShapeDtypeStruct((tp, w), F32), jax.ShapeDtypeStruct((CONV_WIDTH, w), F32),
                   jax.ShapeDtypeStruct((1, w), F32), jax.ShapeDtypeStruct((groups, blk, blk), F32),
                   jax.ShapeDtypeStruct((1, w), F32), jax.ShapeDtypeStruct((groups, blk, blk), F32),
                   jax.ShapeDtypeStruct((1, w), F32), jax.ShapeDtypeStruct((1, w), F32)],
        scratch_shapes=[pltpu.VMEM((tm + SUBLANES, w), F32), pltpu.VMEM((tm, w), F32), pltpu.VMEM((tm, w), F32),
                        pltpu.VMEM((tm + SUBLANES, w), F32), pltpu.VMEM((tm + SUBLANES, w), F32),
                        pltpu.VMEM((tm + SUBLANES, w), F32), pltpu.VMEM((SUBLANES, w), F32)],
        compiler_params=_params(("arbitrary",)),
    )(dhs, hs, hs, r, ig, uc, u, u, a, mult, conv_w, w_rg, w_ig, lam)


def _norm_matmul_bwd(name, x, g, w, dys, tm, resid=None, prologue=None, extra_out=None, exchange=None,
                     transposed=False):
    tp, kin = x.shape
    n = w.shape[0] if transposed else w.shape[1]
    w_shape = (n, kin) if transposed else (kin, n)
    nt = tp // tm
    n_dy = len(dys)
    has_res = resid is not None
    has_extra = extra_out is not None

    def compute(*refs):
        x_ref, g_ref, w_ref = refs[:3]
        dy_refs = refs[3:3 + n_dy]
        pos = 3 + n_dy
        res_ref = refs[pos] if has_res else None
        pos += int(has_res)
        dx_ref, dw_ref, dg_ref = refs[pos:pos + 3]
        pos += 3
        ex_ref = refs[pos] if has_extra else None
        pos += int(has_extra)
        dy_s = refs[pos]

        @pl.when(pl.program_id(0) == 0)
        def _():
            dw_ref[...] = jnp.zeros_like(dw_ref)
            dg_ref[...] = jnp.zeros_like(dg_ref)

        if prologue is None:
            c0 = 0
            for ref in dy_refs:
                dy_s[:, c0:c0 + ref.shape[1]] = ref[...].astype(BF16)
                c0 += ref.shape[1]
        else:
            prologue(dy_refs, dy_s, ex_ref)

        xv = x_ref[...]
        gv = g_ref[...]
        r = _rms(xv)
        xh = xv * r
        dyb = dy_s[...]
        nb = (xh * gv).astype(BF16)
        if transposed:
            dn = _dot(dyb, w_ref[...])
            dw_ref[...] += _dot_tn(dyb, nb)
        else:
            dn = _dot_nt(dyb, w_ref[...])
            dw_ref[...] += _dot_tn(nb, dyb)
        dg_ref[...] += jnp.sum(dn * xh, axis=0, keepdims=True)
        dxh = dn * gv
        dx = r * (dxh - xh * jnp.mean(dxh * xh, axis=-1, keepdims=True))
        if has_res:
            dx = dx + res_ref[...]
        dx_ref[...] = dx

    row = lambda width: pl.BlockSpec((tm, width), lambda i: (i, 0))
    in_specs = [row(kin), pl.BlockSpec((1, kin), lambda i: (0, 0)), pl.BlockSpec(w_shape, lambda i: (0, 0))]
    in_specs += [row(a.shape[1]) for a in dys]
    args = [x, g, w, *dys]
    if has_res:
        in_specs.append(row(kin))
        args.append(resid)
    out_specs = [row(kin), pl.BlockSpec(w_shape, lambda i: (0, 0)), pl.BlockSpec((1, kin), lambda i: (0, 0))]
    out_shape = [jax.ShapeDtypeStruct((tp, kin), F32), jax.ShapeDtypeStruct(w_shape, F32),
                 jax.ShapeDtypeStruct((1, kin), F32)]
    if has_extra:
        out_specs.append(row(extra_out[0]))
        out_shape.append(jax.ShapeDtypeStruct((tp, extra_out[0]), extra_out[1]))
    scratch = [pltpu.VMEM((tm, n), BF16)]
    body = _with_exchange(exchange, len(in_specs), len(out_specs),
                          lambda: pl.program_id(0) == 0, lambda: pl.program_id(0) == nt - 1, compute)
    if exchange is not None:
        in_specs = in_specs + exchange.specs(exchange.n_in)
        out_specs = out_specs + exchange.specs(exchange.n_out)
        out_shape = out_shape + exchange.out_shapes
        args, scratch = args + exchange.arrays, scratch + exchange.scratch()
    return pl.pallas_call(
        body, name=name, grid=(nt,),
        in_specs=in_specs, out_specs=out_specs, out_shape=out_shape, scratch_shapes=scratch,
        compiler_params=_params(("arbitrary",)),
    )(*args)


def _flash_bwd(q, k, v, lse, delta, do, heads, pad, tq, exchange=None):
    tp = q.shape[0]
    nq = tp // tq
    scale = (QK_NOPE + QK_ROPE) ** -0.5
    c2 = scale * LOG2E

    def compute(q_ref, k_ref, v_ref, lse_ref, delta_ref, do_ref, dq_ref, dk_ref, dv_ref):
        j = pl.program_id(1)

        @pl.when(j == 0)
        def _():
            dq_ref[...] = jnp.zeros_like(dq_ref)

        kv = k_ref[...]
        vv = v_ref[...]

        def rows_of(ref, i, blocks):
            parts = [ref[i + b][0:1, :] for b in range(blocks)]
            return parts[0] if blocks == 1 else jnp.concatenate(parts, axis=1)

        def make_step(masked, blocks):
            def step(i, carry):
                off = pl.multiple_of(i * tq, tq)
                qv = q_ref[pl.ds(off, blocks * tq), :]
                dob = do_ref[pl.ds(off, blocks * tq), :]
                p = jnp.exp2(_dot_nt(kv, qv) * c2 - rows_of(lse_ref, i, blocks))
                if masked:
                    key = j * tq + lax.broadcasted_iota(jnp.int32, (tq, tq), 0)
                    qry = j * tq + lax.broadcasted_iota(jnp.int32, (tq, tq), 1)
                    first = jnp.where((key <= qry) & (key >= pad), p[:, :tq], 0.0)
                    p = first if blocks == 1 else jnp.concatenate([first, p[:, tq:]], axis=1)
                dv_ref[...] += _dot(p.astype(BF16), dob)
                dp = _dot_nt(vv, dob)
                ds = (p * (dp - rows_of(delta_ref, i, blocks)) * scale).astype(BF16)
                dk_ref[...] += _dot(ds, qv)
                dq_ref[pl.ds(off, blocks * tq), :] += _dot_tn(ds, kv)
                return carry
            return step

        dk_ref[...] = jnp.zeros_like(dk_ref)
        dv_ref[...] = jnp.zeros_like(dv_ref)
        odd = (nq - j) % 2
        lax.fori_loop(0, odd, lambda t, cr: make_step(True, 1)(j, cr), 0)
        lax.fori_loop(0, 1 - odd, lambda t, cr: make_step(True, 2)(j, cr), 0)
        start = j + 2 - odd
        for blocks in (4, 2):
            trips = (nq - start) // blocks
            step_n = make_step(False, blocks)
            lax.fori_loop(0, trips, lambda t, cr, s=start, b=blocks, f=step_n: f(s + b * t, cr), 0)
            start = start + blocks * trips

    in_specs = [pl.BlockSpec((tp, HEAD_W), lambda h, j: (0, h)),
                pl.BlockSpec((tq, HEAD_W), lambda h, j: (j, h)),
                pl.BlockSpec((tq, V_HEAD), lambda h, j: (j, h)),
                pl.BlockSpec((None, nq, SUBLANES, tq), lambda h, j: (h, 0, 0, 0)),
                pl.BlockSpec((None, nq, SUBLANES, tq), lambda h, j: (h, 0, 0, 0)),
                pl.BlockSpec((tp, V_HEAD), lambda h, j: (0, h))]
    out_specs = [pl.BlockSpec((tp, HEAD_W), lambda h, j: (0, h)),
                 pl.BlockSpec((tq, HEAD_W), lambda h, j: (j, h)),
                 pl.BlockSpec((tq, V_HEAD), lambda h, j: (j, h))]
    out_shape = [jax.ShapeDtypeStruct((tp, heads * HEAD_W), F32),
                 jax.ShapeDtypeStruct((tp, heads * HEAD_W), F32),
                 jax.ShapeDtypeStruct((tp, heads * V_HEAD), F32)]
    args, scratch = [q, k, v, lse, delta, do], []
    body = _with_exchange(exchange, len(in_specs), len(out_specs),
                          lambda: (pl.program_id(0) == 0) & (pl.program_id(1) == 0),
                          lambda: (pl.program_id(0) == heads - 1) & (pl.program_id(1) == nq - 1), compute)
    if exchange is not None:
        in_specs = in_specs + exchange.specs(exchange.n_in)
        out_specs = out_specs + exchange.specs(exchange.n_out)
        out_shape = out_shape + exchange.out_shapes
        args, scratch = args + exchange.arrays, exchange.scratch()
    return pl.pallas_call(
        body, name="a_flash_bwd", grid=(heads, nq),
        in_specs=in_specs, out_specs=out_specs, out_shape=out_shape, scratch_shapes=scratch,
        compiler_params=_params(("arbitrary", "arbitrary")),
    )(*args)


def _position():
    return lax.axis_index("x"), lax.axis_index("y"), lax.axis_index("c")


def _other_chips(x, y):
    return [(1 - x, y), (x, 1 - y), (1 - x, 1 - y)]


def _block(ref, shard_axis, n, k, split_axis=None, m=None, h=None):
    idx = []
    for a in range(len(ref.shape)):
        start = 0
        size = None
        if a == shard_axis:
            start, size = k * n, n
        if a == split_axis:
            size = (n if a == shard_axis else m) // 2
            start = start + h * size
        idx.append(slice(None) if size is None else pl.ds(start, size))
    return ref.at[tuple(idx)]


def _gather_weights(split, whole_small):
    ns, nw = len(split), len(whole_small)
    n = ns + nw
    arrs = [s[0] for s in split] + [s[0] for s in whole_small]
    axes = [s[1] for s in split] + [s[1] for s in whole_small]

    def body(*refs):
        ins, outs = refs[:n], refs[n:2 * n]
        ici_send, ici_recv, d2d_send, d2d_recv, sib_send, sib_recv = refs[2 * n:]
        x, y, c = _position()
        me = 2 * x + y
        others = _other_chips(x, y)
        sent, local = [], []

        def remote(src, dst, sems, idx, to):
            return pltpu.make_async_remote_copy(src_ref=src, dst_ref=dst, send_sem=sems[0].at[idx],
                                                recv_sem=sems[1].at[idx], device_id=to, device_id_type=MESH)

        for a in range(n):
            width = ins[a].shape[axes[a]]
            mine = remote(ins[a], _block(outs[a], axes[a], width, me), (sib_send, sib_recv), a, (x, y, 1 - c))
            mine.start()
            local.append(mine)
            for j, (px, py) in enumerate(others):
                if a < ns:
                    sx = split[a][2]
                    src = _block(ins[a], None, None, None, sx, ins[a].shape[sx], c)
                    dst = _block(outs[a], axes[a], width, me, sx, outs[a].shape[sx], c)
                else:
                    src, dst = ins[a], _block(outs[a], axes[a], width, me)
                cp = remote(src, dst, (ici_send, ici_recv), 3 * a + j, (px, py, c))
                cp.start()
                sent.append(cp)
        for a in range(ns):
            width = ins[a].shape[axes[a]]
            sx = split[a][2]
            for j, (px, py) in enumerate(others):
                theirs = _block(outs[a], axes[a], width, 2 * px + py, sx, outs[a].shape[sx], c)
                remote(theirs, theirs, (ici_send, ici_recv), 3 * a + j, (px, py, c)).wait_recv()
                fwd = remote(theirs, theirs, (d2d_send, d2d_recv), 3 * a + j, (x, y, 1 - c))
                fwd.start()
                sent.append(fwd)
        for a in range(ns, n):
            width = ins[a].shape[axes[a]]
            for j, (px, py) in enumerate(others):
                theirs = _block(outs[a], axes[a], width, 2 * px + py)
                remote(theirs, theirs, (ici_send, ici_recv), 3 * a + j, (px, py, c)).wait_recv()
        for a in range(ns):
            width = ins[a].shape[axes[a]]
            sx = split[a][2]
            for j, (px, py) in enumerate(others):
                from_sibling = _block(outs[a], axes[a], width, 2 * px + py, sx, outs[a].shape[sx], 1 - c)
                remote(from_sibling, from_sibling, (d2d_send, d2d_recv), 3 * a + j, (x, y, 1 - c)).wait_recv()
        for cp in sent:
            cp.wait_send()
        for cp in local:
            cp.wait()

    def whole_shape(a, axis):
        shape = list(a.shape)
        shape[axis] *= N_CHIPS
        return jax.ShapeDtypeStruct(tuple(shape), a.dtype)

    any_spec = pl.BlockSpec(memory_space=pl.ANY)
    return pl.pallas_call(
        body, name="gather_weights",
        in_specs=[any_spec] * n, out_specs=[any_spec] * n,
        out_shape=[whole_shape(a, ax) for a, ax in zip(arrs, axes)],
        scratch_shapes=[pltpu.SemaphoreType.DMA((3 * n,)), pltpu.SemaphoreType.DMA((3 * n,)),
                        pltpu.SemaphoreType.DMA((3 * ns,)), pltpu.SemaphoreType.DMA((3 * ns,)),
                        pltpu.SemaphoreType.DMA((n,)), pltpu.SemaphoreType.DMA((n,))],
        compiler_params=pltpu.CompilerParams(has_side_effects=True),
    )(*arrs)


class _Grad:
    def __init__(self, name, g, kind, rh, cols, groups=None):
        self.name, self.g, self.kind, self.rh, self.cols, self.groups = name, g, kind, rh, cols, groups
        if kind == 'rows':
            self.tr = rh
        elif kind == 'gate':
            self.tr = rh // (groups // 2)
        else:
            self.tr = rh if rh <= 512 else 256
        self.nb = rh // self.tr

    def pieces(self, ref, k, h):
        rh, cols = self.rh, self.cols
        if self.kind == 'cols':
            return [(ref.at[pl.ds(h * rh, rh), pl.ds(k * cols, cols)], 0, rh)]
        if self.kind == 'rows':
            return [(ref.at[pl.ds((2 * k + h) * rh, rh), :], 0, rh)]
        if self.kind == 'lead':
            return [(ref.at[k, pl.ds(h * rh, rh), :], 0, rh)]
        per = self.groups // 2
        return [(ref.at[pl.ds((((h * per + gi) * N_CHIPS) + k) * self.tr, self.tr), :], gi * self.tr, self.tr)
                for gi in range(per)]

    def block_spec(self):
        tr, nb, cols = self.tr, self.nb, self.cols
        if self.kind == 'cols':
            return pl.BlockSpec((tr, cols), lambda k, i, c: (c[0] * nb + i, k))
        if self.kind == 'rows':
            return pl.BlockSpec((tr, cols), lambda k, i, c: (2 * k + c[0], 0))
        if self.kind == 'lead':
            return pl.BlockSpec((None, tr, cols), lambda k, i, c: (k, c[0] * nb + i, 0))
        return pl.BlockSpec((tr, cols), lambda k, i, c: ((c[0] * nb + i) * N_CHIPS + k, 0))


class _Exchange:
    def __init__(self, name, arrays, out_shapes, n_copies, copies):
        self.name, self.arrays, self.out_shapes, self.n_copies, self.copies = name, arrays, out_shapes, n_copies, copies
        self.n_in, self.n_out = len(arrays), len(out_shapes)

    def specs(self, n):
        return [pl.BlockSpec(memory_space=pl.ANY)] * n

    def scratch(self):
        return [pltpu.SemaphoreType.DMA((self.n_copies,)), pltpu.SemaphoreType.DMA((self.n_copies,))]

    def _descriptors(self, in_refs, out_refs, sems):
        return self.copies(in_refs, out_refs, sems[0], sems[1])

    def start(self, in_refs, out_refs, sems):
        for cp in self._descriptors(in_refs, out_refs, sems):
            cp.start()

    def finish(self, in_refs, out_refs, sems):
        for cp in self._descriptors(in_refs, out_refs, sems):
            cp.wait()

    def __add__(self, other):
        def copies(ins, outs, send_sems, recv_sems, base=0):
            return (self.copies(ins[:self.n_in], outs[:self.n_out], send_sems, recv_sems, base)
                    + other.copies(ins[self.n_in:], outs[self.n_out:], send_sems, recv_sems, base + self.n_copies))

        return _Exchange(self.name + "_" + other.name, self.arrays + other.arrays, self.out_shapes + other.out_shapes,
                         self.n_copies + other.n_copies, copies)

    def run(self):
        def body(*refs):
            ins, outs, sems = refs[:self.n_in], refs[self.n_in:self.n_in + self.n_out], refs[self.n_in + self.n_out:]
            self.start(ins, outs, sems)
            self.finish(ins, outs, sems)

        return pl.pallas_call(
            body, name=self.name,
            in_specs=self.specs(self.n_in), out_specs=self.specs(self.n_out), out_shape=self.out_shapes,
            scratch_shapes=self.scratch(),
            compiler_params=pltpu.CompilerParams(has_side_effects=True),
        )(*self.arrays)


def _gather_whole(name, shards):
    def copies(ins, outs, send_sems, recv_sems, base=0):
        x, y, c = _position()
        me = 2 * x + y
        made = []
        for a, (_, axis) in enumerate(shards):
            dst = _block(outs[a], axis, ins[a].shape[axis], me)
            for j, to in enumerate([(x, y, 1 - c)] + [(px, py, c) for px, py in _other_chips(x, y)]):
                idx = base + 4 * a + j
                made.append(pltpu.make_async_remote_copy(
                    src_ref=ins[a], dst_ref=dst, send_sem=send_sems.at[idx], recv_sem=recv_sems.at[idx],
                    device_id=to, device_id_type=MESH))
        return made

    def whole_shape(a, axis):
        shape = list(a.shape)
        shape[axis] *= N_CHIPS
        return jax.ShapeDtypeStruct(tuple(shape), a.dtype)

    return _Exchange(name, [s[0] for s in shards], [whole_shape(*s) for s in shards], 4 * len(shards), copies)


def _halves_to_sibling(name, grads):
    total = sum(len(gr.pieces(gr.g, 0, 0)) * N_CHIPS for gr in grads)

    def copies(ins, outs, send_sems, recv_sems, base=0):
        x, y, c = _position()
        made = []
        for gr, g_ref, got_ref in zip(grads, ins, outs):
            for k in range(N_CHIPS):
                for src, r0, nr in gr.pieces(g_ref, k, 1 - c):
                    idx = base + len(made)
                    made.append(pltpu.make_async_remote_copy(
                        src_ref=src, dst_ref=got_ref.at[k, pl.ds(r0, nr), :],
                        send_sem=send_sems.at[idx], recv_sem=recv_sems.at[idx],
                        device_id=(x, y, 1 - c), device_id_type=MESH))
        return made

    return _Exchange(name, [gr.g for gr in grads],
                     [jax.ShapeDtypeStruct((N_CHIPS, gr.rh, gr.cols), F32) for gr in grads], total, copies)


def _chip_sum(gr, got, c, wire=BF16):
    def body(c_ref, g_ref, got_ref, o_ref):
        o_ref[...] = (g_ref[...] + got_ref[...]).astype(wire)

    tile = pl.BlockSpec((None, gr.tr, gr.cols), lambda k, i, c_ref: (k, i, 0))
    return pl.pallas_call(
        body, name="chip_sum_" + gr.name,
        grid_spec=pltpu.PrefetchScalarGridSpec(
            num_scalar_prefetch=1, grid=(N_CHIPS, gr.nb),
            in_specs=[gr.block_spec(), tile], out_specs=tile),
        out_shape=jax.ShapeDtypeStruct((N_CHIPS, gr.rh, gr.cols), wire),
        compiler_params=_params(("parallel", "parallel")),
    )(c, gr.g, got)


def _blocks_to_chips(name, parts):
    n = len(parts)

    def copies(ins, outs, send_sems, recv_sems, base=0):
        x, y, c = _position()
        made = []
        for a in range(n):
            for j, (px, py) in enumerate(_other_chips(x, y)):
                idx = base + 3 * a + j
                made.append(pltpu.make_async_remote_copy(
                    src_ref=ins[a].at[2 * px + py], dst_ref=outs[a].at[j],
                    send_sem=send_sems.at[idx], recv_sem=recv_sems.at[idx],
                    device_id=(px, py, c), device_id_type=MESH))
        return made

    return _Exchange(name, parts, [jax.ShapeDtypeStruct((3,) + p.shape[1:], p.dtype) for p in parts], 3 * n, copies)


def _sum_chips(name, part, got, me, c):
    nk, rh, cols = part.shape
    tr = rh if rh <= 512 else 256
    nb = rh // tr

    def body(me_ref, c_ref, own_ref, *rest):
        got_refs, o_ref = rest[:nk], rest[nk]
        own = own_ref[...].astype(F32)
        acc = None
        for k in range(nk):
            term = jnp.where(me_ref[0] == k, own, got_refs[k][...].astype(F32))
            acc = term if acc is None else acc + term
        o_ref[...] = acc

    def got_map(k):
        def index(i, me_ref, c_ref):
            xor = jnp.bitwise_xor(me_ref[0], k)
            slot = jnp.where(xor == 1, 1, jnp.where(xor == 3, 2, 0))
            return (slot, i, 0)
        return index

    return pl.pallas_call(
        body, name="sum_" + name,
        grid_spec=pltpu.PrefetchScalarGridSpec(
            num_scalar_prefetch=2, grid=(nb,),
            in_specs=[pl.BlockSpec((None, tr, cols), lambda i, me_ref, c_ref: (me_ref[0], i, 0))]
            + [pl.BlockSpec((None, tr, cols), got_map(k)) for k in range(nk)],
            out_specs=pl.BlockSpec((tr, cols), lambda i, me_ref, c_ref: (c_ref[0] * nb + i, 0))),
        out_shape=jax.ShapeDtypeStruct((2 * rh, cols), F32),
        compiler_params=_params(("parallel",)),
    )(me, c, part, *([got] * nk))


def _share_with_sibling(halves):
    n = len(halves)

    def body(*refs):
        outs = refs[n:2 * n]
        send_sems, recv_sems = refs[2 * n:]
        x, y, c = _position()
        copies = []
        for a in range(n):
            rh = outs[a].shape[0] // 2
            mine = outs[a].at[pl.ds(c * rh, rh), :]
            cp = pltpu.make_async_remote_copy(
                src_ref=mine, dst_ref=mine, send_sem=send_sems.at[a], recv_sem=recv_sems.at[a],
                device_id=(x, y, 1 - c), device_id_type=MESH)
            cp.start()
            copies.append(cp)
        for cp in copies:
            cp.wait()

    any_spec = pl.BlockSpec(memory_space=pl.ANY)
    return pl.pallas_call(
        body, name="grads_share",
        in_specs=[any_spec] * n, out_specs=[any_spec] * n,
        out_shape=[jax.ShapeDtypeStruct(h.shape, h.dtype) for h in halves],
        input_output_aliases={a: a for a in range(n)},
        scratch_shapes=[pltpu.SemaphoreType.DMA((n,)), pltpu.SemaphoreType.DMA((n,))],
        compiler_params=pltpu.CompilerParams(has_side_effects=True),
    )(*halves)


def _adamw(name, w, g, m, v):
    rows, cols = w.shape
    tr = 256 if rows % 256 == 0 else rows

    def body(w_ref, g_ref, m_ref, v_ref, g_out_ref, d_ref, nm_ref, nv_ref):
        gv = g_ref[...]
        g_out_ref[...] = gv
        mn = ADAM_B1 * m_ref[...] + (1.0 - ADAM_B1) * gv
        vn = ADAM_B2 * v_ref[...] + (1.0 - ADAM_B2) * (gv * gv)
        m_hat = mn / (1.0 - ADAM_B1 ** ADAM_STEP)
        v_hat = vn / (1.0 - ADAM_B2 ** ADAM_STEP)
        d_ref[...] = -ADAM_LR * (m_hat / (jnp.sqrt(v_hat) + ADAM_EPS) + ADAM_WD * w_ref[...])
        nm_ref[...] = mn
        nv_ref[...] = vn

    spec = pl.BlockSpec((tr, cols), lambda i: (i, 0))
    return pl.pallas_call(
        body, name=name, grid=(rows // tr,),
        in_specs=[spec] * 4, out_specs=[spec] * 4,
        out_shape=[jax.ShapeDtypeStruct((rows, cols), F32)] * 4,
        compiler_params=_params(("parallel",)),
    )(w, g, m, v)


def _as2d(a):
    if a.ndim == 1:
        return a.reshape(1, -1)
    return a.reshape(-1, a.shape[-1])


def _rope_tables(tp, pad):
    pos = jnp.arange(tp, dtype=F32) - pad
    inv_freq = ROPE_BASE ** (-jnp.arange(0, QK_ROPE, 2, dtype=F32) / QK_ROPE)
    ang = pos[:, None] * inv_freq[None, :]
    cos, sin = jnp.cos(ang), jnp.sin(ang)
    zeros = jnp.zeros((tp, LANES - QK_ROPE), F32)
    return jnp.concatenate([cos, cos, zeros], axis=1), jnp.concatenate([-sin, sin, zeros], axis=1)


def _matrix_grad(name, g, heads):
    if name in ('b_w_rg', 'b_w_ig'):
        groups, blk, cols = g.shape
        return _Grad(name, g.reshape(groups * blk, cols), 'gate', (groups // 2) * (blk // N_CHIPS), cols, groups)
    rows, cols = g.shape
    if name in ('a_w_in', 'a_w_out', 'b_w_out'):
        return _Grad(name, g, 'rows', rows // (2 * N_CHIPS), cols)
    if name == 'a_w_uq' and heads % N_CHIPS != 0:
        g = g.reshape(rows, heads, HEAD_W)[:, :, :QK_NOPE + QK_ROPE].reshape(rows, -1)
        cols = g.shape[1]
        g = jnp.moveaxis(g.reshape(rows, N_CHIPS, cols // N_CHIPS), 1, 0)
        return _Grad(name, g, 'lead', rows // 2, cols // N_CHIPS)
    return _Grad(name, g, 'cols', rows // 2, cols // N_CHIPS)


def _kernel_form(name, w):
    return w[0] if name in ('b_w_rg', 'b_w_ig', 'b_conv_w') else _as2d(w)


def _local_grads(x, target, wt, heads, c_idx, mid_names, mid_gather, late_names, late_gather):
    wt = dict(wt)
    seq, d = x.shape
    n_meta = wt['meta_tokens'].shape[0]
    t = seq + n_meta
    pad = (-t) % Q_BLOCK
    tp = t + pad
    x0 = pad + n_meta
    tm = _row_tile(tp)
    ql = wt['a_q_norm_g'].shape[1]
    kvl = wt['a_kv_norm_g'].shape[1]
    mla_w = heads * V_HEAD

    cos_t, sin_t = _rope_tables(tp, pad)

    w_in_a = wt['a_w_in']
    zrow = jnp.zeros((LANES - QK_ROPE, d), BF16)
    w_in_a = jnp.concatenate([w_in_a[:ql + kvl + QK_ROPE], zrow, w_in_a[ql + kvl + QK_ROPE:]], axis=0)
    c_kv, c_kr, c_gate = ql, ql + kvl, ql + kvl + LANES
    splits_a = [(0, c_kv), (c_kv, c_kr), (c_kr, c_gate), (c_gate, c_gate + mla_w)]

    h0, q_lat, kv_lat, kr_raw, gate_a, *mid_whole = _tokens_in_proj(
        "a_in_proj", x, wt['meta_tokens'], wt['a_norm_g'], w_in_a, splits_a, tm, pad, exchange=mid_gather)
    wt.update({n: _kernel_form(n, w) for n, w in zip(mid_names, mid_whole)})
    w_uq = wt['a_w_uq'].reshape(ql, heads, QK_NOPE + QK_ROPE)
    w_uq = jnp.pad(w_uq, ((0, 0), (0, 0), (0, HEAD_W - QK_NOPE - QK_ROPE))).reshape(ql, heads * HEAD_W)
    w_ukv = wt['a_w_ukv']
    q = _q_proj(q_lat, wt['a_q_norm_g'], w_uq, cos_t, sin_t, heads, tm)
    k, v = _kv_proj(kv_lat, wt['a_kv_norm_g'], w_ukv, kr_raw, cos_t, sin_t, heads, tm)
    attn, lse, *late_whole = _flash_fwd(q, k, v, heads, pad, tm, exchange=late_gather)
    wt.update({n: _kernel_form(n, w) for n, w in zip(late_names, late_whole)})
    lru_w = wt['b_conv_w'].shape[1]

    h1, u, gate_b = _out_proj_in_proj("a_out_b_in_proj", attn, gate_a, wt['a_w_out'], h0, wt['b_norm_g'], wt['b_w_in'],
                                      [(0, lru_w), (lru_w, 2 * lru_w)], tm)
    uc, r, ig, hs, decay, mult = _rglru_fwd(u, wt['b_conv_w'], wt['b_conv_b'], wt['b_w_rg'], wt['b_b_rg'],
                                            wt['b_w_ig'], wt['b_b_ig'], wt['b_lam'], pad, tm)

    dh2, loss, d_final_g, dhs, dgate_b, dw_out_b = _out_proj_loss(
        hs, gate_b, wt['b_w_out'], h1, wt['final_norm_g'], target, x0, tm)
    du, dconv_w, dconv_b, dw_rg, db_rg, dw_ig, db_ig, dlam = _rglru_bwd(
        dhs, hs, r, ig, uc, u, decay, mult, wt['b_conv_w'], wt['b_w_rg'], wt['b_w_ig'], wt['b_lam'], pad, tm)
    dh1, dw_in_b, dg_b = _norm_matmul_bwd("b_in_proj_bwd", h1, wt['b_norm_g'], wt['b_w_in'], [du, dgate_b], tm, resid=dh2)

    grads_b = [_matrix_grad(n, g, heads) for n, g in
               (('b_w_in', dw_in_b), ('b_w_rg', dw_rg), ('b_w_ig', dw_ig), ('b_w_out', dw_out_b))]
    dattn, dgate_a, dw_out_a, delta, *got = _gated_out_bwd(
        "a_out_proj_bwd", dh1, attn, gate_a, wt['a_w_out'], tm, delta_heads=heads,
        exchange=_halves_to_sibling("swap_b", grads_b))
    sums_b = [_chip_sum(gr, r, c_idx) for gr, r in zip(grads_b, got)]
    grad_out = _matrix_grad('a_w_out', dw_out_a, heads)
    dq, dk, dv, *landed = _flash_bwd(
        q, k, v, lse, delta, dattn, heads, pad, tm,
        exchange=_blocks_to_chips("chips_b", sums_b) + _halves_to_sibling("swap_out", [grad_out]))
    through = list(zip(grads_b, sums_b, landed[:len(grads_b)]))
    sum_out = _chip_sum(grad_out, landed[len(grads_b)], c_idx)

    def q_prologue(dy_refs, dy_s, ex_ref):
        (dq_ref,), cos_v, sin_v = dy_refs[:1], dy_refs[1][...], dy_refs[2][...]
        for h in range(heads):
            c0 = h * HEAD_W
            dy_s[:, c0:c0 + QK_NOPE] = dq_ref[:, c0:c0 + QK_NOPE].astype(BF16)
            dy_s[:, c0 + QK_NOPE:c0 + HEAD_W] = _unrope(dq_ref[:, c0 + QK_NOPE:c0 + HEAD_W], cos_v, sin_v).astype(BF16)

    dq_lat, dw_uq, dg_q, from_chips_out = _norm_matmul_bwd(
        "a_q_proj_bwd", q_lat, wt['a_q_norm_g'], w_uq, [dq, cos_t, sin_t], tm, prologue=q_prologue,
        exchange=_blocks_to_chips("chips_out", [sum_out]))
    through.append((grad_out, sum_out, from_chips_out))
    grad_uq = _matrix_grad('a_w_uq', dw_uq, heads)

    def kv_prologue(dy_refs, dy_s, ex_ref):
        dk_ref, dv_ref = dy_refs[:2]
        cos_v, sin_v = dy_refs[2][...], dy_refs[3][...]
        dkr = jnp.zeros((dk_ref.shape[0], LANES), F32)
        for h in range(heads):
            c0 = h * (QK_NOPE + V_HEAD)
            dy_s[:, c0:c0 + QK_NOPE] = dk_ref[:, h * HEAD_W:h * HEAD_W + QK_NOPE].astype(BF16)
            dy_s[:, c0 + QK_NOPE:c0 + QK_NOPE + V_HEAD] = dv_ref[:, h * V_HEAD:(h + 1) * V_HEAD].astype(BF16)
            dkr = dkr + dk_ref[:, h * HEAD_W + QK_NOPE:(h + 1) * HEAD_W]
        ex_ref[...] = _unrope(dkr, cos_v, sin_v)

    dkv_lat, dw_ukv, dg_kv, dkr_raw, got_uq = _norm_matmul_bwd(
        "a_kv_proj_bwd", kv_lat, wt['a_kv_norm_g'], w_ukv, [dk, dv, cos_t, sin_t], tm,
        prologue=kv_prologue, extra_out=(LANES, F32), exchange=_halves_to_sibling("swap_uq", [grad_uq]))
    sum_uq = _chip_sum(grad_uq, got_uq, c_idx)
    grad_ukv = _matrix_grad('a_w_ukv', dw_ukv, heads)

    dh0, dw_in_a, dg_a, from_chips_uq, got_ukv = _norm_matmul_bwd(
        "a_in_proj_bwd", h0, wt['a_norm_g'], w_in_a, [dq_lat, dkv_lat, dkr_raw, dgate_a], tm, resid=dh1, transposed=True,
        exchange=_blocks_to_chips("chips_uq", [sum_uq]) + _halves_to_sibling("swap_ukv", [grad_ukv]))
    through.append((grad_uq, sum_uq, from_chips_uq))
    swapped = [(grad_ukv, _chip_sum(grad_ukv, got_ukv, c_idx))]

    dw_in_a = jnp.concatenate([dw_in_a[:c_kr + QK_ROPE], dw_in_a[c_gate:]], axis=0)
    grads = {
        'meta_tokens': dh0[pad:x0], 'a_norm_g': dg_a, 'a_w_in': dw_in_a, 'a_q_norm_g': dg_q, 'a_kv_norm_g': dg_kv,
        'a_w_uq': dw_uq, 'a_w_ukv': dw_ukv, 'a_w_out': dw_out_a, 'b_norm_g': dg_b, 'b_w_in': dw_in_b,
        'b_conv_w': dconv_w, 'b_conv_b': dconv_b, 'b_w_rg': dw_rg, 'b_b_rg': db_rg, 'b_w_ig': dw_ig,
        'b_b_ig': db_ig, 'b_lam': dlam, 'b_w_out': dw_out_b, 'final_norm_g': d_final_g,
    }
    return loss, dh0[x0:], grads, through, swapped


def _chip_major(whole, local_shape, axis):
    if axis is None:
        return jnp.broadcast_to(whole.reshape(1, -1), (N_CHIPS, whole.size))
    shape = list(local_shape)
    g = whole.reshape(shape[:axis] + [N_CHIPS, shape[axis]] + shape[axis + 1:])
    return jnp.moveaxis(g, axis, 0).reshape(N_CHIPS, -1)


def kernel(x, meta_tokens, a_norm_g, a_w_in, a_q_norm_g, a_kv_norm_g, a_w_uq, a_w_ukv, a_w_out, b_norm_g, b_w_in, b_conv_w, b_conv_b, b_w_rg, b_b_rg, b_w_ig, b_b_ig, b_lam, b_w_out, final_norm_g, loss_target, m_meta_tokens, m_a_norm_g, m_a_w_in, m_a_q_norm_g, m_a_kv_norm_g, m_a_w_uq, m_a_w_ukv, m_a_w_out, m_b_norm_g, m_b_w_in, m_b_conv_w, m_b_conv_b, m_b_w_rg, m_b_b_rg, m_b_w_ig, m_b_b_ig, m_b_lam, m_b_w_out, m_final_norm_g, v_meta_tokens, v_a_norm_g, v_a_w_in, v_a_q_norm_g, v_a_kv_norm_g, v_a_w_uq, v_a_w_ukv, v_a_w_out, v_b_norm_g, v_b_w_in, v_b_conv_w, v_b_conv_b, v_b_w_rg, v_b_b_rg, v_b_w_ig, v_b_b_ig, v_b_lam, v_b_w_out, v_final_norm_g):
    local_w = dict(zip(WEIGHTS, (meta_tokens, a_norm_g, a_w_in, a_q_norm_g, a_kv_norm_g, a_w_uq, a_w_ukv, a_w_out,
                                 b_norm_g, b_w_in, b_conv_w, b_conv_b, b_w_rg, b_b_rg, b_w_ig, b_b_ig, b_lam,
                                 b_w_out, final_norm_g)))
    local_m = dict(zip(WEIGHTS, (m_meta_tokens, m_a_norm_g, m_a_w_in, m_a_q_norm_g, m_a_kv_norm_g, m_a_w_uq,
                                 m_a_w_ukv, m_a_w_out, m_b_norm_g, m_b_w_in, m_b_conv_w, m_b_conv_b, m_b_w_rg,
                                 m_b_b_rg, m_b_w_ig, m_b_b_ig, m_b_lam, m_b_w_out, m_final_norm_g)))
    local_v = dict(zip(WEIGHTS, (v_meta_tokens, v_a_norm_g, v_a_w_in, v_a_q_norm_g, v_a_kv_norm_g, v_a_w_uq,
                                 v_a_w_ukv, v_a_w_out, v_b_norm_g, v_b_w_in, v_b_conv_w, v_b_conv_b, v_b_w_rg,
                                 v_b_b_rg, v_b_w_ig, v_b_b_ig, v_b_lam, v_b_w_out, v_final_norm_g)))
    matrices = ('a_w_in', 'a_w_uq', 'a_w_ukv', 'a_w_out', 'b_w_in', 'b_w_rg', 'b_w_ig', 'b_w_out')
    heads = a_w_uq.shape[-1] * N_CHIPS // (QK_NOPE + QK_ROPE)

    def transposed(a):
        return jnp.swapaxes(a, 1, 2)

    split, small, mid, late = [], [], [], []
    for n in WEIGHTS:
        if SHARD_AXIS[n] is None:
            continue
        if n.startswith('b_') or n == 'a_w_out':
            late.append((n, local_w[n].astype(BF16) if n in matrices else local_w[n], SHARD_AXIS[n]))
        elif n == 'a_w_in':
            split.append((n, transposed(local_w[n]).astype(BF16), 1, 2))
        elif n in matrices:
            mid.append((n, local_w[n].astype(BF16), SHARD_AXIS[n]))
        else:
            small.append((n, local_w[n], SHARD_AXIS[n]))
    gathered = _gather_weights([s[1:] for s in split], [s[1:] for s in small])
    whole = dict(zip([s[0] for s in split + small], gathered))
    mid_names, late_names = [s[0] for s in mid], [s[0] for s in late]
    wt = {n: _kernel_form(n, whole.get(n, local_w[n])) for n in WEIGHTS if n not in mid_names + late_names}

    c_idx = lax.axis_index("c").astype(jnp.int32).reshape(1)
    me_idx = (2 * lax.axis_index("x") + lax.axis_index("y")).astype(jnp.int32).reshape(1)
    loss, grad_x, grads, through, swapped = _local_grads(
        x[0], loss_target[0], wt, heads, c_idx,
        mid_names, _gather_whole("gather_weights_a", [s[1:] for s in mid]),
        late_names, _gather_whole("gather_weights_b", [s[1:] for s in late]))

    ext_uq = heads % N_CHIPS == 0
    started = [gr.name for gr, *_ in through + swapped]
    last = [_matrix_grad(n, grads[n], heads) for n in matrices if n not in started]
    rest = [n for n in WEIGHTS if n not in matrices]
    pieces = [_chip_major(grads[n], local_w[n].shape, SHARD_AXIS[n]) for n in rest]
    pieces.append(jnp.broadcast_to(loss[0:1, 0:1], (N_CHIPS, 1)))
    length = sum(p.shape[1] for p in pieces)
    unit = 2 * SUBLANES * 1024
    padded = -(-length // unit) * unit
    flat = jnp.concatenate(pieces + [jnp.zeros((N_CHIPS, padded - length), F32)], axis=1)
    last.append(_Grad('small', flat.reshape(N_CHIPS, padded // 1024, 1024), 'lead', padded // 2048, 1024))

    got = _halves_to_sibling("grads_to_sibling", last).run()
    swapped = swapped + [(gr, _chip_sum(gr, r, c_idx, F32 if gr.name == 'small' else BF16))
                         for gr, r in zip(last, got)]
    from_chips = _blocks_to_chips("grads_to_chips", [p for _, p in swapped]).run()
    through = through + [(gr, p, r) for (gr, p), r in zip(swapped, from_chips)]
    halves = [_sum_chips(gr.name, p, r, me_idx, c_idx) for gr, p, r in through]
    summed = dict(zip([gr.name for gr, _, _ in through], _share_with_sibling(halves)))
    if ext_uq:
        g = summed['a_w_uq']
        summed['a_w_uq'] = g.reshape(g.shape[0], -1, HEAD_W)[:, :, :QK_NOPE + QK_ROPE]
    total = summed['small'].reshape(-1)

    out_g, out_d, out_m, out_v = [], [], [], []
    off = 0
    for n in WEIGHTS:
        shape = local_w[n].shape
        view = transposed if n == 'a_w_in' else (lambda a: a)
        if n in matrices:
            g = summed[n].reshape(view(local_w[n]).shape)
        else:
            size = 1
            for s in shape:
                size *= s
            g = total[off:off + size].reshape(shape)
            off += size
        results = _adamw("adamw_" + n, _as2d(view(local_w[n])), _as2d(g), _as2d(view(local_m[n])), _as2d(view(local_v[n])))
        for out, r in zip((out_g, out_d, out_m, out_v), results):
            out.append(view(r.reshape(view(local_w[n]).shape)))

    return (total[off], grad_x[None], *out_g, *out_d, *out_m, *out_v)
```

```python
import functools

import jax
import jax.numpy as jnp
from jax import lax
from jax.experimental import pallas as pl
from jax.experimental.pallas import tpu as pltpu

F32 = jnp.float32
BF16 = jnp.bfloat16
MESH = pl.DeviceIdType.MESH

RMS_EPS = 1e-6
QK_NOPE = 128
QK_ROPE = 64
V_HEAD = 128
HEAD_W = 256
ROPE_BASE = 10000.0
Q_BLOCK = 128
MASK_VALUE = -1e30
CONV_WIDTH = 4
LRU_C = 8.0
N_CHIPS = 4

ADAM_LR = 0.001
ADAM_B1 = 0.9
ADAM_B2 = 0.999
ADAM_EPS = 1e-08
ADAM_WD = 0.01
ADAM_STEP = 10

VMEM_LIMIT_V7X = 56 * 1024 * 1024
LANES = 128
SUBLANES = 8

WEIGHTS = ['meta_tokens', 'a_norm_g', 'a_w_in', 'a_q_norm_g', 'a_kv_norm_g', 'a_w_uq', 'a_w_ukv',
           'a_w_out', 'b_norm_g', 'b_w_in', 'b_conv_w', 'b_conv_b', 'b_w_rg', 'b_b_rg', 'b_w_ig',
           'b_b_ig', 'b_lam', 'b_w_out', 'final_norm_g']
SHARD_AXIS = {'meta_tokens': 1, 'a_norm_g': None, 'a_w_in': 2, 'a_q_norm_g': None, 'a_kv_norm_g': None,
              'a_w_uq': 2, 'a_w_ukv': 2, 'a_w_out': 1, 'b_norm_g': 1, 'b_w_in': 2, 'b_conv_w': 2,
              'b_conv_b': 1, 'b_w_rg': 2, 'b_b_rg': 1, 'b_w_ig': 2, 'b_b_ig': 1, 'b_lam': 1,
              'b_w_out': 1, 'final_norm_g': None}


def _params(sem=None):
    return pltpu.CompilerParams(dimension_semantics=sem, vmem_limit_bytes=VMEM_LIMIT_V7X)


def _row_tile(tp):
    return 384 if (tp % 384 == 0 and tp >= 1152) else 128


def _sigmoid(x):
    return 1.0 / (1.0 + jnp.exp(-x))


def _rms(x):
    return lax.rsqrt(jnp.mean(x * x, axis=-1, keepdims=True) + RMS_EPS)


def _swap32(x):
    lane = lax.broadcasted_iota(jnp.int32, x.shape, 1)
    return jnp.where(lane < 32, pltpu.roll(x, 96, 1), pltpu.roll(x, 32, 1))


def _rope(x, cos_t, sin_t):
    return x * cos_t + _swap32(x) * sin_t


def _unrope(d, cos_t, sin_t):
    lane = lax.broadcasted_iota(jnp.int32, d.shape, 1)
    return jnp.where(lane < QK_ROPE, d * cos_t + _swap32(d * sin_t), 0.0)


def _dot(a, b):
    return jnp.dot(a, b, preferred_element_type=F32)


def _dot_nt(a, b):
    return lax.dot_general(a, b, (((1,), (1,)), ((), ())), preferred_element_type=F32)


def _dot_tn(a, b):
    return lax.dot_general(a, b, (((0,), (0,)), ((), ())), preferred_element_type=F32)


def _tokens_in_proj(name, tokens, meta, g, wt, splits, tm, pad, exchange=None):
    seq, kin = tokens.shape
    x0 = pad + meta.shape[0]
    tp = x0 + seq
    n = wt.shape[0]
    nt = tp // tm
    assert x0 <= tm and x0 % SUBLANES == 0 and pad % SUBLANES == 0 and nt >= 2

    def compute(x_hbm, meta_ref, g_ref, w_ref, h_ref, *rest):
        outs, (tile_s, sems) = rest[:len(splits)], rest[len(splits):]
        i = pl.program_id(0)

        def first_rows():
            return pltpu.make_async_copy(x_hbm.at[pl.ds(0, tm - x0), :], tile_s.at[0, pl.ds(x0, tm - x0), :], sems.at[0])

        def rows_of(t):
            return pltpu.make_async_copy(x_hbm.at[pl.ds(pl.multiple_of(t * tm - x0, SUBLANES), tm), :],
                                         tile_s.at[t % 2], sems.at[t % 2])

        @pl.when(i == 0)
        def _():
            tile_s[0, 0:pad, :] = jnp.zeros((pad, kin), F32)
            tile_s[0, pad:x0, :] = meta_ref[...]
            if x0 < tm:
                first_rows().start()
                first_rows().wait()

        pl.when(i > 0)(lambda: rows_of(i).wait())
        pl.when(i + 1 < nt)(lambda: rows_of(i + 1).start())

        xv = tile_s[i % 2]
        h_ref[...] = xv
        nrm = ((xv * _rms(xv)) * g_ref[...]).astype(BF16)
        y = _dot_nt(nrm, w_ref[...])
        for o_ref, (c0, c1) in zip(outs, splits):
            o_ref[...] = y[:, c0:c1]

    in_specs = [pl.BlockSpec(memory_space=pl.ANY),
                pl.BlockSpec(meta.shape, lambda i: (0, 0)),
                pl.BlockSpec((1, kin), lambda i: (0, 0)),
                pl.BlockSpec((n, kin), lambda i: (0, 0))]
    out_specs = [pl.BlockSpec((tm, kin), lambda i: (i, 0))]
    out_specs += [pl.BlockSpec((tm, c1 - c0), lambda i: (i, 0)) for c0, c1 in splits]
    out_shape = [jax.ShapeDtypeStruct((tp, kin), F32)]
    out_shape += [jax.ShapeDtypeStruct((tp, c1 - c0), F32) for c0, c1 in splits]
    args, scratch = [tokens, meta, g, wt], [pltpu.VMEM((2, tm, kin), F32), pltpu.SemaphoreType.DMA((2,))]
    body = _with_exchange(exchange, len(in_specs), len(out_specs),
                          lambda: pl.program_id(0) == 0, lambda: pl.program_id(0) == nt - 1, compute)
    if exchange is not None:
        in_specs = in_specs + exchange.specs(exchange.n_in)
        out_specs = out_specs + exchange.specs(exchange.n_out)
        out_shape = out_shape + exchange.out_shapes
        args, scratch = args + exchange.arrays, scratch + exchange.scratch()
    return pl.pallas_call(
        body, name=name, grid=(nt,),
        in_specs=in_specs, out_specs=out_specs, out_shape=out_shape, scratch_shapes=scratch,
        compiler_params=_params(("arbitrary",)),
    )(*args)


def _q_proj(q_lat, g, w_uq, cos_t, sin_t, heads, tm):
    tp, kin = q_lat.shape
    n = heads * HEAD_W

    def body(x_ref, g_ref, w_ref, cos_ref, sin_ref, q_ref):
        xv = x_ref[...]
        nrm = ((xv * _rms(xv)) * g_ref[...]).astype(BF16)
        y = _dot(nrm, w_ref[...])
        cos_v, sin_v = cos_ref[...], sin_ref[...]
        for h in range(heads):
            c0 = h * HEAD_W
            q_ref[:, c0:c0 + QK_NOPE] = y[:, c0:c0 + QK_NOPE].astype(BF16)
            q_ref[:, c0 + QK_NOPE:c0 + HEAD_W] = _rope(y[:, c0 + QK_NOPE:c0 + HEAD_W], cos_v, sin_v).astype(BF16)

    return pl.pallas_call(
        body, name="a_q_proj", grid=(tp // tm,),
        in_specs=[pl.BlockSpec((tm, kin), lambda i: (i, 0)),
                  pl.BlockSpec((1, kin), lambda i: (0, 0)),
                  pl.BlockSpec((kin, n), lambda i: (0, 0)),
                  pl.BlockSpec((tm, LANES), lambda i: (i, 0)),
                  pl.BlockSpec((tm, LANES), lambda i: (i, 0))],
        out_specs=pl.BlockSpec((tm, n), lambda i: (i, 0)),
        out_shape=jax.ShapeDtypeStruct((tp, n), BF16),
        compiler_params=_params(("parallel",)),
    )(q_lat, g, w_uq, cos_t, sin_t)


def _kv_proj(kv_lat, g, w_ukv, k_rope_raw, cos_t, sin_t, heads, tm):
    tp, kin = kv_lat.shape
    n = heads * (QK_NOPE + V_HEAD)

    def body(x_ref, g_ref, w_ref, kr_ref, cos_ref, sin_ref, k_ref, v_ref):
        xv = x_ref[...]
        nrm = ((xv * _rms(xv)) * g_ref[...]).astype(BF16)
        y = _dot(nrm, w_ref[...])
        kr = _rope(kr_ref[...], cos_ref[...], sin_ref[...]).astype(BF16)
        for h in range(heads):
            c0 = h * (QK_NOPE + V_HEAD)
            k_ref[:, h * HEAD_W:h * HEAD_W + QK_NOPE] = y[:, c0:c0 + QK_NOPE].astype(BF16)
            k_ref[:, h * HEAD_W + QK_NOPE:(h + 1) * HEAD_W] = kr
            v_ref[:, h * V_HEAD:(h + 1) * V_HEAD] = y[:, c0 + QK_NOPE:c0 + QK_NOPE + V_HEAD].astype(BF16)

    return pl.pallas_call(
        body, name="a_kv_proj", grid=(tp // tm,),
        in_specs=[pl.BlockSpec((tm, kin), lambda i: (i, 0)),
                  pl.BlockSpec((1, kin), lambda i: (0, 0)),
                  pl.BlockSpec((kin, n), lambda i: (0, 0)),
                  pl.BlockSpec((tm, LANES), lambda i: (i, 0)),
                  pl.BlockSpec((tm, LANES), lambda i: (i, 0)),
                  pl.BlockSpec((tm, LANES), lambda i: (i, 0))],
        out_specs=[pl.BlockSpec((tm, heads * HEAD_W), lambda i: (i, 0)),
                   pl.BlockSpec((tm, heads * V_HEAD), lambda i: (i, 0))],
        out_shape=[jax.ShapeDtypeStruct((tp, heads * HEAD_W), BF16),
                   jax.ShapeDtypeStruct((tp, heads * V_HEAD), BF16)],
        compiler_params=_params(("parallel",)),
    )(kv_lat, g, w_ukv, k_rope_raw, cos_t, sin_t)


def _as_rows(col):
    rows = col.shape[0]
    return jnp.transpose(jnp.broadcast_to(col, (rows, LANES)))[0:SUBLANES, :]


def _attn_mask(row0, col0, rows, cols, pad):
    row = row0 + lax.broadcasted_iota(jnp.int32, (rows, cols), 0)
    col = col0 + lax.broadcasted_iota(jnp.int32, (rows, cols), 1)
    return (col <= row) & (col >= pad)


LOG2E = 1.4426950408889634
FLASH_FWD_TRIPS = ((4, 2), (2, 2), (1, 1))


def _flash_fwd(q, k, v, heads, pad, tq, exchange=None):
    tp = q.shape[0]
    nq = tp // tq
    c2 = (QK_NOPE + QK_ROPE) ** -0.5 * LOG2E

    def compute(q_ref, k_ref, v_ref, o_ref, lse_ref):
        i = pl.program_id(1)

        def make_step(masked, blocks, parts=1):
            keys = blocks * tq // parts

            def step(j, carry):
                m, l, acc = carry
                offs = [pl.multiple_of(j * tq + part * keys, tq) for part in range(parts)]
                scores = [_dot_nt(q_ref[...], k_ref[pl.ds(off, keys), :]) for off in offs]
                for off, s in zip(offs, scores):
                    s = s * c2
                    if masked:
                        s = jnp.where(_attn_mask(i * tq, j * tq, tq, keys, pad), s, MASK_VALUE)
                    m_new = jnp.maximum(m, jnp.max(s, axis=-1, keepdims=True))
                    p = jnp.exp2(s - m_new)
                    alpha = jnp.exp2(m - m_new)
                    l = alpha * l + jnp.sum(p, axis=-1, keepdims=True)
                    acc = alpha * acc + _dot(p.astype(BF16), v_ref[pl.ds(off, keys), :])
                    m = m_new
                return m, l, acc
            return step

        init = (jnp.full((tq, 1), MASK_VALUE, F32), jnp.zeros((tq, 1), F32), jnp.zeros((tq, V_HEAD), F32))
        carry = make_step(True, 1)(0, init)
        first = 1
        for blocks, parts in FLASH_FWD_TRIPS:
            trips = jnp.maximum(i - first, 0) // blocks
            step_n = make_step(False, blocks, parts)
            carry = lax.fori_loop(0, trips, lambda t, cr, f=first, b=blocks, s=step_n: s(f + b * t, cr), carry)
            first = first + blocks * trips
        m, l, acc = lax.fori_loop(jnp.maximum(i, 1), i + 1, make_step(True, 1), carry)
        o_ref[...] = acc / l
        lse_ref[...] = _as_rows(m + jnp.log(l) * LOG2E)

    in_specs = [pl.BlockSpec((tq, HEAD_W), lambda h, i: (i, h)),
                pl.BlockSpec((tp, HEAD_W), lambda h, i: (0, h)),
                pl.BlockSpec((tp, V_HEAD), lambda h, i: (0, h))]
    out_specs = [pl.BlockSpec((tq, V_HEAD), lambda h, i: (i, h)),
                 pl.BlockSpec((None, None, SUBLANES, tq), lambda h, i: (h, i, 0, 0))]
    out_shape = [jax.ShapeDtypeStruct((tp, heads * V_HEAD), F32),
                 jax.ShapeDtypeStruct((heads, nq, SUBLANES, tq), F32)]
    args, scratch = [q, k, v], []
    body = _with_exchange(exchange, len(in_specs), len(out_specs),
                          lambda: (pl.program_id(0) == 0) & (pl.program_id(1) == 0),
                          lambda: (pl.program_id(0) == heads - 1) & (pl.program_id(1) == nq - 1), compute)
    if exchange is not None:
        in_specs = in_specs + exchange.specs(exchange.n_in)
        out_specs = out_specs + exchange.specs(exchange.n_out)
        out_shape = out_shape + exchange.out_shapes
        args, scratch = args + exchange.arrays, exchange.scratch()
    return pl.pallas_call(
        body, name="a_flash_fwd", grid=(heads, nq),
        in_specs=in_specs, out_specs=out_specs, out_shape=out_shape, scratch_shapes=scratch,
        compiler_params=_params(("arbitrary", "arbitrary")),
    )(*args)


def _out_proj_in_proj(name, a, gate, w_out, resid, g, w_in, splits, tm):
    tp, wd = a.shape
    d = w_out.shape[1]
    n = w_in.shape[1]

    def body(a_ref, gate_ref, wo_ref, res_ref, g_ref, wi_ref, h_ref, *outs):
        gv = gate_ref[...]
        y = (a_ref[...] * (gv * _sigmoid(gv))).astype(BF16)
        h = res_ref[...] + _dot(y, wo_ref[...])
        h_ref[...] = h
        nrm = ((h * _rms(h)) * g_ref[...]).astype(BF16)
        z = _dot(nrm, wi_ref[...])
        for o_ref, (c0, c1) in zip(outs, splits):
            o_ref[...] = z[:, c0:c1]

    return pl.pallas_call(
        body, name=name, grid=(tp // tm,),
        in_specs=[pl.BlockSpec((tm, wd), lambda i: (i, 0)),
                  pl.BlockSpec((tm, wd), lambda i: (i, 0)),
                  pl.BlockSpec((wd, d), lambda i: (0, 0)),
                  pl.BlockSpec((tm, d), lambda i: (i, 0)),
                  pl.BlockSpec((1, d), lambda i: (0, 0)),
                  pl.BlockSpec((d, n), lambda i: (0, 0))],
        out_specs=[pl.BlockSpec((tm, d), lambda i: (i, 0))]
        + [pl.BlockSpec((tm, c1 - c0), lambda i: (i, 0)) for c0, c1 in splits],
        out_shape=[jax.ShapeDtypeStruct((tp, d), F32)]
        + [jax.ShapeDtypeStruct((tp, c1 - c0), F32) for c0, c1 in splits],
        compiler_params=_params(("parallel",)),
    )(a, gate, w_out, resid, g, w_in)


def _lru_decay(r, sp):
    log_a = -LRU_C * r * sp
    a = jnp.exp(log_a)
    e2 = a * a
    x2 = 2.0 * log_a
    series = x2 * (1.0 + x2 * (0.5 + x2 * (1.0 / 6.0)))
    em1 = jnp.where(x2 > -0.02, series, e2 - 1.0)
    return a, e2, jnp.sqrt(-em1)


def _softplus(x):
    return jnp.maximum(x, 0.0) + jnp.log1p(jnp.exp(-jnp.abs(x)))


def _rglru_fwd(u, conv_w, conv_b, w_rg, b_rg, w_ig, b_ig, lam, pad, tm):
    tp, w = u.shape
    groups, blk = w_rg.shape[0], w_rg.shape[1]

    def body(u_ref, cw_ref, cb_ref, wr_ref, br_ref, wi_ref, bi_ref, lam_ref,
             uc_ref, r_ref, ig_ref, hs_ref, a_s, mult_ref, uext, b_s, hc):
        i = pl.program_id(0)

        @pl.when(i == 0)
        def _():
            uext[0:SUBLANES, :] = jnp.zeros((SUBLANES, w), F32)
            hc[...] = jnp.zeros((SUBLANES, w), F32)

        uext[SUBLANES:SUBLANES + tm, :] = u_ref[...]
        cw = cw_ref[...]
        uc = cb_ref[...] + uext[pl.ds(SUBLANES - 3, tm), :] * cw[0:1, :]
        uc = uc + uext[pl.ds(SUBLANES - 2, tm), :] * cw[1:2, :]
        uc = uc + uext[pl.ds(SUBLANES - 1, tm), :] * cw[2:3, :]
        uc = uc + uext[pl.ds(SUBLANES, tm), :] * cw[3:4, :]
        uc_ref[...] = uc
        uext[0:SUBLANES, :] = uext[tm:tm + SUBLANES, :]

        sp = _softplus(-lam_ref[...])
        for g in range(groups):
            sl = slice(g * blk, (g + 1) * blk)
            ucg = uc_ref[:, sl]
            ucb = ucg.astype(BF16)
            r = _sigmoid(_dot(ucb, wr_ref[g]) + br_ref[:, sl])
            ig = _sigmoid(_dot(ucb, wi_ref[g]) + bi_ref[:, sl])
            r_ref[:, sl] = r
            ig_ref[:, sl] = ig
            a, _, mult = _lru_decay(r, sp[:, sl])
            a_s[:, sl] = a
            mult_ref[:, sl] = mult
            b_s[:, sl] = mult * (ig * ucg)

        @pl.when(i == 0)
        def _():
            row = lax.broadcasted_iota(jnp.int32, (Q_BLOCK, w), 0)
            start = ig_ref[0:Q_BLOCK, :] * uc_ref[0:Q_BLOCK, :]
            b_s[0:Q_BLOCK, :] = jnp.where(row < pad, 0.0, jnp.where(row == pad, start, b_s[0:Q_BLOCK, :]))
            mult_ref[0:Q_BLOCK, :] = jnp.where(row == pad, 1.0, mult_ref[0:Q_BLOCK, :])

        row8 = lax.broadcasted_iota(jnp.int32, (SUBLANES, w), 0)

        def group(gi, h_in):
            off = pl.multiple_of(gi * SUBLANES, SUBLANES)
            av = a_s[pl.ds(off, SUBLANES), :]
            bv = b_s[pl.ds(off, SUBLANES), :]
            for k in (1, 2, 4):
                keep = row8 >= k
                bv = jnp.where(keep, av * pltpu.roll(bv, k, 0) + bv, bv)
                av = jnp.where(keep, av * pltpu.roll(av, k, 0), av)
            hv = av * h_in + bv
            hs_ref[pl.ds(off, SUBLANES), :] = hv
            return jnp.broadcast_to(hv[SUBLANES - 1:SUBLANES, :], (SUBLANES, w))

        hc[...] = lax.fori_loop(0, tm // SUBLANES, group, hc[...])

    row_spec = pl.BlockSpec((tm, w), lambda i: (i, 0))
    vec_spec = pl.BlockSpec((1, w), lambda i: (0, 0))
    mat_spec = pl.BlockSpec((groups, blk, blk), lambda i: (0, 0, 0))
    return pl.pallas_call(
        body, name="b_rglru_fwd", grid=(tp // tm,),
        in_specs=[row_spec, pl.BlockSpec((CONV_WIDTH, w), lambda i: (0, 0)), vec_spec,
                  mat_spec, vec_spec, mat_spec, vec_spec, vec_spec],
        out_specs=[row_spec] * 6,
        out_shape=[jax.ShapeDtypeStruct((tp, w), F32)] * 6,
        scratch_shapes=[pltpu.VMEM((tm + SUBLANES, w), F32), pltpu.VMEM((tm, w), F32),
                        pltpu.VMEM((SUBLANES, w), F32)],
        compiler_params=_params(("arbitrary",)),
    )(u, conv_w, conv_b, w_rg, b_rg, w_ig, b_ig, lam)


def _out_proj_loss(a, gate, w, resid, g, target, x0, tm):
    tp, wd = a.shape
    d = w.shape[1]
    assert x0 % Q_BLOCK == 0 and tm % Q_BLOCK == 0 and target.shape[0] == tp - x0
    lead = x0 // Q_BLOCK
    per = tm // Q_BLOCK

    def body(a_ref, gate_ref, w_ref, res_ref, g_ref, *rest):
        t_refs, (dh_ref, loss_ref, dg_ref, da_ref, dgate_ref, dw_ref) = rest[:per], rest[per:]
        i = pl.program_id(0)

        @pl.when(i == 0)
        def _():
            loss_ref[...] = jnp.zeros_like(loss_ref)
            dg_ref[...] = jnp.zeros_like(dg_ref)
            dw_ref[...] = jnp.zeros_like(dw_ref)

        gate_v = gate_ref[...]
        av = a_ref[...]
        sg = _sigmoid(gate_v)
        silu = gate_v * sg
        y = (av * silu).astype(BF16)
        h = res_ref[...] + _dot(y, w_ref[...])
        gv = g_ref[...]
        for b in range(per):
            rows = slice(b * Q_BLOCK, (b + 1) * Q_BLOCK)
            xv = h[rows, :]
            r = _rms(xv)
            xh = xv * r
            err = jnp.where(i * per + b >= lead, xh * gv - t_refs[b][...], 0.0)
            loss_ref[...] += 0.5 * jnp.sum(jnp.mean(err * err, axis=-1, keepdims=True))
            dy = err / d
            dg_ref[...] += jnp.sum(dy * xh, axis=0, keepdims=True)
            dxh = dy * gv
            dh_ref[rows, :] = r * (dxh - xh * jnp.mean(dxh * xh, axis=-1, keepdims=True))

        dob = dh_ref[...].astype(BF16)
        dyv = _dot_nt(dob, w_ref[...])
        da_ref[...] = dyv * silu
        dgate_ref[...] = (dyv * av * (sg * (1.0 + gate_v * (1.0 - sg)))).astype(BF16)
        dw_ref[...] += _dot_tn(y, dob)

    def piece(b):
        return pl.BlockSpec((Q_BLOCK, d), lambda i: (jnp.maximum(i * per + b - lead, 0), 0))

    return pl.pallas_call(
        body, name="b_out_proj_loss", grid=(tp // tm,),
        in_specs=[pl.BlockSpec((tm, wd), lambda i: (i, 0)),
                  pl.BlockSpec((tm, wd), lambda i: (i, 0)),
                  pl.BlockSpec((wd, d), lambda i: (0, 0)),
                  pl.BlockSpec((tm, d), lambda i: (i, 0)),
                  pl.BlockSpec((1, d), lambda i: (0, 0))] + [piece(b) for b in range(per)],
        out_specs=[pl.BlockSpec((tm, d), lambda i: (i, 0)),
                   pl.BlockSpec((SUBLANES, LANES), lambda i: (0, 0)),
                   pl.BlockSpec((1, d), lambda i: (0, 0)),
                   pl.BlockSpec((tm, wd), lambda i: (i, 0)),
                   pl.BlockSpec((tm, wd), lambda i: (i, 0)),
                   pl.BlockSpec((wd, d), lambda i: (0, 0))],
        out_shape=[jax.ShapeDtypeStruct((tp, d), F32),
                   jax.ShapeDtypeStruct((SUBLANES, LANES), F32),
                   jax.ShapeDtypeStruct((1, d), F32),
                   jax.ShapeDtypeStruct((tp, wd), F32),
                   jax.ShapeDtypeStruct((tp, wd), BF16),
                   jax.ShapeDtypeStruct((wd, d), F32)],
        compiler_params=_params(("arbitrary",)),
    )(a, gate, w, resid, g, *([target] * per))


def _with_exchange(exchange, n_in, n_out, first, last, compute):
    if exchange is None:
        return compute
    ex_in, ex_out = exchange.n_in, exchange.n_out

    def body(*refs):
        own_in, their_in = refs[:n_in], refs[n_in:n_in + ex_in]
        pos = n_in + ex_in
        own_out, their_out = refs[pos:pos + n_out], refs[pos + n_out:pos + n_out + ex_out]
        rest = refs[pos + n_out + ex_out:]
        own_scratch, sems = rest[:len(rest) - 2], rest[len(rest) - 2:]

        @pl.when(first())
        def _():
            exchange.start(their_in, their_out, sems)

        compute(*own_in, *own_out, *own_scratch)

        @pl.when(last())
        def _():
            exchange.finish(their_in, their_out, sems)

    return body


def _gated_out_bwd(name, dout, a, gate, w, tm, delta_heads=0, exchange=None):
    tp, wd = a.shape
    d = w.shape[1]
    nt = tp // tm

    def compute(do_ref, a_ref, gate_ref, w_ref, da_ref, dgate_ref, dw_ref, *delta_ref):
        @pl.when(pl.program_id(0) == 0)
        def _():
            dw_ref[...] = jnp.zeros_like(dw_ref)

        gv = gate_ref[...]
        av = a_ref[...]
        sg = _sigmoid(gv)
        silu = gv * sg
        dob = do_ref[...].astype(BF16)
        dy = _dot_nt(dob, w_ref[...])
        da = dy * silu
        da_ref[...] = da.astype(BF16)
        dgate_ref[...] = (dy * av * (sg * (1.0 + gv * (1.0 - sg)))).astype(BF16)
        dw_ref[...] += _dot_tn((av * silu).astype(BF16), dob)
        for h in range(delta_heads):
            sl = slice(h * V_HEAD, (h + 1) * V_HEAD)
            delta_ref[0][h] = _as_rows(jnp.sum(da[:, sl] * av[:, sl], axis=-1, keepdims=True))

    out_specs = [pl.BlockSpec((tm, wd), lambda i: (i, 0)),
                 pl.BlockSpec((tm, wd), lambda i: (i, 0)),
                 pl.BlockSpec((wd, d), lambda i: (0, 0))]
    out_shape = [jax.ShapeDtypeStruct((tp, wd), BF16),
                 jax.ShapeDtypeStruct((tp, wd), BF16),
                 jax.ShapeDtypeStruct((wd, d), F32)]
    if delta_heads:
        out_specs.append(pl.BlockSpec((delta_heads, None, SUBLANES, tm), lambda i: (0, i, 0, 0)))
        out_shape.append(jax.ShapeDtypeStruct((delta_heads, tp // tm, SUBLANES, tm), F32))
    in_specs = [pl.BlockSpec((tm, d), lambda i: (i, 0)),
                pl.BlockSpec((tm, wd), lambda i: (i, 0)),
                pl.BlockSpec((tm, wd), lambda i: (i, 0)),
                pl.BlockSpec((wd, d), lambda i: (0, 0))]
    args, scratch = [dout, a, gate, w], []
    body = _with_exchange(exchange, len(in_specs), len(out_specs),
                          lambda: pl.program_id(0) == 0, lambda: pl.program_id(0) == nt - 1, compute)
    if exchange is not None:
        in_specs = in_specs + exchange.specs(exchange.n_in)
        out_specs = out_specs + exchange.specs(exchange.n_out)
        out_shape = out_shape + exchange.out_shapes
        args, scratch = args + exchange.arrays, exchange.scratch()
    return pl.pallas_call(
        body, name=name, grid=(nt,),
        in_specs=in_specs, out_specs=out_specs, out_shape=out_shape, scratch_shapes=scratch,
        compiler_params=_params(("arbitrary",)),
    )(*args)


def _rglru_bwd(dhs, hs, r, ig, uc, u, a, mult, conv_w, w_rg, w_ig, lam, pad, tm):
    tp, w = u.shape
    groups, blk = w_rg.shape[0], w_rg.shape[1]
    nt = tp // tm
    per8 = tm // SUBLANES

    def body(dhs_ref, hs_ref, hprev_ref, r_ref, ig_ref, uc_ref, u_ref, uprev_ref, a_ref, mult_ref,
             cw_ref, wr_ref, wi_ref, lam_ref,
             du_ref, dcw_ref, dcb_ref, dwr_ref, dbr_ref, dwi_ref, dbi_ref, dlam_ref,
             aext, c_s, g_s, hext, uext, ducext, gc):
        step = pl.program_id(0)
        ti = nt - 1 - step

        @pl.when(step == 0)
        def _():
            for ref in (dcw_ref, dcb_ref, dwr_ref, dbr_ref, dwi_ref, dbi_ref, dlam_ref):
                ref[...] = jnp.zeros_like(ref)
            aext[tm:tm + SUBLANES, :] = jnp.zeros((SUBLANES, w), F32)
            ducext[tm:tm + SUBLANES, :] = jnp.zeros((SUBLANES, w), F32)
            gc[...] = jnp.zeros((SUBLANES, w), F32)

        lam_v = lam_ref[...]
        sp = _softplus(-lam_v)
        row = ti * tm + lax.broadcasted_iota(jnp.int32, (tm, w), 0)

        rv = r_ref[...]
        a = a_ref[...]
        mult = mult_ref[...]
        aext[0:tm, :] = a
        c_s[...] = aext[pl.ds(1, tm), :]
        row8 = lax.broadcasted_iota(jnp.int32, (SUBLANES, w), 0)

        def group(gi, g_in):
            off = pl.multiple_of((per8 - 1 - gi) * SUBLANES, SUBLANES)
            cv = c_s[pl.ds(off, SUBLANES), :]
            dv = dhs_ref[pl.ds(off, SUBLANES), :]
            for k in (1, 2, 4):
                keep = row8 < SUBLANES - k
                dv = jnp.where(keep, cv * pltpu.roll(dv, SUBLANES - k, 0) + dv, dv)
                cv = jnp.where(keep, cv * pltpu.roll(cv, SUBLANES - k, 0), cv)
            gv = cv * g_in + dv
            g_s[pl.ds(off, SUBLANES), :] = gv
            return jnp.broadcast_to(gv[0:1, :], (SUBLANES, w))

        gc[...] = lax.fori_loop(0, per8, group, gc[...])
        aext[tm:tm + SUBLANES, :] = aext[0:SUBLANES, :]

        gsc = jnp.where(row < pad, 0.0, g_s[...])
        hext[0:SUBLANES, :] = hprev_ref[...]
        hext[SUBLANES:SUBLANES + tm, :] = hs_ref[...]
        hprev = jnp.where(row == 0, 0.0, hext[pl.ds(SUBLANES - 1, tm), :])
        igv = ig_ref[...]
        ucv = uc_ref[...]
        first = row == pad
        dmult = gsc * (igv * ucv)
        dig = gsc * mult * ucv
        duc = gsc * mult * igv
        dlog_a = (gsc * hprev) * a + jnp.where(first, 0.0, dmult * (-(a * a) / mult))
        dlam_ref[...] += jnp.sum(dlog_a * rv, axis=0, keepdims=True) * (LRU_C * _sigmoid(-lam_v))
        dpre_r = dlog_a * (-LRU_C * sp) * (rv * (1.0 - rv))
        dpre_i = dig * (igv * (1.0 - igv))
        dbr_ref[...] += jnp.sum(dpre_r, axis=0, keepdims=True)
        dbi_ref[...] += jnp.sum(dpre_i, axis=0, keepdims=True)
        for g in range(groups):
            sl = slice(g * blk, (g + 1) * blk)
            ucb = ucv[:, sl].astype(BF16)
            drb = dpre_r[:, sl].astype(BF16)
            dib = dpre_i[:, sl].astype(BF16)
            dwr_ref[g] += _dot_tn(ucb, drb)
            dwi_ref[g] += _dot_tn(ucb, dib)
            ducext[0:tm, sl] = duc[:, sl] + _dot_nt(drb, wr_ref[g]) + _dot_nt(dib, wi_ref[g])

        ducv = ducext[0:tm, :]
        cw = cw_ref[...]
        dcb_ref[...] += jnp.sum(ducv, axis=0, keepdims=True)
        uext[0:SUBLANES, :] = jnp.where(ti == 0, 0.0, uprev_ref[...])
        uext[SUBLANES:SUBLANES + tm, :] = u_ref[...]
        for j in range(CONV_WIDTH):
            ush = uext[pl.ds(SUBLANES - (CONV_WIDTH - 1 - j), tm), :]
            dcw_ref[j:j + 1, :] += jnp.sum(ducv * ush, axis=0, keepdims=True)
        du = ducv * cw[3:4, :]
        for k in range(1, CONV_WIDTH):
            du = du + ducext[pl.ds(k, tm), :] * cw[3 - k:4 - k, :]
        du_ref[...] = du.astype(BF16)
        ducext[tm:tm + SUBLANES, :] = ducext[0:SUBLANES, :]

    rev = lambda s: (nt - 1 - s, 0)
    halo = lambda s: (jnp.maximum((nt - 1 - s) * per8 - 1, 0), 0)
    row_spec = pl.BlockSpec((tm, w), rev)
    halo_spec = pl.BlockSpec((SUBLANES, w), halo)
    vec_spec = pl.BlockSpec((1, w), lambda s: (0, 0))
    mat_spec = pl.BlockSpec((groups, blk, blk), lambda s: (0, 0, 0))
    cw_spec = pl.BlockSpec((CONV_WIDTH, w), lambda s: (0, 0))
    return pl.pallas_call(
        body, name="b_rglru_bwd", grid=(nt,),
        in_specs=[row_spec, row_spec, halo_spec, row_spec, row_spec, row_spec, row_spec, halo_spec, row_spec, row_spec,
                  cw_spec, mat_spec, mat_spec, vec_spec],
        out_specs=[row_spec, cw_spec, vec_spec, mat_spec, vec_spec, mat_spec, vec_spec, vec_spec],
        out_shape=[jax.ShapeDtypeStruct((tp, w), BF16), jax.ShapeDtypeStruct((CONV_WIDTH, w), F32),
                   jax.ShapeDtypeStruct((1, w), F32), jax.ShapeDtypeStruct((groups, blk, blk), F32),
                   jax.ShapeDtypeStruct((1, w), F32), jax.ShapeDtypeStruct((groups, blk, blk), F32),
                   jax.ShapeDtypeStruct((1, w), F32), jax.ShapeDtypeStruct((1, w), F32)],
        scratch_shapes=[pltpu.VMEM((tm + SUBLANES, w), F32), pltpu.VMEM((tm, w), F32), pltpu.VMEM((tm, w), F32),
                        pltpu.VMEM((tm + SUBLANES, w), F32), pltpu.VMEM((tm + SUBLANES, w), F32),
                        pltpu.VMEM((tm + SUBLANES, w), F32), pltpu.VMEM((SUBLANES, w), F32)],
        compiler_params=_params(("arbitrary",)),
    )(dhs, hs, hs, r, ig, uc, u, u, a, mult, conv_w, w_rg, w_ig, lam)


def _norm_matmul_bwd(name, x, g, w, dys, tm, resid=None, prologue=None, extra_out=None, exchange=None,
                     transposed=False, dx_dtype=F32):
    tp, kin = x.shape
    n = w.shape[0] if transposed else w.shape[1]
    w_shape = (n, kin) if transposed else (kin, n)
    nt = tp // tm
    n_dy = len(dys)
    has_res = resid is not None
    has_extra = extra_out is not None

    def compute(*refs):
        x_ref, g_ref, w_ref = refs[:3]
        dy_refs = refs[3:3 + n_dy]
        pos = 3 + n_dy
        res_ref = refs[pos] if has_res else None
        pos += int(has_res)
        dx_ref, dw_ref, dg_ref = refs[pos:pos + 3]
        pos += 3
        ex_ref = refs[pos] if has_extra else None
        pos += int(has_extra)
        dy_s = refs[pos]

        @pl.when(pl.program_id(0) == 0)
        def _():
            dw_ref[...] = jnp.zeros_like(dw_ref)
            dg_ref[...] = jnp.zeros_like(dg_ref)

        if prologue is None:
            c0 = 0
            for ref in dy_refs:
                dy_s[:, c0:c0 + ref.shape[1]] = ref[...].astype(BF16)
                c0 += ref.shape[1]
        else:
            prologue(dy_refs, dy_s, ex_ref)

        xv = x_ref[...]
        gv = g_ref[...]
        r = _rms(xv)
        xh = xv * r
        dyb = dy_s[...]
        nb = (xh * gv).astype(BF16)
        if transposed:
            dn = _dot(dyb, w_ref[...])
            dw_ref[...] += _dot_tn(dyb, nb)
        else:
            dn = _dot_nt(dyb, w_ref[...])
            dw_ref[...] += _dot_tn(nb, dyb)
        dg_ref[...] += jnp.sum(dn * xh, axis=0, keepdims=True)
        dxh = dn * gv
        dx = r * (dxh - xh * jnp.mean(dxh * xh, axis=-1, keepdims=True))
        if has_res:
            dx = dx + res_ref[...]
        dx_ref[...] = dx.astype(dx_dtype)

    row = lambda width: pl.BlockSpec((tm, width), lambda i: (i, 0))
    in_specs = [row(kin), pl.BlockSpec((1, kin), lambda i: (0, 0)), pl.BlockSpec(w_shape, lambda i: (0, 0))]
    in_specs += [row(a.shape[1]) for a in dys]
    args = [x, g, w, *dys]
    if has_res:
        in_specs.append(row(kin))
        args.append(resid)
    out_specs = [row(kin), pl.BlockSpec(w_shape, lambda i: (0, 0)), pl.BlockSpec((1, kin), lambda i: (0, 0))]
    out_shape = [jax.ShapeDtypeStruct((tp, kin), dx_dtype), jax.ShapeDtypeStruct(w_shape, F32),
                 jax.ShapeDtypeStruct((1, kin), F32)]
    if has_extra:
        out_specs.append(row(extra_out[0]))
        out_shape.append(jax.ShapeDtypeStruct((tp, extra_out[0]), extra_out[1]))
    scratch = [pltpu.VMEM((tm, n), BF16)]
    body = _with_exchange(exchange, len(in_specs), len(out_specs),
                          lambda: pl.program_id(0) == 0, lambda: pl.program_id(0) == nt - 1, compute)
    if exchange is not None:
        in_specs = in_specs + exchange.specs(exchange.n_in)
        out_specs = out_specs + exchange.specs(exchange.n_out)
        out_shape = out_shape + exchange.out_shapes
        args, scratch = args + exchange.arrays, scratch + exchange.scratch()
    return pl.pallas_call(
        body, name=name, grid=(nt,),
        in_specs=in_specs, out_specs=out_specs, out_shape=out_shape, scratch_shapes=scratch,
        compiler_params=_params(("arbitrary",)),
    )(*args)


def _flash_bwd(q, k, v, lse, delta, do, heads, pad, tq, exchange=None):
    tp = q.shape[0]
    nq = tp // tq
    scale = (QK_NOPE + QK_ROPE) ** -0.5
    c2 = scale * LOG2E

    def compute(q_ref, k_ref, v_ref, lse_ref, delta_ref, do_ref, dq_ref, dk_ref, dv_ref):
        j = pl.program_id(1)

        @pl.when(j == 0)
        def _():
            dq_ref[...] = jnp.zeros_like(dq_ref)

        kv = k_ref[...]
        vv = v_ref[...]

        def rows_of(ref, i, blocks):
            parts = [ref[i + b][0:1, :] for b in range(blocks)]
            return parts[0] if blocks == 1 else jnp.concatenate(parts, axis=1)

        def make_step(masked, blocks):
            def step(i, carry):
                off = pl.multiple_of(i * tq, tq)
                qv = q_ref[pl.ds(off, blocks * tq), :]
                dob = do_ref[pl.ds(off, blocks * tq), :]
                p = jnp.exp2(_dot_nt(kv, qv) * c2 - rows_of(lse_ref, i, blocks))
                if masked:
                    key = j * tq + lax.broadcasted_iota(jnp.int32, (tq, tq), 0)
                    qry = j * tq + lax.broadcasted_iota(jnp.int32, (tq, tq), 1)
                    first = jnp.where((key <= qry) & (key >= pad), p[:, :tq], 0.0)
                    p = first if blocks == 1 else jnp.concatenate([first, p[:, tq:]], axis=1)
                dv_ref[...] += _dot(p.astype(BF16), dob)
                dp = _dot_nt(vv, dob)
                ds = (p * (dp - rows_of(delta_ref, i, blocks)) * scale).astype(BF16)
                dk_ref[...] += _dot(ds, qv)
                dq_ref[pl.ds(off, blocks * tq), :] += _dot_tn(ds, kv)
                return carry
            return step

        dk_ref[...] = jnp.zeros_like(dk_ref)
        dv_ref[...] = jnp.zeros_like(dv_ref)
        odd = (nq - j) % 2
        lax.fori_loop(0, odd, lambda t, cr: make_step(True, 1)(j, cr), 0)
        lax.fori_loop(0, 1 - odd, lambda t, cr: make_step(True, 2)(j, cr), 0)
        start = j + 2 - odd
        for blocks in (4, 2):
            trips = (nq - start) // blocks
            step_n = make_step(False, blocks)
            lax.fori_loop(0, trips, lambda t, cr, s=start, b=blocks, f=step_n: f(s + b * t, cr), 0)
            start = start + blocks * trips

    in_specs = [pl.BlockSpec((tp, HEAD_W), lambda h, j: (0, h)),
                pl.BlockSpec((tq, HEAD_W), lambda h, j: (j, h)),
                pl.BlockSpec((tq, V_HEAD), lambda h, j: (j, h)),
                pl.BlockSpec((None, nq, SUBLANES, tq), lambda h, j: (h, 0, 0, 0)),
                pl.BlockSpec((None, nq, SUBLANES, tq), lambda h, j: (h, 0, 0, 0)),
                pl.BlockSpec((tp, V_HEAD), lambda h, j: (0, h))]
    out_specs = [pl.BlockSpec((tp, HEAD_W), lambda h, j: (0, h)),
                 pl.BlockSpec((tq, HEAD_W), lambda h, j: (j, h)),
                 pl.BlockSpec((tq, V_HEAD), lambda h, j: (j, h))]
    out_shape = [jax.ShapeDtypeStruct((tp, heads * HEAD_W), F32),
                 jax.ShapeDtypeStruct((tp, heads * HEAD_W), F32),
                 jax.ShapeDtypeStruct((tp, heads * V_HEAD), F32)]
    args, scratch = [q, k, v, lse, delta, do], []
    body = _with_exchange(exchange, len(in_specs), len(out_specs),
                          lambda: (pl.program_id(0) == 0) & (pl.program_id(1) == 0),
                          lambda: (pl.program_id(0) == heads - 1) & (pl.program_id(1) == nq - 1), compute)
    if exchange is not None:
        in_specs = in_specs + exchange.specs(exchange.n_in)
        out_specs = out_specs + exchange.specs(exchange.n_out)
        out_shape = out_shape + exchange.out_shapes
        args, scratch = args + exchange.arrays, exchange.scratch()
    return pl.pallas_call(
        body, name="a_flash_bwd", grid=(heads, nq),
        in_specs=in_specs, out_specs=out_specs, out_shape=out_shape, scratch_shapes=scratch,
        compiler_params=_params(("arbitrary", "arbitrary")),
    )(*args)


def _position():
    return lax.axis_index("x"), lax.axis_index("y"), lax.axis_index("c")


def _other_chips(x, y):
    return [(1 - x, y), (x, 1 - y), (1 - x, 1 - y)]


def _block(ref, shard_axis, n, k, split_axis=None, m=None, h=None):
    idx = []
    for a in range(len(ref.shape)):
        start = 0
        size = None
        if a == shard_axis:
            start, size = k * n, n
        if a == split_axis:
            size = (n if a == shard_axis else m) // 2
            start = start + h * size
        idx.append(slice(None) if size is None else pl.ds(start, size))
    return ref.at[tuple(idx)]


def _gather_weights(split, whole_small):
    ns, nw = len(split), len(whole_small)
    n = ns + nw
    arrs = [s[0] for s in split] + [s[0] for s in whole_small]
    axes = [s[1] for s in split] + [s[1] for s in whole_small]

    def body(*refs):
        ins, outs = refs[:n], refs[n:2 * n]
        ici_send, ici_recv, d2d_send, d2d_recv, sib_send, sib_recv = refs[2 * n:]
        x, y, c = _position()
        me = 2 * x + y
        others = _other_chips(x, y)
        sent, local = [], []

        def remote(src, dst, sems, idx, to):
            return pltpu.make_async_remote_copy(src_ref=src, dst_ref=dst, send_sem=sems[0].at[idx],
                                                recv_sem=sems[1].at[idx], device_id=to, device_id_type=MESH)

        for a in range(n):
            width = ins[a].shape[axes[a]]
            mine = remote(ins[a], _block(outs[a], axes[a], width, me), (sib_send, sib_recv), a, (x, y, 1 - c))
            mine.start()
            local.append(mine)
            for j, (px, py) in enumerate(others):
                if a < ns:
                    sx = split[a][2]
                    src = _block(ins[a], None, None, None, sx, ins[a].shape[sx], c)
                    dst = _block(outs[a], axes[a], width, me, sx, outs[a].shape[sx], c)
                else:
                    src, dst = ins[a], _block(outs[a], axes[a], width, me)
                cp = remote(src, dst, (ici_send, ici_recv), 3 * a + j, (px, py, c))
                cp.start()
                sent.append(cp)
        for a in range(ns):
            width = ins[a].shape[axes[a]]
            sx = split[a][2]
            for j, (px, py) in enumerate(others):
                theirs = _block(outs[a], axes[a], width, 2 * px + py, sx, outs[a].shape[sx], c)
                remote(theirs, theirs, (ici_send, ici_recv), 3 * a + j, (px, py, c)).wait_recv()
                fwd = remote(theirs, theirs, (d2d_send, d2d_recv), 3 * a + j, (x, y, 1 - c))
                fwd.start()
                sent.append(fwd)
        for a in range(ns, n):
            width = ins[a].shape[axes[a]]
            for j, (px, py) in enumerate(others):
                theirs = _block(outs[a], axes[a], width, 2 * px + py)
                remote(theirs, theirs, (ici_send, ici_recv), 3 * a + j, (px, py, c)).wait_recv()
        for a in range(ns):
            width = ins[a].shape[axes[a]]
            sx = split[a][2]
            for j, (px, py) in enumerate(others):
                from_sibling = _block(outs[a], axes[a], width, 2 * px + py, sx, outs[a].shape[sx], 1 - c)
                remote(from_sibling, from_sibling, (d2d_send, d2d_recv), 3 * a + j, (x, y, 1 - c)).wait_recv()
        for cp in sent:
            cp.wait_send()
        for cp in local:
            cp.wait()

    def whole_shape(a, axis):
        shape = list(a.shape)
        shape[axis] *= N_CHIPS
        return jax.ShapeDtypeStruct(tuple(shape), a.dtype)

    any_spec = pl.BlockSpec(memory_space=pl.ANY)
    return pl.pallas_call(
        body, name="gather_weights",
        in_specs=[any_spec] * n, out_specs=[any_spec] * n,
        out_shape=[whole_shape(a, ax) for a, ax in zip(arrs, axes)],
        scratch_shapes=[pltpu.SemaphoreType.DMA((3 * n,)), pltpu.SemaphoreType.DMA((3 * n,)),
                        pltpu.SemaphoreType.DMA((3 * ns,)), pltpu.SemaphoreType.DMA((3 * ns,)),
                        pltpu.SemaphoreType.DMA((n,)), pltpu.SemaphoreType.DMA((n,))],
        compiler_params=pltpu.CompilerParams(has_side_effects=True),
    )(*arrs)


class _Grad:
    def __init__(self, name, g, kind, rh, cols, groups=None):
        self.name, self.g, self.kind, self.rh, self.cols, self.groups = name, g, kind, rh, cols, groups
        if kind == 'rows':
            self.tr = rh
        elif kind == 'gate':
            self.tr = rh // (groups // 2)
        else:
            self.tr = rh if rh <= 512 else 256
        self.nb = rh // self.tr

    def pieces(self, ref, k, h):
        rh, cols = self.rh, self.cols
        if self.kind == 'cols':
            return [(ref.at[pl.ds(h * rh, rh), pl.ds(k * cols, cols)], 0, rh)]
        if self.kind == 'rows':
            return [(ref.at[pl.ds((2 * k + h) * rh, rh), :], 0, rh)]
        if self.kind == 'lead':
            return [(ref.at[k, pl.ds(h * rh, rh), :], 0, rh)]
        per = self.groups // 2
        return [(ref.at[pl.ds((((h * per + gi) * N_CHIPS) + k) * self.tr, self.tr), :], gi * self.tr, self.tr)
                for gi in range(per)]

    def block_spec(self):
        tr, nb, cols = self.tr, self.nb, self.cols
        if self.kind == 'cols':
            return pl.BlockSpec((tr, cols), lambda k, i, c: (c[0] * nb + i, k))
        if self.kind == 'rows':
            return pl.BlockSpec((tr, cols), lambda k, i, c: (2 * k + c[0], 0))
        if self.kind == 'lead':
            return pl.BlockSpec((None, tr, cols), lambda k, i, c: (k, c[0] * nb + i, 0))
        return pl.BlockSpec((tr, cols), lambda k, i, c: ((c[0] * nb + i) * N_CHIPS + k, 0))


class _Exchange:
    def __init__(self, name, arrays, out_shapes, n_copies, copies):
        self.name, self.arrays, self.out_shapes, self.n_copies, self.copies = name, arrays, out_shapes, n_copies, copies
        self.n_in, self.n_out = len(arrays), len(out_shapes)

    def specs(self, n):
        return [pl.BlockSpec(memory_space=pl.ANY)] * n

    def scratch(self):
        return [pltpu.SemaphoreType.DMA((self.n_copies,)), pltpu.SemaphoreType.DMA((self.n_copies,))]

    def _descriptors(self, in_refs, out_refs, sems):
        return self.copies(in_refs, out_refs, sems[0], sems[1])

    def start(self, in_refs, out_refs, sems):
        for cp in self._descriptors(in_refs, out_refs, sems):
            cp.start()

    def finish(self, in_refs, out_refs, sems):
        for cp in self._descriptors(in_refs, out_refs, sems):
            cp.wait()

    def __add__(self, other):
        def copies(ins, outs, send_sems, recv_sems, base=0):
            return (self.copies(ins[:self.n_in], outs[:self.n_out], send_sems, recv_sems, base)
                    + other.copies(ins[self.n_in:], outs[self.n_out:], send_sems, recv_sems, base + self.n_copies))

        return _Exchange(self.name + "_" + other.name, self.arrays + other.arrays, self.out_shapes + other.out_shapes,
                         self.n_copies + other.n_copies, copies)

    def run(self):
        def body(*refs):
            ins, outs, sems = refs[:self.n_in], refs[self.n_in:self.n_in + self.n_out], refs[self.n_in + self.n_out:]
            self.start(ins, outs, sems)
            self.finish(ins, outs, sems)

        return pl.pallas_call(
            body, name=self.name,
            in_specs=self.specs(self.n_in), out_specs=self.specs(self.n_out), out_shape=self.out_shapes,
            scratch_shapes=self.scratch(),
            compiler_params=pltpu.CompilerParams(has_side_effects=True),
        )(*self.arrays)


def _gather_whole(name, shards):
    def copies(ins, outs, send_sems, recv_sems, base=0):
        x, y, c = _position()
        me = 2 * x + y
        made = []
        for a, (_, axis) in enumerate(shards):
            dst = _block(outs[a], axis, ins[a].shape[axis], me)
            for j, to in enumerate([(x, y, 1 - c)] + [(px, py, c) for px, py in _other_chips(x, y)]):
                idx = base + 4 * a + j
                made.append(pltpu.make_async_remote_copy(
                    src_ref=ins[a], dst_ref=dst, send_sem=send_sems.at[idx], recv_sem=recv_sems.at[idx],
                    device_id=to, device_id_type=MESH))
        return made

    def whole_shape(a, axis):
        shape = list(a.shape)
        shape[axis] *= N_CHIPS
        return jax.ShapeDtypeStruct(tuple(shape), a.dtype)

    return _Exchange(name, [s[0] for s in shards], [whole_shape(*s) for s in shards], 4 * len(shards), copies)


def _halves_to_sibling(name, grads):
    total = sum(len(gr.pieces(gr.g, 0, 0)) * N_CHIPS for gr in grads)

    def copies(ins, outs, send_sems, recv_sems, base=0):
        x, y, c = _position()
        made = []
        for gr, g_ref, got_ref in zip(grads, ins, outs):
            for k in range(N_CHIPS):
                for src, r0, nr in gr.pieces(g_ref, k, 1 - c):
                    idx = base + len(made)
                    made.append(pltpu.make_async_remote_copy(
                        src_ref=src, dst_ref=got_ref.at[k, pl.ds(r0, nr), :],
                        send_sem=send_sems.at[idx], recv_sem=recv_sems.at[idx],
                        device_id=(x, y, 1 - c), device_id_type=MESH))
        return made

    return _Exchange(name, [gr.g for gr in grads],
                     [jax.ShapeDtypeStruct((N_CHIPS, gr.rh, gr.cols), F32) for gr in grads], total, copies)


def _chip_sum(gr, got, c, wire=BF16):
    def body(c_ref, g_ref, got_ref, o_ref):
        o_ref[...] = (g_ref[...] + got_ref[...]).astype(wire)

    tile = pl.BlockSpec((None, gr.tr, gr.cols), lambda k, i, c_ref: (k, i, 0))
    return pl.pallas_call(
        body, name="chip_sum_" + gr.name,
        grid_spec=pltpu.PrefetchScalarGridSpec(
            num_scalar_prefetch=1, grid=(N_CHIPS, gr.nb),
            in_specs=[gr.block_spec(), tile], out_specs=tile),
        out_shape=jax.ShapeDtypeStruct((N_CHIPS, gr.rh, gr.cols), wire),
        compiler_params=_params(("parallel", "parallel")),
    )(c, gr.g, got)


def _blocks_to_chips(name, parts):
    n = len(parts)

    def copies(ins, outs, send_sems, recv_sems, base=0):
        x, y, c = _position()
        made = []
        for a in range(n):
            for j, (px, py) in enumerate(_other_chips(x, y)):
                idx = base + 3 * a + j
                made.append(pltpu.make_async_remote_copy(
                    src_ref=ins[a].at[2 * px + py], dst_ref=outs[a].at[j],
                    send_sem=send_sems.at[idx], recv_sem=recv_sems.at[idx],
                    device_id=(px, py, c), device_id_type=MESH))
        return made

    return _Exchange(name, parts, [jax.ShapeDtypeStruct((3,) + p.shape[1:], p.dtype) for p in parts], 3 * n, copies)


def _sum_chips(name, part, got, me, c):
    nk, rh, cols = part.shape
    tr = rh if rh <= 512 else 256
    nb = rh // tr

    def body(me_ref, c_ref, own_ref, *rest):
        got_refs, o_ref = rest[:nk], rest[nk]
        own = own_ref[...].astype(F32)
        acc = None
        for k in range(nk):
            term = jnp.where(me_ref[0] == k, own, got_refs[k][...].astype(F32))
            acc = term if acc is None else acc + term
        o_ref[...] = acc

    def got_map(k):
        def index(i, me_ref, c_ref):
            xor = jnp.bitwise_xor(me_ref[0], k)
            slot = jnp.where(xor == 1, 1, jnp.where(xor == 3, 2, 0))
            return (slot, i, 0)
        return index

    return pl.pallas_call(
        body, name="sum_" + name,
        grid_spec=pltpu.PrefetchScalarGridSpec(
            num_scalar_prefetch=2, grid=(nb,),
            in_specs=[pl.BlockSpec((None, tr, cols), lambda i, me_ref, c_ref: (me_ref[0], i, 0))]
            + [pl.BlockSpec((None, tr, cols), got_map(k)) for k in range(nk)],
            out_specs=pl.BlockSpec((tr, cols), lambda i, me_ref, c_ref: (c_ref[0] * nb + i, 0))),
        out_shape=jax.ShapeDtypeStruct((2 * rh, cols), F32),
        compiler_params=_params(("parallel",)),
    )(me, c, part, *([got] * nk))


def _share_with_sibling(halves):
    n = len(halves)

    def body(*refs):
        outs = refs[n:2 * n]
        send_sems, recv_sems = refs[2 * n:]
        x, y, c = _position()
        copies = []
        for a in range(n):
            rh = outs[a].shape[0] // 2
            mine = outs[a].at[pl.ds(c * rh, rh), :]
            cp = pltpu.make_async_remote_copy(
                src_ref=mine, dst_ref=mine, send_sem=send_sems.at[a], recv_sem=recv_sems.at[a],
                device_id=(x, y, 1 - c), device_id_type=MESH)
            cp.start()
            copies.append(cp)
        for cp in copies:
            cp.wait()

    any_spec = pl.BlockSpec(memory_space=pl.ANY)
    return pl.pallas_call(
        body, name="grads_share",
        in_specs=[any_spec] * n, out_specs=[any_spec] * n,
        out_shape=[jax.ShapeDtypeStruct(h.shape, h.dtype) for h in halves],
        input_output_aliases={a: a for a in range(n)},
        scratch_shapes=[pltpu.SemaphoreType.DMA((n,)), pltpu.SemaphoreType.DMA((n,))],
        compiler_params=pltpu.CompilerParams(has_side_effects=True),
    )(*halves)


def _adamw(name, w, g, m, v):
    rows, cols = w.shape
    tr = 256 if rows % 256 == 0 else rows

    def body(w_ref, g_ref, m_ref, v_ref, g_out_ref, d_ref, nm_ref, nv_ref):
        gv = g_ref[...]
        g_out_ref[...] = gv
        mn = ADAM_B1 * m_ref[...] + (1.0 - ADAM_B1) * gv
        vn = ADAM_B2 * v_ref[...] + (1.0 - ADAM_B2) * (gv * gv)
        m_hat = mn / (1.0 - ADAM_B1 ** ADAM_STEP)
        v_hat = vn / (1.0 - ADAM_B2 ** ADAM_STEP)
        d_ref[...] = -ADAM_LR * (m_hat / (jnp.sqrt(v_hat) + ADAM_EPS) + ADAM_WD * w_ref[...])
        nm_ref[...] = mn
        nv_ref[...] = vn

    spec = pl.BlockSpec((tr, cols), lambda i: (i, 0))
    return pl.pallas_call(
        body, name=name, grid=(rows // tr,),
        in_specs=[spec] * 4, out_specs=[spec] * 4,
        out_shape=[jax.ShapeDtypeStruct((rows, cols), F32)] * 4,
        compiler_params=_params(("parallel",)),
    )(w, g, m, v)


def _as2d(a):
    if a.ndim == 1:
        return a.reshape(1, -1)
    return a.reshape(-1, a.shape[-1])


def _rope_tables(tp, pad):
    pos = jnp.arange(tp, dtype=F32) - pad
    inv_freq = ROPE_BASE ** (-jnp.arange(0, QK_ROPE, 2, dtype=F32) / QK_ROPE)
    ang = pos[:, None] * inv_freq[None, :]
    cos, sin = jnp.cos(ang), jnp.sin(ang)
    zeros = jnp.zeros((tp, LANES - QK_ROPE), F32)
    return jnp.concatenate([cos, cos, zeros], axis=1), jnp.concatenate([-sin, sin, zeros], axis=1)


def _matrix_grad(name, g, heads):
    if name in ('b_w_rg', 'b_w_ig'):
        groups, blk, cols = g.shape
        return _Grad(name, g.reshape(groups * blk, cols), 'gate', (groups // 2) * (blk // N_CHIPS), cols, groups)
    rows, cols = g.shape
    if name in ('a_w_in', 'a_w_out', 'b_w_out'):
        return _Grad(name, g, 'rows', rows // (2 * N_CHIPS), cols)
    if name == 'a_w_uq' and heads % N_CHIPS != 0:
        g = g.reshape(rows, heads, HEAD_W)[:, :, :QK_NOPE + QK_ROPE].reshape(rows, -1)
        cols = g.shape[1]
        g = jnp.moveaxis(g.reshape(rows, N_CHIPS, cols // N_CHIPS), 1, 0)
        return _Grad(name, g, 'lead', rows // 2, cols // N_CHIPS)
    return _Grad(name, g, 'cols', rows // 2, cols // N_CHIPS)


def _kernel_form(name, w):
    return w[0] if name in ('b_w_rg', 'b_w_ig', 'b_conv_w') else _as2d(w)


def _local_grads(x, target, wt, heads, c_idx, mid_names, mid_gather, late_names, late_gather):
    wt = dict(wt)
    seq, d = x.shape
    n_meta = wt['meta_tokens'].shape[0]
    t = seq + n_meta
    pad = (-t) % Q_BLOCK
    tp = t + pad
    x0 = pad + n_meta
    tm = _row_tile(tp)
    ql = wt['a_q_norm_g'].shape[1]
    kvl = wt['a_kv_norm_g'].shape[1]
    mla_w = heads * V_HEAD

    cos_t, sin_t = _rope_tables(tp, pad)

    w_in_a = wt['a_w_in']
    zrow = jnp.zeros((LANES - QK_ROPE, d), BF16)
    w_in_a = jnp.concatenate([w_in_a[:ql + kvl + QK_ROPE], zrow, w_in_a[ql + kvl + QK_ROPE:]], axis=0)
    c_kv, c_kr, c_gate = ql, ql + kvl, ql + kvl + LANES
    splits_a = [(0, c_kv), (c_kv, c_kr), (c_kr, c_gate), (c_gate, c_gate + mla_w)]

    h0, q_lat, kv_lat, kr_raw, gate_a, *mid_whole = _tokens_in_proj(
        "a_in_proj", x, wt['meta_tokens'], wt['a_norm_g'], w_in_a, splits_a, tm, pad, exchange=mid_gather)
    wt.update({n: _kernel_form(n, w) for n, w in zip(mid_names, mid_whole)})
    w_uq = wt['a_w_uq'].reshape(ql, heads, QK_NOPE + QK_ROPE)
    w_uq = jnp.pad(w_uq, ((0, 0), (0, 0), (0, HEAD_W - QK_NOPE - QK_ROPE))).reshape(ql, heads * HEAD_W)
    w_ukv = wt['a_w_ukv']
    q = _q_proj(q_lat, wt['a_q_norm_g'], w_uq, cos_t, sin_t, heads, tm)
    k, v = _kv_proj(kv_lat, wt['a_kv_norm_g'], w_ukv, kr_raw, cos_t, sin_t, heads, tm)
    attn, lse, *late_whole = _flash_fwd(q, k, v, heads, pad, tm, exchange=late_gather)
    wt.update({n: _kernel_form(n, w) for n, w in zip(late_names, late_whole)})
    lru_w = wt['b_conv_w'].shape[1]

    h1, u, gate_b = _out_proj_in_proj("a_out_b_in_proj", attn, gate_a, wt['a_w_out'], h0, wt['b_norm_g'], wt['b_w_in'],
                                      [(0, lru_w), (lru_w, 2 * lru_w)], tm)
    uc, r, ig, hs, decay, mult = _rglru_fwd(u, wt['b_conv_w'], wt['b_conv_b'], wt['b_w_rg'], wt['b_b_rg'],
                                            wt['b_w_ig'], wt['b_b_ig'], wt['b_lam'], pad, tm)

    dh2, loss, d_final_g, dhs, dgate_b, dw_out_b = _out_proj_loss(
        hs, gate_b, wt['b_w_out'], h1, wt['final_norm_g'], target, x0, tm)
    du, dconv_w, dconv_b, dw_rg, db_rg, dw_ig, db_ig, dlam = _rglru_bwd(
        dhs, hs, r, ig, uc, u, decay, mult, wt['b_conv_w'], wt['b_w_rg'], wt['b_w_ig'], wt['b_lam'], pad, tm)
    dh1, dw_in_b, dg_b = _norm_matmul_bwd("b_in_proj_bwd", h1, wt['b_norm_g'], wt['b_w_in'], [du, dgate_b], tm, resid=dh2)

    grads_b = [_matrix_grad(n, g, heads) for n, g in
               (('b_w_in', dw_in_b), ('b_w_rg', dw_rg), ('b_w_ig', dw_ig), ('b_w_out', dw_out_b))]
    dattn, dgate_a, dw_out_a, delta, *got = _gated_out_bwd(
        "a_out_proj_bwd", dh1, attn, gate_a, wt['a_w_out'], tm, delta_heads=heads,
        exchange=_halves_to_sibling("swap_b", grads_b))
    sums_b = [_chip_sum(gr, r, c_idx) for gr, r in zip(grads_b, got)]
    grad_out = _matrix_grad('a_w_out', dw_out_a, heads)
    dq, dk, dv, *landed = _flash_bwd(
        q, k, v, lse, delta, dattn, heads, pad, tm,
        exchange=_blocks_to_chips("chips_b", sums_b) + _halves_to_sibling("swap_out", [grad_out]))
    through = list(zip(grads_b, sums_b, landed[:len(grads_b)]))
    sum_out = _chip_sum(grad_out, landed[len(grads_b)], c_idx)

    def q_prologue(dy_refs, dy_s, ex_ref):
        (dq_ref,), cos_v, sin_v = dy_refs[:1], dy_refs[1][...], dy_refs[2][...]
        for h in range(heads):
            c0 = h * HEAD_W
            dy_s[:, c0:c0 + QK_NOPE] = dq_ref[:, c0:c0 + QK_NOPE].astype(BF16)
            dy_s[:, c0 + QK_NOPE:c0 + HEAD_W] = _unrope(dq_ref[:, c0 + QK_NOPE:c0 + HEAD_W], cos_v, sin_v).astype(BF16)

    dq_lat, dw_uq, dg_q, from_chips_out = _norm_matmul_bwd(
        "a_q_proj_bwd", q_lat, wt['a_q_norm_g'], w_uq, [dq, cos_t, sin_t], tm, prologue=q_prologue, dx_dtype=BF16,
        exchange=_blocks_to_chips("chips_out", [sum_out]))
    through.append((grad_out, sum_out, from_chips_out))
    grad_uq = _matrix_grad('a_w_uq', dw_uq, heads)

    def kv_prologue(dy_refs, dy_s, ex_ref):
        dk_ref, dv_ref = dy_refs[:2]
        cos_v, sin_v = dy_refs[2][...], dy_refs[3][...]
        dkr = jnp.zeros((dk_ref.shape[0], LANES), F32)
        for h in range(heads):
            c0 = h * (QK_NOPE + V_HEAD)
            dy_s[:, c0:c0 + QK_NOPE] = dk_ref[:, h * HEAD_W:h * HEAD_W + QK_NOPE].astype(BF16)
            dy_s[:, c0 + QK_NOPE:c0 + QK_NOPE + V_HEAD] = dv_ref[:, h * V_HEAD:(h + 1) * V_HEAD].astype(BF16)
            dkr = dkr + dk_ref[:, h * HEAD_W + QK_NOPE:(h + 1) * HEAD_W]
        ex_ref[...] = _unrope(dkr, cos_v, sin_v).astype(ex_ref.dtype)

    dkv_lat, dw_ukv, dg_kv, dkr_raw, got_uq = _norm_matmul_bwd(
        "a_kv_proj_bwd", kv_lat, wt['a_kv_norm_g'], w_ukv, [dk, dv, cos_t, sin_t], tm,
        prologue=kv_prologue, extra_out=(LANES, BF16), dx_dtype=BF16, exchange=_halves_to_sibling("swap_uq", [grad_uq]))
    sum_uq = _chip_sum(grad_uq, got_uq, c_idx)
    grad_ukv = _matrix_grad('a_w_ukv', dw_ukv, heads)

    dh0, dw_in_a, dg_a, from_chips_uq, got_ukv = _norm_matmul_bwd(
        "a_in_proj_bwd", h0, wt['a_norm_g'], w_in_a, [dq_lat, dkv_lat, dkr_raw, dgate_a], tm, resid=dh1, transposed=True,
        exchange=_blocks_to_chips("chips_uq", [sum_uq]) + _halves_to_sibling("swap_ukv", [grad_ukv]))
    through.append((grad_uq, sum_uq, from_chips_uq))
    swapped = [(grad_ukv, _chip_sum(grad_ukv, got_ukv, c_idx))]

    dw_in_a = jnp.concatenate([dw_in_a[:c_kr + QK_ROPE], dw_in_a[c_gate:]], axis=0)
    grads = {
        'meta_tokens': dh0[pad:x0], 'a_norm_g': dg_a, 'a_w_in': dw_in_a, 'a_q_norm_g': dg_q, 'a_kv_norm_g': dg_kv,
        'a_w_uq': dw_uq, 'a_w_ukv': dw_ukv, 'a_w_out': dw_out_a, 'b_norm_g': dg_b, 'b_w_in': dw_in_b,
        'b_conv_w': dconv_w, 'b_conv_b': dconv_b, 'b_w_rg': dw_rg, 'b_b_rg': db_rg, 'b_w_ig': dw_ig,
        'b_b_ig': db_ig, 'b_lam': dlam, 'b_w_out': dw_out_b, 'final_norm_g': d_final_g,
    }
    return loss, dh0[x0:], grads, through, swapped


def _chip_major(whole, local_shape, axis):
    if axis is None:
        return jnp.broadcast_to(whole.reshape(1, -1), (N_CHIPS, whole.size))
    shape = list(local_shape)
    g = whole.reshape(shape[:axis] + [N_CHIPS, shape[axis]] + shape[axis + 1:])
    return jnp.moveaxis(g, axis, 0).reshape(N_CHIPS, -1)


def kernel(x, meta_tokens, a_norm_g, a_w_in, a_q_norm_g, a_kv_norm_g, a_w_uq, a_w_ukv, a_w_out, b_norm_g, b_w_in, b_conv_w, b_conv_b, b_w_rg, b_b_rg, b_w_ig, b_b_ig, b_lam, b_w_out, final_norm_g, loss_target, m_meta_tokens, m_a_norm_g, m_a_w_in, m_a_q_norm_g, m_a_kv_norm_g, m_a_w_uq, m_a_w_ukv, m_a_w_out, m_b_norm_g, m_b_w_in, m_b_conv_w, m_b_conv_b, m_b_w_rg, m_b_b_rg, m_b_w_ig, m_b_b_ig, m_b_lam, m_b_w_out, m_final_norm_g, v_meta_tokens, v_a_norm_g, v_a_w_in, v_a_q_norm_g, v_a_kv_norm_g, v_a_w_uq, v_a_w_ukv, v_a_w_out, v_b_norm_g, v_b_w_in, v_b_conv_w, v_b_conv_b, v_b_w_rg, v_b_b_rg, v_b_w_ig, v_b_b_ig, v_b_lam, v_b_w_out, v_final_norm_g):
    local_w = dict(zip(WEIGHTS, (meta_tokens, a_norm_g, a_w_in, a_q_norm_g, a_kv_norm_g, a_w_uq, a_w_ukv, a_w_out,
                                 b_norm_g, b_w_in, b_conv_w, b_conv_b, b_w_rg, b_b_rg, b_w_ig, b_b_ig, b_lam,
                                 b_w_out, final_norm_g)))
    local_m = dict(zip(WEIGHTS, (m_meta_tokens, m_a_norm_g, m_a_w_in, m_a_q_norm_g, m_a_kv_norm_g, m_a_w_uq,
                                 m_a_w_ukv, m_a_w_out, m_b_norm_g, m_b_w_in, m_b_conv_w, m_b_conv_b, m_b_w_rg,
                                 m_b_b_rg, m_b_w_ig, m_b_b_ig, m_b_lam, m_b_w_out, m_final_norm_g)))
    local_v = dict(zip(WEIGHTS, (v_meta_tokens, v_a_norm_g, v_a_w_in, v_a_q_norm_g, v_a_kv_norm_g, v_a_w_uq,
                                 v_a_w_ukv, v_a_w_out, v_b_norm_g, v_b_w_in, v_b_conv_w, v_b_conv_b, v_b_w_rg,
                                 v_b_b_rg, v_b_w_ig, v_b_b_ig, v_b_lam, v_b_w_out, v_final_norm_g)))
    matrices = ('a_w_in', 'a_w_uq', 'a_w_ukv', 'a_w_out', 'b_w_in', 'b_w_rg', 'b_w_ig', 'b_w_out')
    heads = a_w_uq.shape[-1] * N_CHIPS // (QK_NOPE + QK_ROPE)

    def transposed(a):
        return jnp.swapaxes(a, 1, 2)

    split, small, mid, late = [], [], [], []
    for n in WEIGHTS:
        if SHARD_AXIS[n] is None:
            continue
        if n.startswith('b_') or n == 'a_w_out':
            late.append((n, local_w[n].astype(BF16) if n in matrices else local_w[n], SHARD_AXIS[n]))
        elif n == 'a_w_in':
            split.append((n, transposed(local_w[n]).astype(BF16), 1, 2))
        elif n in matrices:
            mid.append((n, local_w[n].astype(BF16), SHARD_AXIS[n]))
        else:
            small.append((n, local_w[n], SHARD_AXIS[n]))
    gathered = _gather_weights([s[1:] for s in split], [s[1:] for s in small])
    whole = dict(zip([s[0] for s in split + small], gathered))
    mid_names, late_names = [s[0] for s in mid], [s[0] for s in late]
    wt = {n: _kernel_form(n, whole.get(n, local_w[n])) for n in WEIGHTS if n not in mid_names + late_names}

    c_idx = lax.axis_index("c").astype(jnp.int32).reshape(1)
    me_idx = (2 * lax.axis_index("x") + lax.axis_index("y")).astype(jnp.int32).reshape(1)
    loss, grad_x, grads, through, swapped = _local_grads(
        x[0], loss_target[0], wt, heads, c_idx,
        mid_names, _gather_whole("gather_weights_a", [s[1:] for s in mid]),
        late_names, _gather_whole("gather_weights_b", [s[1:] for s in late]))

    ext_uq = heads % N_CHIPS == 0
    started = [gr.name for gr, *_ in through + swapped]
    last = [_matrix_grad(n, grads[n], heads) for n in matrices if n not in started]
    rest = [n for n in WEIGHTS if n not in matrices]
    pieces = [_chip_major(grads[n], local_w[n].shape, SHARD_AXIS[n]) for n in rest]
    pieces.append(jnp.broadcast_to(loss[0:1, 0:1], (N_CHIPS, 1)))
    length = sum(p.shape[1] for p in pieces)
    unit = 2 * SUBLANES * 1024
    padded = -(-length // unit) * unit
    flat = jnp.concatenate(pieces + [jnp.zeros((N_CHIPS, padded - length), F32)], axis=1)
    last.append(_Grad('small', flat.reshape(N_CHIPS, padded // 1024, 1024), 'lead', padded // 2048, 1024))

    got = _halves_to_sibling("grads_to_sibling", last).run()
    swapped = swapped + [(gr, _chip_sum(gr, r, c_idx, F32 if gr.name == 'small' else BF16))
                         for gr, r in zip(last, got)]
    from_chips = _blocks_to_chips("grads_to_chips", [p for _, p in swapped]).run()
    through = through + [(gr, p, r) for (gr, p), r in zip(swapped, from_chips)]
    halves = [_sum_chips(gr.name, p, r, me_idx, c_idx) for gr, p, r in through]
    summed = dict(zip([gr.name for gr, _, _ in through], _share_with_sibling(halves)))
    if ext_uq:
        g = summed['a_w_uq']
        summed['a_w_uq'] = g.reshape(g.shape[0], -1, HEAD_W)[:, :, :QK_NOPE + QK_ROPE]
    total = summed['small'].reshape(-1)

    out_g, out_d, out_m, out_v = [], [], [], []
    off = 0
    for n in WEIGHTS:
        shape = local_w[n].shape
        view = transposed if n == 'a_w_in' else (lambda a: a)
        if n in matrices:
            g = summed[n].reshape(view(local_w[n]).shape)
        else:
            size = 1
            for s in shape:
                size *= s
            g = total[off:off + size].reshape(shape)
            off += size
        results = _adamw("adamw_" + n, _as2d(view(local_w[n])), _as2d(g), _as2d(view(local_m[n])), _as2d(view(local_v[n])))
        for out, r in zip((out_g, out_d, out_m, out_v), results):
            out.append(view(r.reshape(view(local_w[n]).shape)))

    return (total[off], grad_x[None], *out_g, *out_d, *out_m, *out_v)
```

```python
import functools

import jax
import jax.numpy as jnp
from jax import lax
from jax.experimental import pallas as pl
from jax.experimental.pallas import tpu as pltpu

F32 = jnp.float32
BF16 = jnp.bfloat16
MESH = pl.DeviceIdType.MESH

RMS_EPS = 1e-6
QK_NOPE = 128
QK_ROPE = 64
V_HEAD = 128
HEAD_W = 256
ROPE_BASE = 10000.0
Q_BLOCK = 128
MASK_VALUE = -1e30
CONV_WIDTH = 4
LRU_C = 8.0
N_CHIPS = 4

ADAM_LR = 0.001
ADAM_B1 = 0.9
ADAM_B2 = 0.999
ADAM_EPS = 1e-08
ADAM_WD = 0.01
ADAM_STEP = 10

VMEM_LIMIT_V7X = 56 * 1024 * 1024
LANES = 128
SUBLANES = 8

WEIGHTS = ['meta_tokens', 'a_norm_g', 'a_w_in', 'a_q_norm_g', 'a_kv_norm_g', 'a_w_uq', 'a_w_ukv',
           'a_w_out', 'b_norm_g', 'b_w_in', 'b_conv_w', 'b_conv_b', 'b_w_rg', 'b_b_rg', 'b_w_ig',
           'b_b_ig', 'b_lam', 'b_w_out', 'final_norm_g']
SHARD_AXIS = {'meta_tokens': 1, 'a_norm_g': None, 'a_w_in': 2, 'a_q_norm_g': None, 'a_kv_norm_g': None,
              'a_w_uq': 2, 'a_w_ukv': 2, 'a_w_out': 1, 'b_norm_g': 1, 'b_w_in': 2, 'b_conv_w': 2,
              'b_conv_b': 1, 'b_w_rg': 2, 'b_b_rg': 1, 'b_w_ig': 2, 'b_b_ig': 1, 'b_lam': 1,
              'b_w_out': 1, 'final_norm_g': None}


def _params(sem=None):
    return pltpu.CompilerParams(dimension_semantics=sem, vmem_limit_bytes=VMEM_LIMIT_V7X)


def _row_tile(tp):
    return 384 if (tp % 384 == 0 and tp >= 1152) else 128


def _sigmoid(x):
    return 1.0 / (1.0 + jnp.exp(-x))


def _rms(x):
    return lax.rsqrt(jnp.mean(x * x, axis=-1, keepdims=True) + RMS_EPS)


def _swap32(x):
    lane = lax.broadcasted_iota(jnp.int32, x.shape, 1)
    return jnp.where(lane < 32, pltpu.roll(x, 96, 1), pltpu.roll(x, 32, 1))


def _rope(x, cos_t, sin_t):
    return x * cos_t + _swap32(x) * sin_t


def _unrope(d, cos_t, sin_t):
    lane = lax.broadcasted_iota(jnp.int32, d.shape, 1)
    return jnp.where(lane < QK_ROPE, d * cos_t + _swap32(d * sin_t), 0.0)


def _dot(a, b):
    return jnp.dot(a, b, preferred_element_type=F32)


def _dot_nt(a, b):
    return lax.dot_general(a, b, (((1,), (1,)), ((), ())), preferred_element_type=F32)


def _dot_tn(a, b):
    return lax.dot_general(a, b, (((0,), (0,)), ((), ())), preferred_element_type=F32)


def _tokens_in_proj(name, tokens, meta, g, wt, splits, tm, pad, exchange=None):
    seq, kin = tokens.shape
    x0 = pad + meta.shape[0]
    tp = x0 + seq
    n = wt.shape[0]
    nt = tp // tm
    assert x0 <= tm and x0 % SUBLANES == 0 and pad % SUBLANES == 0 and nt >= 2

    def compute(x_hbm, meta_ref, g_ref, w_ref, h_ref, *rest):
        outs, (tile_s, sems) = rest[:len(splits)], rest[len(splits):]
        i = pl.program_id(0)

        def first_rows():
            return pltpu.make_async_copy(x_hbm.at[pl.ds(0, tm - x0), :], tile_s.at[0, pl.ds(x0, tm - x0), :], sems.at[0])

        def rows_of(t):
            return pltpu.make_async_copy(x_hbm.at[pl.ds(pl.multiple_of(t * tm - x0, SUBLANES), tm), :],
                                         tile_s.at[t % 2], sems.at[t % 2])

        @pl.when(i == 0)
        def _():
            tile_s[0, 0:pad, :] = jnp.zeros((pad, kin), F32)
            tile_s[0, pad:x0, :] = meta_ref[...]
            if x0 < tm:
                first_rows().start()
                first_rows().wait()

        pl.when(i > 0)(lambda: rows_of(i).wait())
        pl.when(i + 1 < nt)(lambda: rows_of(i + 1).start())

        xv = tile_s[i % 2]
        h_ref[...] = xv
        nrm = ((xv * _rms(xv)) * g_ref[...]).astype(BF16)
        y = _dot_nt(nrm, w_ref[...])
        for o_ref, (c0, c1) in zip(outs, splits):
            o_ref[...] = y[:, c0:c1]

    in_specs = [pl.BlockSpec(memory_space=pl.ANY),
                pl.BlockSpec(meta.shape, lambda i: (0, 0)),
                pl.BlockSpec((1, kin), lambda i: (0, 0)),
                pl.BlockSpec((n, kin), lambda i: (0, 0))]
    out_specs = [pl.BlockSpec((tm, kin), lambda i: (i, 0))]
    out_specs += [pl.BlockSpec((tm, c1 - c0), lambda i: (i, 0)) for c0, c1 in splits]
    out_shape = [jax.ShapeDtypeStruct((tp, kin), F32)]
    out_shape += [jax.ShapeDtypeStruct((tp, c1 - c0), F32) for c0, c1 in splits]
    args, scratch = [tokens, meta, g, wt], [pltpu.VMEM((2, tm, kin), F32), pltpu.SemaphoreType.DMA((2,))]
    body = _with_exchange(exchange, len(in_specs), len(out_specs),
                          lambda: pl.program_id(0) == 0, lambda: pl.program_id(0) == nt - 1, compute)
    if exchange is not None:
        in_specs = in_specs + exchange.specs(exchange.n_in)
        out_specs = out_specs + exchange.specs(exchange.n_out)
        out_shape = out_shape + exchange.out_shapes
        args, scratch = args + exchange.arrays, scratch + exchange.scratch()
    return pl.pallas_call(
        body, name=name, grid=(nt,),
        in_specs=in_specs, out_specs=out_specs, out_shape=out_shape, scratch_shapes=scratch,
        compiler_params=_params(("arbitrary",)),
    )(*args)


def _q_proj(q_lat, g, w_uq, cos_t, sin_t, heads, tm):
    tp, kin = q_lat.shape
    n = heads * HEAD_W

    def body(x_ref, g_ref, w_ref, cos_ref, sin_ref, q_ref):
        xv = x_ref[...]
        nrm = ((xv * _rms(xv)) * g_ref[...]).astype(BF16)
        y = _dot(nrm, w_ref[...])
        cos_v, sin_v = cos_ref[...], sin_ref[...]
        for h in range(heads):
            c0 = h * HEAD_W
            q_ref[:, c0:c0 + QK_NOPE] = y[:, c0:c0 + QK_NOPE].astype(BF16)
            q_ref[:, c0 + QK_NOPE:c0 + HEAD_W] = _rope(y[:, c0 + QK_NOPE:c0 + HEAD_W], cos_v, sin_v).astype(BF16)

    return pl.pallas_call(
        body, name="a_q_proj", grid=(tp // tm,),
        in_specs=[pl.BlockSpec((tm, kin), lambda i: (i, 0)),
                  pl.BlockSpec((1, kin), lambda i: (0, 0)),
                  pl.BlockSpec((kin, n), lambda i: (0, 0)),
                  pl.BlockSpec((tm, LANES), lambda i: (i, 0)),
                  pl.BlockSpec((tm, LANES), lambda i: (i, 0))],
        out_specs=pl.BlockSpec((tm, n), lambda i: (i, 0)),
        out_shape=jax.ShapeDtypeStruct((tp, n), BF16),
        compiler_params=_params(("parallel",)),
    )(q_lat, g, w_uq, cos_t, sin_t)


def _kv_proj(kv_lat, g, w_ukv, k_rope_raw, cos_t, sin_t, heads, tm):
    tp, kin = kv_lat.shape
    n = heads * (QK_NOPE + V_HEAD)

    def body(x_ref, g_ref, w_ref, kr_ref, cos_ref, sin_ref, k_ref, v_ref):
        xv = x_ref[...]
        nrm = ((xv * _rms(xv)) * g_ref[...]).astype(BF16)
        y = _dot(nrm, w_ref[...])
        kr = _rope(kr_ref[...], cos_ref[...], sin_ref[...]).astype(BF16)
        for h in range(heads):
            c0 = h * (QK_NOPE + V_HEAD)
            k_ref[:, h * HEAD_W:h * HEAD_W + QK_NOPE] = y[:, c0:c0 + QK_NOPE].astype(BF16)
            k_ref[:, h * HEAD_W + QK_NOPE:(h + 1) * HEAD_W] = kr
            v_ref[:, h * V_HEAD:(h + 1) * V_HEAD] = y[:, c0 + QK_NOPE:c0 + QK_NOPE + V_HEAD].astype(BF16)

    return pl.pallas_call(
        body, name="a_kv_proj", grid=(tp // tm,),
        in_specs=[pl.BlockSpec((tm, kin), lambda i: (i, 0)),
                  pl.BlockSpec((1, kin), lambda i: (0, 0)),
                  pl.BlockSpec((kin, n), lambda i: (0, 0)),
                  pl.BlockSpec((tm, LANES), lambda i: (i, 0)),
                  pl.BlockSpec((tm, LANES), lambda i: (i, 0)),
                  pl.BlockSpec((tm, LANES), lambda i: (i, 0))],
        out_specs=[pl.BlockSpec((tm, heads * HEAD_W), lambda i: (i, 0)),
                   pl.BlockSpec((tm, heads * V_HEAD), lambda i: (i, 0))],
        out_shape=[jax.ShapeDtypeStruct((tp, heads * HEAD_W), BF16),
                   jax.ShapeDtypeStruct((tp, heads * V_HEAD), BF16)],
        compiler_params=_params(("parallel",)),
    )(kv_lat, g, w_ukv, k_rope_raw, cos_t, sin_t)


def _as_rows(col):
    rows = col.shape[0]
    return jnp.transpose(jnp.broadcast_to(col, (rows, LANES)))[0:SUBLANES, :]


def _attn_mask(row0, col0, rows, cols, pad):
    row = row0 + lax.broadcasted_iota(jnp.int32, (rows, cols), 0)
    col = col0 + lax.broadcasted_iota(jnp.int32, (rows, cols), 1)
    return (col <= row) & (col >= pad)


LOG2E = 1.4426950408889634
FLASH_FWD_TRIPS = ((4, 2), (2, 2), (1, 1))


def _flash_fwd(q, k, v, heads, pad, tq, exchange=None):
    tp = q.shape[0]
    nq = tp // tq
    c2 = (QK_NOPE + QK_ROPE) ** -0.5 * LOG2E

    def compute(q_ref, k_ref, v_ref, o_ref, lse_ref):
        i = pl.program_id(1)

        def make_step(masked, blocks, parts=1):
            keys = blocks * tq // parts

            def step(j, carry):
                m, l, acc = carry
                offs = [pl.multiple_of(j * tq + part * keys, tq) for part in range(parts)]
                scores = [_dot_nt(q_ref[...], k_ref[pl.ds(off, keys), :]) for off in offs]
                for off, s in zip(offs, scores):
                    s = s * c2
                    if masked:
                        s = jnp.where(_attn_mask(i * tq, j * tq, tq, keys, pad), s, MASK_VALUE)
                    m_new = jnp.maximum(m, jnp.max(s, axis=-1, keepdims=True))
                    p = jnp.exp2(s - m_new)
                    alpha = jnp.exp2(m - m_new)
                    l = alpha * l + jnp.sum(p, axis=-1, keepdims=True)
                    acc = alpha * acc + _dot(p.astype(BF16), v_ref[pl.ds(off, keys), :])
                    m = m_new
                return m, l, acc
            return step

        init = (jnp.full((tq, 1), MASK_VALUE, F32), jnp.zeros((tq, 1), F32), jnp.zeros((tq, V_HEAD), F32))
        carry = make_step(True, 1)(0, init)
        first = 1
        for blocks, parts in FLASH_FWD_TRIPS:
            trips = jnp.maximum(i - first, 0) // blocks
            step_n = make_step(False, blocks, parts)
            carry = lax.fori_loop(0, trips, lambda t, cr, f=first, b=blocks, s=step_n: s(f + b * t, cr), carry)
            first = first + blocks * trips
        m, l, acc = lax.fori_loop(jnp.maximum(i, 1), i + 1, make_step(True, 1), carry)
        o_ref[...] = acc / l
        lse_ref[...] = _as_rows(m + jnp.log(l) * LOG2E)

    in_specs = [pl.BlockSpec((tq, HEAD_W), lambda h, i: (i, h)),
                pl.BlockSpec((tp, HEAD_W), lambda h, i: (0, h)),
                pl.BlockSpec((tp, V_HEAD), lambda h, i: (0, h))]
    out_specs = [pl.BlockSpec((tq, V_HEAD), lambda h, i: (i, h)),
                 pl.BlockSpec((None, None, SUBLANES, tq), lambda h, i: (h, i, 0, 0))]
    out_shape = [jax.ShapeDtypeStruct((tp, heads * V_HEAD), F32),
                 jax.ShapeDtypeStruct((heads, nq, SUBLANES, tq), F32)]
    args, scratch = [q, k, v], []
    body = _with_exchange(exchange, len(in_specs), len(out_specs),
                          lambda: (pl.program_id(0) == 0) & (pl.program_id(1) == 0),
                          lambda: (pl.program_id(0) == heads - 1) & (pl.program_id(1) == nq - 1), compute)
    if exchange is not None:
        in_specs = in_specs + exchange.specs(exchange.n_in)
        out_specs = out_specs + exchange.specs(exchange.n_out)
        out_shape = out_shape + exchange.out_shapes
        args, scratch = args + exchange.arrays, exchange.scratch()
    return pl.pallas_call(
        body, name="a_flash_fwd", grid=(heads, nq),
        in_specs=in_specs, out_specs=out_specs, out_shape=out_shape, scratch_shapes=scratch,
        compiler_params=_params(("arbitrary", "arbitrary")),
    )(*args)


def _out_proj_in_proj(name, a, gate, w_out, resid, g, w_in, splits, tm):
    tp, wd = a.shape
    d = w_out.shape[1]
    n = w_in.shape[1]

    def body(a_ref, gate_ref, wo_ref, res_ref, g_ref, wi_ref, h_ref, *outs):
        gv = gate_ref[...]
        y = (a_ref[...] * (gv * _sigmoid(gv))).astype(BF16)
        h = res_ref[...] + _dot(y, wo_ref[...])
        h_ref[...] = h
        nrm = ((h * _rms(h)) * g_ref[...]).astype(BF16)
        z = _dot(nrm, wi_ref[...])
        for o_ref, (c0, c1) in zip(outs, splits):
            o_ref[...] = z[:, c0:c1]

    return pl.pallas_call(
        body, name=name, grid=(tp // tm,),
        in_specs=[pl.BlockSpec((tm, wd), lambda i: (i, 0)),
                  pl.BlockSpec((tm, wd), lambda i: (i, 0)),
                  pl.BlockSpec((wd, d), lambda i: (0, 0)),
                  pl.BlockSpec((tm, d), lambda i: (i, 0)),
                  pl.BlockSpec((1, d), lambda i: (0, 0)),
                  pl.BlockSpec((d, n), lambda i: (0, 0))],
        out_specs=[pl.BlockSpec((tm, d), lambda i: (i, 0))]
        + [pl.BlockSpec((tm, c1 - c0), lambda i: (i, 0)) for c0, c1 in splits],
        out_shape=[jax.ShapeDtypeStruct((tp, d), F32)]
        + [jax.ShapeDtypeStruct((tp, c1 - c0), F32) for c0, c1 in splits],
        compiler_params=_params(("parallel",)),
    )(a, gate, w_out, resid, g, w_in)


def _lru_decay(r, sp):
    log_a = -LRU_C * r * sp
    a = jnp.exp(log_a)
    e2 = a * a
    x2 = 2.0 * log_a
    series = x2 * (1.0 + x2 * (0.5 + x2 * (1.0 / 6.0)))
    em1 = jnp.where(x2 > -0.02, series, e2 - 1.0)
    return a, e2, jnp.sqrt(-em1)


def _softplus(x):
    return jnp.maximum(x, 0.0) + jnp.log1p(jnp.exp(-jnp.abs(x)))


def _rglru_fwd(u, conv_w, conv_b, w_rg, b_rg, w_ig, b_ig, lam, pad, tm):
    tp, w = u.shape
    groups, blk = w_rg.shape[0], w_rg.shape[1]

    def body(u_ref, cw_ref, cb_ref, wr_ref, br_ref, wi_ref, bi_ref, lam_ref,
             uc_ref, r_ref, ig_ref, hs_ref, a_s, mult_ref, uext, b_s, hc):
        i = pl.program_id(0)

        @pl.when(i == 0)
        def _():
            uext[0:SUBLANES, :] = jnp.zeros((SUBLANES, w), F32)
            hc[...] = jnp.zeros((SUBLANES, w), F32)

        uext[SUBLANES:SUBLANES + tm, :] = u_ref[...]
        cw = cw_ref[...]
        uc = cb_ref[...] + uext[pl.ds(SUBLANES - 3, tm), :] * cw[0:1, :]
        uc = uc + uext[pl.ds(SUBLANES - 2, tm), :] * cw[1:2, :]
        uc = uc + uext[pl.ds(SUBLANES - 1, tm), :] * cw[2:3, :]
        uc = uc + uext[pl.ds(SUBLANES, tm), :] * cw[3:4, :]
        uc_ref[...] = uc
        uext[0:SUBLANES, :] = uext[tm:tm + SUBLANES, :]

        sp = _softplus(-lam_ref[...])
        for g in range(groups):
            sl = slice(g * blk, (g + 1) * blk)
            ucg = uc_ref[:, sl]
            ucb = ucg.astype(BF16)
            r = _sigmoid(_dot(ucb, wr_ref[g]) + br_ref[:, sl])
            ig = _sigmoid(_dot(ucb, wi_ref[g]) + bi_ref[:, sl])
            r_ref[:, sl] = r
            ig_ref[:, sl] = ig
            a, _, mult = _lru_decay(r, sp[:, sl])
            a_s[:, sl] = a
            mult_ref[:, sl] = mult
            b_s[:, sl] = mult * (ig * ucg)

        @pl.when(i == 0)
        def _():
            row = lax.broadcasted_iota(jnp.int32, (Q_BLOCK, w), 0)
            start = ig_ref[0:Q_BLOCK, :] * uc_ref[0:Q_BLOCK, :]
            b_s[0:Q_BLOCK, :] = jnp.where(row < pad, 0.0, jnp.where(row == pad, start, b_s[0:Q_BLOCK, :]))
            mult_ref[0:Q_BLOCK, :] = jnp.where(row == pad, 1.0, mult_ref[0:Q_BLOCK, :])

        row8 = lax.broadcasted_iota(jnp.int32, (SUBLANES, w), 0)

        def group(gi, h_in):
            off = pl.multiple_of(gi * SUBLANES, SUBLANES)
            av = a_s[pl.ds(off, SUBLANES), :]
            bv = b_s[pl.ds(off, SUBLANES), :]
            for k in (1, 2, 4):
                keep = row8 >= k
                bv = jnp.where(keep, av * pltpu.roll(bv, k, 0) + bv, bv)
                av = jnp.where(keep, av * pltpu.roll(av, k, 0), av)
            hv = av * h_in + bv
            hs_ref[pl.ds(off, SUBLANES), :] = hv
            return jnp.broadcast_to(hv[SUBLANES - 1:SUBLANES, :], (SUBLANES, w))

        hc[...] = lax.fori_loop(0, tm // SUBLANES, group, hc[...])

    row_spec = pl.BlockSpec((tm, w), lambda i: (i, 0))
    vec_spec = pl.BlockSpec((1, w), lambda i: (0, 0))
    mat_spec = pl.BlockSpec((groups, blk, blk), lambda i: (0, 0, 0))
    return pl.pallas_call(
        body, name="b_rglru_fwd", grid=(tp // tm,),
        in_specs=[row_spec, pl.BlockSpec((CONV_WIDTH, w), lambda i: (0, 0)), vec_spec,
                  mat_spec, vec_spec, mat_spec, vec_spec, vec_spec],
        out_specs=[row_spec] * 6,
        out_shape=[jax.ShapeDtypeStruct((tp, w), F32)] * 6,
        scratch_shapes=[pltpu.VMEM((tm + SUBLANES, w), F32), pltpu.VMEM((tm, w), F32),
                        pltpu.VMEM((SUBLANES, w), F32)],
        compiler_params=_params(("arbitrary",)),
    )(u, conv_w, conv_b, w_rg, b_rg, w_ig, b_ig, lam)


def _out_proj_loss(a, gate, w, resid, g, target, x0, tm):
    tp, wd = a.shape
    d = w.shape[1]
    assert x0 % Q_BLOCK == 0 and tm % Q_BLOCK == 0 and target.shape[0] == tp - x0
    lead = x0 // Q_BLOCK
    per = tm // Q_BLOCK

    def body(a_ref, gate_ref, w_ref, res_ref, g_ref, *rest):
        t_refs, (dh_ref, loss_ref, dg_ref, da_ref, dgate_ref, dw_ref) = rest[:per], rest[per:]
        i = pl.program_id(0)

        @pl.when(i == 0)
        def _():
            loss_ref[...] = jnp.zeros_like(loss_ref)
            dg_ref[...] = jnp.zeros_like(dg_ref)
            dw_ref[...] = jnp.zeros_like(dw_ref)

        gate_v = gate_ref[...]
        av = a_ref[...]
        sg = _sigmoid(gate_v)
        silu = gate_v * sg
        y = (av * silu).astype(BF16)
        h = res_ref[...] + _dot(y, w_ref[...])
        gv = g_ref[...]
        for b in range(per):
            rows = slice(b * Q_BLOCK, (b + 1) * Q_BLOCK)
            xv = h[rows, :]
            r = _rms(xv)
            xh = xv * r
            err = jnp.where(i * per + b >= lead, xh * gv - t_refs[b][...], 0.0)
            loss_ref[...] += 0.5 * jnp.sum(jnp.mean(err * err, axis=-1, keepdims=True))
            dy = err / d
            dg_ref[...] += jnp.sum(dy * xh, axis=0, keepdims=True)
            dxh = dy * gv
            dh_ref[rows, :] = r * (dxh - xh * jnp.mean(dxh * xh, axis=-1, keepdims=True))

        dob = dh_ref[...].astype(BF16)
        dyv = _dot_nt(dob, w_ref[...])
        da_ref[...] = dyv * silu
        dgate_ref[...] = (dyv * av * (sg * (1.0 + gate_v * (1.0 - sg)))).astype(BF16)
        dw_ref[...] += _dot_tn(y, dob)

    def piece(b):
        return pl.BlockSpec((Q_BLOCK, d), lambda i: (jnp.maximum(i * per + b - lead, 0), 0))

    return pl.pallas_call(
        body, name="b_out_proj_loss", grid=(tp // tm,),
        in_specs=[pl.BlockSpec((tm, wd), lambda i: (i, 0)),
                  pl.BlockSpec((tm, wd), lambda i: (i, 0)),
                  pl.BlockSpec((wd, d), lambda i: (0, 0)),
                  pl.BlockSpec((tm, d), lambda i: (i, 0)),
                  pl.BlockSpec((1, d), lambda i: (0, 0))] + [piece(b) for b in range(per)],
        out_specs=[pl.BlockSpec((tm, d), lambda i: (i, 0)),
                   pl.BlockSpec((SUBLANES, LANES), lambda i: (0, 0)),
                   pl.BlockSpec((1, d), lambda i: (0, 0)),
                   pl.BlockSpec((tm, wd), lambda i: (i, 0)),
                   pl.BlockSpec((tm, wd), lambda i: (i, 0)),
                   pl.BlockSpec((wd, d), lambda i: (0, 0))],
        out_shape=[jax.ShapeDtypeStruct((tp, d), F32),
                   jax.ShapeDtypeStruct((SUBLANES, LANES), F32),
                   jax.ShapeDtypeStruct((1, d), F32),
                   jax.ShapeDtypeStruct((tp, wd), F32),
                   jax.ShapeDtypeStruct((tp, wd), BF16),
                   jax.ShapeDtypeStruct((wd, d), F32)],
        compiler_params=_params(("arbitrary",)),
    )(a, gate, w, resid, g, *([target] * per))


def _with_exchange(exchange, n_in, n_out, first, last, compute):
    if exchange is None:
        return compute
    ex_in, ex_out = exchange.n_in, exchange.n_out

    def body(*refs):
        own_in, their_in = refs[:n_in], refs[n_in:n_in + ex_in]
        pos = n_in + ex_in
        own_out, their_out = refs[pos:pos + n_out], refs[pos + n_out:pos + n_out + ex_out]
        rest = refs[pos + n_out + ex_out:]
        own_scratch, sems = rest[:len(rest) - 2], rest[len(rest) - 2:]

        @pl.when(first())
        def _():
            exchange.start(their_in, their_out, sems)

        compute(*own_in, *own_out, *own_scratch)

        @pl.when(last())
        def _():
            exchange.finish(their_in, their_out, sems)

    return body


def _gated_out_bwd(name, dout, a, gate, w, tm, delta_heads=0, exchange=None):
    tp, wd = a.shape
    d = w.shape[1]
    nt = tp // tm

    def compute(do_ref, a_ref, gate_ref, w_ref, da_ref, dgate_ref, dw_ref, *delta_ref):
        @pl.when(pl.program_id(0) == 0)
        def _():
            dw_ref[...] = jnp.zeros_like(dw_ref)

        gv = gate_ref[...]
        av = a_ref[...]
        sg = _sigmoid(gv)
        silu = gv * sg
        dob = do_ref[...].astype(BF16)
        dy = _dot_nt(dob, w_ref[...])
        da = dy * silu
        da_ref[...] = da.astype(BF16)
        dgate_ref[...] = (dy * av * (sg * (1.0 + gv * (1.0 - sg)))).astype(BF16)
        dw_ref[...] += _dot_tn((av * silu).astype(BF16), dob)
        for h in range(delta_heads):
            sl = slice(h * V_HEAD, (h + 1) * V_HEAD)
            delta_ref[0][h] = _as_rows(jnp.sum(da[:, sl] * av[:, sl], axis=-1, keepdims=True))

    out_specs = [pl.BlockSpec((tm, wd), lambda i: (i, 0)),
                 pl.BlockSpec((tm, wd), lambda i: (i, 0)),
                 pl.BlockSpec((wd, d), lambda i: (0, 0))]
    out_shape = [jax.ShapeDtypeStruct((tp, wd), BF16),
                 jax.ShapeDtypeStruct((tp, wd), BF16),
                 jax.ShapeDtypeStruct((wd, d), F32)]
    if delta_heads:
        out_specs.append(pl.BlockSpec((delta_heads, None, SUBLANES, tm), lambda i: (0, i, 0, 0)))
        out_shape.append(jax.ShapeDtypeStruct((delta_heads, tp // tm, SUBLANES, tm), F32))
    in_specs = [pl.BlockSpec((tm, d), lambda i: (i, 0)),
                pl.BlockSpec((tm, wd), lambda i: (i, 0)),
                pl.BlockSpec((tm, wd), lambda i: (i, 0)),
                pl.BlockSpec((wd, d), lambda i: (0, 0))]
    args, scratch = [dout, a, gate, w], []
    body = _with_exchange(exchange, len(in_specs), len(out_specs),
                          lambda: pl.program_id(0) == 0, lambda: pl.program_id(0) == nt - 1, compute)
    if exchange is not None:
        in_specs = in_specs + exchange.specs(exchange.n_in)
        out_specs = out_specs + exchange.specs(exchange.n_out)
        out_shape = out_shape + exchange.out_shapes
        args, scratch = args + exchange.arrays, exchange.scratch()
    return pl.pallas_call(
        body, name=name, grid=(nt,),
        in_specs=in_specs, out_specs=out_specs, out_shape=out_shape, scratch_shapes=scratch,
        compiler_params=_params(("arbitrary",)),
    )(*args)


def _rglru_bwd(dhs, hs, r, ig, uc, u, a, mult, conv_w, w_rg, w_ig, lam, pad, tm):
    tp, w = u.shape
    groups, blk = w_rg.shape[0], w_rg.shape[1]
    nt = tp // tm
    per8 = tm // SUBLANES

    def body(dhs_ref, hs_ref, hprev_ref, r_ref, ig_ref, uc_ref, u_ref, uprev_ref, a_ref, mult_ref,
             cw_ref, wr_ref, wi_ref, lam_ref,
             du_ref, dcw_ref, dcb_ref, dwr_ref, dbr_ref, dwi_ref, dbi_ref, dlam_ref,
             aext, c_s, g_s, hext, uext, ducext, gc):
        step = pl.program_id(0)
        ti = nt - 1 - step

        @pl.when(step == 0)
        def _():
            for ref in (dcw_ref, dcb_ref, dwr_ref, dbr_ref, dwi_ref, dbi_ref, dlam_ref):
                ref[...] = jnp.zeros_like(ref)
            aext[tm:tm + SUBLANES, :] = jnp.zeros((SUBLANES, w), F32)
            ducext[tm:tm + SUBLANES, :] = jnp.zeros((SUBLANES, w), F32)
            gc[...] = jnp.zeros((SUBLANES, w), F32)

        lam_v = lam_ref[...]
        sp = _softplus(-lam_v)
        row = ti * tm + lax.broadcasted_iota(jnp.int32, (tm, w), 0)

        rv = r_ref[...]
        a = a_ref[...]
        mult = mult_ref[...]
        aext[0:tm, :] = a
        c_s[...] = aext[pl.ds(1, tm), :]
        row8 = lax.broadcasted_iota(jnp.int32, (SUBLANES, w), 0)

        def group(gi, g_in):
            off = pl.multiple_of((per8 - 1 - gi) * SUBLANES, SUBLANES)
            cv = c_s[pl.ds(off, SUBLANES), :]
            dv = dhs_ref[pl.ds(off, SUBLANES), :]
            for k in (1, 2, 4):
                keep = row8 < SUBLANES - k
                dv = jnp.where(keep, cv * pltpu.roll(dv, SUBLANES - k, 0) + dv, dv)
                cv = jnp.where(keep, cv * pltpu.roll(cv, SUBLANES - k, 0), cv)
            gv = cv * g_in + dv
            g_s[pl.ds(off, SUBLANES), :] = gv
            return jnp.broadcast_to(gv[0:1, :], (SUBLANES, w))

        gc[...] = lax.fori_loop(0, per8, group, gc[...])
        aext[tm:tm + SUBLANES, :] = aext[0:SUBLANES, :]

        gsc = jnp.where(row < pad, 0.0, g_s[...])
        hext[0:SUBLANES, :] = hprev_ref[...]
        hext[SUBLANES:SUBLANES + tm, :] = hs_ref[...]
        hprev = jnp.where(row == 0, 0.0, hext[pl.ds(SUBLANES - 1, tm), :])
        igv = ig_ref[...]
        ucv = uc_ref[...]
        first = row == pad
        dmult = gsc * (igv * ucv)
        dig = gsc * mult * ucv
        duc = gsc * mult * igv
        dlog_a = (gsc * hprev) * a + jnp.where(first, 0.0, dmult * (-(a * a) / mult))
        dlam_ref[...] += jnp.sum(dlog_a * rv, axis=0, keepdims=True) * (LRU_C * _sigmoid(-lam_v))
        dpre_r = dlog_a * (-LRU_C * sp) * (rv * (1.0 - rv))
        dpre_i = dig * (igv * (1.0 - igv))
        dbr_ref[...] += jnp.sum(dpre_r, axis=0, keepdims=True)
        dbi_ref[...] += jnp.sum(dpre_i, axis=0, keepdims=True)
        for g in range(groups):
            sl = slice(g * blk, (g + 1) * blk)
            ucb = ucv[:, sl].astype(BF16)
            drb = dpre_r[:, sl].astype(BF16)
            dib = dpre_i[:, sl].astype(BF16)
            dwr_ref[g] += _dot_tn(ucb, drb)
            dwi_ref[g] += _dot_tn(ucb, dib)
            ducext[0:tm, sl] = duc[:, sl] + _dot_nt(drb, wr_ref[g]) + _dot_nt(dib, wi_ref[g])

        ducv = ducext[0:tm, :]
        cw = cw_ref[...]
        dcb_ref[...] += jnp.sum(ducv, axis=0, keepdims=True)
        uext[0:SUBLANES, :] = jnp.where(ti == 0, 0.0, uprev_ref[...])
        uext[SUBLANES:SUBLANES + tm, :] = u_ref[...]
        for j in range(CONV_WIDTH):
            ush = uext[pl.ds(SUBLANES - (CONV_WIDTH - 1 - j), tm), :]
            dcw_ref[j:j + 1, :] += jnp.sum(ducv * ush, axis=0, keepdims=True)
        du = ducv * cw[3:4, :]
        for k in range(1, CONV_WIDTH):
            du = du + ducext[pl.ds(k, tm), :] * cw[3 - k:4 - k, :]
        du_ref[...] = du.astype(BF16)
        ducext[tm:tm + SUBLANES, :] = ducext[0:SUBLANES, :]

    rev = lambda s: (nt - 1 - s, 0)
    halo = lambda s: (jnp.maximum((nt - 1 - s) * per8 - 1, 0), 0)
    row_spec = pl.BlockSpec((tm, w), rev)
    halo_spec = pl.BlockSpec((SUBLANES, w), halo)
    vec_spec = pl.BlockSpec((1, w), lambda s: (0, 0))
    mat_spec = pl.BlockSpec((groups, blk, blk), lambda s: (0, 0, 0))
    cw_spec = pl.BlockSpec((CONV_WIDTH, w), lambda s: (0, 0))
    return pl.pallas_call(
        body, name="b_rglru_bwd", grid=(nt,),
        in_specs=[row_spec, row_spec, halo_spec, row_spec, row_spec, row_spec, row_spec, halo_spec, row_spec, row_spec,
                  cw_spec, mat_spec, mat_spec, vec_spec],
        out_specs=[row_spec, cw_spec, vec_spec, mat_spec, vec_spec, mat_spec, vec_spec, vec_spec],
        out_shape=[jax.ShapeDtypeStruct((tp, w), BF16), jax.ShapeDtypeStruct((CONV_WIDTH, w), F32),
                   jax.ShapeDtypeStruct((1, w), F32), jax.ShapeDtypeStruct((groups, blk, blk), F32),
                   jax.ShapeDtypeStruct((1, w), F32), jax.ShapeDtypeStruct((groups, blk, blk), F32),
                   jax.ShapeDtypeStruct((1, w), F32), jax.ShapeDtypeStruct((1, w), F32)],
        scratch_shapes=[pltpu.VMEM((tm + SUBLANES, w), F32), pltpu.VMEM((tm, w), F32), pltpu.VMEM((tm, w), F32),
                        pltpu.VMEM((tm + SUBLANES, w), F32), pltpu.VMEM((tm + SUBLANES, w), F32),
                        pltpu.VMEM((tm + SUBLANES, w), F32), pltpu.VMEM((SUBLANES, w), F32)],
        compiler_params=_params(("arbitrary",)),
    )(dhs, hs, hs, r, ig, uc, u, u, a, mult, conv_w, w_rg, w_ig, lam)


def _norm_matmul_bwd(name, x, g, w, dys, tm, resid=None, prologue=None, extra_out=None, exchange=None,
                     transposed=False, dx_dtype=F32):
    tp, kin = x.shape
    n = w.shape[0] if transposed else w.shape[1]
    w_shape = (n, kin) if transposed else (kin, n)
    nt = tp // tm
    n_dy = len(dys)
    has_res = resid is not None
    has_extra = extra_out is not None

    def compute(*refs):
        x_ref, g_ref, w_ref = refs[:3]
        dy_refs = refs[3:3 + n_dy]
        pos = 3 + n_dy
        res_ref = refs[pos] if has_res else None
        pos += int(has_res)
        dx_ref, dw_ref, dg_ref = refs[pos:pos + 3]
        pos += 3
        ex_ref = refs[pos] if has_extra else None
        pos += int(has_extra)
        dy_s = refs[pos]

        @pl.when(pl.program_id(0) == 0)
        def _():
            dw_ref[...] = jnp.zeros_like(dw_ref)
            dg_ref[...] = jnp.zeros_like(dg_ref)

        if prologue is None:
            c0 = 0
            for ref in dy_refs:
                dy_s[:, c0:c0 + ref.shape[1]] = ref[...].astype(BF16)
                c0 += ref.shape[1]
        else:
            prologue(dy_refs, dy_s, ex_ref)

        xv = x_ref[...]
        gv = g_ref[...]
        r = _rms(xv)
        xh = xv * r
        dyb = dy_s[...]
        nb = (xh * gv).astype(BF16)
        if transposed:
            dn = _dot(dyb, w_ref[...])
            dw_ref[...] += _dot_tn(dyb, nb)
        else:
            dn = _dot_nt(dyb, w_ref[...])
            dw_ref[...] += _dot_tn(nb, dyb)
        dg_ref[...] += jnp.sum(dn * xh, axis=0, keepdims=True)
        dxh = dn * gv
        dx = r * (dxh - xh * jnp.mean(dxh * xh, axis=-1, keepdims=True))
        if has_res:
            dx = dx + res_ref[...]
        dx_ref[...] = dx.astype(dx_dtype)

    row = lambda width: pl.BlockSpec((tm, width), lambda i: (i, 0))
    in_specs = [row(kin), pl.BlockSpec((1, kin), lambda i: (0, 0)), pl.BlockSpec(w_shape, lambda i: (0, 0))]
    in_specs += [row(a.shape[1]) for a in dys]
    args = [x, g, w, *dys]
    if has_res:
        in_specs.append(row(kin))
        args.append(resid)
    out_specs = [row(kin), pl.BlockSpec(w_shape, lambda i: (0, 0)), pl.BlockSpec((1, kin), lambda i: (0, 0))]
    out_shape = [jax.ShapeDtypeStruct((tp, kin), dx_dtype), jax.ShapeDtypeStruct(w_shape, F32),
                 jax.ShapeDtypeStruct((1, kin), F32)]
    if has_extra:
        out_specs.append(row(extra_out[0]))
        out_shape.append(jax.ShapeDtypeStruct((tp, extra_out[0]), extra_out[1]))
    scratch = [pltpu.VMEM((tm, n), BF16)]
    body = _with_exchange(exchange, len(in_specs), len(out_specs),
                          lambda: pl.program_id(0) == 0, lambda: pl.program_id(0) == nt - 1, compute)
    if exchange is not None:
        in_specs = in_specs + exchange.specs(exchange.n_in)
        out_specs = out_specs + exchange.specs(exchange.n_out)
        out_shape = out_shape + exchange.out_shapes
        args, scratch = args + exchange.arrays, scratch + exchange.scratch()
    return pl.pallas_call(
        body, name=name, grid=(nt,),
        in_specs=in_specs, out_specs=out_specs, out_shape=out_shape, scratch_shapes=scratch,
        compiler_params=_params(("arbitrary",)),
    )(*args)


def _flash_bwd(q, k, v, lse, delta, do, heads, pad, tq, exchange=None):
    tp = q.shape[0]
    nq = tp // tq
    scale = (QK_NOPE + QK_ROPE) ** -0.5
    c2 = scale * LOG2E

    def compute(q_ref, k_ref, v_ref, lse_ref, delta_ref, do_ref, dq_ref, dk_ref, dv_ref):
        j = pl.program_id(1)

        @pl.when(j == 0)
        def _():
            dq_ref[...] = jnp.zeros_like(dq_ref)

        kv = k_ref[...]
        vv = v_ref[...]

        def rows_of(ref, i, blocks):
            parts = [ref[i + b][0:1, :] for b in range(blocks)]
            return parts[0] if blocks == 1 else jnp.concatenate(parts, axis=1)

        def make_step(masked, blocks):
            def step(i, carry):
                off = pl.multiple_of(i * tq, tq)
                qv = q_ref[pl.ds(off, blocks * tq), :]
                dob = do_ref[pl.ds(off, blocks * tq), :]
                p = jnp.exp2(_dot_nt(kv, qv) * c2 - rows_of(lse_ref, i, blocks))
                if masked:
                    key = j * tq + lax.broadcasted_iota(jnp.int32, (tq, tq), 0)
                    qry = j * tq + lax.broadcasted_iota(jnp.int32, (tq, tq), 1)
                    first = jnp.where((key <= qry) & (key >= pad), p[:, :tq], 0.0)
                    p = first if blocks == 1 else jnp.concatenate([first, p[:, tq:]], axis=1)
                dv_ref[...] += _dot(p.astype(BF16), dob)
                dp = _dot_nt(vv, dob)
                ds = (p * (dp - rows_of(delta_ref, i, blocks)) * scale).astype(BF16)
                dk_ref[...] += _dot(ds, qv)
                dq_ref[pl.ds(off, blocks * tq), :] += _dot_tn(ds, kv)
                return carry
            return step

        dk_ref[...] = jnp.zeros_like(dk_ref)
        dv_ref[...] = jnp.zeros_like(dv_ref)
        odd = (nq - j) % 2
        lax.fori_loop(0, odd, lambda t, cr: make_step(True, 1)(j, cr), 0)
        lax.fori_loop(0, 1 - odd, lambda t, cr: make_step(True, 2)(j, cr), 0)
        start = j + 2 - odd
        for blocks in (4, 2):
            trips = (nq - start) // blocks
            step_n = make_step(False, blocks)
            lax.fori_loop(0, trips, lambda t, cr, s=start, b=blocks, f=step_n: f(s + b * t, cr), 0)
            start = start + blocks * trips

    in_specs = [pl.BlockSpec((tp, HEAD_W), lambda h, j: (0, h)),
                pl.BlockSpec((tq, HEAD_W), lambda h, j: (j, h)),
                pl.BlockSpec((tq, V_HEAD), lambda h, j: (j, h)),
                pl.BlockSpec((None, nq, SUBLANES, tq), lambda h, j: (h, 0, 0, 0)),
                pl.BlockSpec((None, nq, SUBLANES, tq), lambda h, j: (h, 0, 0, 0)),
                pl.BlockSpec((tp, V_HEAD), lambda h, j: (0, h))]
    out_specs = [pl.BlockSpec((tp, HEAD_W), lambda h, j: (0, h)),
                 pl.BlockSpec((tq, HEAD_W), lambda h, j: (j, h)),
                 pl.BlockSpec((tq, V_HEAD), lambda h, j: (j, h))]
    out_shape = [jax.ShapeDtypeStruct((tp, heads * HEAD_W), F32),
                 jax.ShapeDtypeStruct((tp, heads * HEAD_W), F32),
                 jax.ShapeDtypeStruct((tp, heads * V_HEAD), F32)]
    args, scratch = [q, k, v, lse, delta, do], []
    body = _with_exchange(exchange, len(in_specs), len(out_specs),
                          lambda: (pl.program_id(0) == 0) & (pl.program_id(1) == 0),
                          lambda: (pl.program_id(0) == heads - 1) & (pl.program_id(1) == nq - 1), compute)
    if exchange is not None:
        in_specs = in_specs + exchange.specs(exchange.n_in)
        out_specs = out_specs + exchange.specs(exchange.n_out)
        out_shape = out_shape + exchange.out_shapes
        args, scratch = args + exchange.arrays, exchange.scratch()
    return pl.pallas_call(
        body, name="a_flash_bwd", grid=(heads, nq),
        in_specs=in_specs, out_specs=out_specs, out_shape=out_shape, scratch_shapes=scratch,
        compiler_params=_params(("arbitrary", "arbitrary")),
    )(*args)


def _position():
    return lax.axis_index("x"), lax.axis_index("y"), lax.axis_index("c")


def _other_chips(x, y):
    return [(1 - x, y), (x, 1 - y), (1 - x, 1 - y)]


def _block(ref, shard_axis, n, k, split_axis=None, m=None, h=None):
    idx = []
    for a in range(len(ref.shape)):
        start = 0
        size = None
        if a == shard_axis:
            start, size = k * n, n
        if a == split_axis:
            size = (n if a == shard_axis else m) // 2
            start = start + h * size
        idx.append(slice(None) if size is None else pl.ds(start, size))
    return ref.at[tuple(idx)]


def _gather_weights(split, whole_small):
    ns, nw = len(split), len(whole_small)
    n = ns + nw
    arrs = [s[0] for s in split] + [s[0] for s in whole_small]
    axes = [s[1] for s in split] + [s[1] for s in whole_small]

    def body(*refs):
        ins, outs = refs[:n], refs[n:2 * n]
        ici_send, ici_recv, d2d_send, d2d_recv, sib_send, sib_recv = refs[2 * n:]
        x, y, c = _position()
        me = 2 * x + y
        others = _other_chips(x, y)
        sent, local = [], []

        def remote(src, dst, sems, idx, to):
            return pltpu.make_async_remote_copy(src_ref=src, dst_ref=dst, send_sem=sems[0].at[idx],
                                                recv_sem=sems[1].at[idx], device_id=to, device_id_type=MESH)

        for a in range(n):
            width = ins[a].shape[axes[a]]
            mine = remote(ins[a], _block(outs[a], axes[a], width, me), (sib_send, sib_recv), a, (x, y, 1 - c))
            mine.start()
            local.append(mine)
            for j, (px, py) in enumerate(others):
                if a < ns:
                    sx = split[a][2]
                    src = _block(ins[a], None, None, None, sx, ins[a].shape[sx], c)
                    dst = _block(outs[a], axes[a], width, me, sx, outs[a].shape[sx], c)
                else:
                    src, dst = ins[a], _block(outs[a], axes[a], width, me)
                cp = remote(src, dst, (ici_send, ici_recv), 3 * a + j, (px, py, c))
                cp.start()
                sent.append(cp)
        for a in range(ns):
            width = ins[a].shape[axes[a]]
            sx = split[a][2]
            for j, (px, py) in enumerate(others):
                theirs = _block(outs[a], axes[a], width, 2 * px + py, sx, outs[a].shape[sx], c)
                remote(theirs, theirs, (ici_send, ici_recv), 3 * a + j, (px, py, c)).wait_recv()
                fwd = remote(theirs, theirs, (d2d_send, d2d_recv), 3 * a + j, (x, y, 1 - c))
                fwd.start()
                sent.append(fwd)
        for a in range(ns, n):
            width = ins[a].shape[axes[a]]
            for j, (px, py) in enumerate(others):
                theirs = _block(outs[a], axes[a], width, 2 * px + py)
                remote(theirs, theirs, (ici_send, ici_recv), 3 * a + j, (px, py, c)).wait_recv()
        for a in range(ns):
            width = ins[a].shape[axes[a]]
            sx = split[a][2]
            for j, (px, py) in enumerate(others):
                from_sibling = _block(outs[a], axes[a], width, 2 * px + py, sx, outs[a].shape[sx], 1 - c)
                remote(from_sibling, from_sibling, (d2d_send, d2d_recv), 3 * a + j, (x, y, 1 - c)).wait_recv()
        for cp in sent:
            cp.wait_send()
        for cp in local:
            cp.wait()

    def whole_shape(a, axis):
        shape = list(a.shape)
        shape[axis] *= N_CHIPS
        return jax.ShapeDtypeStruct(tuple(shape), a.dtype)

    any_spec = pl.BlockSpec(memory_space=pl.ANY)
    return pl.pallas_call(
        body, name="gather_weights",
        in_specs=[any_spec] * n, out_specs=[any_spec] * n,
        out_shape=[whole_shape(a, ax) for a, ax in zip(arrs, axes)],
        scratch_shapes=[pltpu.SemaphoreType.DMA((3 * n,)), pltpu.SemaphoreType.DMA((3 * n,)),
                        pltpu.SemaphoreType.DMA((3 * ns,)), pltpu.SemaphoreType.DMA((3 * ns,)),
                        pltpu.SemaphoreType.DMA((n,)), pltpu.SemaphoreType.DMA((n,))],
        compiler_params=pltpu.CompilerParams(has_side_effects=True),
    )(*arrs)


class _Grad:
    def __init__(self, name, g, kind, rh, cols, groups=None):
        self.name, self.g, self.kind, self.rh, self.cols, self.groups = name, g, kind, rh, cols, groups
        if kind == 'rows':
            self.tr = rh
        elif kind == 'gate':
            self.tr = rh // (groups // 2)
        else:
            self.tr = rh if rh <= 512 else 256
        self.nb = rh // self.tr

    def pieces(self, ref, k, h):
        rh, cols = self.rh, self.cols
        if self.kind == 'cols':
            return [(ref.at[pl.ds(h * rh, rh), pl.ds(k * cols, cols)], 0, rh)]
        if self.kind == 'rows':
            return [(ref.at[pl.ds((2 * k + h) * rh, rh), :], 0, rh)]
        if self.kind == 'lead':
            return [(ref.at[k, pl.ds(h * rh, rh), :], 0, rh)]
        per = self.groups // 2
        return [(ref.at[pl.ds((((h * per + gi) * N_CHIPS) + k) * self.tr, self.tr), :], gi * self.tr, self.tr)
                for gi in range(per)]

    def block_spec(self):
        tr, nb, cols = self.tr, self.nb, self.cols
        if self.kind == 'cols':
            return pl.BlockSpec((tr, cols), lambda k, i, c: (c[0] * nb + i, k))
        if self.kind == 'rows':
            return pl.BlockSpec((tr, cols), lambda k, i, c: (2 * k + c[0], 0))
        if self.kind == 'lead':
            return pl.BlockSpec((None, tr, cols), lambda k, i, c: (k, c[0] * nb + i, 0))
        return pl.BlockSpec((tr, cols), lambda k, i, c: ((c[0] * nb + i) * N_CHIPS + k, 0))


class _Exchange:
    def __init__(self, name, arrays, out_shapes, n_copies, copies):
        self.name, self.arrays, self.out_shapes, self.n_copies, self.copies = name, arrays, out_shapes, n_copies, copies
        self.n_in, self.n_out = len(arrays), len(out_shapes)

    def specs(self, n):
        return [pl.BlockSpec(memory_space=pl.ANY)] * n

    def scratch(self):
        return [pltpu.SemaphoreType.DMA((self.n_copies,)), pltpu.SemaphoreType.DMA((self.n_copies,))]

    def _descriptors(self, in_refs, out_refs, sems):
        return self.copies(in_refs, out_refs, sems[0], sems[1])

    def start(self, in_refs, out_refs, sems):
        for cp in self._descriptors(in_refs, out_refs, sems):
            cp.start()

    def finish(self, in_refs, out_refs, sems):
        for cp in self._descriptors(in_refs, out_refs, sems):
            cp.wait()

    def __add__(self, other):
        def copies(ins, outs, send_sems, recv_sems, base=0):
            return (self.copies(ins[:self.n_in], outs[:self.n_out], send_sems, recv_sems, base)
                    + other.copies(ins[self.n_in:], outs[self.n_out:], send_sems, recv_sems, base + self.n_copies))

        return _Exchange(self.name + "_" + other.name, self.arrays + other.arrays, self.out_shapes + other.out_shapes,
                         self.n_copies + other.n_copies, copies)

    def run(self):
        def body(*refs):
            ins, outs, sems = refs[:self.n_in], refs[self.n_in:self.n_in + self.n_out], refs[self.n_in + self.n_out:]
            self.start(ins, outs, sems)
            self.finish(ins, outs, sems)

        return pl.pallas_call(
            body, name=self.name,
            in_specs=self.specs(self.n_in), out_specs=self.specs(self.n_out), out_shape=self.out_shapes,
            scratch_shapes=self.scratch(),
            compiler_params=pltpu.CompilerParams(has_side_effects=True),
        )(*self.arrays)


def _gather_whole(name, shards):
    def copies(ins, outs, send_sems, recv_sems, base=0):
        x, y, c = _position()
        me = 2 * x + y
        made = []
        for a, (_, axis) in enumerate(shards):
            dst = _block(outs[a], axis, ins[a].shape[axis], me)
            for j, to in enumerate([(x, y, 1 - c)] + [(px, py, c) for px, py in _other_chips(x, y)]):
                idx = base + 4 * a + j
                made.append(pltpu.make_async_remote_copy(
                    src_ref=ins[a], dst_ref=dst, send_sem=send_sems.at[idx], recv_sem=recv_sems.at[idx],
                    device_id=to, device_id_type=MESH))
        return made

    def whole_shape(a, axis):
        shape = list(a.shape)
        shape[axis] *= N_CHIPS
        return jax.ShapeDtypeStruct(tuple(shape), a.dtype)

    return _Exchange(name, [s[0] for s in shards], [whole_shape(*s) for s in shards], 4 * len(shards), copies)


def _halves_to_sibling(name, grads):
    total = sum(len(gr.pieces(gr.g, 0, 0)) * N_CHIPS for gr in grads)

    def copies(ins, outs, send_sems, recv_sems, base=0):
        x, y, c = _position()
        made = []
        for gr, g_ref, got_ref in zip(grads, ins, outs):
            for k in range(N_CHIPS):
                for src, r0, nr in gr.pieces(g_ref, k, 1 - c):
                    idx = base + len(made)
                    made.append(pltpu.make_async_remote_copy(
                        src_ref=src, dst_ref=got_ref.at[k, pl.ds(r0, nr), :],
                        send_sem=send_sems.at[idx], recv_sem=recv_sems.at[idx],
                        device_id=(x, y, 1 - c), device_id_type=MESH))
        return made

    return _Exchange(name, [gr.g for gr in grads],
                     [jax.ShapeDtypeStruct((N_CHIPS, gr.rh, gr.cols), F32) for gr in grads], total, copies)


def _chip_sum(gr, got, c, wire=BF16):
    def body(c_ref, g_ref, got_ref, o_ref):
        o_ref[...] = (g_ref[...] + got_ref[...]).astype(wire)

    tile = pl.BlockSpec((None, gr.tr, gr.cols), lambda k, i, c_ref: (k, i, 0))
    return pl.pallas_call(
        body, name="chip_sum_" + gr.name,
        grid_spec=pltpu.PrefetchScalarGridSpec(
            num_scalar_prefetch=1, grid=(N_CHIPS, gr.nb),
            in_specs=[gr.block_spec(), tile], out_specs=tile),
        out_shape=jax.ShapeDtypeStruct((N_CHIPS, gr.rh, gr.cols), wire),
        compiler_params=_params(("parallel", "parallel")),
    )(c, gr.g, got)


def _blocks_to_chips(name, parts):
    n = len(parts)

    def copies(ins, outs, send_sems, recv_sems, base=0):
        x, y, c = _position()
        made = []
        for a in range(n):
            for j, (px, py) in enumerate(_other_chips(x, y)):
                idx = base + 3 * a + j
                made.append(pltpu.make_async_remote_copy(
                    src_ref=ins[a].at[2 * px + py], dst_ref=outs[a].at[j],
                    send_sem=send_sems.at[idx], recv_sem=recv_sems.at[idx],
                    device_id=(px, py, c), device_id_type=MESH))
        return made

    return _Exchange(name, parts, [jax.ShapeDtypeStruct((3,) + p.shape[1:], p.dtype) for p in parts], 3 * n, copies)


def _sum_chips(name, part, got, me, c):
    nk, rh, cols = part.shape
    tr = rh if rh <= 512 else 256
    nb = rh // tr

    def body(me_ref, c_ref, own_ref, *rest):
        got_refs, o_ref = rest[:nk], rest[nk]
        own = own_ref[...].astype(F32)
        acc = None
        for k in range(nk):
            term = jnp.where(me_ref[0] == k, own, got_refs[k][...].astype(F32))
            acc = term if acc is None else acc + term
        o_ref[...] = acc

    def got_map(k):
        def index(i, me_ref, c_ref):
            xor = jnp.bitwise_xor(me_ref[0], k)
            slot = jnp.where(xor == 1, 1, jnp.where(xor == 3, 2, 0))
            return (slot, i, 0)
        return index

    return pl.pallas_call(
        body, name="sum_" + name,
        grid_spec=pltpu.PrefetchScalarGridSpec(
            num_scalar_prefetch=2, grid=(nb,),
            in_specs=[pl.BlockSpec((None, tr, cols), lambda i, me_ref, c_ref: (me_ref[0], i, 0))]
            + [pl.BlockSpec((None, tr, cols), got_map(k)) for k in range(nk)],
            out_specs=pl.BlockSpec((tr, cols), lambda i, me_ref, c_ref: (c_ref[0] * nb + i, 0))),
        out_shape=jax.ShapeDtypeStruct((2 * rh, cols), F32),
        compiler_params=_params(("parallel",)),
    )(me, c, part, *([got] * nk))


def _share_with_sibling(halves):
    n = len(halves)

    def body(*refs):
        outs = refs[n:2 * n]
        send_sems, recv_sems = refs[2 * n:]
        x, y, c = _position()
        copies = []
        for a in range(n):
            rh = outs[a].shape[0] // 2
            mine = outs[a].at[pl.ds(c * rh, rh), :]
            cp = pltpu.make_async_remote_copy(
                src_ref=mine, dst_ref=mine, send_sem=send_sems.at[a], recv_sem=recv_sems.at[a],
                device_id=(x, y, 1 - c), device_id_type=MESH)
            cp.start()
            copies.append(cp)
        for cp in copies:
            cp.wait()

    any_spec = pl.BlockSpec(memory_space=pl.ANY)
    return pl.pallas_call(
        body, name="grads_share",
        in_specs=[any_spec] * n, out_specs=[any_spec] * n,
        out_shape=[jax.ShapeDtypeStruct(h.shape, h.dtype) for h in halves],
        input_output_aliases={a: a for a in range(n)},
        scratch_shapes=[pltpu.SemaphoreType.DMA((n,)), pltpu.SemaphoreType.DMA((n,))],
        compiler_params=pltpu.CompilerParams(has_side_effects=True),
    )(*halves)


def _adamw(name, w, g, m, v):
    rows, cols = w.shape
    tr = 256 if rows % 256 == 0 else rows

    def body(*refs):
        _adamw_update(*refs)

    spec = pl.BlockSpec((tr, cols), lambda i: (i, 0))
    return pl.pallas_call(
        body, name=name, grid=(rows // tr,),
        in_specs=[spec] * 4, out_specs=[spec] * 4,
        out_shape=[jax.ShapeDtypeStruct((rows, cols), F32)] * 4,
        compiler_params=_params(("parallel",)),
    )(w, g, m, v)


def _adamw_update(w_ref, g_ref, m_ref, v_ref, g_out_ref, d_ref, nm_ref, nv_ref):
    gv = g_ref[...]
    g_out_ref[...] = gv
    mn = ADAM_B1 * m_ref[...] + (1.0 - ADAM_B1) * gv
    vn = ADAM_B2 * v_ref[...] + (1.0 - ADAM_B2) * (gv * gv)
    m_hat = mn / (1.0 - ADAM_B1 ** ADAM_STEP)
    v_hat = vn / (1.0 - ADAM_B2 ** ADAM_STEP)
    d_ref[...] = -ADAM_LR * (m_hat / (jnp.sqrt(v_hat) + ADAM_EPS) + ADAM_WD * w_ref[...])
    nm_ref[...] = mn
    nv_ref[...] = vn


def _adamw_vectors(tensors):
    n = len(tensors)

    def body(*refs):
        ins, outs = refs[:4 * n], refs[4 * n:]
        for t in range(n):
            _adamw_update(*ins[4 * t:4 * t + 4], *outs[4 * t:4 * t + 4])

    flat = [a for t in tensors for a in t]
    vmem = pl.BlockSpec(memory_space=pltpu.VMEM)
    out = pl.pallas_call(
        body, name="adamw_vectors",
        in_specs=[vmem] * (4 * n), out_specs=[vmem] * (4 * n),
        out_shape=[jax.ShapeDtypeStruct(t[0].shape, F32) for t in tensors for _ in range(4)],
        compiler_params=_params(),
    )(*flat)
    return [out[4 * t:4 * t + 4] for t in range(n)]


def _as2d(a):
    if a.ndim == 1:
        return a.reshape(1, -1)
    return a.reshape(-1, a.shape[-1])


def _rope_tables(tp, pad):
    pos = jnp.arange(tp, dtype=F32) - pad
    inv_freq = ROPE_BASE ** (-jnp.arange(0, QK_ROPE, 2, dtype=F32) / QK_ROPE)
    ang = pos[:, None] * inv_freq[None, :]
    cos, sin = jnp.cos(ang), jnp.sin(ang)
    zeros = jnp.zeros((tp, LANES - QK_ROPE), F32)
    return jnp.concatenate([cos, cos, zeros], axis=1), jnp.concatenate([-sin, sin, zeros], axis=1)


def _matrix_grad(name, g, heads):
    if name in ('b_w_rg', 'b_w_ig'):
        groups, blk, cols = g.shape
        return _Grad(name, g.reshape(groups * blk, cols), 'gate', (groups // 2) * (blk // N_CHIPS), cols, groups)
    rows, cols = g.shape
    if name in ('a_w_in', 'a_w_out', 'b_w_out'):
        return _Grad(name, g, 'rows', rows // (2 * N_CHIPS), cols)
    if name == 'a_w_uq' and heads % N_CHIPS != 0:
        g = g.reshape(rows, heads, HEAD_W)[:, :, :QK_NOPE + QK_ROPE].reshape(rows, -1)
        cols = g.shape[1]
        g = jnp.moveaxis(g.reshape(rows, N_CHIPS, cols // N_CHIPS), 1, 0)
        return _Grad(name, g, 'lead', rows // 2, cols // N_CHIPS)
    return _Grad(name, g, 'cols', rows // 2, cols // N_CHIPS)


def _kernel_form(name, w):
    return w[0] if name in ('b_w_rg', 'b_w_ig', 'b_conv_w') else _as2d(w)


def _local_grads(x, target, wt, heads, c_idx, mid_names, mid_gather, late_names, late_gather):
    wt = dict(wt)
    seq, d = x.shape
    n_meta = wt['meta_tokens'].shape[0]
    t = seq + n_meta
    pad = (-t) % Q_BLOCK
    tp = t + pad
    x0 = pad + n_meta
    tm = _row_tile(tp)
    ql = wt['a_q_norm_g'].shape[1]
    kvl = wt['a_kv_norm_g'].shape[1]
    mla_w = heads * V_HEAD

    cos_t, sin_t = _rope_tables(tp, pad)

    w_in_a = wt['a_w_in']
    zrow = jnp.zeros((LANES - QK_ROPE, d), BF16)
    w_in_a = jnp.concatenate([w_in_a[:ql + kvl + QK_ROPE], zrow, w_in_a[ql + kvl + QK_ROPE:]], axis=0)
    c_kv, c_kr, c_gate = ql, ql + kvl, ql + kvl + LANES
    splits_a = [(0, c_kv), (c_kv, c_kr), (c_kr, c_gate), (c_gate, c_gate + mla_w)]

    h0, q_lat, kv_lat, kr_raw, gate_a, *mid_whole = _tokens_in_proj(
        "a_in_proj", x, wt['meta_tokens'], wt['a_norm_g'], w_in_a, splits_a, tm, pad, exchange=mid_gather)
    wt.update({n: _kernel_form(n, w) for n, w in zip(mid_names, mid_whole)})
    w_uq = wt['a_w_uq'].reshape(ql, heads, QK_NOPE + QK_ROPE)
    w_uq = jnp.pad(w_uq, ((0, 0), (0, 0), (0, HEAD_W - QK_NOPE - QK_ROPE))).reshape(ql, heads * HEAD_W)
    w_ukv = wt['a_w_ukv']
    q = _q_proj(q_lat, wt['a_q_norm_g'], w_uq, cos_t, sin_t, heads, tm)
    k, v = _kv_proj(kv_lat, wt['a_kv_norm_g'], w_ukv, kr_raw, cos_t, sin_t, heads, tm)
    attn, lse, *late_whole = _flash_fwd(q, k, v, heads, pad, tm, exchange=late_gather)
    wt.update({n: _kernel_form(n, w) for n, w in zip(late_names, late_whole)})
    lru_w = wt['b_conv_w'].shape[1]

    h1, u, gate_b = _out_proj_in_proj("a_out_b_in_proj", attn, gate_a, wt['a_w_out'], h0, wt['b_norm_g'], wt['b_w_in'],
                                      [(0, lru_w), (lru_w, 2 * lru_w)], tm)
    uc, r, ig, hs, decay, mult = _rglru_fwd(u, wt['b_conv_w'], wt['b_conv_b'], wt['b_w_rg'], wt['b_b_rg'],
                                            wt['b_w_ig'], wt['b_b_ig'], wt['b_lam'], pad, tm)

    dh2, loss, d_final_g, dhs, dgate_b, dw_out_b = _out_proj_loss(
        hs, gate_b, wt['b_w_out'], h1, wt['final_norm_g'], target, x0, tm)
    du, dconv_w, dconv_b, dw_rg, db_rg, dw_ig, db_ig, dlam = _rglru_bwd(
        dhs, hs, r, ig, uc, u, decay, mult, wt['b_conv_w'], wt['b_w_rg'], wt['b_w_ig'], wt['b_lam'], pad, tm)
    dh1, dw_in_b, dg_b = _norm_matmul_bwd("b_in_proj_bwd", h1, wt['b_norm_g'], wt['b_w_in'], [du, dgate_b], tm, resid=dh2)

    grads_b = [_matrix_grad(n, g, heads) for n, g in
               (('b_w_in', dw_in_b), ('b_w_rg', dw_rg), ('b_w_ig', dw_ig), ('b_w_out', dw_out_b))]
    dattn, dgate_a, dw_out_a, delta, *got = _gated_out_bwd(
        "a_out_proj_bwd", dh1, attn, gate_a, wt['a_w_out'], tm, delta_heads=heads,
        exchange=_halves_to_sibling("swap_b", grads_b))
    sums_b = [_chip_sum(gr, r, c_idx) for gr, r in zip(grads_b, got)]
    grad_out = _matrix_grad('a_w_out', dw_out_a, heads)
    dq, dk, dv, *landed = _flash_bwd(
        q, k, v, lse, delta, dattn, heads, pad, tm,
        exchange=_blocks_to_chips("chips_b", sums_b) + _halves_to_sibling("swap_out", [grad_out]))
    through = list(zip(grads_b, sums_b, landed[:len(grads_b)]))
    sum_out = _chip_sum(grad_out, landed[len(grads_b)], c_idx)

    def q_prologue(dy_refs, dy_s, ex_ref):
        (dq_ref,), cos_v, sin_v = dy_refs[:1], dy_refs[1][...], dy_refs[2][...]
        for h in range(heads):
            c0 = h * HEAD_W
            dy_s[:, c0:c0 + QK_NOPE] = dq_ref[:, c0:c0 + QK_NOPE].astype(BF16)
            dy_s[:, c0 + QK_NOPE:c0 + HEAD_W] = _unrope(dq_ref[:, c0 + QK_NOPE:c0 + HEAD_W], cos_v, sin_v).astype(BF16)

    dq_lat, dw_uq, dg_q, from_chips_out = _norm_matmul_bwd(
        "a_q_proj_bwd", q_lat, wt['a_q_norm_g'], w_uq, [dq, cos_t, sin_t], tm, prologue=q_prologue, dx_dtype=BF16,
        exchange=_blocks_to_chips("chips_out", [sum_out]))
    through.append((grad_out, sum_out, from_chips_out))
    grad_uq = _matrix_grad('a_w_uq', dw_uq, heads)

    def kv_prologue(dy_refs, dy_s, ex_ref):
        dk_ref, dv_ref = dy_refs[:2]
        cos_v, sin_v = dy_refs[2][...], dy_refs[3][...]
        dkr = jnp.zeros((dk_ref.shape[0], LANES), F32)
        for h in range(heads):
            c0 = h * (QK_NOPE + V_HEAD)
            dy_s[:, c0:c0 + QK_NOPE] = dk_ref[:, h * HEAD_W:h * HEAD_W + QK_NOPE].astype(BF16)
            dy_s[:, c0 + QK_NOPE:c0 + QK_NOPE + V_HEAD] = dv_ref[:, h * V_HEAD:(h + 1) * V_HEAD].astype(BF16)
            dkr = dkr + dk_ref[:, h * HEAD_W + QK_NOPE:(h + 1) * HEAD_W]
        ex_ref[...] = _unrope(dkr, cos_v, sin_v).astype(ex_ref.dtype)

    dkv_lat, dw_ukv, dg_kv, dkr_raw, got_uq = _norm_matmul_bwd(
        "a_kv_proj_bwd", kv_lat, wt['a_kv_norm_g'], w_ukv, [dk, dv, cos_t, sin_t], tm,
        prologue=kv_prologue, extra_out=(LANES, BF16), dx_dtype=BF16, exchange=_halves_to_sibling("swap_uq", [grad_uq]))
    sum_uq = _chip_sum(grad_uq, got_uq, c_idx)
    grad_ukv = _matrix_grad('a_w_ukv', dw_ukv, heads)

    dh0, dw_in_a, dg_a, from_chips_uq, got_ukv = _norm_matmul_bwd(
        "a_in_proj_bwd", h0, wt['a_norm_g'], w_in_a, [dq_lat, dkv_lat, dkr_raw, dgate_a], tm, resid=dh1, transposed=True,
        exchange=_blocks_to_chips("chips_uq", [sum_uq]) + _halves_to_sibling("swap_ukv", [grad_ukv]))
    through.append((grad_uq, sum_uq, from_chips_uq))
    swapped = [(grad_ukv, _chip_sum(grad_ukv, got_ukv, c_idx))]

    dw_in_a = jnp.concatenate([dw_in_a[:c_kr + QK_ROPE], dw_in_a[c_gate:]], axis=0)
    grads = {
        'meta_tokens': dh0[pad:x0], 'a_norm_g': dg_a, 'a_w_in': dw_in_a, 'a_q_norm_g': dg_q, 'a_kv_norm_g': dg_kv,
        'a_w_uq': dw_uq, 'a_w_ukv': dw_ukv, 'a_w_out': dw_out_a, 'b_norm_g': dg_b, 'b_w_in': dw_in_b,
        'b_conv_w': dconv_w, 'b_conv_b': dconv_b, 'b_w_rg': dw_rg, 'b_b_rg': db_rg, 'b_w_ig': dw_ig,
        'b_b_ig': db_ig, 'b_lam': dlam, 'b_w_out': dw_out_b, 'final_norm_g': d_final_g,
    }
    return loss, dh0[x0:], grads, through, swapped


def _chip_major(whole, local_shape, axis):
    if axis is None:
        return jnp.broadcast_to(whole.reshape(1, -1), (N_CHIPS, whole.size))
    shape = list(local_shape)
    g = whole.reshape(shape[:axis] + [N_CHIPS, shape[axis]] + shape[axis + 1:])
    return jnp.moveaxis(g, axis, 0).reshape(N_CHIPS, -1)


def kernel(x, meta_tokens, a_norm_g, a_w_in, a_q_norm_g, a_kv_norm_g, a_w_uq, a_w_ukv, a_w_out, b_norm_g, b_w_in, b_conv_w, b_conv_b, b_w_rg, b_b_rg, b_w_ig, b_b_ig, b_lam, b_w_out, final_norm_g, loss_target, m_meta_tokens, m_a_norm_g, m_a_w_in, m_a_q_norm_g, m_a_kv_norm_g, m_a_w_uq, m_a_w_ukv, m_a_w_out, m_b_norm_g, m_b_w_in, m_b_conv_w, m_b_conv_b, m_b_w_rg, m_b_b_rg, m_b_w_ig, m_b_b_ig, m_b_lam, m_b_w_out, m_final_norm_g, v_meta_tokens, v_a_norm_g, v_a_w_in, v_a_q_norm_g, v_a_kv_norm_g, v_a_w_uq, v_a_w_ukv, v_a_w_out, v_b_norm_g, v_b_w_in, v_b_conv_w, v_b_conv_b, v_b_w_rg, v_b_b_rg, v_b_w_ig, v_b_b_ig, v_b_lam, v_b_w_out, v_final_norm_g):
    local_w = dict(zip(WEIGHTS, (meta_tokens, a_norm_g, a_w_in, a_q_norm_g, a_kv_norm_g, a_w_uq, a_w_ukv, a_w_out,
                                 b_norm_g, b_w_in, b_conv_w, b_conv_b, b_w_rg, b_b_rg, b_w_ig, b_b_ig, b_lam,
                                 b_w_out, final_norm_g)))
    local_m = dict(zip(WEIGHTS, (m_meta_tokens, m_a_norm_g, m_a_w_in, m_a_q_norm_g, m_a_kv_norm_g, m_a_w_uq,
                                 m_a_w_ukv, m_a_w_out, m_b_norm_g, m_b_w_in, m_b_conv_w, m_b_conv_b, m_b_w_rg,
                                 m_b_b_rg, m_b_w_ig, m_b_b_ig, m_b_lam, m_b_w_out, m_final_norm_g)))
    local_v = dict(zip(WEIGHTS, (v_meta_tokens, v_a_norm_g, v_a_w_in, v_a_q_norm_g, v_a_kv_norm_g, v_a_w_uq,
                                 v_a_w_ukv, v_a_w_out, v_b_norm_g, v_b_w_in, v_b_conv_w, v_b_conv_b, v_b_w_rg,
                                 v_b_b_rg, v_b_w_ig, v_b_b_ig, v_b_lam, v_b_w_out, v_final_norm_g)))
    matrices = ('a_w_in', 'a_w_uq', 'a_w_ukv', 'a_w_out', 'b_w_in', 'b_w_rg', 'b_w_ig', 'b_w_out')
    heads = a_w_uq.shape[-1] * N_CHIPS // (QK_NOPE + QK_ROPE)

    def transposed(a):
        return jnp.swapaxes(a, 1, 2)

    split, small, mid, late = [], [], [], []
    for n in WEIGHTS:
        if SHARD_AXIS[n] is None:
            continue
        if n.startswith('b_') or n == 'a_w_out':
            late.append((n, local_w[n].astype(BF16) if n in matrices else local_w[n], SHARD_AXIS[n]))
        elif n == 'a_w_in':
            split.append((n, transposed(local_w[n]).astype(BF16), 1, 2))
        elif n in matrices:
            mid.append((n, local_w[n].astype(BF16), SHARD_AXIS[n]))
        else:
            small.append((n, local_w[n], SHARD_AXIS[n]))
    gathered = _gather_weights([s[1:] for s in split], [s[1:] for s in small])
    whole = dict(zip([s[0] for s in split + small], gathered))
    mid_names, late_names = [s[0] for s in mid], [s[0] for s in late]
    wt = {n: _kernel_form(n, whole.get(n, local_w[n])) for n in WEIGHTS if n not in mid_names + late_names}

    c_idx = lax.axis_index("c").astype(jnp.int32).reshape(1)
    me_idx = (2 * lax.axis_index("x") + lax.axis_index("y")).astype(jnp.int32).reshape(1)
    loss, grad_x, grads, through, swapped = _local_grads(
        x[0], loss_target[0], wt, heads, c_idx,
        mid_names, _gather_whole("gather_weights_a", [s[1:] for s in mid]),
        late_names, _gather_whole("gather_weights_b", [s[1:] for s in late]))

    ext_uq = heads % N_CHIPS == 0
    started = [gr.name for gr, *_ in through + swapped]
    last = [_matrix_grad(n, grads[n], heads) for n in matrices if n not in started]
    rest = [n for n in WEIGHTS if n not in matrices]
    pieces = [_chip_major(grads[n], local_w[n].shape, SHARD_AXIS[n]) for n in rest]
    pieces.append(jnp.broadcast_to(loss[0:1, 0:1], (N_CHIPS, 1)))
    length = sum(p.shape[1] for p in pieces)
    unit = 2 * SUBLANES * 1024
    padded = -(-length // unit) * unit
    flat = jnp.concatenate(pieces + [jnp.zeros((N_CHIPS, padded - length), F32)], axis=1)
    last.append(_Grad('small', flat.reshape(N_CHIPS, padded // 1024, 1024), 'lead', padded // 2048, 1024))

    got = _halves_to_sibling("grads_to_sibling", last).run()
    swapped = swapped + [(gr, _chip_sum(gr, r, c_idx, F32 if gr.name == 'small' else BF16))
                         for gr, r in zip(last, got)]
    from_chips = _blocks_to_chips("grads_to_chips", [p for _, p in swapped]).run()
    through = through + [(gr, p, r) for (gr, p), r in zip(swapped, from_chips)]
    halves = [_sum_chips(gr.name, p, r, me_idx, c_idx) for gr, p, r in through]
    summed = dict(zip([gr.name for gr, _, _ in through], _share_with_sibling(halves)))
    if ext_uq:
        g = summed['a_w_uq']
        summed['a_w_uq'] = g.reshape(g.shape[0], -1, HEAD_W)[:, :, :QK_NOPE + QK_ROPE]
    total = summed['small'].reshape(-1)

    operands, views = {}, {}
    off = 0
    for n in WEIGHTS:
        shape = local_w[n].shape
        view = transposed if n == 'a_w_in' else (lambda a: a)
        if n in matrices:
            g = summed[n].reshape(view(local_w[n]).shape)
        else:
            size = 1
            for s in shape:
                size *= s
            g = total[off:off + size].reshape(shape)
            off += size
        operands[n] = (_as2d(view(local_w[n])), _as2d(g), _as2d(view(local_m[n])), _as2d(view(local_v[n])))
        views[n] = view
    results = {n: _adamw("adamw_" + n, *operands[n]) for n in WEIGHTS if n in matrices}
    vectors = [n for n in WEIGHTS if n not in matrices]
    results.update(zip(vectors, _adamw_vectors([operands[n] for n in vectors])))
    out_g, out_d, out_m, out_v = [], [], [], []
    for n in WEIGHTS:
        for out, r in zip((out_g, out_d, out_m, out_v), results[n]):
            out.append(views[n](r.reshape(views[n](local_w[n]).shape)))

    return (total[off], grad_x[None], *out_g, *out_d, *out_m, *out_v)
```

```python
import functools

import jax
import jax.numpy as jnp
import numpy as np
from jax import lax
from jax.experimental import pallas as pl
from jax.experimental.pallas import tpu as pltpu

F32 = jnp.float32
BF16 = jnp.bfloat16
MESH = pl.DeviceIdType.MESH

RMS_EPS = 1e-6
QK_NOPE = 128
QK_ROPE = 64
V_HEAD = 128
HEAD_W = 256
ROPE_BASE = 10000.0
Q_BLOCK = 128
MASK_VALUE = -1e30
CONV_WIDTH = 4
LRU_C = 8.0
N_CHIPS = 4

ADAM_LR = 0.001
ADAM_B1 = 0.9
ADAM_B2 = 0.999
ADAM_EPS = 1e-08
ADAM_WD = 0.01
ADAM_STEP = 10

VMEM_LIMIT_V7X = 56 * 1024 * 1024
LANES = 128
SUBLANES = 8

WEIGHTS = ['meta_tokens', 'a_norm_g', 'a_w_in', 'a_q_norm_g', 'a_kv_norm_g', 'a_w_uq', 'a_w_ukv',
           'a_w_out', 'b_norm_g', 'b_w_in', 'b_conv_w', 'b_conv_b', 'b_w_rg', 'b_b_rg', 'b_w_ig',
           'b_b_ig', 'b_lam', 'b_w_out', 'final_norm_g']
SHARD_AXIS = {'meta_tokens': 1, 'a_norm_g': None, 'a_w_in': 2, 'a_q_norm_g': None, 'a_kv_norm_g': None,
              'a_w_uq': 2, 'a_w_ukv': 2, 'a_w_out': 1, 'b_norm_g': 1, 'b_w_in': 2, 'b_conv_w': 2,
              'b_conv_b': 1, 'b_w_rg': 2, 'b_b_rg': 1, 'b_w_ig': 2, 'b_b_ig': 1, 'b_lam': 1,
              'b_w_out': 1, 'final_norm_g': None}


def _params(sem=None):
    return pltpu.CompilerParams(dimension_semantics=sem, vmem_limit_bytes=VMEM_LIMIT_V7X)


def _row_tile(tp):
    return 384 if (tp % 384 == 0 and tp >= 1152) else 128


def _sigmoid(x):
    return 1.0 / (1.0 + jnp.exp(-x))


def _rms(x):
    return lax.rsqrt(jnp.mean(x * x, axis=-1, keepdims=True) + RMS_EPS)


def _swap32(x):
    lane = lax.broadcasted_iota(jnp.int32, x.shape, 1)
    return jnp.where(lane < 32, pltpu.roll(x, 96, 1), pltpu.roll(x, 32, 1))


def _rope(x, cos_t, sin_t):
    return x * cos_t + _swap32(x) * sin_t


def _unrope(d, cos_t, sin_t):
    lane = lax.broadcasted_iota(jnp.int32, d.shape, 1)
    return jnp.where(lane < QK_ROPE, d * cos_t + _swap32(d * sin_t), 0.0)


def _dot(a, b):
    return jnp.dot(a, b, preferred_element_type=F32)


def _dot_nt(a, b):
    return lax.dot_general(a, b, (((1,), (1,)), ((), ())), preferred_element_type=F32)


def _dot_tn(a, b):
    return lax.dot_general(a, b, (((0,), (0,)), ((), ())), preferred_element_type=F32)


def _tokens_in_proj(name, tokens, meta, g, wt, splits, tm, pad, exchange=None):
    seq, kin = tokens.shape
    x0 = pad + meta.shape[0]
    tp = x0 + seq
    n = wt.shape[0]
    nt = tp // tm
    assert x0 <= tm and x0 % SUBLANES == 0 and pad % SUBLANES == 0 and nt >= 2

    def compute(x_hbm, meta_ref, g_ref, w_ref, h_ref, *rest):
        outs, (tile_s, sems) = rest[:len(splits)], rest[len(splits):]
        i = pl.program_id(0)

        def first_rows():
            return pltpu.make_async_copy(x_hbm.at[pl.ds(0, tm - x0), :], tile_s.at[0, pl.ds(x0, tm - x0), :], sems.at[0])

        def rows_of(t):
            return pltpu.make_async_copy(x_hbm.at[pl.ds(pl.multiple_of(t * tm - x0, SUBLANES), tm), :],
                                         tile_s.at[t % 2], sems.at[t % 2])

        @pl.when(i == 0)
        def _():
            tile_s[0, 0:pad, :] = jnp.zeros((pad, kin), F32)
            tile_s[0, pad:x0, :] = meta_ref[...]
            if x0 < tm:
                first_rows().start()
                first_rows().wait()

        pl.when(i > 0)(lambda: rows_of(i).wait())
        pl.when(i + 1 < nt)(lambda: rows_of(i + 1).start())

        xv = tile_s[i % 2]
        h_ref[...] = xv
        nrm = ((xv * _rms(xv)) * g_ref[...]).astype(BF16)
        y = _dot_nt(nrm, w_ref[...])
        for o_ref, (c0, c1) in zip(outs, splits):
            o_ref[...] = y[:, c0:c1]

    in_specs = [pl.BlockSpec(memory_space=pl.ANY),
                pl.BlockSpec(meta.shape, lambda i: (0, 0)),
                pl.BlockSpec((1, kin), lambda i: (0, 0)),
                pl.BlockSpec((n, kin), lambda i: (0, 0))]
    out_specs = [pl.BlockSpec((tm, kin), lambda i: (i, 0))]
    out_specs += [pl.BlockSpec((tm, c1 - c0), lambda i: (i, 0)) for c0, c1 in splits]
    out_shape = [jax.ShapeDtypeStruct((tp, kin), F32)]
    out_shape += [jax.ShapeDtypeStruct((tp, c1 - c0), F32) for c0, c1 in splits]
    args, scratch = [tokens, meta, g, wt], [pltpu.VMEM((2, tm, kin), F32), pltpu.SemaphoreType.DMA((2,))]
    body = _with_exchange(exchange, len(in_specs), len(out_specs),
                          lambda: pl.program_id(0) == 0, lambda: pl.program_id(0) == nt - 1, compute)
    if exchange is not None:
        in_specs = in_specs + exchange.specs(exchange.n_in)
        out_specs = out_specs + exchange.specs(exchange.n_out)
        out_shape = out_shape + exchange.out_shapes
        args, scratch = args + exchange.arrays, scratch + exchange.scratch()
    return pl.pallas_call(
        body, name=name, grid=(nt,),
        in_specs=in_specs, out_specs=out_specs, out_shape=out_shape, scratch_shapes=scratch,
        compiler_params=_params(("arbitrary",)),
    )(*args)


def _q_proj(q_lat, g, w_uq, cos_t, sin_t, heads, tm):
    tp, kin = q_lat.shape
    n = heads * HEAD_W

    def body(x_ref, g_ref, w_ref, cos_ref, sin_ref, q_ref):
        xv = x_ref[...]
        nrm = ((xv * _rms(xv)) * g_ref[...]).astype(BF16)
        y = _dot(nrm, w_ref[...])
        cos_v, sin_v = cos_ref[...], sin_ref[...]
        for h in range(heads):
            c0 = h * HEAD_W
            q_ref[:, c0:c0 + QK_NOPE] = y[:, c0:c0 + QK_NOPE].astype(BF16)
            q_ref[:, c0 + QK_NOPE:c0 + HEAD_W] = _rope(y[:, c0 + QK_NOPE:c0 + HEAD_W], cos_v, sin_v).astype(BF16)

    return pl.pallas_call(
        body, name="a_q_proj", grid=(tp // tm,),
        in_specs=[pl.BlockSpec((tm, kin), lambda i: (i, 0)),
                  pl.BlockSpec((1, kin), lambda i: (0, 0)),
                  pl.BlockSpec((kin, n), lambda i: (0, 0)),
                  pl.BlockSpec((tm, LANES), lambda i: (i, 0)),
                  pl.BlockSpec((tm, LANES), lambda i: (i, 0))],
        out_specs=pl.BlockSpec((tm, n), lambda i: (i, 0)),
        out_shape=jax.ShapeDtypeStruct((tp, n), BF16),
        compiler_params=_params(("parallel",)),
    )(q_lat, g, w_uq, cos_t, sin_t)


def _kv_proj(kv_lat, g, w_ukv, k_rope_raw, cos_t, sin_t, heads, tm):
    tp, kin = kv_lat.shape
    n = heads * (QK_NOPE + V_HEAD)

    def body(x_ref, g_ref, w_ref, kr_ref, cos_ref, sin_ref, k_ref, v_ref):
        xv = x_ref[...]
        nrm = ((xv * _rms(xv)) * g_ref[...]).astype(BF16)
        y = _dot(nrm, w_ref[...])
        kr = _rope(kr_ref[...], cos_ref[...], sin_ref[...]).astype(BF16)
        for h in range(heads):
            c0 = h * (QK_NOPE + V_HEAD)
            k_ref[:, h * HEAD_W:h * HEAD_W + QK_NOPE] = y[:, c0:c0 + QK_NOPE].astype(BF16)
            k_ref[:, h * HEAD_W + QK_NOPE:(h + 1) * HEAD_W] = kr
            v_ref[:, h * V_HEAD:(h + 1) * V_HEAD] = y[:, c0 + QK_NOPE:c0 + QK_NOPE + V_HEAD].astype(BF16)

    return pl.pallas_call(
        body, name="a_kv_proj", grid=(tp // tm,),
        in_specs=[pl.BlockSpec((tm, kin), lambda i: (i, 0)),
                  pl.BlockSpec((1, kin), lambda i: (0, 0)),
                  pl.BlockSpec((kin, n), lambda i: (0, 0)),
                  pl.BlockSpec((tm, LANES), lambda i: (i, 0)),
                  pl.BlockSpec((tm, LANES), lambda i: (i, 0)),
                  pl.BlockSpec((tm, LANES), lambda i: (i, 0))],
        out_specs=[pl.BlockSpec((tm, heads * HEAD_W), lambda i: (i, 0)),
                   pl.BlockSpec((tm, heads * V_HEAD), lambda i: (i, 0))],
        out_shape=[jax.ShapeDtypeStruct((tp, heads * HEAD_W), BF16),
                   jax.ShapeDtypeStruct((tp, heads * V_HEAD), BF16)],
        compiler_params=_params(("parallel",)),
    )(kv_lat, g, w_ukv, k_rope_raw, cos_t, sin_t)


def _as_rows(col):
    rows = col.shape[0]
    return jnp.transpose(jnp.broadcast_to(col, (rows, LANES)))[0:SUBLANES, :]


def _attn_mask(row0, col0, rows, cols, pad):
    row = row0 + lax.broadcasted_iota(jnp.int32, (rows, cols), 0)
    col = col0 + lax.broadcasted_iota(jnp.int32, (rows, cols), 1)
    return (col <= row) & (col >= pad)


LOG2E = 1.4426950408889634
FLASH_FWD_TRIPS = ((4, 2), (2, 2), (1, 1))


def _flash_fwd(q, k, v, heads, pad, tq, exchange=None):
    tp = q.shape[0]
    nq = tp // tq
    c2 = (QK_NOPE + QK_ROPE) ** -0.5 * LOG2E

    def compute(q_ref, k_ref, v_ref, o_ref, lse_ref):
        i = pl.program_id(1)

        def make_step(masked, blocks, parts=1):
            keys = blocks * tq // parts

            def step(j, carry):
                m, l, acc = carry
                offs = [pl.multiple_of(j * tq + part * keys, tq) for part in range(parts)]
                scores = [_dot_nt(q_ref[...], k_ref[pl.ds(off, keys), :]) for off in offs]
                for off, s in zip(offs, scores):
                    s = s * c2
                    if masked:
                        s = jnp.where(_attn_mask(i * tq, j * tq, tq, keys, pad), s, MASK_VALUE)
                    m_new = jnp.maximum(m, jnp.max(s, axis=-1, keepdims=True))
                    p = jnp.exp2(s - m_new)
                    alpha = jnp.exp2(m - m_new)
                    l = alpha * l + jnp.sum(p, axis=-1, keepdims=True)
                    acc = alpha * acc + _dot(p.astype(BF16), v_ref[pl.ds(off, keys), :])
                    m = m_new
                return m, l, acc
            return step

        init = (jnp.full((tq, 1), MASK_VALUE, F32), jnp.zeros((tq, 1), F32), jnp.zeros((tq, V_HEAD), F32))
        carry = make_step(True, 1)(0, init)
        first = 1
        for blocks, parts in FLASH_FWD_TRIPS:
            trips = jnp.maximum(i - first, 0) // blocks
            step_n = make_step(False, blocks, parts)
            carry = lax.fori_loop(0, trips, lambda t, cr, f=first, b=blocks, s=step_n: s(f + b * t, cr), carry)
            first = first + blocks * trips
        m, l, acc = lax.fori_loop(jnp.maximum(i, 1), i + 1, make_step(True, 1), carry)
        o_ref[...] = acc / l
        lse_ref[...] = _as_rows(m + jnp.log(l) * LOG2E)

    in_specs = [pl.BlockSpec((tq, HEAD_W), lambda h, i: (i, h)),
                pl.BlockSpec((tp, HEAD_W), lambda h, i: (0, h)),
                pl.BlockSpec((tp, V_HEAD), lambda h, i: (0, h))]
    out_specs = [pl.BlockSpec((tq, V_HEAD), lambda h, i: (i, h)),
                 pl.BlockSpec((None, None, SUBLANES, tq), lambda h, i: (h, i, 0, 0))]
    out_shape = [jax.ShapeDtypeStruct((tp, heads * V_HEAD), F32),
                 jax.ShapeDtypeStruct((heads, nq, SUBLANES, tq), F32)]
    args, scratch = [q, k, v], []
    body = _with_exchange(exchange, len(in_specs), len(out_specs),
                          lambda: (pl.program_id(0) == 0) & (pl.program_id(1) == 0),
                          lambda: (pl.program_id(0) == heads - 1) & (pl.program_id(1) == nq - 1), compute)
    if exchange is not None:
        in_specs = in_specs + exchange.specs(exchange.n_in)
        out_specs = out_specs + exchange.specs(exchange.n_out)
        out_shape = out_shape + exchange.out_shapes
        args, scratch = args + exchange.arrays, exchange.scratch()
    return pl.pallas_call(
        body, name="a_flash_fwd", grid=(heads, nq),
        in_specs=in_specs, out_specs=out_specs, out_shape=out_shape, scratch_shapes=scratch,
        compiler_params=_params(("arbitrary", "arbitrary")),
    )(*args)


def _out_proj_in_proj(name, a, gate, w_out, resid, g, w_in, splits, tm):
    tp, wd = a.shape
    d = w_out.shape[1]
    n = w_in.shape[1]

    def body(a_ref, gate_ref, wo_ref, res_ref, g_ref, wi_ref, h_ref, *outs):
        gv = gate_ref[...]
        y = (a_ref[...] * (gv * _sigmoid(gv))).astype(BF16)
        h = res_ref[...] + _dot(y, wo_ref[...])
        h_ref[...] = h
        nrm = ((h * _rms(h)) * g_ref[...]).astype(BF16)
        z = _dot(nrm, wi_ref[...])
        for o_ref, (c0, c1) in zip(outs, splits):
            o_ref[...] = z[:, c0:c1]

    return pl.pallas_call(
        body, name=name, grid=(tp // tm,),
        in_specs=[pl.BlockSpec((tm, wd), lambda i: (i, 0)),
                  pl.BlockSpec((tm, wd), lambda i: (i, 0)),
                  pl.BlockSpec((wd, d), lambda i: (0, 0)),
                  pl.BlockSpec((tm, d), lambda i: (i, 0)),
                  pl.BlockSpec((1, d), lambda i: (0, 0)),
                  pl.BlockSpec((d, n), lambda i: (0, 0))],
        out_specs=[pl.BlockSpec((tm, d), lambda i: (i, 0))]
        + [pl.BlockSpec((tm, c1 - c0), lambda i: (i, 0)) for c0, c1 in splits],
        out_shape=[jax.ShapeDtypeStruct((tp, d), F32)]
        + [jax.ShapeDtypeStruct((tp, c1 - c0), F32) for c0, c1 in splits],
        compiler_params=_params(("parallel",)),
    )(a, gate, w_out, resid, g, w_in)


def _lru_decay(r, sp):
    log_a = -LRU_C * r * sp
    a = jnp.exp(log_a)
    e2 = a * a
    x2 = 2.0 * log_a
    series = x2 * (1.0 + x2 * (0.5 + x2 * (1.0 / 6.0)))
    em1 = jnp.where(x2 > -0.02, series, e2 - 1.0)
    return a, e2, jnp.sqrt(-em1)


def _softplus(x):
    return jnp.maximum(x, 0.0) + jnp.log1p(jnp.exp(-jnp.abs(x)))


def _rglru_fwd(u, conv_w, conv_b, w_rg, b_rg, w_ig, b_ig, lam, pad, tm):
    tp, w = u.shape
    groups, blk = w_rg.shape[0], w_rg.shape[1]

    def body(u_ref, cw_ref, cb_ref, wr_ref, br_ref, wi_ref, bi_ref, lam_ref,
             uc_ref, r_ref, ig_ref, hs_ref, a_s, mult_ref, uext, b_s, hc):
        i = pl.program_id(0)

        @pl.when(i == 0)
        def _():
            uext[0:SUBLANES, :] = jnp.zeros((SUBLANES, w), F32)
            hc[...] = jnp.zeros((SUBLANES, w), F32)

        uext[SUBLANES:SUBLANES + tm, :] = u_ref[...]
        cw = cw_ref[...]
        uc = cb_ref[...] + uext[pl.ds(SUBLANES - 3, tm), :] * cw[0:1, :]
        uc = uc + uext[pl.ds(SUBLANES - 2, tm), :] * cw[1:2, :]
        uc = uc + uext[pl.ds(SUBLANES - 1, tm), :] * cw[2:3, :]
        uc = uc + uext[pl.ds(SUBLANES, tm), :] * cw[3:4, :]
        uc_ref[...] = uc
        uext[0:SUBLANES, :] = uext[tm:tm + SUBLANES, :]

        sp = _softplus(-lam_ref[...])
        for g in range(groups):
            sl = slice(g * blk, (g + 1) * blk)
            ucg = uc_ref[:, sl]
            ucb = ucg.astype(BF16)
            r = _sigmoid(_dot(ucb, wr_ref[g]) + br_ref[:, sl])
            ig = _sigmoid(_dot(ucb, wi_ref[g]) + bi_ref[:, sl])
            r_ref[:, sl] = r
            ig_ref[:, sl] = ig
            a, _, mult = _lru_decay(r, sp[:, sl])
            a_s[:, sl] = a
            mult_ref[:, sl] = mult
            b_s[:, sl] = mult * (ig * ucg)

        @pl.when(i == 0)
        def _():
            row = lax.broadcasted_iota(jnp.int32, (Q_BLOCK, w), 0)
            start = ig_ref[0:Q_BLOCK, :] * uc_ref[0:Q_BLOCK, :]
            b_s[0:Q_BLOCK, :] = jnp.where(row < pad, 0.0, jnp.where(row == pad, start, b_s[0:Q_BLOCK, :]))
            mult_ref[0:Q_BLOCK, :] = jnp.where(row == pad, 1.0, mult_ref[0:Q_BLOCK, :])

        row8 = lax.broadcasted_iota(jnp.int32, (SUBLANES, w), 0)

        def group(gi, h_in):
            off = pl.multiple_of(gi * SUBLANES, SUBLANES)
            av = a_s[pl.ds(off, SUBLANES), :]
            bv = b_s[pl.ds(off, SUBLANES), :]
            for k in (1, 2, 4):
                keep = row8 >= k
                bv = jnp.where(keep, av * pltpu.roll(bv, k, 0) + bv, bv)
                av = jnp.where(keep, av * pltpu.roll(av, k, 0), av)
            hv = av * h_in + bv
            hs_ref[pl.ds(off, SUBLANES), :] = hv
            return jnp.broadcast_to(hv[SUBLANES - 1:SUBLANES, :], (SUBLANES, w))

        hc[...] = lax.fori_loop(0, tm // SUBLANES, group, hc[...])

    row_spec = pl.BlockSpec((tm, w), lambda i: (i, 0))
    vec_spec = pl.BlockSpec((1, w), lambda i: (0, 0))
    mat_spec = pl.BlockSpec((groups, blk, blk), lambda i: (0, 0, 0))
    return pl.pallas_call(
        body, name="b_rglru_fwd", grid=(tp // tm,),
        in_specs=[row_spec, pl.BlockSpec((CONV_WIDTH, w), lambda i: (0, 0)), vec_spec,
                  mat_spec, vec_spec, mat_spec, vec_spec, vec_spec],
        out_specs=[row_spec] * 6,
        out_shape=[jax.ShapeDtypeStruct((tp, w), F32)] * 6,
        scratch_shapes=[pltpu.VMEM((tm + SUBLANES, w), F32), pltpu.VMEM((tm, w), F32),
                        pltpu.VMEM((SUBLANES, w), F32)],
        compiler_params=_params(("arbitrary",)),
    )(u, conv_w, conv_b, w_rg, b_rg, w_ig, b_ig, lam)


def _out_proj_loss(a, gate, w, resid, g, target, x0, tm):
    tp, wd = a.shape
    d = w.shape[1]
    assert x0 % Q_BLOCK == 0 and tm % Q_BLOCK == 0 and target.shape[0] == tp - x0
    lead = x0 // Q_BLOCK
    per = tm // Q_BLOCK

    def body(a_ref, gate_ref, w_ref, res_ref, g_ref, *rest):
        t_refs, (dh_ref, loss_ref, dg_ref, da_ref, dgate_ref, dw_ref) = rest[:per], rest[per:]
        i = pl.program_id(0)

        @pl.when(i == 0)
        def _():
            loss_ref[...] = jnp.zeros_like(loss_ref)
            dg_ref[...] = jnp.zeros_like(dg_ref)
            dw_ref[...] = jnp.zeros_like(dw_ref)

        gate_v = gate_ref[...]
        av = a_ref[...]
        sg = _sigmoid(gate_v)
        silu = gate_v * sg
        y = (av * silu).astype(BF16)
        h = res_ref[...] + _dot(y, w_ref[...])
        gv = g_ref[...]
        for b in range(per):
            rows = slice(b * Q_BLOCK, (b + 1) * Q_BLOCK)
            xv = h[rows, :]
            r = _rms(xv)
            xh = xv * r
            err = jnp.where(i * per + b >= lead, xh * gv - t_refs[b][...], 0.0)
            loss_ref[...] += 0.5 * jnp.sum(jnp.mean(err * err, axis=-1, keepdims=True))
            dy = err / d
            dg_ref[...] += jnp.sum(dy * xh, axis=0, keepdims=True)
            dxh = dy * gv
            dh_ref[rows, :] = r * (dxh - xh * jnp.mean(dxh * xh, axis=-1, keepdims=True))

        dob = dh_ref[...].astype(BF16)
        dyv = _dot_nt(dob, w_ref[...])
        da_ref[...] = dyv * silu
        dgate_ref[...] = (dyv * av * (sg * (1.0 + gate_v * (1.0 - sg)))).astype(BF16)
        dw_ref[...] += _dot_tn(y, dob)

    def piece(b):
        return pl.BlockSpec((Q_BLOCK, d), lambda i: (jnp.maximum(i * per + b - lead, 0), 0))

    return pl.pallas_call(
        body, name="b_out_proj_loss", grid=(tp // tm,),
        in_specs=[pl.BlockSpec((tm, wd), lambda i: (i, 0)),
                  pl.BlockSpec((tm, wd), lambda i: (i, 0)),
                  pl.BlockSpec((wd, d), lambda i: (0, 0)),
                  pl.BlockSpec((tm, d), lambda i: (i, 0)),
                  pl.BlockSpec((1, d), lambda i: (0, 0))] + [piece(b) for b in range(per)],
        out_specs=[pl.BlockSpec((tm, d), lambda i: (i, 0)),
                   pl.BlockSpec((SUBLANES, LANES), lambda i: (0, 0)),
                   pl.BlockSpec((1, d), lambda i: (0, 0)),
                   pl.BlockSpec((tm, wd), lambda i: (i, 0)),
                   pl.BlockSpec((tm, wd), lambda i: (i, 0)),
                   pl.BlockSpec((wd, d), lambda i: (0, 0))],
        out_shape=[jax.ShapeDtypeStruct((tp, d), F32),
                   jax.ShapeDtypeStruct((SUBLANES, LANES), F32),
                   jax.ShapeDtypeStruct((1, d), F32),
                   jax.ShapeDtypeStruct((tp, wd), F32),
                   jax.ShapeDtypeStruct((tp, wd), BF16),
                   jax.ShapeDtypeStruct((wd, d), F32)],
        compiler_params=_params(("arbitrary",)),
    )(a, gate, w, resid, g, *([target] * per))


def _with_exchange(exchange, n_in, n_out, first, last, compute):
    if exchange is None:
        return compute
    ex_in, ex_out = exchange.n_in, exchange.n_out

    def body(*refs):
        own_in, their_in = refs[:n_in], refs[n_in:n_in + ex_in]
        pos = n_in + ex_in
        own_out, their_out = refs[pos:pos + n_out], refs[pos + n_out:pos + n_out + ex_out]
        rest = refs[pos + n_out + ex_out:]
        own_scratch, sems = rest[:len(rest) - 2], rest[len(rest) - 2:]

        @pl.when(first())
        def _():
            exchange.start(their_in, their_out, sems)

        compute(*own_in, *own_out, *own_scratch)

        @pl.when(last())
        def _():
            exchange.finish(their_in, their_out, sems)

    return body


def _gated_out_bwd(name, dout, a, gate, w, tm, delta_heads=0, exchange=None):
    tp, wd = a.shape
    d = w.shape[1]
    nt = tp // tm

    def compute(do_ref, a_ref, gate_ref, w_ref, da_ref, dgate_ref, dw_ref, *delta_ref):
        @pl.when(pl.program_id(0) == 0)
        def _():
            dw_ref[...] = jnp.zeros_like(dw_ref)

        gv = gate_ref[...]
        av = a_ref[...]
        sg = _sigmoid(gv)
        silu = gv * sg
        dob = do_ref[...].astype(BF16)
        dy = _dot_nt(dob, w_ref[...])
        da = dy * silu
        da_ref[...] = da.astype(BF16)
        dgate_ref[...] = (dy * av * (sg * (1.0 + gv * (1.0 - sg)))).astype(BF16)
        dw_ref[...] += _dot_tn((av * silu).astype(BF16), dob)
        for h in range(delta_heads):
            sl = slice(h * V_HEAD, (h + 1) * V_HEAD)
            delta_ref[0][h] = _as_rows(jnp.sum(da[:, sl] * av[:, sl], axis=-1, keepdims=True))

    out_specs = [pl.BlockSpec((tm, wd), lambda i: (i, 0)),
                 pl.BlockSpec((tm, wd), lambda i: (i, 0)),
                 pl.BlockSpec((wd, d), lambda i: (0, 0))]
    out_shape = [jax.ShapeDtypeStruct((tp, wd), BF16),
                 jax.ShapeDtypeStruct((tp, wd), BF16),
                 jax.ShapeDtypeStruct((wd, d), F32)]
    if delta_heads:
        out_specs.append(pl.BlockSpec((delta_heads, None, SUBLANES, tm), lambda i: (0, i, 0, 0)))
        out_shape.append(jax.ShapeDtypeStruct((delta_heads, tp // tm, SUBLANES, tm), F32))
    in_specs = [pl.BlockSpec((tm, d), lambda i: (i, 0)),
                pl.BlockSpec((tm, wd), lambda i: (i, 0)),
                pl.BlockSpec((tm, wd), lambda i: (i, 0)),
                pl.BlockSpec((wd, d), lambda i: (0, 0))]
    args, scratch = [dout, a, gate, w], []
    body = _with_exchange(exchange, len(in_specs), len(out_specs),
                          lambda: pl.program_id(0) == 0, lambda: pl.program_id(0) == nt - 1, compute)
    if exchange is not None:
        in_specs = in_specs + exchange.specs(exchange.n_in)
        out_specs = out_specs + exchange.specs(exchange.n_out)
        out_shape = out_shape + exchange.out_shapes
        args, scratch = args + exchange.arrays, exchange.scratch()
    return pl.pallas_call(
        body, name=name, grid=(nt,),
        in_specs=in_specs, out_specs=out_specs, out_shape=out_shape, scratch_shapes=scratch,
        compiler_params=_params(("arbitrary",)),
    )(*args)


def _rglru_bwd(dhs, hs, r, ig, uc, u, a, mult, conv_w, w_rg, w_ig, lam, pad, tm):
    tp, w = u.shape
    groups, blk = w_rg.shape[0], w_rg.shape[1]
    nt = tp // tm
    per8 = tm // SUBLANES

    def body(dhs_ref, hs_ref, hprev_ref, r_ref, ig_ref, uc_ref, u_ref, uprev_ref, a_ref, mult_ref,
             cw_ref, wr_ref, wi_ref, lam_ref,
             du_ref, dcw_ref, dcb_ref, dwr_ref, dbr_ref, dwi_ref, dbi_ref, dlam_ref,
             aext, c_s, g_s, hext, uext, ducext, gc):
        step = pl.program_id(0)
        ti = nt - 1 - step

        @pl.when(step == 0)
        def _():
            for ref in (dcw_ref, dcb_ref, dwr_ref, dbr_ref, dwi_ref, dbi_ref, dlam_ref):
                ref[...] = jnp.zeros_like(ref)
            aext[tm:tm + SUBLANES, :] = jnp.zeros((SUBLANES, w), F32)
            ducext[tm:tm + SUBLANES, :] = jnp.zeros((SUBLANES, w), F32)
            gc[...] = jnp.zeros((SUBLANES, w), F32)

        lam_v = lam_ref[...]
        sp = _softplus(-lam_v)
        row = ti * tm + lax.broadcasted_iota(jnp.int32, (tm, w), 0)

        rv = r_ref[...]
        a = a_ref[...]
        mult = mult_ref[...]
        aext[0:tm, :] = a
        c_s[...] = aext[pl.ds(1, tm), :]
        row8 = lax.broadcasted_iota(jnp.int32, (SUBLANES, w), 0)

        def group(gi, g_in):
            off = pl.multiple_of((per8 - 1 - gi) * SUBLANES, SUBLANES)
            cv = c_s[pl.ds(off, SUBLANES), :]
            dv = dhs_ref[pl.ds(off, SUBLANES), :]
            for k in (1, 2, 4):
                keep = row8 < SUBLANES - k
                dv = jnp.where(keep, cv * pltpu.roll(dv, SUBLANES - k, 0) + dv, dv)
                cv = jnp.where(keep, cv * pltpu.roll(cv, SUBLANES - k, 0), cv)
            gv = cv * g_in + dv
            g_s[pl.ds(off, SUBLANES), :] = gv
            return jnp.broadcast_to(gv[0:1, :], (SUBLANES, w))

        gc[...] = lax.fori_loop(0, per8, group, gc[...])
        aext[tm:tm + SUBLANES, :] = aext[0:SUBLANES, :]

        gsc = jnp.where(row < pad, 0.0, g_s[...])
        hext[0:SUBLANES, :] = hprev_ref[...]
        hext[SUBLANES:SUBLANES + tm, :] = hs_ref[...]
        hprev = jnp.where(row == 0, 0.0, hext[pl.ds(SUBLANES - 1, tm), :])
        igv = ig_ref[...]
        ucv = uc_ref[...]
        first = row == pad
        dmult = gsc * (igv * ucv)
        dig = gsc * mult * ucv
        duc = gsc * mult * igv
        dlog_a = (gsc * hprev) * a + jnp.where(first, 0.0, dmult * (-(a * a) / mult))
        dlam_ref[...] += jnp.sum(dlog_a * rv, axis=0, keepdims=True) * (LRU_C * _sigmoid(-lam_v))
        dpre_r = dlog_a * (-LRU_C * sp) * (rv * (1.0 - rv))
        dpre_i = dig * (igv * (1.0 - igv))
        dbr_ref[...] += jnp.sum(dpre_r, axis=0, keepdims=True)
        dbi_ref[...] += jnp.sum(dpre_i, axis=0, keepdims=True)
        for g in range(groups):
            sl = slice(g * blk, (g + 1) * blk)
            ucb = ucv[:, sl].astype(BF16)
            drb = dpre_r[:, sl].astype(BF16)
            dib = dpre_i[:, sl].astype(BF16)
            dwr_ref[g] += _dot_tn(ucb, drb)
            dwi_ref[g] += _dot_tn(ucb, dib)
            ducext[0:tm, sl] = duc[:, sl] + _dot_nt(drb, wr_ref[g]) + _dot_nt(dib, wi_ref[g])

        ducv = ducext[0:tm, :]
        cw = cw_ref[...]
        dcb_ref[...] += jnp.sum(ducv, axis=0, keepdims=True)
        uext[0:SUBLANES, :] = jnp.where(ti == 0, 0.0, uprev_ref[...])
        uext[SUBLANES:SUBLANES + tm, :] = u_ref[...]
        for j in range(CONV_WIDTH):
            ush = uext[pl.ds(SUBLANES - (CONV_WIDTH - 1 - j), tm), :]
            dcw_ref[j:j + 1, :] += jnp.sum(ducv * ush, axis=0, keepdims=True)
        du = ducv * cw[3:4, :]
        for k in range(1, CONV_WIDTH):
            du = du + ducext[pl.ds(k, tm), :] * cw[3 - k:4 - k, :]
        du_ref[...] = du.astype(BF16)
        ducext[tm:tm + SUBLANES, :] = ducext[0:SUBLANES, :]

    rev = lambda s: (nt - 1 - s, 0)
    halo = lambda s: (jnp.maximum((nt - 1 - s) * per8 - 1, 0), 0)
    row_spec = pl.BlockSpec((tm, w), rev)
    halo_spec = pl.BlockSpec((SUBLANES, w), halo)
    vec_spec = pl.BlockSpec((1, w), lambda s: (0, 0))
    mat_spec = pl.BlockSpec((groups, blk, blk), lambda s: (0, 0, 0))
    cw_spec = pl.BlockSpec((CONV_WIDTH, w), lambda s: (0, 0))
    return pl.pallas_call(
        body, name="b_rglru_bwd", grid=(nt,),
        in_specs=[row_spec, row_spec, halo_spec, row_spec, row_spec, row_spec, row_spec, halo_spec, row_spec, row_spec,
                  cw_spec, mat_spec, mat_spec, vec_spec],
        out_specs=[row_spec, cw_spec, vec_spec, mat_spec, vec_spec, mat_spec, vec_spec, vec_spec],
        out_shape=[jax.ShapeDtypeStruct((tp, w), BF16), jax.ShapeDtypeStruct((CONV_WIDTH, w), F32),
                   jax.ShapeDtypeStruct((1, w), F32), jax.ShapeDtypeStruct((groups, blk, blk), F32),
                   jax.ShapeDtypeStruct((1, w), F32), jax.ShapeDtypeStruct((groups, blk, blk), F32),
                   jax.ShapeDtypeStruct((1, w), F32), jax.ShapeDtypeStruct((1, w), F32)],
        scratch_shapes=[pltpu.VMEM((tm + SUBLANES, w), F32), pltpu.VMEM((tm, w), F32), pltpu.VMEM((tm, w), F32),
                        pltpu.VMEM((tm + SUBLANES, w), F32), pltpu.VMEM((tm + SUBLANES, w), F32),
                        pltpu.VMEM((tm + SUBLANES, w), F32), pltpu.VMEM((SUBLANES, w), F32)],
        compiler_params=_params(("arbitrary",)),
    )(dhs, hs, hs, r, ig, uc, u, u, a, mult, conv_w, w_rg, w_ig, lam)


def _norm_matmul_bwd(name, x, g, w, dys, tm, resid=None, prologue=None, extra_out=None, exchange=None,
                     transposed=False, dx_dtype=F32):
    tp, kin = x.shape
    n = w.shape[0] if transposed else w.shape[1]
    w_shape = (n, kin) if transposed else (kin, n)
    nt = tp // tm
    n_dy = len(dys)
    has_res = resid is not None
    has_extra = extra_out is not None

    def compute(*refs):
        x_ref, g_ref, w_ref = refs[:3]
        dy_refs = refs[3:3 + n_dy]
        pos = 3 + n_dy
        res_ref = refs[pos] if has_res else None
        pos += int(has_res)
        dx_ref, dw_ref, dg_ref = refs[pos:pos + 3]
        pos += 3
        ex_ref = refs[pos] if has_extra else None
        pos += int(has_extra)
        dy_s = refs[pos]

        @pl.when(pl.program_id(0) == 0)
        def _():
            dw_ref[...] = jnp.zeros_like(dw_ref)
            dg_ref[...] = jnp.zeros_like(dg_ref)

        if prologue is None:
            c0 = 0
            for ref in dy_refs:
                dy_s[:, c0:c0 + ref.shape[1]] = ref[...].astype(BF16)
                c0 += ref.shape[1]
        else:
            prologue(dy_refs, dy_s, ex_ref)

        xv = x_ref[...]
        gv = g_ref[...]
        r = _rms(xv)
        xh = xv * r
        dyb = dy_s[...]
        nb = (xh * gv).astype(BF16)
        if transposed:
            dn = _dot(dyb, w_ref[...])
            dw_ref[...] += _dot_tn(dyb, nb)
        else:
            dn = _dot_nt(dyb, w_ref[...])
            dw_ref[...] += _dot_tn(nb, dyb)
        dg_ref[...] += jnp.sum(dn * xh, axis=0, keepdims=True)
        dxh = dn * gv
        dx = r * (dxh - xh * jnp.mean(dxh * xh, axis=-1, keepdims=True))
        if has_res:
            dx = dx + res_ref[...]
        dx_ref[...] = dx.astype(dx_dtype)

    row = lambda width: pl.BlockSpec((tm, width), lambda i: (i, 0))
    in_specs = [row(kin), pl.BlockSpec((1, kin), lambda i: (0, 0)), pl.BlockSpec(w_shape, lambda i: (0, 0))]
    in_specs += [row(a.shape[1]) for a in dys]
    args = [x, g, w, *dys]
    if has_res:
        in_specs.append(row(kin))
        args.append(resid)
    out_specs = [row(kin), pl.BlockSpec(w_shape, lambda i: (0, 0)), pl.BlockSpec((1, kin), lambda i: (0, 0))]
    out_shape = [jax.ShapeDtypeStruct((tp, kin), dx_dtype), jax.ShapeDtypeStruct(w_shape, F32),
                 jax.ShapeDtypeStruct((1, kin), F32)]
    if has_extra:
        out_specs.append(row(extra_out[0]))
        out_shape.append(jax.ShapeDtypeStruct((tp, extra_out[0]), extra_out[1]))
    scratch = [pltpu.VMEM((tm, n), BF16)]
    body = _with_exchange(exchange, len(in_specs), len(out_specs),
                          lambda: pl.program_id(0) == 0, lambda: pl.program_id(0) == nt - 1, compute)
    if exchange is not None:
        in_specs = in_specs + exchange.specs(exchange.n_in)
        out_specs = out_specs + exchange.specs(exchange.n_out)
        out_shape = out_shape + exchange.out_shapes
        args, scratch = args + exchange.arrays, scratch + exchange.scratch()
    return pl.pallas_call(
        body, name=name, grid=(nt,),
        in_specs=in_specs, out_specs=out_specs, out_shape=out_shape, scratch_shapes=scratch,
        compiler_params=_params(("arbitrary",)),
    )(*args)


def _flash_bwd(q, k, v, lse, delta, do, heads, pad, tq, exchange=None):
    tp = q.shape[0]
    nq = tp // tq
    scale = (QK_NOPE + QK_ROPE) ** -0.5
    c2 = scale * LOG2E

    def compute(q_ref, k_ref, v_ref, lse_ref, delta_ref, do_ref, dq_ref, dk_ref, dv_ref):
        j = pl.program_id(1)

        @pl.when(j == 0)
        def _():
            dq_ref[...] = jnp.zeros_like(dq_ref)

        kv = k_ref[...]
        vv = v_ref[...]

        def rows_of(ref, i, blocks):
            parts = [ref[i + b][0:1, :] for b in range(blocks)]
            return parts[0] if blocks == 1 else jnp.concatenate(parts, axis=1)

        def make_step(masked, blocks):
            def step(i, carry):
                off = pl.multiple_of(i * tq, tq)
                qv = q_ref[pl.ds(off, blocks * tq), :]
                dob = do_ref[pl.ds(off, blocks * tq), :]
                p = jnp.exp2(_dot_nt(kv, qv) * c2 - rows_of(lse_ref, i, blocks))
                if masked:
                    key = j * tq + lax.broadcasted_iota(jnp.int32, (tq, tq), 0)
                    qry = j * tq + lax.broadcasted_iota(jnp.int32, (tq, tq), 1)
                    first = jnp.where((key <= qry) & (key >= pad), p[:, :tq], 0.0)
                    p = first if blocks == 1 else jnp.concatenate([first, p[:, tq:]], axis=1)
                dv_ref[...] += _dot(p.astype(BF16), dob)
                dp = _dot_nt(vv, dob)
                ds = (p * (dp - rows_of(delta_ref, i, blocks)) * scale).astype(BF16)
                dk_ref[...] += _dot(ds, qv)
                dq_ref[pl.ds(off, blocks * tq), :] += _dot_tn(ds, kv)
                return carry
            return step

        dk_ref[...] = jnp.zeros_like(dk_ref)
        dv_ref[...] = jnp.zeros_like(dv_ref)
        odd = (nq - j) % 2
        lax.fori_loop(0, odd, lambda t, cr: make_step(True, 1)(j, cr), 0)
        lax.fori_loop(0, 1 - odd, lambda t, cr: make_step(True, 2)(j, cr), 0)
        start = j + 2 - odd
        for blocks in (4, 2):
            trips = (nq - start) // blocks
            step_n = make_step(False, blocks)
            lax.fori_loop(0, trips, lambda t, cr, s=start, b=blocks, f=step_n: f(s + b * t, cr), 0)
            start = start + blocks * trips

    in_specs = [pl.BlockSpec((tp, HEAD_W), lambda h, j: (0, h)),
                pl.BlockSpec((tq, HEAD_W), lambda h, j: (j, h)),
                pl.BlockSpec((tq, V_HEAD), lambda h, j: (j, h)),
                pl.BlockSpec((None, nq, SUBLANES, tq), lambda h, j: (h, 0, 0, 0)),
                pl.BlockSpec((None, nq, SUBLANES, tq), lambda h, j: (h, 0, 0, 0)),
                pl.BlockSpec((tp, V_HEAD), lambda h, j: (0, h))]
    out_specs = [pl.BlockSpec((tp, HEAD_W), lambda h, j: (0, h)),
                 pl.BlockSpec((tq, HEAD_W), lambda h, j: (j, h)),
                 pl.BlockSpec((tq, V_HEAD), lambda h, j: (j, h))]
    out_shape = [jax.ShapeDtypeStruct((tp, heads * HEAD_W), F32),
                 jax.ShapeDtypeStruct((tp, heads * HEAD_W), F32),
                 jax.ShapeDtypeStruct((tp, heads * V_HEAD), F32)]
    args, scratch = [q, k, v, lse, delta, do], []
    body = _with_exchange(exchange, len(in_specs), len(out_specs),
                          lambda: (pl.program_id(0) == 0) & (pl.program_id(1) == 0),
                          lambda: (pl.program_id(0) == heads - 1) & (pl.program_id(1) == nq - 1), compute)
    if exchange is not None:
        in_specs = in_specs + exchange.specs(exchange.n_in)
        out_specs = out_specs + exchange.specs(exchange.n_out)
        out_shape = out_shape + exchange.out_shapes
        args, scratch = args + exchange.arrays, exchange.scratch()
    return pl.pallas_call(
        body, name="a_flash_bwd", grid=(heads, nq),
        in_specs=in_specs, out_specs=out_specs, out_shape=out_shape, scratch_shapes=scratch,
        compiler_params=_params(("arbitrary", "arbitrary")),
    )(*args)


def _position():
    return lax.axis_index("x"), lax.axis_index("y"), lax.axis_index("c")


def _other_chips(x, y):
    return [(1 - x, y), (x, 1 - y), (1 - x, 1 - y)]


def _block(ref, shard_axis, n, k, split_axis=None, m=None, h=None):
    idx = []
    for a in range(len(ref.shape)):
        start = 0
        size = None
        if a == shard_axis:
            start, size = k * n, n
        if a == split_axis:
            size = (n if a == shard_axis else m) // 2
            start = start + h * size
        idx.append(slice(None) if size is None else pl.ds(start, size))
    return ref.at[tuple(idx)]


def _gather_weights(split, whole_small):
    ns, nw = len(split), len(whole_small)
    n = ns + nw
    arrs = [s[0] for s in split] + [s[0] for s in whole_small]
    axes = [s[1] for s in split] + [s[1] for s in whole_small]

    def body(*refs):
        ins, outs = refs[:n], refs[n:2 * n]
        ici_send, ici_recv, d2d_send, d2d_recv, sib_send, sib_recv = refs[2 * n:]
        x, y, c = _position()
        me = 2 * x + y
        others = _other_chips(x, y)
        sent, local = [], []

        def remote(src, dst, sems, idx, to):
            return pltpu.make_async_remote_copy(src_ref=src, dst_ref=dst, send_sem=sems[0].at[idx],
                                                recv_sem=sems[1].at[idx], device_id=to, device_id_type=MESH)

        for a in range(n):
            width = ins[a].shape[axes[a]]
            mine = remote(ins[a], _block(outs[a], axes[a], width, me), (sib_send, sib_recv), a, (x, y, 1 - c))
            mine.start()
            local.append(mine)
            for j, (px, py) in enumerate(others):
                if a < ns:
                    sx = split[a][2]
                    src = _block(ins[a], None, None, None, sx, ins[a].shape[sx], c)
                    dst = _block(outs[a], axes[a], width, me, sx, outs[a].shape[sx], c)
                else:
                    src, dst = ins[a], _block(outs[a], axes[a], width, me)
                cp = remote(src, dst, (ici_send, ici_recv), 3 * a + j, (px, py, c))
                cp.start()
                sent.append(cp)
        for a in range(ns):
            width = ins[a].shape[axes[a]]
            sx = split[a][2]
            for j, (px, py) in enumerate(others):
                theirs = _block(outs[a], axes[a], width, 2 * px + py, sx, outs[a].shape[sx], c)
                remote(theirs, theirs, (ici_send, ici_recv), 3 * a + j, (px, py, c)).wait_recv()
                fwd = remote(theirs, theirs, (d2d_send, d2d_recv), 3 * a + j, (x, y, 1 - c))
                fwd.start()
                sent.append(fwd)
        for a in range(ns, n):
            width = ins[a].shape[axes[a]]
            for j, (px, py) in enumerate(others):
                theirs = _block(outs[a], axes[a], width, 2 * px + py)
                remote(theirs, theirs, (ici_send, ici_recv), 3 * a + j, (px, py, c)).wait_recv()
        for a in range(ns):
            width = ins[a].shape[axes[a]]
            sx = split[a][2]
            for j, (px, py) in enumerate(others):
                from_sibling = _block(outs[a], axes[a], width, 2 * px + py, sx, outs[a].shape[sx], 1 - c)
                remote(from_sibling, from_sibling, (d2d_send, d2d_recv), 3 * a + j, (x, y, 1 - c)).wait_recv()
        for cp in sent:
            cp.wait_send()
        for cp in local:
            cp.wait()

    def whole_shape(a, axis):
        shape = list(a.shape)
        shape[axis] *= N_CHIPS
        return jax.ShapeDtypeStruct(tuple(shape), a.dtype)

    any_spec = pl.BlockSpec(memory_space=pl.ANY)
    return pl.pallas_call(
        body, name="gather_weights",
        in_specs=[any_spec] * n, out_specs=[any_spec] * n,
        out_shape=[whole_shape(a, ax) for a, ax in zip(arrs, axes)],
        scratch_shapes=[pltpu.SemaphoreType.DMA((3 * n,)), pltpu.SemaphoreType.DMA((3 * n,)),
                        pltpu.SemaphoreType.DMA((3 * ns,)), pltpu.SemaphoreType.DMA((3 * ns,)),
                        pltpu.SemaphoreType.DMA((n,)), pltpu.SemaphoreType.DMA((n,))],
        compiler_params=pltpu.CompilerParams(has_side_effects=True),
    )(*arrs)


class _Grad:
    def __init__(self, name, g, kind, rh, cols, groups=None):
        self.name, self.g, self.kind, self.rh, self.cols, self.groups = name, g, kind, rh, cols, groups
        if kind == 'rows':
            self.tr = rh
        elif kind == 'gate':
            self.tr = rh // (groups // 2)
        else:
            self.tr = rh if rh <= 512 else 256
        self.nb = rh // self.tr

    def pieces(self, ref, k, h):
        rh, cols = self.rh, self.cols
        if self.kind == 'cols':
            return [(ref.at[pl.ds(h * rh, rh), pl.ds(k * cols, cols)], 0, rh)]
        if self.kind == 'rows':
            return [(ref.at[pl.ds((2 * k + h) * rh, rh), :], 0, rh)]
        if self.kind == 'lead':
            return [(ref.at[k, pl.ds(h * rh, rh), :], 0, rh)]
        per = self.groups // 2
        return [(ref.at[pl.ds((((h * per + gi) * N_CHIPS) + k) * self.tr, self.tr), :], gi * self.tr, self.tr)
                for gi in range(per)]

    def block_spec(self):
        tr, nb, cols = self.tr, self.nb, self.cols
        if self.kind == 'cols':
            return pl.BlockSpec((tr, cols), lambda k, i, c: (c[0] * nb + i, k))
        if self.kind == 'rows':
            return pl.BlockSpec((tr, cols), lambda k, i, c: (2 * k + c[0], 0))
        if self.kind == 'lead':
            return pl.BlockSpec((None, tr, cols), lambda k, i, c: (k, c[0] * nb + i, 0))
        return pl.BlockSpec((tr, cols), lambda k, i, c: ((c[0] * nb + i) * N_CHIPS + k, 0))


class _Exchange:
    def __init__(self, name, arrays, out_shapes, n_copies, copies):
        self.name, self.arrays, self.out_shapes, self.n_copies, self.copies = name, arrays, out_shapes, n_copies, copies
        self.n_in, self.n_out = len(arrays), len(out_shapes)

    def specs(self, n):
        return [pl.BlockSpec(memory_space=pl.ANY)] * n

    def scratch(self):
        return [pltpu.SemaphoreType.DMA((self.n_copies,)), pltpu.SemaphoreType.DMA((self.n_copies,))]

    def _descriptors(self, in_refs, out_refs, sems):
        return self.copies(in_refs, out_refs, sems[0], sems[1])

    def start(self, in_refs, out_refs, sems):
        for cp in self._descriptors(in_refs, out_refs, sems):
            cp.start()

    def finish(self, in_refs, out_refs, sems):
        for cp in self._descriptors(in_refs, out_refs, sems):
            cp.wait()

    def __add__(self, other):
        def copies(ins, outs, send_sems, recv_sems, base=0):
            return (self.copies(ins[:self.n_in], outs[:self.n_out], send_sems, recv_sems, base)
                    + other.copies(ins[self.n_in:], outs[self.n_out:], send_sems, recv_sems, base + self.n_copies))

        return _Exchange(self.name + "_" + other.name, self.arrays + other.arrays, self.out_shapes + other.out_shapes,
                         self.n_copies + other.n_copies, copies)

    def run(self):
        def body(*refs):
            ins, outs, sems = refs[:self.n_in], refs[self.n_in:self.n_in + self.n_out], refs[self.n_in + self.n_out:]
            self.start(ins, outs, sems)
            self.finish(ins, outs, sems)

        return pl.pallas_call(
            body, name=self.name,
            in_specs=self.specs(self.n_in), out_specs=self.specs(self.n_out), out_shape=self.out_shapes,
            scratch_shapes=self.scratch(),
            compiler_params=pltpu.CompilerParams(has_side_effects=True),
        )(*self.arrays)


def _gather_whole(name, shards):
    def copies(ins, outs, send_sems, recv_sems, base=0):
        x, y, c = _position()
        me = 2 * x + y
        made = []
        for a, (_, axis) in enumerate(shards):
            dst = _block(outs[a], axis, ins[a].shape[axis], me)
            for j, to in enumerate([(x, y, 1 - c)] + [(px, py, c) for px, py in _other_chips(x, y)]):
                idx = base + 4 * a + j
                made.append(pltpu.make_async_remote_copy(
                    src_ref=ins[a], dst_ref=dst, send_sem=send_sems.at[idx], recv_sem=recv_sems.at[idx],
                    device_id=to, device_id_type=MESH))
        return made

    def whole_shape(a, axis):
        shape = list(a.shape)
        shape[axis] *= N_CHIPS
        return jax.ShapeDtypeStruct(tuple(shape), a.dtype)

    return _Exchange(name, [s[0] for s in shards], [whole_shape(*s) for s in shards], 4 * len(shards), copies)


def _halves_to_sibling(name, grads):
    total = sum(len(gr.pieces(gr.g, 0, 0)) * N_CHIPS for gr in grads)

    def copies(ins, outs, send_sems, recv_sems, base=0):
        x, y, c = _position()
        made = []
        for gr, g_ref, got_ref in zip(grads, ins, outs):
            for k in range(N_CHIPS):
                for src, r0, nr in gr.pieces(g_ref, k, 1 - c):
                    idx = base + len(made)
                    made.append(pltpu.make_async_remote_copy(
                        src_ref=src, dst_ref=got_ref.at[k, pl.ds(r0, nr), :],
                        send_sem=send_sems.at[idx], recv_sem=recv_sems.at[idx],
                        device_id=(x, y, 1 - c), device_id_type=MESH))
        return made

    return _Exchange(name, [gr.g for gr in grads],
                     [jax.ShapeDtypeStruct((N_CHIPS, gr.rh, gr.cols), F32) for gr in grads], total, copies)


def _chip_sum(gr, got, c, wire=BF16):
    def body(c_ref, g_ref, got_ref, o_ref):
        o_ref[...] = (g_ref[...] + got_ref[...]).astype(wire)

    tile = pl.BlockSpec((None, gr.tr, gr.cols), lambda k, i, c_ref: (k, i, 0))
    return pl.pallas_call(
        body, name="chip_sum_" + gr.name,
        grid_spec=pltpu.PrefetchScalarGridSpec(
            num_scalar_prefetch=1, grid=(N_CHIPS, gr.nb),
            in_specs=[gr.block_spec(), tile], out_specs=tile),
        out_shape=jax.ShapeDtypeStruct((N_CHIPS, gr.rh, gr.cols), wire),
        compiler_params=_params(("parallel", "parallel")),
    )(c, gr.g, got)


def _blocks_to_chips(name, parts):
    n = len(parts)

    def copies(ins, outs, send_sems, recv_sems, base=0):
        x, y, c = _position()
        made = []
        for a in range(n):
            for j, (px, py) in enumerate(_other_chips(x, y)):
                idx = base + 3 * a + j
                made.append(pltpu.make_async_remote_copy(
                    src_ref=ins[a].at[2 * px + py], dst_ref=outs[a].at[j],
                    send_sem=send_sems.at[idx], recv_sem=recv_sems.at[idx],
                    device_id=(px, py, c), device_id_type=MESH))
        return made

    return _Exchange(name, parts, [jax.ShapeDtypeStruct((3,) + p.shape[1:], p.dtype) for p in parts], 3 * n, copies)


def _sum_chips(name, part, got, me, c):
    nk, rh, cols = part.shape
    tr = rh if rh <= 512 else 256
    nb = rh // tr

    def body(me_ref, c_ref, own_ref, *rest):
        got_refs, o_ref = rest[:nk], rest[nk]
        own = own_ref[...].astype(F32)
        acc = None
        for k in range(nk):
            term = jnp.where(me_ref[0] == k, own, got_refs[k][...].astype(F32))
            acc = term if acc is None else acc + term
        o_ref[...] = acc

    def got_map(k):
        def index(i, me_ref, c_ref):
            xor = jnp.bitwise_xor(me_ref[0], k)
            slot = jnp.where(xor == 1, 1, jnp.where(xor == 3, 2, 0))
            return (slot, i, 0)
        return index

    return pl.pallas_call(
        body, name="sum_" + name,
        grid_spec=pltpu.PrefetchScalarGridSpec(
            num_scalar_prefetch=2, grid=(nb,),
            in_specs=[pl.BlockSpec((None, tr, cols), lambda i, me_ref, c_ref: (me_ref[0], i, 0))]
            + [pl.BlockSpec((None, tr, cols), got_map(k)) for k in range(nk)],
            out_specs=pl.BlockSpec((tr, cols), lambda i, me_ref, c_ref: (c_ref[0] * nb + i, 0))),
        out_shape=jax.ShapeDtypeStruct((2 * rh, cols), F32),
        compiler_params=_params(("parallel",)),
    )(me, c, part, *([got] * nk))


def _share_with_sibling(halves):
    n = len(halves)

    def body(*refs):
        outs = refs[n:2 * n]
        send_sems, recv_sems = refs[2 * n:]
        x, y, c = _position()
        copies = []
        for a in range(n):
            rh = outs[a].shape[0] // 2
            mine = outs[a].at[pl.ds(c * rh, rh), :]
            cp = pltpu.make_async_remote_copy(
                src_ref=mine, dst_ref=mine, send_sem=send_sems.at[a], recv_sem=recv_sems.at[a],
                device_id=(x, y, 1 - c), device_id_type=MESH)
            cp.start()
            copies.append(cp)
        for cp in copies:
            cp.wait()

    any_spec = pl.BlockSpec(memory_space=pl.ANY)
    return pl.pallas_call(
        body, name="grads_share",
        in_specs=[any_spec] * n, out_specs=[any_spec] * n,
        out_shape=[jax.ShapeDtypeStruct(h.shape, h.dtype) for h in halves],
        input_output_aliases={a: a for a in range(n)},
        scratch_shapes=[pltpu.SemaphoreType.DMA((n,)), pltpu.SemaphoreType.DMA((n,))],
        compiler_params=pltpu.CompilerParams(has_side_effects=True),
    )(*halves)


def _adamw(name, w, g, m, v):
    rows, cols = w.shape
    tr = 256 if rows % 256 == 0 else rows

    def body(*refs):
        _adamw_update(*refs)

    spec = pl.BlockSpec((tr, cols), lambda i: (i, 0))
    return pl.pallas_call(
        body, name=name, grid=(rows // tr,),
        in_specs=[spec] * 4, out_specs=[spec] * 4,
        out_shape=[jax.ShapeDtypeStruct((rows, cols), F32)] * 4,
        compiler_params=_params(("parallel",)),
    )(w, g, m, v)


def _adamw_update(w_ref, g_ref, m_ref, v_ref, g_out_ref, d_ref, nm_ref, nv_ref):
    gv = g_ref[...]
    g_out_ref[...] = gv
    mn = ADAM_B1 * m_ref[...] + (1.0 - ADAM_B1) * gv
    vn = ADAM_B2 * v_ref[...] + (1.0 - ADAM_B2) * (gv * gv)
    m_hat = mn / (1.0 - ADAM_B1 ** ADAM_STEP)
    v_hat = vn / (1.0 - ADAM_B2 ** ADAM_STEP)
    d_ref[...] = -ADAM_LR * (m_hat / (jnp.sqrt(v_hat) + ADAM_EPS) + ADAM_WD * w_ref[...])
    nm_ref[...] = mn
    nv_ref[...] = vn


def _adamw_vectors(tensors):
    n = len(tensors)

    def body(*refs):
        ins, outs = refs[:4 * n], refs[4 * n:]
        for t in range(n):
            _adamw_update(*ins[4 * t:4 * t + 4], *outs[4 * t:4 * t + 4])

    flat = [a for t in tensors for a in t]
    vmem = pl.BlockSpec(memory_space=pltpu.VMEM)
    out = pl.pallas_call(
        body, name="adamw_vectors",
        in_specs=[vmem] * (4 * n), out_specs=[vmem] * (4 * n),
        out_shape=[jax.ShapeDtypeStruct(t[0].shape, F32) for t in tensors for _ in range(4)],
        compiler_params=_params(),
    )(*flat)
    return [out[4 * t:4 * t + 4] for t in range(n)]


def _as2d(a):
    if a.ndim == 1:
        return a.reshape(1, -1)
    return a.reshape(-1, a.shape[-1])


def _rope_tables(tp, pad):
    pos = np.arange(tp, dtype=np.float32) - np.float32(pad)
    inv_freq = np.float32(ROPE_BASE) ** (-np.arange(0, QK_ROPE, 2, dtype=np.float32) / np.float32(QK_ROPE))
    ang = (pos[:, None] * inv_freq[None, :]).astype(np.float32)
    cos, sin = np.cos(ang), np.sin(ang)
    zeros = np.zeros((tp, LANES - QK_ROPE), np.float32)
    return (jnp.asarray(np.concatenate([cos, cos, zeros], axis=1), F32),
            jnp.asarray(np.concatenate([-sin, sin, zeros], axis=1), F32))


def _matrix_grad(name, g, heads):
    if name in ('b_w_rg', 'b_w_ig'):
        groups, blk, cols = g.shape
        return _Grad(name, g.reshape(groups * blk, cols), 'gate', (groups // 2) * (blk // N_CHIPS), cols, groups)
    rows, cols = g.shape
    if name in ('a_w_in', 'a_w_out', 'b_w_out'):
        return _Grad(name, g, 'rows', rows // (2 * N_CHIPS), cols)
    if name == 'a_w_uq' and heads % N_CHIPS != 0:
        g = g.reshape(rows, heads, HEAD_W)[:, :, :QK_NOPE + QK_ROPE].reshape(rows, -1)
        cols = g.shape[1]
        g = jnp.moveaxis(g.reshape(rows, N_CHIPS, cols // N_CHIPS), 1, 0)
        return _Grad(name, g, 'lead', rows // 2, cols // N_CHIPS)
    return _Grad(name, g, 'cols', rows // 2, cols // N_CHIPS)


def _kernel_form(name, w):
    return w[0] if name in ('b_w_rg', 'b_w_ig', 'b_conv_w') else _as2d(w)


def _local_grads(x, target, wt, heads, c_idx, mid_names, mid_gather, late_names, late_gather):
    wt = dict(wt)
    seq, d = x.shape
    n_meta = wt['meta_tokens'].shape[0]
    t = seq + n_meta
    pad = (-t) % Q_BLOCK
    tp = t + pad
    x0 = pad + n_meta
    tm = _row_tile(tp)
    ql = wt['a_q_norm_g'].shape[1]
    kvl = wt['a_kv_norm_g'].shape[1]
    mla_w = heads * V_HEAD

    cos_t, sin_t = _rope_tables(tp, pad)

    w_in_a = wt['a_w_in']
    zrow = jnp.zeros((LANES - QK_ROPE, d), BF16)
    w_in_a = jnp.concatenate([w_in_a[:ql + kvl + QK_ROPE], zrow, w_in_a[ql + kvl + QK_ROPE:]], axis=0)
    c_kv, c_kr, c_gate = ql, ql + kvl, ql + kvl + LANES
    splits_a = [(0, c_kv), (c_kv, c_kr), (c_kr, c_gate), (c_gate, c_gate + mla_w)]

    h0, q_lat, kv_lat, kr_raw, gate_a, *mid_whole = _tokens_in_proj(
        "a_in_proj", x, wt['meta_tokens'], wt['a_norm_g'], w_in_a, splits_a, tm, pad, exchange=mid_gather)
    wt.update({n: _kernel_form(n, w) for n, w in zip(mid_names, mid_whole)})
    w_uq = wt['a_w_uq'].reshape(ql, heads, QK_NOPE + QK_ROPE)
    w_uq = jnp.pad(w_uq, ((0, 0), (0, 0), (0, HEAD_W - QK_NOPE - QK_ROPE))).reshape(ql, heads * HEAD_W)
    w_ukv = wt['a_w_ukv']
    q = _q_proj(q_lat, wt['a_q_norm_g'], w_uq, cos_t, sin_t, heads, tm)
    k, v = _kv_proj(kv_lat, wt['a_kv_norm_g'], w_ukv, kr_raw, cos_t, sin_t, heads, tm)
    attn, lse, *late_whole = _flash_fwd(q, k, v, heads, pad, tm, exchange=late_gather)
    wt.update({n: _kernel_form(n, w) for n, w in zip(late_names, late_whole)})
    lru_w = wt['b_conv_w'].shape[1]

    h1, u, gate_b = _out_proj_in_proj("a_out_b_in_proj", attn, gate_a, wt['a_w_out'], h0, wt['b_norm_g'], wt['b_w_in'],
                                      [(0, lru_w), (lru_w, 2 * lru_w)], tm)
    uc, r, ig, hs, decay, mult = _rglru_fwd(u, wt['b_conv_w'], wt['b_conv_b'], wt['b_w_rg'], wt['b_b_rg'],
                                            wt['b_w_ig'], wt['b_b_ig'], wt['b_lam'], pad, tm)

    dh2, loss, d_final_g, dhs, dgate_b, dw_out_b = _out_proj_loss(
        hs, gate_b, wt['b_w_out'], h1, wt['final_norm_g'], target, x0, tm)
    du, dconv_w, dconv_b, dw_rg, db_rg, dw_ig, db_ig, dlam = _rglru_bwd(
        dhs, hs, r, ig, uc, u, decay, mult, wt['b_conv_w'], wt['b_w_rg'], wt['b_w_ig'], wt['b_lam'], pad, tm)
    dh1, dw_in_b, dg_b = _norm_matmul_bwd("b_in_proj_bwd", h1, wt['b_norm_g'], wt['b_w_in'], [du, dgate_b], tm, resid=dh2)

    grads_b = [_matrix_grad(n, g, heads) for n, g in
               (('b_w_in', dw_in_b), ('b_w_rg', dw_rg), ('b_w_ig', dw_ig), ('b_w_out', dw_out_b))]
    dattn, dgate_a, dw_out_a, delta, *got = _gated_out_bwd(
        "a_out_proj_bwd", dh1, attn, gate_a, wt['a_w_out'], tm, delta_heads=heads,
        exchange=_halves_to_sibling("swap_b", grads_b))
    sums_b = [_chip_sum(gr, r, c_idx) for gr, r in zip(grads_b, got)]
    grad_out = _matrix_grad('a_w_out', dw_out_a, heads)
    dq, dk, dv, *landed = _flash_bwd(
        q, k, v, lse, delta, dattn, heads, pad, tm,
        exchange=_blocks_to_chips("chips_b", sums_b) + _halves_to_sibling("swap_out", [grad_out]))
    through = list(zip(grads_b, sums_b, landed[:len(grads_b)]))
    sum_out = _chip_sum(grad_out, landed[len(grads_b)], c_idx)

    def q_prologue(dy_refs, dy_s, ex_ref):
        (dq_ref,), cos_v, sin_v = dy_refs[:1], dy_refs[1][...], dy_refs[2][...]
        for h in range(heads):
            c0 = h * HEAD_W
            dy_s[:, c0:c0 + QK_NOPE] = dq_ref[:, c0:c0 + QK_NOPE].astype(BF16)
            dy_s[:, c0 + QK_NOPE:c0 + HEAD_W] = _unrope(dq_ref[:, c0 + QK_NOPE:c0 + HEAD_W], cos_v, sin_v).astype(BF16)

    dq_lat, dw_uq, dg_q, from_chips_out = _norm_matmul_bwd(
        "a_q_proj_bwd", q_lat, wt['a_q_norm_g'], w_uq, [dq, cos_t, sin_t], tm, prologue=q_prologue, dx_dtype=BF16,
        exchange=_blocks_to_chips("chips_out", [sum_out]))
    through.append((grad_out, sum_out, from_chips_out))
    grad_uq = _matrix_grad('a_w_uq', dw_uq, heads)

    def kv_prologue(dy_refs, dy_s, ex_ref):
        dk_ref, dv_ref = dy_refs[:2]
        cos_v, sin_v = dy_refs[2][...], dy_refs[3][...]
        dkr = jnp.zeros((dk_ref.shape[0], LANES), F32)
        for h in range(heads):
            c0 = h * (QK_NOPE + V_HEAD)
            dy_s[:, c0:c0 + QK_NOPE] = dk_ref[:, h * HEAD_W:h * HEAD_W + QK_NOPE].astype(BF16)
            dy_s[:, c0 + QK_NOPE:c0 + QK_NOPE + V_HEAD] = dv_ref[:, h * V_HEAD:(h + 1) * V_HEAD].astype(BF16)
            dkr = dkr + dk_ref[:, h * HEAD_W + QK_NOPE:(h + 1) * HEAD_W]
        ex_ref[...] = _unrope(dkr, cos_v, sin_v).astype(ex_ref.dtype)

    dkv_lat, dw_ukv, dg_kv, dkr_raw, got_uq = _norm_matmul_bwd(
        "a_kv_proj_bwd", kv_lat, wt['a_kv_norm_g'], w_ukv, [dk, dv, cos_t, sin_t], tm,
        prologue=kv_prologue, extra_out=(LANES, BF16), dx_dtype=BF16, exchange=_halves_to_sibling("swap_uq", [grad_uq]))
    sum_uq = _chip_sum(grad_uq, got_uq, c_idx)
    grad_ukv = _matrix_grad('a_w_ukv', dw_ukv, heads)

    dh0, dw_in_a, dg_a, from_chips_uq, got_ukv = _norm_matmul_bwd(
        "a_in_proj_bwd", h0, wt['a_norm_g'], w_in_a, [dq_lat, dkv_lat, dkr_raw, dgate_a], tm, resid=dh1, transposed=True,
        exchange=_blocks_to_chips("chips_uq", [sum_uq]) + _halves_to_sibling("swap_ukv", [grad_ukv]))
    through.append((grad_uq, sum_uq, from_chips_uq))
    swapped = [(grad_ukv, _chip_sum(grad_ukv, got_ukv, c_idx))]

    dw_in_a = jnp.concatenate([dw_in_a[:c_kr + QK_ROPE], dw_in_a[c_gate:]], axis=0)
    grads = {
        'meta_tokens': dh0[pad:x0], 'a_norm_g': dg_a, 'a_w_in': dw_in_a, 'a_q_norm_g': dg_q, 'a_kv_norm_g': dg_kv,
        'a_w_uq': dw_uq, 'a_w_ukv': dw_ukv, 'a_w_out': dw_out_a, 'b_norm_g': dg_b, 'b_w_in': dw_in_b,
        'b_conv_w': dconv_w, 'b_conv_b': dconv_b, 'b_w_rg': dw_rg, 'b_b_rg': db_rg, 'b_w_ig': dw_ig,
        'b_b_ig': db_ig, 'b_lam': dlam, 'b_w_out': dw_out_b, 'final_norm_g': d_final_g,
    }
    return loss, dh0[x0:], grads, through, swapped


def _chip_major(whole, local_shape, axis):
    if axis is None:
        return jnp.broadcast_to(whole.reshape(1, -1), (N_CHIPS, whole.size))
    shape = list(local_shape)
    g = whole.reshape(shape[:axis] + [N_CHIPS, shape[axis]] + shape[axis + 1:])
    return jnp.moveaxis(g, axis, 0).reshape(N_CHIPS, -1)


def kernel(x, meta_tokens, a_norm_g, a_w_in, a_q_norm_g, a_kv_norm_g, a_w_uq, a_w_ukv, a_w_out, b_norm_g, b_w_in, b_conv_w, b_conv_b, b_w_rg, b_b_rg, b_w_ig, b_b_ig, b_lam, b_w_out, final_norm_g, loss_target, m_meta_tokens, m_a_norm_g, m_a_w_in, m_a_q_norm_g, m_a_kv_norm_g, m_a_w_uq, m_a_w_ukv, m_a_w_out, m_b_norm_g, m_b_w_in, m_b_conv_w, m_b_conv_b, m_b_w_rg, m_b_b_rg, m_b_w_ig, m_b_b_ig, m_b_lam, m_b_w_out, m_final_norm_g, v_meta_tokens, v_a_norm_g, v_a_w_in, v_a_q_norm_g, v_a_kv_norm_g, v_a_w_uq, v_a_w_ukv, v_a_w_out, v_b_norm_g, v_b_w_in, v_b_conv_w, v_b_conv_b, v_b_w_rg, v_b_b_rg, v_b_w_ig, v_b_b_ig, v_b_lam, v_b_w_out, v_final_norm_g):
    local_w = dict(zip(WEIGHTS, (meta_tokens, a_norm_g, a_w_in, a_q_norm_g, a_kv_norm_g, a_w_uq, a_w_ukv, a_w_out,
                                 b_norm_g, b_w_in, b_conv_w, b_conv_b, b_w_rg, b_b_rg, b_w_ig, b_b_ig, b_lam,
                                 b_w_out, final_norm_g)))
    local_m = dict(zip(WEIGHTS, (m_meta_tokens, m_a_norm_g, m_a_w_in, m_a_q_norm_g, m_a_kv_norm_g, m_a_w_uq,
                                 m_a_w_ukv, m_a_w_out, m_b_norm_g, m_b_w_in, m_b_conv_w, m_b_conv_b, m_b_w_rg,
                                 m_b_b_rg, m_b_w_ig, m_b_b_ig, m_b_lam, m_b_w_out, m_final_norm_g)))
    local_v = dict(zip(WEIGHTS, (v_meta_tokens, v_a_norm_g, v_a_w_in, v_a_q_norm_g, v_a_kv_norm_g, v_a_w_uq,
                                 v_a_w_ukv, v_a_w_out, v_b_norm_g, v_b_w_in, v_b_conv_w, v_b_conv_b, v_b_w_rg,
                                 v_b_b_rg, v_b_w_ig, v_b_b_ig, v_b_lam, v_b_w_out, v_final_norm_g)))
    matrices = ('a_w_in', 'a_w_uq', 'a_w_ukv', 'a_w_out', 'b_w_in', 'b_w_rg', 'b_w_ig', 'b_w_out')
    heads = a_w_uq.shape[-1] * N_CHIPS // (QK_NOPE + QK_ROPE)

    def transposed(a):
        return jnp.swapaxes(a, 1, 2)

    split, small, mid, late = [], [], [], []
    for n in WEIGHTS:
        if SHARD_AXIS[n] is None:
            continue
        if n.startswith('b_') or n == 'a_w_out':
            late.append((n, local_w[n].astype(BF16) if n in matrices else local_w[n], SHARD_AXIS[n]))
        elif n == 'a_w_in':
            split.append((n, transposed(local_w[n]).astype(BF16), 1, 2))
        elif n in matrices:
            mid.append((n, local_w[n].astype(BF16), SHARD_AXIS[n]))
        else:
            small.append((n, local_w[n], SHARD_AXIS[n]))
    gathered = _gather_weights([s[1:] for s in split], [s[1:] for s in small])
    whole = dict(zip([s[0] for s in split + small], gathered))
    mid_names, late_names = [s[0] for s in mid], [s[0] for s in late]
    wt = {n: _kernel_form(n, whole.get(n, local_w[n])) for n in WEIGHTS if n not in mid_names + late_names}

    c_idx = lax.axis_index("c").astype(jnp.int32).reshape(1)
    me_idx = (2 * lax.axis_index("x") + lax.axis_index("y")).astype(jnp.int32).reshape(1)
    loss, grad_x, grads, through, swapped = _local_grads(
        x[0], loss_target[0], wt, heads, c_idx,
        mid_names, _gather_whole("gather_weights_a", [s[1:] for s in mid]),
        late_names, _gather_whole("gather_weights_b", [s[1:] for s in late]))

    ext_uq = heads % N_CHIPS == 0
    started = [gr.name for gr, *_ in through + swapped]
    last = [_matrix_grad(n, grads[n], heads) for n in matrices if n not in started]
    rest = [n for n in WEIGHTS if n not in matrices]
    pieces = [_chip_major(grads[n], local_w[n].shape, SHARD_AXIS[n]) for n in rest]
    pieces.append(jnp.broadcast_to(loss[0:1, 0:1], (N_CHIPS, 1)))
    length = sum(p.shape[1] for p in pieces)
    unit = 2 * SUBLANES * 1024
    padded = -(-length // unit) * unit
    flat = jnp.concatenate(pieces + [jnp.zeros((N_CHIPS, padded - length), F32)], axis=1)
    last.append(_Grad('small', flat.reshape(N_CHIPS, padded // 1024, 1024), 'lead', padded // 2048, 1024))

    got = _halves_to_sibling("grads_to_sibling", last).run()
    swapped = swapped + [(gr, _chip_sum(gr, r, c_idx, F32 if gr.name == 'small' else BF16))
                         for gr, r in zip(last, got)]
    from_chips = _blocks_to_chips("grads_to_chips", [p for _, p in swapped]).run()
    through = through + [(gr, p, r) for (gr, p), r in zip(swapped, from_chips)]
    halves = [_sum_chips(gr.name, p, r, me_idx, c_idx) for gr, p, r in through]
    summed = dict(zip([gr.name for gr, _, _ in through], _share_with_sibling(halves)))
    if ext_uq:
        g = summed['a_w_uq']
        summed['a_w_uq'] = g.reshape(g.shape[0], -1, HEAD_W)[:, :, :QK_NOPE + QK_ROPE]
    total = summed['small'].reshape(-1)

    operands, views = {}, {}
    off = 0
    for n in WEIGHTS:
        shape = local_w[n].shape
        view = transposed if n == 'a_w_in' else (lambda a: a)
        if n in matrices:
            g = summed[n].reshape(view(local_w[n]).shape)
        else:
            size = 1
            for s in shape:
                size *= s
            g = total[off:off + size].reshape(shape)
            off += size
        operands[n] = (_as2d(view(local_w[n])), _as2d(g), _as2d(view(local_m[n])), _as2d(view(local_v[n])))
        views[n] = view
    results = {n: _adamw("adamw_" + n, *operands[n]) for n in WEIGHTS if n in matrices}
    vectors = [n for n in WEIGHTS if n not in matrices]
    results.update(zip(vectors, _adamw_vectors([operands[n] for n in vectors])))
    out_g, out_d, out_m, out_v = [], [], [], []
    for n in WEIGHTS:
        for out, r in zip((out_g, out_d, out_m, out_v), results[n]):
            out.append(views[n](r.reshape(views[n](local_w[n]).shape)))

    return (total[off], grad_x[None], *out_g, *out_d, *out_m, *out_v)
```
